```python
import math
import jax, jax.numpy as jnp
from jax import lax
import numpy as np

D_MODEL = 1024
BATCH = 16
SEQ = 256
DEPTH = 1
DEC_BATCH = 2
DEC_SEQ = 1024
PAST_LEN = 256

GRID_W = 64
D_S5 = 768
S5_H = 16
N_S5_GROUPS = D_S5 // S5_H
S5_STATE = 64
D_FNET = D_MODEL - D_S5
N_FNET_GROUPS = 4
FNET_GW = D_FNET // N_FNET_GROUPS
N_EXPERTS = 32
TOP_K = 4
D_FF = D_MODEL
SWIGLU_LIMIT = 7.0
SWIGLU_ALPHA = 1.702
RMS_EPS = 1e-6
DT_MIN = 1e-3
DT_MAX = 1e-1
POS_TEMP = 10000.0

kernel_name = 'hymba_s5_fnet_moe_diffusion_step'


def rms_norm(x, g):
    xf = x.astype(jnp.float32)
    y = xf * lax.rsqrt(jnp.mean(xf * xf, axis=-1, keepdims=True) + RMS_EPS)
    return (y * g.astype(jnp.float32)).astype(x.dtype)


def grid_pos_embed(n_tokens, dim, dtype):
    rows = n_tokens // GRID_W
    t = jnp.arange(rows * GRID_W)
    r = (t // GRID_W).astype(jnp.float32)
    col = (t % GRID_W).astype(jnp.float32)
    q = dim // 4
    omega = 1.0 / POS_TEMP ** (jnp.arange(q, dtype=jnp.float32) / q)

    def emb(p):
        a = p[:, None] * omega[None, :]
        return jnp.concatenate([jnp.sin(a), jnp.cos(a)], axis=-1)

    return jnp.concatenate([emb(r), emb(col)], axis=-1).astype(dtype)


def cmul(ar, ai, br, bi):
    return ar * br - ai * bi, ar * bi + ai * br


def scan_combine(e1, e2):
    a1r, a1i, b1r, b1i = e1
    a2r, a2i, b2r, b2i = e2
    ar, ai = cmul(a2r, a2i, a1r, a1i)
    br, bi = cmul(a2r, a2i, b1r, b1i)
    return ar, ai, br + b2r, bi + b2i


def s5_mixer(u, init_re, init_im, lam_re, lam_im, log_dt, b_re, b_im, c_re, c_im, d_skip, w_glu):
    bsz, L, _ = u.shape
    uf = u.astype(jnp.float32)
    ug = uf.reshape(bsz, L, N_S5_GROUPS, S5_H)
    y = d_skip.astype(jnp.float32) * uf
    finals_re = []
    finals_im = []
    for d in range(2):
        reverse = d == 1
        dt = jnp.exp(log_dt[d].astype(jnp.float32))[:, None]
        a_re = jnp.minimum(lam_re[d].astype(jnp.float32), -1e-4)
        a_im = lam_im[d].astype(jnp.float32)
        mag = jnp.exp(a_re * dt)
        ab_re = mag * jnp.cos(a_im * dt)
        ab_im = mag * jnp.sin(a_im * dt)
        den = a_re * a_re + a_im * a_im
        nr = ab_re - 1.0
        f_re = (nr * a_re + ab_im * a_im) / den
        f_im = (ab_im * a_re - nr * a_im) / den
        bb_re, bb_im = cmul(f_re[..., None], f_im[..., None],
                            b_re[d].astype(jnp.float32), b_im[d].astype(jnp.float32))
        bu_re = jnp.einsum('blgh,gph->blgp', ug, bb_re)
        bu_im = jnp.einsum('blgh,gph->blgp', ug, bb_im)
        s0r, s0i = cmul(ab_re, ab_im, init_re[:, d].astype(jnp.float32), init_im[:, d].astype(jnp.float32))
        first = L - 1 if reverse else 0
        last = 0 if reverse else L - 1
        bu_re = bu_re.at[:, first].add(s0r)
        bu_im = bu_im.at[:, first].add(s0i)
        a_br = jnp.broadcast_to(ab_re, bu_re.shape)
        a_bi = jnp.broadcast_to(ab_im, bu_im.shape)
        _, _, s_re, s_im = lax.associative_scan(scan_combine, (a_br, a_bi, bu_re, bu_im),
                                                reverse=reverse, axis=1)
        finals_re.append(s_re[:, last])
        finals_im.append(s_im[:, last])
        yd = (jnp.einsum('blgp,ghp->blgh', s_re, c_re[d].astype(jnp.float32))
              - jnp.einsum('blgp,ghp->blgh', s_im, c_im[d].astype(jnp.float32)))
        y = y + yd.reshape(bsz, L, D_S5)
    z = jax.nn.gelu(y)
    out = z * jax.nn.sigmoid(z @ w_glu.astype(jnp.float32))
    return out.astype(u.dtype), jnp.stack(finals_re, axis=1), jnp.stack(finals_im, axis=1)


def fourier_mixer(u, w_fnet):
    bsz, L, _ = u.shape
    ug = u.astype(jnp.float32).reshape(bsz, L, N_FNET_GROUPS, FNET_GW)
    f = jnp.fft.fft2(ug, axes=(1, 3), norm='ortho').real
    out = jnp.einsum('blgc,gce->blge', f, w_fnet.astype(jnp.float32)).reshape(bsz, L, D_FNET)
    return out.astype(u.dtype)


def moe(h, w_router, b_router, w_gate, b_gate, w_up, b_up, w_down, b_down):
    bsz, L, D = h.shape
    t = h.reshape(bsz * L, D)
    logits = t.astype(jnp.float32) @ w_router.astype(jnp.float32) + b_router.astype(jnp.float32)
    top_v, top_i = lax.top_k(logits, TOP_K)
    top_w = jax.nn.softmax(top_v, axis=-1)
    comb = jnp.sum(jax.nn.one_hot(top_i, N_EXPERTS, dtype=jnp.float32) * top_w[..., None], axis=1)
    comb = comb.astype(h.dtype)
    gate = jnp.einsum('td,edf->tef', t, w_gate) + b_gate
    up = jnp.einsum('td,edf->tef', t, w_up) + b_up
    gate = jnp.minimum(gate, SWIGLU_LIMIT)
    up = jnp.clip(up, -SWIGLU_LIMIT, SWIGLU_LIMIT)
    act = (up + 1.0) * gate * jax.nn.sigmoid(SWIGLU_ALPHA * gate) * comb[..., None]
    out = jnp.einsum('tef,efd->td', act, w_down) + comb @ b_down
    return out.reshape(bsz, L, D)


def trunk_layer(x, mod, s_init_re, s_init_im, p):
    shift1, scale1, gate1, shift2, scale2, gate2 = jnp.split(mod, 6, axis=-1)
    h = rms_norm(x, p['norm1_g']) * (1.0 + scale1) + shift1
    proj = h @ p['w_in']
    u_s5 = proj[..., :D_S5]
    u_f = proj[..., D_S5:]
    y_s5, fin_re, fin_im = s5_mixer(u_s5, s_init_re, s_init_im, p['lam_re'], p['lam_im'], p['log_dt'],
                                    p['b_re'], p['b_im'], p['c_re'], p['c_im'], p['d_skip'], p['w_glu'])
    y_f = fourier_mixer(u_f, p['w_fnet'])
    x = x + gate1 * (jnp.concatenate([y_s5, y_f], axis=-1) @ p['w_out'])
    h2 = rms_norm(x, p['norm2_g']) * (1.0 + scale2) + shift2
    x = x + gate2 * moe(h2, p['w_router'], p['b_router'], p['w_gate'], p['b_gate'],
                        p['w_up'], p['b_up'], p['w_down'], p['b_down'])
    return x, fin_re, fin_im


def setup_inputs(seed: int = 0) -> dict:
    key = jax.random.key(seed)
    ks = jax.random.split(key, 32)
    f32 = jnp.float32

    def nrm(k, shape, scale):
        return scale * jax.random.normal(k, shape, f32)

    G, P, H, E, D = N_S5_GROUPS, S5_STATE, S5_H, N_EXPERTS, D_MODEL
    lam_im = jnp.broadcast_to(jnp.pi * jnp.arange(P, dtype=f32), (DEPTH, 2, G, P)) + nrm(ks[7], (DEPTH, 2, G, P), 0.01)
    return {
        'x_prompt': nrm(ks[0], (BATCH, SEQ, D), 1.0),
        'x_sample': nrm(ks[1], (DEC_BATCH, DEC_SEQ, D), 1.0),
        'c': nrm(ks[2], (DEC_BATCH, D), 1.0),
        'state_s5_re': nrm(ks[3], (DEC_BATCH, DEPTH, 2, G, P), 0.3),
        'state_s5_im': nrm(ks[4], (DEC_BATCH, DEPTH, 2, G, P), 0.3),
        'c_ctx': nrm(ks[5], (D,), 1.0),
        'w_ada': nrm(ks[6], (DEPTH, D, 6 * D), D ** -0.5),
        'b_ada': nrm(ks[8], (DEPTH, 6 * D), 0.01),
        'norm1_g': 1.0 + nrm(ks[9], (DEPTH, D), 0.02),
        'w_in': nrm(ks[10], (DEPTH, D, D), D ** -0.5),
        's5_lam_re': -0.5 + nrm(ks[11], (DEPTH, 2, G, P), 0.01),
        's5_lam_im': lam_im,
        's5_log_dt': jax.random.uniform(ks[12], (DEPTH, 2, G), f32, math.log(DT_MIN), math.log(DT_MAX)),
        's5_b_re': nrm(ks[13], (DEPTH, 2, G, P, H), (2 * H) ** -0.5),
        's5_b_im': nrm(ks[14], (DEPTH, 2, G, P, H), (2 * H) ** -0.5),
        's5_c_re': nrm(ks[15], (DEPTH, 2, G, H, P), (2 * P) ** -0.5),
        's5_c_im': nrm(ks[16], (DEPTH, 2, G, H, P), (2 * P) ** -0.5),
        's5_d': nrm(ks[17], (DEPTH, D_S5), 1.0),
        's5_w_glu': nrm(ks[18], (DEPTH, D_S5, D_S5), D_S5 ** -0.5),
        'w_fnet': nrm(ks[19], (DEPTH, N_FNET_GROUPS, FNET_GW, FNET_GW), FNET_GW ** -0.5),
        'w_out': nrm(ks[20], (DEPTH, D, D), D ** -0.5),
        'norm2_g': 1.0 + nrm(ks[21], (DEPTH, D), 0.02),
        'w_router': nrm(ks[22], (DEPTH, D, E), D ** -0.5),
        'b_router': nrm(ks[23], (DEPTH, E), 0.01),
        'w_gate': nrm(ks[24], (DEPTH, E, D, D_FF), D ** -0.5),
        'b_gate': nrm(ks[25], (DEPTH, E, D_FF), 0.01),
        'w_up': nrm(ks[26], (DEPTH, E, D, D_FF), D ** -0.5),
        'b_up': nrm(ks[27], (DEPTH, E, D_FF), 0.01),
        'w_down': nrm(ks[28], (DEPTH, E, D_FF, D), D_FF ** -0.5),
        'b_down': nrm(ks[29], (DEPTH, E, D), 0.01),
        'norm_f_g': 1.0 + nrm(ks[30], (D,), 0.02),
    }


def reference(x_prompt, x_sample, c, state_s5_re, state_s5_im, c_ctx,
              w_ada, b_ada, norm1_g, w_in, s5_lam_re, s5_lam_im, s5_log_dt,
              s5_b_re, s5_b_im, s5_c_re, s5_c_im, s5_d, s5_w_glu, w_fnet, w_out,
              norm2_g, w_router, b_router, w_gate, b_gate, w_up, b_up, w_down, b_down,
              norm_f_g):
    n_prompt = x_prompt.shape[0]
    n_lat = x_sample.shape[1]
    ctx = x_prompt
    lat = x_sample + grid_pos_embed(n_lat, D_MODEL, x_sample.dtype)[None]
    zero_state = jnp.zeros((n_prompt, 2, N_S5_GROUPS, S5_STATE), jnp.float32)
    new_re = []
    new_im = []
    for l in range(DEPTH):
        p = {
            'norm1_g': norm1_g[l], 'w_in': w_in[l],
            'lam_re': s5_lam_re[l], 'lam_im': s5_lam_im[l], 'log_dt': s5_log_dt[l],
            'b_re': s5_b_re[l], 'b_im': s5_b_im[l], 'c_re': s5_c_re[l], 'c_im': s5_c_im[l],
            'd_skip': s5_d[l], 'w_glu': s5_w_glu[l], 'w_fnet': w_fnet[l], 'w_out': w_out[l],
            'norm2_g': norm2_g[l], 'w_router': w_router[l], 'b_router': b_router[l],
            'w_gate': w_gate[l], 'b_gate': b_gate[l], 'w_up': w_up[l], 'b_up': b_up[l],
            'w_down': w_down[l], 'b_down': b_down[l],
        }
        mod_ctx = (jax.nn.silu(c_ctx) @ w_ada[l] + b_ada[l])[None, None, :]
        mod_lat = (jax.nn.silu(c) @ w_ada[l] + b_ada[l])[:, None, :]
        ctx, fin_re, fin_im = trunk_layer(ctx, mod_ctx, zero_state, zero_state, p)
        new_re.append(fin_re)
        new_im.append(fin_im)
        lat, _, _ = trunk_layer(lat, mod_lat, state_s5_re[:, l], state_s5_im[:, l], p)
    y_prompt = rms_norm(ctx, norm_f_g)
    y_sample = rms_norm(lat, norm_f_g)
    new_state_s5_re = jnp.stack(new_re, axis=1)
    new_state_s5_im = jnp.stack(new_im, axis=1)
    return (y_prompt, y_sample, new_state_s5_re, new_state_s5_im)
```

```python
import functools
import math

import numpy as np
import jax
import jax.numpy as jnp
from jax import lax
from jax.experimental import pallas as pl
from jax.experimental.pallas import tpu as pltpu

F32 = jnp.float32
BF16 = jnp.bfloat16
HIGHEST = lax.Precision.HIGHEST

D = 1024
D_S5 = 768
S5_H = 16
S5_P = 64
N_S5_GROUPS = 48
D_FN = 256
FN_GW = 64
N_EXPERTS = 32
TOP_K = 4
E_PAD = 128
SWIGLU_LIMIT = 7.0
SWIGLU_ALPHA = 1.702
RMS_EPS = 1e-6
POS_TEMP = 10000.0
GRID_W = 64

L_BLK = 256
SUB = 8
N_GRP = 3
GRP_ROWS = L_BLK * SUB
LAT_CHUNKS = 4
T_TOK = N_GRP * GRP_ROWS
TC = 128
N_TC = L_BLK // TC
PERM_T = 32
S5_CHUNK = 16
N_S5_CHUNKS = N_S5_GROUPS // S5_CHUNK
CH_U = S5_CHUNK * S5_H
CH_P = S5_CHUNK * S5_P
ROW_BLK = 256
VMEM_LIMIT = 56 * 1024 * 1024


def _cparams(sem):
    return pltpu.CompilerParams(dimension_semantics=sem, vmem_limit_bytes=VMEM_LIMIT)


def _adaln_kernel(c_ref, w_ref, b_ref, o_ref):
    c = c_ref[...]
    s = c * jax.nn.sigmoid(c)
    o_ref[...] = jnp.dot(s, w_ref[...], preferred_element_type=F32, precision=HIGHEST) + b_ref[...]


def _adaln(cvec, w_ada, b_ada):
    n = w_ada.shape[1]
    return pl.pallas_call(
        _adaln_kernel,
        grid=(n // D,),
        in_specs=[pl.BlockSpec((SUB, D), lambda j: (0, 0)),
                  pl.BlockSpec((D, D), lambda j: (0, j)),
                  pl.BlockSpec((1, D), lambda j: (0, j))],
        out_specs=pl.BlockSpec((SUB, D), lambda j: (0, j)),
        out_shape=jax.ShapeDtypeStruct((SUB, n), F32),
        compiler_params=_cparams(("arbitrary",)),
        name="adaln",
    )(cvec, w_ada, b_ada.reshape(1, n))


def _disc_kernel(lre_ref, lim_ref, ldt_ref, bre_ref, bim_ref, are_ref, aim_ref, bbre_ref, bbim_ref):
    dt = jnp.exp(ldt_ref[...])
    a_re = jnp.minimum(lre_ref[...], -1e-4)
    a_im = lim_ref[...]
    mag = jnp.exp(a_re * dt)
    ab_re = mag * jnp.cos(a_im * dt)
    ab_im = mag * jnp.sin(a_im * dt)
    den = a_re * a_re + a_im * a_im
    nr = ab_re - 1.0
    f_re = (nr * a_re + ab_im * a_im) / den
    f_im = (ab_im * a_re - nr * a_im) / den
    b_re = bre_ref[...]
    b_im = bim_ref[...]
    are_ref[...] = ab_re
    aim_ref[...] = ab_im
    bbre_ref[...] = f_re * b_re - f_im * b_im
    bbim_ref[...] = f_re * b_im + f_im * b_re


def _discretise(lam_re, lam_im, log_dt, b_re, b_im):
    n = 2 * N_S5_GROUPS
    lre = lam_re.reshape(n, 1, S5_P)
    lim = lam_im.reshape(n, 1, S5_P)
    ldt = jnp.broadcast_to(log_dt.reshape(n, 1, 1), (n, 1, S5_P))
    bre = jnp.swapaxes(b_re, -1, -2).reshape(n, S5_H, S5_P)
    bim = jnp.swapaxes(b_im, -1, -2).reshape(n, S5_H, S5_P)
    small = jax.ShapeDtypeStruct((n, 1, S5_P), F32)
    big = jax.ShapeDtypeStruct((n, S5_H, S5_P), F32)
    return pl.pallas_call(_disc_kernel, out_shape=(small, small, big, big), name="s5_disc")(
        lre, lim, ldt, bre, bim)


def _rms(x, g):
    return x * lax.rsqrt(jnp.mean(x * x, axis=-1, keepdims=True) + RMS_EPS) * g


def _load_x(g, xp_ref, xs_ref, pos_ref):
    is_lat = g == N_GRP - 1
    pos = pos_ref[...]
    pos8 = jnp.concatenate([pos, pos], axis=0)
    return jnp.where(is_lat, xs_ref[0] + pos8, xp_ref[0])


def _x_specs():
    n_ctx = N_GRP - 1
    xp_spec = pl.BlockSpec(
        (1, SUB, TC, D),
        lambda g, tc: (jnp.minimum(g, n_ctx - 1), 0, jnp.where(g >= n_ctx, N_TC - 1, tc), 0))
    xs_spec = pl.BlockSpec((1, SUB, TC, D), lambda g, tc: (0, 0, jnp.where(g >= n_ctx, tc, 0), 0))
    pos_spec = pl.BlockSpec((LAT_CHUNKS, TC, D), lambda g, tc: (0, tc, 0))
    return xp_spec, xs_spec, pos_spec


def _perm_matrix():
    p = np.zeros((SUB * PERM_T, SUB * PERM_T), np.float32)
    for s in range(SUB):
        for j in range(PERM_T):
            p[j * SUB + s, s * PERM_T + j] = 1.0
    return p


def _pre_kernel(xp_ref, xs_ref, pos_ref, mod_ref, g1_ref, win_ref, perm_ref, us_ref, uf_ref):
    g = pl.program_id(0)
    x = _load_x(g, xp_ref, xs_ref, pos_ref)
    mod = mod_ref[0]
    shift1 = mod[:, :, 0:D]
    scale1 = mod[:, :, D:2 * D]
    h = _rms(x, g1_ref[...]) * (1.0 + scale1) + shift1
    h2d = h.reshape(SUB * TC, D).astype(BF16)
    proj = jnp.dot(h2d, win_ref[...].astype(BF16), preferred_element_type=F32)
    uf_ref[0] = proj[:, D_S5:].astype(BF16).reshape(SUB, TC, D_FN)
    u = proj[:, :D_S5].astype(BF16)
    perm = perm_ref[...]
    for q in range(TC // PERM_T):
        piece = jnp.concatenate(
            [u[s * TC + q * PERM_T: s * TC + (q + 1) * PERM_T] for s in range(SUB)], axis=0)
        us_ref[0, q * SUB * PERM_T:(q + 1) * SUB * PERM_T, :] = jnp.dot(
            perm, piece, preferred_element_type=F32).astype(BF16)


def _pre(xp4, xs4, pos3, modg, norm1_g, w_in, perm):
    xp_spec, xs_spec, pos_spec = _x_specs()
    return pl.pallas_call(
        _pre_kernel,
        grid=(N_GRP, N_TC),
        in_specs=[xp_spec, xs_spec, pos_spec,
                  pl.BlockSpec((1, SUB, 1, 6 * D), lambda g, tc: (g, 0, 0, 0)),
                  pl.BlockSpec((1, D), lambda g, tc: (0, 0)),
                  pl.BlockSpec((D, D), lambda g, tc: (0, 0)),
                  pl.BlockSpec((SUB * PERM_T, SUB * PERM_T), lambda g, tc: (0, 0))],
        out_specs=[pl.BlockSpec((1, SUB * TC, D_S5), lambda g, tc: (g, tc, 0)),
                   pl.BlockSpec((1, SUB, TC, D_FN), lambda g, tc: (g, 0, tc, 0))],
        out_shape=[jax.ShapeDtypeStruct((N_GRP, GRP_ROWS, D_S5), BF16),
                   jax.ShapeDtypeStruct((N_GRP, SUB, L_BLK, D_FN), BF16)],
        compiler_params=_cparams(("arbitrary", "arbitrary")),
        name="pre_mixer",
    )(xp4, xs4, pos3, modg, norm1_g.reshape(1, D), w_in, perm)


def _cmul(ar, ai, br, bi):
    return ar * br - ai * bi, ar * bi + ai * br


def _s5_kernel(us_ref, bb_ref, cc_ref, a_ref, init_ref, dskip_ref, y_ref, fin_ref, s_ref):
    g = pl.program_id(0)
    d = pl.program_id(2)
    n_rb = GRP_ROWS // ROW_BLK
    half = CH_P // 2

    def proj_in(rb, _):
        rows = pl.ds(pl.multiple_of(rb * ROW_BLK, ROW_BLK), ROW_BLK)
        s_ref[rows, :] = jnp.dot(us_ref[0, rows, :], bb_ref[0, 0], preferred_element_type=F32)
        return 0

    lax.fori_loop(0, n_rb, proj_in, 0)

    def tile_row(k):
        t = k + d * (L_BLK - 1 - 2 * k)
        return pl.ds(pl.multiple_of(t * SUB, SUB), SUB)

    is_lat = g == N_GRP - 1
    sub_id = lax.broadcasted_iota(jnp.int32, (SUB, half), 0) % LAT_CHUNKS

    for hh in range(2):
        re_cols = slice(hh * half, (hh + 1) * half)
        im_cols = slice(CH_P + hh * half, CH_P + (hh + 1) * half)
        a_re = a_ref[0, 0, :, re_cols]
        a_im = a_ref[0, 0, :, im_cols]
        zero = jnp.zeros((SUB, half), F32)
        s0_re = jnp.where(is_lat, init_ref[0, 0, :, re_cols], zero)
        s0_im = jnp.where(is_lat, init_ref[0, 0, :, im_cols], zero)

        def scan_body(i, carry, a_re=a_re, a_im=a_im, re_cols=re_cols, im_cols=im_cols):
            s_re, s_im = carry
            for j in range(8):
                rows = tile_row(i * 8 + j)
                n_re = a_re * s_re - a_im * s_im + s_ref[rows, re_cols]
                n_im = a_re * s_im + a_im * s_re + s_ref[rows, im_cols]
                s_ref[rows, re_cols] = n_re
                s_ref[rows, im_cols] = n_im
                s_re, s_im = n_re, n_im
            return s_re, s_im

        f_re, f_im = lax.fori_loop(0, L_BLK // 8, scan_body, (s0_re, s0_im))
        fin_ref[0, 0, :, re_cols] = f_re
        fin_ref[0, 0, :, im_cols] = f_im

        @pl.when(is_lat)
        def _(a_re=a_re, a_im=a_im, f_re=f_re, f_im=f_im, re_cols=re_cols, im_cols=im_cols):
            p_re, p_im = a_re, a_im
            for _ in range(8):
                p_re, p_im = _cmul(p_re, p_im, p_re, p_im)
            fwd = d == 0
            lo = jnp.where(fwd, 1, 0)
            hi = jnp.where(fwd, LAT_CHUNKS - 1, LAT_CHUNKS - 2)
            keep = (sub_id >= lo) & (sub_id <= hi)

            def from_prev(v):
                return jnp.where(keep, jnp.where(fwd, pltpu.roll(v, 1, 0), pltpu.roll(v, SUB - 1, 0)), 0.0)

            t_re, t_im = f_re, f_im
            for _ in range(LAT_CHUNKS - 2):
                m_re, m_im = _cmul(p_re, p_im, from_prev(t_re), from_prev(t_im))
                t_re, t_im = f_re + m_re, f_im + m_im
            c_re, c_im = from_prev(t_re), from_prev(t_im)

            def fix_body(i, carry):
                q_re, q_im = carry
                for j in range(8):
                    rows = tile_row(i * 8 + j)
                    m_re, m_im = _cmul(q_re, q_im, c_re, c_im)
                    s_ref[rows, re_cols] = s_ref[rows, re_cols] + m_re
                    s_ref[rows, im_cols] = s_ref[rows, im_cols] + m_im
                    q_re, q_im = _cmul(q_re, q_im, a_re, a_im)
                return q_re, q_im

            lax.fori_loop(0, L_BLK // 8, fix_body, (a_re, a_im))

    def proj_out(rb):
        rows = pl.ds(pl.multiple_of(rb * ROW_BLK, ROW_BLK), ROW_BLK)
        return rows, jnp.dot(s_ref[rows, :].astype(BF16), cc_ref[0, 0], preferred_element_type=F32)

    @pl.when(d == 0)
    def _():
        def body(rb, _):
            rows, yb = proj_out(rb)
            y_ref[0, rows, :] = dskip_ref[...] * us_ref[0, rows, :].astype(F32) + yb
            return 0
        lax.fori_loop(0, n_rb, body, 0)

    @pl.when(d == 1)
    def _():
        def body(rb, _):
            rows, yb = proj_out(rb)
            y_ref[0, rows, :] = y_ref[0, rows, :] + yb
            return 0
        lax.fori_loop(0, n_rb, body, 0)


def _s5(us, bbmat, ccmat, a8, init8, d_skip):
    return pl.pallas_call(
        _s5_kernel,
        grid=(N_GRP, N_S5_CHUNKS, 2),
        in_specs=[pl.BlockSpec((1, GRP_ROWS, CH_U), lambda g, c, d: (g, 0, c)),
                  pl.BlockSpec((1, 1, CH_U, 2 * CH_P), lambda g, c, d: (d, c, 0, 0)),
                  pl.BlockSpec((1, 1, 2 * CH_P, CH_U), lambda g, c, d: (d, c, 0, 0)),
                  pl.BlockSpec((1, 1, SUB, 2 * CH_P), lambda g, c, d: (d, c, 0, 0)),
                  pl.BlockSpec((1, 1, SUB, 2 * CH_P), lambda g, c, d: (d, c, 0, 0)),
                  pl.BlockSpec((1, CH_U), lambda g, c, d: (0, c))],
        out_specs=[pl.BlockSpec((1, GRP_ROWS, CH_U), lambda g, c, d: (g, 0, c)),
                   pl.BlockSpec((1, 1, SUB, 2 * CH_P), lambda g, c, d: (d, g, 0, c))],
        out_shape=[jax.ShapeDtypeStruct((N_GRP, GRP_ROWS, D_S5), F32),
                   jax.ShapeDtypeStruct((2, N_GRP, SUB, N_S5_CHUNKS * 2 * CH_P), F32)],
        scratch_shapes=[pltpu.VMEM((GRP_ROWS, 2 * CH_P), F32)],
        compiler_params=_cparams(("arbitrary", "arbitrary", "arbitrary")),
        name="s5_scan",
    )(us, bbmat, ccmat, a8, init8, d_skip.reshape(1, D_S5))


def _fnet_w_kernel(c_ref, s_ref, w_ref, m1_ref, m2_ref):
    w = w_ref[...]
    m1_ref[...] = jnp.dot(c_ref[...], w, preferred_element_type=F32, precision=HIGHEST).astype(BF16)
    m2_ref[...] = jnp.dot(s_ref[...], w, preferred_element_type=F32, precision=HIGHEST).astype(BF16)


def _fnet_kernel(u_ref, cos_ref, sin_ref, m1_ref, m2_ref, o_ref):
    u = u_ref[0]
    v1 = jnp.dot(u, m1_ref[...], preferred_element_type=F32).astype(BF16)
    v2 = jnp.dot(u, m2_ref[...], preferred_element_type=F32).astype(BF16)
    o_ref[0] = (jnp.dot(cos_ref[...].astype(BF16), v1, preferred_element_type=F32)
                - jnp.dot(sin_ref[...].astype(BF16), v2, preferred_element_type=F32)).astype(BF16)


def _dft_tables(n):
    k = np.arange(n, dtype=np.int64)
    ang = (2.0 * np.pi / n) * ((k[:, None] * k[None, :]) % n).astype(np.float64)
    scale = 1.0 / math.sqrt(n)
    return (np.cos(ang) * scale).astype(np.float32), (np.sin(ang) * scale).astype(np.float32)


def _fnet(u, m1, m2):
    n_seq, length, _ = u.shape
    cos_l, sin_l = _dft_tables(length)
    full = lambda i: (0, 0)
    return pl.pallas_call(
        _fnet_kernel,
        grid=(n_seq,),
        in_specs=[pl.BlockSpec((1, length, D_FN), lambda i: (i, 0, 0)),
                  pl.BlockSpec((length, length), full),
                  pl.BlockSpec((length, length), full),
                  pl.BlockSpec((D_FN, D_FN), full),
                  pl.BlockSpec((D_FN, D_FN), full)],
        out_specs=pl.BlockSpec((1, length, D_FN), lambda i: (i, 0, 0)),
        out_shape=jax.ShapeDtypeStruct(u.shape, BF16),
        compiler_params=_cparams(("arbitrary",)),
        name=f"fnet_{length}",
    )(u, jnp.asarray(cos_l), jnp.asarray(sin_l), m1, m2)


def _fnet_weights(w_fnet):
    n_g = D_FN // FN_GW
    cos_c, sin_c = _dft_tables(FN_GW)
    eye = np.eye(n_g, dtype=np.float32)
    cos_bd = np.kron(eye, cos_c)
    sin_bd = np.kron(eye, sin_c)
    w_bd = (w_fnet[:, :, None, :] * jnp.asarray(eye)[:, None, :, None]).reshape(D_FN, D_FN)
    out = jax.ShapeDtypeStruct((D_FN, D_FN), BF16)
    return pl.pallas_call(_fnet_w_kernel, out_shape=(out, out), name="fnet_weights")(
        jnp.asarray(cos_bd), jnp.asarray(sin_bd), w_bd)


def _gelu_tanh(x):
    return 0.5 * x * (1.0 + jnp.tanh(math.sqrt(2.0 / math.pi) * (x + 0.044715 * (x * x * x))))


def _post_kernel(y_ref, yf_ref, xp_ref, xs_ref, pos_ref, mod_ref, wglu_ref, wout_ref, g2_ref, wr_ref,
                 br_ref, permt_ref, x1_ref, h2_ref, comb_ref):
    g = pl.program_id(0)
    z = _gelu_tanh(y_ref[0])
    gate = jnp.dot(z.astype(BF16), wglu_ref[...].astype(BF16), preferred_element_type=F32)
    gl = (z * jax.nn.sigmoid(gate)).astype(BF16)
    permt = permt_ref[...]
    n_q = TC // PERM_T
    nat = [jnp.dot(permt, gl[q * SUB * PERM_T:(q + 1) * SUB * PERM_T], preferred_element_type=F32).astype(BF16)
           for q in range(n_q)]
    gl_nat = jnp.concatenate(
        [nat[q][s * PERM_T:(s + 1) * PERM_T] for s in range(SUB) for q in range(n_q)], axis=0)
    w_out = wout_ref[...].astype(BF16)
    mixed = (jnp.dot(gl_nat, w_out[:D_S5], preferred_element_type=F32)
             + jnp.dot(yf_ref[0].reshape(SUB * TC, D_FN), w_out[D_S5:], preferred_element_type=F32))
    x = _load_x(g, xp_ref, xs_ref, pos_ref)
    mod = mod_ref[0]
    gate1 = mod[:, :, 2 * D:3 * D]
    shift2 = mod[:, :, 3 * D:4 * D]
    scale2 = mod[:, :, 4 * D:5 * D]
    x1 = x + gate1 * mixed.reshape(SUB, TC, D)
    x1_ref[0] = x1
    h2 = _rms(x1, g2_ref[...]) * (1.0 + scale2) + shift2
    h2_ref[0] = h2.astype(BF16)

    logits = jnp.dot(h2.reshape(SUB * TC, D), wr_ref[...], preferred_element_type=F32,
                     precision=HIGHEST) + br_ref[...]
    lane = lax.broadcasted_iota(jnp.int32, logits.shape, 1)
    top_v, hots = [], []
    cur = logits
    for _ in range(TOP_K):
        m = jnp.max(cur, axis=-1, keepdims=True)
        idx = jnp.min(jnp.where(cur == m, lane, E_PAD), axis=-1, keepdims=True)
        hot = lane == idx
        top_v.append(m)
        hots.append(hot)
        cur = jnp.where(hot, -3.0e38, cur)
    exps = [jnp.exp(v - top_v[0]) for v in top_v]
    denom = exps[0] + exps[1] + exps[2] + exps[3]
    comb = jnp.zeros(logits.shape, F32)
    for k in range(TOP_K):
        comb = comb + jnp.where(hots[k], exps[k] / denom, 0.0)
    comb_ref[0] = comb.reshape(SUB, TC, E_PAD)


def _post(y, yf, xp4, xs4, pos3, modg, w_glu, w_out, norm2_g, w_router, b_router, permt):
    xp_spec, xs_spec, pos_spec = _x_specs()
    const2 = lambda g, tc: (0, 0)
    wr = jnp.zeros((D, E_PAD), F32).at[:, :N_EXPERTS].set(w_router)
    br = jnp.full((1, E_PAD), -1.0e30, F32).at[0, :N_EXPERTS].set(b_router)
    blk = lambda w: pl.BlockSpec((1, SUB, TC, w), lambda g, tc: (g, 0, tc, 0))
    return pl.pallas_call(
        _post_kernel,
        grid=(N_GRP, N_TC),
        in_specs=[pl.BlockSpec((1, SUB * TC, D_S5), lambda g, tc: (g, tc, 0)),
                  blk(D_FN), xp_spec, xs_spec, pos_spec,
                  pl.BlockSpec((1, SUB, 1, 6 * D), lambda g, tc: (g, 0, 0, 0)),
                  pl.BlockSpec((D_S5, D_S5), const2),
                  pl.BlockSpec((D, D), const2),
                  pl.BlockSpec((1, D), const2),
                  pl.BlockSpec((D, E_PAD), const2),
                  pl.BlockSpec((1, E_PAD), const2),
                  pl.BlockSpec((SUB * PERM_T, SUB * PERM_T), const2)],
        out_specs=[blk(D), blk(D), blk(E_PAD)],
        out_shape=[jax.ShapeDtypeStruct((N_GRP, SUB, L_BLK, D), F32),
                   jax.ShapeDtypeStruct((N_GRP, SUB, L_BLK, D), BF16),
                   jax.ShapeDtypeStruct((N_GRP, SUB, L_BLK, E_PAD), F32)],
        compiler_params=_cparams(("arbitrary", "arbitrary")),
        name="post_mixer",
    )(y, yf, xp4, xs4, pos3, modg, w_glu, w_out, norm2_g.reshape(1, D), wr, br, permt)


MOE_ROWS = 1024
MOE_SUB_ROWS = 256


def _moe_kernel(h_ref, comb_ref, wg_ref, bg_ref, wu_ref, bu_ref, wd_ref, bd_ref, o_ref):
    e = pl.program_id(1)

    @pl.when(e == 0)
    def _():
        o_ref[...] = jnp.zeros_like(o_ref)

    wg = wg_ref[0].astype(BF16)
    wu = wu_ref[0].astype(BF16)
    wd = wd_ref[0].astype(BF16)
    lane = lax.broadcasted_iota(jnp.int32, (MOE_SUB_ROWS, E_PAD), 1)

    def body(r, _):
        rows = pl.ds(pl.multiple_of(r * MOE_SUB_ROWS, MOE_SUB_ROWS), MOE_SUB_ROWS)
        x = h_ref[rows, :]
        ce = jnp.sum(jnp.where(lane == e, comb_ref[rows, :], 0.0), axis=-1, keepdims=True)
        gate = jnp.dot(x, wg, preferred_element_type=F32) + bg_ref[0]
        up = jnp.dot(x, wu, preferred_element_type=F32) + bu_ref[0]
        gate = jnp.minimum(gate, SWIGLU_LIMIT)
        up = jnp.clip(up, -SWIGLU_LIMIT, SWIGLU_LIMIT)
        act = (up + 1.0) * gate * jax.nn.sigmoid(SWIGLU_ALPHA * gate) * ce
        o_ref[rows, :] = (o_ref[rows, :] + jnp.dot(act.astype(BF16), wd, preferred_element_type=F32)
                          + ce * bd_ref[0])
        return 0

    lax.fori_loop(0, MOE_ROWS // MOE_SUB_ROWS, body, 0)


def _moe(h2, comb, w_gate, b_gate, w_up, b_up, w_down, b_down):
    wspec = pl.BlockSpec((1, D, D), lambda i, e: (e, 0, 0))
    bspec = pl.BlockSpec((1, 1, D), lambda i, e: (e, 0, 0))
    return pl.pallas_call(
        _moe_kernel,
        grid=(T_TOK // MOE_ROWS, N_EXPERTS),
        in_specs=[pl.BlockSpec((MOE_ROWS, D), lambda i, e: (i, 0)),
                  pl.BlockSpec((MOE_ROWS, E_PAD), lambda i, e: (i, 0)),
                  wspec, bspec, wspec, bspec, wspec, bspec],
        out_specs=pl.BlockSpec((MOE_ROWS, D), lambda i, e: (i, 0)),
        out_shape=jax.ShapeDtypeStruct((T_TOK, D), F32),
        compiler_params=_cparams(("arbitrary", "arbitrary")),
        name="moe",
    )(h2, comb, w_gate, b_gate.reshape(N_EXPERTS, 1, D), w_up, b_up.reshape(N_EXPERTS, 1, D),
      w_down, b_down.reshape(N_EXPERTS, 1, D))


def _final_kernel(x1_ref, moe_ref, mod_ref, gf_ref, o_ref):
    gate2 = mod_ref[0][:, :, 5 * D:6 * D]
    x2 = x1_ref[0] + gate2 * moe_ref[0]
    o_ref[0] = _rms(x2, gf_ref[...])


def _final(x1, moe_out, modg, norm_f_g):
    blk = pl.BlockSpec((1, SUB, TC, D), lambda g, tc: (g, 0, tc, 0))
    return pl.pallas_call(
        _final_kernel,
        grid=(N_GRP, N_TC),
        in_specs=[blk, blk,
                  pl.BlockSpec((1, SUB, 1, 6 * D), lambda g, tc: (g, 0, 0, 0)),
                  pl.BlockSpec((1, D), lambda g, tc: (0, 0))],
        out_specs=blk,
        out_shape=jax.ShapeDtypeStruct((N_GRP, SUB, L_BLK, D), F32),
        compiler_params=_cparams(("arbitrary", "arbitrary")),
        name="final_norm",
    )(x1, moe_out, modg, norm_f_g.reshape(1, D))


def _grid_pos_embed(n_tokens, dim):
    rows = n_tokens // GRID_W
    t = jnp.arange(rows * GRID_W)
    r = (t // GRID_W).astype(F32)
    col = (t % GRID_W).astype(F32)
    q = dim // 4
    omega = 1.0 / POS_TEMP ** (jnp.arange(q, dtype=F32) / q)

    def emb(p):
        a = p[:, None] * omega[None, :]
        return jnp.concatenate([jnp.sin(a), jnp.cos(a)], axis=-1)

    return jnp.concatenate([emb(r), emb(col)], axis=-1)


def _block_diag_chunks(m):
    _, _, a, b = m.shape
    m = m.reshape(2, N_S5_CHUNKS, S5_CHUNK, a, b)
    eye = jnp.eye(S5_CHUNK, dtype=m.dtype)
    bd = m[:, :, :, :, None, :] * eye[None, None, :, None, :, None]
    return bd.reshape(2, N_S5_CHUNKS, S5_CHUNK * a, S5_CHUNK * b)


def kernel(x_prompt, x_sample, c, state_s5_re, state_s5_im, c_ctx, w_ada, b_ada, norm1_g, w_in, s5_lam_re,
           s5_lam_im, s5_log_dt, s5_b_re, s5_b_im, s5_c_re, s5_c_im, s5_d, s5_w_glu, w_fnet, w_out, norm2_g,
           w_router, b_router, w_gate, b_gate, w_up, b_up, w_down, b_down, norm_f_g):
    n_ctx, n_lat = x_prompt.shape[0], x_sample.shape[0]
    assert x_prompt.shape == (SUB * (N_GRP - 1), L_BLK, D) and x_sample.shape == (2, LAT_CHUNKS * L_BLK, D)
    assert w_ada.shape[0] == 1, "one trunk layer"
    layer = 0

    cvec = jnp.zeros((SUB, D), F32).at[0].set(c_ctx).at[1:1 + n_lat].set(c)
    mod = _adaln(cvec, w_ada[layer], b_ada[layer])
    lat_rows = jnp.repeat(mod[1:1 + n_lat], LAT_CHUNKS, axis=0)
    ctx_rows = jnp.broadcast_to(mod[0:1], (SUB, 6 * D))
    modg = jnp.stack([ctx_rows] * (N_GRP - 1) + [lat_rows])[:, :, None, :]

    xp4 = x_prompt.reshape(N_GRP - 1, SUB, L_BLK, D)
    xs4 = x_sample.reshape(1, SUB, L_BLK, D)
    pos3 = _grid_pos_embed(LAT_CHUNKS * L_BLK, D).reshape(LAT_CHUNKS, L_BLK, D)
    perm = _perm_matrix()

    us, uf = _pre(xp4, xs4, pos3, modg, norm1_g[layer], w_in[layer], jnp.asarray(perm, BF16))

    a_re, a_im, bb_re, bb_im = _discretise(s5_lam_re[layer], s5_lam_im[layer], s5_log_dt[layer],
                                           s5_b_re[layer], s5_b_im[layer])
    a_cat = jnp.concatenate([a_re.reshape(2, N_S5_CHUNKS, CH_P), a_im.reshape(2, N_S5_CHUNKS, CH_P)], axis=-1)
    a8 = jnp.broadcast_to(a_cat[:, :, None, :], (2, N_S5_CHUNKS, SUB, 2 * CH_P))
    bb_re = bb_re.reshape(2, N_S5_GROUPS, S5_H, S5_P)
    bb_im = bb_im.reshape(2, N_S5_GROUPS, S5_H, S5_P)
    bbmat = jnp.concatenate([_block_diag_chunks(bb_re), _block_diag_chunks(bb_im)], axis=-1).astype(BF16)
    c_re_t = jnp.swapaxes(s5_c_re[layer], -1, -2)
    c_im_t = jnp.swapaxes(s5_c_im[layer], -1, -2)
    ccmat = jnp.concatenate([_block_diag_chunks(c_re_t), _block_diag_chunks(-c_im_t)], axis=-2).astype(BF16)

    st = jnp.stack([state_s5_re[:, layer], state_s5_im[:, layer]], axis=2)
    st = st.reshape(n_lat, 2, 2, N_S5_CHUNKS, CH_P)
    st = jnp.transpose(st, (1, 3, 0, 2, 4)).reshape(2, N_S5_CHUNKS, n_lat, 2 * CH_P)
    init8 = jnp.zeros((2, N_S5_CHUNKS, n_lat, LAT_CHUNKS, 2 * CH_P), F32)
    init8 = init8.at[0, :, :, 0].set(st[0]).at[1, :, :, LAT_CHUNKS - 1].set(st[1])
    init8 = init8.reshape(2, N_S5_CHUNKS, SUB, 2 * CH_P)

    y_s5, fin = _s5(us, bbmat, ccmat, a8, init8, s5_d[layer])

    m1, m2 = _fnet_weights(w_fnet[layer])
    uf_ctx = uf[:N_GRP - 1].reshape(n_ctx, L_BLK, D_FN)
    uf_lat = uf[N_GRP - 1].reshape(n_lat, LAT_CHUNKS * L_BLK, D_FN)
    yf = jnp.concatenate([_fnet(uf_ctx, m1, m2).reshape(N_GRP - 1, SUB, L_BLK, D_FN),
                          _fnet(uf_lat, m1, m2).reshape(1, SUB, L_BLK, D_FN)], axis=0)

    x1, h2, comb = _post(y_s5, yf, xp4, xs4, pos3, modg, s5_w_glu[layer], w_out[layer], norm2_g[layer],
                         w_router[layer], b_router[layer], jnp.asarray(perm.T, BF16))

    moe_out = _moe(h2.reshape(T_TOK, D), comb.reshape(T_TOK, E_PAD), w_gate[layer], b_gate[layer],
                   w_up[layer], b_up[layer], w_down[layer], b_down[layer])
    y_all = _final(x1, moe_out.reshape(N_GRP, SUB, L_BLK, D), modg, norm_f_g)

    y_prompt = y_all[:N_GRP - 1].reshape(n_ctx, L_BLK, D)
    y_sample = y_all[N_GRP - 1].reshape(n_lat, LAT_CHUNKS * L_BLK, D)

    fin = fin[:, :N_GRP - 1].reshape(2, N_GRP - 1, SUB, N_S5_CHUNKS, 2, CH_P)
    fin = jnp.transpose(fin, (4, 1, 2, 0, 3, 5)).reshape(2, n_ctx, 1, 2, N_S5_GROUPS, S5_P)
    return (y_prompt, y_sample, fin[0], fin[1])
```

```python
import functools
import math

import numpy as np
import jax
import jax.numpy as jnp
from jax import lax
from jax.experimental import pallas as pl
from jax.experimental.pallas import tpu as pltpu

F32 = jnp.float32
BF16 = jnp.bfloat16
HIGHEST = lax.Precision.HIGHEST

D = 1024
D_S5 = 768
S5_H = 16
S5_P = 64
N_S5_GROUPS = 48
D_FN = 256
FN_GW = 64
N_EXPERTS = 32
TOP_K = 4
E_PAD = 128
SWIGLU_LIMIT = 7.0
SWIGLU_ALPHA = 1.702
RMS_EPS = 1e-6
POS_TEMP = 10000.0
GRID_W = 64

L_BLK = 256
SUB = 8
N_GRP = 3
GRP_ROWS = L_BLK * SUB
LAT_CHUNKS = 4
T_TOK = N_GRP * GRP_ROWS
TC = 128
N_TC = L_BLK // TC
PERM_T = 32
S5_CHUNK = 16
N_S5_CHUNKS = N_S5_GROUPS // S5_CHUNK
CH_U = S5_CHUNK * S5_H
CH_P = S5_CHUNK * S5_P
ROW_BLK = 256
VMEM_LIMIT = 56 * 1024 * 1024


def _cparams(sem):
    return pltpu.CompilerParams(dimension_semantics=sem, vmem_limit_bytes=VMEM_LIMIT)


def _adaln_kernel(c_ref, w_ref, b_ref, o_ref):
    c = c_ref[...]
    s = c * jax.nn.sigmoid(c)
    o_ref[...] = jnp.dot(s, w_ref[...], preferred_element_type=F32, precision=HIGHEST) + b_ref[...]


def _adaln(cvec, w_ada, b_ada):
    n = w_ada.shape[1]
    return pl.pallas_call(
        _adaln_kernel,
        grid=(n // D,),
        in_specs=[pl.BlockSpec((SUB, D), lambda j: (0, 0)),
                  pl.BlockSpec((D, D), lambda j: (0, j)),
                  pl.BlockSpec((1, D), lambda j: (0, j))],
        out_specs=pl.BlockSpec((SUB, D), lambda j: (0, j)),
        out_shape=jax.ShapeDtypeStruct((SUB, n), F32),
        compiler_params=_cparams(("arbitrary",)),
        name="adaln",
    )(cvec, w_ada, b_ada.reshape(1, n))


def _disc_kernel(lre_ref, lim_ref, ldt_ref, bre_ref, bim_ref, are_ref, aim_ref, bbre_ref, bbim_ref):
    dt = jnp.exp(ldt_ref[...])
    a_re = jnp.minimum(lre_ref[...], -1e-4)
    a_im = lim_ref[...]
    mag = jnp.exp(a_re * dt)
    ab_re = mag * jnp.cos(a_im * dt)
    ab_im = mag * jnp.sin(a_im * dt)
    den = a_re * a_re + a_im * a_im
    nr = ab_re - 1.0
    f_re = (nr * a_re + ab_im * a_im) / den
    f_im = (ab_im * a_re - nr * a_im) / den
    b_re = bre_ref[...]
    b_im = bim_ref[...]
    are_ref[...] = ab_re
    aim_ref[...] = ab_im
    bbre_ref[...] = f_re * b_re - f_im * b_im
    bbim_ref[...] = f_re * b_im + f_im * b_re


def _discretise(lam_re, lam_im, log_dt, b_re, b_im):
    n = 2 * N_S5_GROUPS
    lre = lam_re.reshape(n, 1, S5_P)
    lim = lam_im.reshape(n, 1, S5_P)
    ldt = jnp.broadcast_to(log_dt.reshape(n, 1, 1), (n, 1, S5_P))
    bre = jnp.swapaxes(b_re, -1, -2).reshape(n, S5_H, S5_P)
    bim = jnp.swapaxes(b_im, -1, -2).reshape(n, S5_H, S5_P)
    small = jax.ShapeDtypeStruct((n, 1, S5_P), F32)
    big = jax.ShapeDtypeStruct((n, S5_H, S5_P), F32)
    return pl.pallas_call(_disc_kernel, out_shape=(small, small, big, big), name="s5_disc")(
        lre, lim, ldt, bre, bim)


def _rms(x, g):
    return x * lax.rsqrt(jnp.mean(x * x, axis=-1, keepdims=True) + RMS_EPS) * g


def _load_x(g, xp_ref, xs_ref, pos_ref):
    is_lat = g == N_GRP - 1
    pos = pos_ref[...]
    pos8 = jnp.concatenate([pos, pos], axis=0)
    return jnp.where(is_lat, xs_ref[0] + pos8, xp_ref[0])


def _x_specs():
    n_ctx = N_GRP - 1
    xp_spec = pl.BlockSpec(
        (1, SUB, TC, D),
        lambda g, tc: (jnp.minimum(g, n_ctx - 1), 0, jnp.where(g >= n_ctx, N_TC - 1, tc), 0))
    xs_spec = pl.BlockSpec((1, SUB, TC, D), lambda g, tc: (0, 0, jnp.where(g >= n_ctx, tc, 0), 0))
    pos_spec = pl.BlockSpec((LAT_CHUNKS, TC, D), lambda g, tc: (0, tc, 0))
    return xp_spec, xs_spec, pos_spec


def _perm_matrix():
    p = np.zeros((SUB * PERM_T, SUB * PERM_T), np.float32)
    for s in range(SUB):
        for j in range(PERM_T):
            p[j * SUB + s, s * PERM_T + j] = 1.0
    return p


def _pre_kernel(xp_ref, xs_ref, pos_ref, mod_ref, g1_ref, win_ref, perm_ref, us_ref, uf_ref):
    g = pl.program_id(0)
    x = _load_x(g, xp_ref, xs_ref, pos_ref)
    mod = mod_ref[0]
    shift1 = mod[:, :, 0:D]
    scale1 = mod[:, :, D:2 * D]
    h = _rms(x, g1_ref[...]) * (1.0 + scale1) + shift1
    h2d = h.reshape(SUB * TC, D).astype(BF16)
    proj = jnp.dot(h2d, win_ref[...].astype(BF16), preferred_element_type=F32)
    uf_ref[0] = proj[:, D_S5:].astype(BF16).reshape(SUB, TC, D_FN)
    u = proj[:, :D_S5].astype(BF16)
    perm = perm_ref[...]
    for q in range(TC // PERM_T):
        piece = jnp.concatenate(
            [u[s * TC + q * PERM_T: s * TC + (q + 1) * PERM_T] for s in range(SUB)], axis=0)
        us_ref[0, q * SUB * PERM_T:(q + 1) * SUB * PERM_T, :] = jnp.dot(
            perm, piece, preferred_element_type=F32).astype(BF16)


def _pre(xp4, xs4, pos3, modg, norm1_g, w_in, perm):
    xp_spec, xs_spec, pos_spec = _x_specs()
    return pl.pallas_call(
        _pre_kernel,
        grid=(N_GRP, N_TC),
        in_specs=[xp_spec, xs_spec, pos_spec,
                  pl.BlockSpec((1, SUB, 1, 6 * D), lambda g, tc: (g, 0, 0, 0)),
                  pl.BlockSpec((1, D), lambda g, tc: (0, 0)),
                  pl.BlockSpec((D, D), lambda g, tc: (0, 0)),
                  pl.BlockSpec((SUB * PERM_T, SUB * PERM_T), lambda g, tc: (0, 0))],
        out_specs=[pl.BlockSpec((1, SUB * TC, D_S5), lambda g, tc: (g, tc, 0)),
                   pl.BlockSpec((1, SUB, TC, D_FN), lambda g, tc: (g, 0, tc, 0))],
        out_shape=[jax.ShapeDtypeStruct((N_GRP, GRP_ROWS, D_S5), BF16),
                   jax.ShapeDtypeStruct((N_GRP, SUB, L_BLK, D_FN), BF16)],
        compiler_params=_cparams(("arbitrary", "arbitrary")),
        name="pre_mixer",
    )(xp4, xs4, pos3, modg, norm1_g.reshape(1, D), w_in, perm)


def _cmul(ar, ai, br, bi):
    return ar * br - ai * bi, ar * bi + ai * br


def _s5_kernel(us_ref, bb_ref, cc_ref, a_ref, init_ref, dskip_ref, y_ref, fin_ref, s_ref):
    g = pl.program_id(0)
    d = pl.program_id(2)
    n_rb = GRP_ROWS // ROW_BLK
    half = CH_P // 2

    def proj_in(rb, _):
        rows = pl.ds(pl.multiple_of(rb * ROW_BLK, ROW_BLK), ROW_BLK)
        s_ref[rows, :] = jnp.dot(us_ref[0, rows, :], bb_ref[0, 0], preferred_element_type=F32)
        return 0

    lax.fori_loop(0, n_rb, proj_in, 0)

    def tile_row(k):
        t = k + d * (L_BLK - 1 - 2 * k)
        return pl.ds(pl.multiple_of(t * SUB, SUB), SUB)

    is_lat = g == N_GRP - 1
    sub_id = lax.broadcasted_iota(jnp.int32, (SUB, half), 0) % LAT_CHUNKS

    for hh in range(2):
        re_cols = slice(hh * half, (hh + 1) * half)
        im_cols = slice(CH_P + hh * half, CH_P + (hh + 1) * half)
        a_re = a_ref[0, 0, :, re_cols]
        a_im = a_ref[0, 0, :, im_cols]
        zero = jnp.zeros((SUB, half), F32)
        s0_re = jnp.where(is_lat, init_ref[0, 0, :, re_cols], zero)
        s0_im = jnp.where(is_lat, init_ref[0, 0, :, im_cols], zero)

        def scan_body(i, carry, a_re=a_re, a_im=a_im, re_cols=re_cols, im_cols=im_cols):
            s_re, s_im = carry
            for j in range(8):
                rows = tile_row(i * 8 + j)
                n_re = a_re * s_re - a_im * s_im + s_ref[rows, re_cols]
                n_im = a_re * s_im + a_im * s_re + s_ref[rows, im_cols]
                s_ref[rows, re_cols] = n_re
                s_ref[rows, im_cols] = n_im
                s_re, s_im = n_re, n_im
            return s_re, s_im

        f_re, f_im = lax.fori_loop(0, L_BLK // 8, scan_body, (s0_re, s0_im))
        fin_ref[0, 0, :, re_cols] = f_re
        fin_ref[0, 0, :, im_cols] = f_im

        @pl.when(is_lat)
        def _(a_re=a_re, a_im=a_im, f_re=f_re, f_im=f_im, re_cols=re_cols, im_cols=im_cols):
            p_re, p_im = a_re, a_im
            for _ in range(8):
                p_re, p_im = _cmul(p_re, p_im, p_re, p_im)
            fwd = d == 0
            lo = jnp.where(fwd, 1, 0)
            hi = jnp.where(fwd, LAT_CHUNKS - 1, LAT_CHUNKS - 2)
            keep = (sub_id >= lo) & (sub_id <= hi)

            def from_prev(v):
                return jnp.where(keep, jnp.where(fwd, pltpu.roll(v, 1, 0), pltpu.roll(v, SUB - 1, 0)), 0.0)

            t_re, t_im = f_re, f_im
            for _ in range(LAT_CHUNKS - 2):
                m_re, m_im = _cmul(p_re, p_im, from_prev(t_re), from_prev(t_im))
                t_re, t_im = f_re + m_re, f_im + m_im
            c_re, c_im = from_prev(t_re), from_prev(t_im)

            def fix_body(i, carry):
                q_re, q_im = carry
                for j in range(8):
                    rows = tile_row(i * 8 + j)
                    m_re, m_im = _cmul(q_re, q_im, c_re, c_im)
                    s_ref[rows, re_cols] = s_ref[rows, re_cols] + m_re
                    s_ref[rows, im_cols] = s_ref[rows, im_cols] + m_im
                    q_re, q_im = _cmul(q_re, q_im, a_re, a_im)
                return q_re, q_im

            lax.fori_loop(0, L_BLK // 8, fix_body, (a_re, a_im))

    def proj_out(rb):
        rows = pl.ds(pl.multiple_of(rb * ROW_BLK, ROW_BLK), ROW_BLK)
        return rows, jnp.dot(s_ref[rows, :].astype(BF16), cc_ref[0, 0], preferred_element_type=F32)

    @pl.when(d == 0)
    def _():
        def body(rb, _):
            rows, yb = proj_out(rb)
            y_ref[0, rows, :] = dskip_ref[...] * us_ref[0, rows, :].astype(F32) + yb
            return 0
        lax.fori_loop(0, n_rb, body, 0)

    @pl.when(d == 1)
    def _():
        def body(rb, _):
            rows, yb = proj_out(rb)
            y_ref[0, rows, :] = y_ref[0, rows, :] + yb
            return 0
        lax.fori_loop(0, n_rb, body, 0)


def _s5(us, bbmat, ccmat, a8, init8, d_skip):
    return pl.pallas_call(
        _s5_kernel,
        grid=(N_GRP, N_S5_CHUNKS, 2),
        in_specs=[pl.BlockSpec((1, GRP_ROWS, CH_U), lambda g, c, d: (g, 0, c)),
                  pl.BlockSpec((1, 1, CH_U, 2 * CH_P), lambda g, c, d: (d, c, 0, 0)),
                  pl.BlockSpec((1, 1, 2 * CH_P, CH_U), lambda g, c, d: (d, c, 0, 0)),
                  pl.BlockSpec((1, 1, SUB, 2 * CH_P), lambda g, c, d: (d, c, 0, 0)),
                  pl.BlockSpec((1, 1, SUB, 2 * CH_P), lambda g, c, d: (d, c, 0, 0)),
                  pl.BlockSpec((1, CH_U), lambda g, c, d: (0, c))],
        out_specs=[pl.BlockSpec((1, GRP_ROWS, CH_U), lambda g, c, d: (g, 0, c)),
                   pl.BlockSpec((1, 1, SUB, 2 * CH_P), lambda g, c, d: (d, g, 0, c))],
        out_shape=[jax.ShapeDtypeStruct((N_GRP, GRP_ROWS, D_S5), F32),
                   jax.ShapeDtypeStruct((2, N_GRP, SUB, N_S5_CHUNKS * 2 * CH_P), F32)],
        scratch_shapes=[pltpu.VMEM((GRP_ROWS, 2 * CH_P), F32)],
        compiler_params=_cparams(("arbitrary", "arbitrary", "arbitrary")),
        name="s5_scan",
    )(us, bbmat, ccmat, a8, init8, d_skip.reshape(1, D_S5))


def _fnet_w_kernel(c_ref, s_ref, w_ref, m1_ref, m2_ref):
    w = w_ref[...]
    m1_ref[...] = jnp.dot(c_ref[...], w, preferred_element_type=F32, precision=HIGHEST).astype(BF16)
    m2_ref[...] = jnp.dot(s_ref[...], w, preferred_element_type=F32, precision=HIGHEST).astype(BF16)


def _fnet_kernel(u_ref, cos_ref, sin_ref, m1_ref, m2_ref, o_ref):
    u = u_ref[0]
    v1 = jnp.dot(u, m1_ref[...], preferred_element_type=F32).astype(BF16)
    v2 = jnp.dot(u, m2_ref[...], preferred_element_type=F32).astype(BF16)
    o_ref[0] = (jnp.dot(cos_ref[...].astype(BF16), v1, preferred_element_type=F32)
                - jnp.dot(sin_ref[...].astype(BF16), v2, preferred_element_type=F32)).astype(BF16)


def _dft_tables(n):
    k = np.arange(n, dtype=np.int64)
    ang = (2.0 * np.pi / n) * ((k[:, None] * k[None, :]) % n).astype(np.float64)
    scale = 1.0 / math.sqrt(n)
    return (np.cos(ang) * scale).astype(np.float32), (np.sin(ang) * scale).astype(np.float32)


def _fnet(u, m1, m2):
    n_seq, length, _ = u.shape
    cos_l, sin_l = _dft_tables(length)
    full = lambda i: (0, 0)
    return pl.pallas_call(
        _fnet_kernel,
        grid=(n_seq,),
        in_specs=[pl.BlockSpec((1, length, D_FN), lambda i: (i, 0, 0)),
                  pl.BlockSpec((length, length), full),
                  pl.BlockSpec((length, length), full),
                  pl.BlockSpec((D_FN, D_FN), full),
                  pl.BlockSpec((D_FN, D_FN), full)],
        out_specs=pl.BlockSpec((1, length, D_FN), lambda i: (i, 0, 0)),
        out_shape=jax.ShapeDtypeStruct(u.shape, BF16),
        compiler_params=_cparams(("arbitrary",)),
        name=f"fnet_{length}",
    )(u, jnp.asarray(cos_l), jnp.asarray(sin_l), m1, m2)


def _fnet_weights(w_fnet):
    n_g = D_FN // FN_GW
    cos_c, sin_c = _dft_tables(FN_GW)
    eye = np.eye(n_g, dtype=np.float32)
    cos_bd = np.kron(eye, cos_c)
    sin_bd = np.kron(eye, sin_c)
    w_bd = (w_fnet[:, :, None, :] * jnp.asarray(eye)[:, None, :, None]).reshape(D_FN, D_FN)
    out = jax.ShapeDtypeStruct((D_FN, D_FN), BF16)
    return pl.pallas_call(_fnet_w_kernel, out_shape=(out, out), name="fnet_weights")(
        jnp.asarray(cos_bd), jnp.asarray(sin_bd), w_bd)


def _gelu_tanh(x):
    return 0.5 * x * (1.0 + jnp.tanh(math.sqrt(2.0 / math.pi) * (x + 0.044715 * (x * x * x))))


def _post_kernel(y_ref, yf_ref, xp_ref, xs_ref, pos_ref, mod_ref, wglu_ref, wout_ref, g2_ref, wr_ref,
                 br_ref, permt_ref, x1_ref, h2_ref, comb_ref):
    g = pl.program_id(0)
    z = _gelu_tanh(y_ref[0])
    gate = jnp.dot(z.astype(BF16), wglu_ref[...].astype(BF16), preferred_element_type=F32)
    gl = (z * jax.nn.sigmoid(gate)).astype(BF16)
    permt = permt_ref[...]
    n_q = TC // PERM_T
    nat = [jnp.dot(permt, gl[q * SUB * PERM_T:(q + 1) * SUB * PERM_T], preferred_element_type=F32).astype(BF16)
           for q in range(n_q)]
    gl_nat = jnp.concatenate(
        [nat[q][s * PERM_T:(s + 1) * PERM_T] for s in range(SUB) for q in range(n_q)], axis=0)
    w_out = wout_ref[...].astype(BF16)
    mixed = (jnp.dot(gl_nat, w_out[:D_S5], preferred_element_type=F32)
             + jnp.dot(yf_ref[0].reshape(SUB * TC, D_FN), w_out[D_S5:], preferred_element_type=F32))
    x = _load_x(g, xp_ref, xs_ref, pos_ref)
    mod = mod_ref[0]
    gate1 = mod[:, :, 2 * D:3 * D]
    shift2 = mod[:, :, 3 * D:4 * D]
    scale2 = mod[:, :, 4 * D:5 * D]
    x1 = x + gate1 * mixed.reshape(SUB, TC, D)
    x1_ref[0] = x1
    h2 = _rms(x1, g2_ref[...]) * (1.0 + scale2) + shift2
    h2_ref[0] = h2.astype(BF16)

    logits = jnp.dot(h2.reshape(SUB * TC, D), wr_ref[...], preferred_element_type=F32,
                     precision=HIGHEST) + br_ref[...]
    lane = lax.broadcasted_iota(jnp.int32, logits.shape, 1)
    top_v, hots = [], []
    cur = logits
    for _ in range(TOP_K):
        m = jnp.max(cur, axis=-1, keepdims=True)
        idx = jnp.min(jnp.where(cur == m, lane, E_PAD), axis=-1, keepdims=True)
        hot = lane == idx
        top_v.append(m)
        hots.append(hot)
        cur = jnp.where(hot, -3.0e38, cur)
    exps = [jnp.exp(v - top_v[0]) for v in top_v]
    denom = exps[0] + exps[1] + exps[2] + exps[3]
    comb = jnp.zeros(logits.shape, F32)
    for k in range(TOP_K):
        comb = comb + jnp.where(hots[k], exps[k] / denom, 0.0)
    comb_ref[0] = comb.reshape(SUB, TC, E_PAD)


def _post(y, yf, xp4, xs4, pos3, modg, w_glu, w_out, norm2_g, w_router, b_router, permt):
    xp_spec, xs_spec, pos_spec = _x_specs()
    const2 = lambda g, tc: (0, 0)
    wr = jnp.zeros((D, E_PAD), F32).at[:, :N_EXPERTS].set(w_router)
    br = jnp.full((1, E_PAD), -1.0e30, F32).at[0, :N_EXPERTS].set(b_router)
    blk = lambda w: pl.BlockSpec((1, SUB, TC, w), lambda g, tc: (g, 0, tc, 0))
    return pl.pallas_call(
        _post_kernel,
        grid=(N_GRP, N_TC),
        in_specs=[pl.BlockSpec((1, SUB * TC, D_S5), lambda g, tc: (g, tc, 0)),
                  blk(D_FN), xp_spec, xs_spec, pos_spec,
                  pl.BlockSpec((1, SUB, 1, 6 * D), lambda g, tc: (g, 0, 0, 0)),
                  pl.BlockSpec((D_S5, D_S5), const2),
                  pl.BlockSpec((D, D), const2),
                  pl.BlockSpec((1, D), const2),
                  pl.BlockSpec((D, E_PAD), const2),
                  pl.BlockSpec((1, E_PAD), const2),
                  pl.BlockSpec((SUB * PERM_T, SUB * PERM_T), const2)],
        out_specs=[blk(D), blk(D), blk(E_PAD)],
        out_shape=[jax.ShapeDtypeStruct((N_GRP, SUB, L_BLK, D), F32),
                   jax.ShapeDtypeStruct((N_GRP, SUB, L_BLK, D), BF16),
                   jax.ShapeDtypeStruct((N_GRP, SUB, L_BLK, E_PAD), F32)],
        compiler_params=_cparams(("arbitrary", "arbitrary")),
        name="post_mixer",
    )(y, yf, xp4, xs4, pos3, modg, w_glu, w_out, norm2_g.reshape(1, D), wr, br, permt)


TBLK = L_BLK
N_BLK = T_TOK // TBLK
SEG_ALIGN = 8
RB = 1280
assert RB >= TBLK * TOP_K + N_EXPERTS * (SEG_ALIGN - 1) and RB % 128 == 0
TM = 256
TM_SHIFT = 8
assert 1 << TM_SHIFT == TM
R_TOT = T_TOK * TOP_K + N_BLK * N_EXPERTS * (SEG_ALIGN - 1) + TM
TAB_ROWS = 32
assert TAB_ROWS >= N_BLK


def _plan_kernel(comb_ref, z_ref, ptab_ref, loff_ref, gst_ref, eoff_ref):
    row = lax.broadcasted_iota(jnp.int32, (TBLK, TBLK), 0)
    col = lax.broadcasted_iota(jnp.int32, (TBLK, TBLK), 1)
    earlier = jnp.where(row > col, 1.0, 0.0).astype(BF16)
    ptab_ref[...] = jnp.zeros_like(ptab_ref)

    def body(b, _):
        rows = pl.ds(pl.multiple_of(b * TBLK, TBLK), TBLK)
        m = jnp.where(comb_ref[rows, :] > 0.0, 1.0, 0.0)
        rank = jnp.dot(earlier, m.astype(BF16), preferred_element_type=F32)
        z_ref[rows, :] = m * (rank + 1.0)
        n = jnp.sum(m, axis=0, keepdims=True)
        ptab_ref[pl.ds(b, 1), :] = jnp.floor((n + (SEG_ALIGN - 1)) * (1.0 / SEG_ALIGN)) * SEG_ALIGN
        return 0

    lax.fori_loop(0, N_BLK, body, 0)

    ptab = ptab_ref[...]
    er = lax.broadcasted_iota(jnp.int32, (E_PAD, E_PAD), 0)
    ec = lax.broadcasted_iota(jnp.int32, (E_PAD, E_PAD), 1)
    before = jnp.where(er < ec, 1.0, 0.0)
    exact = functools.partial(jnp.dot, preferred_element_type=F32, precision=HIGHEST)
    loff_ref[...] = exact(ptab, before)
    tot = jnp.sum(ptab, axis=0, keepdims=True)
    eoff = exact(jnp.broadcast_to(tot, (SUB, E_PAD)), before)
    eoff_ref[...] = eoff
    br = lax.broadcasted_iota(jnp.int32, (TAB_ROWS, TAB_ROWS), 0)
    bc = lax.broadcasted_iota(jnp.int32, (TAB_ROWS, TAB_ROWS), 1)
    gst_ref[...] = eoff[0:1] + exact(jnp.where(br > bc, 1.0, 0.0), ptab)


def _plan(comb):
    tab = jax.ShapeDtypeStruct((TAB_ROWS, E_PAD), F32)
    return pl.pallas_call(
        _plan_kernel,
        out_shape=(jax.ShapeDtypeStruct((T_TOK, E_PAD), F32), tab, tab, tab,
                   jax.ShapeDtypeStruct((SUB, E_PAD), F32)),
        compiler_params=pltpu.CompilerParams(vmem_limit_bytes=VMEM_LIMIT),
        name="moe_plan",
    )(comb)


def _sort_matrix(b, z_ref, loff_ref, ptab_ref):
    loff = loff_ref[pl.ds(b, 1), :]
    size = ptab_ref[pl.ds(b, 1), :]
    r = lax.broadcasted_iota(jnp.int32, (RB, E_PAD), 0).astype(F32)
    owner = jnp.where(r >= loff, jnp.where(r < loff + size, 1.0, 0.0), 0.0)
    rank1 = r[:, 0:1] - jnp.sum(owner * loff, axis=-1, keepdims=True) + 1.0
    zt = z_ref[...].T
    v = jnp.dot(owner.astype(BF16), zt.astype(BF16), preferred_element_type=F32)
    return jnp.where(v == rank1, 1.0, 0.0), owner


def _segment_copies(b, loff_s, gst_s, p8_s, make_copy):
    def per_expert(e, n_started):
        k = b * N_EXPERTS + e
        chunks = p8_s[k]
        local0 = loff_s[k]
        global0 = gst_s[k]

        def per_chunk(j, _):
            make_copy(pl.multiple_of(local0 + j * SEG_ALIGN, SEG_ALIGN),
                      pl.multiple_of(global0 + j * SEG_ALIGN, SEG_ALIGN)).start()
            return 0

        lax.fori_loop(0, chunks, per_chunk, 0)
        return n_started + chunks

    return lax.fori_loop(0, N_EXPERTS, per_expert, 0)


def _wait_copies(n, make_copy):
    def wait_one(i, _):
        make_copy(0, 0).wait()
        return 0
    lax.fori_loop(0, n, wait_one, 0)


def _zero_rows_from(hbm, zeros_vmem, first_row, sem):
    def copy_to(row):
        return pltpu.make_async_copy(zeros_vmem, hbm.at[pl.ds(row, TM)], sem)

    n_full = lax.shift_right_logical(R_TOT - first_row, TM_SHIFT)

    def start_one(j, _):
        copy_to(pl.multiple_of(first_row + j * TM, SEG_ALIGN)).start()
        return 0

    def wait_one(j, _):
        copy_to(0).wait()
        return 0

    lax.fori_loop(0, n_full, start_one, 0)
    lax.fori_loop(0, n_full, wait_one, 0)
    last = copy_to(R_TOT - TM)
    last.start()
    last.wait()


def _dispatch_kernel(loff_s, gst_s, p8_s, tot_s, h_ref, z_ref, loff_ref, ptab_ref, xs_hbm, xbuf, sem):
    b = pl.program_id(0)
    pm, _ = _sort_matrix(b, z_ref, loff_ref, ptab_ref)
    xbuf[...] = jnp.dot(pm.astype(BF16), h_ref[...], preferred_element_type=F32)

    def make_copy(local_row, global_row):
        return pltpu.make_async_copy(xbuf.at[pl.ds(local_row, SEG_ALIGN)],
                                     xs_hbm.at[pl.ds(global_row, SEG_ALIGN)], sem)

    _wait_copies(_segment_copies(b, loff_s, gst_s, p8_s, make_copy), make_copy)

    @pl.when(b == N_BLK - 1)
    def _():
        xbuf[0:TM, :] = jnp.zeros((TM, D), F32)
        _zero_rows_from(xs_hbm, xbuf.at[0:TM], tot_s[0], sem)


def _dispatch(h2, z, loff, ptab, loff_i, gst_i, p8_i, tot_i):
    whole = pl.BlockSpec((TAB_ROWS, E_PAD), lambda b, *_: (0, 0))
    return pl.pallas_call(
        _dispatch_kernel,
        grid_spec=pltpu.PrefetchScalarGridSpec(
            num_scalar_prefetch=4,
            grid=(N_BLK,),
            in_specs=[pl.BlockSpec((TBLK, D), lambda b, *_: (b, 0)),
                      pl.BlockSpec((TBLK, E_PAD), lambda b, *_: (b, 0)),
                      whole, whole],
            out_specs=pl.BlockSpec(memory_space=pl.ANY),
            scratch_shapes=[pltpu.VMEM((RB, D), F32), pltpu.SemaphoreType.DMA(())]),
        out_shape=jax.ShapeDtypeStruct((R_TOT, D), F32),
        compiler_params=_cparams(("arbitrary",)),
        name="moe_dispatch",
    )(loff_i, gst_i, p8_i, tot_i, h2, z, loff, ptab)


def _expert_kernel(eoff_s, xs_hbm, wg_ref, bg_ref, wu_ref, bu_ref, wd_ref, bd_ref, ys_hbm, xin, yout,
                   sem_in, sem_out):
    e = pl.program_id(0)
    start = eoff_s[e]
    n_tiles = lax.shift_right_logical(eoff_s[e + 1] - start + (TM - 1), TM_SHIFT)
    wg = wg_ref[0].astype(BF16)
    wu = wu_ref[0].astype(BF16)
    wd = wd_ref[0].astype(BF16)

    def tile_rows(i):
        return pl.ds(pl.multiple_of(start + i * TM, SEG_ALIGN), TM)

    def in_copy(i, slot):
        return pltpu.make_async_copy(xs_hbm.at[tile_rows(i)], xin.at[slot], sem_in.at[slot])

    def out_copy(i):
        return pltpu.make_async_copy(yout, ys_hbm.at[tile_rows(i)], sem_out)

    @pl.when(n_tiles > 0)
    def _():
        in_copy(0, 0).start()

    def body(i, _):
        slot = lax.rem(i, 2)
        in_copy(i, slot).wait()

        @pl.when(i + 1 < n_tiles)
        def _():
            in_copy(i + 1, 1 - slot).start()

        x = xin[slot].astype(BF16)
        gate = jnp.dot(x, wg, preferred_element_type=F32) + bg_ref[0]
        up = jnp.dot(x, wu, preferred_element_type=F32) + bu_ref[0]
        gate = jnp.minimum(gate, SWIGLU_LIMIT)
        up = jnp.clip(up, -SWIGLU_LIMIT, SWIGLU_LIMIT)
        act = (up + 1.0) * gate * jax.nn.sigmoid(SWIGLU_ALPHA * gate)
        y = jnp.dot(act.astype(BF16), wd, preferred_element_type=F32) + bd_ref[0]

        @pl.when(i > 0)
        def _():
            out_copy(i - 1).wait()

        yout[...] = y
        out_copy(i).start()
        return 0

    lax.fori_loop(0, n_tiles, body, 0)

    @pl.when(n_tiles > 0)
    def _():
        out_copy(n_tiles - 1).wait()

    @pl.when(e == N_EXPERTS - 1)
    def _():
        yout[...] = jnp.zeros((TM, D), F32)
        _zero_rows_from(ys_hbm, yout, eoff_s[N_EXPERTS], sem_out)


def _experts(xs, eoff_i, w_gate, b_gate, w_up, b_up, w_down, b_down):
    wspec = pl.BlockSpec((1, D, D), lambda e, *_: (e, 0, 0))
    bspec = pl.BlockSpec((1, 1, D), lambda e, *_: (e, 0, 0))
    return pl.pallas_call(
        _expert_kernel,
        grid_spec=pltpu.PrefetchScalarGridSpec(
            num_scalar_prefetch=1,
            grid=(N_EXPERTS,),
            in_specs=[pl.BlockSpec(memory_space=pl.ANY), wspec, bspec, wspec, bspec, wspec, bspec],
            out_specs=pl.BlockSpec(memory_space=pl.ANY),
            scratch_shapes=[pltpu.VMEM((2, TM, D), F32), pltpu.VMEM((TM, D), F32),
                            pltpu.SemaphoreType.DMA((2,)), pltpu.SemaphoreType.DMA(())]),
        out_shape=jax.ShapeDtypeStruct((R_TOT, D), F32),
        compiler_params=_cparams(("arbitrary",)),
        name="moe_experts",
    )(eoff_i, xs, w_gate, b_gate.reshape(N_EXPERTS, 1, D), w_up, b_up.reshape(N_EXPERTS, 1, D),
      w_down, b_down.reshape(N_EXPERTS, 1, D))


def _combine_kernel(loff_s, gst_s, p8_s, ys_hbm, z_ref, comb_ref, loff_ref, ptab_ref, x1_ref, mod_ref, gf_ref,
                    o_ref, ybuf, sem):
    b = pl.program_id(0)

    def make_copy(local_row, global_row):
        return pltpu.make_async_copy(ys_hbm.at[pl.ds(global_row, SEG_ALIGN)],
                                     ybuf.at[pl.ds(local_row, SEG_ALIGN)], sem)

    n_started = _segment_copies(b, loff_s, gst_s, p8_s, make_copy)

    last = b * N_EXPERTS + N_EXPERTS - 1
    used_chunks = lax.shift_right_logical(loff_s[last], 3) + p8_s[last]

    def zero_chunk(j, _):
        ybuf[pl.ds(pl.multiple_of(j * SEG_ALIGN, SEG_ALIGN), SEG_ALIGN), :] = jnp.zeros((SEG_ALIGN, D), F32)
        return 0

    lax.fori_loop(used_chunks, RB // SEG_ALIGN, zero_chunk, 0)
    _wait_copies(n_started, make_copy)

    pm, owner = _sort_matrix(b, z_ref, loff_ref, ptab_ref)
    w_all = jnp.dot(owner, comb_ref[...].T, preferred_element_type=F32, precision=HIGHEST)
    w_row = jnp.sum(pm * w_all, axis=-1, keepdims=True)
    y = (ybuf[...] * w_row).astype(BF16)
    moe = jnp.dot(pm.T.astype(BF16), y, preferred_element_type=F32)
    gate2 = mod_ref[0][:, 5 * D:6 * D]
    x2 = x1_ref[0] + gate2 * moe
    o_ref[0] = _rms(x2, gf_ref[...])


def _combine(ys, z, comb, loff, ptab, x1, modv, norm_f_g, loff_i, gst_i, p8_i):
    whole = pl.BlockSpec((TAB_ROWS, E_PAD), lambda b, *_: (0, 0))
    tok = pl.BlockSpec((TBLK, E_PAD), lambda b, *_: (b, 0))
    return pl.pallas_call(
        _combine_kernel,
        grid_spec=pltpu.PrefetchScalarGridSpec(
            num_scalar_prefetch=3,
            grid=(N_BLK,),
            in_specs=[pl.BlockSpec(memory_space=pl.ANY), tok, tok, whole, whole,
                      pl.BlockSpec((1, TBLK, D), lambda b, *_: (b, 0, 0)),
                      pl.BlockSpec((1, 1, 6 * D), lambda b, *_: (b, 0, 0)),
                      pl.BlockSpec((1, D), lambda b, *_: (0, 0))],
            out_specs=pl.BlockSpec((1, TBLK, D), lambda b, *_: (b, 0, 0)),
            scratch_shapes=[pltpu.VMEM((RB, D), F32), pltpu.SemaphoreType.DMA(())]),
        out_shape=jax.ShapeDtypeStruct((N_BLK, TBLK, D), F32),
        compiler_params=_cparams(("arbitrary",)),
        name="moe_combine",
    )(loff_i, gst_i, p8_i, ys, z, comb, loff, ptab, x1, modv, norm_f_g.reshape(1, D))


def _moe_and_final(x1, h2, comb, modg, norm_f_g, w_gate, b_gate, w_up, b_up, w_down, b_down):
    z, ptab, loff, gst, eoff = _plan(comb)
    as_scalars = lambda t: t[:N_BLK, :N_EXPERTS].astype(jnp.int32).reshape(N_BLK * N_EXPERTS)
    loff_i, gst_i = as_scalars(loff), as_scalars(gst)
    p8_i = as_scalars(ptab) // SEG_ALIGN
    eoff_i = eoff[0, :N_EXPERTS + 1].astype(jnp.int32)
    xs = _dispatch(h2, z, loff, ptab, loff_i, gst_i, p8_i, eoff_i[N_EXPERTS:])
    ys = _experts(xs, eoff_i, w_gate, b_gate, w_up, b_up, w_down, b_down)
    return _combine(ys, z, comb, loff, ptab, x1, modg.reshape(N_BLK, 1, 6 * D), norm_f_g,
                    loff_i, gst_i, p8_i)


def _grid_pos_embed(n_tokens, dim):
    rows = n_tokens // GRID_W
    t = jnp.arange(rows * GRID_W)
    r = (t // GRID_W).astype(F32)
    col = (t % GRID_W).astype(F32)
    q = dim // 4
    omega = 1.0 / POS_TEMP ** (jnp.arange(q, dtype=F32) / q)

    def emb(p):
        a = p[:, None] * omega[None, :]
        return jnp.concatenate([jnp.sin(a), jnp.cos(a)], axis=-1)

    return jnp.concatenate([emb(r), emb(col)], axis=-1)


def _block_diag_chunks(m):
    _, _, a, b = m.shape
    m = m.reshape(2, N_S5_CHUNKS, S5_CHUNK, a, b)
    eye = jnp.eye(S5_CHUNK, dtype=m.dtype)
    bd = m[:, :, :, :, None, :] * eye[None, None, :, None, :, None]
    return bd.reshape(2, N_S5_CHUNKS, S5_CHUNK * a, S5_CHUNK * b)


def kernel(x_prompt, x_sample, c, state_s5_re, state_s5_im, c_ctx, w_ada, b_ada, norm1_g, w_in, s5_lam_re,
           s5_lam_im, s5_log_dt, s5_b_re, s5_b_im, s5_c_re, s5_c_im, s5_d, s5_w_glu, w_fnet, w_out, norm2_g,
           w_router, b_router, w_gate, b_gate, w_up, b_up, w_down, b_down, norm_f_g):
    n_ctx, n_lat = x_prompt.shape[0], x_sample.shape[0]
    assert x_prompt.shape == (SUB * (N_GRP - 1), L_BLK, D) and x_sample.shape == (2, LAT_CHUNKS * L_BLK, D)
    assert w_ada.shape[0] == 1, "one trunk layer"
    layer = 0

    cvec = jnp.zeros((SUB, D), F32).at[0].set(c_ctx).at[1:1 + n_lat].set(c)
    mod = _adaln(cvec, w_ada[layer], b_ada[layer])
    lat_rows = jnp.repeat(mod[1:1 + n_lat], LAT_CHUNKS, axis=0)
    ctx_rows = jnp.broadcast_to(mod[0:1], (SUB, 6 * D))
    modg = jnp.stack([ctx_rows] * (N_GRP - 1) + [lat_rows])[:, :, None, :]

    xp4 = x_prompt.reshape(N_GRP - 1, SUB, L_BLK, D)
    xs4 = x_sample.reshape(1, SUB, L_BLK, D)
    pos3 = _grid_pos_embed(LAT_CHUNKS * L_BLK, D).reshape(LAT_CHUNKS, L_BLK, D)
    perm = _perm_matrix()

    us, uf = _pre(xp4, xs4, pos3, modg, norm1_g[layer], w_in[layer], jnp.asarray(perm, BF16))

    a_re, a_im, bb_re, bb_im = _discretise(s5_lam_re[layer], s5_lam_im[layer], s5_log_dt[layer],
                                           s5_b_re[layer], s5_b_im[layer])
    a_cat = jnp.concatenate([a_re.reshape(2, N_S5_CHUNKS, CH_P), a_im.reshape(2, N_S5_CHUNKS, CH_P)], axis=-1)
    a8 = jnp.broadcast_to(a_cat[:, :, None, :], (2, N_S5_CHUNKS, SUB, 2 * CH_P))
    bb_re = bb_re.reshape(2, N_S5_GROUPS, S5_H, S5_P)
    bb_im = bb_im.reshape(2, N_S5_GROUPS, S5_H, S5_P)
    bbmat = jnp.concatenate([_block_diag_chunks(bb_re), _block_diag_chunks(bb_im)], axis=-1).astype(BF16)
    c_re_t = jnp.swapaxes(s5_c_re[layer], -1, -2)
    c_im_t = jnp.swapaxes(s5_c_im[layer], -1, -2)
    ccmat = jnp.concatenate([_block_diag_chunks(c_re_t), _block_diag_chunks(-c_im_t)], axis=-2).astype(BF16)

    st = jnp.stack([state_s5_re[:, layer], state_s5_im[:, layer]], axis=2)
    st = st.reshape(n_lat, 2, 2, N_S5_CHUNKS, CH_P)
    st = jnp.transpose(st, (1, 3, 0, 2, 4)).reshape(2, N_S5_CHUNKS, n_lat, 2 * CH_P)
    init8 = jnp.zeros((2, N_S5_CHUNKS, n_lat, LAT_CHUNKS, 2 * CH_P), F32)
    init8 = init8.at[0, :, :, 0].set(st[0]).at[1, :, :, LAT_CHUNKS - 1].set(st[1])
    init8 = init8.reshape(2, N_S5_CHUNKS, SUB, 2 * CH_P)

    y_s5, fin = _s5(us, bbmat, ccmat, a8, init8, s5_d[layer])

    m1, m2 = _fnet_weights(w_fnet[layer])
    uf_ctx = uf[:N_GRP - 1].reshape(n_ctx, L_BLK, D_FN)
    uf_lat = uf[N_GRP - 1].reshape(n_lat, LAT_CHUNKS * L_BLK, D_FN)
    yf = jnp.concatenate([_fnet(uf_ctx, m1, m2).reshape(N_GRP - 1, SUB, L_BLK, D_FN),
                          _fnet(uf_lat, m1, m2).reshape(1, SUB, L_BLK, D_FN)], axis=0)

    x1, h2, comb = _post(y_s5, yf, xp4, xs4, pos3, modg, s5_w_glu[layer], w_out[layer], norm2_g[layer],
                         w_router[layer], b_router[layer], jnp.asarray(perm.T, BF16))

    y_all = _moe_and_final(x1.reshape(N_BLK, TBLK, D), h2.reshape(T_TOK, D), comb.reshape(T_TOK, E_PAD), modg,
                           norm_f_g, w_gate[layer], b_gate[layer], w_up[layer], b_up[layer], w_down[layer],
                           b_down[layer])

    y_prompt = y_all[:n_ctx]
    y_sample = y_all[n_ctx:].reshape(n_lat, LAT_CHUNKS * L_BLK, D)

    fin = fin[:, :N_GRP - 1].reshape(2, N_GRP - 1, SUB, N_S5_CHUNKS, 2, CH_P)
    fin = jnp.transpose(fin, (4, 1, 2, 0, 3, 5)).reshape(2, n_ctx, 1, 2, N_S5_GROUPS, S5_P)
    return (y_prompt, y_sample, fin[0], fin[1])
```

```python
import functools
import math

import numpy as np
import jax
import jax.numpy as jnp
from jax import lax
from jax.experimental import pallas as pl
from jax.experimental.pallas import tpu as pltpu

F32 = jnp.float32
BF16 = jnp.bfloat16
HIGHEST = lax.Precision.HIGHEST

D = 1024
D_S5 = 768
S5_H = 16
S5_P = 64
N_S5_GROUPS = 48
D_FN = 256
FN_GW = 64
N_EXPERTS = 32
TOP_K = 4
E_PAD = 128
SWIGLU_LIMIT = 7.0
SWIGLU_ALPHA = 1.702
RMS_EPS = 1e-6
POS_TEMP = 10000.0
GRID_W = 64

L_BLK = 256
SUB = 8
N_GRP = 3
GRP_ROWS = L_BLK * SUB
LAT_CHUNKS = 4
T_TOK = N_GRP * GRP_ROWS
TC = 128
N_TC = L_BLK // TC
PERM_T = 32
S5_CHUNK = 16
N_S5_CHUNKS = N_S5_GROUPS // S5_CHUNK
CH_U = S5_CHUNK * S5_H
CH_P = S5_CHUNK * S5_P
ROW_BLK = 256
VMEM_LIMIT = 56 * 1024 * 1024


def _cparams(sem):
    return pltpu.CompilerParams(dimension_semantics=sem, vmem_limit_bytes=VMEM_LIMIT)


def _adaln_kernel(c_ref, w_ref, b_ref, o_ref):
    c = c_ref[...]
    s = c * jax.nn.sigmoid(c)
    o_ref[...] = jnp.dot(s, w_ref[...], preferred_element_type=F32, precision=HIGHEST) + b_ref[...]


def _adaln(cvec, w_ada, b_ada):
    n = w_ada.shape[1]
    return pl.pallas_call(
        _adaln_kernel,
        grid=(n // D,),
        in_specs=[pl.BlockSpec((SUB, D), lambda j: (0, 0)),
                  pl.BlockSpec((D, D), lambda j: (0, j)),
                  pl.BlockSpec((1, D), lambda j: (0, j))],
        out_specs=pl.BlockSpec((SUB, D), lambda j: (0, j)),
        out_shape=jax.ShapeDtypeStruct((SUB, n), F32),
        compiler_params=_cparams(("arbitrary",)),
        name="adaln",
    )(cvec, w_ada, b_ada.reshape(1, n))


def _disc_kernel(lre_ref, lim_ref, ldt_ref, bre_ref, bim_ref, are_ref, aim_ref, bbre_ref, bbim_ref):
    dt = jnp.exp(ldt_ref[...])
    a_re = jnp.minimum(lre_ref[...], -1e-4)
    a_im = lim_ref[...]
    mag = jnp.exp(a_re * dt)
    ab_re = mag * jnp.cos(a_im * dt)
    ab_im = mag * jnp.sin(a_im * dt)
    den = a_re * a_re + a_im * a_im
    nr = ab_re - 1.0
    f_re = (nr * a_re + ab_im * a_im) / den
    f_im = (ab_im * a_re - nr * a_im) / den
    b_re = bre_ref[...]
    b_im = bim_ref[...]
    are_ref[...] = ab_re
    aim_ref[...] = ab_im
    bbre_ref[...] = f_re * b_re - f_im * b_im
    bbim_ref[...] = f_re * b_im + f_im * b_re


def _discretise(lam_re, lam_im, log_dt, b_re, b_im):
    n = 2 * N_S5_GROUPS
    lre = lam_re.reshape(n, 1, S5_P)
    lim = lam_im.reshape(n, 1, S5_P)
    ldt = jnp.broadcast_to(log_dt.reshape(n, 1, 1), (n, 1, S5_P))
    bre = jnp.swapaxes(b_re, -1, -2).reshape(n, S5_H, S5_P)
    bim = jnp.swapaxes(b_im, -1, -2).reshape(n, S5_H, S5_P)
    small = jax.ShapeDtypeStruct((n, 1, S5_P), F32)
    big = jax.ShapeDtypeStruct((n, S5_H, S5_P), F32)
    return pl.pallas_call(_disc_kernel, out_shape=(small, small, big, big), name="s5_disc")(
        lre, lim, ldt, bre, bim)


def _rms(x, g):
    return x * lax.rsqrt(jnp.mean(x * x, axis=-1, keepdims=True) + RMS_EPS) * g


def _load_x(g, xp_ref, xs_ref, pos_ref):
    is_lat = g == N_GRP - 1
    pos = pos_ref[...]
    pos8 = jnp.concatenate([pos, pos], axis=0)
    return jnp.where(is_lat, xs_ref[0] + pos8, xp_ref[0])


def _x_specs():
    n_ctx = N_GRP - 1
    xp_spec = pl.BlockSpec(
        (1, SUB, TC, D),
        lambda g, tc: (jnp.minimum(g, n_ctx - 1), 0, jnp.where(g >= n_ctx, N_TC - 1, tc), 0))
    xs_spec = pl.BlockSpec((1, SUB, TC, D), lambda g, tc: (0, 0, jnp.where(g >= n_ctx, tc, 0), 0))
    pos_spec = pl.BlockSpec((LAT_CHUNKS, TC, D), lambda g, tc: (0, tc, 0))
    return xp_spec, xs_spec, pos_spec


def _perm_matrix():
    p = np.zeros((SUB * PERM_T, SUB * PERM_T), np.float32)
    for s in range(SUB):
        for j in range(PERM_T):
            p[j * SUB + s, s * PERM_T + j] = 1.0
    return p


def _pre_kernel(xp_ref, xs_ref, pos_ref, mod_ref, g1_ref, win_ref, perm_ref, us_ref, uf_ref):
    g = pl.program_id(0)
    x = _load_x(g, xp_ref, xs_ref, pos_ref)
    mod = mod_ref[0]
    shift1 = mod[:, :, 0:D]
    scale1 = mod[:, :, D:2 * D]
    h = _rms(x, g1_ref[...]) * (1.0 + scale1) + shift1
    h2d = h.reshape(SUB * TC, D).astype(BF16)
    proj = jnp.dot(h2d, win_ref[...].astype(BF16), preferred_element_type=F32)
    uf_ref[0] = proj[:, D_S5:].astype(BF16).reshape(SUB, TC, D_FN)
    u = proj[:, :D_S5].astype(BF16)
    perm = perm_ref[...]
    for q in range(TC // PERM_T):
        piece = jnp.concatenate(
            [u[s * TC + q * PERM_T: s * TC + (q + 1) * PERM_T] for s in range(SUB)], axis=0)
        us_ref[0, q * SUB * PERM_T:(q + 1) * SUB * PERM_T, :] = jnp.dot(
            perm, piece, preferred_element_type=F32).astype(BF16)


def _pre(xp4, xs4, pos3, modg, norm1_g, w_in, perm):
    xp_spec, xs_spec, pos_spec = _x_specs()
    return pl.pallas_call(
        _pre_kernel,
        grid=(N_GRP, N_TC),
        in_specs=[xp_spec, xs_spec, pos_spec,
                  pl.BlockSpec((1, SUB, 1, 6 * D), lambda g, tc: (g, 0, 0, 0)),
                  pl.BlockSpec((1, D), lambda g, tc: (0, 0)),
                  pl.BlockSpec((D, D), lambda g, tc: (0, 0)),
                  pl.BlockSpec((SUB * PERM_T, SUB * PERM_T), lambda g, tc: (0, 0))],
        out_specs=[pl.BlockSpec((1, SUB * TC, D_S5), lambda g, tc: (g, tc, 0)),
                   pl.BlockSpec((1, SUB, TC, D_FN), lambda g, tc: (g, 0, tc, 0))],
        out_shape=[jax.ShapeDtypeStruct((N_GRP, GRP_ROWS, D_S5), BF16),
                   jax.ShapeDtypeStruct((N_GRP, SUB, L_BLK, D_FN), BF16)],
        compiler_params=_cparams(("arbitrary", "arbitrary")),
        name="pre_mixer",
    )(xp4, xs4, pos3, modg, norm1_g.reshape(1, D), w_in, perm)


def _cmul(ar, ai, br, bi):
    return ar * br - ai * bi, ar * bi + ai * br


def _s5_kernel(us_ref, bb_ref, cc_ref, a_ref, init_ref, dskip_ref, y_ref, fin_ref, s_ref):
    g = pl.program_id(0)
    d = pl.program_id(2)
    n_rb = GRP_ROWS // ROW_BLK
    half = CH_P // 2

    def proj_in(rb, _):
        rows = pl.ds(pl.multiple_of(rb * ROW_BLK, ROW_BLK), ROW_BLK)
        s_ref[rows, :] = jnp.dot(us_ref[0, rows, :], bb_ref[0, 0], preferred_element_type=F32)
        return 0

    lax.fori_loop(0, n_rb, proj_in, 0)

    def tile_row(k):
        t = k + d * (L_BLK - 1 - 2 * k)
        return pl.ds(pl.multiple_of(t * SUB, SUB), SUB)

    is_lat = g == N_GRP - 1
    sub_id = lax.broadcasted_iota(jnp.int32, (SUB, half), 0) % LAT_CHUNKS

    for hh in range(2):
        re_cols = slice(hh * half, (hh + 1) * half)
        im_cols = slice(CH_P + hh * half, CH_P + (hh + 1) * half)
        a_re = a_ref[0, 0, :, re_cols]
        a_im = a_ref[0, 0, :, im_cols]
        zero = jnp.zeros((SUB, half), F32)
        s0_re = jnp.where(is_lat, init_ref[0, 0, :, re_cols], zero)
        s0_im = jnp.where(is_lat, init_ref[0, 0, :, im_cols], zero)

        def scan_body(i, carry, a_re=a_re, a_im=a_im, re_cols=re_cols, im_cols=im_cols):
            s_re, s_im = carry
            for j in range(8):
                rows = tile_row(i * 8 + j)
                n_re = a_re * s_re - a_im * s_im + s_ref[rows, re_cols]
                n_im = a_re * s_im + a_im * s_re + s_ref[rows, im_cols]
                s_ref[rows, re_cols] = n_re
                s_ref[rows, im_cols] = n_im
                s_re, s_im = n_re, n_im
            return s_re, s_im

        f_re, f_im = lax.fori_loop(0, L_BLK // 8, scan_body, (s0_re, s0_im))
        fin_ref[0, 0, :, re_cols] = f_re
        fin_ref[0, 0, :, im_cols] = f_im

        @pl.when(is_lat)
        def _(a_re=a_re, a_im=a_im, f_re=f_re, f_im=f_im, re_cols=re_cols, im_cols=im_cols):
            p_re, p_im = a_re, a_im
            for _ in range(8):
                p_re, p_im = _cmul(p_re, p_im, p_re, p_im)
            fwd = d == 0
            lo = jnp.where(fwd, 1, 0)
            hi = jnp.where(fwd, LAT_CHUNKS - 1, LAT_CHUNKS - 2)
            keep = (sub_id >= lo) & (sub_id <= hi)

            def from_prev(v):
                return jnp.where(keep, jnp.where(fwd, pltpu.roll(v, 1, 0), pltpu.roll(v, SUB - 1, 0)), 0.0)

            t_re, t_im = f_re, f_im
            for _ in range(LAT_CHUNKS - 2):
                m_re, m_im = _cmul(p_re, p_im, from_prev(t_re), from_prev(t_im))
                t_re, t_im = f_re + m_re, f_im + m_im
            c_re, c_im = from_prev(t_re), from_prev(t_im)

            def fix_body(i, carry):
                q_re, q_im = carry
                for j in range(8):
                    rows = tile_row(i * 8 + j)
                    m_re, m_im = _cmul(q_re, q_im, c_re, c_im)
                    s_ref[rows, re_cols] = s_ref[rows, re_cols] + m_re
                    s_ref[rows, im_cols] = s_ref[rows, im_cols] + m_im
                    q_re, q_im = _cmul(q_re, q_im, a_re, a_im)
                return q_re, q_im

            lax.fori_loop(0, L_BLK // 8, fix_body, (a_re, a_im))

    def proj_out(rb):
        rows = pl.ds(pl.multiple_of(rb * ROW_BLK, ROW_BLK), ROW_BLK)
        return rows, jnp.dot(s_ref[rows, :].astype(BF16), cc_ref[0, 0], preferred_element_type=F32)

    @pl.when(d == 0)
    def _():
        def body(rb, _):
            rows, yb = proj_out(rb)
            y_ref[0, rows, :] = dskip_ref[...] * us_ref[0, rows, :].astype(F32) + yb
            return 0
        lax.fori_loop(0, n_rb, body, 0)

    @pl.when(d == 1)
    def _():
        def body(rb, _):
            rows, yb = proj_out(rb)
            y_ref[0, rows, :] = y_ref[0, rows, :] + yb
            return 0
        lax.fori_loop(0, n_rb, body, 0)


def _s5(us, bbmat, ccmat, a8, init8, d_skip):
    return pl.pallas_call(
        _s5_kernel,
        grid=(N_GRP, N_S5_CHUNKS, 2),
        in_specs=[pl.BlockSpec((1, GRP_ROWS, CH_U), lambda g, c, d: (g, 0, c)),
                  pl.BlockSpec((1, 1, CH_U, 2 * CH_P), lambda g, c, d: (d, c, 0, 0)),
                  pl.BlockSpec((1, 1, 2 * CH_P, CH_U), lambda g, c, d: (d, c, 0, 0)),
                  pl.BlockSpec((1, 1, SUB, 2 * CH_P), lambda g, c, d: (d, c, 0, 0)),
                  pl.BlockSpec((1, 1, SUB, 2 * CH_P), lambda g, c, d: (d, c, 0, 0)),
                  pl.BlockSpec((1, CH_U), lambda g, c, d: (0, c))],
        out_specs=[pl.BlockSpec((1, GRP_ROWS, CH_U), lambda g, c, d: (g, 0, c)),
                   pl.BlockSpec((1, 1, SUB, 2 * CH_P), lambda g, c, d: (d, g, 0, c))],
        out_shape=[jax.ShapeDtypeStruct((N_GRP, GRP_ROWS, D_S5), F32),
                   jax.ShapeDtypeStruct((2, N_GRP, SUB, N_S5_CHUNKS * 2 * CH_P), F32)],
        scratch_shapes=[pltpu.VMEM((GRP_ROWS, 2 * CH_P), F32)],
        compiler_params=_cparams(("arbitrary", "arbitrary", "arbitrary")),
        name="s5_scan",
    )(us, bbmat, ccmat, a8, init8, d_skip.reshape(1, D_S5))


def _fnet_w_kernel(c_ref, s_ref, w_ref, m1_ref, m2_ref):
    w = w_ref[...]
    m1_ref[...] = jnp.dot(c_ref[...], w, preferred_element_type=F32, precision=HIGHEST).astype(BF16)
    m2_ref[...] = jnp.dot(s_ref[...], w, preferred_element_type=F32, precision=HIGHEST).astype(BF16)


def _fnet_kernel(u_ref, cos_ref, sin_ref, m1_ref, m2_ref, o_ref):
    u = u_ref[0]
    v1 = jnp.dot(u, m1_ref[...], preferred_element_type=F32).astype(BF16)
    v2 = jnp.dot(u, m2_ref[...], preferred_element_type=F32).astype(BF16)
    o_ref[0] = (jnp.dot(cos_ref[...].astype(BF16), v1, preferred_element_type=F32)
                - jnp.dot(sin_ref[...].astype(BF16), v2, preferred_element_type=F32)).astype(BF16)


def _dft_tables(n):
    k = np.arange(n, dtype=np.int64)
    ang = (2.0 * np.pi / n) * ((k[:, None] * k[None, :]) % n).astype(np.float64)
    scale = 1.0 / math.sqrt(n)
    return (np.cos(ang) * scale).astype(np.float32), (np.sin(ang) * scale).astype(np.float32)


def _fnet(u, m1, m2):
    n_seq, length, _ = u.shape
    cos_l, sin_l = _dft_tables(length)
    full = lambda i: (0, 0)
    return pl.pallas_call(
        _fnet_kernel,
        grid=(n_seq,),
        in_specs=[pl.BlockSpec((1, length, D_FN), lambda i: (i, 0, 0)),
                  pl.BlockSpec((length, length), full),
                  pl.BlockSpec((length, length), full),
                  pl.BlockSpec((D_FN, D_FN), full),
                  pl.BlockSpec((D_FN, D_FN), full)],
        out_specs=pl.BlockSpec((1, length, D_FN), lambda i: (i, 0, 0)),
        out_shape=jax.ShapeDtypeStruct(u.shape, BF16),
        compiler_params=_cparams(("arbitrary",)),
        name=f"fnet_{length}",
    )(u, jnp.asarray(cos_l), jnp.asarray(sin_l), m1, m2)


def _fnet_weights(w_fnet):
    n_g = D_FN // FN_GW
    cos_c, sin_c = _dft_tables(FN_GW)
    eye = np.eye(n_g, dtype=np.float32)
    cos_bd = np.kron(eye, cos_c)
    sin_bd = np.kron(eye, sin_c)
    w_bd = (w_fnet[:, :, None, :] * jnp.asarray(eye)[:, None, :, None]).reshape(D_FN, D_FN)
    out = jax.ShapeDtypeStruct((D_FN, D_FN), BF16)
    return pl.pallas_call(_fnet_w_kernel, out_shape=(out, out), name="fnet_weights")(
        jnp.asarray(cos_bd), jnp.asarray(sin_bd), w_bd)


def _gelu_tanh(x):
    return 0.5 * x * (1.0 + jnp.tanh(math.sqrt(2.0 / math.pi) * (x + 0.044715 * (x * x * x))))


def _post_kernel(y_ref, yf_ref, xp_ref, xs_ref, pos_ref, mod_ref, wglu_ref, wout_ref, g2_ref, wr_ref,
                 br_ref, permt_ref, x1_ref, h2_ref, comb_ref):
    g = pl.program_id(0)
    z = _gelu_tanh(y_ref[0])
    gate = jnp.dot(z.astype(BF16), wglu_ref[...].astype(BF16), preferred_element_type=F32)
    gl = (z * jax.nn.sigmoid(gate)).astype(BF16)
    permt = permt_ref[...]
    n_q = TC // PERM_T
    nat = [jnp.dot(permt, gl[q * SUB * PERM_T:(q + 1) * SUB * PERM_T], preferred_element_type=F32).astype(BF16)
           for q in range(n_q)]
    gl_nat = jnp.concatenate(
        [nat[q][s * PERM_T:(s + 1) * PERM_T] for s in range(SUB) for q in range(n_q)], axis=0)
    w_out = wout_ref[...].astype(BF16)
    mixed = (jnp.dot(gl_nat, w_out[:D_S5], preferred_element_type=F32)
             + jnp.dot(yf_ref[0].reshape(SUB * TC, D_FN), w_out[D_S5:], preferred_element_type=F32))
    x = _load_x(g, xp_ref, xs_ref, pos_ref)
    mod = mod_ref[0]
    gate1 = mod[:, :, 2 * D:3 * D]
    shift2 = mod[:, :, 3 * D:4 * D]
    scale2 = mod[:, :, 4 * D:5 * D]
    x1 = x + gate1 * mixed.reshape(SUB, TC, D)
    x1_ref[0] = x1
    h2 = _rms(x1, g2_ref[...]) * (1.0 + scale2) + shift2
    h2_ref[0] = h2.astype(BF16)

    logits = jnp.dot(h2.reshape(SUB * TC, D), wr_ref[...], preferred_element_type=F32,
                     precision=HIGHEST) + br_ref[...]
    lane = lax.broadcasted_iota(jnp.int32, logits.shape, 1)
    top_v, hots = [], []
    cur = logits
    for _ in range(TOP_K):
        m = jnp.max(cur, axis=-1, keepdims=True)
        idx = jnp.min(jnp.where(cur == m, lane, E_PAD), axis=-1, keepdims=True)
        hot = lane == idx
        top_v.append(m)
        hots.append(hot)
        cur = jnp.where(hot, -3.0e38, cur)
    exps = [jnp.exp(v - top_v[0]) for v in top_v]
    denom = exps[0] + exps[1] + exps[2] + exps[3]
    comb = jnp.zeros(logits.shape, F32)
    for k in range(TOP_K):
        comb = comb + jnp.where(hots[k], exps[k] / denom, 0.0)
    comb_ref[0] = comb.reshape(SUB, TC, E_PAD)


def _post(y, yf, xp4, xs4, pos3, modg, w_glu, w_out, norm2_g, w_router, b_router, permt):
    xp_spec, xs_spec, pos_spec = _x_specs()
    const2 = lambda g, tc: (0, 0)
    wr = jnp.zeros((D, E_PAD), F32).at[:, :N_EXPERTS].set(w_router)
    br = jnp.full((1, E_PAD), -1.0e30, F32).at[0, :N_EXPERTS].set(b_router)
    blk = lambda w: pl.BlockSpec((1, SUB, TC, w), lambda g, tc: (g, 0, tc, 0))
    return pl.pallas_call(
        _post_kernel,
        grid=(N_GRP, N_TC),
        in_specs=[pl.BlockSpec((1, SUB * TC, D_S5), lambda g, tc: (g, tc, 0)),
                  blk(D_FN), xp_spec, xs_spec, pos_spec,
                  pl.BlockSpec((1, SUB, 1, 6 * D), lambda g, tc: (g, 0, 0, 0)),
                  pl.BlockSpec((D_S5, D_S5), const2),
                  pl.BlockSpec((D, D), const2),
                  pl.BlockSpec((1, D), const2),
                  pl.BlockSpec((D, E_PAD), const2),
                  pl.BlockSpec((1, E_PAD), const2),
                  pl.BlockSpec((SUB * PERM_T, SUB * PERM_T), const2)],
        out_specs=[blk(D), blk(D), blk(E_PAD)],
        out_shape=[jax.ShapeDtypeStruct((N_GRP, SUB, L_BLK, D), F32),
                   jax.ShapeDtypeStruct((N_GRP, SUB, L_BLK, D), BF16),
                   jax.ShapeDtypeStruct((N_GRP, SUB, L_BLK, E_PAD), F32)],
        compiler_params=_cparams(("arbitrary", "arbitrary")),
        name="post_mixer",
    )(y, yf, xp4, xs4, pos3, modg, w_glu, w_out, norm2_g.reshape(1, D), wr, br, permt)


TBLK = L_BLK
N_BLK = T_TOK // TBLK
SEG_ALIGN = 8
RB = 1280
assert RB >= TBLK * TOP_K + N_EXPERTS * (SEG_ALIGN - 1) and RB % 128 == 0
TM = 256
TM_SHIFT = 8
assert 1 << TM_SHIFT == TM
R_TOT = T_TOK * TOP_K + N_BLK * N_EXPERTS * (SEG_ALIGN - 1) + TM
TAB_ROWS = 32
assert TAB_ROWS >= N_BLK
BIG_ROWS = 32
BIG_SHIFT = 2
assert SEG_ALIGN << BIG_SHIFT == BIG_ROWS


def _plan_kernel(comb_ref, z_ref, ptab_ref, loff_ref, gst_ref, eoff_ref):
    row = lax.broadcasted_iota(jnp.int32, (TBLK, TBLK), 0)
    col = lax.broadcasted_iota(jnp.int32, (TBLK, TBLK), 1)
    earlier = jnp.where(row > col, 1.0, 0.0).astype(BF16)
    ptab_ref[...] = jnp.zeros_like(ptab_ref)

    def body(b, _):
        rows = pl.ds(pl.multiple_of(b * TBLK, TBLK), TBLK)
        m = jnp.where(comb_ref[rows, :] > 0.0, 1.0, 0.0)
        rank = jnp.dot(earlier, m.astype(BF16), preferred_element_type=F32)
        z_ref[rows, :] = m * (rank + 1.0)
        n = jnp.sum(m, axis=0, keepdims=True)
        ptab_ref[pl.ds(b, 1), :] = jnp.floor((n + (SEG_ALIGN - 1)) * (1.0 / SEG_ALIGN)) * SEG_ALIGN
        return 0

    lax.fori_loop(0, N_BLK, body, 0)

    ptab = ptab_ref[...]
    er = lax.broadcasted_iota(jnp.int32, (E_PAD, E_PAD), 0)
    ec = lax.broadcasted_iota(jnp.int32, (E_PAD, E_PAD), 1)
    before = jnp.where(er < ec, 1.0, 0.0)
    exact = functools.partial(jnp.dot, preferred_element_type=F32, precision=HIGHEST)
    loff_ref[...] = exact(ptab, before)
    tot = jnp.sum(ptab, axis=0, keepdims=True)
    eoff = exact(jnp.broadcast_to(tot, (SUB, E_PAD)), before)
    eoff_ref[...] = eoff
    br = lax.broadcasted_iota(jnp.int32, (TAB_ROWS, TAB_ROWS), 0)
    bc = lax.broadcasted_iota(jnp.int32, (TAB_ROWS, TAB_ROWS), 1)
    gst_ref[...] = eoff[0:1] + exact(jnp.where(br > bc, 1.0, 0.0), ptab)


def _plan(comb):
    tab = jax.ShapeDtypeStruct((TAB_ROWS, E_PAD), F32)
    return pl.pallas_call(
        _plan_kernel,
        out_shape=(jax.ShapeDtypeStruct((T_TOK, E_PAD), F32), tab, tab, tab,
                   jax.ShapeDtypeStruct((SUB, E_PAD), F32)),
        compiler_params=pltpu.CompilerParams(vmem_limit_bytes=VMEM_LIMIT),
        name="moe_plan",
    )(comb)


def _sort_matrix(b, z_ref, loff_ref, ptab_ref):
    loff = loff_ref[pl.ds(b, 1), :]
    size = ptab_ref[pl.ds(b, 1), :]
    r = lax.broadcasted_iota(jnp.int32, (RB, E_PAD), 0).astype(F32)
    owner = jnp.where(r >= loff, jnp.where(r < loff + size, 1.0, 0.0), 0.0)
    rank1 = r[:, 0:1] - jnp.sum(owner * loff, axis=-1, keepdims=True) + 1.0
    zt = z_ref[...].T
    v = jnp.dot(owner.astype(BF16), zt.astype(BF16), preferred_element_type=F32)
    return jnp.where(v == rank1, 1.0, 0.0), owner


def _segment_copies(b, loff_s, gst_s, p8_s, make_copy):
    def per_expert(e, counts):
        k = b * N_EXPERTS + e
        chunks = p8_s[k]
        n_big = lax.shift_right_logical(chunks, BIG_SHIFT)
        n_small = chunks - n_big * (BIG_ROWS // SEG_ALIGN)
        local0 = loff_s[k]
        global0 = gst_s[k]

        def big(j, _):
            make_copy(pl.multiple_of(local0 + j * BIG_ROWS, SEG_ALIGN),
                      pl.multiple_of(global0 + j * BIG_ROWS, SEG_ALIGN), BIG_ROWS).start()
            return 0

        def small(j, _):
            off = n_big * BIG_ROWS + j * SEG_ALIGN
            make_copy(pl.multiple_of(local0 + off, SEG_ALIGN),
                      pl.multiple_of(global0 + off, SEG_ALIGN), SEG_ALIGN).start()
            return 0

        lax.fori_loop(0, n_big, big, 0)
        lax.fori_loop(0, n_small, small, 0)
        return counts[0] + n_big, counts[1] + n_small

    return lax.fori_loop(0, N_EXPERTS, per_expert, (0, 0))


def _wait_copies(counts, make_copy):
    for n, rows in zip(counts, (BIG_ROWS, SEG_ALIGN)):
        def wait_one(i, _, rows=rows):
            make_copy(0, 0, rows).wait()
            return 0
        lax.fori_loop(0, n, wait_one, 0)


def _zero_rows_from(hbm, zeros_vmem, first_row, sem):
    def copy_to(row):
        return pltpu.make_async_copy(zeros_vmem, hbm.at[pl.ds(row, TM)], sem)

    n_full = lax.shift_right_logical(R_TOT - first_row, TM_SHIFT)

    def start_one(j, _):
        copy_to(pl.multiple_of(first_row + j * TM, SEG_ALIGN)).start()
        return 0

    def wait_one(j, _):
        copy_to(0).wait()
        return 0

    lax.fori_loop(0, n_full, start_one, 0)
    lax.fori_loop(0, n_full, wait_one, 0)
    last = copy_to(R_TOT - TM)
    last.start()
    last.wait()


def _dispatch_kernel(loff_s, gst_s, p8_s, tot_s, h_ref, z_ref, loff_ref, ptab_ref, xs_hbm, xbuf, sem, pending):
    b = pl.program_id(0)

    def make_copy(local_row, global_row, rows):
        return pltpu.make_async_copy(xbuf.at[pl.ds(local_row, rows)], xs_hbm.at[pl.ds(global_row, rows)], sem)

    pm = _sort_matrix(b, z_ref, loff_ref, ptab_ref)[0].astype(BF16)

    @pl.when(b > 0)
    def _():
        _wait_copies((pending[0], pending[1]), make_copy)

    xbuf[...] = jnp.dot(pm, h_ref[...], preferred_element_type=F32)
    n_big, n_small = _segment_copies(b, loff_s, gst_s, p8_s, make_copy)
    pending[0] = n_big
    pending[1] = n_small

    @pl.when(b == N_BLK - 1)
    def _():
        _wait_copies((n_big, n_small), make_copy)
        xbuf[0:TM, :] = jnp.zeros((TM, D), F32)
        _zero_rows_from(xs_hbm, xbuf.at[0:TM], tot_s[0], sem)


def _dispatch(h2, z, loff, ptab, loff_i, gst_i, p8_i, tot_i):
    whole = pl.BlockSpec((TAB_ROWS, E_PAD), lambda b, *_: (0, 0))
    return pl.pallas_call(
        _dispatch_kernel,
        grid_spec=pltpu.PrefetchScalarGridSpec(
            num_scalar_prefetch=4,
            grid=(N_BLK,),
            in_specs=[pl.BlockSpec((TBLK, D), lambda b, *_: (b, 0)),
                      pl.BlockSpec((TBLK, E_PAD), lambda b, *_: (b, 0)),
                      whole, whole],
            out_specs=pl.BlockSpec(memory_space=pl.ANY),
            scratch_shapes=[pltpu.VMEM((RB, D), F32), pltpu.SemaphoreType.DMA(()),
                            pltpu.SMEM((2,), jnp.int32)]),
        out_shape=jax.ShapeDtypeStruct((R_TOT, D), F32),
        compiler_params=_cparams(("arbitrary",)),
        name="moe_dispatch",
    )(loff_i, gst_i, p8_i, tot_i, h2, z, loff, ptab)


TILE_DMA_PRIORITY = 1


def _expert_kernel(eoff_s, xs_hbm, wg_ref, bg_ref, wu_ref, bu_ref, wd_ref, bd_ref, ys_hbm, xin, yout,
                   sem_in, sem_out):
    e = pl.program_id(0)
    start = eoff_s[e]
    n_tiles = lax.shift_right_logical(eoff_s[e + 1] - start + (TM - 1), TM_SHIFT)
    wg = wg_ref[0].astype(BF16)
    wu = wu_ref[0].astype(BF16)
    wd = wd_ref[0].astype(BF16)

    def tile_rows(i):
        return pl.ds(pl.multiple_of(start + i * TM, SEG_ALIGN), TM)

    def in_copy(i, slot):
        return pltpu.make_async_copy(xs_hbm.at[tile_rows(i)], xin.at[slot], sem_in.at[slot])

    def out_copy(i):
        return pltpu.make_async_copy(yout, ys_hbm.at[tile_rows(i)], sem_out)

    @pl.when(n_tiles > 0)
    def _():
        in_copy(0, 0).start(priority=TILE_DMA_PRIORITY)

    def body(i, _):
        slot = lax.rem(i, 2)
        in_copy(i, slot).wait()

        @pl.when(i + 1 < n_tiles)
        def _():
            in_copy(i + 1, 1 - slot).start(priority=TILE_DMA_PRIORITY)

        x = xin[slot].astype(BF16)
        gate = jnp.dot(x, wg, preferred_element_type=F32) + bg_ref[0]
        up = jnp.dot(x, wu, preferred_element_type=F32) + bu_ref[0]
        gate = jnp.minimum(gate, SWIGLU_LIMIT)
        up = jnp.clip(up, -SWIGLU_LIMIT, SWIGLU_LIMIT)
        act = (up + 1.0) * gate * jax.nn.sigmoid(SWIGLU_ALPHA * gate)
        y = jnp.dot(act.astype(BF16), wd, preferred_element_type=F32) + bd_ref[0]

        @pl.when(i > 0)
        def _():
            out_copy(i - 1).wait()

        yout[...] = y
        out_copy(i).start(priority=TILE_DMA_PRIORITY)
        return 0

    lax.fori_loop(0, n_tiles, body, 0)

    @pl.when(n_tiles > 0)
    def _():
        out_copy(n_tiles - 1).wait()

    @pl.when(e == N_EXPERTS - 1)
    def _():
        yout[...] = jnp.zeros((TM, D), F32)
        _zero_rows_from(ys_hbm, yout, eoff_s[N_EXPERTS], sem_out)


def _experts(xs, eoff_i, w_gate, b_gate, w_up, b_up, w_down, b_down):
    wspec = pl.BlockSpec((1, D, D), lambda e, *_: (e, 0, 0))
    bspec = pl.BlockSpec((1, 1, D), lambda e, *_: (e, 0, 0))
    return pl.pallas_call(
        _expert_kernel,
        grid_spec=pltpu.PrefetchScalarGridSpec(
            num_scalar_prefetch=1,
            grid=(N_EXPERTS,),
            in_specs=[pl.BlockSpec(memory_space=pl.ANY), wspec, bspec, wspec, bspec, wspec, bspec],
            out_specs=pl.BlockSpec(memory_space=pl.ANY),
            scratch_shapes=[pltpu.VMEM((2, TM, D), F32), pltpu.VMEM((TM, D), F32),
                            pltpu.SemaphoreType.DMA((2,)), pltpu.SemaphoreType.DMA(())]),
        out_shape=jax.ShapeDtypeStruct((R_TOT, D), F32),
        compiler_params=_cparams(("arbitrary",)),
        name="moe_experts",
    )(eoff_i, xs, w_gate, b_gate.reshape(N_EXPERTS, 1, D), w_up, b_up.reshape(N_EXPERTS, 1, D),
      w_down, b_down.reshape(N_EXPERTS, 1, D))


def _combine_kernel(loff_s, gst_s, p8_s, ys_hbm, z_ref, comb_ref, loff_ref, ptab_ref, x1_ref, mod_ref, gf_ref,
                    o_ref, ybuf, sem):
    b = pl.program_id(0)

    def make_copy(local_row, global_row, rows):
        return pltpu.make_async_copy(ys_hbm.at[pl.ds(global_row, rows)], ybuf.at[pl.ds(local_row, rows)], sem)

    started = _segment_copies(b, loff_s, gst_s, p8_s, make_copy)

    last = b * N_EXPERTS + N_EXPERTS - 1
    used_chunks = lax.shift_right_logical(loff_s[last], 3) + p8_s[last]

    def zero_chunk(j, _):
        ybuf[pl.ds(pl.multiple_of(j * SEG_ALIGN, SEG_ALIGN), SEG_ALIGN), :] = jnp.zeros((SEG_ALIGN, D), F32)
        return 0

    lax.fori_loop(used_chunks, RB // SEG_ALIGN, zero_chunk, 0)

    pm, owner = _sort_matrix(b, z_ref, loff_ref, ptab_ref)
    w_all = jnp.dot(owner, comb_ref[...].T, preferred_element_type=F32, precision=HIGHEST)
    w_row = jnp.sum(pm * w_all, axis=-1, keepdims=True)
    pmt = pm.T.astype(BF16)
    _wait_copies(started, make_copy)
    y = (ybuf[...] * w_row).astype(BF16)
    moe = jnp.dot(pmt, y, preferred_element_type=F32)
    gate2 = mod_ref[0][:, 5 * D:6 * D]
    x2 = x1_ref[0] + gate2 * moe
    o_ref[0] = _rms(x2, gf_ref[...])


def _combine(ys, z, comb, loff, ptab, x1, modv, norm_f_g, loff_i, gst_i, p8_i):
    whole = pl.BlockSpec((TAB_ROWS, E_PAD), lambda b, *_: (0, 0))
    tok = pl.BlockSpec((TBLK, E_PAD), lambda b, *_: (b, 0))
    return pl.pallas_call(
        _combine_kernel,
        grid_spec=pltpu.PrefetchScalarGridSpec(
            num_scalar_prefetch=3,
            grid=(N_BLK,),
            in_specs=[pl.BlockSpec(memory_space=pl.ANY), tok, tok, whole, whole,
                      pl.BlockSpec((1, TBLK, D), lambda b, *_: (b, 0, 0)),
                      pl.BlockSpec((1, 1, 6 * D), lambda b, *_: (b, 0, 0)),
                      pl.BlockSpec((1, D), lambda b, *_: (0, 0))],
            out_specs=pl.BlockSpec((1, TBLK, D), lambda b, *_: (b, 0, 0)),
            scratch_shapes=[pltpu.VMEM((RB, D), F32), pltpu.SemaphoreType.DMA(())]),
        out_shape=jax.ShapeDtypeStruct((N_BLK, TBLK, D), F32),
        compiler_params=_cparams(("arbitrary",)),
        name="moe_combine",
    )(loff_i, gst_i, p8_i, ys, z, comb, loff, ptab, x1, modv, norm_f_g.reshape(1, D))


def _moe_and_final(x1, h2, comb, modg, norm_f_g, w_gate, b_gate, w_up, b_up, w_down, b_down):
    z, ptab, loff, gst, eoff = _plan(comb)
    as_scalars = lambda t: t[:N_BLK, :N_EXPERTS].astype(jnp.int32).reshape(N_BLK * N_EXPERTS)
    loff_i, gst_i = as_scalars(loff), as_scalars(gst)
    p8_i = as_scalars(ptab) // SEG_ALIGN
    eoff_i = eoff[0, :N_EXPERTS + 1].astype(jnp.int32)
    xs = _dispatch(h2, z, loff, ptab, loff_i, gst_i, p8_i, eoff_i[N_EXPERTS:])
    ys = _experts(xs, eoff_i, w_gate, b_gate, w_up, b_up, w_down, b_down)
    return _combine(ys, z, comb, loff, ptab, x1, modg.reshape(N_BLK, 1, 6 * D), norm_f_g,
                    loff_i, gst_i, p8_i)


def _grid_pos_embed(n_tokens, dim):
    rows = n_tokens // GRID_W
    t = jnp.arange(rows * GRID_W)
    r = (t // GRID_W).astype(F32)
    col = (t % GRID_W).astype(F32)
    q = dim // 4
    omega = 1.0 / POS_TEMP ** (jnp.arange(q, dtype=F32) / q)

    def emb(p):
        a = p[:, None] * omega[None, :]
        return jnp.concatenate([jnp.sin(a), jnp.cos(a)], axis=-1)

    return jnp.concatenate([emb(r), emb(col)], axis=-1)


def _block_diag_chunks(m):
    _, _, a, b = m.shape
    m = m.reshape(2, N_S5_CHUNKS, S5_CHUNK, a, b)
    eye = jnp.eye(S5_CHUNK, dtype=m.dtype)
    bd = m[:, :, :, :, None, :] * eye[None, None, :, None, :, None]
    return bd.reshape(2, N_S5_CHUNKS, S5_CHUNK * a, S5_CHUNK * b)


def kernel(x_prompt, x_sample, c, state_s5_re, state_s5_im, c_ctx, w_ada, b_ada, norm1_g, w_in, s5_lam_re,
           s5_lam_im, s5_log_dt, s5_b_re, s5_b_im, s5_c_re, s5_c_im, s5_d, s5_w_glu, w_fnet, w_out, norm2_g,
           w_router, b_router, w_gate, b_gate, w_up, b_up, w_down, b_down, norm_f_g):
    n_ctx, n_lat = x_prompt.shape[0], x_sample.shape[0]
    assert x_prompt.shape == (SUB * (N_GRP - 1), L_BLK, D) and x_sample.shape == (2, LAT_CHUNKS * L_BLK, D)
    assert w_ada.shape[0] == 1, "one trunk layer"
    layer = 0

    cvec = jnp.zeros((SUB, D), F32).at[0].set(c_ctx).at[1:1 + n_lat].set(c)
    mod = _adaln(cvec, w_ada[layer], b_ada[layer])
    lat_rows = jnp.repeat(mod[1:1 + n_lat], LAT_CHUNKS, axis=0)
    ctx_rows = jnp.broadcast_to(mod[0:1], (SUB, 6 * D))
    modg = jnp.stack([ctx_rows] * (N_GRP - 1) + [lat_rows])[:, :, None, :]

    xp4 = x_prompt.reshape(N_GRP - 1, SUB, L_BLK, D)
    xs4 = x_sample.reshape(1, SUB, L_BLK, D)
    pos3 = _grid_pos_embed(LAT_CHUNKS * L_BLK, D).reshape(LAT_CHUNKS, L_BLK, D)
    perm = _perm_matrix()

    us, uf = _pre(xp4, xs4, pos3, modg, norm1_g[layer], w_in[layer], jnp.asarray(perm, BF16))

    a_re, a_im, bb_re, bb_im = _discretise(s5_lam_re[layer], s5_lam_im[layer], s5_log_dt[layer],
                                           s5_b_re[layer], s5_b_im[layer])
    a_cat = jnp.concatenate([a_re.reshape(2, N_S5_CHUNKS, CH_P), a_im.reshape(2, N_S5_CHUNKS, CH_P)], axis=-1)
    a8 = jnp.broadcast_to(a_cat[:, :, None, :], (2, N_S5_CHUNKS, SUB, 2 * CH_P))
    bb_re = bb_re.reshape(2, N_S5_GROUPS, S5_H, S5_P)
    bb_im = bb_im.reshape(2, N_S5_GROUPS, S5_H, S5_P)
    bbmat = jnp.concatenate([_block_diag_chunks(bb_re), _block_diag_chunks(bb_im)], axis=-1).astype(BF16)
    c_re_t = jnp.swapaxes(s5_c_re[layer], -1, -2)
    c_im_t = jnp.swapaxes(s5_c_im[layer], -1, -2)
    ccmat = jnp.concatenate([_block_diag_chunks(c_re_t), _block_diag_chunks(-c_im_t)], axis=-2).astype(BF16)

    st = jnp.stack([state_s5_re[:, layer], state_s5_im[:, layer]], axis=2)
    st = st.reshape(n_lat, 2, 2, N_S5_CHUNKS, CH_P)
    st = jnp.transpose(st, (1, 3, 0, 2, 4)).reshape(2, N_S5_CHUNKS, n_lat, 2 * CH_P)
    init8 = jnp.zeros((2, N_S5_CHUNKS, n_lat, LAT_CHUNKS, 2 * CH_P), F32)
    init8 = init8.at[0, :, :, 0].set(st[0]).at[1, :, :, LAT_CHUNKS - 1].set(st[1])
    init8 = init8.reshape(2, N_S5_CHUNKS, SUB, 2 * CH_P)

    y_s5, fin = _s5(us, bbmat, ccmat, a8, init8, s5_d[layer])

    m1, m2 = _fnet_weights(w_fnet[layer])
    uf_ctx = uf[:N_GRP - 1].reshape(n_ctx, L_BLK, D_FN)
    uf_lat = uf[N_GRP - 1].reshape(n_lat, LAT_CHUNKS * L_BLK, D_FN)
    yf = jnp.concatenate([_fnet(uf_ctx, m1, m2).reshape(N_GRP - 1, SUB, L_BLK, D_FN),
                          _fnet(uf_lat, m1, m2).reshape(1, SUB, L_BLK, D_FN)], axis=0)

    x1, h2, comb = _post(y_s5, yf, xp4, xs4, pos3, modg, s5_w_glu[layer], w_out[layer], norm2_g[layer],
                         w_router[layer], b_router[layer], jnp.asarray(perm.T, BF16))

    y_all = _moe_and_final(x1.reshape(N_BLK, TBLK, D), h2.reshape(T_TOK, D), comb.reshape(T_TOK, E_PAD), modg,
                           norm_f_g, w_gate[layer], b_gate[layer], w_up[layer], b_up[layer], w_down[layer],
                           b_down[layer])

    y_prompt = y_all[:n_ctx]
    y_sample = y_all[n_ctx:].reshape(n_lat, LAT_CHUNKS * L_BLK, D)

    fin = fin[:, :N_GRP - 1].reshape(2, N_GRP - 1, SUB, N_S5_CHUNKS, 2, CH_P)
    fin = jnp.transpose(fin, (4, 1, 2, 0, 3, 5)).reshape(2, n_ctx, 1, 2, N_S5_GROUPS, S5_P)
    return (y_prompt, y_sample, fin[0], fin[1])
```

```python
import functools
import math

import numpy as np
import jax
import jax.numpy as jnp
from jax import lax
from jax.experimental import pallas as pl
from jax.experimental.pallas import tpu as pltpu

F32 = jnp.float32
BF16 = jnp.bfloat16
HIGHEST = lax.Precision.HIGHEST

D = 1024
D_S5 = 768
S5_H = 16
S5_P = 64
N_S5_GROUPS = 48
D_FN = 256
FN_GW = 64
N_EXPERTS = 32
TOP_K = 4
E_PAD = 128
SWIGLU_LIMIT = 7.0
SWIGLU_ALPHA = 1.702
RMS_EPS = 1e-6
POS_TEMP = 10000.0
GRID_W = 64

L_BLK = 256
SUB = 8
N_GRP = 3
GRP_ROWS = L_BLK * SUB
LAT_CHUNKS = 4
T_TOK = N_GRP * GRP_ROWS
TC = 128
N_TC = L_BLK // TC
PERM_T = 32
S5_CHUNK = 16
N_S5_CHUNKS = N_S5_GROUPS // S5_CHUNK
CH_U = S5_CHUNK * S5_H
CH_P = S5_CHUNK * S5_P
ROW_BLK = 256
VMEM_LIMIT = 56 * 1024 * 1024


def _cparams(sem):
    return pltpu.CompilerParams(dimension_semantics=sem, vmem_limit_bytes=VMEM_LIMIT)


def _adaln_kernel(c_ref, w_ref, b_ref, o_ref):
    c = c_ref[...]
    s = c * jax.nn.sigmoid(c)
    o_ref[...] = jnp.dot(s, w_ref[...], preferred_element_type=F32, precision=HIGHEST) + b_ref[...]


def _adaln(cvec, w_ada, b_ada):
    n = w_ada.shape[1]
    return pl.pallas_call(
        _adaln_kernel,
        grid=(n // D,),
        in_specs=[pl.BlockSpec((SUB, D), lambda j: (0, 0)),
                  pl.BlockSpec((D, D), lambda j: (0, j)),
                  pl.BlockSpec((1, D), lambda j: (0, j))],
        out_specs=pl.BlockSpec((SUB, D), lambda j: (0, j)),
        out_shape=jax.ShapeDtypeStruct((SUB, n), F32),
        compiler_params=_cparams(("arbitrary",)),
        name="adaln",
    )(cvec, w_ada, b_ada.reshape(1, n))


def _disc_kernel(lre_ref, lim_ref, ldt_ref, bre_ref, bim_ref, are_ref, aim_ref, bbre_ref, bbim_ref):
    dt = jnp.exp(ldt_ref[...])
    a_re = jnp.minimum(lre_ref[...], -1e-4)
    a_im = lim_ref[...]
    mag = jnp.exp(a_re * dt)
    ab_re = mag * jnp.cos(a_im * dt)
    ab_im = mag * jnp.sin(a_im * dt)
    den = a_re * a_re + a_im * a_im
    nr = ab_re - 1.0
    f_re = (nr * a_re + ab_im * a_im) / den
    f_im = (ab_im * a_re - nr * a_im) / den
    b_re = bre_ref[...]
    b_im = bim_ref[...]
    are_ref[...] = ab_re
    aim_ref[...] = ab_im
    bbre_ref[...] = f_re * b_re - f_im * b_im
    bbim_ref[...] = f_re * b_im + f_im * b_re


def _discretise(lam_re, lam_im, log_dt, b_re, b_im):
    n = 2 * N_S5_GROUPS
    lre = lam_re.reshape(n, 1, S5_P)
    lim = lam_im.reshape(n, 1, S5_P)
    ldt = jnp.broadcast_to(log_dt.reshape(n, 1, 1), (n, 1, S5_P))
    bre = jnp.swapaxes(b_re, -1, -2).reshape(n, S5_H, S5_P)
    bim = jnp.swapaxes(b_im, -1, -2).reshape(n, S5_H, S5_P)
    small = jax.ShapeDtypeStruct((n, 1, S5_P), F32)
    big = jax.ShapeDtypeStruct((n, S5_H, S5_P), F32)
    return pl.pallas_call(_disc_kernel, out_shape=(small, small, big, big), name="s5_disc")(
        lre, lim, ldt, bre, bim)


def _rms(x, g):
    return x * lax.rsqrt(jnp.mean(x * x, axis=-1, keepdims=True) + RMS_EPS) * g


def _load_x(g, xp_ref, xs_ref, pos_ref):
    is_lat = g == N_GRP - 1
    pos = pos_ref[...]
    pos8 = jnp.concatenate([pos, pos], axis=0)
    return jnp.where(is_lat, xs_ref[0] + pos8, xp_ref[0])


def _x_specs():
    n_ctx = N_GRP - 1
    xp_spec = pl.BlockSpec(
        (1, SUB, TC, D),
        lambda g, tc: (jnp.minimum(g, n_ctx - 1), 0, jnp.where(g >= n_ctx, N_TC - 1, tc), 0))
    xs_spec = pl.BlockSpec((1, SUB, TC, D), lambda g, tc: (0, 0, jnp.where(g >= n_ctx, tc, 0), 0))
    pos_spec = pl.BlockSpec((LAT_CHUNKS, TC, D), lambda g, tc: (0, tc, 0))
    return xp_spec, xs_spec, pos_spec


def _perm_matrix():
    p = np.zeros((SUB * PERM_T, SUB * PERM_T), np.float32)
    for s in range(SUB):
        for j in range(PERM_T):
            p[j * SUB + s, s * PERM_T + j] = 1.0
    return p


def _pre_kernel(xp_ref, xs_ref, pos_ref, mod_ref, g1_ref, win_ref, perm_ref, us_ref, uf_ref):
    g = pl.program_id(0)
    x = _load_x(g, xp_ref, xs_ref, pos_ref)
    mod = mod_ref[0]
    shift1 = mod[:, :, 0:D]
    scale1 = mod[:, :, D:2 * D]
    h = _rms(x, g1_ref[...]) * (1.0 + scale1) + shift1
    h2d = h.reshape(SUB * TC, D).astype(BF16)
    proj = jnp.dot(h2d, win_ref[...].astype(BF16), preferred_element_type=F32)
    uf_ref[0] = proj[:, D_S5:].astype(BF16).reshape(SUB, TC, D_FN)
    u = proj[:, :D_S5].astype(BF16)
    perm = perm_ref[...]
    for q in range(TC // PERM_T):
        piece = jnp.concatenate(
            [u[s * TC + q * PERM_T: s * TC + (q + 1) * PERM_T] for s in range(SUB)], axis=0)
        us_ref[0, q * SUB * PERM_T:(q + 1) * SUB * PERM_T, :] = jnp.dot(
            perm, piece, preferred_element_type=F32).astype(BF16)


def _pre(xp4, xs4, pos3, modg, norm1_g, w_in, perm):
    xp_spec, xs_spec, pos_spec = _x_specs()
    return pl.pallas_call(
        _pre_kernel,
        grid=(N_GRP, N_TC),
        in_specs=[xp_spec, xs_spec, pos_spec,
                  pl.BlockSpec((1, SUB, 1, 6 * D), lambda g, tc: (g, 0, 0, 0)),
                  pl.BlockSpec((1, D), lambda g, tc: (0, 0)),
                  pl.BlockSpec((D, D), lambda g, tc: (0, 0)),
                  pl.BlockSpec((SUB * PERM_T, SUB * PERM_T), lambda g, tc: (0, 0))],
        out_specs=[pl.BlockSpec((1, SUB * TC, D_S5), lambda g, tc: (g, tc, 0)),
                   pl.BlockSpec((1, SUB, TC, D_FN), lambda g, tc: (g, 0, tc, 0))],
        out_shape=[jax.ShapeDtypeStruct((N_GRP, GRP_ROWS, D_S5), BF16),
                   jax.ShapeDtypeStruct((N_GRP, SUB, L_BLK, D_FN), BF16)],
        compiler_params=_cparams(("arbitrary", "arbitrary")),
        name="pre_mixer",
    )(xp4, xs4, pos3, modg, norm1_g.reshape(1, D), w_in, perm)


def _cmul(ar, ai, br, bi):
    return ar * br - ai * bi, ar * bi + ai * br


def _s5_kernel(us_ref, bb_ref, cc_ref, a_ref, init_ref, dskip_ref, y_ref, fin_ref, s_ref):
    g = pl.program_id(0)
    d = pl.program_id(2)
    n_rb = GRP_ROWS // ROW_BLK
    half = CH_P // 2

    def proj_in(rb, _):
        rows = pl.ds(pl.multiple_of(rb * ROW_BLK, ROW_BLK), ROW_BLK)
        s_ref[rows, :] = jnp.dot(us_ref[0, rows, :], bb_ref[0, 0], preferred_element_type=F32)
        return 0

    lax.fori_loop(0, n_rb, proj_in, 0)

    def tile_row(k):
        t = k + d * (L_BLK - 1 - 2 * k)
        return pl.ds(pl.multiple_of(t * SUB, SUB), SUB)

    is_lat = g == N_GRP - 1
    sub_id = lax.broadcasted_iota(jnp.int32, (SUB, half), 0) % LAT_CHUNKS

    for hh in range(2):
        re_cols = slice(hh * half, (hh + 1) * half)
        im_cols = slice(CH_P + hh * half, CH_P + (hh + 1) * half)
        a_re = a_ref[0, 0, :, re_cols]
        a_im = a_ref[0, 0, :, im_cols]
        zero = jnp.zeros((SUB, half), F32)
        s0_re = jnp.where(is_lat, init_ref[0, 0, :, re_cols], zero)
        s0_im = jnp.where(is_lat, init_ref[0, 0, :, im_cols], zero)

        def scan_body(i, carry, a_re=a_re, a_im=a_im, re_cols=re_cols, im_cols=im_cols):
            s_re, s_im = carry
            for j in range(8):
                rows = tile_row(i * 8 + j)
                n_re = a_re * s_re - a_im * s_im + s_ref[rows, re_cols]
                n_im = a_re * s_im + a_im * s_re + s_ref[rows, im_cols]
                s_ref[rows, re_cols] = n_re
                s_ref[rows, im_cols] = n_im
                s_re, s_im = n_re, n_im
            return s_re, s_im

        f_re, f_im = lax.fori_loop(0, L_BLK // 8, scan_body, (s0_re, s0_im))
        fin_ref[0, 0, :, re_cols] = f_re
        fin_ref[0, 0, :, im_cols] = f_im

        @pl.when(is_lat)
        def _(a_re=a_re, a_im=a_im, f_re=f_re, f_im=f_im, re_cols=re_cols, im_cols=im_cols):
            p_re, p_im = a_re, a_im
            for _ in range(8):
                p_re, p_im = _cmul(p_re, p_im, p_re, p_im)
            fwd = d == 0
            lo = jnp.where(fwd, 1, 0)
            hi = jnp.where(fwd, LAT_CHUNKS - 1, LAT_CHUNKS - 2)
            keep = (sub_id >= lo) & (sub_id <= hi)

            def from_prev(v):
                return jnp.where(keep, jnp.where(fwd, pltpu.roll(v, 1, 0), pltpu.roll(v, SUB - 1, 0)), 0.0)

            t_re, t_im = f_re, f_im
            for _ in range(LAT_CHUNKS - 2):
                m_re, m_im = _cmul(p_re, p_im, from_prev(t_re), from_prev(t_im))
                t_re, t_im = f_re + m_re, f_im + m_im
            c_re, c_im = from_prev(t_re), from_prev(t_im)

            def fix_body(i, carry):
                q_re, q_im = carry
                for j in range(8):
                    rows = tile_row(i * 8 + j)
                    m_re, m_im = _cmul(q_re, q_im, c_re, c_im)
                    s_ref[rows, re_cols] = s_ref[rows, re_cols] + m_re
                    s_ref[rows, im_cols] = s_ref[rows, im_cols] + m_im
                    q_re, q_im = _cmul(q_re, q_im, a_re, a_im)
                return q_re, q_im

            lax.fori_loop(0, L_BLK // 8, fix_body, (a_re, a_im))

    def proj_out(rb):
        rows = pl.ds(pl.multiple_of(rb * ROW_BLK, ROW_BLK), ROW_BLK)
        return rows, jnp.dot(s_ref[rows, :].astype(BF16), cc_ref[0, 0], preferred_element_type=F32)

    @pl.when(d == 0)
    def _():
        def body(rb, _):
            rows, yb = proj_out(rb)
            y_ref[0, rows, :] = dskip_ref[...] * us_ref[0, rows, :].astype(F32) + yb
            return 0
        lax.fori_loop(0, n_rb, body, 0)

    @pl.when(d == 1)
    def _():
        def body(rb, _):
            rows, yb = proj_out(rb)
            y_ref[0, rows, :] = y_ref[0, rows, :] + yb
            return 0
        lax.fori_loop(0, n_rb, body, 0)


def _s5(us, bbmat, ccmat, a8, init8, d_skip):
    return pl.pallas_call(
        _s5_kernel,
        grid=(N_GRP, N_S5_CHUNKS, 2),
        in_specs=[pl.BlockSpec((1, GRP_ROWS, CH_U), lambda g, c, d: (g, 0, c)),
                  pl.BlockSpec((1, 1, CH_U, 2 * CH_P), lambda g, c, d: (d, c, 0, 0)),
                  pl.BlockSpec((1, 1, 2 * CH_P, CH_U), lambda g, c, d: (d, c, 0, 0)),
                  pl.BlockSpec((1, 1, SUB, 2 * CH_P), lambda g, c, d: (d, c, 0, 0)),
                  pl.BlockSpec((1, 1, SUB, 2 * CH_P), lambda g, c, d: (d, c, 0, 0)),
                  pl.BlockSpec((1, CH_U), lambda g, c, d: (0, c))],
        out_specs=[pl.BlockSpec((1, GRP_ROWS, CH_U), lambda g, c, d: (g, 0, c)),
                   pl.BlockSpec((1, 1, SUB, 2 * CH_P), lambda g, c, d: (d, g, 0, c))],
        out_shape=[jax.ShapeDtypeStruct((N_GRP, GRP_ROWS, D_S5), F32),
                   jax.ShapeDtypeStruct((2, N_GRP, SUB, N_S5_CHUNKS * 2 * CH_P), F32)],
        scratch_shapes=[pltpu.VMEM((GRP_ROWS, 2 * CH_P), F32)],
        compiler_params=_cparams(("arbitrary", "arbitrary", "arbitrary")),
        name="s5_scan",
    )(us, bbmat, ccmat, a8, init8, d_skip.reshape(1, D_S5))


def _fnet_w_kernel(c_ref, s_ref, w_ref, m1_ref, m2_ref):
    w = w_ref[...]
    m1_ref[...] = jnp.dot(c_ref[...], w, preferred_element_type=F32, precision=HIGHEST).astype(BF16)
    m2_ref[...] = jnp.dot(s_ref[...], w, preferred_element_type=F32, precision=HIGHEST).astype(BF16)


def _fnet_kernel(u_ref, cos_ref, sin_ref, m1_ref, m2_ref, o_ref):
    u = u_ref[0]
    v1 = jnp.dot(u, m1_ref[...], preferred_element_type=F32).astype(BF16)
    v2 = jnp.dot(u, m2_ref[...], preferred_element_type=F32).astype(BF16)
    o_ref[0] = (jnp.dot(cos_ref[...].astype(BF16), v1, preferred_element_type=F32)
                - jnp.dot(sin_ref[...].astype(BF16), v2, preferred_element_type=F32)).astype(BF16)


def _dft_tables(n):
    k = np.arange(n, dtype=np.int64)
    ang = (2.0 * np.pi / n) * ((k[:, None] * k[None, :]) % n).astype(np.float64)
    scale = 1.0 / math.sqrt(n)
    return (np.cos(ang) * scale).astype(np.float32), (np.sin(ang) * scale).astype(np.float32)


def _fnet(u, m1, m2):
    n_seq, length, _ = u.shape
    cos_l, sin_l = _dft_tables(length)
    full = lambda i: (0, 0)
    return pl.pallas_call(
        _fnet_kernel,
        grid=(n_seq,),
        in_specs=[pl.BlockSpec((1, length, D_FN), lambda i: (i, 0, 0)),
                  pl.BlockSpec((length, length), full),
                  pl.BlockSpec((length, length), full),
                  pl.BlockSpec((D_FN, D_FN), full),
                  pl.BlockSpec((D_FN, D_FN), full)],
        out_specs=pl.BlockSpec((1, length, D_FN), lambda i: (i, 0, 0)),
        out_shape=jax.ShapeDtypeStruct(u.shape, BF16),
        compiler_params=_cparams(("arbitrary",)),
        name=f"fnet_{length}",
    )(u, jnp.asarray(cos_l), jnp.asarray(sin_l), m1, m2)


def _fnet_weights(w_fnet):
    n_g = D_FN // FN_GW
    cos_c, sin_c = _dft_tables(FN_GW)
    eye = np.eye(n_g, dtype=np.float32)
    cos_bd = np.kron(eye, cos_c)
    sin_bd = np.kron(eye, sin_c)
    w_bd = (w_fnet[:, :, None, :] * jnp.asarray(eye)[:, None, :, None]).reshape(D_FN, D_FN)
    out = jax.ShapeDtypeStruct((D_FN, D_FN), BF16)
    return pl.pallas_call(_fnet_w_kernel, out_shape=(out, out), name="fnet_weights")(
        jnp.asarray(cos_bd), jnp.asarray(sin_bd), w_bd)


def _gelu_tanh(x):
    return 0.5 * x * (1.0 + jnp.tanh(math.sqrt(2.0 / math.pi) * (x + 0.044715 * (x * x * x))))


def _post_kernel(y_ref, yf_ref, xp_ref, xs_ref, pos_ref, mod_ref, wglu_ref, wout_ref, g2_ref, wr_ref,
                 br_ref, permt_ref, x1_ref, h2_ref, comb_ref):
    g = pl.program_id(0)
    z = _gelu_tanh(y_ref[0])
    gate = jnp.dot(z.astype(BF16), wglu_ref[...].astype(BF16), preferred_element_type=F32)
    gl = (z * jax.nn.sigmoid(gate)).astype(BF16)
    permt = permt_ref[...]
    n_q = TC // PERM_T
    nat = [jnp.dot(permt, gl[q * SUB * PERM_T:(q + 1) * SUB * PERM_T], preferred_element_type=F32).astype(BF16)
           for q in range(n_q)]
    gl_nat = jnp.concatenate(
        [nat[q][s * PERM_T:(s + 1) * PERM_T] for s in range(SUB) for q in range(n_q)], axis=0)
    w_out = wout_ref[...].astype(BF16)
    mixed = (jnp.dot(gl_nat, w_out[:D_S5], preferred_element_type=F32)
             + jnp.dot(yf_ref[0].reshape(SUB * TC, D_FN), w_out[D_S5:], preferred_element_type=F32))
    x = _load_x(g, xp_ref, xs_ref, pos_ref)
    mod = mod_ref[0]
    gate1 = mod[:, :, 2 * D:3 * D]
    shift2 = mod[:, :, 3 * D:4 * D]
    scale2 = mod[:, :, 4 * D:5 * D]
    x1 = x + gate1 * mixed.reshape(SUB, TC, D)
    x1_ref[0] = x1
    h2 = _rms(x1, g2_ref[...]) * (1.0 + scale2) + shift2
    h2_ref[0] = h2.astype(BF16)

    logits = jnp.dot(h2.reshape(SUB * TC, D), wr_ref[...], preferred_element_type=F32,
                     precision=HIGHEST) + br_ref[...]
    lane = lax.broadcasted_iota(jnp.int32, logits.shape, 1)
    top_v, hots = [], []
    cur = logits
    for _ in range(TOP_K):
        m = jnp.max(cur, axis=-1, keepdims=True)
        idx = jnp.min(jnp.where(cur == m, lane, E_PAD), axis=-1, keepdims=True)
        hot = lane == idx
        top_v.append(m)
        hots.append(hot)
        cur = jnp.where(hot, -3.0e38, cur)
    exps = [jnp.exp(v - top_v[0]) for v in top_v]
    denom = exps[0] + exps[1] + exps[2] + exps[3]
    comb = jnp.zeros(logits.shape, F32)
    for k in range(TOP_K):
        comb = comb + jnp.where(hots[k], exps[k] / denom, 0.0)
    comb_ref[0] = comb.reshape(SUB, TC, E_PAD)


def _post(y, yf, xp4, xs4, pos3, modg, w_glu, w_out, norm2_g, w_router, b_router, permt):
    xp_spec, xs_spec, pos_spec = _x_specs()
    const2 = lambda g, tc: (0, 0)
    wr = jnp.zeros((D, E_PAD), F32).at[:, :N_EXPERTS].set(w_router)
    br = jnp.full((1, E_PAD), -1.0e30, F32).at[0, :N_EXPERTS].set(b_router)
    blk = lambda w: pl.BlockSpec((1, SUB, TC, w), lambda g, tc: (g, 0, tc, 0))
    return pl.pallas_call(
        _post_kernel,
        grid=(N_GRP, N_TC),
        in_specs=[pl.BlockSpec((1, SUB * TC, D_S5), lambda g, tc: (g, tc, 0)),
                  blk(D_FN), xp_spec, xs_spec, pos_spec,
                  pl.BlockSpec((1, SUB, 1, 6 * D), lambda g, tc: (g, 0, 0, 0)),
                  pl.BlockSpec((D_S5, D_S5), const2),
                  pl.BlockSpec((D, D), const2),
                  pl.BlockSpec((1, D), const2),
                  pl.BlockSpec((D, E_PAD), const2),
                  pl.BlockSpec((1, E_PAD), const2),
                  pl.BlockSpec((SUB * PERM_T, SUB * PERM_T), const2)],
        out_specs=[blk(D), blk(D), blk(E_PAD)],
        out_shape=[jax.ShapeDtypeStruct((N_GRP, SUB, L_BLK, D), F32),
                   jax.ShapeDtypeStruct((N_GRP, SUB, L_BLK, D), BF16),
                   jax.ShapeDtypeStruct((N_GRP, SUB, L_BLK, E_PAD), F32)],
        compiler_params=_cparams(("arbitrary", "arbitrary")),
        name="post_mixer",
    )(y, yf, xp4, xs4, pos3, modg, w_glu, w_out, norm2_g.reshape(1, D), wr, br, permt)


TBLK = L_BLK
N_BLK = T_TOK // TBLK
SEG_ALIGN = 8
RB = 1280
assert RB >= TBLK * TOP_K + N_EXPERTS * (SEG_ALIGN - 1) and RB % 128 == 0
TM = 256
TM_SHIFT = 8
assert 1 << TM_SHIFT == TM
R_TOT = T_TOK * TOP_K + N_BLK * N_EXPERTS * (SEG_ALIGN - 1) + TM
TAB_ROWS = 32
assert TAB_ROWS >= N_BLK
BIG_ROWS = 32
BIG_SHIFT = 2
assert SEG_ALIGN << BIG_SHIFT == BIG_ROWS


def _plan_kernel(comb_ref, z_ref, ptab_ref, loff_ref, gst_ref, eoff_ref):
    row = lax.broadcasted_iota(jnp.int32, (TBLK, TBLK), 0)
    col = lax.broadcasted_iota(jnp.int32, (TBLK, TBLK), 1)
    earlier = jnp.where(row > col, 1.0, 0.0).astype(BF16)
    ptab_ref[...] = jnp.zeros_like(ptab_ref)

    def body(b, _):
        rows = pl.ds(pl.multiple_of(b * TBLK, TBLK), TBLK)
        m = jnp.where(comb_ref[rows, :] > 0.0, 1.0, 0.0)
        rank = jnp.dot(earlier, m.astype(BF16), preferred_element_type=F32)
        z_ref[rows, :] = m * (rank + 1.0)
        n = jnp.sum(m, axis=0, keepdims=True)
        ptab_ref[pl.ds(b, 1), :] = jnp.floor((n + (SEG_ALIGN - 1)) * (1.0 / SEG_ALIGN)) * SEG_ALIGN
        return 0

    lax.fori_loop(0, N_BLK, body, 0)

    ptab = ptab_ref[...]
    er = lax.broadcasted_iota(jnp.int32, (E_PAD, E_PAD), 0)
    ec = lax.broadcasted_iota(jnp.int32, (E_PAD, E_PAD), 1)
    before = jnp.where(er < ec, 1.0, 0.0)
    exact = functools.partial(jnp.dot, preferred_element_type=F32, precision=HIGHEST)
    loff_ref[...] = exact(ptab, before)
    tot = jnp.sum(ptab, axis=0, keepdims=True)
    eoff = exact(jnp.broadcast_to(tot, (SUB, E_PAD)), before)
    eoff_ref[...] = eoff
    br = lax.broadcasted_iota(jnp.int32, (TAB_ROWS, TAB_ROWS), 0)
    bc = lax.broadcasted_iota(jnp.int32, (TAB_ROWS, TAB_ROWS), 1)
    gst_ref[...] = eoff[0:1] + exact(jnp.where(br > bc, 1.0, 0.0), ptab)


def _plan(comb):
    tab = jax.ShapeDtypeStruct((TAB_ROWS, E_PAD), F32)
    return pl.pallas_call(
        _plan_kernel,
        out_shape=(jax.ShapeDtypeStruct((T_TOK, E_PAD), F32), tab, tab, tab,
                   jax.ShapeDtypeStruct((SUB, E_PAD), F32)),
        compiler_params=pltpu.CompilerParams(vmem_limit_bytes=VMEM_LIMIT),
        name="moe_plan",
    )(comb)


def _sort_matrix(b, z_ref, loff_ref, ptab_ref):
    loff = loff_ref[pl.ds(b, 1), :]
    size = ptab_ref[pl.ds(b, 1), :]
    r = lax.broadcasted_iota(jnp.int32, (RB, E_PAD), 0).astype(F32)
    owner = jnp.where(r >= loff, jnp.where(r < loff + size, 1.0, 0.0), 0.0)
    rank1 = r[:, 0:1] - jnp.sum(owner * loff, axis=-1, keepdims=True) + 1.0
    zt = z_ref[...].T
    v = jnp.dot(owner.astype(BF16), zt.astype(BF16), preferred_element_type=F32)
    return jnp.where(v == rank1, 1.0, 0.0), owner


def _segment_copies(b, loff_s, gst_s, p8_s, make_copy):
    def per_expert(e, counts):
        k = b * N_EXPERTS + e
        chunks = p8_s[k]
        n_big = lax.shift_right_logical(chunks, BIG_SHIFT)
        n_small = chunks - n_big * (BIG_ROWS // SEG_ALIGN)
        local0 = loff_s[k]
        global0 = gst_s[k]

        def big(j, _):
            make_copy(pl.multiple_of(local0 + j * BIG_ROWS, SEG_ALIGN),
                      pl.multiple_of(global0 + j * BIG_ROWS, SEG_ALIGN), BIG_ROWS).start()
            return 0

        def small(j, _):
            off = n_big * BIG_ROWS + j * SEG_ALIGN
            make_copy(pl.multiple_of(local0 + off, SEG_ALIGN),
                      pl.multiple_of(global0 + off, SEG_ALIGN), SEG_ALIGN).start()
            return 0

        lax.fori_loop(0, n_big, big, 0)
        lax.fori_loop(0, n_small, small, 0)
        return counts[0] + n_big, counts[1] + n_small

    return lax.fori_loop(0, N_EXPERTS, per_expert, (0, 0))


def _wait_copies(counts, make_copy):
    for n, rows in zip(counts, (BIG_ROWS, SEG_ALIGN)):
        def wait_one(i, _, rows=rows):
            make_copy(0, 0, rows).wait()
            return 0
        lax.fori_loop(0, n, wait_one, 0)


def _zero_rows_from(hbm, zeros_vmem, first_row, sem):
    def copy_to(row):
        return pltpu.make_async_copy(zeros_vmem, hbm.at[pl.ds(row, TM)], sem)

    n_full = lax.shift_right_logical(R_TOT - first_row, TM_SHIFT)

    def start_one(j, _):
        copy_to(pl.multiple_of(first_row + j * TM, SEG_ALIGN)).start()
        return 0

    def wait_one(j, _):
        copy_to(0).wait()
        return 0

    lax.fori_loop(0, n_full, start_one, 0)
    lax.fori_loop(0, n_full, wait_one, 0)
    last = copy_to(R_TOT - TM)
    last.start()
    last.wait()


def _dispatch_kernel(loff_s, gst_s, p8_s, tot_s, h_ref, z_ref, loff_ref, ptab_ref, xs_hbm, xbuf, sem, pending):
    b = pl.program_id(0)

    def make_copy(local_row, global_row, rows):
        return pltpu.make_async_copy(xbuf.at[pl.ds(local_row, rows)], xs_hbm.at[pl.ds(global_row, rows)], sem)

    pm = _sort_matrix(b, z_ref, loff_ref, ptab_ref)[0].astype(BF16)

    @pl.when(b > 0)
    def _():
        _wait_copies((pending[0], pending[1]), make_copy)

    xbuf[...] = jnp.dot(pm, h_ref[...], preferred_element_type=F32)
    n_big, n_small = _segment_copies(b, loff_s, gst_s, p8_s, make_copy)
    pending[0] = n_big
    pending[1] = n_small

    @pl.when(b == N_BLK - 1)
    def _():
        _wait_copies((n_big, n_small), make_copy)
        xbuf[0:TM, :] = jnp.zeros((TM, D), F32)
        _zero_rows_from(xs_hbm, xbuf.at[0:TM], tot_s[0], sem)


def _dispatch(h2, z, loff, ptab, loff_i, gst_i, p8_i, tot_i):
    whole = pl.BlockSpec((TAB_ROWS, E_PAD), lambda b, *_: (0, 0))
    return pl.pallas_call(
        _dispatch_kernel,
        grid_spec=pltpu.PrefetchScalarGridSpec(
            num_scalar_prefetch=4,
            grid=(N_BLK,),
            in_specs=[pl.BlockSpec((TBLK, D), lambda b, *_: (b, 0)),
                      pl.BlockSpec((TBLK, E_PAD), lambda b, *_: (b, 0)),
                      whole, whole],
            out_specs=pl.BlockSpec(memory_space=pl.ANY),
            scratch_shapes=[pltpu.VMEM((RB, D), F32), pltpu.SemaphoreType.DMA(()),
                            pltpu.SMEM((2,), jnp.int32)]),
        out_shape=jax.ShapeDtypeStruct((R_TOT, D), F32),
        compiler_params=_cparams(("arbitrary",)),
        name="moe_dispatch",
    )(loff_i, gst_i, p8_i, tot_i, h2, z, loff, ptab)


MAX_TILES = R_TOT // TM + N_EXPERTS
X_SLOTS = 3
N_MATS = 3


def _expert_kernel(eoff_s, xs_hbm, wg_hbm, wu_hbm, wd_hbm, bg_ref, bu_ref, bd_ref, ys_hbm,
                   wst, wbf, xin, yout, w_sem, x_sem, y_sem, t_exp, t_row, t_first, live):
    def add_expert(e, carry):
        n_t, n_live = carry
        start = eoff_s[e]
        tiles = lax.shift_right_logical(eoff_s[e + 1] - start + (TM - 1), TM_SHIFT)

        def add_tile(i, _):
            t_exp[n_t + i] = e
            t_row[n_t + i] = start + i * TM
            t_first[n_t + i] = jnp.where(i == 0, 1, 0)
            return 0

        lax.fori_loop(0, tiles, add_tile, 0)
        has_rows = jnp.where(tiles > 0, 1, 0)

        @pl.when(tiles > 0)
        def _():
            live[n_live] = e

        return n_t + tiles, n_live + has_rows

    n_tiles, n_live = lax.fori_loop(0, N_EXPERTS, add_expert, (0, 0))

    def w_copies(e, slot):
        return [pltpu.make_async_copy(w.at[e], wst.at[slot, m], w_sem.at[slot])
                for m, w in enumerate((wg_hbm, wu_hbm, wd_hbm))]

    def tile_rows(j):
        return pl.ds(pl.multiple_of(t_row[j], SEG_ALIGN), TM)

    def x_copy(j, slot):
        return pltpu.make_async_copy(xs_hbm.at[tile_rows(j)], xin.at[slot], x_sem.at[slot])

    def y_copy(j):
        return pltpu.make_async_copy(yout, ys_hbm.at[tile_rows(j)], y_sem)

    @pl.when(n_tiles > 0)
    def _():
        for cp in w_copies(live[0], 0):
            cp.start()
        x_copy(0, 0).start()

    @pl.when(n_tiles > 1)
    def _():
        x_copy(1, 1).start()

    def body(j, k):
        e = t_exp[j]

        @pl.when(t_first[j] == 1)
        def _():
            ws = lax.rem(k, 2)
            for cp in w_copies(e, ws):
                cp.wait()
            for m in range(N_MATS):
                wbf[m] = wst[ws, m].astype(BF16)

            @pl.when(k + 1 < n_live)
            def _():
                for cp in w_copies(live[k + 1], 1 - ws):
                    cp.start()

        slot = lax.rem(j, X_SLOTS)
        x_copy(j, slot).wait()

        @pl.when(j + 2 < n_tiles)
        def _():
            x_copy(j + 2, lax.rem(j + 2, X_SLOTS)).start()

        x = xin[slot].astype(BF16)
        gate = jnp.dot(x, wbf[0], preferred_element_type=F32) + bg_ref[e]
        up = jnp.dot(x, wbf[1], preferred_element_type=F32) + bu_ref[e]
        gate = jnp.minimum(gate, SWIGLU_LIMIT)
        up = jnp.clip(up, -SWIGLU_LIMIT, SWIGLU_LIMIT)
        act = (up + 1.0) * gate * jax.nn.sigmoid(SWIGLU_ALPHA * gate)
        y = jnp.dot(act.astype(BF16), wbf[2], preferred_element_type=F32) + bd_ref[e]

        @pl.when(j > 0)
        def _():
            y_copy(j - 1).wait()

        yout[...] = y
        y_copy(j).start()
        return k + t_first[j]

    lax.fori_loop(0, n_tiles, body, 0)

    @pl.when(n_tiles > 0)
    def _():
        y_copy(n_tiles - 1).wait()

    yout[...] = jnp.zeros((TM, D), F32)
    _zero_rows_from(ys_hbm, yout, eoff_s[N_EXPERTS], y_sem)


def _experts(xs, eoff_i, w_gate, b_gate, w_up, b_up, w_down, b_down):
    hbm = pl.BlockSpec(memory_space=pl.ANY)
    bspec = pl.BlockSpec((N_EXPERTS, 1, D), lambda i, *_: (0, 0, 0))
    return pl.pallas_call(
        _expert_kernel,
        grid_spec=pltpu.PrefetchScalarGridSpec(
            num_scalar_prefetch=1,
            grid=(1,),
            in_specs=[hbm, hbm, hbm, hbm, bspec, bspec, bspec],
            out_specs=hbm,
            scratch_shapes=[pltpu.VMEM((2, N_MATS, D, D), F32), pltpu.VMEM((N_MATS, D, D), BF16),
                            pltpu.VMEM((X_SLOTS, TM, D), F32), pltpu.VMEM((TM, D), F32),
                            pltpu.SemaphoreType.DMA((2,)), pltpu.SemaphoreType.DMA((X_SLOTS,)),
                            pltpu.SemaphoreType.DMA(()),
                            pltpu.SMEM((MAX_TILES,), jnp.int32), pltpu.SMEM((MAX_TILES,), jnp.int32),
                            pltpu.SMEM((MAX_TILES,), jnp.int32), pltpu.SMEM((N_EXPERTS,), jnp.int32)]),
        out_shape=jax.ShapeDtypeStruct((R_TOT, D), F32),
        compiler_params=_cparams(("arbitrary",)),
        name="moe_experts",
    )(eoff_i, xs, w_gate, w_up, w_down, b_gate.reshape(N_EXPERTS, 1, D), b_up.reshape(N_EXPERTS, 1, D),
      b_down.reshape(N_EXPERTS, 1, D))


def _combine_kernel(loff_s, gst_s, p8_s, ys_hbm, z_ref, comb_ref, loff_ref, ptab_ref, x1_ref, mod_ref, gf_ref,
                    o_ref, ybuf, sem):
    b = pl.program_id(0)

    def make_copy(local_row, global_row, rows):
        return pltpu.make_async_copy(ys_hbm.at[pl.ds(global_row, rows)], ybuf.at[pl.ds(local_row, rows)], sem)

    started = _segment_copies(b, loff_s, gst_s, p8_s, make_copy)

    last = b * N_EXPERTS + N_EXPERTS - 1
    used_chunks = lax.shift_right_logical(loff_s[last], 3) + p8_s[last]

    def zero_chunk(j, _):
        ybuf[pl.ds(pl.multiple_of(j * SEG_ALIGN, SEG_ALIGN), SEG_ALIGN), :] = jnp.zeros((SEG_ALIGN, D), F32)
        return 0

    lax.fori_loop(used_chunks, RB // SEG_ALIGN, zero_chunk, 0)

    pm, owner = _sort_matrix(b, z_ref, loff_ref, ptab_ref)
    w_all = jnp.dot(owner, comb_ref[...].T, preferred_element_type=F32, precision=HIGHEST)
    w_row = jnp.sum(pm * w_all, axis=-1, keepdims=True)
    pmt = pm.T.astype(BF16)
    _wait_copies(started, make_copy)
    y = (ybuf[...] * w_row).astype(BF16)
    moe = jnp.dot(pmt, y, preferred_element_type=F32)
    gate2 = mod_ref[0][:, 5 * D:6 * D]
    x2 = x1_ref[0] + gate2 * moe
    o_ref[0] = _rms(x2, gf_ref[...])


def _combine(ys, z, comb, loff, ptab, x1, modv, norm_f_g, loff_i, gst_i, p8_i):
    whole = pl.BlockSpec((TAB_ROWS, E_PAD), lambda b, *_: (0, 0))
    tok = pl.BlockSpec((TBLK, E_PAD), lambda b, *_: (b, 0))
    return pl.pallas_call(
        _combine_kernel,
        grid_spec=pltpu.PrefetchScalarGridSpec(
            num_scalar_prefetch=3,
            grid=(N_BLK,),
            in_specs=[pl.BlockSpec(memory_space=pl.ANY), tok, tok, whole, whole,
                      pl.BlockSpec((1, TBLK, D), lambda b, *_: (b, 0, 0)),
                      pl.BlockSpec((1, 1, 6 * D), lambda b, *_: (b, 0, 0)),
                      pl.BlockSpec((1, D), lambda b, *_: (0, 0))],
            out_specs=pl.BlockSpec((1, TBLK, D), lambda b, *_: (b, 0, 0)),
            scratch_shapes=[pltpu.VMEM((RB, D), F32), pltpu.SemaphoreType.DMA(())]),
        out_shape=jax.ShapeDtypeStruct((N_BLK, TBLK, D), F32),
        compiler_params=_cparams(("arbitrary",)),
        name="moe_combine",
    )(loff_i, gst_i, p8_i, ys, z, comb, loff, ptab, x1, modv, norm_f_g.reshape(1, D))


def _moe_and_final(x1, h2, comb, modg, norm_f_g, w_gate, b_gate, w_up, b_up, w_down, b_down):
    z, ptab, loff, gst, eoff = _plan(comb)
    as_scalars = lambda t: t[:N_BLK, :N_EXPERTS].astype(jnp.int32).reshape(N_BLK * N_EXPERTS)
    loff_i, gst_i = as_scalars(loff), as_scalars(gst)
    p8_i = as_scalars(ptab) // SEG_ALIGN
    eoff_i = eoff[0, :N_EXPERTS + 1].astype(jnp.int32)
    xs = _dispatch(h2, z, loff, ptab, loff_i, gst_i, p8_i, eoff_i[N_EXPERTS:])
    ys = _experts(xs, eoff_i, w_gate, b_gate, w_up, b_up, w_down, b_down)
    return _combine(ys, z, comb, loff, ptab, x1, modg.reshape(N_BLK, 1, 6 * D), norm_f_g,
                    loff_i, gst_i, p8_i)


def _grid_pos_embed(n_tokens, dim):
    rows = n_tokens // GRID_W
    t = jnp.arange(rows * GRID_W)
    r = (t // GRID_W).astype(F32)
    col = (t % GRID_W).astype(F32)
    q = dim // 4
    omega = 1.0 / POS_TEMP ** (jnp.arange(q, dtype=F32) / q)

    def emb(p):
        a = p[:, None] * omega[None, :]
        return jnp.concatenate([jnp.sin(a), jnp.cos(a)], axis=-1)

    return jnp.concatenate([emb(r), emb(col)], axis=-1)


def _block_diag_chunks(m):
    _, _, a, b = m.shape
    m = m.reshape(2, N_S5_CHUNKS, S5_CHUNK, a, b)
    eye = jnp.eye(S5_CHUNK, dtype=m.dtype)
    bd = m[:, :, :, :, None, :] * eye[None, None, :, None, :, None]
    return bd.reshape(2, N_S5_CHUNKS, S5_CHUNK * a, S5_CHUNK * b)


def kernel(x_prompt, x_sample, c, state_s5_re, state_s5_im, c_ctx, w_ada, b_ada, norm1_g, w_in, s5_lam_re,
           s5_lam_im, s5_log_dt, s5_b_re, s5_b_im, s5_c_re, s5_c_im, s5_d, s5_w_glu, w_fnet, w_out, norm2_g,
           w_router, b_router, w_gate, b_gate, w_up, b_up, w_down, b_down, norm_f_g):
    n_ctx, n_lat = x_prompt.shape[0], x_sample.shape[0]
    assert x_prompt.shape == (SUB * (N_GRP - 1), L_BLK, D) and x_sample.shape == (2, LAT_CHUNKS * L_BLK, D)
    assert w_ada.shape[0] == 1, "one trunk layer"
    layer = 0

    cvec = jnp.zeros((SUB, D), F32).at[0].set(c_ctx).at[1:1 + n_lat].set(c)
    mod = _adaln(cvec, w_ada[layer], b_ada[layer])
    lat_rows = jnp.repeat(mod[1:1 + n_lat], LAT_CHUNKS, axis=0)
    ctx_rows = jnp.broadcast_to(mod[0:1], (SUB, 6 * D))
    modg = jnp.stack([ctx_rows] * (N_GRP - 1) + [lat_rows])[:, :, None, :]

    xp4 = x_prompt.reshape(N_GRP - 1, SUB, L_BLK, D)
    xs4 = x_sample.reshape(1, SUB, L_BLK, D)
    pos3 = _grid_pos_embed(LAT_CHUNKS * L_BLK, D).reshape(LAT_CHUNKS, L_BLK, D)
    perm = _perm_matrix()

    us, uf = _pre(xp4, xs4, pos3, modg, norm1_g[layer], w_in[layer], jnp.asarray(perm, BF16))

    a_re, a_im, bb_re, bb_im = _discretise(s5_lam_re[layer], s5_lam_im[layer], s5_log_dt[layer],
                                           s5_b_re[layer], s5_b_im[layer])
    a_cat = jnp.concatenate([a_re.reshape(2, N_S5_CHUNKS, CH_P), a_im.reshape(2, N_S5_CHUNKS, CH_P)], axis=-1)
    a8 = jnp.broadcast_to(a_cat[:, :, None, :], (2, N_S5_CHUNKS, SUB, 2 * CH_P))
    bb_re = bb_re.reshape(2, N_S5_GROUPS, S5_H, S5_P)
    bb_im = bb_im.reshape(2, N_S5_GROUPS, S5_H, S5_P)
    bbmat = jnp.concatenate([_block_diag_chunks(bb_re), _block_diag_chunks(bb_im)], axis=-1).astype(BF16)
    c_re_t = jnp.swapaxes(s5_c_re[layer], -1, -2)
    c_im_t = jnp.swapaxes(s5_c_im[layer], -1, -2)
    ccmat = jnp.concatenate([_block_diag_chunks(c_re_t), _block_diag_chunks(-c_im_t)], axis=-2).astype(BF16)

    st = jnp.stack([state_s5_re[:, layer], state_s5_im[:, layer]], axis=2)
    st = st.reshape(n_lat, 2, 2, N_S5_CHUNKS, CH_P)
    st = jnp.transpose(st, (1, 3, 0, 2, 4)).reshape(2, N_S5_CHUNKS, n_lat, 2 * CH_P)
    init8 = jnp.zeros((2, N_S5_CHUNKS, n_lat, LAT_CHUNKS, 2 * CH_P), F32)
    init8 = init8.at[0, :, :, 0].set(st[0]).at[1, :, :, LAT_CHUNKS - 1].set(st[1])
    init8 = init8.reshape(2, N_S5_CHUNKS, SUB, 2 * CH_P)

    y_s5, fin = _s5(us, bbmat, ccmat, a8, init8, s5_d[layer])

    m1, m2 = _fnet_weights(w_fnet[layer])
    uf_ctx = uf[:N_GRP - 1].reshape(n_ctx, L_BLK, D_FN)
    uf_lat = uf[N_GRP - 1].reshape(n_lat, LAT_CHUNKS * L_BLK, D_FN)
    yf = jnp.concatenate([_fnet(uf_ctx, m1, m2).reshape(N_GRP - 1, SUB, L_BLK, D_FN),
                          _fnet(uf_lat, m1, m2).reshape(1, SUB, L_BLK, D_FN)], axis=0)

    x1, h2, comb = _post(y_s5, yf, xp4, xs4, pos3, modg, s5_w_glu[layer], w_out[layer], norm2_g[layer],
                         w_router[layer], b_router[layer], jnp.asarray(perm.T, BF16))

    y_all = _moe_and_final(x1.reshape(N_BLK, TBLK, D), h2.reshape(T_TOK, D), comb.reshape(T_TOK, E_PAD), modg,
                           norm_f_g, w_gate[layer], b_gate[layer], w_up[layer], b_up[layer], w_down[layer],
                           b_down[layer])

    y_prompt = y_all[:n_ctx]
    y_sample = y_all[n_ctx:].reshape(n_lat, LAT_CHUNKS * L_BLK, D)

    fin = fin[:, :N_GRP - 1].reshape(2, N_GRP - 1, SUB, N_S5_CHUNKS, 2, CH_P)
    fin = jnp.transpose(fin, (4, 1, 2, 0, 3, 5)).reshape(2, n_ctx, 1, 2, N_S5_GROUPS, S5_P)
    return (y_prompt, y_sample, fin[0], fin[1])
```

```python
import functools
import math

import numpy as np
import jax
import jax.numpy as jnp
from jax import lax
from jax.experimental import pallas as pl
from jax.experimental.pallas import tpu as pltpu

F32 = jnp.float32
BF16 = jnp.bfloat16
HIGHEST = lax.Precision.HIGHEST

D = 1024
D_S5 = 768
S5_H = 16
S5_P = 64
N_S5_GROUPS = 48
D_FN = 256
FN_GW = 64
N_EXPERTS = 32
TOP_K = 4
E_PAD = 128
SWIGLU_LIMIT = 7.0
SWIGLU_ALPHA = 1.702
RMS_EPS = 1e-6
POS_TEMP = 10000.0
GRID_W = 64

L_BLK = 256
SUB = 8
N_GRP = 3
GRP_ROWS = L_BLK * SUB
LAT_CHUNKS = 4
T_TOK = N_GRP * GRP_ROWS
TC = 128
N_TC = L_BLK // TC
PERM_T = 32
S5_CHUNK = 16
N_S5_CHUNKS = N_S5_GROUPS // S5_CHUNK
CH_U = S5_CHUNK * S5_H
CH_P = S5_CHUNK * S5_P
ROW_BLK = 256
VMEM_LIMIT = 56 * 1024 * 1024


def _cparams(sem):
    return pltpu.CompilerParams(dimension_semantics=sem, vmem_limit_bytes=VMEM_LIMIT)


def _adaln_kernel(c_ref, w_ref, b_ref, o_ref):
    c = c_ref[...]
    s = c * jax.nn.sigmoid(c)
    o_ref[...] = jnp.dot(s, w_ref[...], preferred_element_type=F32, precision=HIGHEST) + b_ref[...]


def _adaln(cvec, w_ada, b_ada):
    n = w_ada.shape[1]
    rows = cvec.shape[0]
    return pl.pallas_call(
        _adaln_kernel,
        grid=(n // D,),
        in_specs=[pl.BlockSpec((rows, D), lambda j: (0, 0)),
                  pl.BlockSpec((D, D), lambda j: (0, j)),
                  pl.BlockSpec((1, D), lambda j: (0, j))],
        out_specs=pl.BlockSpec((rows, D), lambda j: (0, j)),
        out_shape=jax.ShapeDtypeStruct((rows, n), F32),
        compiler_params=_cparams(("arbitrary",)),
        name="adaln",
    )(cvec, w_ada, b_ada.reshape(1, n))


def _zoh(lam_re, lam_im, log_dt):
    dt = jnp.exp(log_dt)
    a_re = jnp.minimum(lam_re, -1e-4)
    a_im = lam_im
    mag = jnp.exp(a_re * dt)
    ab_re = mag * jnp.cos(a_im * dt)
    ab_im = mag * jnp.sin(a_im * dt)
    den = a_re * a_re + a_im * a_im
    nr = ab_re - 1.0
    f_re = (nr * a_re + ab_im * a_im) / den
    f_im = (ab_im * a_re - nr * a_im) / den
    return ab_re, ab_im, f_re, f_im


def _s5_params_kernel(lre3_ref, lim3_ref, ldt3_ref, lre2_ref, lim2_ref, ldt2_ref, bre_ref, bim_ref,
                      cre_ref, cim_ref, bb_ref, cc_ref, a_ref):
    exact = functools.partial(jnp.dot, preferred_element_type=F32, precision=HIGHEST)
    spread = jnp.where(lax.broadcasted_iota(jnp.int32, (S5_P, CH_P), 1) % S5_P
                       == lax.broadcasted_iota(jnp.int32, (S5_P, CH_P), 0), 1.0, 0.0)

    def block_diag(m):
        wide = exact(m.reshape(CH_U, S5_P), spread)
        row_g = lax.broadcasted_iota(jnp.int32, (CH_U, CH_P), 0) // S5_H
        col_g = lax.broadcasted_iota(jnp.int32, (CH_U, CH_P), 1) // S5_P
        return jnp.where(row_g == col_g, wide, 0.0)

    _, _, f_re, f_im = _zoh(lre3_ref[0, 0], lim3_ref[0, 0], ldt3_ref[0, 0])
    b_re = bre_ref[0, 0]
    b_im = bim_ref[0, 0]
    bb_ref[0, 0] = jnp.concatenate([block_diag(f_re * b_re - f_im * b_im),
                                    block_diag(f_re * b_im + f_im * b_re)], axis=1).astype(BF16)
    cc_ref[0, 0] = jnp.concatenate([block_diag(cre_ref[0, 0]).T,
                                    -block_diag(cim_ref[0, 0]).T], axis=0).astype(BF16)

    ab_re, ab_im, _, _ = _zoh(lre2_ref[0, 0], lim2_ref[0, 0], ldt2_ref[0, 0])
    row_g = lax.broadcasted_iota(jnp.int32, (S5_CHUNK, CH_P), 0)
    col_g = lax.broadcasted_iota(jnp.int32, (S5_CHUNK, CH_P), 1) // S5_P

    def lane_row(a):
        flat = jnp.sum(jnp.where(row_g == col_g, exact(a, spread), 0.0), axis=0, keepdims=True)
        return jnp.broadcast_to(flat, (SUB, CH_P))

    a_ref[0, 0] = jnp.concatenate([lane_row(ab_re), lane_row(ab_im)], axis=1)


def _s5_params(lam_re, lam_im, log_dt, b_re, b_im, c_re, c_im):
    nc = N_S5_CHUNKS
    ldt = jnp.broadcast_to(log_dt[:, :, None], (2, N_S5_GROUPS, S5_P))
    g3 = lambda a: a.reshape(2, nc, S5_CHUNK, 1, S5_P)
    g2 = lambda a: a.reshape(2, nc, S5_CHUNK, S5_P)
    ghp = lambda a: a.reshape(2, nc, S5_CHUNK, S5_H, S5_P)
    spec3 = pl.BlockSpec((1, 1, S5_CHUNK, 1, S5_P), lambda d, c: (d, c, 0, 0, 0))
    spec2 = pl.BlockSpec((1, 1, S5_CHUNK, S5_P), lambda d, c: (d, c, 0, 0))
    spec_ghp = pl.BlockSpec((1, 1, S5_CHUNK, S5_H, S5_P), lambda d, c: (d, c, 0, 0, 0))
    return pl.pallas_call(
        _s5_params_kernel,
        grid=(2, nc),
        in_specs=[spec3, spec3, spec3, spec2, spec2, spec2, spec_ghp, spec_ghp, spec_ghp, spec_ghp],
        out_specs=[pl.BlockSpec((1, 1, CH_U, 2 * CH_P), lambda d, c: (d, c, 0, 0)),
                   pl.BlockSpec((1, 1, 2 * CH_P, CH_U), lambda d, c: (d, c, 0, 0)),
                   pl.BlockSpec((1, 1, SUB, 2 * CH_P), lambda d, c: (d, c, 0, 0))],
        out_shape=[jax.ShapeDtypeStruct((2, nc, CH_U, 2 * CH_P), BF16),
                   jax.ShapeDtypeStruct((2, nc, 2 * CH_P, CH_U), BF16),
                   jax.ShapeDtypeStruct((2, nc, SUB, 2 * CH_P), F32)],
        compiler_params=_cparams(("arbitrary", "arbitrary")),
        name="s5_params",
    )(g3(lam_re), g3(lam_im), g3(ldt), g2(lam_re), g2(lam_im), g2(ldt),
      ghp(jnp.swapaxes(b_re, -1, -2)), ghp(jnp.swapaxes(b_im, -1, -2)), ghp(c_re), ghp(c_im))


def _rms(x, g):
    return x * lax.rsqrt(jnp.mean(x * x, axis=-1, keepdims=True) + RMS_EPS) * g


def _load_x(g, xp_ref, xs_ref, pos_ref):
    is_lat = g == N_GRP - 1
    pos = pos_ref[...]
    pos8 = jnp.concatenate([pos, pos], axis=0)
    return jnp.where(is_lat, xs_ref[0] + pos8, xp_ref[0])


def _x_specs():
    n_ctx = N_GRP - 1
    xp_spec = pl.BlockSpec(
        (1, SUB, TC, D),
        lambda g, tc: (jnp.minimum(g, n_ctx - 1), 0, jnp.where(g >= n_ctx, N_TC - 1, tc), 0))
    xs_spec = pl.BlockSpec((1, SUB, TC, D), lambda g, tc: (0, 0, jnp.where(g >= n_ctx, tc, 0), 0))
    pos_spec = pl.BlockSpec((LAT_CHUNKS, TC, D), lambda g, tc: (0, tc, 0))
    return xp_spec, xs_spec, pos_spec


def _perm_matrix():
    p = np.zeros((SUB * PERM_T, SUB * PERM_T), np.float32)
    for s in range(SUB):
        for j in range(PERM_T):
            p[j * SUB + s, s * PERM_T + j] = 1.0
    return p


def _pre_kernel(xp_ref, xs_ref, pos_ref, mod_ref, g1_ref, win_ref, perm_ref, us_ref, uf_ref):
    g = pl.program_id(0)
    x = _load_x(g, xp_ref, xs_ref, pos_ref)
    mod = mod_ref[0]
    shift1 = mod[:, :, 0:D]
    scale1 = mod[:, :, D:2 * D]
    h = _rms(x, g1_ref[...]) * (1.0 + scale1) + shift1
    h2d = h.reshape(SUB * TC, D).astype(BF16)
    proj = jnp.dot(h2d, win_ref[...].astype(BF16), preferred_element_type=F32)
    uf_ref[0] = proj[:, D_S5:].astype(BF16).reshape(SUB, TC, D_FN)
    u = proj[:, :D_S5].astype(BF16)
    perm = perm_ref[...]
    for q in range(TC // PERM_T):
        piece = jnp.concatenate(
            [u[s * TC + q * PERM_T: s * TC + (q + 1) * PERM_T] for s in range(SUB)], axis=0)
        us_ref[0, q * SUB * PERM_T:(q + 1) * SUB * PERM_T, :] = jnp.dot(
            perm, piece, preferred_element_type=F32).astype(BF16)


def _pre(xp4, xs4, pos3, modg, norm1_g, w_in, perm):
    xp_spec, xs_spec, pos_spec = _x_specs()
    return pl.pallas_call(
        _pre_kernel,
        grid=(N_GRP, N_TC),
        in_specs=[xp_spec, xs_spec, pos_spec,
                  pl.BlockSpec((1, SUB, 1, 6 * D), lambda g, tc: (g, 0, 0, 0)),
                  pl.BlockSpec((1, D), lambda g, tc: (0, 0)),
                  pl.BlockSpec((D, D), lambda g, tc: (0, 0)),
                  pl.BlockSpec((SUB * PERM_T, SUB * PERM_T), lambda g, tc: (0, 0))],
        out_specs=[pl.BlockSpec((1, SUB * TC, D_S5), lambda g, tc: (g, tc, 0)),
                   pl.BlockSpec((1, SUB, TC, D_FN), lambda g, tc: (g, 0, tc, 0))],
        out_shape=[jax.ShapeDtypeStruct((N_GRP, GRP_ROWS, D_S5), BF16),
                   jax.ShapeDtypeStruct((N_GRP, SUB, L_BLK, D_FN), BF16)],
        compiler_params=_cparams(("arbitrary", "arbitrary")),
        name="pre_mixer",
    )(xp4, xs4, pos3, modg, norm1_g.reshape(1, D), w_in, perm)


def _cmul(ar, ai, br, bi):
    return ar * br - ai * bi, ar * bi + ai * br


def _s5_kernel(us_ref, bb_ref, cc_ref, a_ref, init_ref, dskip_ref, y_ref, fin_ref, s_ref):
    g = pl.program_id(0)
    d = pl.program_id(2)
    n_rb = GRP_ROWS // ROW_BLK
    half = CH_P // 2

    def proj_in(rb, _):
        rows = pl.ds(pl.multiple_of(rb * ROW_BLK, ROW_BLK), ROW_BLK)
        s_ref[rows, :] = jnp.dot(us_ref[0, rows, :], bb_ref[0, 0], preferred_element_type=F32)
        return 0

    lax.fori_loop(0, n_rb, proj_in, 0)

    def tile_row(k):
        t = k + d * (L_BLK - 1 - 2 * k)
        return pl.ds(pl.multiple_of(t * SUB, SUB), SUB)

    is_lat = g == N_GRP - 1
    sub_id = lax.broadcasted_iota(jnp.int32, (SUB, half), 0) % LAT_CHUNKS

    for hh in range(2):
        re_cols = slice(hh * half, (hh + 1) * half)
        im_cols = slice(CH_P + hh * half, CH_P + (hh + 1) * half)
        a_re = a_ref[0, 0, :, re_cols]
        a_im = a_ref[0, 0, :, im_cols]
        zero = jnp.zeros((SUB, half), F32)
        s0_re = jnp.where(is_lat, init_ref[0, 0, :, re_cols], zero)
        s0_im = jnp.where(is_lat, init_ref[0, 0, :, im_cols], zero)

        def scan_body(i, carry, a_re=a_re, a_im=a_im, re_cols=re_cols, im_cols=im_cols):
            s_re, s_im = carry
            for j in range(8):
                rows = tile_row(i * 8 + j)
                n_re = a_re * s_re - a_im * s_im + s_ref[rows, re_cols]
                n_im = a_re * s_im + a_im * s_re + s_ref[rows, im_cols]
                s_ref[rows, re_cols] = n_re
                s_ref[rows, im_cols] = n_im
                s_re, s_im = n_re, n_im
            return s_re, s_im

        f_re, f_im = lax.fori_loop(0, L_BLK // 8, scan_body, (s0_re, s0_im))
        fin_ref[0, 0, :, re_cols] = f_re
        fin_ref[0, 0, :, im_cols] = f_im

        @pl.when(is_lat)
        def _(a_re=a_re, a_im=a_im, f_re=f_re, f_im=f_im, re_cols=re_cols, im_cols=im_cols):
            p_re, p_im = a_re, a_im
            for _ in range(8):
                p_re, p_im = _cmul(p_re, p_im, p_re, p_im)
            fwd = d == 0
            lo = jnp.where(fwd, 1, 0)
            hi = jnp.where(fwd, LAT_CHUNKS - 1, LAT_CHUNKS - 2)
            keep = (sub_id >= lo) & (sub_id <= hi)

            def from_prev(v):
                return jnp.where(keep, jnp.where(fwd, pltpu.roll(v, 1, 0), pltpu.roll(v, SUB - 1, 0)), 0.0)

            t_re, t_im = f_re, f_im
            for _ in range(LAT_CHUNKS - 2):
                m_re, m_im = _cmul(p_re, p_im, from_prev(t_re), from_prev(t_im))
                t_re, t_im = f_re + m_re, f_im + m_im
            c_re, c_im = from_prev(t_re), from_prev(t_im)

            def fix_body(i, carry):
                q_re, q_im = carry
                for j in range(8):
                    rows = tile_row(i * 8 + j)
                    m_re, m_im = _cmul(q_re, q_im, c_re, c_im)
                    s_ref[rows, re_cols] = s_ref[rows, re_cols] + m_re
                    s_ref[rows, im_cols] = s_ref[rows, im_cols] + m_im
                    q_re, q_im = _cmul(q_re, q_im, a_re, a_im)
                return q_re, q_im

            lax.fori_loop(0, L_BLK // 8, fix_body, (a_re, a_im))

    def proj_out(rb):
        rows = pl.ds(pl.multiple_of(rb * ROW_BLK, ROW_BLK), ROW_BLK)
        return rows, jnp.dot(s_ref[rows, :].astype(BF16), cc_ref[0, 0], preferred_element_type=F32)

    @pl.when(d == 0)
    def _():
        def body(rb, _):
            rows, yb = proj_out(rb)
            y_ref[0, rows, :] = dskip_ref[...] * us_ref[0, rows, :].astype(F32) + yb
            return 0
        lax.fori_loop(0, n_rb, body, 0)

    @pl.when(d == 1)
    def _():
        def body(rb, _):
            rows, yb = proj_out(rb)
            y_ref[0, rows, :] = y_ref[0, rows, :] + yb
            return 0
        lax.fori_loop(0, n_rb, body, 0)


def _s5(us, bbmat, ccmat, a8, init8, d_skip):
    return pl.pallas_call(
        _s5_kernel,
        grid=(N_GRP, N_S5_CHUNKS, 2),
        in_specs=[pl.BlockSpec((1, GRP_ROWS, CH_U), lambda g, c, d: (g, 0, c)),
                  pl.BlockSpec((1, 1, CH_U, 2 * CH_P), lambda g, c, d: (d, c, 0, 0)),
                  pl.BlockSpec((1, 1, 2 * CH_P, CH_U), lambda g, c, d: (d, c, 0, 0)),
                  pl.BlockSpec((1, 1, SUB, 2 * CH_P), lambda g, c, d: (d, c, 0, 0)),
                  pl.BlockSpec((1, 1, SUB, 2 * CH_P), lambda g, c, d: (d, c, 0, 0)),
                  pl.BlockSpec((1, CH_U), lambda g, c, d: (0, c))],
        out_specs=[pl.BlockSpec((1, GRP_ROWS, CH_U), lambda g, c, d: (g, 0, c)),
                   pl.BlockSpec((1, 1, SUB, 2 * CH_P), lambda g, c, d: (d, g, 0, c))],
        out_shape=[jax.ShapeDtypeStruct((N_GRP, GRP_ROWS, D_S5), F32),
                   jax.ShapeDtypeStruct((2, N_GRP, SUB, N_S5_CHUNKS * 2 * CH_P), F32)],
        scratch_shapes=[pltpu.VMEM((GRP_ROWS, 2 * CH_P), F32)],
        compiler_params=_cparams(("arbitrary", "arbitrary", "arbitrary")),
        name="s5_scan",
    )(us, bbmat, ccmat, a8, init8, d_skip.reshape(1, D_S5))


def _fnet_w_kernel(c_ref, s_ref, w_ref, m1_ref, m2_ref):
    w = w_ref[...]
    m1_ref[...] = jnp.dot(c_ref[...], w, preferred_element_type=F32, precision=HIGHEST).astype(BF16)
    m2_ref[...] = jnp.dot(s_ref[...], w, preferred_element_type=F32, precision=HIGHEST).astype(BF16)


FN_SEQ = LAT_CHUNKS
FN_CTX_STEPS = (N_GRP - 1) * SUB // FN_SEQ


def _fnet_kernel(u_ref, cos_s_ref, sin_s_ref, cos_l_ref, sin_l_ref, m1_ref, m2_ref, o_ref):
    i = pl.program_id(0)
    m1 = m1_ref[...]
    m2 = m2_ref[...]

    def mix(u, cos_ref, sin_ref):
        v1 = jnp.dot(u, m1, preferred_element_type=F32).astype(BF16)
        v2 = jnp.dot(u, m2, preferred_element_type=F32).astype(BF16)
        return (jnp.dot(cos_ref[...].astype(BF16), v1, preferred_element_type=F32)
                - jnp.dot(sin_ref[...].astype(BF16), v2, preferred_element_type=F32)).astype(BF16)

    @pl.when(i < FN_CTX_STEPS)
    def _():
        for s in range(FN_SEQ):
            o_ref[0, s] = mix(u_ref[0, s], cos_s_ref, sin_s_ref)

    @pl.when(i >= FN_CTX_STEPS)
    def _():
        u = u_ref[0].reshape(FN_SEQ * L_BLK, D_FN)
        o_ref[0] = mix(u, cos_l_ref, sin_l_ref).reshape(FN_SEQ, L_BLK, D_FN)


def _dft_tables(n):
    k = np.arange(n, dtype=np.int64)
    ang = (2.0 * np.pi / n) * ((k[:, None] * k[None, :]) % n).astype(np.float64)
    scale = 1.0 / math.sqrt(n)
    return (np.cos(ang) * scale).astype(np.float32), (np.sin(ang) * scale).astype(np.float32)


def _fnet(uf, m1, m2):
    n_lat_steps = SUB // FN_SEQ
    cos_s, sin_s = _dft_tables(L_BLK)
    cos_l, sin_l = _dft_tables(FN_SEQ * L_BLK)
    full = lambda i: (0, 0)
    per_grp = SUB // FN_SEQ
    blk = pl.BlockSpec(
        (1, FN_SEQ, L_BLK, D_FN),
        lambda i: (jnp.minimum(i // per_grp, N_GRP - 1),
                   jnp.where(i < FN_CTX_STEPS, i % per_grp, i - FN_CTX_STEPS), 0, 0))
    table = lambda n: pl.BlockSpec((n, n), full)
    return pl.pallas_call(
        _fnet_kernel,
        grid=(FN_CTX_STEPS + n_lat_steps,),
        in_specs=[blk, table(L_BLK), table(L_BLK), table(FN_SEQ * L_BLK), table(FN_SEQ * L_BLK),
                  table(D_FN), table(D_FN)],
        out_specs=blk,
        out_shape=jax.ShapeDtypeStruct(uf.shape, BF16),
        compiler_params=_cparams(("arbitrary",)),
        name="fnet",
    )(uf, jnp.asarray(cos_s), jnp.asarray(sin_s), jnp.asarray(cos_l), jnp.asarray(sin_l), m1, m2)


def _fnet_weights(w_fnet):
    n_g = D_FN // FN_GW
    cos_c, sin_c = _dft_tables(FN_GW)
    eye = np.eye(n_g, dtype=np.float32)
    cos_bd = np.kron(eye, cos_c)
    sin_bd = np.kron(eye, sin_c)
    w_bd = (w_fnet[:, :, None, :] * jnp.asarray(eye)[:, None, :, None]).reshape(D_FN, D_FN)
    out = jax.ShapeDtypeStruct((D_FN, D_FN), BF16)
    return pl.pallas_call(_fnet_w_kernel, out_shape=(out, out), name="fnet_weights")(
        jnp.asarray(cos_bd), jnp.asarray(sin_bd), w_bd)


def _gelu_tanh(x):
    return 0.5 * x * (1.0 + jnp.tanh(math.sqrt(2.0 / math.pi) * (x + 0.044715 * (x * x * x))))


def _post_kernel(y_ref, yf_ref, xp_ref, xs_ref, pos_ref, mod_ref, wglu_ref, wout_ref, g2_ref, wr_ref,
                 br_ref, permt_ref, x1_ref, h2_ref, comb_ref):
    g = pl.program_id(0)
    z = _gelu_tanh(y_ref[0])
    gate = jnp.dot(z.astype(BF16), wglu_ref[...].astype(BF16), preferred_element_type=F32)
    gl = (z * jax.nn.sigmoid(gate)).astype(BF16)
    permt = permt_ref[...]
    n_q = TC // PERM_T
    nat = [jnp.dot(permt, gl[q * SUB * PERM_T:(q + 1) * SUB * PERM_T], preferred_element_type=F32).astype(BF16)
           for q in range(n_q)]
    gl_nat = jnp.concatenate(
        [nat[q][s * PERM_T:(s + 1) * PERM_T] for s in range(SUB) for q in range(n_q)], axis=0)
    w_out = wout_ref[...].astype(BF16)
    mixed = (jnp.dot(gl_nat, w_out[:D_S5], preferred_element_type=F32)
             + jnp.dot(yf_ref[0].reshape(SUB * TC, D_FN), w_out[D_S5:], preferred_element_type=F32))
    x = _load_x(g, xp_ref, xs_ref, pos_ref)
    mod = mod_ref[0]
    gate1 = mod[:, :, 2 * D:3 * D]
    shift2 = mod[:, :, 3 * D:4 * D]
    scale2 = mod[:, :, 4 * D:5 * D]
    x1 = x + gate1 * mixed.reshape(SUB, TC, D)
    x1_ref[0] = x1
    h2 = _rms(x1, g2_ref[...]) * (1.0 + scale2) + shift2
    h2_ref[0] = h2.astype(BF16)

    logits = jnp.dot(h2.reshape(SUB * TC, D), wr_ref[...], preferred_element_type=F32,
                     precision=HIGHEST) + br_ref[...]
    lane = lax.broadcasted_iota(jnp.int32, logits.shape, 1)
    top_v, hots = [], []
    cur = logits
    for _ in range(TOP_K):
        m = jnp.max(cur, axis=-1, keepdims=True)
        idx = jnp.min(jnp.where(cur == m, lane, E_PAD), axis=-1, keepdims=True)
        hot = lane == idx
        top_v.append(m)
        hots.append(hot)
        cur = jnp.where(hot, -3.0e38, cur)
    exps = [jnp.exp(v - top_v[0]) for v in top_v]
    denom = exps[0] + exps[1] + exps[2] + exps[3]
    comb = jnp.zeros(logits.shape, F32)
    for k in range(TOP_K):
        comb = comb + jnp.where(hots[k], exps[k] / denom, 0.0)
    comb_ref[0] = comb.reshape(SUB, TC, E_PAD)


def _post(y, yf, xp4, xs4, pos3, modg, w_glu, w_out, norm2_g, w_router, b_router, permt):
    xp_spec, xs_spec, pos_spec = _x_specs()
    const2 = lambda g, tc: (0, 0)
    wr = jnp.zeros((D, E_PAD), F32).at[:, :N_EXPERTS].set(w_router)
    br = jnp.full((1, E_PAD), -1.0e30, F32).at[0, :N_EXPERTS].set(b_router)
    blk = lambda w: pl.BlockSpec((1, SUB, TC, w), lambda g, tc: (g, 0, tc, 0))
    return pl.pallas_call(
        _post_kernel,
        grid=(N_GRP, N_TC),
        in_specs=[pl.BlockSpec((1, SUB * TC, D_S5), lambda g, tc: (g, tc, 0)),
                  blk(D_FN), xp_spec, xs_spec, pos_spec,
                  pl.BlockSpec((1, SUB, 1, 6 * D), lambda g, tc: (g, 0, 0, 0)),
                  pl.BlockSpec((D_S5, D_S5), const2),
                  pl.BlockSpec((D, D), const2),
                  pl.BlockSpec((1, D), const2),
                  pl.BlockSpec((D, E_PAD), const2),
                  pl.BlockSpec((1, E_PAD), const2),
                  pl.BlockSpec((SUB * PERM_T, SUB * PERM_T), const2)],
        out_specs=[blk(D), blk(D), blk(E_PAD)],
        out_shape=[jax.ShapeDtypeStruct((N_GRP, SUB, L_BLK, D), F32),
                   jax.ShapeDtypeStruct((N_GRP, SUB, L_BLK, D), BF16),
                   jax.ShapeDtypeStruct((N_GRP, SUB, L_BLK, E_PAD), F32)],
        compiler_params=_cparams(("arbitrary", "arbitrary")),
        name="post_mixer",
    )(y, yf, xp4, xs4, pos3, modg, w_glu, w_out, norm2_g.reshape(1, D), wr, br, permt)


TBLK = L_BLK
N_BLK = T_TOK // TBLK
N_CTX_BLK = (N_GRP - 1) * SUB
SEG_ALIGN = 8
RB = 1280
assert RB >= TBLK * TOP_K + N_EXPERTS * (SEG_ALIGN - 1) and RB % 128 == 0
TM = 256
TM_SHIFT = 8
assert 1 << TM_SHIFT == TM
R_TOT = T_TOK * TOP_K + N_BLK * N_EXPERTS * (SEG_ALIGN - 1) + TM
TAB_ROWS = 32
assert TAB_ROWS >= N_BLK
BIG_ROWS = 32
BIG_SHIFT = 2
assert SEG_ALIGN << BIG_SHIFT == BIG_ROWS


def _plan_kernel(comb_ref, z_ref, ptab_ref, loff_ref, gst_ref, eoff_ref):
    row = lax.broadcasted_iota(jnp.int32, (TBLK, TBLK), 0)
    col = lax.broadcasted_iota(jnp.int32, (TBLK, TBLK), 1)
    earlier = jnp.where(row > col, 1.0, 0.0).astype(BF16)
    ptab_ref[...] = jnp.zeros_like(ptab_ref)

    def body(b, _):
        rows = pl.ds(pl.multiple_of(b * TBLK, TBLK), TBLK)
        m = jnp.where(comb_ref[rows, :] > 0.0, 1.0, 0.0)
        rank = jnp.dot(earlier, m.astype(BF16), preferred_element_type=F32)
        z_ref[rows, :] = m * (rank + 1.0)
        n = jnp.sum(m, axis=0, keepdims=True)
        ptab_ref[pl.ds(b, 1), :] = jnp.floor((n + (SEG_ALIGN - 1)) * (1.0 / SEG_ALIGN)) * SEG_ALIGN
        return 0

    lax.fori_loop(0, N_BLK, body, 0)

    ptab = ptab_ref[...]
    er = lax.broadcasted_iota(jnp.int32, (E_PAD, E_PAD), 0)
    ec = lax.broadcasted_iota(jnp.int32, (E_PAD, E_PAD), 1)
    before = jnp.where(er < ec, 1.0, 0.0)
    exact = functools.partial(jnp.dot, preferred_element_type=F32, precision=HIGHEST)
    loff_ref[...] = exact(ptab, before)
    tot = jnp.sum(ptab, axis=0, keepdims=True)
    eoff = exact(jnp.broadcast_to(tot, (SUB, E_PAD)), before)
    eoff_ref[...] = eoff
    br = lax.broadcasted_iota(jnp.int32, (TAB_ROWS, TAB_ROWS), 0)
    bc = lax.broadcasted_iota(jnp.int32, (TAB_ROWS, TAB_ROWS), 1)
    gst_ref[...] = eoff[0:1] + exact(jnp.where(br > bc, 1.0, 0.0), ptab)


def _plan(comb):
    tab = jax.ShapeDtypeStruct((TAB_ROWS, E_PAD), F32)
    return pl.pallas_call(
        _plan_kernel,
        out_shape=(jax.ShapeDtypeStruct((T_TOK, E_PAD), F32), tab, tab, tab,
                   jax.ShapeDtypeStruct((SUB, E_PAD), F32)),
        compiler_params=pltpu.CompilerParams(vmem_limit_bytes=VMEM_LIMIT),
        name="moe_plan",
    )(comb)


def _sort_matrix(b, z_ref, loff_ref, ptab_ref):
    loff = loff_ref[pl.ds(b, 1), :]
    size = ptab_ref[pl.ds(b, 1), :]
    r = lax.broadcasted_iota(jnp.int32, (RB, E_PAD), 0).astype(F32)
    owner = jnp.where(r >= loff, jnp.where(r < loff + size, 1.0, 0.0), 0.0)
    rank1 = r[:, 0:1] - jnp.sum(owner * loff, axis=-1, keepdims=True) + 1.0
    zt = z_ref[...].T
    v = jnp.dot(owner.astype(BF16), zt.astype(BF16), preferred_element_type=F32)
    return jnp.where(v == rank1, 1.0, 0.0), owner


def _segment_copies(b, loff_s, gst_s, p8_s, make_copy):
    def per_expert(e, counts):
        k = b * N_EXPERTS + e
        chunks = p8_s[k]
        n_big = lax.shift_right_logical(chunks, BIG_SHIFT)
        n_small = chunks - n_big * (BIG_ROWS // SEG_ALIGN)
        local0 = loff_s[k]
        global0 = gst_s[k]

        def big(j, _):
            make_copy(pl.multiple_of(local0 + j * BIG_ROWS, SEG_ALIGN),
                      pl.multiple_of(global0 + j * BIG_ROWS, SEG_ALIGN), BIG_ROWS).start()
            return 0

        def small(j, _):
            off = n_big * BIG_ROWS + j * SEG_ALIGN
            make_copy(pl.multiple_of(local0 + off, SEG_ALIGN),
                      pl.multiple_of(global0 + off, SEG_ALIGN), SEG_ALIGN).start()
            return 0

        lax.fori_loop(0, n_big, big, 0)
        lax.fori_loop(0, n_small, small, 0)
        return counts[0] + n_big, counts[1] + n_small

    return lax.fori_loop(0, N_EXPERTS, per_expert, (0, 0))


def _wait_copies(counts, make_copy):
    for n, rows in zip(counts, (BIG_ROWS, SEG_ALIGN)):
        def wait_one(i, _, rows=rows):
            make_copy(0, 0, rows).wait()
            return 0
        lax.fori_loop(0, n, wait_one, 0)


def _zero_rows_from(hbm, zeros_vmem, first_row, sem):
    def copy_to(row):
        return pltpu.make_async_copy(zeros_vmem, hbm.at[pl.ds(row, TM)], sem)

    n_full = lax.shift_right_logical(R_TOT - first_row, TM_SHIFT)

    def start_one(j, _):
        copy_to(pl.multiple_of(first_row + j * TM, SEG_ALIGN)).start()
        return 0

    def wait_one(j, _):
        copy_to(0).wait()
        return 0

    lax.fori_loop(0, n_full, start_one, 0)
    lax.fori_loop(0, n_full, wait_one, 0)
    last = copy_to(R_TOT - TM)
    last.start()
    last.wait()


def _dispatch_kernel(loff_s, gst_s, p8_s, tot_s, h_ref, z_ref, loff_ref, ptab_ref, xs_hbm, xbuf, sem, pending):
    b = pl.program_id(0)

    def make_copy(local_row, global_row, rows):
        return pltpu.make_async_copy(xbuf.at[pl.ds(local_row, rows)], xs_hbm.at[pl.ds(global_row, rows)], sem)

    pm = _sort_matrix(b, z_ref, loff_ref, ptab_ref)[0].astype(BF16)

    @pl.when(b > 0)
    def _():
        _wait_copies((pending[0], pending[1]), make_copy)

    xbuf[...] = jnp.dot(pm, h_ref[...], preferred_element_type=F32)
    n_big, n_small = _segment_copies(b, loff_s, gst_s, p8_s, make_copy)
    pending[0] = n_big
    pending[1] = n_small

    @pl.when(b == N_BLK - 1)
    def _():
        _wait_copies((n_big, n_small), make_copy)
        xbuf[0:TM, :] = jnp.zeros((TM, D), F32)
        _zero_rows_from(xs_hbm, xbuf.at[0:TM], tot_s[0], sem)


def _dispatch(h2, z, loff, ptab, loff_i, gst_i, p8_i, tot_i):
    whole = pl.BlockSpec((TAB_ROWS, E_PAD), lambda b, *_: (0, 0))
    return pl.pallas_call(
        _dispatch_kernel,
        grid_spec=pltpu.PrefetchScalarGridSpec(
            num_scalar_prefetch=4,
            grid=(N_BLK,),
            in_specs=[pl.BlockSpec((TBLK, D), lambda b, *_: (b, 0)),
                      pl.BlockSpec((TBLK, E_PAD), lambda b, *_: (b, 0)),
                      whole, whole],
            out_specs=pl.BlockSpec(memory_space=pl.ANY),
            scratch_shapes=[pltpu.VMEM((RB, D), F32), pltpu.SemaphoreType.DMA(()),
                            pltpu.SMEM((2,), jnp.int32)]),
        out_shape=jax.ShapeDtypeStruct((R_TOT, D), F32),
        compiler_params=_cparams(("arbitrary",)),
        name="moe_dispatch",
    )(loff_i, gst_i, p8_i, tot_i, h2, z, loff, ptab)


MAX_TILES = R_TOT // TM + N_EXPERTS
X_SLOTS = 3
N_MATS = 3


def _expert_kernel(eoff_s, xs_hbm, wg_hbm, wu_hbm, wd_hbm, bg_ref, bu_ref, bd_ref, ys_hbm,
                   wst, wbf, xin, yout, w_sem, x_sem, y_sem, t_exp, t_row, t_first, live):
    def add_expert(e, carry):
        n_t, n_live = carry
        start = eoff_s[e]
        tiles = lax.shift_right_logical(eoff_s[e + 1] - start + (TM - 1), TM_SHIFT)

        def add_tile(i, _):
            t_exp[n_t + i] = e
            t_row[n_t + i] = start + i * TM
            t_first[n_t + i] = jnp.where(i == 0, 1, 0)
            return 0

        lax.fori_loop(0, tiles, add_tile, 0)
        has_rows = jnp.where(tiles > 0, 1, 0)

        @pl.when(tiles > 0)
        def _():
            live[n_live] = e

        return n_t + tiles, n_live + has_rows

    n_tiles, n_live = lax.fori_loop(0, N_EXPERTS, add_expert, (0, 0))

    def w_copies(e, slot):
        return [pltpu.make_async_copy(w.at[e], wst.at[slot, m], w_sem.at[slot])
                for m, w in enumerate((wg_hbm, wu_hbm, wd_hbm))]

    def tile_rows(j):
        return pl.ds(pl.multiple_of(t_row[j], SEG_ALIGN), TM)

    def x_copy(j, slot):
        return pltpu.make_async_copy(xs_hbm.at[tile_rows(j)], xin.at[slot], x_sem.at[slot])

    def y_copy(j):
        return pltpu.make_async_copy(yout, ys_hbm.at[tile_rows(j)], y_sem)

    @pl.when(n_tiles > 0)
    def _():
        for cp in w_copies(live[0], 0):
            cp.start()
        x_copy(0, 0).start()

    @pl.when(n_tiles > 1)
    def _():
        x_copy(1, 1).start()

    def body(j, k):
        e = t_exp[j]

        @pl.when(t_first[j] == 1)
        def _():
            ws = lax.rem(k, 2)
            for cp in w_copies(e, ws):
                cp.wait()
            for m in range(N_MATS):
                wbf[m] = wst[ws, m].astype(BF16)

            @pl.when(k + 1 < n_live)
            def _():
                for cp in w_copies(live[k + 1], 1 - ws):
                    cp.start()

        slot = lax.rem(j, X_SLOTS)
        x_copy(j, slot).wait()

        @pl.when(j + 2 < n_tiles)
        def _():
            x_copy(j + 2, lax.rem(j + 2, X_SLOTS)).start()

        x = xin[slot].astype(BF16)
        gate = jnp.dot(x, wbf[0], preferred_element_type=F32) + bg_ref[e]
        up = jnp.dot(x, wbf[1], preferred_element_type=F32) + bu_ref[e]
        gate = jnp.minimum(gate, SWIGLU_LIMIT)
        up = jnp.clip(up, -SWIGLU_LIMIT, SWIGLU_LIMIT)
        act = (up + 1.0) * gate * jax.nn.sigmoid(SWIGLU_ALPHA * gate)
        y = jnp.dot(act.astype(BF16), wbf[2], preferred_element_type=F32) + bd_ref[e]

        @pl.when(j > 0)
        def _():
            y_copy(j - 1).wait()

        yout[...] = y
        y_copy(j).start()
        return k + t_first[j]

    lax.fori_loop(0, n_tiles, body, 0)

    @pl.when(n_tiles > 0)
    def _():
        y_copy(n_tiles - 1).wait()

    yout[...] = jnp.zeros((TM, D), F32)
    _zero_rows_from(ys_hbm, yout, eoff_s[N_EXPERTS], y_sem)


def _experts(xs, eoff_i, w_gate, b_gate, w_up, b_up, w_down, b_down):
    hbm = pl.BlockSpec(memory_space=pl.ANY)
    bspec = pl.BlockSpec((N_EXPERTS, 1, D), lambda i, *_: (0, 0, 0))
    return pl.pallas_call(
        _expert_kernel,
        grid_spec=pltpu.PrefetchScalarGridSpec(
            num_scalar_prefetch=1,
            grid=(1,),
            in_specs=[hbm, hbm, hbm, hbm, bspec, bspec, bspec],
            out_specs=hbm,
            scratch_shapes=[pltpu.VMEM((2, N_MATS, D, D), F32), pltpu.VMEM((N_MATS, D, D), BF16),
                            pltpu.VMEM((X_SLOTS, TM, D), F32), pltpu.VMEM((TM, D), F32),
                            pltpu.SemaphoreType.DMA((2,)), pltpu.SemaphoreType.DMA((X_SLOTS,)),
                            pltpu.SemaphoreType.DMA(()),
                            pltpu.SMEM((MAX_TILES,), jnp.int32), pltpu.SMEM((MAX_TILES,), jnp.int32),
                            pltpu.SMEM((MAX_TILES,), jnp.int32), pltpu.SMEM((N_EXPERTS,), jnp.int32)]),
        out_shape=jax.ShapeDtypeStruct((R_TOT, D), F32),
        compiler_params=_cparams(("arbitrary",)),
        name="moe_experts",
    )(eoff_i, xs, w_gate, w_up, w_down, b_gate.reshape(N_EXPERTS, 1, D), b_up.reshape(N_EXPERTS, 1, D),
      b_down.reshape(N_EXPERTS, 1, D))


def _combine_kernel(loff_s, gst_s, p8_s, ys_hbm, z_ref, comb_ref, loff_ref, ptab_ref, x1_ref, mod_ref, gf_ref,
                    yctx_ref, ylat_ref, ybuf, sem):
    b = pl.program_id(0)

    def make_copy(local_row, global_row, rows):
        return pltpu.make_async_copy(ys_hbm.at[pl.ds(global_row, rows)], ybuf.at[pl.ds(local_row, rows)], sem)

    started = _segment_copies(b, loff_s, gst_s, p8_s, make_copy)

    last = b * N_EXPERTS + N_EXPERTS - 1
    used_chunks = lax.shift_right_logical(loff_s[last], 3) + p8_s[last]

    def zero_chunk(j, _):
        ybuf[pl.ds(pl.multiple_of(j * SEG_ALIGN, SEG_ALIGN), SEG_ALIGN), :] = jnp.zeros((SEG_ALIGN, D), F32)
        return 0

    lax.fori_loop(used_chunks, RB // SEG_ALIGN, zero_chunk, 0)

    pm, owner = _sort_matrix(b, z_ref, loff_ref, ptab_ref)
    w_all = jnp.dot(owner, comb_ref[...].T, preferred_element_type=F32, precision=HIGHEST)
    w_row = jnp.sum(pm * w_all, axis=-1, keepdims=True)
    pmt = pm.T.astype(BF16)
    _wait_copies(started, make_copy)
    y = (ybuf[...] * w_row).astype(BF16)
    moe = jnp.dot(pmt, y, preferred_element_type=F32)
    gate2 = mod_ref[0][:, 5 * D:6 * D]
    x2 = x1_ref[0] + gate2 * moe
    y = _rms(x2, gf_ref[...])

    @pl.when(b < N_CTX_BLK)
    def _():
        yctx_ref[0] = y

    @pl.when(b >= N_CTX_BLK)
    def _():
        ylat_ref[0] = y


def _combine(ys, z, comb, loff, ptab, x1, modv, norm_f_g, loff_i, gst_i, p8_i):
    whole = pl.BlockSpec((TAB_ROWS, E_PAD), lambda b, *_: (0, 0))
    tok = pl.BlockSpec((TBLK, E_PAD), lambda b, *_: (b, 0))
    return pl.pallas_call(
        _combine_kernel,
        grid_spec=pltpu.PrefetchScalarGridSpec(
            num_scalar_prefetch=3,
            grid=(N_BLK,),
            in_specs=[pl.BlockSpec(memory_space=pl.ANY), tok, tok, whole, whole,
                      pl.BlockSpec((1, TBLK, D), lambda b, *_: (b, 0, 0)),
                      pl.BlockSpec((1, 1, 6 * D), lambda b, *_: (b, 0, 0)),
                      pl.BlockSpec((1, D), lambda b, *_: (0, 0))],
            out_specs=[pl.BlockSpec((1, TBLK, D), lambda b, *_: (jnp.minimum(b, N_CTX_BLK - 1), 0, 0)),
                       pl.BlockSpec((1, TBLK, D), lambda b, *_: (jnp.maximum(b - N_CTX_BLK, 0), 0, 0))],
            scratch_shapes=[pltpu.VMEM((RB, D), F32), pltpu.SemaphoreType.DMA(())]),
        out_shape=[jax.ShapeDtypeStruct((N_CTX_BLK, TBLK, D), F32),
                   jax.ShapeDtypeStruct((N_BLK - N_CTX_BLK, TBLK, D), F32)],
        compiler_params=_cparams(("arbitrary",)),
        name="moe_combine",
    )(loff_i, gst_i, p8_i, ys, z, comb, loff, ptab, x1, modv, norm_f_g.reshape(1, D))


def _moe_and_final(x1, h2, comb, modg, norm_f_g, w_gate, b_gate, w_up, b_up, w_down, b_down):
    z, ptab, loff, gst, eoff = _plan(comb)
    as_scalars = lambda t: t[:N_BLK, :N_EXPERTS].astype(jnp.int32).reshape(N_BLK * N_EXPERTS)
    loff_i, gst_i = as_scalars(loff), as_scalars(gst)
    p8_i = as_scalars(ptab) // SEG_ALIGN
    eoff_i = eoff[0, :N_EXPERTS + 1].astype(jnp.int32)
    xs = _dispatch(h2, z, loff, ptab, loff_i, gst_i, p8_i, eoff_i[N_EXPERTS:])
    ys = _experts(xs, eoff_i, w_gate, b_gate, w_up, b_up, w_down, b_down)
    return _combine(ys, z, comb, loff, ptab, x1, modg.reshape(N_BLK, 1, 6 * D), norm_f_g,
                    loff_i, gst_i, p8_i)


def _grid_pos_embed(n_tokens, dim):
    rows = n_tokens // GRID_W
    t = np.arange(rows * GRID_W)
    r = (t // GRID_W).astype(np.float32)
    col = (t % GRID_W).astype(np.float32)
    q = dim // 4
    omega = (1.0 / np.float32(POS_TEMP) ** (np.arange(q, dtype=np.float32) / np.float32(q))).astype(np.float32)

    def emb(p):
        a = p[:, None] * omega[None, :]
        return np.concatenate([np.sin(a), np.cos(a)], axis=-1)

    return np.concatenate([emb(r), emb(col)], axis=-1).astype(np.float32)


def kernel(x_prompt, x_sample, c, state_s5_re, state_s5_im, c_ctx, w_ada, b_ada, norm1_g, w_in, s5_lam_re,
           s5_lam_im, s5_log_dt, s5_b_re, s5_b_im, s5_c_re, s5_c_im, s5_d, s5_w_glu, w_fnet, w_out, norm2_g,
           w_router, b_router, w_gate, b_gate, w_up, b_up, w_down, b_down, norm_f_g):
    n_ctx, n_lat = x_prompt.shape[0], x_sample.shape[0]
    assert x_prompt.shape == (SUB * (N_GRP - 1), L_BLK, D) and x_sample.shape == (2, LAT_CHUNKS * L_BLK, D)
    assert w_ada.shape[0] == 1, "one trunk layer"
    layer = 0

    cvec = jnp.concatenate([jnp.broadcast_to(c_ctx[None], (n_ctx, D)), jnp.repeat(c, LAT_CHUNKS, axis=0)], axis=0)
    modg = _adaln(cvec, w_ada[layer], b_ada[layer]).reshape(N_GRP, SUB, 1, 6 * D)

    xp4 = x_prompt.reshape(N_GRP - 1, SUB, L_BLK, D)
    xs4 = x_sample.reshape(1, SUB, L_BLK, D)
    pos3 = jnp.asarray(_grid_pos_embed(LAT_CHUNKS * L_BLK, D).reshape(LAT_CHUNKS, L_BLK, D))
    perm = _perm_matrix()

    us, uf = _pre(xp4, xs4, pos3, modg, norm1_g[layer], w_in[layer], jnp.asarray(perm, BF16))

    bbmat, ccmat, a8 = _s5_params(s5_lam_re[layer], s5_lam_im[layer], s5_log_dt[layer], s5_b_re[layer],
                                  s5_b_im[layer], s5_c_re[layer], s5_c_im[layer])

    st = jnp.stack([state_s5_re[:, layer], state_s5_im[:, layer]], axis=2)
    st = st.reshape(n_lat, 2, 2, N_S5_CHUNKS, CH_P)
    st = jnp.transpose(st, (1, 3, 0, 2, 4)).reshape(2, N_S5_CHUNKS, n_lat, 2 * CH_P)
    init8 = jnp.zeros((2, N_S5_CHUNKS, n_lat, LAT_CHUNKS, 2 * CH_P), F32)
    init8 = init8.at[0, :, :, 0].set(st[0]).at[1, :, :, LAT_CHUNKS - 1].set(st[1])
    init8 = init8.reshape(2, N_S5_CHUNKS, SUB, 2 * CH_P)

    y_s5, fin = _s5(us, bbmat, ccmat, a8, init8, s5_d[layer])

    m1, m2 = _fnet_weights(w_fnet[layer])
    yf = _fnet(uf, m1, m2)

    x1, h2, comb = _post(y_s5, yf, xp4, xs4, pos3, modg, s5_w_glu[layer], w_out[layer], norm2_g[layer],
                         w_router[layer], b_router[layer], jnp.asarray(perm.T, BF16))

    y_prompt, y_lat = _moe_and_final(x1.reshape(N_BLK, TBLK, D), h2.reshape(T_TOK, D), comb.reshape(T_TOK, E_PAD),
                                     modg, norm_f_g, w_gate[layer], b_gate[layer], w_up[layer], b_up[layer],
                                     w_down[layer], b_down[layer])
    y_sample = y_lat.reshape(n_lat, LAT_CHUNKS * L_BLK, D)

    fin = fin[:, :N_GRP - 1].reshape(2, N_GRP - 1, SUB, N_S5_CHUNKS, 2, CH_P)
    fin = jnp.transpose(fin, (4, 1, 2, 0, 3, 5)).reshape(2, n_ctx, 1, 2, N_S5_GROUPS, S5_P)
    return (y_prompt, y_sample, fin[0], fin[1])
```

```python
import functools
import math

import numpy as np
import jax
import jax.numpy as jnp
from jax import lax
from jax.experimental import pallas as pl
from jax.experimental.pallas import tpu as pltpu

F32 = jnp.float32
BF16 = jnp.bfloat16
HIGHEST = lax.Precision.HIGHEST

D = 1024
D_S5 = 768
S5_H = 16
S5_P = 64
N_S5_GROUPS = 48
D_FN = 256
FN_GW = 64
N_EXPERTS = 32
TOP_K = 4
E_PAD = 128
SWIGLU_LIMIT = 7.0
SWIGLU_ALPHA = 1.702
RMS_EPS = 1e-6
POS_TEMP = 10000.0
GRID_W = 64

L_BLK = 256
SUB = 8
N_GRP = 3
GRP_ROWS = L_BLK * SUB
LAT_CHUNKS = 4
T_TOK = N_GRP * GRP_ROWS
TC = 128
N_TC = L_BLK // TC
PERM_T = 32
S5_CHUNK = 16
N_S5_CHUNKS = N_S5_GROUPS // S5_CHUNK
CH_U = S5_CHUNK * S5_H
CH_P = S5_CHUNK * S5_P
ROW_BLK = 512
FIX_PARTS = 2
VMEM_LIMIT = 56 * 1024 * 1024


def _cparams(sem):
    return pltpu.CompilerParams(dimension_semantics=sem, vmem_limit_bytes=VMEM_LIMIT)


def _adaln_kernel(c_ref, w_ref, b_ref, o_ref):
    c = c_ref[...]
    s = c * jax.nn.sigmoid(c)
    o_ref[...] = jnp.dot(s, w_ref[...], preferred_element_type=F32, precision=HIGHEST) + b_ref[...]


def _adaln(cvec, w_ada, b_ada):
    n = w_ada.shape[1]
    rows = cvec.shape[0]
    return pl.pallas_call(
        _adaln_kernel,
        grid=(n // D,),
        in_specs=[pl.BlockSpec((rows, D), lambda j: (0, 0)),
                  pl.BlockSpec((D, D), lambda j: (0, j)),
                  pl.BlockSpec((1, D), lambda j: (0, j))],
        out_specs=pl.BlockSpec((rows, D), lambda j: (0, j)),
        out_shape=jax.ShapeDtypeStruct((rows, n), F32),
        compiler_params=_cparams(("arbitrary",)),
        name="adaln",
    )(cvec, w_ada, b_ada.reshape(1, n))


def _zoh(lam_re, lam_im, log_dt):
    dt = jnp.exp(log_dt)
    a_re = jnp.minimum(lam_re, -1e-4)
    a_im = lam_im
    mag = jnp.exp(a_re * dt)
    ab_re = mag * jnp.cos(a_im * dt)
    ab_im = mag * jnp.sin(a_im * dt)
    den = a_re * a_re + a_im * a_im
    nr = ab_re - 1.0
    f_re = (nr * a_re + ab_im * a_im) / den
    f_im = (ab_im * a_re - nr * a_im) / den
    return ab_re, ab_im, f_re, f_im


def _s5_params_kernel(lre3_ref, lim3_ref, ldt3_ref, lre2_ref, lim2_ref, ldt2_ref, bre_ref, bim_ref,
                      cre_ref, cim_ref, bb_ref, cc_ref, a_ref):
    exact = functools.partial(jnp.dot, preferred_element_type=F32, precision=HIGHEST)
    spread = jnp.where(lax.broadcasted_iota(jnp.int32, (S5_P, CH_P), 1) % S5_P
                       == lax.broadcasted_iota(jnp.int32, (S5_P, CH_P), 0), 1.0, 0.0)

    def block_diag(m):
        wide = jnp.dot(m.reshape(CH_U, S5_P).astype(BF16), spread.astype(BF16), preferred_element_type=F32)
        row_g = lax.broadcasted_iota(jnp.int32, (CH_U, CH_P), 0) // S5_H
        col_g = lax.broadcasted_iota(jnp.int32, (CH_U, CH_P), 1) // S5_P
        return jnp.where(row_g == col_g, wide, 0.0)

    _, _, f_re, f_im = _zoh(lre3_ref[0, 0], lim3_ref[0, 0], ldt3_ref[0, 0])
    b_re = bre_ref[0, 0]
    b_im = bim_ref[0, 0]
    bb_ref[0, 0] = jnp.concatenate([block_diag(f_re * b_re - f_im * b_im),
                                    block_diag(f_re * b_im + f_im * b_re)], axis=1).astype(BF16)
    cc_ref[0, 0] = jnp.concatenate([block_diag(cre_ref[0, 0]).T,
                                    -block_diag(cim_ref[0, 0]).T], axis=0).astype(BF16)

    ab_re, ab_im, _, _ = _zoh(lre2_ref[0, 0], lim2_ref[0, 0], ldt2_ref[0, 0])
    row_g = lax.broadcasted_iota(jnp.int32, (S5_CHUNK, CH_P), 0)
    col_g = lax.broadcasted_iota(jnp.int32, (S5_CHUNK, CH_P), 1) // S5_P

    def lane_row(a):
        flat = jnp.sum(jnp.where(row_g == col_g, exact(a, spread), 0.0), axis=0, keepdims=True)
        return jnp.broadcast_to(flat, (SUB, CH_P))

    a_ref[0, 0] = jnp.concatenate([lane_row(ab_re), lane_row(ab_im)], axis=1)


def _s5_params(lam_re, lam_im, log_dt, b_re, b_im, c_re, c_im):
    nc = N_S5_CHUNKS
    ldt = jnp.broadcast_to(log_dt[:, :, None], (2, N_S5_GROUPS, S5_P))
    g3 = lambda a: a.reshape(2, nc, S5_CHUNK, 1, S5_P)
    g2 = lambda a: a.reshape(2, nc, S5_CHUNK, S5_P)
    ghp = lambda a: a.reshape(2, nc, S5_CHUNK, S5_H, S5_P)
    spec3 = pl.BlockSpec((1, 1, S5_CHUNK, 1, S5_P), lambda d, c: (d, c, 0, 0, 0))
    spec2 = pl.BlockSpec((1, 1, S5_CHUNK, S5_P), lambda d, c: (d, c, 0, 0))
    spec_ghp = pl.BlockSpec((1, 1, S5_CHUNK, S5_H, S5_P), lambda d, c: (d, c, 0, 0, 0))
    return pl.pallas_call(
        _s5_params_kernel,
        grid=(2, nc),
        in_specs=[spec3, spec3, spec3, spec2, spec2, spec2, spec_ghp, spec_ghp, spec_ghp, spec_ghp],
        out_specs=[pl.BlockSpec((1, 1, CH_U, 2 * CH_P), lambda d, c: (d, c, 0, 0)),
                   pl.BlockSpec((1, 1, 2 * CH_P, CH_U), lambda d, c: (d, c, 0, 0)),
                   pl.BlockSpec((1, 1, SUB, 2 * CH_P), lambda d, c: (d, c, 0, 0))],
        out_shape=[jax.ShapeDtypeStruct((2, nc, CH_U, 2 * CH_P), BF16),
                   jax.ShapeDtypeStruct((2, nc, 2 * CH_P, CH_U), BF16),
                   jax.ShapeDtypeStruct((2, nc, SUB, 2 * CH_P), F32)],
        compiler_params=_cparams(("arbitrary", "arbitrary")),
        name="s5_params",
    )(g3(lam_re), g3(lam_im), g3(ldt), g2(lam_re), g2(lam_im), g2(ldt),
      ghp(jnp.swapaxes(b_re, -1, -2)), ghp(jnp.swapaxes(b_im, -1, -2)), ghp(c_re), ghp(c_im))


def _rms(x, g):
    return x * lax.rsqrt(jnp.mean(x * x, axis=-1, keepdims=True) + RMS_EPS) * g


def _load_x(g, xp_ref, xs_ref, pos_ref):
    is_lat = g == N_GRP - 1
    pos = pos_ref[...]
    pos8 = jnp.concatenate([pos, pos], axis=0)
    return jnp.where(is_lat, xs_ref[0] + pos8, xp_ref[0])


def _x_specs():
    n_ctx = N_GRP - 1
    xp_spec = pl.BlockSpec(
        (1, SUB, TC, D),
        lambda g, tc: (jnp.minimum(g, n_ctx - 1), 0, jnp.where(g >= n_ctx, N_TC - 1, tc), 0))
    xs_spec = pl.BlockSpec((1, SUB, TC, D), lambda g, tc: (0, 0, jnp.where(g >= n_ctx, tc, 0), 0))
    pos_spec = pl.BlockSpec((LAT_CHUNKS, TC, D), lambda g, tc: (0, tc, 0))
    return xp_spec, xs_spec, pos_spec


def _perm_matrix():
    p = np.zeros((SUB * PERM_T, SUB * PERM_T), np.float32)
    for s in range(SUB):
        for j in range(PERM_T):
            p[j * SUB + s, s * PERM_T + j] = 1.0
    return p


def _pre_kernel(xp_ref, xs_ref, pos_ref, mod_ref, g1_ref, win_ref, perm_ref, us_ref, uf_ref):
    g = pl.program_id(0)
    x = _load_x(g, xp_ref, xs_ref, pos_ref)
    mod = mod_ref[0]
    shift1 = mod[:, :, 0:D]
    scale1 = mod[:, :, D:2 * D]
    h = _rms(x, g1_ref[...]) * (1.0 + scale1) + shift1
    h2d = h.reshape(SUB * TC, D).astype(BF16)
    proj = jnp.dot(h2d, win_ref[...].astype(BF16), preferred_element_type=F32)
    uf_ref[0] = proj[:, D_S5:].astype(BF16).reshape(SUB, TC, D_FN)
    u = proj[:, :D_S5].astype(BF16)
    perm = perm_ref[...]
    for q in range(TC // PERM_T):
        piece = jnp.concatenate(
            [u[s * TC + q * PERM_T: s * TC + (q + 1) * PERM_T] for s in range(SUB)], axis=0)
        us_ref[0, q * SUB * PERM_T:(q + 1) * SUB * PERM_T, :] = jnp.dot(
            perm, piece, preferred_element_type=F32).astype(BF16)


def _pre(xp4, xs4, pos3, modg, norm1_g, w_in, perm):
    xp_spec, xs_spec, pos_spec = _x_specs()
    return pl.pallas_call(
        _pre_kernel,
        grid=(N_GRP, N_TC),
        in_specs=[xp_spec, xs_spec, pos_spec,
                  pl.BlockSpec((1, SUB, 1, 6 * D), lambda g, tc: (g, 0, 0, 0)),
                  pl.BlockSpec((1, D), lambda g, tc: (0, 0)),
                  pl.BlockSpec((D, D), lambda g, tc: (0, 0)),
                  pl.BlockSpec((SUB * PERM_T, SUB * PERM_T), lambda g, tc: (0, 0))],
        out_specs=[pl.BlockSpec((1, SUB * TC, D_S5), lambda g, tc: (g, tc, 0)),
                   pl.BlockSpec((1, SUB, TC, D_FN), lambda g, tc: (g, 0, tc, 0))],
        out_shape=[jax.ShapeDtypeStruct((N_GRP, GRP_ROWS, D_S5), BF16),
                   jax.ShapeDtypeStruct((N_GRP, SUB, L_BLK, D_FN), BF16)],
        compiler_params=_cparams(("arbitrary", "arbitrary")),
        name="pre_mixer",
    )(xp4, xs4, pos3, modg, norm1_g.reshape(1, D), w_in, perm)


def _cmul(ar, ai, br, bi):
    return ar * br - ai * bi, ar * bi + ai * br


def _s5_kernel(us_ref, bb_ref, cc_ref, a_ref, init_ref, dskip_ref, y_ref, fin_ref, s_ref):
    g = pl.program_id(0)
    d = pl.program_id(2)
    n_rb = GRP_ROWS // ROW_BLK

    def proj_in(rb, _):
        rows = pl.ds(pl.multiple_of(rb * ROW_BLK, ROW_BLK), ROW_BLK)
        s_ref[rows, :] = jnp.dot(us_ref[0, rows, :], bb_ref[0, 0], preferred_element_type=F32)
        return 0

    lax.fori_loop(0, n_rb, proj_in, 0)

    def tile_row(k):
        t = k + d * (L_BLK - 1 - 2 * k)
        return pl.ds(pl.multiple_of(t * SUB, SUB), SUB)

    is_lat = g == N_GRP - 1
    re_all = slice(0, CH_P)
    im_all = slice(CH_P, 2 * CH_P)
    a_re = a_ref[0, 0, :, re_all]
    a_im = a_ref[0, 0, :, im_all]
    zero = jnp.zeros((SUB, CH_P), F32)
    s0_re = jnp.where(is_lat, init_ref[0, 0, :, re_all], zero)
    s0_im = jnp.where(is_lat, init_ref[0, 0, :, im_all], zero)

    def scan_body(i, carry):
        s_re, s_im = carry
        for j in range(8):
            rows = tile_row(i * 8 + j)
            n_re = a_re * s_re - a_im * s_im + s_ref[rows, re_all]
            n_im = a_re * s_im + a_im * s_re + s_ref[rows, im_all]
            s_ref[rows, re_all] = n_re
            s_ref[rows, im_all] = n_im
            s_re, s_im = n_re, n_im
        return s_re, s_im

    f_re, f_im = lax.fori_loop(0, L_BLK // 8, scan_body, (s0_re, s0_im))
    fin_ref[0, 0, :, re_all] = f_re
    fin_ref[0, 0, :, im_all] = f_im

    @pl.when(is_lat)
    def _():
        part = CH_P // FIX_PARTS
        sub_id = lax.broadcasted_iota(jnp.int32, (SUB, part), 0) % LAT_CHUNKS
        fwd = d == 0
        lo = jnp.where(fwd, 1, 0)
        hi = jnp.where(fwd, LAT_CHUNKS - 1, LAT_CHUNKS - 2)
        keep = (sub_id >= lo) & (sub_id <= hi)

        def from_prev(v):
            return jnp.where(keep, jnp.where(fwd, pltpu.roll(v, 1, 0), pltpu.roll(v, SUB - 1, 0)), 0.0)

        for hh in range(FIX_PARTS):
            lanes = slice(hh * part, (hh + 1) * part)
            re_cols = lanes
            im_cols = slice(CH_P + hh * part, CH_P + (hh + 1) * part)
            pa_re, pa_im = a_re[:, lanes], a_im[:, lanes]
            pf_re, pf_im = f_re[:, lanes], f_im[:, lanes]
            p_re, p_im = pa_re, pa_im
            for _ in range(8):
                p_re, p_im = _cmul(p_re, p_im, p_re, p_im)
            t_re, t_im = pf_re, pf_im
            for _ in range(LAT_CHUNKS - 2):
                m_re, m_im = _cmul(p_re, p_im, from_prev(t_re), from_prev(t_im))
                t_re, t_im = pf_re + m_re, pf_im + m_im
            c_re, c_im = from_prev(t_re), from_prev(t_im)

            def fix_body(i, carry, c_re=c_re, c_im=c_im, pa_re=pa_re, pa_im=pa_im, re_cols=re_cols,
                         im_cols=im_cols):
                q_re, q_im = carry
                for j in range(8):
                    rows = tile_row(i * 8 + j)
                    m_re, m_im = _cmul(q_re, q_im, c_re, c_im)
                    s_ref[rows, re_cols] = s_ref[rows, re_cols] + m_re
                    s_ref[rows, im_cols] = s_ref[rows, im_cols] + m_im
                    q_re, q_im = _cmul(q_re, q_im, pa_re, pa_im)
                return q_re, q_im

            lax.fori_loop(0, L_BLK // 8, fix_body, (pa_re, pa_im))

    def proj_out(rb):
        rows = pl.ds(pl.multiple_of(rb * ROW_BLK, ROW_BLK), ROW_BLK)
        return rows, jnp.dot(s_ref[rows, :].astype(BF16), cc_ref[0, 0], preferred_element_type=F32)

    @pl.when(d == 0)
    def _():
        def body(rb, _):
            rows, yb = proj_out(rb)
            y_ref[0, rows, :] = dskip_ref[...] * us_ref[0, rows, :].astype(F32) + yb
            return 0
        lax.fori_loop(0, n_rb, body, 0)

    @pl.when(d == 1)
    def _():
        def body(rb, _):
            rows, yb = proj_out(rb)
            y_ref[0, rows, :] = y_ref[0, rows, :] + yb
            return 0
        lax.fori_loop(0, n_rb, body, 0)


def _s5(us, bbmat, ccmat, a8, init8, d_skip):
    return pl.pallas_call(
        _s5_kernel,
        grid=(N_GRP, N_S5_CHUNKS, 2),
        in_specs=[pl.BlockSpec((1, GRP_ROWS, CH_U), lambda g, c, d: (g, 0, c)),
                  pl.BlockSpec((1, 1, CH_U, 2 * CH_P), lambda g, c, d: (d, c, 0, 0)),
                  pl.BlockSpec((1, 1, 2 * CH_P, CH_U), lambda g, c, d: (d, c, 0, 0)),
                  pl.BlockSpec((1, 1, SUB, 2 * CH_P), lambda g, c, d: (d, c, 0, 0)),
                  pl.BlockSpec((1, 1, SUB, 2 * CH_P), lambda g, c, d: (d, c, 0, 0)),
                  pl.BlockSpec((1, CH_U), lambda g, c, d: (0, c))],
        out_specs=[pl.BlockSpec((1, GRP_ROWS, CH_U), lambda g, c, d: (g, 0, c)),
                   pl.BlockSpec((1, 1, SUB, 2 * CH_P), lambda g, c, d: (d, g, 0, c))],
        out_shape=[jax.ShapeDtypeStruct((N_GRP, GRP_ROWS, D_S5), F32),
                   jax.ShapeDtypeStruct((2, N_GRP, SUB, N_S5_CHUNKS * 2 * CH_P), F32)],
        scratch_shapes=[pltpu.VMEM((GRP_ROWS, 2 * CH_P), F32)],
        compiler_params=_cparams(("arbitrary", "arbitrary", "arbitrary")),
        name="s5_scan",
    )(us, bbmat, ccmat, a8, init8, d_skip.reshape(1, D_S5))


def _fnet_w_kernel(c_ref, s_ref, w_ref, m1_ref, m2_ref):
    w = w_ref[...]
    m1_ref[...] = jnp.dot(c_ref[...], w, preferred_element_type=F32, precision=HIGHEST).astype(BF16)
    m2_ref[...] = jnp.dot(s_ref[...], w, preferred_element_type=F32, precision=HIGHEST).astype(BF16)


FN_SEQ = LAT_CHUNKS
FN_CTX_STEPS = (N_GRP - 1) * SUB // FN_SEQ


def _fnet_kernel(u_ref, cos_s_ref, sin_s_ref, cos_l_ref, sin_l_ref, m1_ref, m2_ref, o_ref):
    i = pl.program_id(0)
    m1 = m1_ref[...]
    m2 = m2_ref[...]

    def mix(u, cos_ref, sin_ref):
        v1 = jnp.dot(u, m1, preferred_element_type=F32).astype(BF16)
        v2 = jnp.dot(u, m2, preferred_element_type=F32).astype(BF16)
        return (jnp.dot(cos_ref[...].astype(BF16), v1, preferred_element_type=F32)
                - jnp.dot(sin_ref[...].astype(BF16), v2, preferred_element_type=F32)).astype(BF16)

    @pl.when(i < FN_CTX_STEPS)
    def _():
        for s in range(FN_SEQ):
            o_ref[0, s] = mix(u_ref[0, s], cos_s_ref, sin_s_ref)

    @pl.when(i >= FN_CTX_STEPS)
    def _():
        u = u_ref[0].reshape(FN_SEQ * L_BLK, D_FN)
        o_ref[0] = mix(u, cos_l_ref, sin_l_ref).reshape(FN_SEQ, L_BLK, D_FN)


def _dft_tables(n):
    k = np.arange(n, dtype=np.int64)
    ang = (2.0 * np.pi / n) * ((k[:, None] * k[None, :]) % n).astype(np.float64)
    scale = 1.0 / math.sqrt(n)
    return (np.cos(ang) * scale).astype(np.float32), (np.sin(ang) * scale).astype(np.float32)


def _fnet(uf, m1, m2):
    n_lat_steps = SUB // FN_SEQ
    cos_s, sin_s = _dft_tables(L_BLK)
    cos_l, sin_l = _dft_tables(FN_SEQ * L_BLK)
    full = lambda i: (0, 0)
    per_grp = SUB // FN_SEQ
    blk = pl.BlockSpec(
        (1, FN_SEQ, L_BLK, D_FN),
        lambda i: (jnp.minimum(i // per_grp, N_GRP - 1),
                   jnp.where(i < FN_CTX_STEPS, i % per_grp, i - FN_CTX_STEPS), 0, 0))
    table = lambda n: pl.BlockSpec((n, n), full)
    return pl.pallas_call(
        _fnet_kernel,
        grid=(FN_CTX_STEPS + n_lat_steps,),
        in_specs=[blk, table(L_BLK), table(L_BLK), table(FN_SEQ * L_BLK), table(FN_SEQ * L_BLK),
                  table(D_FN), table(D_FN)],
        out_specs=blk,
        out_shape=jax.ShapeDtypeStruct(uf.shape, BF16),
        compiler_params=_cparams(("arbitrary",)),
        name="fnet",
    )(uf, jnp.asarray(cos_s), jnp.asarray(sin_s), jnp.asarray(cos_l), jnp.asarray(sin_l), m1, m2)


def _fnet_weights(w_fnet):
    n_g = D_FN // FN_GW
    cos_c, sin_c = _dft_tables(FN_GW)
    eye = np.eye(n_g, dtype=np.float32)
    cos_bd = np.kron(eye, cos_c)
    sin_bd = np.kron(eye, sin_c)
    w_bd = (w_fnet[:, :, None, :] * jnp.asarray(eye)[:, None, :, None]).reshape(D_FN, D_FN)
    out = jax.ShapeDtypeStruct((D_FN, D_FN), BF16)
    return pl.pallas_call(_fnet_w_kernel, out_shape=(out, out), name="fnet_weights")(
        jnp.asarray(cos_bd), jnp.asarray(sin_bd), w_bd)


def _gelu_tanh(x):
    return 0.5 * x * (1.0 + jnp.tanh(math.sqrt(2.0 / math.pi) * (x + 0.044715 * (x * x * x))))


def _post_kernel(y_ref, yf_ref, xp_ref, xs_ref, pos_ref, mod_ref, wglu_ref, wout_ref, g2_ref, wr_ref,
                 br_ref, permt_ref, x1_ref, h2_ref, comb_ref):
    g = pl.program_id(0)
    z = _gelu_tanh(y_ref[0])
    gate = jnp.dot(z.astype(BF16), wglu_ref[...].astype(BF16), preferred_element_type=F32)
    gl = (z * jax.nn.sigmoid(gate)).astype(BF16)
    permt = permt_ref[...]
    n_q = TC // PERM_T
    nat = [jnp.dot(permt, gl[q * SUB * PERM_T:(q + 1) * SUB * PERM_T], preferred_element_type=F32).astype(BF16)
           for q in range(n_q)]
    gl_nat = jnp.concatenate(
        [nat[q][s * PERM_T:(s + 1) * PERM_T] for s in range(SUB) for q in range(n_q)], axis=0)
    w_out = wout_ref[...].astype(BF16)
    mixed = (jnp.dot(gl_nat, w_out[:D_S5], preferred_element_type=F32)
             + jnp.dot(yf_ref[0].reshape(SUB * TC, D_FN), w_out[D_S5:], preferred_element_type=F32))
    x = _load_x(g, xp_ref, xs_ref, pos_ref)
    mod = mod_ref[0]
    gate1 = mod[:, :, 2 * D:3 * D]
    shift2 = mod[:, :, 3 * D:4 * D]
    scale2 = mod[:, :, 4 * D:5 * D]
    x1 = x + gate1 * mixed.reshape(SUB, TC, D)
    x1_ref[0] = x1
    h2 = _rms(x1, g2_ref[...]) * (1.0 + scale2) + shift2
    h2_ref[0] = h2.astype(BF16)

    hr = h2.reshape(SUB * TC, D)
    h_hi = hr.astype(BF16)
    h_lo = (hr - h_hi.astype(F32)).astype(BF16)
    wr = wr_ref[...]
    w_hi = wr.astype(BF16)
    w_lo = (wr - w_hi.astype(F32)).astype(BF16)
    logits = (jnp.dot(h_hi, w_hi, preferred_element_type=F32) + jnp.dot(h_lo, w_hi, preferred_element_type=F32)
              + jnp.dot(h_hi, w_lo, preferred_element_type=F32) + br_ref[...])
    lane = lax.broadcasted_iota(jnp.int32, logits.shape, 1).astype(F32)
    top_v, hots = [], []
    cur = logits
    for _ in range(TOP_K):
        m = jnp.max(cur, axis=-1, keepdims=True)
        idx = jnp.min(jnp.where(cur == m, lane, float(E_PAD)), axis=-1, keepdims=True)
        hot = lane == idx
        top_v.append(m)
        hots.append(hot)
        cur = jnp.where(hot, -3.0e38, cur)
    exps = [jnp.exp(v - top_v[0]) for v in top_v]
    denom = exps[0] + exps[1] + exps[2] + exps[3]
    comb = jnp.zeros(logits.shape, F32)
    for k in range(TOP_K):
        comb = comb + jnp.where(hots[k], exps[k] / denom, 0.0)
    comb_ref[0] = comb.reshape(SUB, TC, E_PAD)


def _post(y, yf, xp4, xs4, pos3, modg, w_glu, w_out, norm2_g, w_router, b_router, permt):
    xp_spec, xs_spec, pos_spec = _x_specs()
    const2 = lambda g, tc: (0, 0)
    wr = jnp.zeros((D, E_PAD), F32).at[:, :N_EXPERTS].set(w_router)
    br = jnp.full((1, E_PAD), -1.0e30, F32).at[0, :N_EXPERTS].set(b_router)
    blk = lambda w: pl.BlockSpec((1, SUB, TC, w), lambda g, tc: (g, 0, tc, 0))
    return pl.pallas_call(
        _post_kernel,
        grid=(N_GRP, N_TC),
        in_specs=[pl.BlockSpec((1, SUB * TC, D_S5), lambda g, tc: (g, tc, 0)),
                  blk(D_FN), xp_spec, xs_spec, pos_spec,
                  pl.BlockSpec((1, SUB, 1, 6 * D), lambda g, tc: (g, 0, 0, 0)),
                  pl.BlockSpec((D_S5, D_S5), const2),
                  pl.BlockSpec((D, D), const2),
                  pl.BlockSpec((1, D), const2),
                  pl.BlockSpec((D, E_PAD), const2),
                  pl.BlockSpec((1, E_PAD), const2),
                  pl.BlockSpec((SUB * PERM_T, SUB * PERM_T), const2)],
        out_specs=[blk(D), blk(D), blk(E_PAD)],
        out_shape=[jax.ShapeDtypeStruct((N_GRP, SUB, L_BLK, D), F32),
                   jax.ShapeDtypeStruct((N_GRP, SUB, L_BLK, D), BF16),
                   jax.ShapeDtypeStruct((N_GRP, SUB, L_BLK, E_PAD), F32)],
        compiler_params=_cparams(("arbitrary", "arbitrary")),
        name="post_mixer",
    )(y, yf, xp4, xs4, pos3, modg, w_glu, w_out, norm2_g.reshape(1, D), wr, br, permt)


TBLK = L_BLK
N_BLK = T_TOK // TBLK
N_CTX_BLK = (N_GRP - 1) * SUB
SEG_ALIGN = 8
RB = 1280
assert RB >= TBLK * TOP_K + N_EXPERTS * (SEG_ALIGN - 1) and RB % 128 == 0
TM = 256
TM_SHIFT = 8
assert 1 << TM_SHIFT == TM
R_TOT = T_TOK * TOP_K + N_BLK * N_EXPERTS * (SEG_ALIGN - 1) + TM
TAB_ROWS = 32
assert TAB_ROWS >= N_BLK
BIG_ROWS = 32
BIG_SHIFT = 2
assert SEG_ALIGN << BIG_SHIFT == BIG_ROWS


def _plan_kernel(comb_ref, z_ref, ptab_ref, loff_ref, gst_ref, eoff_ref):
    row = lax.broadcasted_iota(jnp.int32, (TBLK, TBLK), 0)
    col = lax.broadcasted_iota(jnp.int32, (TBLK, TBLK), 1)
    earlier = jnp.where(row > col, 1.0, 0.0).astype(BF16)
    ptab_ref[...] = jnp.zeros_like(ptab_ref)

    def body(b, _):
        rows = pl.ds(pl.multiple_of(b * TBLK, TBLK), TBLK)
        m = jnp.where(comb_ref[rows, :] > 0.0, 1.0, 0.0)
        rank = jnp.dot(earlier, m.astype(BF16), preferred_element_type=F32)
        z_ref[rows, :] = m * (rank + 1.0)
        n = jnp.sum(m, axis=0, keepdims=True)
        ptab_ref[pl.ds(b, 1), :] = jnp.floor((n + (SEG_ALIGN - 1)) * (1.0 / SEG_ALIGN)) * SEG_ALIGN
        return 0

    lax.fori_loop(0, N_BLK, body, 0)

    ptab = ptab_ref[...]
    er = lax.broadcasted_iota(jnp.int32, (E_PAD, E_PAD), 0)
    ec = lax.broadcasted_iota(jnp.int32, (E_PAD, E_PAD), 1)
    before = jnp.where(er < ec, 1.0, 0.0)
    exact = functools.partial(jnp.dot, preferred_element_type=F32, precision=HIGHEST)
    loff_ref[...] = exact(ptab, before)
    tot = jnp.sum(ptab, axis=0, keepdims=True)
    eoff = exact(jnp.broadcast_to(tot, (SUB, E_PAD)), before)
    eoff_ref[...] = eoff
    br = lax.broadcasted_iota(jnp.int32, (TAB_ROWS, TAB_ROWS), 0)
    bc = lax.broadcasted_iota(jnp.int32, (TAB_ROWS, TAB_ROWS), 1)
    gst_ref[...] = eoff[0:1] + exact(jnp.where(br > bc, 1.0, 0.0), ptab)


def _plan(comb):
    tab = jax.ShapeDtypeStruct((TAB_ROWS, E_PAD), F32)
    return pl.pallas_call(
        _plan_kernel,
        out_shape=(jax.ShapeDtypeStruct((T_TOK, E_PAD), F32), tab, tab, tab,
                   jax.ShapeDtypeStruct((SUB, E_PAD), F32)),
        compiler_params=pltpu.CompilerParams(vmem_limit_bytes=VMEM_LIMIT),
        name="moe_plan",
    )(comb)


def _sort_matrix(b, z_ref, loff_ref, ptab_ref):
    loff = loff_ref[pl.ds(b, 1), :]
    size = ptab_ref[pl.ds(b, 1), :]
    r = lax.broadcasted_iota(jnp.int32, (RB, E_PAD), 0).astype(F32)
    owner = jnp.where(r >= loff, jnp.where(r < loff + size, 1.0, 0.0), 0.0)
    rank1 = r[:, 0:1] - jnp.sum(owner * loff, axis=-1, keepdims=True) + 1.0
    zt = z_ref[...].T
    v = jnp.dot(owner.astype(BF16), zt.astype(BF16), preferred_element_type=F32)
    return jnp.where(v == rank1, 1.0, 0.0), owner


def _segment_copies(b, loff_s, gst_s, p8_s, make_copy):
    def per_expert(e, counts):
        k = b * N_EXPERTS + e
        chunks = p8_s[k]
        n_big = lax.shift_right_logical(chunks, BIG_SHIFT)
        n_small = chunks - n_big * (BIG_ROWS // SEG_ALIGN)
        local0 = loff_s[k]
        global0 = gst_s[k]

        def big(j, _):
            make_copy(pl.multiple_of(local0 + j * BIG_ROWS, SEG_ALIGN),
                      pl.multiple_of(global0 + j * BIG_ROWS, SEG_ALIGN), BIG_ROWS).start()
            return 0

        def small(j, _):
            off = n_big * BIG_ROWS + j * SEG_ALIGN
            make_copy(pl.multiple_of(local0 + off, SEG_ALIGN),
                      pl.multiple_of(global0 + off, SEG_ALIGN), SEG_ALIGN).start()
            return 0

        lax.fori_loop(0, n_big, big, 0)
        lax.fori_loop(0, n_small, small, 0)
        return counts[0] + n_big, counts[1] + n_small

    return lax.fori_loop(0, N_EXPERTS, per_expert, (0, 0))


def _wait_copies(counts, make_copy):
    for n, rows in zip(counts, (BIG_ROWS, SEG_ALIGN)):
        def wait_one(i, _, rows=rows):
            make_copy(0, 0, rows).wait()
            return 0
        lax.fori_loop(0, n, wait_one, 0)


def _zero_rows_from(hbm, zeros_vmem, first_row, sem):
    def copy_to(row):
        return pltpu.make_async_copy(zeros_vmem, hbm.at[pl.ds(row, TM)], sem)

    n_full = lax.shift_right_logical(R_TOT - first_row, TM_SHIFT)

    def start_one(j, _):
        copy_to(pl.multiple_of(first_row + j * TM, SEG_ALIGN)).start()
        return 0

    def wait_one(j, _):
        copy_to(0).wait()
        return 0

    lax.fori_loop(0, n_full, start_one, 0)
    lax.fori_loop(0, n_full, wait_one, 0)
    last = copy_to(R_TOT - TM)
    last.start()
    last.wait()


def _dispatch_kernel(loff_s, gst_s, p8_s, tot_s, h_ref, z_ref, loff_ref, ptab_ref, xs_hbm, xbuf, sem, pending):
    b = pl.program_id(0)

    def make_copy(local_row, global_row, rows):
        return pltpu.make_async_copy(xbuf.at[pl.ds(local_row, rows)], xs_hbm.at[pl.ds(global_row, rows)], sem)

    pm = _sort_matrix(b, z_ref, loff_ref, ptab_ref)[0].astype(BF16)

    @pl.when(b > 0)
    def _():
        _wait_copies((pending[0], pending[1]), make_copy)

    xbuf[...] = jnp.dot(pm, h_ref[...], preferred_element_type=F32)
    n_big, n_small = _segment_copies(b, loff_s, gst_s, p8_s, make_copy)
    pending[0] = n_big
    pending[1] = n_small

    @pl.when(b == N_BLK - 1)
    def _():
        _wait_copies((n_big, n_small), make_copy)
        xbuf[0:TM, :] = jnp.zeros((TM, D), F32)
        _zero_rows_from(xs_hbm, xbuf.at[0:TM], tot_s[0], sem)


def _dispatch(h2, z, loff, ptab, loff_i, gst_i, p8_i, tot_i):
    whole = pl.BlockSpec((TAB_ROWS, E_PAD), lambda b, *_: (0, 0))
    return pl.pallas_call(
        _dispatch_kernel,
        grid_spec=pltpu.PrefetchScalarGridSpec(
            num_scalar_prefetch=4,
            grid=(N_BLK,),
            in_specs=[pl.BlockSpec((TBLK, D), lambda b, *_: (b, 0)),
                      pl.BlockSpec((TBLK, E_PAD), lambda b, *_: (b, 0)),
                      whole, whole],
            out_specs=pl.BlockSpec(memory_space=pl.ANY),
            scratch_shapes=[pltpu.VMEM((RB, D), F32), pltpu.SemaphoreType.DMA(()),
                            pltpu.SMEM((2,), jnp.int32)]),
        out_shape=jax.ShapeDtypeStruct((R_TOT, D), F32),
        compiler_params=_cparams(("arbitrary",)),
        name="moe_dispatch",
    )(loff_i, gst_i, p8_i, tot_i, h2, z, loff, ptab)


MAX_TILES = R_TOT // TM + N_EXPERTS
X_SLOTS = 3
N_MATS = 3


def _expert_kernel(eoff_s, xs_hbm, wg_hbm, wu_hbm, wd_hbm, bg_ref, bu_ref, bd_ref, ys_hbm,
                   wst, wbf, xin, yout, w_sem, x_sem, y_sem, t_exp, t_row, t_first, live):
    def add_expert(e, carry):
        n_t, n_live = carry
        start = eoff_s[e]
        tiles = lax.shift_right_logical(eoff_s[e + 1] - start + (TM - 1), TM_SHIFT)

        def add_tile(i, _):
            t_exp[n_t + i] = e
            t_row[n_t + i] = start + i * TM
            t_first[n_t + i] = jnp.where(i == 0, 1, 0)
            return 0

        lax.fori_loop(0, tiles, add_tile, 0)
        has_rows = jnp.where(tiles > 0, 1, 0)

        @pl.when(tiles > 0)
        def _():
            live[n_live] = e

        return n_t + tiles, n_live + has_rows

    n_tiles, n_live = lax.fori_loop(0, N_EXPERTS, add_expert, (0, 0))

    def w_copies(e, slot):
        return [pltpu.make_async_copy(w.at[e], wst.at[slot, m], w_sem.at[slot])
                for m, w in enumerate((wg_hbm, wu_hbm, wd_hbm))]

    def tile_rows(j):
        return pl.ds(pl.multiple_of(t_row[j], SEG_ALIGN), TM)

    def x_copy(j, slot):
        return pltpu.make_async_copy(xs_hbm.at[tile_rows(j)], xin.at[slot], x_sem.at[slot])

    def y_copy(j):
        return pltpu.make_async_copy(yout, ys_hbm.at[tile_rows(j)], y_sem)

    @pl.when(n_tiles > 0)
    def _():
        for cp in w_copies(live[0], 0):
            cp.start()
        x_copy(0, 0).start()

    @pl.when(n_tiles > 1)
    def _():
        x_copy(1, 1).start()

    def body(j, k):
        e = t_exp[j]

        @pl.when(t_first[j] == 1)
        def _():
            ws = lax.rem(k, 2)
            for cp in w_copies(e, ws):
                cp.wait()
            for m in range(N_MATS):
                wbf[m] = wst[ws, m].astype(BF16)

            @pl.when(k + 1 < n_live)
            def _():
                for cp in w_copies(live[k + 1], 1 - ws):
                    cp.start()

        slot = lax.rem(j, X_SLOTS)
        x_copy(j, slot).wait()

        @pl.when(j + 2 < n_tiles)
        def _():
            x_copy(j + 2, lax.rem(j + 2, X_SLOTS)).start()

        x = xin[slot].astype(BF16)
        gate = jnp.dot(x, wbf[0], preferred_element_type=F32) + bg_ref[e]
        up = jnp.dot(x, wbf[1], preferred_element_type=F32) + bu_ref[e]
        gate = jnp.minimum(gate, SWIGLU_LIMIT)
        up = jnp.clip(up, -SWIGLU_LIMIT, SWIGLU_LIMIT)
        act = (up + 1.0) * gate * jax.nn.sigmoid(SWIGLU_ALPHA * gate)
        y = jnp.dot(act.astype(BF16), wbf[2], preferred_element_type=F32) + bd_ref[e]

        @pl.when(j > 0)
        def _():
            y_copy(j - 1).wait()

        yout[...] = y
        y_copy(j).start()
        return k + t_first[j]

    lax.fori_loop(0, n_tiles, body, 0)

    @pl.when(n_tiles > 0)
    def _():
        y_copy(n_tiles - 1).wait()

    yout[...] = jnp.zeros((TM, D), F32)
    _zero_rows_from(ys_hbm, yout, eoff_s[N_EXPERTS], y_sem)


def _experts(xs, eoff_i, w_gate, b_gate, w_up, b_up, w_down, b_down):
    hbm = pl.BlockSpec(memory_space=pl.ANY)
    bspec = pl.BlockSpec((N_EXPERTS, 1, D), lambda i, *_: (0, 0, 0))
    return pl.pallas_call(
        _expert_kernel,
        grid_spec=pltpu.PrefetchScalarGridSpec(
            num_scalar_prefetch=1,
            grid=(1,),
            in_specs=[hbm, hbm, hbm, hbm, bspec, bspec, bspec],
            out_specs=hbm,
            scratch_shapes=[pltpu.VMEM((2, N_MATS, D, D), F32), pltpu.VMEM((N_MATS, D, D), BF16),
                            pltpu.VMEM((X_SLOTS, TM, D), F32), pltpu.VMEM((TM, D), F32),
                            pltpu.SemaphoreType.DMA((2,)), pltpu.SemaphoreType.DMA((X_SLOTS,)),
                            pltpu.SemaphoreType.DMA(()),
                            pltpu.SMEM((MAX_TILES,), jnp.int32), pltpu.SMEM((MAX_TILES,), jnp.int32),
                            pltpu.SMEM((MAX_TILES,), jnp.int32), pltpu.SMEM((N_EXPERTS,), jnp.int32)]),
        out_shape=jax.ShapeDtypeStruct((R_TOT, D), F32),
        compiler_params=_cparams(("arbitrary",)),
        name="moe_experts",
    )(eoff_i, xs, w_gate, w_up, w_down, b_gate.reshape(N_EXPERTS, 1, D), b_up.reshape(N_EXPERTS, 1, D),
      b_down.reshape(N_EXPERTS, 1, D))


def _combine_kernel(loff_s, gst_s, p8_s, ys_hbm, z_ref, comb_ref, loff_ref, ptab_ref, x1_ref, mod_ref, gf_ref,
                    yctx_ref, ylat_ref, ybuf, sem):
    b = pl.program_id(0)

    def make_copy(local_row, global_row, rows):
        return pltpu.make_async_copy(ys_hbm.at[pl.ds(global_row, rows)], ybuf.at[pl.ds(local_row, rows)], sem)

    started = _segment_copies(b, loff_s, gst_s, p8_s, make_copy)

    last = b * N_EXPERTS + N_EXPERTS - 1
    used_chunks = lax.shift_right_logical(loff_s[last], 3) + p8_s[last]

    def zero_chunk(j, _):
        ybuf[pl.ds(pl.multiple_of(j * SEG_ALIGN, SEG_ALIGN), SEG_ALIGN), :] = jnp.zeros((SEG_ALIGN, D), F32)
        return 0

    lax.fori_loop(used_chunks, RB // SEG_ALIGN, zero_chunk, 0)

    pm, owner = _sort_matrix(b, z_ref, loff_ref, ptab_ref)
    w_all = jnp.dot(owner, comb_ref[...].T, preferred_element_type=F32, precision=HIGHEST)
    w_row = jnp.sum(pm * w_all, axis=-1, keepdims=True)
    pmt = pm.T.astype(BF16)
    _wait_copies(started, make_copy)
    y = (ybuf[...] * w_row).astype(BF16)
    moe = jnp.dot(pmt, y, preferred_element_type=F32)
    gate2 = mod_ref[0][:, 5 * D:6 * D]
    x2 = x1_ref[0] + gate2 * moe
    y = _rms(x2, gf_ref[...])

    @pl.when(b < N_CTX_BLK)
    def _():
        yctx_ref[0] = y

    @pl.when(b >= N_CTX_BLK)
    def _():
        ylat_ref[0] = y


def _combine(ys, z, comb, loff, ptab, x1, modv, norm_f_g, loff_i, gst_i, p8_i):
    whole = pl.BlockSpec((TAB_ROWS, E_PAD), lambda b, *_: (0, 0))
    tok = pl.BlockSpec((TBLK, E_PAD), lambda b, *_: (b, 0))
    return pl.pallas_call(
        _combine_kernel,
        grid_spec=pltpu.PrefetchScalarGridSpec(
            num_scalar_prefetch=3,
            grid=(N_BLK,),
            in_specs=[pl.BlockSpec(memory_space=pl.ANY), tok, tok, whole, whole,
                      pl.BlockSpec((1, TBLK, D), lambda b, *_: (b, 0, 0)),
                      pl.BlockSpec((1, 1, 6 * D), lambda b, *_: (b, 0, 0)),
                      pl.BlockSpec((1, D), lambda b, *_: (0, 0))],
            out_specs=[pl.BlockSpec((1, TBLK, D), lambda b, *_: (jnp.minimum(b, N_CTX_BLK - 1), 0, 0)),
                       pl.BlockSpec((1, TBLK, D), lambda b, *_: (jnp.maximum(b - N_CTX_BLK, 0), 0, 0))],
            scratch_shapes=[pltpu.VMEM((RB, D), F32), pltpu.SemaphoreType.DMA(())]),
        out_shape=[jax.ShapeDtypeStruct((N_CTX_BLK, TBLK, D), F32),
                   jax.ShapeDtypeStruct((N_BLK - N_CTX_BLK, TBLK, D), F32)],
        compiler_params=_cparams(("arbitrary",)),
        name="moe_combine",
    )(loff_i, gst_i, p8_i, ys, z, comb, loff, ptab, x1, modv, norm_f_g.reshape(1, D))


def _moe_and_final(x1, h2, comb, modg, norm_f_g, w_gate, b_gate, w_up, b_up, w_down, b_down):
    z, ptab, loff, gst, eoff = _plan(comb)
    as_scalars = lambda t: t[:N_BLK, :N_EXPERTS].astype(jnp.int32).reshape(N_BLK * N_EXPERTS)
    loff_i, gst_i = as_scalars(loff), as_scalars(gst)
    p8_i = as_scalars(ptab) // SEG_ALIGN
    eoff_i = eoff[0, :N_EXPERTS + 1].astype(jnp.int32)
    xs = _dispatch(h2, z, loff, ptab, loff_i, gst_i, p8_i, eoff_i[N_EXPERTS:])
    ys = _experts(xs, eoff_i, w_gate, b_gate, w_up, b_up, w_down, b_down)
    return _combine(ys, z, comb, loff, ptab, x1, modg.reshape(N_BLK, 1, 6 * D), norm_f_g,
                    loff_i, gst_i, p8_i)


def _grid_pos_embed(n_tokens, dim):
    rows = n_tokens // GRID_W
    t = np.arange(rows * GRID_W)
    r = (t // GRID_W).astype(np.float32)
    col = (t % GRID_W).astype(np.float32)
    q = dim // 4
    omega = (1.0 / np.float32(POS_TEMP) ** (np.arange(q, dtype=np.float32) / np.float32(q))).astype(np.float32)

    def emb(p):
        a = p[:, None] * omega[None, :]
        return np.concatenate([np.sin(a), np.cos(a)], axis=-1)

    return np.concatenate([emb(r), emb(col)], axis=-1).astype(np.float32)


def kernel(x_prompt, x_sample, c, state_s5_re, state_s5_im, c_ctx, w_ada, b_ada, norm1_g, w_in, s5_lam_re,
           s5_lam_im, s5_log_dt, s5_b_re, s5_b_im, s5_c_re, s5_c_im, s5_d, s5_w_glu, w_fnet, w_out, norm2_g,
           w_router, b_router, w_gate, b_gate, w_up, b_up, w_down, b_down, norm_f_g):
    n_ctx, n_lat = x_prompt.shape[0], x_sample.shape[0]
    assert x_prompt.shape == (SUB * (N_GRP - 1), L_BLK, D) and x_sample.shape == (2, LAT_CHUNKS * L_BLK, D)
    assert w_ada.shape[0] == 1, "one trunk layer"
    layer = 0

    cvec = jnp.concatenate([jnp.broadcast_to(c_ctx[None], (n_ctx, D)), jnp.repeat(c, LAT_CHUNKS, axis=0)], axis=0)
    modg = _adaln(cvec, w_ada[layer], b_ada[layer]).reshape(N_GRP, SUB, 1, 6 * D)

    xp4 = x_prompt.reshape(N_GRP - 1, SUB, L_BLK, D)
    xs4 = x_sample.reshape(1, SUB, L_BLK, D)
    pos3 = jnp.asarray(_grid_pos_embed(LAT_CHUNKS * L_BLK, D).reshape(LAT_CHUNKS, L_BLK, D))
    perm = _perm_matrix()

    us, uf = _pre(xp4, xs4, pos3, modg, norm1_g[layer], w_in[layer], jnp.asarray(perm, BF16))

    bbmat, ccmat, a8 = _s5_params(s5_lam_re[layer], s5_lam_im[layer], s5_log_dt[layer], s5_b_re[layer],
                                  s5_b_im[layer], s5_c_re[layer], s5_c_im[layer])

    st = jnp.stack([state_s5_re[:, layer], state_s5_im[:, layer]], axis=2)
    st = st.reshape(n_lat, 2, 2, N_S5_CHUNKS, CH_P)
    st = jnp.transpose(st, (1, 3, 0, 2, 4)).reshape(2, N_S5_CHUNKS, n_lat, 2 * CH_P)
    init8 = jnp.zeros((2, N_S5_CHUNKS, n_lat, LAT_CHUNKS, 2 * CH_P), F32)
    init8 = init8.at[0, :, :, 0].set(st[0]).at[1, :, :, LAT_CHUNKS - 1].set(st[1])
    init8 = init8.reshape(2, N_S5_CHUNKS, SUB, 2 * CH_P)

    y_s5, fin = _s5(us, bbmat, ccmat, a8, init8, s5_d[layer])

    m1, m2 = _fnet_weights(w_fnet[layer])
    yf = _fnet(uf, m1, m2)

    x1, h2, comb = _post(y_s5, yf, xp4, xs4, pos3, modg, s5_w_glu[layer], w_out[layer], norm2_g[layer],
                         w_router[layer], b_router[layer], jnp.asarray(perm.T, BF16))

    y_prompt, y_lat = _moe_and_final(x1.reshape(N_BLK, TBLK, D), h2.reshape(T_TOK, D), comb.reshape(T_TOK, E_PAD),
                                     modg, norm_f_g, w_gate[layer], b_gate[layer], w_up[layer], b_up[layer],
                                     w_down[layer], b_down[layer])
    y_sample = y_lat.reshape(n_lat, LAT_CHUNKS * L_BLK, D)

    fin = fin[:, :N_GRP - 1].reshape(2, N_GRP - 1, SUB, N_S5_CHUNKS, 2, CH_P)
    fin = jnp.transpose(fin, (4, 1, 2, 0, 3, 5)).reshape(2, n_ctx, 1, 2, N_S5_GROUPS, S5_P)
    return (y_prompt, y_sample, fin[0], fin[1])
```

```python
import functools
import math

import numpy as np
import jax
import jax.numpy as jnp
from jax import lax
from jax.experimental import pallas as pl
from jax.experimental.pallas import tpu as pltpu

F32 = jnp.float32
BF16 = jnp.bfloat16
HIGHEST = lax.Precision.HIGHEST

D = 1024
D_S5 = 768
S5_H = 16
S5_P = 64
N_S5_GROUPS = 48
D_FN = 256
FN_GW = 64
N_EXPERTS = 32
TOP_K = 4
E_PAD = 128
SWIGLU_LIMIT = 7.0
SWIGLU_ALPHA = 1.702
RMS_EPS = 1e-6
POS_TEMP = 10000.0
GRID_W = 64

L_BLK = 256
SUB = 8
N_GRP = 3
GRP_ROWS = L_BLK * SUB
LAT_CHUNKS = 4
T_TOK = N_GRP * GRP_ROWS
TC = 128
N_TC = L_BLK // TC
PERM_T = 32
S5_CHUNK = 16
N_S5_CHUNKS = N_S5_GROUPS // S5_CHUNK
CH_U = S5_CHUNK * S5_H
CH_P = S5_CHUNK * S5_P
ROW_BLK = 512
FIX_PARTS = 2
VMEM_LIMIT = 56 * 1024 * 1024


def _cparams(sem):
    return pltpu.CompilerParams(dimension_semantics=sem, vmem_limit_bytes=VMEM_LIMIT)


def _adaln_kernel(c_ref, w_ref, b_ref, o_ref):
    c = c_ref[...]
    s = c * jax.nn.sigmoid(c)
    o_ref[...] = jnp.dot(s, w_ref[...], preferred_element_type=F32, precision=HIGHEST) + b_ref[...]


def _adaln(cvec, w_ada, b_ada):
    n = w_ada.shape[1]
    rows = cvec.shape[0]
    return pl.pallas_call(
        _adaln_kernel,
        grid=(n // D,),
        in_specs=[pl.BlockSpec((rows, D), lambda j: (0, 0)),
                  pl.BlockSpec((D, D), lambda j: (0, j)),
                  pl.BlockSpec((1, D), lambda j: (0, j))],
        out_specs=pl.BlockSpec((rows, D), lambda j: (0, j)),
        out_shape=jax.ShapeDtypeStruct((rows, n), F32),
        compiler_params=_cparams(("arbitrary",)),
        name="adaln",
    )(cvec, w_ada, b_ada.reshape(1, n))


def _zoh(lam_re, lam_im, log_dt):
    dt = jnp.exp(log_dt)
    a_re = jnp.minimum(lam_re, -1e-4)
    a_im = lam_im
    mag = jnp.exp(a_re * dt)
    ab_re = mag * jnp.cos(a_im * dt)
    ab_im = mag * jnp.sin(a_im * dt)
    den = a_re * a_re + a_im * a_im
    nr = ab_re - 1.0
    f_re = (nr * a_re + ab_im * a_im) / den
    f_im = (ab_im * a_re - nr * a_im) / den
    return ab_re, ab_im, f_re, f_im


def _s5_params_kernel(lre3_ref, lim3_ref, ldt3_ref, lre2_ref, lim2_ref, ldt2_ref, bre_ref, bim_ref,
                      cre_ref, cim_ref, bb_ref, cc_ref, a_ref):
    exact = functools.partial(jnp.dot, preferred_element_type=F32, precision=HIGHEST)
    spread = jnp.where(lax.broadcasted_iota(jnp.int32, (S5_P, CH_P), 1) % S5_P
                       == lax.broadcasted_iota(jnp.int32, (S5_P, CH_P), 0), 1.0, 0.0)

    def block_diag(m):
        wide = jnp.dot(m.reshape(CH_U, S5_P).astype(BF16), spread.astype(BF16), preferred_element_type=F32)
        row_g = lax.broadcasted_iota(jnp.int32, (CH_U, CH_P), 0) // S5_H
        col_g = lax.broadcasted_iota(jnp.int32, (CH_U, CH_P), 1) // S5_P
        return jnp.where(row_g == col_g, wide, 0.0)

    _, _, f_re, f_im = _zoh(lre3_ref[0, 0], lim3_ref[0, 0], ldt3_ref[0, 0])
    b_re = bre_ref[0, 0]
    b_im = bim_ref[0, 0]
    bb_ref[0, 0] = jnp.concatenate([block_diag(f_re * b_re - f_im * b_im),
                                    block_diag(f_re * b_im + f_im * b_re)], axis=1).astype(BF16)
    cc_ref[0, 0] = jnp.concatenate([block_diag(cre_ref[0, 0]).T,
                                    -block_diag(cim_ref[0, 0]).T], axis=0).astype(BF16)

    ab_re, ab_im, _, _ = _zoh(lre2_ref[0, 0], lim2_ref[0, 0], ldt2_ref[0, 0])
    row_g = lax.broadcasted_iota(jnp.int32, (S5_CHUNK, CH_P), 0)
    col_g = lax.broadcasted_iota(jnp.int32, (S5_CHUNK, CH_P), 1) // S5_P

    def lane_row(a):
        flat = jnp.sum(jnp.where(row_g == col_g, exact(a, spread), 0.0), axis=0, keepdims=True)
        return jnp.broadcast_to(flat, (SUB, CH_P))

    a_ref[0, 0] = jnp.concatenate([lane_row(ab_re), lane_row(ab_im)], axis=1)


def _s5_params(lam_re, lam_im, log_dt, b_re, b_im, c_re, c_im):
    nc = N_S5_CHUNKS
    ldt = jnp.broadcast_to(log_dt[:, :, None], (2, N_S5_GROUPS, S5_P))
    g3 = lambda a: a.reshape(2, nc, S5_CHUNK, 1, S5_P)
    g2 = lambda a: a.reshape(2, nc, S5_CHUNK, S5_P)
    ghp = lambda a: a.reshape(2, nc, S5_CHUNK, S5_H, S5_P)
    spec3 = pl.BlockSpec((1, 1, S5_CHUNK, 1, S5_P), lambda d, c: (d, c, 0, 0, 0))
    spec2 = pl.BlockSpec((1, 1, S5_CHUNK, S5_P), lambda d, c: (d, c, 0, 0))
    spec_ghp = pl.BlockSpec((1, 1, S5_CHUNK, S5_H, S5_P), lambda d, c: (d, c, 0, 0, 0))
    return pl.pallas_call(
        _s5_params_kernel,
        grid=(2, nc),
        in_specs=[spec3, spec3, spec3, spec2, spec2, spec2, spec_ghp, spec_ghp, spec_ghp, spec_ghp],
        out_specs=[pl.BlockSpec((1, 1, CH_U, 2 * CH_P), lambda d, c: (d, c, 0, 0)),
                   pl.BlockSpec((1, 1, 2 * CH_P, CH_U), lambda d, c: (d, c, 0, 0)),
                   pl.BlockSpec((1, 1, SUB, 2 * CH_P), lambda d, c: (d, c, 0, 0))],
        out_shape=[jax.ShapeDtypeStruct((2, nc, CH_U, 2 * CH_P), BF16),
                   jax.ShapeDtypeStruct((2, nc, 2 * CH_P, CH_U), BF16),
                   jax.ShapeDtypeStruct((2, nc, SUB, 2 * CH_P), F32)],
        compiler_params=_cparams(("arbitrary", "arbitrary")),
        name="s5_params",
    )(g3(lam_re), g3(lam_im), g3(ldt), g2(lam_re), g2(lam_im), g2(ldt),
      ghp(jnp.swapaxes(b_re, -1, -2)), ghp(jnp.swapaxes(b_im, -1, -2)), ghp(c_re), ghp(c_im))


def _rms(x, g):
    return x * lax.rsqrt(jnp.mean(x * x, axis=-1, keepdims=True) + RMS_EPS) * g


def _load_x(g, xp_ref, xs_ref, pos_ref):
    is_lat = g == N_GRP - 1
    pos = pos_ref[...]
    pos8 = jnp.concatenate([pos, pos], axis=0)
    return jnp.where(is_lat, xs_ref[0] + pos8, xp_ref[0])


def _x_specs():
    n_ctx = N_GRP - 1
    xp_spec = pl.BlockSpec(
        (1, SUB, TC, D),
        lambda g, tc: (jnp.minimum(g, n_ctx - 1), 0, jnp.where(g >= n_ctx, N_TC - 1, tc), 0))
    xs_spec = pl.BlockSpec((1, SUB, TC, D), lambda g, tc: (0, 0, jnp.where(g >= n_ctx, tc, 0), 0))
    pos_spec = pl.BlockSpec((LAT_CHUNKS, TC, D), lambda g, tc: (0, tc, 0))
    return xp_spec, xs_spec, pos_spec


def _perm_matrix():
    p = np.zeros((SUB * PERM_T, SUB * PERM_T), np.float32)
    for s in range(SUB):
        for j in range(PERM_T):
            p[j * SUB + s, s * PERM_T + j] = 1.0
    return p


def _pre_kernel(xp_ref, xs_ref, pos_ref, mod_ref, g1_ref, win_ref, perm_ref, us_ref, uf_ref):
    g = pl.program_id(0)
    x = _load_x(g, xp_ref, xs_ref, pos_ref)
    mod = mod_ref[0]
    shift1 = mod[:, :, 0:D]
    scale1 = mod[:, :, D:2 * D]
    h = _rms(x, g1_ref[...]) * (1.0 + scale1) + shift1
    h2d = h.reshape(SUB * TC, D).astype(BF16)
    proj = jnp.dot(h2d, win_ref[...].astype(BF16), preferred_element_type=F32)
    uf_ref[0] = proj[:, D_S5:].astype(BF16).reshape(SUB, TC, D_FN)
    u = proj[:, :D_S5].astype(BF16)
    perm = perm_ref[...]
    for q in range(TC // PERM_T):
        piece = jnp.concatenate(
            [u[s * TC + q * PERM_T: s * TC + (q + 1) * PERM_T] for s in range(SUB)], axis=0)
        us_ref[0, q * SUB * PERM_T:(q + 1) * SUB * PERM_T, :] = jnp.dot(
            perm, piece, preferred_element_type=F32).astype(BF16)


def _pre(xp4, xs4, pos3, modg, norm1_g, w_in, perm):
    xp_spec, xs_spec, pos_spec = _x_specs()
    return pl.pallas_call(
        _pre_kernel,
        grid=(N_GRP, N_TC),
        in_specs=[xp_spec, xs_spec, pos_spec,
                  pl.BlockSpec((1, SUB, 1, 6 * D), lambda g, tc: (g, 0, 0, 0)),
                  pl.BlockSpec((1, D), lambda g, tc: (0, 0)),
                  pl.BlockSpec((D, D), lambda g, tc: (0, 0)),
                  pl.BlockSpec((SUB * PERM_T, SUB * PERM_T), lambda g, tc: (0, 0))],
        out_specs=[pl.BlockSpec((1, SUB * TC, D_S5), lambda g, tc: (g, tc, 0)),
                   pl.BlockSpec((1, SUB, TC, D_FN), lambda g, tc: (g, 0, tc, 0))],
        out_shape=[jax.ShapeDtypeStruct((N_GRP, GRP_ROWS, D_S5), BF16),
                   jax.ShapeDtypeStruct((N_GRP, SUB, L_BLK, D_FN), BF16)],
        compiler_params=_cparams(("arbitrary", "arbitrary")),
        name="pre_mixer",
    )(xp4, xs4, pos3, modg, norm1_g.reshape(1, D), w_in, perm)


def _cmul(ar, ai, br, bi):
    return ar * br - ai * bi, ar * bi + ai * br


def _s5_kernel(us_ref, bb_ref, cc_ref, a_ref, init_ref, dskip_ref, y_ref, fin_ref, s_ref):
    g = pl.program_id(0)
    d = pl.program_id(2)
    n_rb = GRP_ROWS // ROW_BLK

    def proj_in(rb, _):
        rows = pl.ds(pl.multiple_of(rb * ROW_BLK, ROW_BLK), ROW_BLK)
        s_ref[rows, :] = jnp.dot(us_ref[0, rows, :], bb_ref[0, 0], preferred_element_type=F32)
        return 0

    lax.fori_loop(0, n_rb, proj_in, 0)

    def tile_row(k):
        t = k + d * (L_BLK - 1 - 2 * k)
        return pl.ds(pl.multiple_of(t * SUB, SUB), SUB)

    is_lat = g == N_GRP - 1
    re_all = slice(0, CH_P)
    im_all = slice(CH_P, 2 * CH_P)
    a_re = a_ref[0, 0, :, re_all]
    a_im = a_ref[0, 0, :, im_all]
    zero = jnp.zeros((SUB, CH_P), F32)
    s0_re = jnp.where(is_lat, init_ref[0, 0, :, re_all], zero)
    s0_im = jnp.where(is_lat, init_ref[0, 0, :, im_all], zero)

    def scan_body(i, carry):
        s_re, s_im = carry
        for j in range(8):
            rows = tile_row(i * 8 + j)
            n_re = a_re * s_re - a_im * s_im + s_ref[rows, re_all]
            n_im = a_re * s_im + a_im * s_re + s_ref[rows, im_all]
            s_ref[rows, re_all] = n_re
            s_ref[rows, im_all] = n_im
            s_re, s_im = n_re, n_im
        return s_re, s_im

    f_re, f_im = lax.fori_loop(0, L_BLK // 8, scan_body, (s0_re, s0_im))
    fin_ref[0, 0, :, re_all] = f_re
    fin_ref[0, 0, :, im_all] = f_im

    @pl.when(is_lat)
    def _():
        part = CH_P // FIX_PARTS
        sub_id = lax.broadcasted_iota(jnp.int32, (SUB, part), 0) % LAT_CHUNKS
        fwd = d == 0
        lo = jnp.where(fwd, 1, 0)
        hi = jnp.where(fwd, LAT_CHUNKS - 1, LAT_CHUNKS - 2)
        keep = (sub_id >= lo) & (sub_id <= hi)

        def from_prev(v):
            return jnp.where(keep, jnp.where(fwd, pltpu.roll(v, 1, 0), pltpu.roll(v, SUB - 1, 0)), 0.0)

        for hh in range(FIX_PARTS):
            lanes = slice(hh * part, (hh + 1) * part)
            re_cols = lanes
            im_cols = slice(CH_P + hh * part, CH_P + (hh + 1) * part)
            pa_re, pa_im = a_re[:, lanes], a_im[:, lanes]
            pf_re, pf_im = f_re[:, lanes], f_im[:, lanes]
            p_re, p_im = pa_re, pa_im
            for _ in range(8):
                p_re, p_im = _cmul(p_re, p_im, p_re, p_im)
            t_re, t_im = pf_re, pf_im
            for _ in range(LAT_CHUNKS - 2):
                m_re, m_im = _cmul(p_re, p_im, from_prev(t_re), from_prev(t_im))
                t_re, t_im = pf_re + m_re, pf_im + m_im
            c_re, c_im = from_prev(t_re), from_prev(t_im)

            def fix_body(i, carry, c_re=c_re, c_im=c_im, pa_re=pa_re, pa_im=pa_im, re_cols=re_cols,
                         im_cols=im_cols):
                q_re, q_im = carry
                for j in range(8):
                    rows = tile_row(i * 8 + j)
                    m_re, m_im = _cmul(q_re, q_im, c_re, c_im)
                    s_ref[rows, re_cols] = s_ref[rows, re_cols] + m_re
                    s_ref[rows, im_cols] = s_ref[rows, im_cols] + m_im
                    q_re, q_im = _cmul(q_re, q_im, pa_re, pa_im)
                return q_re, q_im

            lax.fori_loop(0, L_BLK // 8, fix_body, (pa_re, pa_im))

    def proj_out(rb):
        rows = pl.ds(pl.multiple_of(rb * ROW_BLK, ROW_BLK), ROW_BLK)
        return rows, jnp.dot(s_ref[rows, :].astype(BF16), cc_ref[0, 0], preferred_element_type=F32)

    @pl.when(d == 0)
    def _():
        def body(rb, _):
            rows, yb = proj_out(rb)
            y_ref[0, rows, :] = dskip_ref[...] * us_ref[0, rows, :].astype(F32) + yb
            return 0
        lax.fori_loop(0, n_rb, body, 0)

    @pl.when(d == 1)
    def _():
        def body(rb, _):
            rows, yb = proj_out(rb)
            y_ref[0, rows, :] = y_ref[0, rows, :] + yb
            return 0
        lax.fori_loop(0, n_rb, body, 0)


def _s5(us, bbmat, ccmat, a8, init8, d_skip):
    return pl.pallas_call(
        _s5_kernel,
        grid=(N_GRP, N_S5_CHUNKS, 2),
        in_specs=[pl.BlockSpec((1, GRP_ROWS, CH_U), lambda g, c, d: (g, 0, c)),
                  pl.BlockSpec((1, 1, CH_U, 2 * CH_P), lambda g, c, d: (d, c, 0, 0)),
                  pl.BlockSpec((1, 1, 2 * CH_P, CH_U), lambda g, c, d: (d, c, 0, 0)),
                  pl.BlockSpec((1, 1, SUB, 2 * CH_P), lambda g, c, d: (d, c, 0, 0)),
                  pl.BlockSpec((1, 1, SUB, 2 * CH_P), lambda g, c, d: (d, c, 0, 0)),
                  pl.BlockSpec((1, CH_U), lambda g, c, d: (0, c))],
        out_specs=[pl.BlockSpec((1, GRP_ROWS, CH_U), lambda g, c, d: (g, 0, c)),
                   pl.BlockSpec((1, 1, SUB, 2 * CH_P), lambda g, c, d: (d, g, 0, c))],
        out_shape=[jax.ShapeDtypeStruct((N_GRP, GRP_ROWS, D_S5), F32),
                   jax.ShapeDtypeStruct((2, N_GRP, SUB, N_S5_CHUNKS * 2 * CH_P), F32)],
        scratch_shapes=[pltpu.VMEM((GRP_ROWS, 2 * CH_P), F32)],
        compiler_params=_cparams(("arbitrary", "arbitrary", "arbitrary")),
        name="s5_scan",
    )(us, bbmat, ccmat, a8, init8, d_skip.reshape(1, D_S5))


def _fnet_w_kernel(c_ref, s_ref, w_ref, m1_ref, m2_ref):
    w = w_ref[...]
    m1_ref[...] = jnp.dot(c_ref[...], w, preferred_element_type=F32, precision=HIGHEST).astype(BF16)
    m2_ref[...] = jnp.dot(s_ref[...], w, preferred_element_type=F32, precision=HIGHEST).astype(BF16)


FN_SEQ = LAT_CHUNKS
FN_CTX_STEPS = (N_GRP - 1) * SUB // FN_SEQ


def _fnet_kernel(u_ref, cos_s_ref, sin_s_ref, cos_l_ref, sin_l_ref, m1_ref, m2_ref, o_ref):
    i = pl.program_id(0)
    m1 = m1_ref[...]
    m2 = m2_ref[...]

    def mix(u, cos_ref, sin_ref):
        v1 = jnp.dot(u, m1, preferred_element_type=F32).astype(BF16)
        v2 = jnp.dot(u, m2, preferred_element_type=F32).astype(BF16)
        return (jnp.dot(cos_ref[...].astype(BF16), v1, preferred_element_type=F32)
                - jnp.dot(sin_ref[...].astype(BF16), v2, preferred_element_type=F32)).astype(BF16)

    @pl.when(i < FN_CTX_STEPS)
    def _():
        for s in range(FN_SEQ):
            o_ref[0, s] = mix(u_ref[0, s], cos_s_ref, sin_s_ref)

    @pl.when(i >= FN_CTX_STEPS)
    def _():
        u = u_ref[0].reshape(FN_SEQ * L_BLK, D_FN)
        o_ref[0] = mix(u, cos_l_ref, sin_l_ref).reshape(FN_SEQ, L_BLK, D_FN)


def _dft_tables(n):
    k = np.arange(n, dtype=np.int64)
    ang = (2.0 * np.pi / n) * ((k[:, None] * k[None, :]) % n).astype(np.float64)
    scale = 1.0 / math.sqrt(n)
    return (np.cos(ang) * scale).astype(np.float32), (np.sin(ang) * scale).astype(np.float32)


def _fnet(uf, m1, m2):
    n_lat_steps = SUB // FN_SEQ
    cos_s, sin_s = _dft_tables(L_BLK)
    cos_l, sin_l = _dft_tables(FN_SEQ * L_BLK)
    full = lambda i: (0, 0)
    per_grp = SUB // FN_SEQ
    blk = pl.BlockSpec(
        (1, FN_SEQ, L_BLK, D_FN),
        lambda i: (jnp.minimum(i // per_grp, N_GRP - 1),
                   jnp.where(i < FN_CTX_STEPS, i % per_grp, i - FN_CTX_STEPS), 0, 0))
    table = lambda n: pl.BlockSpec((n, n), full)
    return pl.pallas_call(
        _fnet_kernel,
        grid=(FN_CTX_STEPS + n_lat_steps,),
        in_specs=[blk, table(L_BLK), table(L_BLK), table(FN_SEQ * L_BLK), table(FN_SEQ * L_BLK),
                  table(D_FN), table(D_FN)],
        out_specs=blk,
        out_shape=jax.ShapeDtypeStruct(uf.shape, BF16),
        compiler_params=_cparams(("arbitrary",)),
        name="fnet",
    )(uf, jnp.asarray(cos_s), jnp.asarray(sin_s), jnp.asarray(cos_l), jnp.asarray(sin_l), m1, m2)


def _fnet_weights(w_fnet):
    n_g = D_FN // FN_GW
    cos_c, sin_c = _dft_tables(FN_GW)
    eye = np.eye(n_g, dtype=np.float32)
    cos_bd = np.kron(eye, cos_c)
    sin_bd = np.kron(eye, sin_c)
    w_bd = (w_fnet[:, :, None, :] * jnp.asarray(eye)[:, None, :, None]).reshape(D_FN, D_FN)
    out = jax.ShapeDtypeStruct((D_FN, D_FN), BF16)
    return pl.pallas_call(_fnet_w_kernel, out_shape=(out, out), name="fnet_weights")(
        jnp.asarray(cos_bd), jnp.asarray(sin_bd), w_bd)


def _gelu_tanh(x):
    return 0.5 * x * (1.0 + jnp.tanh(math.sqrt(2.0 / math.pi) * (x + 0.044715 * (x * x * x))))


def _post_kernel(y_ref, yf_ref, xp_ref, xs_ref, pos_ref, mod_ref, wglu_ref, wout_ref, g2_ref, wr_ref,
                 br_ref, permt_ref, x1_ref, h2_ref, comb_ref):
    g = pl.program_id(0)
    z = _gelu_tanh(y_ref[0])
    gate = jnp.dot(z.astype(BF16), wglu_ref[...].astype(BF16), preferred_element_type=F32)
    gl = (z * jax.nn.sigmoid(gate)).astype(BF16)
    permt = permt_ref[...]
    n_q = TC // PERM_T
    nat = [jnp.dot(permt, gl[q * SUB * PERM_T:(q + 1) * SUB * PERM_T], preferred_element_type=F32).astype(BF16)
           for q in range(n_q)]
    gl_nat = jnp.concatenate(
        [nat[q][s * PERM_T:(s + 1) * PERM_T] for s in range(SUB) for q in range(n_q)], axis=0)
    w_out = wout_ref[...].astype(BF16)
    mixed = (jnp.dot(gl_nat, w_out[:D_S5], preferred_element_type=F32)
             + jnp.dot(yf_ref[0].reshape(SUB * TC, D_FN), w_out[D_S5:], preferred_element_type=F32))
    x = _load_x(g, xp_ref, xs_ref, pos_ref)
    mod = mod_ref[0]
    gate1 = mod[:, :, 2 * D:3 * D]
    shift2 = mod[:, :, 3 * D:4 * D]
    scale2 = mod[:, :, 4 * D:5 * D]
    x1 = x + gate1 * mixed.reshape(SUB, TC, D)
    x1_ref[0] = x1
    h2 = _rms(x1, g2_ref[...]) * (1.0 + scale2) + shift2
    h2_ref[0] = h2.astype(BF16)

    hr = h2.reshape(SUB * TC, D)
    h_hi = hr.astype(BF16)
    h_lo = (hr - h_hi.astype(F32)).astype(BF16)
    wr = wr_ref[...]
    w_hi = wr.astype(BF16)
    w_lo = (wr - w_hi.astype(F32)).astype(BF16)
    logits = (jnp.dot(h_hi, w_hi, preferred_element_type=F32) + jnp.dot(h_lo, w_hi, preferred_element_type=F32)
              + jnp.dot(h_hi, w_lo, preferred_element_type=F32) + br_ref[...])
    lane = lax.broadcasted_iota(jnp.int32, logits.shape, 1).astype(F32)
    top_v, hots = [], []
    cur = logits
    for _ in range(TOP_K):
        m = jnp.max(cur, axis=-1, keepdims=True)
        idx = jnp.min(jnp.where(cur == m, lane, float(E_PAD)), axis=-1, keepdims=True)
        hot = lane == idx
        top_v.append(m)
        hots.append(hot)
        cur = jnp.where(hot, -3.0e38, cur)
    exps = [jnp.exp(v - top_v[0]) for v in top_v]
    denom = exps[0] + exps[1] + exps[2] + exps[3]
    comb = jnp.zeros(logits.shape, F32)
    for k in range(TOP_K):
        comb = comb + jnp.where(hots[k], exps[k] / denom, 0.0)
    comb_ref[0] = comb.reshape(SUB, TC, E_PAD)


def _post(y, yf, xp4, xs4, pos3, modg, w_glu, w_out, norm2_g, w_router, b_router, permt):
    xp_spec, xs_spec, pos_spec = _x_specs()
    const2 = lambda g, tc: (0, 0)
    wr = jnp.zeros((D, E_PAD), F32).at[:, :N_EXPERTS].set(w_router)
    br = jnp.full((1, E_PAD), -1.0e30, F32).at[0, :N_EXPERTS].set(b_router)
    blk = lambda w: pl.BlockSpec((1, SUB, TC, w), lambda g, tc: (g, 0, tc, 0))
    return pl.pallas_call(
        _post_kernel,
        grid=(N_GRP, N_TC),
        in_specs=[pl.BlockSpec((1, SUB * TC, D_S5), lambda g, tc: (g, tc, 0)),
                  blk(D_FN), xp_spec, xs_spec, pos_spec,
                  pl.BlockSpec((1, SUB, 1, 6 * D), lambda g, tc: (g, 0, 0, 0)),
                  pl.BlockSpec((D_S5, D_S5), const2),
                  pl.BlockSpec((D, D), const2),
                  pl.BlockSpec((1, D), const2),
                  pl.BlockSpec((D, E_PAD), const2),
                  pl.BlockSpec((1, E_PAD), const2),
                  pl.BlockSpec((SUB * PERM_T, SUB * PERM_T), const2)],
        out_specs=[blk(D), blk(D), blk(E_PAD)],
        out_shape=[jax.ShapeDtypeStruct((N_GRP, SUB, L_BLK, D), F32),
                   jax.ShapeDtypeStruct((N_GRP, SUB, L_BLK, D), BF16),
                   jax.ShapeDtypeStruct((N_GRP, SUB, L_BLK, E_PAD), F32)],
        compiler_params=_cparams(("arbitrary", "arbitrary")),
        name="post_mixer",
    )(y, yf, xp4, xs4, pos3, modg, w_glu, w_out, norm2_g.reshape(1, D), wr, br, permt)


TBLK = L_BLK
N_BLK = T_TOK // TBLK
N_CTX_BLK = (N_GRP - 1) * SUB
SEG_ALIGN = 8
RB = 1280
assert RB >= TBLK * TOP_K + N_EXPERTS * (SEG_ALIGN - 1) and RB % 128 == 0
TM = 256
TM_SHIFT = 8
assert 1 << TM_SHIFT == TM
R_TOT = T_TOK * TOP_K + N_BLK * N_EXPERTS * (SEG_ALIGN - 1) + TM
TAB_ROWS = 32
assert TAB_ROWS >= N_BLK
BIG_ROWS = 32
BIG_SHIFT = 2
assert SEG_ALIGN << BIG_SHIFT == BIG_ROWS


def _plan_kernel(comb_ref, z_ref, ptab_ref, loff_ref, gst_ref, eoff_ref):
    row = lax.broadcasted_iota(jnp.int32, (TBLK, TBLK), 0)
    col = lax.broadcasted_iota(jnp.int32, (TBLK, TBLK), 1)
    earlier = jnp.where(row > col, 1.0, 0.0).astype(BF16)
    ptab_ref[...] = jnp.zeros_like(ptab_ref)

    def body(b, _):
        rows = pl.ds(pl.multiple_of(b * TBLK, TBLK), TBLK)
        m = jnp.where(comb_ref[rows, :] > 0.0, 1.0, 0.0)
        rank = jnp.dot(earlier, m.astype(BF16), preferred_element_type=F32)
        z_ref[rows, :] = m * (rank + 1.0)
        n = jnp.sum(m, axis=0, keepdims=True)
        ptab_ref[pl.ds(b, 1), :] = jnp.floor((n + (SEG_ALIGN - 1)) * (1.0 / SEG_ALIGN)) * SEG_ALIGN
        return 0

    lax.fori_loop(0, N_BLK, body, 0)

    ptab = ptab_ref[...]
    er = lax.broadcasted_iota(jnp.int32, (E_PAD, E_PAD), 0)
    ec = lax.broadcasted_iota(jnp.int32, (E_PAD, E_PAD), 1)
    before = jnp.where(er < ec, 1.0, 0.0)
    exact = functools.partial(jnp.dot, preferred_element_type=F32, precision=HIGHEST)
    loff_ref[...] = exact(ptab, before)
    tot = jnp.sum(ptab, axis=0, keepdims=True)
    eoff = exact(jnp.broadcast_to(tot, (SUB, E_PAD)), before)
    eoff_ref[...] = eoff
    br = lax.broadcasted_iota(jnp.int32, (TAB_ROWS, TAB_ROWS), 0)
    bc = lax.broadcasted_iota(jnp.int32, (TAB_ROWS, TAB_ROWS), 1)
    gst_ref[...] = eoff[0:1] + exact(jnp.where(br > bc, 1.0, 0.0), ptab)


def _plan(comb):
    tab = jax.ShapeDtypeStruct((TAB_ROWS, E_PAD), F32)
    return pl.pallas_call(
        _plan_kernel,
        out_shape=(jax.ShapeDtypeStruct((T_TOK, E_PAD), F32), tab, tab, tab,
                   jax.ShapeDtypeStruct((SUB, E_PAD), F32)),
        compiler_params=pltpu.CompilerParams(vmem_limit_bytes=VMEM_LIMIT),
        name="moe_plan",
    )(comb)


def _sort_matrix(b, z_ref, loff_ref, ptab_ref):
    loff = loff_ref[pl.ds(b, 1), :]
    size = ptab_ref[pl.ds(b, 1), :]
    r = lax.broadcasted_iota(jnp.int32, (RB, E_PAD), 0).astype(F32)
    owner = jnp.where(r >= loff, jnp.where(r < loff + size, 1.0, 0.0), 0.0)
    rank1 = r[:, 0:1] - jnp.sum(owner * loff, axis=-1, keepdims=True) + 1.0
    zt = z_ref[...].T
    v = jnp.dot(owner.astype(BF16), zt.astype(BF16), preferred_element_type=F32)
    return jnp.where(v == rank1, 1.0, 0.0), owner


def _segment_copies(b, loff_s, gst_s, p8_s, make_copy):
    def per_expert(e, counts):
        k = b * N_EXPERTS + e
        chunks = p8_s[k]
        n_big = lax.shift_right_logical(chunks, BIG_SHIFT)
        n_small = chunks - n_big * (BIG_ROWS // SEG_ALIGN)
        local0 = loff_s[k]
        global0 = gst_s[k]

        def big(j, _):
            make_copy(pl.multiple_of(local0 + j * BIG_ROWS, SEG_ALIGN),
                      pl.multiple_of(global0 + j * BIG_ROWS, SEG_ALIGN), BIG_ROWS).start()
            return 0

        def small(j, _):
            off = n_big * BIG_ROWS + j * SEG_ALIGN
            make_copy(pl.multiple_of(local0 + off, SEG_ALIGN),
                      pl.multiple_of(global0 + off, SEG_ALIGN), SEG_ALIGN).start()
            return 0

        lax.fori_loop(0, n_big, big, 0)
        lax.fori_loop(0, n_small, small, 0)
        return counts[0] + n_big, counts[1] + n_small

    return lax.fori_loop(0, N_EXPERTS, per_expert, (0, 0))


def _wait_copies(counts, make_copy):
    for n, rows in zip(counts, (BIG_ROWS, SEG_ALIGN)):
        def wait_one(i, _, rows=rows):
            make_copy(0, 0, rows).wait()
            return 0
        lax.fori_loop(0, n, wait_one, 0)


def _zero_rows_from(hbm, zeros_vmem, first_row, sem):
    def copy_to(row):
        return pltpu.make_async_copy(zeros_vmem, hbm.at[pl.ds(row, TM)], sem)

    n_full = lax.shift_right_logical(R_TOT - first_row, TM_SHIFT)

    def start_one(j, _):
        copy_to(pl.multiple_of(first_row + j * TM, SEG_ALIGN)).start()
        return 0

    def wait_one(j, _):
        copy_to(0).wait()
        return 0

    lax.fori_loop(0, n_full, start_one, 0)
    lax.fori_loop(0, n_full, wait_one, 0)
    last = copy_to(R_TOT - TM)
    last.start()
    last.wait()


def _dispatch_kernel(loff_s, gst_s, p8_s, tot_s, h_ref, z_ref, loff_ref, ptab_ref, xs_hbm, xbuf, sem, pending):
    b = pl.program_id(0)

    def make_copy(local_row, global_row, rows):
        return pltpu.make_async_copy(xbuf.at[pl.ds(local_row, rows)], xs_hbm.at[pl.ds(global_row, rows)], sem)

    pm = _sort_matrix(b, z_ref, loff_ref, ptab_ref)[0].astype(BF16)

    @pl.when(b > 0)
    def _():
        _wait_copies((pending[0], pending[1]), make_copy)

    xbuf[...] = jnp.dot(pm, h_ref[...], preferred_element_type=F32)
    n_big, n_small = _segment_copies(b, loff_s, gst_s, p8_s, make_copy)
    pending[0] = n_big
    pending[1] = n_small

    @pl.when(b == N_BLK - 1)
    def _():
        _wait_copies((n_big, n_small), make_copy)
        xbuf[0:TM, :] = jnp.zeros((TM, D), F32)
        _zero_rows_from(xs_hbm, xbuf.at[0:TM], tot_s[0], sem)


def _dispatch(h2, z, loff, ptab, loff_i, gst_i, p8_i, tot_i):
    whole = pl.BlockSpec((TAB_ROWS, E_PAD), lambda b, *_: (0, 0))
    return pl.pallas_call(
        _dispatch_kernel,
        grid_spec=pltpu.PrefetchScalarGridSpec(
            num_scalar_prefetch=4,
            grid=(N_BLK,),
            in_specs=[pl.BlockSpec((TBLK, D), lambda b, *_: (b, 0)),
                      pl.BlockSpec((TBLK, E_PAD), lambda b, *_: (b, 0)),
                      whole, whole],
            out_specs=pl.BlockSpec(memory_space=pl.ANY),
            scratch_shapes=[pltpu.VMEM((RB, D), F32), pltpu.SemaphoreType.DMA(()),
                            pltpu.SMEM((2,), jnp.int32)]),
        out_shape=jax.ShapeDtypeStruct((R_TOT, D), F32),
        compiler_params=_cparams(("arbitrary",)),
        name="moe_dispatch",
    )(loff_i, gst_i, p8_i, tot_i, h2, z, loff, ptab)


MAX_TILES = R_TOT // TM + N_EXPERTS
X_SLOTS = 3
N_MATS = 3
FF_CHUNK = 256
CAST_ROWS = 64


def _expert_kernel(eoff_s, xs_hbm, wg_hbm, wu_hbm, wd_hbm, bg_ref, bu_ref, bd_ref, ys_hbm,
                   wst, wbf, xin, yout, act_ref, w_sem, x_sem, y_sem, t_exp, t_row, t_first, live):
    def add_expert(e, carry):
        n_t, n_live = carry
        start = eoff_s[e]
        tiles = lax.shift_right_logical(eoff_s[e + 1] - start + (TM - 1), TM_SHIFT)

        def add_tile(i, _):
            t_exp[n_t + i] = e
            t_row[n_t + i] = start + i * TM
            t_first[n_t + i] = jnp.where(i == 0, 1, 0)
            return 0

        lax.fori_loop(0, tiles, add_tile, 0)
        has_rows = jnp.where(tiles > 0, 1, 0)

        @pl.when(tiles > 0)
        def _():
            live[n_live] = e

        return n_t + tiles, n_live + has_rows

    n_tiles, n_live = lax.fori_loop(0, N_EXPERTS, add_expert, (0, 0))

    def w_copies(e, slot):
        return [pltpu.make_async_copy(w.at[e], wst.at[slot, m], w_sem.at[slot])
                for m, w in enumerate((wg_hbm, wu_hbm, wd_hbm))]

    def tile_rows(j):
        return pl.ds(pl.multiple_of(t_row[j], SEG_ALIGN), TM)

    def x_copy(j, slot):
        return pltpu.make_async_copy(xs_hbm.at[tile_rows(j)], xin.at[slot], x_sem.at[slot])

    def y_copy(j):
        return pltpu.make_async_copy(yout, ys_hbm.at[tile_rows(j)], y_sem)

    @pl.when(n_tiles > 0)
    def _():
        for cp in w_copies(live[0], 0):
            cp.start()
        x_copy(0, 0).start()

    @pl.when(n_tiles > 1)
    def _():
        x_copy(1, 1).start()

    def body(j, k):
        e = t_exp[j]

        @pl.when(t_first[j] == 1)
        def _():
            ws = lax.rem(k, 2)
            for cp in w_copies(e, ws):
                cp.wait()
            def cast_rows(r, _):
                rows = pl.ds(pl.multiple_of(r * CAST_ROWS, CAST_ROWS), CAST_ROWS)
                for m in range(N_MATS):
                    wbf[m, rows, :] = wst[ws, m, rows, :].astype(BF16)
                return 0

            lax.fori_loop(0, D // CAST_ROWS, cast_rows, 0)

            @pl.when(k + 1 < n_live)
            def _():
                for cp in w_copies(live[k + 1], 1 - ws):
                    cp.start()

        slot = lax.rem(j, X_SLOTS)
        x_copy(j, slot).wait()

        @pl.when(j + 2 < n_tiles)
        def _():
            x_copy(j + 2, lax.rem(j + 2, X_SLOTS)).start()

        x = xin[slot].astype(BF16)
        for c in range(D // FF_CHUNK):
            cols = slice(c * FF_CHUNK, (c + 1) * FF_CHUNK)
            gate = jnp.dot(x, wbf[0, :, cols], preferred_element_type=F32) + bg_ref[e][:, cols]
            up = jnp.dot(x, wbf[1, :, cols], preferred_element_type=F32) + bu_ref[e][:, cols]
            gate = jnp.minimum(gate, SWIGLU_LIMIT)
            up = jnp.clip(up, -SWIGLU_LIMIT, SWIGLU_LIMIT)
            act_ref[:, cols] = ((up + 1.0) * gate * jax.nn.sigmoid(SWIGLU_ALPHA * gate)).astype(BF16)
        y = jnp.dot(act_ref[...], wbf[2], preferred_element_type=F32) + bd_ref[e]

        @pl.when(j > 0)
        def _():
            y_copy(j - 1).wait()

        yout[...] = y
        y_copy(j).start()
        return k + t_first[j]

    lax.fori_loop(0, n_tiles, body, 0)

    @pl.when(n_tiles > 0)
    def _():
        y_copy(n_tiles - 1).wait()

    yout[...] = jnp.zeros((TM, D), F32)
    _zero_rows_from(ys_hbm, yout, eoff_s[N_EXPERTS], y_sem)


def _experts(xs, eoff_i, w_gate, b_gate, w_up, b_up, w_down, b_down):
    hbm = pl.BlockSpec(memory_space=pl.ANY)
    bspec = pl.BlockSpec((N_EXPERTS, 1, D), lambda i, *_: (0, 0, 0))
    return pl.pallas_call(
        _expert_kernel,
        grid_spec=pltpu.PrefetchScalarGridSpec(
            num_scalar_prefetch=1,
            grid=(1,),
            in_specs=[hbm, hbm, hbm, hbm, bspec, bspec, bspec],
            out_specs=hbm,
            scratch_shapes=[pltpu.VMEM((2, N_MATS, D, D), F32), pltpu.VMEM((N_MATS, D, D), BF16),
                            pltpu.VMEM((X_SLOTS, TM, D), F32), pltpu.VMEM((TM, D), F32),
                            pltpu.VMEM((TM, D), BF16),
                            pltpu.SemaphoreType.DMA((2,)), pltpu.SemaphoreType.DMA((X_SLOTS,)),
                            pltpu.SemaphoreType.DMA(()),
                            pltpu.SMEM((MAX_TILES,), jnp.int32), pltpu.SMEM((MAX_TILES,), jnp.int32),
                            pltpu.SMEM((MAX_TILES,), jnp.int32), pltpu.SMEM((N_EXPERTS,), jnp.int32)]),
        out_shape=jax.ShapeDtypeStruct((R_TOT, D), F32),
        compiler_params=_cparams(("arbitrary",)),
        name="moe_experts",
    )(eoff_i, xs, w_gate, w_up, w_down, b_gate.reshape(N_EXPERTS, 1, D), b_up.reshape(N_EXPERTS, 1, D),
      b_down.reshape(N_EXPERTS, 1, D))


def _combine_kernel(loff_s, gst_s, p8_s, ys_hbm, z_ref, comb_ref, loff_ref, ptab_ref, x1_ref, mod_ref, gf_ref,
                    yctx_ref, ylat_ref, ybuf, sem):
    b = pl.program_id(0)

    def make_copy(local_row, global_row, rows):
        return pltpu.make_async_copy(ys_hbm.at[pl.ds(global_row, rows)], ybuf.at[pl.ds(local_row, rows)], sem)

    started = _segment_copies(b, loff_s, gst_s, p8_s, make_copy)

    last = b * N_EXPERTS + N_EXPERTS - 1
    used_chunks = lax.shift_right_logical(loff_s[last], 3) + p8_s[last]

    def zero_chunk(j, _):
        ybuf[pl.ds(pl.multiple_of(j * SEG_ALIGN, SEG_ALIGN), SEG_ALIGN), :] = jnp.zeros((SEG_ALIGN, D), F32)
        return 0

    lax.fori_loop(used_chunks, RB // SEG_ALIGN, zero_chunk, 0)

    pm, owner = _sort_matrix(b, z_ref, loff_ref, ptab_ref)
    w_all = jnp.dot(owner, comb_ref[...].T, preferred_element_type=F32, precision=HIGHEST)
    w_row = jnp.sum(pm * w_all, axis=-1, keepdims=True)
    pmt = pm.T.astype(BF16)
    _wait_copies(started, make_copy)
    y = (ybuf[...] * w_row).astype(BF16)
    moe = jnp.dot(pmt, y, preferred_element_type=F32)
    gate2 = mod_ref[0][:, 5 * D:6 * D]
    x2 = x1_ref[0] + gate2 * moe
    y = _rms(x2, gf_ref[...])

    @pl.when(b < N_CTX_BLK)
    def _():
        yctx_ref[0] = y

    @pl.when(b >= N_CTX_BLK)
    def _():
        ylat_ref[0] = y


def _combine(ys, z, comb, loff, ptab, x1, modv, norm_f_g, loff_i, gst_i, p8_i):
    whole = pl.BlockSpec((TAB_ROWS, E_PAD), lambda b, *_: (0, 0))
    tok = pl.BlockSpec((TBLK, E_PAD), lambda b, *_: (b, 0))
    return pl.pallas_call(
        _combine_kernel,
        grid_spec=pltpu.PrefetchScalarGridSpec(
            num_scalar_prefetch=3,
            grid=(N_BLK,),
            in_specs=[pl.BlockSpec(memory_space=pl.ANY), tok, tok, whole, whole,
                      pl.BlockSpec((1, TBLK, D), lambda b, *_: (b, 0, 0)),
                      pl.BlockSpec((1, 1, 6 * D), lambda b, *_: (b, 0, 0)),
                      pl.BlockSpec((1, D), lambda b, *_: (0, 0))],
            out_specs=[pl.BlockSpec((1, TBLK, D), lambda b, *_: (jnp.minimum(b, N_CTX_BLK - 1), 0, 0)),
                       pl.BlockSpec((1, TBLK, D), lambda b, *_: (jnp.maximum(b - N_CTX_BLK, 0), 0, 0))],
            scratch_shapes=[pltpu.VMEM((RB, D), F32), pltpu.SemaphoreType.DMA(())]),
        out_shape=[jax.ShapeDtypeStruct((N_CTX_BLK, TBLK, D), F32),
                   jax.ShapeDtypeStruct((N_BLK - N_CTX_BLK, TBLK, D), F32)],
        compiler_params=_cparams(("arbitrary",)),
        name="moe_combine",
    )(loff_i, gst_i, p8_i, ys, z, comb, loff, ptab, x1, modv, norm_f_g.reshape(1, D))


def _moe_and_final(x1, h2, comb, modg, norm_f_g, w_gate, b_gate, w_up, b_up, w_down, b_down):
    z, ptab, loff, gst, eoff = _plan(comb)
    as_scalars = lambda t: t[:N_BLK, :N_EXPERTS].astype(jnp.int32).reshape(N_BLK * N_EXPERTS)
    loff_i, gst_i = as_scalars(loff), as_scalars(gst)
    p8_i = as_scalars(ptab) // SEG_ALIGN
    eoff_i = eoff[0, :N_EXPERTS + 1].astype(jnp.int32)
    xs = _dispatch(h2, z, loff, ptab, loff_i, gst_i, p8_i, eoff_i[N_EXPERTS:])
    ys = _experts(xs, eoff_i, w_gate, b_gate, w_up, b_up, w_down, b_down)
    return _combine(ys, z, comb, loff, ptab, x1, modg.reshape(N_BLK, 1, 6 * D), norm_f_g,
                    loff_i, gst_i, p8_i)


def _grid_pos_embed(n_tokens, dim):
    rows = n_tokens // GRID_W
    t = np.arange(rows * GRID_W)
    r = (t // GRID_W).astype(np.float32)
    col = (t % GRID_W).astype(np.float32)
    q = dim // 4
    omega = (1.0 / np.float32(POS_TEMP) ** (np.arange(q, dtype=np.float32) / np.float32(q))).astype(np.float32)

    def emb(p):
        a = p[:, None] * omega[None, :]
        return np.concatenate([np.sin(a), np.cos(a)], axis=-1)

    return np.concatenate([emb(r), emb(col)], axis=-1).astype(np.float32)


def kernel(x_prompt, x_sample, c, state_s5_re, state_s5_im, c_ctx, w_ada, b_ada, norm1_g, w_in, s5_lam_re,
           s5_lam_im, s5_log_dt, s5_b_re, s5_b_im, s5_c_re, s5_c_im, s5_d, s5_w_glu, w_fnet, w_out, norm2_g,
           w_router, b_router, w_gate, b_gate, w_up, b_up, w_down, b_down, norm_f_g):
    n_ctx, n_lat = x_prompt.shape[0], x_sample.shape[0]
    assert x_prompt.shape == (SUB * (N_GRP - 1), L_BLK, D) and x_sample.shape == (2, LAT_CHUNKS * L_BLK, D)
    assert w_ada.shape[0] == 1, "one trunk layer"
    layer = 0

    cvec = jnp.concatenate([jnp.broadcast_to(c_ctx[None], (n_ctx, D)), jnp.repeat(c, LAT_CHUNKS, axis=0)], axis=0)
    modg = _adaln(cvec, w_ada[layer], b_ada[layer]).reshape(N_GRP, SUB, 1, 6 * D)

    xp4 = x_prompt.reshape(N_GRP - 1, SUB, L_BLK, D)
    xs4 = x_sample.reshape(1, SUB, L_BLK, D)
    pos3 = jnp.asarray(_grid_pos_embed(LAT_CHUNKS * L_BLK, D).reshape(LAT_CHUNKS, L_BLK, D))
    perm = _perm_matrix()

    us, uf = _pre(xp4, xs4, pos3, modg, norm1_g[layer], w_in[layer], jnp.asarray(perm, BF16))

    bbmat, ccmat, a8 = _s5_params(s5_lam_re[layer], s5_lam_im[layer], s5_log_dt[layer], s5_b_re[layer],
                                  s5_b_im[layer], s5_c_re[layer], s5_c_im[layer])

    st = jnp.stack([state_s5_re[:, layer], state_s5_im[:, layer]], axis=2)
    st = st.reshape(n_lat, 2, 2, N_S5_CHUNKS, CH_P)
    st = jnp.transpose(st, (1, 3, 0, 2, 4)).reshape(2, N_S5_CHUNKS, n_lat, 2 * CH_P)
    init8 = jnp.zeros((2, N_S5_CHUNKS, n_lat, LAT_CHUNKS, 2 * CH_P), F32)
    init8 = init8.at[0, :, :, 0].set(st[0]).at[1, :, :, LAT_CHUNKS - 1].set(st[1])
    init8 = init8.reshape(2, N_S5_CHUNKS, SUB, 2 * CH_P)

    y_s5, fin = _s5(us, bbmat, ccmat, a8, init8, s5_d[layer])

    m1, m2 = _fnet_weights(w_fnet[layer])
    yf = _fnet(uf, m1, m2)

    x1, h2, comb = _post(y_s5, yf, xp4, xs4, pos3, modg, s5_w_glu[layer], w_out[layer], norm2_g[layer],
                         w_router[layer], b_router[layer], jnp.asarray(perm.T, BF16))

    y_prompt, y_lat = _moe_and_final(x1.reshape(N_BLK, TBLK, D), h2.reshape(T_TOK, D), comb.reshape(T_TOK, E_PAD),
                                     modg, norm_f_g, w_gate[layer], b_gate[layer], w_up[layer], b_up[layer],
                                     w_down[layer], b_down[layer])
    y_sample = y_lat.reshape(n_lat, LAT_CHUNKS * L_BLK, D)

    fin = fin[:, :N_GRP - 1].reshape(2, N_GRP - 1, SUB, N_S5_CHUNKS, 2, CH_P)
    fin = jnp.transpose(fin, (4, 1, 2, 0, 3, 5)).reshape(2, n_ctx, 1, 2, N_S5_GROUPS, S5_P)
    return (y_prompt, y_sample, fin[0], fin[1])
```

```python
import functools
import math

import numpy as np
import jax
import jax.numpy as jnp
from jax import lax
from jax.experimental import pallas as pl
from jax.experimental.pallas import tpu as pltpu

F32 = jnp.float32
BF16 = jnp.bfloat16
HIGHEST = lax.Precision.HIGHEST

D = 1024
D_S5 = 768
S5_H = 16
S5_P = 64
N_S5_GROUPS = 48
D_FN = 256
FN_GW = 64
N_EXPERTS = 32
TOP_K = 4
E_PAD = 128
SWIGLU_LIMIT = 7.0
SWIGLU_ALPHA = 1.702
RMS_EPS = 1e-6
POS_TEMP = 10000.0
GRID_W = 64

L_BLK = 256
SUB = 8
N_GRP = 3
GRP_ROWS = L_BLK * SUB
LAT_CHUNKS = 4
T_TOK = N_GRP * GRP_ROWS
TC = 128
N_TC = L_BLK // TC
PERM_T = 32
S5_CHUNK = 16
N_S5_CHUNKS = N_S5_GROUPS // S5_CHUNK
CH_U = S5_CHUNK * S5_H
CH_P = S5_CHUNK * S5_P
ROW_BLK = 512
FIX_PARTS = 2
VMEM_LIMIT = 56 * 1024 * 1024


def _cparams(sem):
    return pltpu.CompilerParams(dimension_semantics=sem, vmem_limit_bytes=VMEM_LIMIT)


def _adaln_kernel(c_ref, w_ref, b_ref, o_ref):
    c = c_ref[...]
    s = c * jax.nn.sigmoid(c)
    o_ref[...] = jnp.dot(s, w_ref[...], preferred_element_type=F32, precision=HIGHEST) + b_ref[...]


def _adaln(cvec, w_ada, b_ada):
    n = w_ada.shape[1]
    rows = cvec.shape[0]
    return pl.pallas_call(
        _adaln_kernel,
        grid=(n // D,),
        in_specs=[pl.BlockSpec((rows, D), lambda j: (0, 0)),
                  pl.BlockSpec((D, D), lambda j: (0, j)),
                  pl.BlockSpec((1, D), lambda j: (0, j))],
        out_specs=pl.BlockSpec((rows, D), lambda j: (0, j)),
        out_shape=jax.ShapeDtypeStruct((rows, n), F32),
        compiler_params=_cparams(("arbitrary",)),
        name="adaln",
    )(cvec, w_ada, b_ada.reshape(1, n))


def _zoh(lam_re, lam_im, log_dt):
    dt = jnp.exp(log_dt)
    a_re = jnp.minimum(lam_re, -1e-4)
    a_im = lam_im
    mag = jnp.exp(a_re * dt)
    ab_re = mag * jnp.cos(a_im * dt)
    ab_im = mag * jnp.sin(a_im * dt)
    den = a_re * a_re + a_im * a_im
    nr = ab_re - 1.0
    f_re = (nr * a_re + ab_im * a_im) / den
    f_im = (ab_im * a_re - nr * a_im) / den
    return ab_re, ab_im, f_re, f_im


def _s5_params_kernel(lre3_ref, lim3_ref, ldt3_ref, lre2_ref, lim2_ref, ldt2_ref, bre_ref, bim_ref,
                      cre_ref, cim_ref, bb_ref, cc_ref, a_ref):
    exact = functools.partial(jnp.dot, preferred_element_type=F32, precision=HIGHEST)
    spread = jnp.where(lax.broadcasted_iota(jnp.int32, (S5_P, CH_P), 1) % S5_P
                       == lax.broadcasted_iota(jnp.int32, (S5_P, CH_P), 0), 1.0, 0.0)

    def block_diag(m):
        wide = jnp.dot(m.reshape(CH_U, S5_P).astype(BF16), spread.astype(BF16), preferred_element_type=F32)
        row_g = lax.broadcasted_iota(jnp.int32, (CH_U, CH_P), 0) // S5_H
        col_g = lax.broadcasted_iota(jnp.int32, (CH_U, CH_P), 1) // S5_P
        return jnp.where(row_g == col_g, wide, 0.0)

    _, _, f_re, f_im = _zoh(lre3_ref[0, 0], lim3_ref[0, 0], ldt3_ref[0, 0])
    b_re = bre_ref[0, 0]
    b_im = bim_ref[0, 0]
    bb_ref[0, 0] = jnp.concatenate([block_diag(f_re * b_re - f_im * b_im),
                                    block_diag(f_re * b_im + f_im * b_re)], axis=1).astype(BF16)
    cc_ref[0, 0] = jnp.concatenate([block_diag(cre_ref[0, 0]).T,
                                    -block_diag(cim_ref[0, 0]).T], axis=0).astype(BF16)

    ab_re, ab_im, _, _ = _zoh(lre2_ref[0, 0], lim2_ref[0, 0], ldt2_ref[0, 0])
    row_g = lax.broadcasted_iota(jnp.int32, (S5_CHUNK, CH_P), 0)
    col_g = lax.broadcasted_iota(jnp.int32, (S5_CHUNK, CH_P), 1) // S5_P

    def lane_row(a):
        flat = jnp.sum(jnp.where(row_g == col_g, exact(a, spread), 0.0), axis=0, keepdims=True)
        return jnp.broadcast_to(flat, (SUB, CH_P))

    a_ref[0, 0] = jnp.concatenate([lane_row(ab_re), lane_row(ab_im)], axis=1)


def _s5_params(lam_re, lam_im, log_dt, b_re, b_im, c_re, c_im):
    nc = N_S5_CHUNKS
    ldt = jnp.broadcast_to(log_dt[:, :, None], (2, N_S5_GROUPS, S5_P))
    g3 = lambda a: a.reshape(2, nc, S5_CHUNK, 1, S5_P)
    g2 = lambda a: a.reshape(2, nc, S5_CHUNK, S5_P)
    ghp = lambda a: a.reshape(2, nc, S5_CHUNK, S5_H, S5_P)
    spec3 = pl.BlockSpec((1, 1, S5_CHUNK, 1, S5_P), lambda d, c: (d, c, 0, 0, 0))
    spec2 = pl.BlockSpec((1, 1, S5_CHUNK, S5_P), lambda d, c: (d, c, 0, 0))
    spec_ghp = pl.BlockSpec((1, 1, S5_CHUNK, S5_H, S5_P), lambda d, c: (d, c, 0, 0, 0))
    return pl.pallas_call(
        _s5_params_kernel,
        grid=(2, nc),
        in_specs=[spec3, spec3, spec3, spec2, spec2, spec2, spec_ghp, spec_ghp, spec_ghp, spec_ghp],
        out_specs=[pl.BlockSpec((1, 1, CH_U, 2 * CH_P), lambda d, c: (d, c, 0, 0)),
                   pl.BlockSpec((1, 1, 2 * CH_P, CH_U), lambda d, c: (d, c, 0, 0)),
                   pl.BlockSpec((1, 1, SUB, 2 * CH_P), lambda d, c: (d, c, 0, 0))],
        out_shape=[jax.ShapeDtypeStruct((2, nc, CH_U, 2 * CH_P), BF16),
                   jax.ShapeDtypeStruct((2, nc, 2 * CH_P, CH_U), BF16),
                   jax.ShapeDtypeStruct((2, nc, SUB, 2 * CH_P), F32)],
        compiler_params=_cparams(("arbitrary", "arbitrary")),
        name="s5_params",
    )(g3(lam_re), g3(lam_im), g3(ldt), g2(lam_re), g2(lam_im), g2(ldt),
      ghp(jnp.swapaxes(b_re, -1, -2)), ghp(jnp.swapaxes(b_im, -1, -2)), ghp(c_re), ghp(c_im))


def _rms(x, g):
    return x * lax.rsqrt(jnp.mean(x * x, axis=-1, keepdims=True) + RMS_EPS) * g


def _load_x(g, xp_ref, xs_ref, pos_ref):
    is_lat = g == N_GRP - 1
    pos = pos_ref[...]
    pos8 = jnp.concatenate([pos, pos], axis=0)
    return jnp.where(is_lat, xs_ref[0] + pos8, xp_ref[0])


def _x_specs():
    n_ctx = N_GRP - 1
    xp_spec = pl.BlockSpec(
        (1, SUB, TC, D),
        lambda g, tc: (jnp.minimum(g, n_ctx - 1), 0, jnp.where(g >= n_ctx, N_TC - 1, tc), 0))
    xs_spec = pl.BlockSpec((1, SUB, TC, D), lambda g, tc: (0, 0, jnp.where(g >= n_ctx, tc, 0), 0))
    pos_spec = pl.BlockSpec((LAT_CHUNKS, TC, D), lambda g, tc: (0, tc, 0))
    return xp_spec, xs_spec, pos_spec


def _perm_matrix():
    p = np.zeros((SUB * PERM_T, SUB * PERM_T), np.float32)
    for s in range(SUB):
        for j in range(PERM_T):
            p[j * SUB + s, s * PERM_T + j] = 1.0
    return p


def _pre_kernel(xp_ref, xs_ref, pos_ref, mod_ref, g1_ref, win_ref, perm_ref, us_ref, uf_ref):
    g = pl.program_id(0)
    x = _load_x(g, xp_ref, xs_ref, pos_ref)
    mod = mod_ref[0]
    shift1 = mod[:, :, 0:D]
    scale1 = mod[:, :, D:2 * D]
    h = _rms(x, g1_ref[...]) * (1.0 + scale1) + shift1
    h2d = h.reshape(SUB * TC, D).astype(BF16)
    proj = jnp.dot(h2d, win_ref[...].astype(BF16), preferred_element_type=F32)
    uf_ref[0] = proj[:, D_S5:].astype(BF16).reshape(SUB, TC, D_FN)
    u = proj[:, :D_S5].astype(BF16)
    perm = perm_ref[...]
    for q in range(TC // PERM_T):
        piece = jnp.concatenate(
            [u[s * TC + q * PERM_T: s * TC + (q + 1) * PERM_T] for s in range(SUB)], axis=0)
        us_ref[0, q * SUB * PERM_T:(q + 1) * SUB * PERM_T, :] = jnp.dot(
            perm, piece, preferred_element_type=F32).astype(BF16)


def _pre(xp4, xs4, pos3, modg, norm1_g, w_in, perm):
    xp_spec, xs_spec, pos_spec = _x_specs()
    return pl.pallas_call(
        _pre_kernel,
        grid=(N_GRP, N_TC),
        in_specs=[xp_spec, xs_spec, pos_spec,
                  pl.BlockSpec((1, SUB, 1, 6 * D), lambda g, tc: (g, 0, 0, 0)),
                  pl.BlockSpec((1, D), lambda g, tc: (0, 0)),
                  pl.BlockSpec((D, D), lambda g, tc: (0, 0)),
                  pl.BlockSpec((SUB * PERM_T, SUB * PERM_T), lambda g, tc: (0, 0))],
        out_specs=[pl.BlockSpec((1, SUB * TC, D_S5), lambda g, tc: (g, tc, 0)),
                   pl.BlockSpec((1, SUB, TC, D_FN), lambda g, tc: (g, 0, tc, 0))],
        out_shape=[jax.ShapeDtypeStruct((N_GRP, GRP_ROWS, D_S5), BF16),
                   jax.ShapeDtypeStruct((N_GRP, SUB, L_BLK, D_FN), BF16)],
        compiler_params=_cparams(("arbitrary", "arbitrary")),
        name="pre_mixer",
    )(xp4, xs4, pos3, modg, norm1_g.reshape(1, D), w_in, perm)


def _cmul(ar, ai, br, bi):
    return ar * br - ai * bi, ar * bi + ai * br


def _s5_kernel(us_ref, bb_ref, cc_ref, a_ref, init_ref, dskip_ref, y_ref, fin_ref, s_ref):
    g = pl.program_id(0)
    d = pl.program_id(2)
    n_rb = GRP_ROWS // ROW_BLK

    def proj_in(rb, _):
        rows = pl.ds(pl.multiple_of(rb * ROW_BLK, ROW_BLK), ROW_BLK)
        s_ref[rows, :] = jnp.dot(us_ref[0, rows, :], bb_ref[0, 0], preferred_element_type=F32)
        return 0

    lax.fori_loop(0, n_rb, proj_in, 0)

    def tile_row(k):
        t = k + d * (L_BLK - 1 - 2 * k)
        return pl.ds(pl.multiple_of(t * SUB, SUB), SUB)

    is_lat = g == N_GRP - 1
    re_all = slice(0, CH_P)
    im_all = slice(CH_P, 2 * CH_P)
    a_re = a_ref[0, 0, :, re_all]
    a_im = a_ref[0, 0, :, im_all]
    zero = jnp.zeros((SUB, CH_P), F32)
    s0_re = jnp.where(is_lat, init_ref[0, 0, :, re_all], zero)
    s0_im = jnp.where(is_lat, init_ref[0, 0, :, im_all], zero)

    def scan_body(i, carry):
        s_re, s_im = carry
        for j in range(8):
            rows = tile_row(i * 8 + j)
            n_re = a_re * s_re - a_im * s_im + s_ref[rows, re_all]
            n_im = a_re * s_im + a_im * s_re + s_ref[rows, im_all]
            s_ref[rows, re_all] = n_re
            s_ref[rows, im_all] = n_im
            s_re, s_im = n_re, n_im
        return s_re, s_im

    f_re, f_im = lax.fori_loop(0, L_BLK // 8, scan_body, (s0_re, s0_im))
    fin_ref[0, 0, :, re_all] = f_re
    fin_ref[0, 0, :, im_all] = f_im

    @pl.when(is_lat)
    def _():
        part = CH_P // FIX_PARTS
        sub_id = lax.broadcasted_iota(jnp.int32, (SUB, part), 0) % LAT_CHUNKS
        fwd = d == 0
        lo = jnp.where(fwd, 1, 0)
        hi = jnp.where(fwd, LAT_CHUNKS - 1, LAT_CHUNKS - 2)
        keep = (sub_id >= lo) & (sub_id <= hi)

        def from_prev(v):
            return jnp.where(keep, jnp.where(fwd, pltpu.roll(v, 1, 0), pltpu.roll(v, SUB - 1, 0)), 0.0)

        for hh in range(FIX_PARTS):
            lanes = slice(hh * part, (hh + 1) * part)
            re_cols = lanes
            im_cols = slice(CH_P + hh * part, CH_P + (hh + 1) * part)
            pa_re, pa_im = a_re[:, lanes], a_im[:, lanes]
            pf_re, pf_im = f_re[:, lanes], f_im[:, lanes]
            p_re, p_im = pa_re, pa_im
            for _ in range(8):
                p_re, p_im = _cmul(p_re, p_im, p_re, p_im)
            t_re, t_im = pf_re, pf_im
            for _ in range(LAT_CHUNKS - 2):
                m_re, m_im = _cmul(p_re, p_im, from_prev(t_re), from_prev(t_im))
                t_re, t_im = pf_re + m_re, pf_im + m_im
            c_re, c_im = from_prev(t_re), from_prev(t_im)

            def fix_body(i, carry, c_re=c_re, c_im=c_im, pa_re=pa_re, pa_im=pa_im, re_cols=re_cols,
                         im_cols=im_cols):
                q_re, q_im = carry
                for j in range(8):
                    rows = tile_row(i * 8 + j)
                    m_re, m_im = _cmul(q_re, q_im, c_re, c_im)
                    s_ref[rows, re_cols] = s_ref[rows, re_cols] + m_re
                    s_ref[rows, im_cols] = s_ref[rows, im_cols] + m_im
                    q_re, q_im = _cmul(q_re, q_im, pa_re, pa_im)
                return q_re, q_im

            lax.fori_loop(0, L_BLK // 8, fix_body, (pa_re, pa_im))

    def proj_out(rb):
        rows = pl.ds(pl.multiple_of(rb * ROW_BLK, ROW_BLK), ROW_BLK)
        return rows, jnp.dot(s_ref[rows, :].astype(BF16), cc_ref[0, 0], preferred_element_type=F32)

    @pl.when(d == 0)
    def _():
        def body(rb, _):
            rows, yb = proj_out(rb)
            y_ref[0, rows, :] = dskip_ref[...] * us_ref[0, rows, :].astype(F32) + yb
            return 0
        lax.fori_loop(0, n_rb, body, 0)

    @pl.when(d == 1)
    def _():
        def body(rb, _):
            rows, yb = proj_out(rb)
            y_ref[0, rows, :] = y_ref[0, rows, :] + yb
            return 0
        lax.fori_loop(0, n_rb, body, 0)


def _s5(us, bbmat, ccmat, a8, init8, d_skip):
    return pl.pallas_call(
        _s5_kernel,
        grid=(N_GRP, N_S5_CHUNKS, 2),
        in_specs=[pl.BlockSpec((1, GRP_ROWS, CH_U), lambda g, c, d: (g, 0, c)),
                  pl.BlockSpec((1, 1, CH_U, 2 * CH_P), lambda g, c, d: (d, c, 0, 0)),
                  pl.BlockSpec((1, 1, 2 * CH_P, CH_U), lambda g, c, d: (d, c, 0, 0)),
                  pl.BlockSpec((1, 1, SUB, 2 * CH_P), lambda g, c, d: (d, c, 0, 0)),
                  pl.BlockSpec((1, 1, SUB, 2 * CH_P), lambda g, c, d: (d, c, 0, 0)),
                  pl.BlockSpec((1, CH_U), lambda g, c, d: (0, c))],
        out_specs=[pl.BlockSpec((1, GRP_ROWS, CH_U), lambda g, c, d: (g, 0, c)),
                   pl.BlockSpec((1, 1, SUB, 2 * CH_P), lambda g, c, d: (d, g, 0, c))],
        out_shape=[jax.ShapeDtypeStruct((N_GRP, GRP_ROWS, D_S5), F32),
                   jax.ShapeDtypeStruct((2, N_GRP, SUB, N_S5_CHUNKS * 2 * CH_P), F32)],
        scratch_shapes=[pltpu.VMEM((GRP_ROWS, 2 * CH_P), F32)],
        compiler_params=_cparams(("arbitrary", "arbitrary", "arbitrary")),
        name="s5_scan",
    )(us, bbmat, ccmat, a8, init8, d_skip.reshape(1, D_S5))


def _fnet_w_kernel(c_ref, s_ref, w_ref, m1_ref, m2_ref):
    w = w_ref[...]
    m1_ref[...] = jnp.dot(c_ref[...], w, preferred_element_type=F32, precision=HIGHEST).astype(BF16)
    m2_ref[...] = jnp.dot(s_ref[...], w, preferred_element_type=F32, precision=HIGHEST).astype(BF16)


FN_SEQ = LAT_CHUNKS
FN_CTX_STEPS = (N_GRP - 1) * SUB // FN_SEQ


def _fnet_kernel(u_ref, cos_s_ref, sin_s_ref, cos_l_ref, sin_l_ref, m1_ref, m2_ref, o_ref):
    i = pl.program_id(0)
    m1 = m1_ref[...]
    m2 = m2_ref[...]

    def mix(u, cos_ref, sin_ref):
        v1 = jnp.dot(u, m1, preferred_element_type=F32).astype(BF16)
        v2 = jnp.dot(u, m2, preferred_element_type=F32).astype(BF16)
        return (jnp.dot(cos_ref[...].astype(BF16), v1, preferred_element_type=F32)
                - jnp.dot(sin_ref[...].astype(BF16), v2, preferred_element_type=F32)).astype(BF16)

    @pl.when(i < FN_CTX_STEPS)
    def _():
        for s in range(FN_SEQ):
            o_ref[0, s] = mix(u_ref[0, s], cos_s_ref, sin_s_ref)

    @pl.when(i >= FN_CTX_STEPS)
    def _():
        u = u_ref[0].reshape(FN_SEQ * L_BLK, D_FN)
        o_ref[0] = mix(u, cos_l_ref, sin_l_ref).reshape(FN_SEQ, L_BLK, D_FN)


def _dft_tables(n):
    k = np.arange(n, dtype=np.int64)
    ang = (2.0 * np.pi / n) * ((k[:, None] * k[None, :]) % n).astype(np.float64)
    scale = 1.0 / math.sqrt(n)
    return (np.cos(ang) * scale).astype(np.float32), (np.sin(ang) * scale).astype(np.float32)


def _fnet(uf, m1, m2):
    n_lat_steps = SUB // FN_SEQ
    cos_s, sin_s = _dft_tables(L_BLK)
    cos_l, sin_l = _dft_tables(FN_SEQ * L_BLK)
    full = lambda i: (0, 0)
    per_grp = SUB // FN_SEQ
    blk = pl.BlockSpec(
        (1, FN_SEQ, L_BLK, D_FN),
        lambda i: (jnp.minimum(i // per_grp, N_GRP - 1),
                   jnp.where(i < FN_CTX_STEPS, i % per_grp, i - FN_CTX_STEPS), 0, 0))
    table = lambda n: pl.BlockSpec((n, n), full)
    return pl.pallas_call(
        _fnet_kernel,
        grid=(FN_CTX_STEPS + n_lat_steps,),
        in_specs=[blk, table(L_BLK), table(L_BLK), table(FN_SEQ * L_BLK), table(FN_SEQ * L_BLK),
                  table(D_FN), table(D_FN)],
        out_specs=blk,
        out_shape=jax.ShapeDtypeStruct(uf.shape, BF16),
        compiler_params=_cparams(("arbitrary",)),
        name="fnet",
    )(uf, jnp.asarray(cos_s), jnp.asarray(sin_s), jnp.asarray(cos_l), jnp.asarray(sin_l), m1, m2)


def _fnet_weights(w_fnet):
    n_g = D_FN // FN_GW
    cos_c, sin_c = _dft_tables(FN_GW)
    eye = np.eye(n_g, dtype=np.float32)
    cos_bd = np.kron(eye, cos_c)
    sin_bd = np.kron(eye, sin_c)
    w_bd = (w_fnet[:, :, None, :] * jnp.asarray(eye)[:, None, :, None]).reshape(D_FN, D_FN)
    out = jax.ShapeDtypeStruct((D_FN, D_FN), BF16)
    return pl.pallas_call(_fnet_w_kernel, out_shape=(out, out), name="fnet_weights")(
        jnp.asarray(cos_bd), jnp.asarray(sin_bd), w_bd)


def _gelu_tanh(x):
    return 0.5 * x * (1.0 + jnp.tanh(math.sqrt(2.0 / math.pi) * (x + 0.044715 * (x * x * x))))


def _post_kernel(y_ref, yf_ref, xp_ref, xs_ref, pos_ref, mod_ref, wglu_ref, wout_ref, g2_ref, wr_ref,
                 br_ref, permt_ref, x1_ref, h2_ref, comb_ref):
    g = pl.program_id(0)
    z = _gelu_tanh(y_ref[0])
    gate = jnp.dot(z.astype(BF16), wglu_ref[...].astype(BF16), preferred_element_type=F32)
    gl = (z * jax.nn.sigmoid(gate)).astype(BF16)
    permt = permt_ref[...]
    n_q = TC // PERM_T
    nat = [jnp.dot(permt, gl[q * SUB * PERM_T:(q + 1) * SUB * PERM_T], preferred_element_type=F32).astype(BF16)
           for q in range(n_q)]
    gl_nat = jnp.concatenate(
        [nat[q][s * PERM_T:(s + 1) * PERM_T] for s in range(SUB) for q in range(n_q)], axis=0)
    w_out = wout_ref[...].astype(BF16)
    mixed = (jnp.dot(gl_nat, w_out[:D_S5], preferred_element_type=F32)
             + jnp.dot(yf_ref[0].reshape(SUB * TC, D_FN), w_out[D_S5:], preferred_element_type=F32))
    x = _load_x(g, xp_ref, xs_ref, pos_ref)
    mod = mod_ref[0]
    gate1 = mod[:, :, 2 * D:3 * D]
    shift2 = mod[:, :, 3 * D:4 * D]
    scale2 = mod[:, :, 4 * D:5 * D]
    x1 = x + gate1 * mixed.reshape(SUB, TC, D)
    x1_ref[0] = x1
    h2 = _rms(x1, g2_ref[...]) * (1.0 + scale2) + shift2
    h2_ref[0] = h2.astype(BF16)

    hr = h2.reshape(SUB * TC, D)
    h_hi = hr.astype(BF16)
    h_lo = (hr - h_hi.astype(F32)).astype(BF16)
    wr = wr_ref[...]
    w_hi = wr.astype(BF16)
    w_lo = (wr - w_hi.astype(F32)).astype(BF16)
    logits = (jnp.dot(h_hi, w_hi, preferred_element_type=F32) + jnp.dot(h_lo, w_hi, preferred_element_type=F32)
              + jnp.dot(h_hi, w_lo, preferred_element_type=F32) + br_ref[...])
    lane = lax.broadcasted_iota(jnp.int32, logits.shape, 1).astype(F32)
    top_v, hots = [], []
    cur = logits
    for _ in range(TOP_K):
        m = jnp.max(cur, axis=-1, keepdims=True)
        idx = jnp.min(jnp.where(cur == m, lane, float(E_PAD)), axis=-1, keepdims=True)
        hot = lane == idx
        top_v.append(m)
        hots.append(hot)
        cur = jnp.where(hot, -3.0e38, cur)
    exps = [jnp.exp(v - top_v[0]) for v in top_v]
    denom = exps[0] + exps[1] + exps[2] + exps[3]
    comb = jnp.zeros(logits.shape, F32)
    for k in range(TOP_K):
        comb = comb + jnp.where(hots[k], exps[k] / denom, 0.0)
    comb_ref[0] = comb.reshape(SUB, TC, E_PAD)


def _post(y, yf, xp4, xs4, pos3, modg, w_glu, w_out, norm2_g, w_router, b_router, permt):
    xp_spec, xs_spec, pos_spec = _x_specs()
    const2 = lambda g, tc: (0, 0)
    wr = jnp.zeros((D, E_PAD), F32).at[:, :N_EXPERTS].set(w_router)
    br = jnp.full((1, E_PAD), -1.0e30, F32).at[0, :N_EXPERTS].set(b_router)
    blk = lambda w: pl.BlockSpec((1, SUB, TC, w), lambda g, tc: (g, 0, tc, 0))
    return pl.pallas_call(
        _post_kernel,
        grid=(N_GRP, N_TC),
        in_specs=[pl.BlockSpec((1, SUB * TC, D_S5), lambda g, tc: (g, tc, 0)),
                  blk(D_FN), xp_spec, xs_spec, pos_spec,
                  pl.BlockSpec((1, SUB, 1, 6 * D), lambda g, tc: (g, 0, 0, 0)),
                  pl.BlockSpec((D_S5, D_S5), const2),
                  pl.BlockSpec((D, D), const2),
                  pl.BlockSpec((1, D), const2),
                  pl.BlockSpec((D, E_PAD), const2),
                  pl.BlockSpec((1, E_PAD), const2),
                  pl.BlockSpec((SUB * PERM_T, SUB * PERM_T), const2)],
        out_specs=[blk(D), blk(D), blk(E_PAD)],
        out_shape=[jax.ShapeDtypeStruct((N_GRP, SUB, L_BLK, D), F32),
                   jax.ShapeDtypeStruct((N_GRP, SUB, L_BLK, D), BF16),
                   jax.ShapeDtypeStruct((N_GRP, SUB, L_BLK, E_PAD), F32)],
        compiler_params=_cparams(("arbitrary", "arbitrary")),
        name="post_mixer",
    )(y, yf, xp4, xs4, pos3, modg, w_glu, w_out, norm2_g.reshape(1, D), wr, br, permt)


TBLK = L_BLK
N_BLK = T_TOK // TBLK
N_CTX_BLK = (N_GRP - 1) * SUB
SEG_ALIGN = 8
RB = 1280
assert RB >= TBLK * TOP_K + N_EXPERTS * (SEG_ALIGN - 1) and RB % 128 == 0
TM = 256
TM_SHIFT = 8
assert 1 << TM_SHIFT == TM
R_TOT = T_TOK * TOP_K + N_BLK * N_EXPERTS * (SEG_ALIGN - 1) + TM
TAB_ROWS = 32
assert TAB_ROWS >= N_BLK
BIG_ROWS = 32
BIG_SHIFT = 2
assert SEG_ALIGN << BIG_SHIFT == BIG_ROWS


def _plan_kernel(comb_ref, z_ref, ptab_ref, loff_ref, gst_ref, eoff_ref):
    row = lax.broadcasted_iota(jnp.int32, (TBLK, TBLK), 0)
    col = lax.broadcasted_iota(jnp.int32, (TBLK, TBLK), 1)
    earlier = jnp.where(row > col, 1.0, 0.0).astype(BF16)
    ptab_ref[...] = jnp.zeros_like(ptab_ref)

    def body(b, _):
        rows = pl.ds(pl.multiple_of(b * TBLK, TBLK), TBLK)
        m = jnp.where(comb_ref[rows, :] > 0.0, 1.0, 0.0)
        rank = jnp.dot(earlier, m.astype(BF16), preferred_element_type=F32)
        z_ref[rows, :] = m * (rank + 1.0)
        n = jnp.sum(m, axis=0, keepdims=True)
        ptab_ref[pl.ds(b, 1), :] = jnp.floor((n + (SEG_ALIGN - 1)) * (1.0 / SEG_ALIGN)) * SEG_ALIGN
        return 0

    lax.fori_loop(0, N_BLK, body, 0)

    ptab = ptab_ref[...]
    er = lax.broadcasted_iota(jnp.int32, (E_PAD, E_PAD), 0)
    ec = lax.broadcasted_iota(jnp.int32, (E_PAD, E_PAD), 1)
    before = jnp.where(er < ec, 1.0, 0.0)
    exact = functools.partial(jnp.dot, preferred_element_type=F32, precision=HIGHEST)
    loff_ref[...] = exact(ptab, before)
    tot = jnp.sum(ptab, axis=0, keepdims=True)
    eoff = exact(jnp.broadcast_to(tot, (SUB, E_PAD)), before)
    eoff_ref[...] = eoff
    br = lax.broadcasted_iota(jnp.int32, (TAB_ROWS, TAB_ROWS), 0)
    bc = lax.broadcasted_iota(jnp.int32, (TAB_ROWS, TAB_ROWS), 1)
    gst_ref[...] = eoff[0:1] + exact(jnp.where(br > bc, 1.0, 0.0), ptab)


def _plan(comb):
    tab = jax.ShapeDtypeStruct((TAB_ROWS, E_PAD), F32)
    return pl.pallas_call(
        _plan_kernel,
        out_shape=(jax.ShapeDtypeStruct((T_TOK, E_PAD), F32), tab, tab, tab,
                   jax.ShapeDtypeStruct((SUB, E_PAD), F32)),
        compiler_params=pltpu.CompilerParams(vmem_limit_bytes=VMEM_LIMIT),
        name="moe_plan",
    )(comb)


def _sort_matrix(b, z_ref, loff_ref, ptab_ref):
    loff = loff_ref[pl.ds(b, 1), :]
    size = ptab_ref[pl.ds(b, 1), :]
    r = lax.broadcasted_iota(jnp.int32, (RB, E_PAD), 0).astype(F32)
    owner = jnp.where(r >= loff, jnp.where(r < loff + size, 1.0, 0.0), 0.0)
    rank1 = r[:, 0:1] - jnp.sum(owner * loff, axis=-1, keepdims=True) + 1.0
    zt = z_ref[...].T
    v = jnp.dot(owner.astype(BF16), zt.astype(BF16), preferred_element_type=F32)
    return jnp.where(v == rank1, 1.0, 0.0), owner


def _segment_copies(b, loff_s, gst_s, p8_s, make_copy):
    def per_expert(e, counts):
        k = b * N_EXPERTS + e
        chunks = p8_s[k]
        n_big = lax.shift_right_logical(chunks, BIG_SHIFT)
        n_small = chunks - n_big * (BIG_ROWS // SEG_ALIGN)
        local0 = loff_s[k]
        global0 = gst_s[k]

        def big(j, _):
            make_copy(pl.multiple_of(local0 + j * BIG_ROWS, SEG_ALIGN),
                      pl.multiple_of(global0 + j * BIG_ROWS, SEG_ALIGN), BIG_ROWS).start()
            return 0

        def small(j, _):
            off = n_big * BIG_ROWS + j * SEG_ALIGN
            make_copy(pl.multiple_of(local0 + off, SEG_ALIGN),
                      pl.multiple_of(global0 + off, SEG_ALIGN), SEG_ALIGN).start()
            return 0

        lax.fori_loop(0, n_big, big, 0)
        lax.fori_loop(0, n_small, small, 0)
        return counts[0] + n_big, counts[1] + n_small

    return lax.fori_loop(0, N_EXPERTS, per_expert, (0, 0))


def _wait_copies(counts, make_copy):
    for n, rows in zip(counts, (BIG_ROWS, SEG_ALIGN)):
        def wait_one(i, _, rows=rows):
            make_copy(0, 0, rows).wait()
            return 0
        lax.fori_loop(0, n, wait_one, 0)


def _zero_rows_from(hbm, zeros_vmem, first_row, sem):
    def copy_to(row):
        return pltpu.make_async_copy(zeros_vmem, hbm.at[pl.ds(row, TM)], sem)

    n_full = lax.shift_right_logical(R_TOT - first_row, TM_SHIFT)

    def start_one(j, _):
        copy_to(pl.multiple_of(first_row + j * TM, SEG_ALIGN)).start()
        return 0

    def wait_one(j, _):
        copy_to(0).wait()
        return 0

    lax.fori_loop(0, n_full, start_one, 0)
    lax.fori_loop(0, n_full, wait_one, 0)
    last = copy_to(R_TOT - TM)
    last.start()
    last.wait()


def _dispatch_kernel(loff_s, gst_s, p8_s, tot_s, h_ref, z_ref, loff_ref, ptab_ref, xs_hbm, xbuf, sem, pending):
    b = pl.program_id(0)

    def make_copy(local_row, global_row, rows):
        return pltpu.make_async_copy(xbuf.at[pl.ds(local_row, rows)], xs_hbm.at[pl.ds(global_row, rows)], sem)

    pm = _sort_matrix(b, z_ref, loff_ref, ptab_ref)[0].astype(BF16)

    @pl.when(b > 0)
    def _():
        _wait_copies((pending[0], pending[1]), make_copy)

    xbuf[...] = jnp.dot(pm, h_ref[...], preferred_element_type=F32)
    n_big, n_small = _segment_copies(b, loff_s, gst_s, p8_s, make_copy)
    pending[0] = n_big
    pending[1] = n_small

    @pl.when(b == N_BLK - 1)
    def _():
        _wait_copies((n_big, n_small), make_copy)
        xbuf[0:TM, :] = jnp.zeros((TM, D), F32)
        _zero_rows_from(xs_hbm, xbuf.at[0:TM], tot_s[0], sem)


def _dispatch(h2, z, loff, ptab, loff_i, gst_i, p8_i, tot_i):
    whole = pl.BlockSpec((TAB_ROWS, E_PAD), lambda b, *_: (0, 0))
    return pl.pallas_call(
        _dispatch_kernel,
        grid_spec=pltpu.PrefetchScalarGridSpec(
            num_scalar_prefetch=4,
            grid=(N_BLK,),
            in_specs=[pl.BlockSpec((TBLK, D), lambda b, *_: (b, 0)),
                      pl.BlockSpec((TBLK, E_PAD), lambda b, *_: (b, 0)),
                      whole, whole],
            out_specs=pl.BlockSpec(memory_space=pl.ANY),
            scratch_shapes=[pltpu.VMEM((RB, D), F32), pltpu.SemaphoreType.DMA(()),
                            pltpu.SMEM((2,), jnp.int32)]),
        out_shape=jax.ShapeDtypeStruct((R_TOT, D), F32),
        compiler_params=_cparams(("arbitrary",)),
        name="moe_dispatch",
    )(loff_i, gst_i, p8_i, tot_i, h2, z, loff, ptab)


MAX_TILES = R_TOT // TM + N_EXPERTS
X_AHEAD = 3
X_SLOTS = X_AHEAD + 1
Y_SLOTS = 2
N_MATS = 3
FF_CHUNK = 256
CAST_ROWS = 64


def _expert_kernel(eoff_s, xs_hbm, wg_hbm, wu_hbm, wd_hbm, bg_ref, bu_ref, bd_ref, ys_hbm,
                   wst, wbf, xin, yout, act_ref, w_sem, x_sem, y_sem, t_exp, t_row, t_first, t_valid, live):
    def add_expert(e, carry):
        n_t, n_live = carry
        start = eoff_s[e]
        count = eoff_s[e + 1] - start
        tiles = lax.shift_right_logical(count + (TM - 1), TM_SHIFT)

        def add_tile(i, _):
            t_exp[n_t + i] = e
            t_row[n_t + i] = start + i * TM
            t_first[n_t + i] = jnp.where(i == 0, 1, 0)
            t_valid[n_t + i] = jnp.minimum(count - i * TM, TM)
            return 0

        lax.fori_loop(0, tiles, add_tile, 0)
        has_rows = jnp.where(tiles > 0, 1, 0)

        @pl.when(tiles > 0)
        def _():
            live[n_live] = e

        return n_t + tiles, n_live + has_rows

    n_tiles, n_live = lax.fori_loop(0, N_EXPERTS, add_expert, (0, 0))

    def w_copies(e, slot):
        return [pltpu.make_async_copy(w.at[e], wst.at[slot, m], w_sem.at[slot])
                for m, w in enumerate((wg_hbm, wu_hbm, wd_hbm))]

    def tile_rows(j):
        return pl.ds(pl.multiple_of(t_row[j], SEG_ALIGN), TM)

    def x_copy(j, slot):
        return pltpu.make_async_copy(xs_hbm.at[tile_rows(j)], xin.at[slot], x_sem.at[slot])

    def y_pieces(j, op):
        ys = lax.rem(j, Y_SLOTS)
        valid = t_valid[j]
        row0 = t_row[j]

        def piece(off, rows):
            return pltpu.make_async_copy(yout.at[ys, pl.ds(off, rows)],
                                         ys_hbm.at[pl.ds(pl.multiple_of(row0 + off, SEG_ALIGN), rows)],
                                         y_sem.at[ys])

        @pl.when(valid == TM)
        def _():
            op(piece(0, TM))

        @pl.when(valid < TM)
        def _():
            n_big = lax.shift_right_logical(valid, BIG_ROWS.bit_length() - 1)
            n_small = lax.shift_right_logical(valid - n_big * BIG_ROWS, SEG_ALIGN.bit_length() - 1)

            def big(i, _):
                op(piece(pl.multiple_of(i * BIG_ROWS, SEG_ALIGN), BIG_ROWS))
                return 0

            def small(i, _):
                op(piece(pl.multiple_of(n_big * BIG_ROWS + i * SEG_ALIGN, SEG_ALIGN), SEG_ALIGN))
                return 0

            lax.fori_loop(0, n_big, big, 0)
            lax.fori_loop(0, n_small, small, 0)

    @pl.when(n_tiles > 0)
    def _():
        for cp in w_copies(live[0], 0):
            cp.start()

    for ahead in range(X_AHEAD):
        @pl.when(n_tiles > ahead)
        def _(ahead=ahead):
            x_copy(ahead, ahead).start()

    def body(j, k):
        e = t_exp[j]

        @pl.when(t_first[j] == 1)
        def _():
            ws = lax.rem(k, 2)

            @pl.when(k + 1 < n_live)
            def _():
                for cp in w_copies(live[k + 1], 1 - ws):
                    cp.start()

            for cp in w_copies(e, ws):
                cp.wait()

            def cast_rows(r, _):
                rows = pl.ds(pl.multiple_of(r * CAST_ROWS, CAST_ROWS), CAST_ROWS)
                for m in range(N_MATS):
                    wbf[m, rows, :] = wst[ws, m, rows, :].astype(BF16)
                return 0

            lax.fori_loop(0, D // CAST_ROWS, cast_rows, 0)

        slot = lax.rem(j, X_SLOTS)
        x_copy(j, slot).wait()

        @pl.when(j + X_AHEAD < n_tiles)
        def _():
            x_copy(j + X_AHEAD, lax.rem(j + X_AHEAD, X_SLOTS)).start()

        x = xin[slot].astype(BF16)
        for c in range(D // FF_CHUNK):
            cols = slice(c * FF_CHUNK, (c + 1) * FF_CHUNK)
            gate = jnp.dot(x, wbf[0, :, cols], preferred_element_type=F32) + bg_ref[e][:, cols]
            up = jnp.dot(x, wbf[1, :, cols], preferred_element_type=F32) + bu_ref[e][:, cols]
            gate = jnp.minimum(gate, SWIGLU_LIMIT)
            up = jnp.clip(up, -SWIGLU_LIMIT, SWIGLU_LIMIT)
            act_ref[:, cols] = ((up + 1.0) * gate * jax.nn.sigmoid(SWIGLU_ALPHA * gate)).astype(BF16)
        y = jnp.dot(act_ref[...], wbf[2], preferred_element_type=F32) + bd_ref[e]

        @pl.when(j >= Y_SLOTS)
        def _():
            y_pieces(j - Y_SLOTS, lambda cp: cp.wait())

        yout[lax.rem(j, Y_SLOTS)] = y
        y_pieces(j, lambda cp: cp.start())
        return k + t_first[j]

    lax.fori_loop(0, n_tiles, body, 0)

    for back in range(Y_SLOTS, 0, -1):
        @pl.when(n_tiles >= back)
        def _(back=back):
            y_pieces(n_tiles - back, lambda cp: cp.wait())

    yout[0] = jnp.zeros((TM, D), F32)
    _zero_rows_from(ys_hbm, yout.at[0], eoff_s[N_EXPERTS], y_sem.at[0])


def _experts(xs, eoff_i, w_gate, b_gate, w_up, b_up, w_down, b_down):
    hbm = pl.BlockSpec(memory_space=pl.ANY)
    bspec = pl.BlockSpec((N_EXPERTS, 1, D), lambda i, *_: (0, 0, 0))
    return pl.pallas_call(
        _expert_kernel,
        grid_spec=pltpu.PrefetchScalarGridSpec(
            num_scalar_prefetch=1,
            grid=(1,),
            in_specs=[hbm, hbm, hbm, hbm, bspec, bspec, bspec],
            out_specs=hbm,
            scratch_shapes=[pltpu.VMEM((2, N_MATS, D, D), F32), pltpu.VMEM((N_MATS, D, D), BF16),
                            pltpu.VMEM((X_SLOTS, TM, D), F32), pltpu.VMEM((Y_SLOTS, TM, D), F32),
                            pltpu.VMEM((TM, D), BF16),
                            pltpu.SemaphoreType.DMA((2,)), pltpu.SemaphoreType.DMA((X_SLOTS,)),
                            pltpu.SemaphoreType.DMA((Y_SLOTS,)),
                            pltpu.SMEM((MAX_TILES,), jnp.int32), pltpu.SMEM((MAX_TILES,), jnp.int32),
                            pltpu.SMEM((MAX_TILES,), jnp.int32), pltpu.SMEM((MAX_TILES,), jnp.int32),
                            pltpu.SMEM((N_EXPERTS,), jnp.int32)]),
        out_shape=jax.ShapeDtypeStruct((R_TOT, D), F32),
        compiler_params=_cparams(("arbitrary",)),
        name="moe_experts",
    )(eoff_i, xs, w_gate, w_up, w_down, b_gate.reshape(N_EXPERTS, 1, D), b_up.reshape(N_EXPERTS, 1, D),
      b_down.reshape(N_EXPERTS, 1, D))


def _combine_kernel(loff_s, gst_s, p8_s, ys_hbm, z_ref, comb_ref, loff_ref, ptab_ref, x1_ref, mod_ref, gf_ref,
                    yctx_ref, ylat_ref, ybuf, sem):
    b = pl.program_id(0)

    def make_copy(local_row, global_row, rows):
        return pltpu.make_async_copy(ys_hbm.at[pl.ds(global_row, rows)], ybuf.at[pl.ds(local_row, rows)], sem)

    started = _segment_copies(b, loff_s, gst_s, p8_s, make_copy)

    last = b * N_EXPERTS + N_EXPERTS - 1
    used_chunks = lax.shift_right_logical(loff_s[last], 3) + p8_s[last]

    def zero_chunk(j, _):
        ybuf[pl.ds(pl.multiple_of(j * SEG_ALIGN, SEG_ALIGN), SEG_ALIGN), :] = jnp.zeros((SEG_ALIGN, D), F32)
        return 0

    lax.fori_loop(used_chunks, RB // SEG_ALIGN, zero_chunk, 0)

    pm, owner = _sort_matrix(b, z_ref, loff_ref, ptab_ref)
    w_all = jnp.dot(owner, comb_ref[...].T, preferred_element_type=F32, precision=HIGHEST)
    w_row = jnp.sum(pm * w_all, axis=-1, keepdims=True)
    pmt = pm.T.astype(BF16)
    _wait_copies(started, make_copy)
    y = (ybuf[...] * w_row).astype(BF16)
    moe = jnp.dot(pmt, y, preferred_element_type=F32)
    gate2 = mod_ref[0][:, 5 * D:6 * D]
    x2 = x1_ref[0] + gate2 * moe
    y = _rms(x2, gf_ref[...])

    @pl.when(b < N_CTX_BLK)
    def _():
        yctx_ref[0] = y

    @pl.when(b >= N_CTX_BLK)
    def _():
        ylat_ref[0] = y


def _combine(ys, z, comb, loff, ptab, x1, modv, norm_f_g, loff_i, gst_i, p8_i):
    whole = pl.BlockSpec((TAB_ROWS, E_PAD), lambda b, *_: (0, 0))
    tok = pl.BlockSpec((TBLK, E_PAD), lambda b, *_: (b, 0))
    return pl.pallas_call(
        _combine_kernel,
        grid_spec=pltpu.PrefetchScalarGridSpec(
            num_scalar_prefetch=3,
            grid=(N_BLK,),
            in_specs=[pl.BlockSpec(memory_space=pl.ANY), tok, tok, whole, whole,
                      pl.BlockSpec((1, TBLK, D), lambda b, *_: (b, 0, 0)),
                      pl.BlockSpec((1, 1, 6 * D), lambda b, *_: (b, 0, 0)),
                      pl.BlockSpec((1, D), lambda b, *_: (0, 0))],
            out_specs=[pl.BlockSpec((1, TBLK, D), lambda b, *_: (jnp.minimum(b, N_CTX_BLK - 1), 0, 0)),
                       pl.BlockSpec((1, TBLK, D), lambda b, *_: (jnp.maximum(b - N_CTX_BLK, 0), 0, 0))],
            scratch_shapes=[pltpu.VMEM((RB, D), F32), pltpu.SemaphoreType.DMA(())]),
        out_shape=[jax.ShapeDtypeStruct((N_CTX_BLK, TBLK, D), F32),
                   jax.ShapeDtypeStruct((N_BLK - N_CTX_BLK, TBLK, D), F32)],
        compiler_params=_cparams(("arbitrary",)),
        name="moe_combine",
    )(loff_i, gst_i, p8_i, ys, z, comb, loff, ptab, x1, modv, norm_f_g.reshape(1, D))


def _moe_and_final(x1, h2, comb, modg, norm_f_g, w_gate, b_gate, w_up, b_up, w_down, b_down):
    z, ptab, loff, gst, eoff = _plan(comb)
    as_scalars = lambda t: t[:N_BLK, :N_EXPERTS].astype(jnp.int32).reshape(N_BLK * N_EXPERTS)
    loff_i, gst_i = as_scalars(loff), as_scalars(gst)
    p8_i = as_scalars(ptab) // SEG_ALIGN
    eoff_i = eoff[0, :N_EXPERTS + 1].astype(jnp.int32)
    xs = _dispatch(h2, z, loff, ptab, loff_i, gst_i, p8_i, eoff_i[N_EXPERTS:])
    ys = _experts(xs, eoff_i, w_gate, b_gate, w_up, b_up, w_down, b_down)
    return _combine(ys, z, comb, loff, ptab, x1, modg.reshape(N_BLK, 1, 6 * D), norm_f_g,
                    loff_i, gst_i, p8_i)


def _grid_pos_embed(n_tokens, dim):
    rows = n_tokens // GRID_W
    t = np.arange(rows * GRID_W)
    r = (t // GRID_W).astype(np.float32)
    col = (t % GRID_W).astype(np.float32)
    q = dim // 4
    omega = (1.0 / np.float32(POS_TEMP) ** (np.arange(q, dtype=np.float32) / np.float32(q))).astype(np.float32)

    def emb(p):
        a = p[:, None] * omega[None, :]
        return np.concatenate([np.sin(a), np.cos(a)], axis=-1)

    return np.concatenate([emb(r), emb(col)], axis=-1).astype(np.float32)


def kernel(x_prompt, x_sample, c, state_s5_re, state_s5_im, c_ctx, w_ada, b_ada, norm1_g, w_in, s5_lam_re,
           s5_lam_im, s5_log_dt, s5_b_re, s5_b_im, s5_c_re, s5_c_im, s5_d, s5_w_glu, w_fnet, w_out, norm2_g,
           w_router, b_router, w_gate, b_gate, w_up, b_up, w_down, b_down, norm_f_g):
    n_ctx, n_lat = x_prompt.shape[0], x_sample.shape[0]
    assert x_prompt.shape == (SUB * (N_GRP - 1), L_BLK, D) and x_sample.shape == (2, LAT_CHUNKS * L_BLK, D)
    assert w_ada.shape[0] == 1, "one trunk layer"
    layer = 0

    cvec = jnp.concatenate([jnp.broadcast_to(c_ctx[None], (n_ctx, D)), jnp.repeat(c, LAT_CHUNKS, axis=0)], axis=0)
    modg = _adaln(cvec, w_ada[layer], b_ada[layer]).reshape(N_GRP, SUB, 1, 6 * D)

    xp4 = x_prompt.reshape(N_GRP - 1, SUB, L_BLK, D)
    xs4 = x_sample.reshape(1, SUB, L_BLK, D)
    pos3 = jnp.asarray(_grid_pos_embed(LAT_CHUNKS * L_BLK, D).reshape(LAT_CHUNKS, L_BLK, D))
    perm = _perm_matrix()

    us, uf = _pre(xp4, xs4, pos3, modg, norm1_g[layer], w_in[layer], jnp.asarray(perm, BF16))

    bbmat, ccmat, a8 = _s5_params(s5_lam_re[layer], s5_lam_im[layer], s5_log_dt[layer], s5_b_re[layer],
                                  s5_b_im[layer], s5_c_re[layer], s5_c_im[layer])

    st = jnp.stack([state_s5_re[:, layer], state_s5_im[:, layer]], axis=2)
    st = st.reshape(n_lat, 2, 2, N_S5_CHUNKS, CH_P)
    st = jnp.transpose(st, (1, 3, 0, 2, 4)).reshape(2, N_S5_CHUNKS, n_lat, 2 * CH_P)
    init8 = jnp.zeros((2, N_S5_CHUNKS, n_lat, LAT_CHUNKS, 2 * CH_P), F32)
    init8 = init8.at[0, :, :, 0].set(st[0]).at[1, :, :, LAT_CHUNKS - 1].set(st[1])
    init8 = init8.reshape(2, N_S5_CHUNKS, SUB, 2 * CH_P)

    y_s5, fin = _s5(us, bbmat, ccmat, a8, init8, s5_d[layer])

    m1, m2 = _fnet_weights(w_fnet[layer])
    yf = _fnet(uf, m1, m2)

    x1, h2, comb = _post(y_s5, yf, xp4, xs4, pos3, modg, s5_w_glu[layer], w_out[layer], norm2_g[layer],
                         w_router[layer], b_router[layer], jnp.asarray(perm.T, BF16))

    y_prompt, y_lat = _moe_and_final(x1.reshape(N_BLK, TBLK, D), h2.reshape(T_TOK, D), comb.reshape(T_TOK, E_PAD),
                                     modg, norm_f_g, w_gate[layer], b_gate[layer], w_up[layer], b_up[layer],
                                     w_down[layer], b_down[layer])
    y_sample = y_lat.reshape(n_lat, LAT_CHUNKS * L_BLK, D)

    fin = fin[:, :N_GRP - 1].reshape(2, N_GRP - 1, SUB, N_S5_CHUNKS, 2, CH_P)
    fin = jnp.transpose(fin, (4, 1, 2, 0, 3, 5)).reshape(2, n_ctx, 1, 2, N_S5_GROUPS, S5_P)
    return (y_prompt, y_sample, fin[0], fin[1])
```

```python
import functools
import math

import numpy as np
import jax
import jax.numpy as jnp
from jax import lax
from jax.experimental import pallas as pl
from jax.experimental.pallas import tpu as pltpu

F32 = jnp.float32
BF16 = jnp.bfloat16
HIGHEST = lax.Precision.HIGHEST

D = 1024
D_S5 = 768
S5_H = 16
S5_P = 64
N_S5_GROUPS = 48
D_FN = 256
FN_GW = 64
N_EXPERTS = 32
TOP_K = 4
E_PAD = 128
SWIGLU_LIMIT = 7.0
SWIGLU_ALPHA = 1.702
RMS_EPS = 1e-6
POS_TEMP = 10000.0
GRID_W = 64

L_BLK = 256
SUB = 8
N_GRP = 3
GRP_ROWS = L_BLK * SUB
LAT_CHUNKS = 4
T_TOK = N_GRP * GRP_ROWS
TC = 128
N_TC = L_BLK // TC
PERM_T = 32
S5_CHUNK = 16
N_S5_CHUNKS = N_S5_GROUPS // S5_CHUNK
CH_U = S5_CHUNK * S5_H
CH_P = S5_CHUNK * S5_P
ROW_BLK = 512
FIX_PARTS = 1
VMEM_LIMIT = 56 * 1024 * 1024


def _cparams(sem):
    return pltpu.CompilerParams(dimension_semantics=sem, vmem_limit_bytes=VMEM_LIMIT)


def _adaln_kernel(c_ref, w_ref, b_ref, o_ref):
    c = c_ref[...]
    s = c * jax.nn.sigmoid(c)
    s_hi = s.astype(BF16)
    s_lo = (s - s_hi.astype(F32)).astype(BF16)
    w = w_ref[...]
    w_hi = w.astype(BF16)
    w_lo = (w - w_hi.astype(F32)).astype(BF16)
    o_ref[...] = (jnp.dot(s_hi, w_hi, preferred_element_type=F32) + jnp.dot(s_lo, w_hi, preferred_element_type=F32)
                  + jnp.dot(s_hi, w_lo, preferred_element_type=F32) + b_ref[...])


def _adaln(cvec, w_ada, b_ada):
    n = w_ada.shape[1]
    rows = cvec.shape[0]
    return pl.pallas_call(
        _adaln_kernel,
        grid=(n // D,),
        in_specs=[pl.BlockSpec((rows, D), lambda j: (0, 0)),
                  pl.BlockSpec((D, D), lambda j: (0, j)),
                  pl.BlockSpec((1, D), lambda j: (0, j))],
        out_specs=pl.BlockSpec((rows, D), lambda j: (0, j)),
        out_shape=jax.ShapeDtypeStruct((rows, n), F32),
        compiler_params=_cparams(("arbitrary",)),
        name="adaln",
    )(cvec, w_ada, b_ada.reshape(1, n))


def _zoh(lam_re, lam_im, log_dt):
    dt = jnp.exp(log_dt)
    a_re = jnp.minimum(lam_re, -1e-4)
    a_im = lam_im
    mag = jnp.exp(a_re * dt)
    ab_re = mag * jnp.cos(a_im * dt)
    ab_im = mag * jnp.sin(a_im * dt)
    den = a_re * a_re + a_im * a_im
    nr = ab_re - 1.0
    f_re = (nr * a_re + ab_im * a_im) / den
    f_im = (ab_im * a_re - nr * a_im) / den
    return ab_re, ab_im, f_re, f_im


def _s5_params_kernel(lre3_ref, lim3_ref, ldt3_ref, lre2_ref, lim2_ref, ldt2_ref, bre_ref, bim_ref,
                      cre_ref, cim_ref, bb_ref, cc_ref, a_ref):
    exact = functools.partial(jnp.dot, preferred_element_type=F32, precision=HIGHEST)
    spread = jnp.where(lax.broadcasted_iota(jnp.int32, (S5_P, CH_P), 1) % S5_P
                       == lax.broadcasted_iota(jnp.int32, (S5_P, CH_P), 0), 1.0, 0.0)

    def block_diag(m):
        wide = jnp.dot(m.reshape(CH_U, S5_P).astype(BF16), spread.astype(BF16), preferred_element_type=F32)
        row_g = lax.broadcasted_iota(jnp.int32, (CH_U, CH_P), 0) // S5_H
        col_g = lax.broadcasted_iota(jnp.int32, (CH_U, CH_P), 1) // S5_P
        return jnp.where(row_g == col_g, wide, 0.0)

    _, _, f_re, f_im = _zoh(lre3_ref[0, 0], lim3_ref[0, 0], ldt3_ref[0, 0])
    b_re = bre_ref[0, 0]
    b_im = bim_ref[0, 0]
    bb_ref[0, 0] = jnp.concatenate([block_diag(f_re * b_re - f_im * b_im),
                                    block_diag(f_re * b_im + f_im * b_re)], axis=1).astype(BF16)
    cc_ref[0, 0] = jnp.concatenate([block_diag(cre_ref[0, 0]).T,
                                    -block_diag(cim_ref[0, 0]).T], axis=0).astype(BF16)

    ab_re, ab_im, _, _ = _zoh(lre2_ref[0, 0], lim2_ref[0, 0], ldt2_ref[0, 0])
    row_g = lax.broadcasted_iota(jnp.int32, (S5_CHUNK, CH_P), 0)
    col_g = lax.broadcasted_iota(jnp.int32, (S5_CHUNK, CH_P), 1) // S5_P

    def lane_row(a):
        flat = jnp.sum(jnp.where(row_g == col_g, exact(a, spread), 0.0), axis=0, keepdims=True)
        return jnp.broadcast_to(flat, (SUB, CH_P))

    a_ref[0, 0] = jnp.concatenate([lane_row(ab_re), lane_row(ab_im)], axis=1)


def _s5_params(lam_re, lam_im, log_dt, b_re, b_im, c_re, c_im):
    nc = N_S5_CHUNKS
    ldt = jnp.broadcast_to(log_dt[:, :, None], (2, N_S5_GROUPS, S5_P))
    g3 = lambda a: a.reshape(2, nc, S5_CHUNK, 1, S5_P)
    g2 = lambda a: a.reshape(2, nc, S5_CHUNK, S5_P)
    ghp = lambda a: a.reshape(2, nc, S5_CHUNK, S5_H, S5_P)
    spec3 = pl.BlockSpec((1, 1, S5_CHUNK, 1, S5_P), lambda d, c: (d, c, 0, 0, 0))
    spec2 = pl.BlockSpec((1, 1, S5_CHUNK, S5_P), lambda d, c: (d, c, 0, 0))
    spec_ghp = pl.BlockSpec((1, 1, S5_CHUNK, S5_H, S5_P), lambda d, c: (d, c, 0, 0, 0))
    return pl.pallas_call(
        _s5_params_kernel,
        grid=(2, nc),
        in_specs=[spec3, spec3, spec3, spec2, spec2, spec2, spec_ghp, spec_ghp, spec_ghp, spec_ghp],
        out_specs=[pl.BlockSpec((1, 1, CH_U, 2 * CH_P), lambda d, c: (d, c, 0, 0)),
                   pl.BlockSpec((1, 1, 2 * CH_P, CH_U), lambda d, c: (d, c, 0, 0)),
                   pl.BlockSpec((1, 1, SUB, 2 * CH_P), lambda d, c: (d, c, 0, 0))],
        out_shape=[jax.ShapeDtypeStruct((2, nc, CH_U, 2 * CH_P), BF16),
                   jax.ShapeDtypeStruct((2, nc, 2 * CH_P, CH_U), BF16),
                   jax.ShapeDtypeStruct((2, nc, SUB, 2 * CH_P), F32)],
        compiler_params=_cparams(("arbitrary", "arbitrary")),
        name="s5_params",
    )(g3(lam_re), g3(lam_im), g3(ldt), g2(lam_re), g2(lam_im), g2(ldt),
      ghp(jnp.swapaxes(b_re, -1, -2)), ghp(jnp.swapaxes(b_im, -1, -2)), ghp(c_re), ghp(c_im))


def _rms(x, g):
    return x * lax.rsqrt(jnp.mean(x * x, axis=-1, keepdims=True) + RMS_EPS) * g


def _load_x(g, xp_ref, xs_ref, pos_ref):
    is_lat = g == N_GRP - 1
    pos = pos_ref[...]
    pos8 = jnp.concatenate([pos, pos], axis=0)
    return jnp.where(is_lat, xs_ref[0] + pos8, xp_ref[0])


def _x_specs():
    n_ctx = N_GRP - 1
    xp_spec = pl.BlockSpec(
        (1, SUB, TC, D),
        lambda g, tc: (jnp.minimum(g, n_ctx - 1), 0, jnp.where(g >= n_ctx, N_TC - 1, tc), 0))
    xs_spec = pl.BlockSpec((1, SUB, TC, D), lambda g, tc: (0, 0, jnp.where(g >= n_ctx, tc, 0), 0))
    pos_spec = pl.BlockSpec((LAT_CHUNKS, TC, D), lambda g, tc: (0, tc, 0))
    return xp_spec, xs_spec, pos_spec


def _perm_matrix():
    p = np.zeros((SUB * PERM_T, SUB * PERM_T), np.float32)
    for s in range(SUB):
        for j in range(PERM_T):
            p[j * SUB + s, s * PERM_T + j] = 1.0
    return p


def _pre_kernel(xp_ref, xs_ref, pos_ref, mod_ref, g1_ref, win_ref, perm_ref, us_ref, uf_ref):
    g = pl.program_id(0)
    x = _load_x(g, xp_ref, xs_ref, pos_ref)
    mod = mod_ref[0]
    shift1 = mod[:, :, 0:D]
    scale1 = mod[:, :, D:2 * D]
    h = _rms(x, g1_ref[...]) * (1.0 + scale1) + shift1
    h2d = h.reshape(SUB * TC, D).astype(BF16)
    proj = jnp.dot(h2d, win_ref[...].astype(BF16), preferred_element_type=F32)
    uf_ref[0] = proj[:, D_S5:].astype(BF16).reshape(SUB, TC, D_FN)
    u = proj[:, :D_S5].astype(BF16)
    perm = perm_ref[...]
    for q in range(TC // PERM_T):
        piece = jnp.concatenate(
            [u[s * TC + q * PERM_T: s * TC + (q + 1) * PERM_T] for s in range(SUB)], axis=0)
        us_ref[0, q * SUB * PERM_T:(q + 1) * SUB * PERM_T, :] = jnp.dot(
            perm, piece, preferred_element_type=F32).astype(BF16)


def _pre(xp4, xs4, pos3, modg, norm1_g, w_in, perm):
    xp_spec, xs_spec, pos_spec = _x_specs()
    return pl.pallas_call(
        _pre_kernel,
        grid=(N_GRP, N_TC),
        in_specs=[xp_spec, xs_spec, pos_spec,
                  pl.BlockSpec((1, SUB, 1, 6 * D), lambda g, tc: (g, 0, 0, 0)),
                  pl.BlockSpec((1, D), lambda g, tc: (0, 0)),
                  pl.BlockSpec((D, D), lambda g, tc: (0, 0)),
                  pl.BlockSpec((SUB * PERM_T, SUB * PERM_T), lambda g, tc: (0, 0))],
        out_specs=[pl.BlockSpec((1, SUB * TC, D_S5), lambda g, tc: (g, tc, 0)),
                   pl.BlockSpec((1, SUB, TC, D_FN), lambda g, tc: (g, 0, tc, 0))],
        out_shape=[jax.ShapeDtypeStruct((N_GRP, GRP_ROWS, D_S5), BF16),
                   jax.ShapeDtypeStruct((N_GRP, SUB, L_BLK, D_FN), BF16)],
        compiler_params=_cparams(("arbitrary", "arbitrary")),
        name="pre_mixer",
    )(xp4, xs4, pos3, modg, norm1_g.reshape(1, D), w_in, perm)


def _cmul(ar, ai, br, bi):
    return ar * br - ai * bi, ar * bi + ai * br


def _s5_kernel(us_ref, bb_ref, cc_ref, a_ref, init_ref, dskip_ref, y_ref, fin_ref, s_ref):
    g = pl.program_id(0)
    d = pl.program_id(2)
    n_rb = GRP_ROWS // ROW_BLK

    def proj_in(rb, _):
        rows = pl.ds(pl.multiple_of(rb * ROW_BLK, ROW_BLK), ROW_BLK)
        s_ref[rows, :] = jnp.dot(us_ref[0, rows, :], bb_ref[0, 0], preferred_element_type=F32)
        return 0

    lax.fori_loop(0, n_rb, proj_in, 0)

    def tile_row(k):
        t = k + d * (L_BLK - 1 - 2 * k)
        return pl.ds(pl.multiple_of(t * SUB, SUB), SUB)

    is_lat = g == N_GRP - 1
    re_all = slice(0, CH_P)
    im_all = slice(CH_P, 2 * CH_P)
    a_re = a_ref[0, 0, :, re_all]
    a_im = a_ref[0, 0, :, im_all]
    zero = jnp.zeros((SUB, CH_P), F32)
    s0_re = jnp.where(is_lat, init_ref[0, 0, :, re_all], zero)
    s0_im = jnp.where(is_lat, init_ref[0, 0, :, im_all], zero)

    def scan_body(i, carry):
        s_re, s_im = carry
        for j in range(8):
            rows = tile_row(i * 8 + j)
            n_re = a_re * s_re - a_im * s_im + s_ref[rows, re_all]
            n_im = a_re * s_im + a_im * s_re + s_ref[rows, im_all]
            s_ref[rows, re_all] = n_re
            s_ref[rows, im_all] = n_im
            s_re, s_im = n_re, n_im
        return s_re, s_im

    f_re, f_im = lax.fori_loop(0, L_BLK // 8, scan_body, (s0_re, s0_im))
    fin_ref[0, 0, :, re_all] = f_re
    fin_ref[0, 0, :, im_all] = f_im

    @pl.when(is_lat)
    def _():
        part = CH_P // FIX_PARTS
        sub_id = lax.broadcasted_iota(jnp.int32, (SUB, part), 0) % LAT_CHUNKS
        fwd = d == 0
        lo = jnp.where(fwd, 1, 0)
        hi = jnp.where(fwd, LAT_CHUNKS - 1, LAT_CHUNKS - 2)
        keep = (sub_id >= lo) & (sub_id <= hi)

        def from_prev(v):
            return jnp.where(keep, jnp.where(fwd, pltpu.roll(v, 1, 0), pltpu.roll(v, SUB - 1, 0)), 0.0)

        for hh in range(FIX_PARTS):
            lanes = slice(hh * part, (hh + 1) * part)
            re_cols = lanes
            im_cols = slice(CH_P + hh * part, CH_P + (hh + 1) * part)
            pa_re, pa_im = a_re[:, lanes], a_im[:, lanes]
            pf_re, pf_im = f_re[:, lanes], f_im[:, lanes]
            p_re, p_im = pa_re, pa_im
            for _ in range(8):
                p_re, p_im = _cmul(p_re, p_im, p_re, p_im)
            t_re, t_im = pf_re, pf_im
            for _ in range(LAT_CHUNKS - 2):
                m_re, m_im = _cmul(p_re, p_im, from_prev(t_re), from_prev(t_im))
                t_re, t_im = pf_re + m_re, pf_im + m_im
            c_re, c_im = from_prev(t_re), from_prev(t_im)

            def fix_body(i, carry, pa_re=pa_re, pa_im=pa_im, re_cols=re_cols, im_cols=im_cols):
                m_re, m_im = carry
                for j in range(8):
                    rows = tile_row(i * 8 + j)
                    m_re, m_im = _cmul(m_re, m_im, pa_re, pa_im)
                    s_ref[rows, re_cols] = s_ref[rows, re_cols] + m_re
                    s_ref[rows, im_cols] = s_ref[rows, im_cols] + m_im
                return m_re, m_im

            lax.fori_loop(0, L_BLK // 8, fix_body, (c_re, c_im))

    def proj_out(rb):
        rows = pl.ds(pl.multiple_of(rb * ROW_BLK, ROW_BLK), ROW_BLK)
        return rows, jnp.dot(s_ref[rows, :].astype(BF16), cc_ref[0, 0], preferred_element_type=F32)

    @pl.when(d == 0)
    def _():
        def body(rb, _):
            rows, yb = proj_out(rb)
            y_ref[0, rows, :] = dskip_ref[...] * us_ref[0, rows, :].astype(F32) + yb
            return 0
        lax.fori_loop(0, n_rb, body, 0)

    @pl.when(d == 1)
    def _():
        def body(rb, _):
            rows, yb = proj_out(rb)
            y_ref[0, rows, :] = y_ref[0, rows, :] + yb
            return 0
        lax.fori_loop(0, n_rb, body, 0)


def _s5(us, bbmat, ccmat, a8, init8, d_skip):
    return pl.pallas_call(
        _s5_kernel,
        grid=(N_GRP, N_S5_CHUNKS, 2),
        in_specs=[pl.BlockSpec((1, GRP_ROWS, CH_U), lambda g, c, d: (g, 0, c)),
                  pl.BlockSpec((1, 1, CH_U, 2 * CH_P), lambda g, c, d: (d, c, 0, 0)),
                  pl.BlockSpec((1, 1, 2 * CH_P, CH_U), lambda g, c, d: (d, c, 0, 0)),
                  pl.BlockSpec((1, 1, SUB, 2 * CH_P), lambda g, c, d: (d, c, 0, 0)),
                  pl.BlockSpec((1, 1, SUB, 2 * CH_P), lambda g, c, d: (d, c, 0, 0)),
                  pl.BlockSpec((1, CH_U), lambda g, c, d: (0, c))],
        out_specs=[pl.BlockSpec((1, GRP_ROWS, CH_U), lambda g, c, d: (g, 0, c)),
                   pl.BlockSpec((1, 1, SUB, 2 * CH_P), lambda g, c, d: (d, g, 0, c))],
        out_shape=[jax.ShapeDtypeStruct((N_GRP, GRP_ROWS, D_S5), F32),
                   jax.ShapeDtypeStruct((2, N_GRP, SUB, N_S5_CHUNKS * 2 * CH_P), F32)],
        scratch_shapes=[pltpu.VMEM((GRP_ROWS, 2 * CH_P), F32)],
        compiler_params=_cparams(("arbitrary", "arbitrary", "arbitrary")),
        name="s5_scan",
    )(us, bbmat, ccmat, a8, init8, d_skip.reshape(1, D_S5))


def _fnet_w_kernel(c_ref, s_ref, w_ref, m1_ref, m2_ref):
    w = w_ref[...]
    m1_ref[...] = jnp.dot(c_ref[...], w, preferred_element_type=F32, precision=HIGHEST).astype(BF16)
    m2_ref[...] = jnp.dot(s_ref[...], w, preferred_element_type=F32, precision=HIGHEST).astype(BF16)


FN_SEQ = LAT_CHUNKS
FN_CTX_STEPS = (N_GRP - 1) * SUB // FN_SEQ


def _fnet_kernel(u_ref, cos_s_ref, sin_s_ref, cos_l_ref, sin_l_ref, m1_ref, m2_ref, o_ref):
    i = pl.program_id(0)
    m1 = m1_ref[...]
    m2 = m2_ref[...]

    def mix(u, cos_ref, sin_ref):
        v1 = jnp.dot(u, m1, preferred_element_type=F32).astype(BF16)
        v2 = jnp.dot(u, m2, preferred_element_type=F32).astype(BF16)
        return (jnp.dot(cos_ref[...].astype(BF16), v1, preferred_element_type=F32)
                - jnp.dot(sin_ref[...].astype(BF16), v2, preferred_element_type=F32)).astype(BF16)

    @pl.when(i < FN_CTX_STEPS)
    def _():
        for s in range(FN_SEQ):
            o_ref[0, s] = mix(u_ref[0, s], cos_s_ref, sin_s_ref)

    @pl.when(i >= FN_CTX_STEPS)
    def _():
        u = u_ref[0].reshape(FN_SEQ * L_BLK, D_FN)
        o_ref[0] = mix(u, cos_l_ref, sin_l_ref).reshape(FN_SEQ, L_BLK, D_FN)


def _dft_tables(n):
    k = np.arange(n, dtype=np.int64)
    ang = (2.0 * np.pi / n) * ((k[:, None] * k[None, :]) % n).astype(np.float64)
    scale = 1.0 / math.sqrt(n)
    return (np.cos(ang) * scale).astype(np.float32), (np.sin(ang) * scale).astype(np.float32)


def _fnet(uf, m1, m2):
    n_lat_steps = SUB // FN_SEQ
    cos_s, sin_s = _dft_tables(L_BLK)
    cos_l, sin_l = _dft_tables(FN_SEQ * L_BLK)
    full = lambda i: (0, 0)
    per_grp = SUB // FN_SEQ
    blk = pl.BlockSpec(
        (1, FN_SEQ, L_BLK, D_FN),
        lambda i: (jnp.minimum(i // per_grp, N_GRP - 1),
                   jnp.where(i < FN_CTX_STEPS, i % per_grp, i - FN_CTX_STEPS), 0, 0))
    table = lambda n: pl.BlockSpec((n, n), full)
    return pl.pallas_call(
        _fnet_kernel,
        grid=(FN_CTX_STEPS + n_lat_steps,),
        in_specs=[blk, table(L_BLK), table(L_BLK), table(FN_SEQ * L_BLK), table(FN_SEQ * L_BLK),
                  table(D_FN), table(D_FN)],
        out_specs=blk,
        out_shape=jax.ShapeDtypeStruct(uf.shape, BF16),
        compiler_params=_cparams(("arbitrary",)),
        name="fnet",
    )(uf, jnp.asarray(cos_s), jnp.asarray(sin_s), jnp.asarray(cos_l), jnp.asarray(sin_l), m1, m2)


def _fnet_weights(w_fnet):
    n_g = D_FN // FN_GW
    cos_c, sin_c = _dft_tables(FN_GW)
    eye = np.eye(n_g, dtype=np.float32)
    cos_bd = np.kron(eye, cos_c)
    sin_bd = np.kron(eye, sin_c)
    w_bd = (w_fnet[:, :, None, :] * jnp.asarray(eye)[:, None, :, None]).reshape(D_FN, D_FN)
    out = jax.ShapeDtypeStruct((D_FN, D_FN), BF16)
    return pl.pallas_call(_fnet_w_kernel, out_shape=(out, out), name="fnet_weights")(
        jnp.asarray(cos_bd), jnp.asarray(sin_bd), w_bd)


def _gelu_tanh(x):
    return 0.5 * x * (1.0 + jnp.tanh(math.sqrt(2.0 / math.pi) * (x + 0.044715 * (x * x * x))))


def _post_kernel(y_ref, yf_ref, xp_ref, xs_ref, pos_ref, mod_ref, wglu_ref, wout_ref, g2_ref, wr_ref,
                 br_ref, permt_ref, x1_ref, h2_ref, comb_ref):
    g = pl.program_id(0)
    z = _gelu_tanh(y_ref[0])
    gate = jnp.dot(z.astype(BF16), wglu_ref[...].astype(BF16), preferred_element_type=F32)
    gl = (z * jax.nn.sigmoid(gate)).astype(BF16)
    permt = permt_ref[...]
    n_q = TC // PERM_T
    nat = [jnp.dot(permt, gl[q * SUB * PERM_T:(q + 1) * SUB * PERM_T], preferred_element_type=F32).astype(BF16)
           for q in range(n_q)]
    gl_nat = jnp.concatenate(
        [nat[q][s * PERM_T:(s + 1) * PERM_T] for s in range(SUB) for q in range(n_q)], axis=0)
    w_out = wout_ref[...].astype(BF16)
    mixed = (jnp.dot(gl_nat, w_out[:D_S5], preferred_element_type=F32)
             + jnp.dot(yf_ref[0].reshape(SUB * TC, D_FN), w_out[D_S5:], preferred_element_type=F32))
    x = _load_x(g, xp_ref, xs_ref, pos_ref)
    mod = mod_ref[0]
    gate1 = mod[:, :, 2 * D:3 * D]
    shift2 = mod[:, :, 3 * D:4 * D]
    scale2 = mod[:, :, 4 * D:5 * D]
    x1 = x + gate1 * mixed.reshape(SUB, TC, D)
    x1_ref[0] = x1
    h2 = _rms(x1, g2_ref[...]) * (1.0 + scale2) + shift2
    h2_ref[0] = h2.astype(BF16)

    hr = h2.reshape(SUB * TC, D)
    h_hi = hr.astype(BF16)
    h_lo = (hr - h_hi.astype(F32)).astype(BF16)
    wr = wr_ref[...]
    w_hi = wr.astype(BF16)
    w_lo = (wr - w_hi.astype(F32)).astype(BF16)
    logits = (jnp.dot(h_hi, w_hi, preferred_element_type=F32) + jnp.dot(h_lo, w_hi, preferred_element_type=F32)
              + jnp.dot(h_hi, w_lo, preferred_element_type=F32) + br_ref[...])
    lane = lax.broadcasted_iota(jnp.int32, logits.shape, 1).astype(F32)
    top_v, hots = [], []
    cur = logits
    for _ in range(TOP_K):
        m = jnp.max(cur, axis=-1, keepdims=True)
        idx = jnp.min(jnp.where(cur == m, lane, float(E_PAD)), axis=-1, keepdims=True)
        hot = lane == idx
        top_v.append(m)
        hots.append(hot)
        cur = jnp.where(hot, -3.0e38, cur)
    exps = [jnp.exp(v - top_v[0]) for v in top_v]
    denom = exps[0] + exps[1] + exps[2] + exps[3]
    comb = jnp.zeros(logits.shape, F32)
    for k in range(TOP_K):
        comb = comb + jnp.where(hots[k], exps[k] / denom, 0.0)
    comb_ref[0] = comb.reshape(SUB, TC, E_PAD)


def _post(y, yf, xp4, xs4, pos3, modg, w_glu, w_out, norm2_g, w_router, b_router, permt):
    xp_spec, xs_spec, pos_spec = _x_specs()
    const2 = lambda g, tc: (0, 0)
    wr = jnp.zeros((D, E_PAD), F32).at[:, :N_EXPERTS].set(w_router)
    br = jnp.full((1, E_PAD), -1.0e30, F32).at[0, :N_EXPERTS].set(b_router)
    blk = lambda w: pl.BlockSpec((1, SUB, TC, w), lambda g, tc: (g, 0, tc, 0))
    return pl.pallas_call(
        _post_kernel,
        grid=(N_GRP, N_TC),
        in_specs=[pl.BlockSpec((1, SUB * TC, D_S5), lambda g, tc: (g, tc, 0)),
                  blk(D_FN), xp_spec, xs_spec, pos_spec,
                  pl.BlockSpec((1, SUB, 1, 6 * D), lambda g, tc: (g, 0, 0, 0)),
                  pl.BlockSpec((D_S5, D_S5), const2),
                  pl.BlockSpec((D, D), const2),
                  pl.BlockSpec((1, D), const2),
                  pl.BlockSpec((D, E_PAD), const2),
                  pl.BlockSpec((1, E_PAD), const2),
                  pl.BlockSpec((SUB * PERM_T, SUB * PERM_T), const2)],
        out_specs=[blk(D), blk(D), blk(E_PAD)],
        out_shape=[jax.ShapeDtypeStruct((N_GRP, SUB, L_BLK, D), F32),
                   jax.ShapeDtypeStruct((N_GRP, SUB, L_BLK, D), BF16),
                   jax.ShapeDtypeStruct((N_GRP, SUB, L_BLK, E_PAD), F32)],
        compiler_params=_cparams(("arbitrary", "arbitrary")),
        name="post_mixer",
    )(y, yf, xp4, xs4, pos3, modg, w_glu, w_out, norm2_g.reshape(1, D), wr, br, permt)


TBLK = L_BLK
N_BLK = T_TOK // TBLK
N_CTX_BLK = (N_GRP - 1) * SUB
SEG_ALIGN = 8
RB = 1280
assert RB >= TBLK * TOP_K + N_EXPERTS * (SEG_ALIGN - 1) and RB % 128 == 0
TM = 256
TM_SHIFT = 8
assert 1 << TM_SHIFT == TM
R_TOT = T_TOK * TOP_K + N_BLK * N_EXPERTS * (SEG_ALIGN - 1) + TM
TAB_ROWS = 32
assert TAB_ROWS >= N_BLK
BIG_ROWS = 32
BIG_SHIFT = 2
assert SEG_ALIGN << BIG_SHIFT == BIG_ROWS


def _plan_kernel(comb_ref, z_ref, ptab_ref, loff_ref, gst_ref, eoff_ref):
    row = lax.broadcasted_iota(jnp.int32, (TBLK, TBLK), 0)
    col = lax.broadcasted_iota(jnp.int32, (TBLK, TBLK), 1)
    earlier = jnp.where(row > col, 1.0, 0.0).astype(BF16)
    ptab_ref[...] = jnp.zeros_like(ptab_ref)

    def body(b, _):
        rows = pl.ds(pl.multiple_of(b * TBLK, TBLK), TBLK)
        m = jnp.where(comb_ref[rows, :] > 0.0, 1.0, 0.0)
        rank = jnp.dot(earlier, m.astype(BF16), preferred_element_type=F32)
        z_ref[rows, :] = m * (rank + 1.0)
        n = jnp.sum(m, axis=0, keepdims=True)
        ptab_ref[pl.ds(b, 1), :] = jnp.floor((n + (SEG_ALIGN - 1)) * (1.0 / SEG_ALIGN)) * SEG_ALIGN
        return 0

    lax.fori_loop(0, N_BLK, body, 0)

    ptab = ptab_ref[...]
    er = lax.broadcasted_iota(jnp.int32, (E_PAD, E_PAD), 0)
    ec = lax.broadcasted_iota(jnp.int32, (E_PAD, E_PAD), 1)
    before = jnp.where(er < ec, 1.0, 0.0)
    exact = functools.partial(jnp.dot, preferred_element_type=F32, precision=HIGHEST)
    loff_ref[...] = exact(ptab, before)
    tot = jnp.sum(ptab, axis=0, keepdims=True)
    eoff = exact(jnp.broadcast_to(tot, (SUB, E_PAD)), before)
    eoff_ref[...] = eoff
    br = lax.broadcasted_iota(jnp.int32, (TAB_ROWS, TAB_ROWS), 0)
    bc = lax.broadcasted_iota(jnp.int32, (TAB_ROWS, TAB_ROWS), 1)
    gst_ref[...] = eoff[0:1] + exact(jnp.where(br > bc, 1.0, 0.0), ptab)


def _plan(comb):
    tab = jax.ShapeDtypeStruct((TAB_ROWS, E_PAD), F32)
    return pl.pallas_call(
        _plan_kernel,
        out_shape=(jax.ShapeDtypeStruct((T_TOK, E_PAD), F32), tab, tab, tab,
                   jax.ShapeDtypeStruct((SUB, E_PAD), F32)),
        compiler_params=pltpu.CompilerParams(vmem_limit_bytes=VMEM_LIMIT),
        name="moe_plan",
    )(comb)


def _sort_matrix(b, z_ref, loff_ref, ptab_ref):
    loff = loff_ref[pl.ds(b, 1), :]
    size = ptab_ref[pl.ds(b, 1), :]
    r = lax.broadcasted_iota(jnp.int32, (RB, E_PAD), 0).astype(F32)
    owner = jnp.where(r >= loff, jnp.where(r < loff + size, 1.0, 0.0), 0.0)
    rank1 = r[:, 0:1] - jnp.sum(owner * loff, axis=-1, keepdims=True) + 1.0
    zt = z_ref[...].T
    v = jnp.dot(owner.astype(BF16), zt.astype(BF16), preferred_element_type=F32)
    return jnp.where(v == rank1, 1.0, 0.0), owner


def _segment_copies(b, loff_s, gst_s, p8_s, make_copy):
    def per_expert(e, counts):
        k = b * N_EXPERTS + e
        chunks = p8_s[k]
        n_big = lax.shift_right_logical(chunks, BIG_SHIFT)
        n_small = chunks - n_big * (BIG_ROWS // SEG_ALIGN)
        local0 = loff_s[k]
        global0 = gst_s[k]

        def big(j, _):
            make_copy(pl.multiple_of(local0 + j * BIG_ROWS, SEG_ALIGN),
                      pl.multiple_of(global0 + j * BIG_ROWS, SEG_ALIGN), BIG_ROWS).start()
            return 0

        def small(j, _):
            off = n_big * BIG_ROWS + j * SEG_ALIGN
            make_copy(pl.multiple_of(local0 + off, SEG_ALIGN),
                      pl.multiple_of(global0 + off, SEG_ALIGN), SEG_ALIGN).start()
            return 0

        lax.fori_loop(0, n_big, big, 0)
        lax.fori_loop(0, n_small, small, 0)
        return counts[0] + n_big, counts[1] + n_small

    return lax.fori_loop(0, N_EXPERTS, per_expert, (0, 0))


def _wait_copies(counts, make_copy):
    for n, rows in zip(counts, (BIG_ROWS, SEG_ALIGN)):
        def wait_one(i, _, rows=rows):
            make_copy(0, 0, rows).wait()
            return 0
        lax.fori_loop(0, n, wait_one, 0)


def _zero_rows_from(hbm, zeros_vmem, first_row, sem):
    def copy_to(row):
        return pltpu.make_async_copy(zeros_vmem, hbm.at[pl.ds(row, TM)], sem)

    n_full = lax.shift_right_logical(R_TOT - first_row, TM_SHIFT)

    def start_one(j, _):
        copy_to(pl.multiple_of(first_row + j * TM, SEG_ALIGN)).start()
        return 0

    def wait_one(j, _):
        copy_to(0).wait()
        return 0

    lax.fori_loop(0, n_full, start_one, 0)
    lax.fori_loop(0, n_full, wait_one, 0)
    last = copy_to(R_TOT - TM)
    last.start()
    last.wait()


def _dispatch_kernel(loff_s, gst_s, p8_s, tot_s, h_ref, z_ref, loff_ref, ptab_ref, xs_hbm, xbuf, sem, pending):
    b = pl.program_id(0)

    def make_copy(local_row, global_row, rows):
        return pltpu.make_async_copy(xbuf.at[pl.ds(local_row, rows)], xs_hbm.at[pl.ds(global_row, rows)], sem)

    pm = _sort_matrix(b, z_ref, loff_ref, ptab_ref)[0].astype(BF16)

    @pl.when(b > 0)
    def _():
        _wait_copies((pending[0], pending[1]), make_copy)

    xbuf[...] = jnp.dot(pm, h_ref[...], preferred_element_type=F32)
    n_big, n_small = _segment_copies(b, loff_s, gst_s, p8_s, make_copy)
    pending[0] = n_big
    pending[1] = n_small

    @pl.when(b == N_BLK - 1)
    def _():
        _wait_copies((n_big, n_small), make_copy)
        xbuf[0:TM, :] = jnp.zeros((TM, D), F32)
        _zero_rows_from(xs_hbm, xbuf.at[0:TM], tot_s[0], sem)


def _dispatch(h2, z, loff, ptab, loff_i, gst_i, p8_i, tot_i):
    whole = pl.BlockSpec((TAB_ROWS, E_PAD), lambda b, *_: (0, 0))
    return pl.pallas_call(
        _dispatch_kernel,
        grid_spec=pltpu.PrefetchScalarGridSpec(
            num_scalar_prefetch=4,
            grid=(N_BLK,),
            in_specs=[pl.BlockSpec((TBLK, D), lambda b, *_: (b, 0)),
                      pl.BlockSpec((TBLK, E_PAD), lambda b, *_: (b, 0)),
                      whole, whole],
            out_specs=pl.BlockSpec(memory_space=pl.ANY),
            scratch_shapes=[pltpu.VMEM((RB, D), F32), pltpu.SemaphoreType.DMA(()),
                            pltpu.SMEM((2,), jnp.int32)]),
        out_shape=jax.ShapeDtypeStruct((R_TOT, D), F32),
        compiler_params=_cparams(("arbitrary",)),
        name="moe_dispatch",
    )(loff_i, gst_i, p8_i, tot_i, h2, z, loff, ptab)


MAX_TILES = R_TOT // TM + N_EXPERTS
X_AHEAD = 3
X_SLOTS = X_AHEAD + 1
Y_SLOTS = 2
N_MATS = 3
FF_CHUNK = 256
CAST_ROWS = 64


def _expert_kernel(eoff_s, xs_hbm, wg_hbm, wu_hbm, wd_hbm, bg_ref, bu_ref, bd_ref, ys_hbm,
                   wst, wbf, xin, yout, act_ref, w_sem, x_sem, y_sem, t_exp, t_row, t_first, t_valid, live):
    def add_expert(e, carry):
        n_t, n_live = carry
        start = eoff_s[e]
        count = eoff_s[e + 1] - start
        tiles = lax.shift_right_logical(count + (TM - 1), TM_SHIFT)

        def add_tile(i, _):
            t_exp[n_t + i] = e
            t_row[n_t + i] = start + i * TM
            t_first[n_t + i] = jnp.where(i == 0, 1, 0)
            t_valid[n_t + i] = jnp.minimum(count - i * TM, TM)
            return 0

        lax.fori_loop(0, tiles, add_tile, 0)
        has_rows = jnp.where(tiles > 0, 1, 0)

        @pl.when(tiles > 0)
        def _():
            live[n_live] = e

        return n_t + tiles, n_live + has_rows

    n_tiles, n_live = lax.fori_loop(0, N_EXPERTS, add_expert, (0, 0))

    def w_copies(e, slot):
        return [pltpu.make_async_copy(w.at[e], wst.at[slot, m], w_sem.at[slot])
                for m, w in enumerate((wg_hbm, wu_hbm, wd_hbm))]

    def tile_rows(j):
        return pl.ds(pl.multiple_of(t_row[j], SEG_ALIGN), TM)

    def x_copy(j, slot):
        return pltpu.make_async_copy(xs_hbm.at[tile_rows(j)], xin.at[slot], x_sem.at[slot])

    def y_pieces(j, op):
        ys = lax.rem(j, Y_SLOTS)
        valid = t_valid[j]
        row0 = t_row[j]

        def piece(off, rows):
            return pltpu.make_async_copy(yout.at[ys, pl.ds(off, rows)],
                                         ys_hbm.at[pl.ds(pl.multiple_of(row0 + off, SEG_ALIGN), rows)],
                                         y_sem.at[ys])

        @pl.when(valid == TM)
        def _():
            op(piece(0, TM))

        @pl.when(valid < TM)
        def _():
            n_big = lax.shift_right_logical(valid, BIG_ROWS.bit_length() - 1)
            n_small = lax.shift_right_logical(valid - n_big * BIG_ROWS, SEG_ALIGN.bit_length() - 1)

            def big(i, _):
                op(piece(pl.multiple_of(i * BIG_ROWS, SEG_ALIGN), BIG_ROWS))
                return 0

            def small(i, _):
                op(piece(pl.multiple_of(n_big * BIG_ROWS + i * SEG_ALIGN, SEG_ALIGN), SEG_ALIGN))
                return 0

            lax.fori_loop(0, n_big, big, 0)
            lax.fori_loop(0, n_small, small, 0)

    @pl.when(n_tiles > 0)
    def _():
        for cp in w_copies(live[0], 0):
            cp.start()

    for ahead in range(X_AHEAD):
        @pl.when(n_tiles > ahead)
        def _(ahead=ahead):
            x_copy(ahead, ahead).start()

    def body(j, k):
        e = t_exp[j]

        @pl.when(t_first[j] == 1)
        def _():
            ws = lax.rem(k, 2)

            @pl.when(k + 1 < n_live)
            def _():
                for cp in w_copies(live[k + 1], 1 - ws):
                    cp.start()

            for cp in w_copies(e, ws):
                cp.wait()

            def cast_rows(r, _):
                rows = pl.ds(pl.multiple_of(r * CAST_ROWS, CAST_ROWS), CAST_ROWS)
                for m in range(N_MATS):
                    wbf[m, rows, :] = wst[ws, m, rows, :].astype(BF16)
                return 0

            lax.fori_loop(0, D // CAST_ROWS, cast_rows, 0)

        slot = lax.rem(j, X_SLOTS)
        x_copy(j, slot).wait()

        @pl.when(j + X_AHEAD < n_tiles)
        def _():
            x_copy(j + X_AHEAD, lax.rem(j + X_AHEAD, X_SLOTS)).start()

        x = xin[slot].astype(BF16)
        for c in range(D // FF_CHUNK):
            cols = slice(c * FF_CHUNK, (c + 1) * FF_CHUNK)
            gate = jnp.dot(x, wbf[0, :, cols], preferred_element_type=F32) + bg_ref[e][:, cols]
            up = jnp.dot(x, wbf[1, :, cols], preferred_element_type=F32) + bu_ref[e][:, cols]
            gate = jnp.minimum(gate, SWIGLU_LIMIT)
            up = jnp.clip(up, -SWIGLU_LIMIT, SWIGLU_LIMIT)
            act_ref[:, cols] = ((up + 1.0) * gate * jax.nn.sigmoid(SWIGLU_ALPHA * gate)).astype(BF16)
        y = jnp.dot(act_ref[...], wbf[2], preferred_element_type=F32) + bd_ref[e]

        @pl.when(j >= Y_SLOTS)
        def _():
            y_pieces(j - Y_SLOTS, lambda cp: cp.wait())

        yout[lax.rem(j, Y_SLOTS)] = y
        y_pieces(j, lambda cp: cp.start())
        return k + t_first[j]

    lax.fori_loop(0, n_tiles, body, 0)

    for back in range(Y_SLOTS, 0, -1):
        @pl.when(n_tiles >= back)
        def _(back=back):
            y_pieces(n_tiles - back, lambda cp: cp.wait())

    yout[0] = jnp.zeros((TM, D), F32)
    _zero_rows_from(ys_hbm, yout.at[0], eoff_s[N_EXPERTS], y_sem.at[0])


def _experts(xs, eoff_i, w_gate, b_gate, w_up, b_up, w_down, b_down):
    hbm = pl.BlockSpec(memory_space=pl.ANY)
    bspec = pl.BlockSpec((N_EXPERTS, 1, D), lambda i, *_: (0, 0, 0))
    return pl.pallas_call(
        _expert_kernel,
        grid_spec=pltpu.PrefetchScalarGridSpec(
            num_scalar_prefetch=1,
            grid=(1,),
            in_specs=[hbm, hbm, hbm, hbm, bspec, bspec, bspec],
            out_specs=hbm,
            scratch_shapes=[pltpu.VMEM((2, N_MATS, D, D), F32), pltpu.VMEM((N_MATS, D, D), BF16),
                            pltpu.VMEM((X_SLOTS, TM, D), F32), pltpu.VMEM((Y_SLOTS, TM, D), F32),
                            pltpu.VMEM((TM, D), BF16),
                            pltpu.SemaphoreType.DMA((2,)), pltpu.SemaphoreType.DMA((X_SLOTS,)),
                            pltpu.SemaphoreType.DMA((Y_SLOTS,)),
                            pltpu.SMEM((MAX_TILES,), jnp.int32), pltpu.SMEM((MAX_TILES,), jnp.int32),
                            pltpu.SMEM((MAX_TILES,), jnp.int32), pltpu.SMEM((MAX_TILES,), jnp.int32),
                            pltpu.SMEM((N_EXPERTS,), jnp.int32)]),
        out_shape=jax.ShapeDtypeStruct((R_TOT, D), F32),
        compiler_params=_cparams(("arbitrary",)),
        name="moe_experts",
    )(eoff_i, xs, w_gate, w_up, w_down, b_gate.reshape(N_EXPERTS, 1, D), b_up.reshape(N_EXPERTS, 1, D),
      b_down.reshape(N_EXPERTS, 1, D))


def _combine_kernel(loff_s, gst_s, p8_s, ys_hbm, z_ref, comb_ref, loff_ref, ptab_ref, x1_ref, mod_ref, gf_ref,
                    yctx_ref, ylat_ref, ybuf, sem):
    b = pl.program_id(0)

    def make_copy(local_row, global_row, rows):
        return pltpu.make_async_copy(ys_hbm.at[pl.ds(global_row, rows)], ybuf.at[pl.ds(local_row, rows)], sem)

    started = _segment_copies(b, loff_s, gst_s, p8_s, make_copy)

    last = b * N_EXPERTS + N_EXPERTS - 1
    used_chunks = lax.shift_right_logical(loff_s[last], 3) + p8_s[last]

    def zero_chunk(j, _):
        ybuf[pl.ds(pl.multiple_of(j * SEG_ALIGN, SEG_ALIGN), SEG_ALIGN), :] = jnp.zeros((SEG_ALIGN, D), F32)
        return 0

    lax.fori_loop(used_chunks, RB // SEG_ALIGN, zero_chunk, 0)

    pm, owner = _sort_matrix(b, z_ref, loff_ref, ptab_ref)
    comb = comb_ref[...]
    c_hi = comb.astype(BF16)
    rest = comb - c_hi.astype(F32)
    c_mid = rest.astype(BF16)
    c_lo = (rest - c_mid.astype(F32)).astype(BF16)
    pmb = pm.astype(BF16)
    moved = (jnp.dot(pmb, c_hi, preferred_element_type=F32) + jnp.dot(pmb, c_mid, preferred_element_type=F32)
             + jnp.dot(pmb, c_lo, preferred_element_type=F32))
    w_row = jnp.sum(owner * moved, axis=-1, keepdims=True)
    pmt = pm.T.astype(BF16)
    _wait_copies(started, make_copy)
    y = (ybuf[...] * w_row).astype(BF16)
    moe = jnp.dot(pmt, y, preferred_element_type=F32)
    gate2 = mod_ref[0][:, 5 * D:6 * D]
    x2 = x1_ref[0] + gate2 * moe
    y = _rms(x2, gf_ref[...])

    @pl.when(b < N_CTX_BLK)
    def _():
        yctx_ref[0] = y

    @pl.when(b >= N_CTX_BLK)
    def _():
        ylat_ref[0] = y


def _combine(ys, z, comb, loff, ptab, x1, modv, norm_f_g, loff_i, gst_i, p8_i):
    whole = pl.BlockSpec((TAB_ROWS, E_PAD), lambda b, *_: (0, 0))
    tok = pl.BlockSpec((TBLK, E_PAD), lambda b, *_: (b, 0))
    return pl.pallas_call(
        _combine_kernel,
        grid_spec=pltpu.PrefetchScalarGridSpec(
            num_scalar_prefetch=3,
            grid=(N_BLK,),
            in_specs=[pl.BlockSpec(memory_space=pl.ANY), tok, tok, whole, whole,
                      pl.BlockSpec((1, TBLK, D), lambda b, *_: (b, 0, 0)),
                      pl.BlockSpec((1, 1, 6 * D), lambda b, *_: (b, 0, 0)),
                      pl.BlockSpec((1, D), lambda b, *_: (0, 0))],
            out_specs=[pl.BlockSpec((1, TBLK, D), lambda b, *_: (jnp.minimum(b, N_CTX_BLK - 1), 0, 0)),
                       pl.BlockSpec((1, TBLK, D), lambda b, *_: (jnp.maximum(b - N_CTX_BLK, 0), 0, 0))],
            scratch_shapes=[pltpu.VMEM((RB, D), F32), pltpu.SemaphoreType.DMA(())]),
        out_shape=[jax.ShapeDtypeStruct((N_CTX_BLK, TBLK, D), F32),
                   jax.ShapeDtypeStruct((N_BLK - N_CTX_BLK, TBLK, D), F32)],
        compiler_params=_cparams(("arbitrary",)),
        name="moe_combine",
    )(loff_i, gst_i, p8_i, ys, z, comb, loff, ptab, x1, modv, norm_f_g.reshape(1, D))


def _moe_and_final(x1, h2, comb, modg, norm_f_g, w_gate, b_gate, w_up, b_up, w_down, b_down):
    z, ptab, loff, gst, eoff = _plan(comb)
    as_scalars = lambda t: t[:N_BLK, :N_EXPERTS].astype(jnp.int32).reshape(N_BLK * N_EXPERTS)
    loff_i, gst_i = as_scalars(loff), as_scalars(gst)
    p8_i = as_scalars(ptab) // SEG_ALIGN
    eoff_i = eoff[0, :N_EXPERTS + 1].astype(jnp.int32)
    xs = _dispatch(h2, z, loff, ptab, loff_i, gst_i, p8_i, eoff_i[N_EXPERTS:])
    ys = _experts(xs, eoff_i, w_gate, b_gate, w_up, b_up, w_down, b_down)
    return _combine(ys, z, comb, loff, ptab, x1, modg.reshape(N_BLK, 1, 6 * D), norm_f_g,
                    loff_i, gst_i, p8_i)


def _grid_pos_embed(n_tokens, dim):
    rows = n_tokens // GRID_W
    t = np.arange(rows * GRID_W)
    r = (t // GRID_W).astype(np.float32)
    col = (t % GRID_W).astype(np.float32)
    q = dim // 4
    omega = (1.0 / np.float32(POS_TEMP) ** (np.arange(q, dtype=np.float32) / np.float32(q))).astype(np.float32)

    def emb(p):
        a = p[:, None] * omega[None, :]
        return np.concatenate([np.sin(a), np.cos(a)], axis=-1)

    return np.concatenate([emb(r), emb(col)], axis=-1).astype(np.float32)


def kernel(x_prompt, x_sample, c, state_s5_re, state_s5_im, c_ctx, w_ada, b_ada, norm1_g, w_in, s5_lam_re,
           s5_lam_im, s5_log_dt, s5_b_re, s5_b_im, s5_c_re, s5_c_im, s5_d, s5_w_glu, w_fnet, w_out, norm2_g,
           w_router, b_router, w_gate, b_gate, w_up, b_up, w_down, b_down, norm_f_g):
    n_ctx, n_lat = x_prompt.shape[0], x_sample.shape[0]
    assert x_prompt.shape == (SUB * (N_GRP - 1), L_BLK, D) and x_sample.shape == (2, LAT_CHUNKS * L_BLK, D)
    assert w_ada.shape[0] == 1, "one trunk layer"
    layer = 0

    cvec = jnp.concatenate([jnp.broadcast_to(c_ctx[None], (n_ctx, D)), jnp.repeat(c, LAT_CHUNKS, axis=0)], axis=0)
    modg = _adaln(cvec, w_ada[layer], b_ada[layer]).reshape(N_GRP, SUB, 1, 6 * D)

    xp4 = x_prompt.reshape(N_GRP - 1, SUB, L_BLK, D)
    xs4 = x_sample.reshape(1, SUB, L_BLK, D)
    pos3 = jnp.asarray(_grid_pos_embed(LAT_CHUNKS * L_BLK, D).reshape(LAT_CHUNKS, L_BLK, D))
    perm = _perm_matrix()

    us, uf = _pre(xp4, xs4, pos3, modg, norm1_g[layer], w_in[layer], jnp.asarray(perm, BF16))

    bbmat, ccmat, a8 = _s5_params(s5_lam_re[layer], s5_lam_im[layer], s5_log_dt[layer], s5_b_re[layer],
                                  s5_b_im[layer], s5_c_re[layer], s5_c_im[layer])

    st = jnp.stack([state_s5_re[:, layer], state_s5_im[:, layer]], axis=2)
    st = st.reshape(n_lat, 2, 2, N_S5_CHUNKS, CH_P)
    st = jnp.transpose(st, (1, 3, 0, 2, 4)).reshape(2, N_S5_CHUNKS, n_lat, 2 * CH_P)
    init8 = jnp.zeros((2, N_S5_CHUNKS, n_lat, LAT_CHUNKS, 2 * CH_P), F32)
    init8 = init8.at[0, :, :, 0].set(st[0]).at[1, :, :, LAT_CHUNKS - 1].set(st[1])
    init8 = init8.reshape(2, N_S5_CHUNKS, SUB, 2 * CH_P)

    y_s5, fin = _s5(us, bbmat, ccmat, a8, init8, s5_d[layer])

    m1, m2 = _fnet_weights(w_fnet[layer])
    yf = _fnet(uf, m1, m2)

    x1, h2, comb = _post(y_s5, yf, xp4, xs4, pos3, modg, s5_w_glu[layer], w_out[layer], norm2_g[layer],
                         w_router[layer], b_router[layer], jnp.asarray(perm.T, BF16))

    y_prompt, y_lat = _moe_and_final(x1.reshape(N_BLK, TBLK, D), h2.reshape(T_TOK, D), comb.reshape(T_TOK, E_PAD),
                                     modg, norm_f_g, w_gate[layer], b_gate[layer], w_up[layer], b_up[layer],
                                     w_down[layer], b_down[layer])
    y_sample = y_lat.reshape(n_lat, LAT_CHUNKS * L_BLK, D)

    fin = fin[:, :N_GRP - 1].reshape(2, N_GRP - 1, SUB, N_S5_CHUNKS, 2, CH_P)
    fin = jnp.transpose(fin, (4, 1, 2, 0, 3, 5)).reshape(2, n_ctx, 1, 2, N_S5_GROUPS, S5_P)
    return (y_prompt, y_sample, fin[0], fin[1])
```

```python
import functools
import math

import numpy as np
import jax
import jax.numpy as jnp
from jax import lax
from jax.experimental import pallas as pl
from jax.experimental.pallas import tpu as pltpu

F32 = jnp.float32
BF16 = jnp.bfloat16
HIGHEST = lax.Precision.HIGHEST

D = 1024
D_S5 = 768
S5_H = 16
S5_P = 64
N_S5_GROUPS = 48
D_FN = 256
FN_GW = 64
N_EXPERTS = 32
TOP_K = 4
E_PAD = 128
SWIGLU_LIMIT = 7.0
SWIGLU_ALPHA = 1.702
RMS_EPS = 1e-6
POS_TEMP = 10000.0
GRID_W = 64

L_BLK = 256
SUB = 8
N_GRP = 3
GRP_ROWS = L_BLK * SUB
LAT_CHUNKS = 4
T_TOK = N_GRP * GRP_ROWS
TC = 128
N_TC = L_BLK // TC
PERM_T = 32
S5_CHUNK = 16
N_S5_CHUNKS = N_S5_GROUPS // S5_CHUNK
CH_U = S5_CHUNK * S5_H
CH_P = S5_CHUNK * S5_P
ROW_BLK = 512
FIX_PARTS = 1
VMEM_LIMIT = 56 * 1024 * 1024


def _cparams(sem):
    return pltpu.CompilerParams(dimension_semantics=sem, vmem_limit_bytes=VMEM_LIMIT)


def _adaln_kernel(c_ref, w_ref, b_ref, o_ref):
    c = c_ref[...]
    s = c * jax.nn.sigmoid(c)
    s_hi = s.astype(BF16)
    s_lo = (s - s_hi.astype(F32)).astype(BF16)
    w = w_ref[...]
    w_hi = w.astype(BF16)
    w_lo = (w - w_hi.astype(F32)).astype(BF16)
    o_ref[...] = (jnp.dot(s_hi, w_hi, preferred_element_type=F32) + jnp.dot(s_lo, w_hi, preferred_element_type=F32)
                  + jnp.dot(s_hi, w_lo, preferred_element_type=F32) + b_ref[...])


def _adaln(cvec, w_ada, b_ada):
    n = w_ada.shape[1]
    rows = cvec.shape[0]
    return pl.pallas_call(
        _adaln_kernel,
        grid=(n // D,),
        in_specs=[pl.BlockSpec((rows, D), lambda j: (0, 0)),
                  pl.BlockSpec((D, D), lambda j: (0, j)),
                  pl.BlockSpec((1, D), lambda j: (0, j))],
        out_specs=pl.BlockSpec((rows, D), lambda j: (0, j)),
        out_shape=jax.ShapeDtypeStruct((rows, n), F32),
        compiler_params=_cparams(("arbitrary",)),
        name="adaln",
    )(cvec, w_ada, b_ada.reshape(1, n))


def _zoh(lam_re, lam_im, log_dt):
    dt = jnp.exp(log_dt)
    a_re = jnp.minimum(lam_re, -1e-4)
    a_im = lam_im
    mag = jnp.exp(a_re * dt)
    ab_re = mag * jnp.cos(a_im * dt)
    ab_im = mag * jnp.sin(a_im * dt)
    den = a_re * a_re + a_im * a_im
    nr = ab_re - 1.0
    f_re = (nr * a_re + ab_im * a_im) / den
    f_im = (ab_im * a_re - nr * a_im) / den
    return ab_re, ab_im, f_re, f_im


def _s5_params_kernel(lre3_ref, lim3_ref, ldt3_ref, lre2_ref, lim2_ref, ldt2_ref, bre_ref, bim_ref,
                      cre_ref, cim_ref, bb_ref, cc_ref, a_ref):
    exact = functools.partial(jnp.dot, preferred_element_type=F32, precision=HIGHEST)
    spread = jnp.where(lax.broadcasted_iota(jnp.int32, (S5_P, CH_P), 1) % S5_P
                       == lax.broadcasted_iota(jnp.int32, (S5_P, CH_P), 0), 1.0, 0.0)

    def block_diag(m):
        wide = jnp.dot(m.reshape(CH_U, S5_P).astype(BF16), spread.astype(BF16), preferred_element_type=F32)
        row_g = lax.broadcasted_iota(jnp.int32, (CH_U, CH_P), 0) // S5_H
        col_g = lax.broadcasted_iota(jnp.int32, (CH_U, CH_P), 1) // S5_P
        return jnp.where(row_g == col_g, wide, 0.0)

    _, _, f_re, f_im = _zoh(lre3_ref[0, 0], lim3_ref[0, 0], ldt3_ref[0, 0])
    b_re = bre_ref[0, 0]
    b_im = bim_ref[0, 0]
    bb_ref[0, 0] = jnp.concatenate([block_diag(f_re * b_re - f_im * b_im),
                                    block_diag(f_re * b_im + f_im * b_re)], axis=1).astype(BF16)
    cc_ref[0, 0] = jnp.concatenate([block_diag(cre_ref[0, 0]).T,
                                    -block_diag(cim_ref[0, 0]).T], axis=0).astype(BF16)

    ab_re, ab_im, _, _ = _zoh(lre2_ref[0, 0], lim2_ref[0, 0], ldt2_ref[0, 0])
    row_g = lax.broadcasted_iota(jnp.int32, (S5_CHUNK, CH_P), 0)
    col_g = lax.broadcasted_iota(jnp.int32, (S5_CHUNK, CH_P), 1) // S5_P

    def lane_row(a):
        flat = jnp.sum(jnp.where(row_g == col_g, exact(a, spread), 0.0), axis=0, keepdims=True)
        return jnp.broadcast_to(flat, (SUB, CH_P))

    a_ref[0, 0] = jnp.concatenate([lane_row(ab_re), lane_row(ab_im)], axis=1)


def _s5_params(lam_re, lam_im, log_dt, b_re, b_im, c_re, c_im):
    nc = N_S5_CHUNKS
    ldt = jnp.broadcast_to(log_dt[:, :, None], (2, N_S5_GROUPS, S5_P))
    g3 = lambda a: a.reshape(2, nc, S5_CHUNK, 1, S5_P)
    g2 = lambda a: a.reshape(2, nc, S5_CHUNK, S5_P)
    ghp = lambda a: a.reshape(2, nc, S5_CHUNK, S5_H, S5_P)
    spec3 = pl.BlockSpec((1, 1, S5_CHUNK, 1, S5_P), lambda d, c: (d, c, 0, 0, 0))
    spec2 = pl.BlockSpec((1, 1, S5_CHUNK, S5_P), lambda d, c: (d, c, 0, 0))
    spec_ghp = pl.BlockSpec((1, 1, S5_CHUNK, S5_H, S5_P), lambda d, c: (d, c, 0, 0, 0))
    return pl.pallas_call(
        _s5_params_kernel,
        grid=(2, nc),
        in_specs=[spec3, spec3, spec3, spec2, spec2, spec2, spec_ghp, spec_ghp, spec_ghp, spec_ghp],
        out_specs=[pl.BlockSpec((1, 1, CH_U, 2 * CH_P), lambda d, c: (d, c, 0, 0)),
                   pl.BlockSpec((1, 1, 2 * CH_P, CH_U), lambda d, c: (d, c, 0, 0)),
                   pl.BlockSpec((1, 1, SUB, 2 * CH_P), lambda d, c: (d, c, 0, 0))],
        out_shape=[jax.ShapeDtypeStruct((2, nc, CH_U, 2 * CH_P), BF16),
                   jax.ShapeDtypeStruct((2, nc, 2 * CH_P, CH_U), BF16),
                   jax.ShapeDtypeStruct((2, nc, SUB, 2 * CH_P), F32)],
        compiler_params=_cparams(("arbitrary", "arbitrary")),
        name="s5_params",
    )(g3(lam_re), g3(lam_im), g3(ldt), g2(lam_re), g2(lam_im), g2(ldt),
      ghp(jnp.swapaxes(b_re, -1, -2)), ghp(jnp.swapaxes(b_im, -1, -2)), ghp(c_re), ghp(c_im))


def _rms(x, g):
    return x * lax.rsqrt(jnp.mean(x * x, axis=-1, keepdims=True) + RMS_EPS) * g


def _load_x(g, xp_ref, xs_ref, pos_ref):
    is_lat = g == N_GRP - 1
    pos = pos_ref[...]
    pos8 = jnp.concatenate([pos, pos], axis=0)
    return jnp.where(is_lat, xs_ref[0] + pos8, xp_ref[0])


def _x_specs():
    n_ctx = N_GRP - 1
    xp_spec = pl.BlockSpec(
        (1, SUB, TC, D),
        lambda g, tc: (jnp.minimum(g, n_ctx - 1), 0, jnp.where(g >= n_ctx, N_TC - 1, tc), 0))
    xs_spec = pl.BlockSpec((1, SUB, TC, D), lambda g, tc: (0, 0, jnp.where(g >= n_ctx, tc, 0), 0))
    pos_spec = pl.BlockSpec((LAT_CHUNKS, TC, D), lambda g, tc: (0, tc, 0))
    return xp_spec, xs_spec, pos_spec


def _perm_matrix():
    p = np.zeros((SUB * PERM_T, SUB * PERM_T), np.float32)
    for s in range(SUB):
        for j in range(PERM_T):
            p[j * SUB + s, s * PERM_T + j] = 1.0
    return p


def _pre_kernel(xp_ref, xs_ref, pos_ref, mod_ref, g1_ref, win_ref, perm_ref, us_ref, uf_ref):
    g = pl.program_id(0)
    x = _load_x(g, xp_ref, xs_ref, pos_ref)
    mod = mod_ref[0]
    shift1 = mod[:, :, 0:D]
    scale1 = mod[:, :, D:2 * D]
    h = _rms(x, g1_ref[...]) * (1.0 + scale1) + shift1
    h2d = h.reshape(SUB * TC, D).astype(BF16)
    proj = jnp.dot(h2d, win_ref[...].astype(BF16), preferred_element_type=F32)
    uf_ref[0] = proj[:, D_S5:].astype(BF16).reshape(SUB, TC, D_FN)
    u = proj[:, :D_S5].astype(BF16)
    perm = perm_ref[...]
    for q in range(TC // PERM_T):
        piece = jnp.concatenate(
            [u[s * TC + q * PERM_T: s * TC + (q + 1) * PERM_T] for s in range(SUB)], axis=0)
        us_ref[0, q * SUB * PERM_T:(q + 1) * SUB * PERM_T, :] = jnp.dot(
            perm, piece, preferred_element_type=F32).astype(BF16)


def _pre(xp4, xs4, pos3, modg, norm1_g, w_in, perm):
    xp_spec, xs_spec, pos_spec = _x_specs()
    return pl.pallas_call(
        _pre_kernel,
        grid=(N_GRP, N_TC),
        in_specs=[xp_spec, xs_spec, pos_spec,
                  pl.BlockSpec((1, SUB, 1, 6 * D), lambda g, tc: (g, 0, 0, 0)),
                  pl.BlockSpec((1, D), lambda g, tc: (0, 0)),
                  pl.BlockSpec((D, D), lambda g, tc: (0, 0)),
                  pl.BlockSpec((SUB * PERM_T, SUB * PERM_T), lambda g, tc: (0, 0))],
        out_specs=[pl.BlockSpec((1, SUB * TC, D_S5), lambda g, tc: (g, tc, 0)),
                   pl.BlockSpec((1, SUB, TC, D_FN), lambda g, tc: (g, 0, tc, 0))],
        out_shape=[jax.ShapeDtypeStruct((N_GRP, GRP_ROWS, D_S5), BF16),
                   jax.ShapeDtypeStruct((N_GRP, SUB, L_BLK, D_FN), BF16)],
        compiler_params=_cparams(("arbitrary", "arbitrary")),
        name="pre_mixer",
    )(xp4, xs4, pos3, modg, norm1_g.reshape(1, D), w_in, perm)


def _cmul(ar, ai, br, bi):
    return ar * br - ai * bi, ar * bi + ai * br


def _s5_kernel(us_ref, bb_ref, cc_ref, a_ref, init_ref, dskip_ref, y_ref, fin_ref, s_ref):
    g = pl.program_id(0)
    d = pl.program_id(2)
    n_rb = GRP_ROWS // ROW_BLK

    def proj_in(rb, _):
        rows = pl.ds(pl.multiple_of(rb * ROW_BLK, ROW_BLK), ROW_BLK)
        s_ref[rows, :] = jnp.dot(us_ref[0, rows, :], bb_ref[0, 0], preferred_element_type=F32)
        return 0

    lax.fori_loop(0, n_rb, proj_in, 0)

    def tile_row(k):
        t = k + d * (L_BLK - 1 - 2 * k)
        return pl.ds(pl.multiple_of(t * SUB, SUB), SUB)

    is_lat = g == N_GRP - 1
    re_all = slice(0, CH_P)
    im_all = slice(CH_P, 2 * CH_P)
    a_re = a_ref[0, 0, :, re_all]
    a_im = a_ref[0, 0, :, im_all]
    zero = jnp.zeros((SUB, CH_P), F32)
    s0_re = jnp.where(is_lat, init_ref[0, 0, :, re_all], zero)
    s0_im = jnp.where(is_lat, init_ref[0, 0, :, im_all], zero)

    def scan_body(i, carry):
        s_re, s_im = carry
        for j in range(8):
            rows = tile_row(i * 8 + j)
            n_re = a_re * s_re - a_im * s_im + s_ref[rows, re_all]
            n_im = a_re * s_im + a_im * s_re + s_ref[rows, im_all]
            s_ref[rows, re_all] = n_re
            s_ref[rows, im_all] = n_im
            s_re, s_im = n_re, n_im
        return s_re, s_im

    f_re, f_im = lax.fori_loop(0, L_BLK // 8, scan_body, (s0_re, s0_im))
    fin_ref[0, 0, :, re_all] = f_re
    fin_ref[0, 0, :, im_all] = f_im

    @pl.when(is_lat)
    def _():
        part = CH_P // FIX_PARTS
        sub_id = lax.broadcasted_iota(jnp.int32, (SUB, part), 0) % LAT_CHUNKS
        fwd = d == 0
        lo = jnp.where(fwd, 1, 0)
        hi = jnp.where(fwd, LAT_CHUNKS - 1, LAT_CHUNKS - 2)
        keep = (sub_id >= lo) & (sub_id <= hi)

        def from_prev(v):
            return jnp.where(keep, jnp.where(fwd, pltpu.roll(v, 1, 0), pltpu.roll(v, SUB - 1, 0)), 0.0)

        for hh in range(FIX_PARTS):
            lanes = slice(hh * part, (hh + 1) * part)
            re_cols = lanes
            im_cols = slice(CH_P + hh * part, CH_P + (hh + 1) * part)
            pa_re, pa_im = a_re[:, lanes], a_im[:, lanes]
            pf_re, pf_im = f_re[:, lanes], f_im[:, lanes]
            p_re, p_im = pa_re, pa_im
            for _ in range(8):
                p_re, p_im = _cmul(p_re, p_im, p_re, p_im)
            t_re, t_im = pf_re, pf_im
            for _ in range(LAT_CHUNKS - 2):
                m_re, m_im = _cmul(p_re, p_im, from_prev(t_re), from_prev(t_im))
                t_re, t_im = pf_re + m_re, pf_im + m_im
            c_re, c_im = from_prev(t_re), from_prev(t_im)

            def fix_body(i, carry, pa_re=pa_re, pa_im=pa_im, re_cols=re_cols, im_cols=im_cols):
                m_re, m_im = carry
                for j in range(8):
                    rows = tile_row(i * 8 + j)
                    m_re, m_im = _cmul(m_re, m_im, pa_re, pa_im)
                    s_ref[rows, re_cols] = s_ref[rows, re_cols] + m_re
                    s_ref[rows, im_cols] = s_ref[rows, im_cols] + m_im
                return m_re, m_im

            lax.fori_loop(0, L_BLK // 8, fix_body, (c_re, c_im))

    def proj_out(rb):
        rows = pl.ds(pl.multiple_of(rb * ROW_BLK, ROW_BLK), ROW_BLK)
        return rows, jnp.dot(s_ref[rows, :].astype(BF16), cc_ref[0, 0], preferred_element_type=F32)

    @pl.when(d == 0)
    def _():
        def body(rb, _):
            rows, yb = proj_out(rb)
            y_ref[0, rows, :] = dskip_ref[...] * us_ref[0, rows, :].astype(F32) + yb
            return 0
        lax.fori_loop(0, n_rb, body, 0)

    @pl.when(d == 1)
    def _():
        def body(rb, _):
            rows, yb = proj_out(rb)
            y_ref[0, rows, :] = y_ref[0, rows, :] + yb
            return 0
        lax.fori_loop(0, n_rb, body, 0)


def _s5(us, bbmat, ccmat, a8, init8, d_skip):
    return pl.pallas_call(
        _s5_kernel,
        grid=(N_GRP, N_S5_CHUNKS, 2),
        in_specs=[pl.BlockSpec((1, GRP_ROWS, CH_U), lambda g, c, d: (g, 0, c)),
                  pl.BlockSpec((1, 1, CH_U, 2 * CH_P), lambda g, c, d: (d, c, 0, 0)),
                  pl.BlockSpec((1, 1, 2 * CH_P, CH_U), lambda g, c, d: (d, c, 0, 0)),
                  pl.BlockSpec((1, 1, SUB, 2 * CH_P), lambda g, c, d: (d, c, 0, 0)),
                  pl.BlockSpec((1, 1, SUB, 2 * CH_P), lambda g, c, d: (d, c, 0, 0)),
                  pl.BlockSpec((1, CH_U), lambda g, c, d: (0, c))],
        out_specs=[pl.BlockSpec((1, GRP_ROWS, CH_U), lambda g, c, d: (g, 0, c)),
                   pl.BlockSpec((1, 1, SUB, 2 * CH_P), lambda g, c, d: (d, g, 0, c))],
        out_shape=[jax.ShapeDtypeStruct((N_GRP, GRP_ROWS, D_S5), F32),
                   jax.ShapeDtypeStruct((2, N_GRP, SUB, N_S5_CHUNKS * 2 * CH_P), F32)],
        scratch_shapes=[pltpu.VMEM((GRP_ROWS, 2 * CH_P), F32)],
        compiler_params=_cparams(("arbitrary", "arbitrary", "arbitrary")),
        name="s5_scan",
    )(us, bbmat, ccmat, a8, init8, d_skip.reshape(1, D_S5))


def _fnet_w_kernel(c_ref, s_ref, w_ref, m1_ref, m2_ref):
    w = w_ref[...]
    m1_ref[...] = jnp.dot(c_ref[...], w, preferred_element_type=F32, precision=HIGHEST).astype(BF16)
    m2_ref[...] = jnp.dot(s_ref[...], w, preferred_element_type=F32, precision=HIGHEST).astype(BF16)


FN_SEQ = LAT_CHUNKS
FN_CTX_STEPS = (N_GRP - 1) * SUB // FN_SEQ


def _fnet_kernel(u_ref, cos_s_ref, sin_s_ref, cos_l_ref, sin_l_ref, m1_ref, m2_ref, o_ref):
    i = pl.program_id(0)
    m1 = m1_ref[...]
    m2 = m2_ref[...]

    def mix(u, cos_ref, sin_ref):
        v1 = jnp.dot(u, m1, preferred_element_type=F32).astype(BF16)
        v2 = jnp.dot(u, m2, preferred_element_type=F32).astype(BF16)
        return (jnp.dot(cos_ref[...].astype(BF16), v1, preferred_element_type=F32)
                - jnp.dot(sin_ref[...].astype(BF16), v2, preferred_element_type=F32)).astype(BF16)

    @pl.when(i < FN_CTX_STEPS)
    def _():
        for s in range(FN_SEQ):
            o_ref[0, s] = mix(u_ref[0, s], cos_s_ref, sin_s_ref)

    @pl.when(i >= FN_CTX_STEPS)
    def _():
        u = u_ref[0].reshape(FN_SEQ * L_BLK, D_FN)
        o_ref[0] = mix(u, cos_l_ref, sin_l_ref).reshape(FN_SEQ, L_BLK, D_FN)


def _dft_tables(n):
    k = np.arange(n, dtype=np.int64)
    ang = (2.0 * np.pi / n) * ((k[:, None] * k[None, :]) % n).astype(np.float64)
    scale = 1.0 / math.sqrt(n)
    return (np.cos(ang) * scale).astype(np.float32), (np.sin(ang) * scale).astype(np.float32)


def _fnet(uf, m1, m2):
    n_lat_steps = SUB // FN_SEQ
    cos_s, sin_s = _dft_tables(L_BLK)
    cos_l, sin_l = _dft_tables(FN_SEQ * L_BLK)
    full = lambda i: (0, 0)
    per_grp = SUB // FN_SEQ
    blk = pl.BlockSpec(
        (1, FN_SEQ, L_BLK, D_FN),
        lambda i: (jnp.minimum(i // per_grp, N_GRP - 1),
                   jnp.where(i < FN_CTX_STEPS, i % per_grp, i - FN_CTX_STEPS), 0, 0))
    table = lambda n: pl.BlockSpec((n, n), full)
    return pl.pallas_call(
        _fnet_kernel,
        grid=(FN_CTX_STEPS + n_lat_steps,),
        in_specs=[blk, table(L_BLK), table(L_BLK), table(FN_SEQ * L_BLK), table(FN_SEQ * L_BLK),
                  table(D_FN), table(D_FN)],
        out_specs=blk,
        out_shape=jax.ShapeDtypeStruct(uf.shape, BF16),
        compiler_params=_cparams(("arbitrary",)),
        name="fnet",
    )(uf, jnp.asarray(cos_s), jnp.asarray(sin_s), jnp.asarray(cos_l), jnp.asarray(sin_l), m1, m2)


def _fnet_weights(w_fnet):
    n_g = D_FN // FN_GW
    cos_c, sin_c = _dft_tables(FN_GW)
    eye = np.eye(n_g, dtype=np.float32)
    cos_bd = np.kron(eye, cos_c)
    sin_bd = np.kron(eye, sin_c)
    w_bd = (w_fnet[:, :, None, :] * jnp.asarray(eye)[:, None, :, None]).reshape(D_FN, D_FN)
    out = jax.ShapeDtypeStruct((D_FN, D_FN), BF16)
    return pl.pallas_call(_fnet_w_kernel, out_shape=(out, out), name="fnet_weights")(
        jnp.asarray(cos_bd), jnp.asarray(sin_bd), w_bd)


def _gelu_tanh(x):
    return 0.5 * x * (1.0 + jnp.tanh(math.sqrt(2.0 / math.pi) * (x + 0.044715 * (x * x * x))))


def _post_kernel(y_ref, yf_ref, xp_ref, xs_ref, pos_ref, mod_ref, wglu_ref, wout_ref, g2_ref, wr_ref,
                 br_ref, permt_ref, x1_ref, h2_ref, comb_ref):
    g = pl.program_id(0)
    z = _gelu_tanh(y_ref[0])
    gate = jnp.dot(z.astype(BF16), wglu_ref[...].astype(BF16), preferred_element_type=F32)
    gl = (z * jax.nn.sigmoid(gate)).astype(BF16)
    permt = permt_ref[...]
    n_q = TC // PERM_T
    nat = [jnp.dot(permt, gl[q * SUB * PERM_T:(q + 1) * SUB * PERM_T], preferred_element_type=F32).astype(BF16)
           for q in range(n_q)]
    gl_nat = jnp.concatenate(
        [nat[q][s * PERM_T:(s + 1) * PERM_T] for s in range(SUB) for q in range(n_q)], axis=0)
    w_out = wout_ref[...].astype(BF16)
    mixed = (jnp.dot(gl_nat, w_out[:D_S5], preferred_element_type=F32)
             + jnp.dot(yf_ref[0].reshape(SUB * TC, D_FN), w_out[D_S5:], preferred_element_type=F32))
    x = _load_x(g, xp_ref, xs_ref, pos_ref)
    mod = mod_ref[0]
    gate1 = mod[:, :, 2 * D:3 * D]
    shift2 = mod[:, :, 3 * D:4 * D]
    scale2 = mod[:, :, 4 * D:5 * D]
    x1 = x + gate1 * mixed.reshape(SUB, TC, D)
    x1_ref[0] = x1
    h2 = _rms(x1, g2_ref[...]) * (1.0 + scale2) + shift2
    h2_ref[0] = h2.astype(BF16)

    hr = h2.reshape(SUB * TC, D)
    h_hi = hr.astype(BF16)
    h_lo = (hr - h_hi.astype(F32)).astype(BF16)
    wr = wr_ref[...]
    w_hi = wr.astype(BF16)
    w_lo = (wr - w_hi.astype(F32)).astype(BF16)
    logits = (jnp.dot(h_hi, w_hi, preferred_element_type=F32) + jnp.dot(h_lo, w_hi, preferred_element_type=F32)
              + jnp.dot(h_hi, w_lo, preferred_element_type=F32) + br_ref[...])
    lane = lax.broadcasted_iota(jnp.int32, logits.shape, 1).astype(F32)
    top_v, hots = [], []
    cur = logits
    for _ in range(TOP_K):
        m = jnp.max(cur, axis=-1, keepdims=True)
        idx = jnp.min(jnp.where(cur == m, lane, float(E_PAD)), axis=-1, keepdims=True)
        hot = lane == idx
        top_v.append(m)
        hots.append(hot)
        cur = jnp.where(hot, -3.0e38, cur)
    exps = [jnp.exp(v - top_v[0]) for v in top_v]
    denom = exps[0] + exps[1] + exps[2] + exps[3]
    comb = jnp.zeros(logits.shape, F32)
    for k in range(TOP_K):
        comb = comb + jnp.where(hots[k], exps[k] / denom, 0.0)
    comb_ref[0] = comb.reshape(SUB, TC, E_PAD)


def _post(y, yf, xp4, xs4, pos3, modg, w_glu, w_out, norm2_g, w_router, b_router, permt):
    xp_spec, xs_spec, pos_spec = _x_specs()
    const2 = lambda g, tc: (0, 0)
    wr = jnp.zeros((D, E_PAD), F32).at[:, :N_EXPERTS].set(w_router)
    br = jnp.full((1, E_PAD), -1.0e30, F32).at[0, :N_EXPERTS].set(b_router)
    blk = lambda w: pl.BlockSpec((1, SUB, TC, w), lambda g, tc: (g, 0, tc, 0))
    return pl.pallas_call(
        _post_kernel,
        grid=(N_GRP, N_TC),
        in_specs=[pl.BlockSpec((1, SUB * TC, D_S5), lambda g, tc: (g, tc, 0)),
                  blk(D_FN), xp_spec, xs_spec, pos_spec,
                  pl.BlockSpec((1, SUB, 1, 6 * D), lambda g, tc: (g, 0, 0, 0)),
                  pl.BlockSpec((D_S5, D_S5), const2),
                  pl.BlockSpec((D, D), const2),
                  pl.BlockSpec((1, D), const2),
                  pl.BlockSpec((D, E_PAD), const2),
                  pl.BlockSpec((1, E_PAD), const2),
                  pl.BlockSpec((SUB * PERM_T, SUB * PERM_T), const2)],
        out_specs=[blk(D), blk(D), blk(E_PAD)],
        out_shape=[jax.ShapeDtypeStruct((N_GRP, SUB, L_BLK, D), F32),
                   jax.ShapeDtypeStruct((N_GRP, SUB, L_BLK, D), BF16),
                   jax.ShapeDtypeStruct((N_GRP, SUB, L_BLK, E_PAD), F32)],
        compiler_params=_cparams(("arbitrary", "arbitrary")),
        name="post_mixer",
    )(y, yf, xp4, xs4, pos3, modg, w_glu, w_out, norm2_g.reshape(1, D), wr, br, permt)


TBLK = L_BLK
N_BLK = T_TOK // TBLK
N_CTX_BLK = (N_GRP - 1) * SUB
SEG_ALIGN = 8
RB = 1280
assert RB >= TBLK * TOP_K + N_EXPERTS * (SEG_ALIGN - 1) and RB % 128 == 0
TM = 256
TM_SHIFT = 8
assert 1 << TM_SHIFT == TM
R_TOT = T_TOK * TOP_K + N_BLK * N_EXPERTS * (SEG_ALIGN - 1) + TM
TAB_ROWS = 32
assert TAB_ROWS >= N_BLK
BIG_ROWS = 32
BIG_SHIFT = 2
assert SEG_ALIGN << BIG_SHIFT == BIG_ROWS


def _plan_kernel(comb_ref, z_ref, ptab_ref, loff_ref, gst_ref, eoff_ref):
    row = lax.broadcasted_iota(jnp.int32, (TBLK, TBLK), 0)
    col = lax.broadcasted_iota(jnp.int32, (TBLK, TBLK), 1)
    earlier = jnp.where(row > col, 1.0, 0.0).astype(BF16)
    ptab_ref[...] = jnp.zeros_like(ptab_ref)

    def body(b, _):
        rows = pl.ds(pl.multiple_of(b * TBLK, TBLK), TBLK)
        m = jnp.where(comb_ref[rows, :] > 0.0, 1.0, 0.0)
        rank = jnp.dot(earlier, m.astype(BF16), preferred_element_type=F32)
        z_ref[rows, :] = m * (rank + 1.0)
        n = jnp.sum(m, axis=0, keepdims=True)
        ptab_ref[pl.ds(b, 1), :] = jnp.floor((n + (SEG_ALIGN - 1)) * (1.0 / SEG_ALIGN)) * SEG_ALIGN
        return 0

    lax.fori_loop(0, N_BLK, body, 0)

    ptab = ptab_ref[...]
    er = lax.broadcasted_iota(jnp.int32, (E_PAD, E_PAD), 0)
    ec = lax.broadcasted_iota(jnp.int32, (E_PAD, E_PAD), 1)
    before = jnp.where(er < ec, 1.0, 0.0)
    exact = functools.partial(jnp.dot, preferred_element_type=F32, precision=HIGHEST)
    loff_ref[...] = exact(ptab, before)
    tot = jnp.sum(ptab, axis=0, keepdims=True)
    eoff = exact(jnp.broadcast_to(tot, (SUB, E_PAD)), before)
    eoff_ref[...] = eoff
    br = lax.broadcasted_iota(jnp.int32, (TAB_ROWS, TAB_ROWS), 0)
    bc = lax.broadcasted_iota(jnp.int32, (TAB_ROWS, TAB_ROWS), 1)
    gst_ref[...] = eoff[0:1] + exact(jnp.where(br > bc, 1.0, 0.0), ptab)


def _plan(comb):
    tab = jax.ShapeDtypeStruct((TAB_ROWS, E_PAD), F32)
    return pl.pallas_call(
        _plan_kernel,
        out_shape=(jax.ShapeDtypeStruct((T_TOK, E_PAD), F32), tab, tab, tab,
                   jax.ShapeDtypeStruct((SUB, E_PAD), F32)),
        compiler_params=pltpu.CompilerParams(vmem_limit_bytes=VMEM_LIMIT),
        name="moe_plan",
    )(comb)


def _sort_matrix(b, z_ref, loff_ref, ptab_ref):
    loff = loff_ref[pl.ds(b, 1), :]
    size = ptab_ref[pl.ds(b, 1), :]
    r = lax.broadcasted_iota(jnp.int32, (RB, E_PAD), 0).astype(F32)
    owner = jnp.where(r >= loff, jnp.where(r < loff + size, 1.0, 0.0), 0.0)
    rank1 = r[:, 0:1] - jnp.sum(owner * loff, axis=-1, keepdims=True) + 1.0
    zt = z_ref[...].T
    v = jnp.dot(owner.astype(BF16), zt.astype(BF16), preferred_element_type=F32)
    return jnp.where(v == rank1, 1.0, 0.0), owner


def _segment_copies(b, loff_s, gst_s, p8_s, make_copy):
    def per_expert(e, counts):
        k = b * N_EXPERTS + e
        chunks = p8_s[k]
        n_big = lax.shift_right_logical(chunks, BIG_SHIFT)
        n_small = chunks - n_big * (BIG_ROWS // SEG_ALIGN)
        local0 = loff_s[k]
        global0 = gst_s[k]

        def big(j, _):
            make_copy(pl.multiple_of(local0 + j * BIG_ROWS, SEG_ALIGN),
                      pl.multiple_of(global0 + j * BIG_ROWS, SEG_ALIGN), BIG_ROWS).start()
            return 0

        def small(j, _):
            off = n_big * BIG_ROWS + j * SEG_ALIGN
            make_copy(pl.multiple_of(local0 + off, SEG_ALIGN),
                      pl.multiple_of(global0 + off, SEG_ALIGN), SEG_ALIGN).start()
            return 0

        lax.fori_loop(0, n_big, big, 0)
        lax.fori_loop(0, n_small, small, 0)
        return counts[0] + n_big, counts[1] + n_small

    return lax.fori_loop(0, N_EXPERTS, per_expert, (0, 0))


def _wait_copies(counts, make_copy):
    for n, rows in zip(counts, (BIG_ROWS, SEG_ALIGN)):
        def wait_one(i, _, rows=rows):
            make_copy(0, 0, rows).wait()
            return 0
        lax.fori_loop(0, n, wait_one, 0)


def _zero_rows_from(hbm, zeros_vmem, first_row, sem):
    def copy_to(row):
        return pltpu.make_async_copy(zeros_vmem, hbm.at[pl.ds(row, TM)], sem)

    n_full = lax.shift_right_logical(R_TOT - first_row, TM_SHIFT)

    def start_one(j, _):
        copy_to(pl.multiple_of(first_row + j * TM, SEG_ALIGN)).start()
        return 0

    def wait_one(j, _):
        copy_to(0).wait()
        return 0

    lax.fori_loop(0, n_full, start_one, 0)
    lax.fori_loop(0, n_full, wait_one, 0)
    last = copy_to(R_TOT - TM)
    last.start()
    last.wait()


def _dispatch_kernel(loff_s, gst_s, p8_s, tot_s, h_ref, z_ref, loff_ref, ptab_ref, xs_hbm, xbuf, sem, pending):
    b = pl.program_id(0)

    def make_copy(local_row, global_row, rows):
        return pltpu.make_async_copy(xbuf.at[pl.ds(local_row, rows)], xs_hbm.at[pl.ds(global_row, rows)], sem)

    pm = _sort_matrix(b, z_ref, loff_ref, ptab_ref)[0].astype(BF16)

    @pl.when(b > 0)
    def _():
        _wait_copies((pending[0], pending[1]), make_copy)

    xbuf[...] = jnp.dot(pm, h_ref[...], preferred_element_type=F32)
    n_big, n_small = _segment_copies(b, loff_s, gst_s, p8_s, make_copy)
    pending[0] = n_big
    pending[1] = n_small

    @pl.when(b == N_BLK - 1)
    def _():
        _wait_copies((n_big, n_small), make_copy)
        xbuf[0:TM, :] = jnp.zeros((TM, D), F32)
        _zero_rows_from(xs_hbm, xbuf.at[0:TM], tot_s[0], sem)


def _dispatch(h2, z, loff, ptab, loff_i, gst_i, p8_i, tot_i):
    whole = pl.BlockSpec((TAB_ROWS, E_PAD), lambda b, *_: (0, 0))
    return pl.pallas_call(
        _dispatch_kernel,
        grid_spec=pltpu.PrefetchScalarGridSpec(
            num_scalar_prefetch=4,
            grid=(N_BLK,),
            in_specs=[pl.BlockSpec((TBLK, D), lambda b, *_: (b, 0)),
                      pl.BlockSpec((TBLK, E_PAD), lambda b, *_: (b, 0)),
                      whole, whole],
            out_specs=pl.BlockSpec(memory_space=pl.ANY),
            scratch_shapes=[pltpu.VMEM((RB, D), F32), pltpu.SemaphoreType.DMA(()),
                            pltpu.SMEM((2,), jnp.int32)]),
        out_shape=jax.ShapeDtypeStruct((R_TOT, D), F32),
        compiler_params=_cparams(("arbitrary",)),
        name="moe_dispatch",
    )(loff_i, gst_i, p8_i, tot_i, h2, z, loff, ptab)


MAX_TILES = R_TOT // TM + N_EXPERTS
X_AHEAD = 3
X_SLOTS = X_AHEAD + 1
Y_SLOTS = 2
W_AHEAD = 2
W_SLOTS = W_AHEAD + 1
N_MATS = 3
FF_CHUNK = 256
CAST_ROWS = 64
TILE_STEP = 64


def _expert_kernel(eoff_s, xs_hbm, wg_hbm, wu_hbm, wd_hbm, bg_ref, bu_ref, bd_ref, ys_hbm,
                   wst, wbf, xin, yout, act_ref, w_sem, x_sem, y_sem, t_exp, t_row, t_first, t_valid, live):
    def add_expert(e, carry):
        n_t, n_live = carry
        start = eoff_s[e]
        count = eoff_s[e + 1] - start
        tiles = lax.shift_right_logical(count + (TM - 1), TM_SHIFT)

        def add_tile(i, _):
            t_exp[n_t + i] = e
            t_row[n_t + i] = start + i * TM
            t_first[n_t + i] = jnp.where(i == 0, 1, 0)
            t_valid[n_t + i] = jnp.minimum(count - i * TM, TM)
            return 0

        lax.fori_loop(0, tiles, add_tile, 0)
        has_rows = jnp.where(tiles > 0, 1, 0)

        @pl.when(tiles > 0)
        def _():
            live[n_live] = e

        return n_t + tiles, n_live + has_rows

    n_tiles, n_live = lax.fori_loop(0, N_EXPERTS, add_expert, (0, 0))

    def w_copies(e, slot):
        return [pltpu.make_async_copy(w.at[e], wst.at[slot, m], w_sem.at[slot])
                for m, w in enumerate((wg_hbm, wu_hbm, wd_hbm))]

    def tile_rows(j):
        return pl.ds(pl.multiple_of(t_row[j], SEG_ALIGN), TM)

    def x_copy(j, slot):
        return pltpu.make_async_copy(xs_hbm.at[tile_rows(j)], xin.at[slot], x_sem.at[slot])

    def y_pieces(j, op):
        ys = lax.rem(j, Y_SLOTS)
        valid = t_valid[j]
        row0 = t_row[j]

        def piece(off, rows):
            return pltpu.make_async_copy(yout.at[ys, pl.ds(off, rows)],
                                         ys_hbm.at[pl.ds(pl.multiple_of(row0 + off, SEG_ALIGN), rows)],
                                         y_sem.at[ys])

        @pl.when(valid == TM)
        def _():
            op(piece(0, TM))

        @pl.when(valid < TM)
        def _():
            n_big = lax.shift_right_logical(valid, BIG_ROWS.bit_length() - 1)
            n_small = lax.shift_right_logical(valid - n_big * BIG_ROWS, SEG_ALIGN.bit_length() - 1)

            def big(i, _):
                op(piece(pl.multiple_of(i * BIG_ROWS, SEG_ALIGN), BIG_ROWS))
                return 0

            def small(i, _):
                op(piece(pl.multiple_of(n_big * BIG_ROWS + i * SEG_ALIGN, SEG_ALIGN), SEG_ALIGN))
                return 0

            lax.fori_loop(0, n_big, big, 0)
            lax.fori_loop(0, n_small, small, 0)

    for ahead in range(W_AHEAD):
        @pl.when(n_live > ahead)
        def _(ahead=ahead):
            for cp in w_copies(live[ahead], ahead):
                cp.start()

    for ahead in range(X_AHEAD):
        @pl.when(n_tiles > ahead)
        def _(ahead=ahead):
            x_copy(ahead, ahead).start()

    def body(j, k):
        e = t_exp[j]

        @pl.when(t_first[j] == 1)
        def _():
            ws = lax.rem(k, W_SLOTS)

            @pl.when(k + W_AHEAD < n_live)
            def _():
                for cp in w_copies(live[k + W_AHEAD], lax.rem(k + W_AHEAD, W_SLOTS)):
                    cp.start()

            for cp in w_copies(e, ws):
                cp.wait()

            def cast_rows(r, _):
                rows = pl.ds(pl.multiple_of(r * CAST_ROWS, CAST_ROWS), CAST_ROWS)
                for m in range(N_MATS):
                    wbf[m, rows, :] = wst[ws, m, rows, :].astype(BF16)
                return 0

            lax.fori_loop(0, D // CAST_ROWS, cast_rows, 0)

        slot = lax.rem(j, X_SLOTS)
        x_copy(j, slot).wait()

        @pl.when(j + X_AHEAD < n_tiles)
        def _():
            x_copy(j + X_AHEAD, lax.rem(j + X_AHEAD, X_SLOTS)).start()

        @pl.when(j >= Y_SLOTS)
        def _():
            y_pieces(j - Y_SLOTS, lambda cp: cp.wait())

        ys = lax.rem(j, Y_SLOTS)

        def compute(rows):
            x = xin[slot, 0:rows, :].astype(BF16)
            for c in range(D // FF_CHUNK):
                cols = slice(c * FF_CHUNK, (c + 1) * FF_CHUNK)
                gate = jnp.dot(x, wbf[0, :, cols], preferred_element_type=F32) + bg_ref[e][:, cols]
                up = jnp.dot(x, wbf[1, :, cols], preferred_element_type=F32) + bu_ref[e][:, cols]
                gate = jnp.minimum(gate, SWIGLU_LIMIT)
                up = jnp.clip(up, -SWIGLU_LIMIT, SWIGLU_LIMIT)
                act_ref[0:rows, cols] = ((up + 1.0) * gate * jax.nn.sigmoid(SWIGLU_ALPHA * gate)).astype(BF16)
            yout[ys, 0:rows, :] = jnp.dot(act_ref[0:rows, :], wbf[2], preferred_element_type=F32) + bd_ref[e]

        valid = t_valid[j]
        for rows in range(TILE_STEP, TM + 1, TILE_STEP):
            @pl.when(jnp.logical_and(valid > rows - TILE_STEP, valid <= rows))
            def _(rows=rows):
                compute(rows)

        y_pieces(j, lambda cp: cp.start())
        return k + t_first[j]

    lax.fori_loop(0, n_tiles, body, 0)

    for back in range(Y_SLOTS, 0, -1):
        @pl.when(n_tiles >= back)
        def _(back=back):
            y_pieces(n_tiles - back, lambda cp: cp.wait())

    yout[0] = jnp.zeros((TM, D), F32)
    _zero_rows_from(ys_hbm, yout.at[0], eoff_s[N_EXPERTS], y_sem.at[0])


def _experts(xs, eoff_i, w_gate, b_gate, w_up, b_up, w_down, b_down):
    hbm = pl.BlockSpec(memory_space=pl.ANY)
    bspec = pl.BlockSpec((N_EXPERTS, 1, D), lambda i, *_: (0, 0, 0))
    return pl.pallas_call(
        _expert_kernel,
        grid_spec=pltpu.PrefetchScalarGridSpec(
            num_scalar_prefetch=1,
            grid=(1,),
            in_specs=[hbm, hbm, hbm, hbm, bspec, bspec, bspec],
            out_specs=hbm,
            scratch_shapes=[pltpu.VMEM((W_SLOTS, N_MATS, D, D), F32), pltpu.VMEM((N_MATS, D, D), BF16),
                            pltpu.VMEM((X_SLOTS, TM, D), F32), pltpu.VMEM((Y_SLOTS, TM, D), F32),
                            pltpu.VMEM((TM, D), BF16),
                            pltpu.SemaphoreType.DMA((W_SLOTS,)), pltpu.SemaphoreType.DMA((X_SLOTS,)),
                            pltpu.SemaphoreType.DMA((Y_SLOTS,)),
                            pltpu.SMEM((MAX_TILES,), jnp.int32), pltpu.SMEM((MAX_TILES,), jnp.int32),
                            pltpu.SMEM((MAX_TILES,), jnp.int32), pltpu.SMEM((MAX_TILES,), jnp.int32),
                            pltpu.SMEM((N_EXPERTS,), jnp.int32)]),
        out_shape=jax.ShapeDtypeStruct((R_TOT, D), F32),
        compiler_params=_cparams(("arbitrary",)),
        name="moe_experts",
    )(eoff_i, xs, w_gate, w_up, w_down, b_gate.reshape(N_EXPERTS, 1, D), b_up.reshape(N_EXPERTS, 1, D),
      b_down.reshape(N_EXPERTS, 1, D))


def _combine_kernel(loff_s, gst_s, p8_s, ys_hbm, z_ref, comb_ref, loff_ref, ptab_ref, x1_ref, mod_ref, gf_ref,
                    yctx_ref, ylat_ref, ybuf, sem):
    b = pl.program_id(0)

    def make_copy(local_row, global_row, rows):
        return pltpu.make_async_copy(ys_hbm.at[pl.ds(global_row, rows)], ybuf.at[pl.ds(local_row, rows)], sem)

    started = _segment_copies(b, loff_s, gst_s, p8_s, make_copy)

    last = b * N_EXPERTS + N_EXPERTS - 1
    used_chunks = lax.shift_right_logical(loff_s[last], 3) + p8_s[last]

    def zero_chunk(j, _):
        ybuf[pl.ds(pl.multiple_of(j * SEG_ALIGN, SEG_ALIGN), SEG_ALIGN), :] = jnp.zeros((SEG_ALIGN, D), F32)
        return 0

    lax.fori_loop(used_chunks, RB // SEG_ALIGN, zero_chunk, 0)

    pm, owner = _sort_matrix(b, z_ref, loff_ref, ptab_ref)
    comb = comb_ref[...]
    c_hi = comb.astype(BF16)
    rest = comb - c_hi.astype(F32)
    c_mid = rest.astype(BF16)
    c_lo = (rest - c_mid.astype(F32)).astype(BF16)
    pmb = pm.astype(BF16)
    moved = (jnp.dot(pmb, c_hi, preferred_element_type=F32) + jnp.dot(pmb, c_mid, preferred_element_type=F32)
             + jnp.dot(pmb, c_lo, preferred_element_type=F32))
    w_row = jnp.sum(owner * moved, axis=-1, keepdims=True)
    pmt = pm.T.astype(BF16)
    _wait_copies(started, make_copy)
    y = (ybuf[...] * w_row).astype(BF16)
    moe = jnp.dot(pmt, y, preferred_element_type=F32)
    gate2 = mod_ref[0][:, 5 * D:6 * D]
    x2 = x1_ref[0] + gate2 * moe
    y = _rms(x2, gf_ref[...])

    @pl.when(b < N_CTX_BLK)
    def _():
        yctx_ref[0] = y

    @pl.when(b >= N_CTX_BLK)
    def _():
        ylat_ref[0] = y


def _combine(ys, z, comb, loff, ptab, x1, modv, norm_f_g, loff_i, gst_i, p8_i):
    whole = pl.BlockSpec((TAB_ROWS, E_PAD), lambda b, *_: (0, 0))
    tok = pl.BlockSpec((TBLK, E_PAD), lambda b, *_: (b, 0))
    return pl.pallas_call(
        _combine_kernel,
        grid_spec=pltpu.PrefetchScalarGridSpec(
            num_scalar_prefetch=3,
            grid=(N_BLK,),
            in_specs=[pl.BlockSpec(memory_space=pl.ANY), tok, tok, whole, whole,
                      pl.BlockSpec((1, TBLK, D), lambda b, *_: (b, 0, 0)),
                      pl.BlockSpec((1, 1, 6 * D), lambda b, *_: (b, 0, 0)),
                      pl.BlockSpec((1, D), lambda b, *_: (0, 0))],
            out_specs=[pl.BlockSpec((1, TBLK, D), lambda b, *_: (jnp.minimum(b, N_CTX_BLK - 1), 0, 0)),
                       pl.BlockSpec((1, TBLK, D), lambda b, *_: (jnp.maximum(b - N_CTX_BLK, 0), 0, 0))],
            scratch_shapes=[pltpu.VMEM((RB, D), F32), pltpu.SemaphoreType.DMA(())]),
        out_shape=[jax.ShapeDtypeStruct((N_CTX_BLK, TBLK, D), F32),
                   jax.ShapeDtypeStruct((N_BLK - N_CTX_BLK, TBLK, D), F32)],
        compiler_params=_cparams(("arbitrary",)),
        name="moe_combine",
    )(loff_i, gst_i, p8_i, ys, z, comb, loff, ptab, x1, modv, norm_f_g.reshape(1, D))


def _moe_and_final(x1, h2, comb, modg, norm_f_g, w_gate, b_gate, w_up, b_up, w_down, b_down):
    z, ptab, loff, gst, eoff = _plan(comb)
    as_scalars = lambda t: t[:N_BLK, :N_EXPERTS].astype(jnp.int32).reshape(N_BLK * N_EXPERTS)
    loff_i, gst_i = as_scalars(loff), as_scalars(gst)
    p8_i = as_scalars(ptab) // SEG_ALIGN
    eoff_i = eoff[0, :N_EXPERTS + 1].astype(jnp.int32)
    xs = _dispatch(h2, z, loff, ptab, loff_i, gst_i, p8_i, eoff_i[N_EXPERTS:])
    ys = _experts(xs, eoff_i, w_gate, b_gate, w_up, b_up, w_down, b_down)
    return _combine(ys, z, comb, loff, ptab, x1, modg.reshape(N_BLK, 1, 6 * D), norm_f_g,
                    loff_i, gst_i, p8_i)


def _grid_pos_embed(n_tokens, dim):
    rows = n_tokens // GRID_W
    t = np.arange(rows * GRID_W)
    r = (t // GRID_W).astype(np.float32)
    col = (t % GRID_W).astype(np.float32)
    q = dim // 4
    omega = (1.0 / np.float32(POS_TEMP) ** (np.arange(q, dtype=np.float32) / np.float32(q))).astype(np.float32)

    def emb(p):
        a = p[:, None] * omega[None, :]
        return np.concatenate([np.sin(a), np.cos(a)], axis=-1)

    return np.concatenate([emb(r), emb(col)], axis=-1).astype(np.float32)


def kernel(x_prompt, x_sample, c, state_s5_re, state_s5_im, c_ctx, w_ada, b_ada, norm1_g, w_in, s5_lam_re,
           s5_lam_im, s5_log_dt, s5_b_re, s5_b_im, s5_c_re, s5_c_im, s5_d, s5_w_glu, w_fnet, w_out, norm2_g,
           w_router, b_router, w_gate, b_gate, w_up, b_up, w_down, b_down, norm_f_g):
    n_ctx, n_lat = x_prompt.shape[0], x_sample.shape[0]
    assert x_prompt.shape == (SUB * (N_GRP - 1), L_BLK, D) and x_sample.shape == (2, LAT_CHUNKS * L_BLK, D)
    assert w_ada.shape[0] == 1, "one trunk layer"
    layer = 0

    cvec = jnp.concatenate([jnp.broadcast_to(c_ctx[None], (n_ctx, D)), jnp.repeat(c, LAT_CHUNKS, axis=0)], axis=0)
    modg = _adaln(cvec, w_ada[layer], b_ada[layer]).reshape(N_GRP, SUB, 1, 6 * D)

    xp4 = x_prompt.reshape(N_GRP - 1, SUB, L_BLK, D)
    xs4 = x_sample.reshape(1, SUB, L_BLK, D)
    pos3 = jnp.asarray(_grid_pos_embed(LAT_CHUNKS * L_BLK, D).reshape(LAT_CHUNKS, L_BLK, D))
    perm = _perm_matrix()

    us, uf = _pre(xp4, xs4, pos3, modg, norm1_g[layer], w_in[layer], jnp.asarray(perm, BF16))

    bbmat, ccmat, a8 = _s5_params(s5_lam_re[layer], s5_lam_im[layer], s5_log_dt[layer], s5_b_re[layer],
                                  s5_b_im[layer], s5_c_re[layer], s5_c_im[layer])

    st = jnp.stack([state_s5_re[:, layer], state_s5_im[:, layer]], axis=2)
    st = st.reshape(n_lat, 2, 2, N_S5_CHUNKS, CH_P)
    st = jnp.transpose(st, (1, 3, 0, 2, 4)).reshape(2, N_S5_CHUNKS, n_lat, 2 * CH_P)
    init8 = jnp.zeros((2, N_S5_CHUNKS, n_lat, LAT_CHUNKS, 2 * CH_P), F32)
    init8 = init8.at[0, :, :, 0].set(st[0]).at[1, :, :, LAT_CHUNKS - 1].set(st[1])
    init8 = init8.reshape(2, N_S5_CHUNKS, SUB, 2 * CH_P)

    y_s5, fin = _s5(us, bbmat, ccmat, a8, init8, s5_d[layer])

    m1, m2 = _fnet_weights(w_fnet[layer])
    yf = _fnet(uf, m1, m2)

    x1, h2, comb = _post(y_s5, yf, xp4, xs4, pos3, modg, s5_w_glu[layer], w_out[layer], norm2_g[layer],
                         w_router[layer], b_router[layer], jnp.asarray(perm.T, BF16))

    y_prompt, y_lat = _moe_and_final(x1.reshape(N_BLK, TBLK, D), h2.reshape(T_TOK, D), comb.reshape(T_TOK, E_PAD),
                                     modg, norm_f_g, w_gate[layer], b_gate[layer], w_up[layer], b_up[layer],
                                     w_down[layer], b_down[layer])
    y_sample = y_lat.reshape(n_lat, LAT_CHUNKS * L_BLK, D)

    fin = fin[:, :N_GRP - 1].reshape(2, N_GRP - 1, SUB, N_S5_CHUNKS, 2, CH_P)
    fin = jnp.transpose(fin, (4, 1, 2, 0, 3, 5)).reshape(2, n_ctx, 1, 2, N_S5_GROUPS, S5_P)
    return (y_prompt, y_sample, fin[0], fin[1])
```

```python
import functools
import math

import numpy as np
import jax
import jax.numpy as jnp
from jax import lax
from jax.experimental import pallas as pl
from jax.experimental.pallas import tpu as pltpu

F32 = jnp.float32
BF16 = jnp.bfloat16
HIGHEST = lax.Precision.HIGHEST

D = 1024
D_S5 = 768
S5_H = 16
S5_P = 64
N_S5_GROUPS = 48
D_FN = 256
FN_GW = 64
N_EXPERTS = 32
TOP_K = 4
E_PAD = 128
SWIGLU_LIMIT = 7.0
SWIGLU_ALPHA = 1.702
RMS_EPS = 1e-6
POS_TEMP = 10000.0
GRID_W = 64

L_BLK = 256
SUB = 8
N_GRP = 3
GRP_ROWS = L_BLK * SUB
LAT_CHUNKS = 4
T_TOK = N_GRP * GRP_ROWS
TC = 128
N_TC = L_BLK // TC
PERM_T = 32
S5_CHUNK = 16
N_S5_CHUNKS = N_S5_GROUPS // S5_CHUNK
CH_U = S5_CHUNK * S5_H
CH_P = S5_CHUNK * S5_P
ROW_BLK = 512
FIX_PARTS = 1
VMEM_LIMIT = 56 * 1024 * 1024


def _cparams(sem):
    return pltpu.CompilerParams(dimension_semantics=sem, vmem_limit_bytes=VMEM_LIMIT)


def _adaln_kernel(c_ref, w_ref, b_ref, o_ref):
    c = c_ref[...]
    s = c * jax.nn.sigmoid(c)
    s_hi = s.astype(BF16)
    s_lo = (s - s_hi.astype(F32)).astype(BF16)
    w = w_ref[...]
    w_hi = w.astype(BF16)
    w_lo = (w - w_hi.astype(F32)).astype(BF16)
    o_ref[...] = (jnp.dot(s_hi, w_hi, preferred_element_type=F32) + jnp.dot(s_lo, w_hi, preferred_element_type=F32)
                  + jnp.dot(s_hi, w_lo, preferred_element_type=F32) + b_ref[...])


def _adaln(cvec, w_ada, b_ada):
    n = w_ada.shape[1]
    rows = cvec.shape[0]
    return pl.pallas_call(
        _adaln_kernel,
        grid=(n // D,),
        in_specs=[pl.BlockSpec((rows, D), lambda j: (0, 0)),
                  pl.BlockSpec((D, D), lambda j: (0, j)),
                  pl.BlockSpec((1, D), lambda j: (0, j))],
        out_specs=pl.BlockSpec((rows, D), lambda j: (0, j)),
        out_shape=jax.ShapeDtypeStruct((rows, n), F32),
        compiler_params=_cparams(("arbitrary",)),
        name="adaln",
    )(cvec, w_ada, b_ada.reshape(1, n))


def _zoh(lam_re, lam_im, log_dt):
    dt = jnp.exp(log_dt)
    a_re = jnp.minimum(lam_re, -1e-4)
    a_im = lam_im
    mag = jnp.exp(a_re * dt)
    ab_re = mag * jnp.cos(a_im * dt)
    ab_im = mag * jnp.sin(a_im * dt)
    den = a_re * a_re + a_im * a_im
    nr = ab_re - 1.0
    f_re = (nr * a_re + ab_im * a_im) / den
    f_im = (ab_im * a_re - nr * a_im) / den
    return ab_re, ab_im, f_re, f_im


def _s5_params_kernel(lre3_ref, lim3_ref, ldt3_ref, lre2_ref, lim2_ref, ldt2_ref, bre_ref, bim_ref,
                      cre_ref, cim_ref, bb_ref, cc_ref, a_ref):
    exact = functools.partial(jnp.dot, preferred_element_type=F32, precision=HIGHEST)
    spread = jnp.where(lax.broadcasted_iota(jnp.int32, (S5_P, CH_P), 1) % S5_P
                       == lax.broadcasted_iota(jnp.int32, (S5_P, CH_P), 0), 1.0, 0.0)

    def block_diag(m):
        wide = jnp.dot(m.reshape(CH_U, S5_P).astype(BF16), spread.astype(BF16), preferred_element_type=F32)
        row_g = lax.broadcasted_iota(jnp.int32, (CH_U, CH_P), 0) // S5_H
        col_g = lax.broadcasted_iota(jnp.int32, (CH_U, CH_P), 1) // S5_P
        return jnp.where(row_g == col_g, wide, 0.0)

    _, _, f_re, f_im = _zoh(lre3_ref[0, 0], lim3_ref[0, 0], ldt3_ref[0, 0])
    b_re = bre_ref[0, 0]
    b_im = bim_ref[0, 0]
    bb_ref[0, 0] = jnp.concatenate([block_diag(f_re * b_re - f_im * b_im),
                                    block_diag(f_re * b_im + f_im * b_re)], axis=1).astype(BF16)
    cc_ref[0, 0] = jnp.concatenate([block_diag(cre_ref[0, 0]).T,
                                    -block_diag(cim_ref[0, 0]).T], axis=0).astype(BF16)

    ab_re, ab_im, _, _ = _zoh(lre2_ref[0, 0], lim2_ref[0, 0], ldt2_ref[0, 0])
    row_g = lax.broadcasted_iota(jnp.int32, (S5_CHUNK, CH_P), 0)
    col_g = lax.broadcasted_iota(jnp.int32, (S5_CHUNK, CH_P), 1) // S5_P

    def lane_row(a):
        flat = jnp.sum(jnp.where(row_g == col_g, exact(a, spread), 0.0), axis=0, keepdims=True)
        return jnp.broadcast_to(flat, (SUB, CH_P))

    a_ref[0, 0] = jnp.concatenate([lane_row(ab_re), lane_row(ab_im)], axis=1)


def _s5_params(lam_re, lam_im, log_dt, b_re, b_im, c_re, c_im):
    nc = N_S5_CHUNKS
    ldt = jnp.broadcast_to(log_dt[:, :, None], (2, N_S5_GROUPS, S5_P))
    g3 = lambda a: a.reshape(2, nc, S5_CHUNK, 1, S5_P)
    g2 = lambda a: a.reshape(2, nc, S5_CHUNK, S5_P)
    ghp = lambda a: a.reshape(2, nc, S5_CHUNK, S5_H, S5_P)
    spec3 = pl.BlockSpec((1, 1, S5_CHUNK, 1, S5_P), lambda d, c: (d, c, 0, 0, 0))
    spec2 = pl.BlockSpec((1, 1, S5_CHUNK, S5_P), lambda d, c: (d, c, 0, 0))
    spec_ghp = pl.BlockSpec((1, 1, S5_CHUNK, S5_H, S5_P), lambda d, c: (d, c, 0, 0, 0))
    return pl.pallas_call(
        _s5_params_kernel,
        grid=(2, nc),
        in_specs=[spec3, spec3, spec3, spec2, spec2, spec2, spec_ghp, spec_ghp, spec_ghp, spec_ghp],
        out_specs=[pl.BlockSpec((1, 1, CH_U, 2 * CH_P), lambda d, c: (d, c, 0, 0)),
                   pl.BlockSpec((1, 1, 2 * CH_P, CH_U), lambda d, c: (d, c, 0, 0)),
                   pl.BlockSpec((1, 1, SUB, 2 * CH_P), lambda d, c: (d, c, 0, 0))],
        out_shape=[jax.ShapeDtypeStruct((2, nc, CH_U, 2 * CH_P), BF16),
                   jax.ShapeDtypeStruct((2, nc, 2 * CH_P, CH_U), BF16),
                   jax.ShapeDtypeStruct((2, nc, SUB, 2 * CH_P), F32)],
        compiler_params=_cparams(("arbitrary", "arbitrary")),
        name="s5_params",
    )(g3(lam_re), g3(lam_im), g3(ldt), g2(lam_re), g2(lam_im), g2(ldt),
      ghp(jnp.swapaxes(b_re, -1, -2)), ghp(jnp.swapaxes(b_im, -1, -2)), ghp(c_re), ghp(c_im))


def _rms(x, g):
    return x * lax.rsqrt(jnp.mean(x * x, axis=-1, keepdims=True) + RMS_EPS) * g


def _load_x(g, xp_ref, xs_ref, pos_ref):
    is_lat = g == N_GRP - 1
    pos = pos_ref[...]
    pos8 = jnp.concatenate([pos, pos], axis=0)
    return jnp.where(is_lat, xs_ref[0] + pos8, xp_ref[0])


def _x_specs():
    n_ctx = N_GRP - 1
    xp_spec = pl.BlockSpec(
        (1, SUB, TC, D),
        lambda g, tc: (jnp.minimum(g, n_ctx - 1), 0, jnp.where(g >= n_ctx, N_TC - 1, tc), 0))
    xs_spec = pl.BlockSpec((1, SUB, TC, D), lambda g, tc: (0, 0, jnp.where(g >= n_ctx, tc, 0), 0))
    pos_spec = pl.BlockSpec((LAT_CHUNKS, TC, D), lambda g, tc: (0, tc, 0))
    return xp_spec, xs_spec, pos_spec


def _perm_matrix():
    p = np.zeros((SUB * PERM_T, SUB * PERM_T), np.float32)
    for s in range(SUB):
        for j in range(PERM_T):
            p[j * SUB + s, s * PERM_T + j] = 1.0
    return p


def _pre_kernel(xp_ref, xs_ref, pos_ref, mod_ref, g1_ref, win_ref, perm_ref, us_ref, uf_ref):
    g = pl.program_id(0)
    x = _load_x(g, xp_ref, xs_ref, pos_ref)
    mod = mod_ref[0]
    shift1 = mod[:, :, 0:D]
    scale1 = mod[:, :, D:2 * D]
    h = _rms(x, g1_ref[...]) * (1.0 + scale1) + shift1
    h2d = h.reshape(SUB * TC, D).astype(BF16)
    proj = jnp.dot(h2d, win_ref[...].astype(BF16), preferred_element_type=F32)
    uf_ref[0] = proj[:, D_S5:].astype(BF16).reshape(SUB, TC, D_FN)
    u = proj[:, :D_S5].astype(BF16)
    perm = perm_ref[...]
    for q in range(TC // PERM_T):
        piece = jnp.concatenate(
            [u[s * TC + q * PERM_T: s * TC + (q + 1) * PERM_T] for s in range(SUB)], axis=0)
        us_ref[0, q * SUB * PERM_T:(q + 1) * SUB * PERM_T, :] = jnp.dot(
            perm, piece, preferred_element_type=F32).astype(BF16)


def _pre(xp4, xs4, pos3, modg, norm1_g, w_in, perm):
    xp_spec, xs_spec, pos_spec = _x_specs()
    return pl.pallas_call(
        _pre_kernel,
        grid=(N_GRP, N_TC),
        in_specs=[xp_spec, xs_spec, pos_spec,
                  pl.BlockSpec((1, SUB, 1, 6 * D), lambda g, tc: (g, 0, 0, 0)),
                  pl.BlockSpec((1, D), lambda g, tc: (0, 0)),
                  pl.BlockSpec((D, D), lambda g, tc: (0, 0)),
                  pl.BlockSpec((SUB * PERM_T, SUB * PERM_T), lambda g, tc: (0, 0))],
        out_specs=[pl.BlockSpec((1, SUB * TC, D_S5), lambda g, tc: (g, tc, 0)),
                   pl.BlockSpec((1, SUB, TC, D_FN), lambda g, tc: (g, 0, tc, 0))],
        out_shape=[jax.ShapeDtypeStruct((N_GRP, GRP_ROWS, D_S5), BF16),
                   jax.ShapeDtypeStruct((N_GRP, SUB, L_BLK, D_FN), BF16)],
        compiler_params=_cparams(("arbitrary", "arbitrary")),
        name="pre_mixer",
    )(xp4, xs4, pos3, modg, norm1_g.reshape(1, D), w_in, perm)


def _cmul(ar, ai, br, bi):
    return ar * br - ai * bi, ar * bi + ai * br


def _s5_kernel_unpipelined(us_ref, bb_ref, cc_ref, a_ref, init_ref, dskip_ref, y_ref, fin_ref, s_ref):
    g = pl.program_id(0)
    d = pl.program_id(2)
    n_rb = GRP_ROWS // ROW_BLK

    def proj_in(rb, _):
        rows = pl.ds(pl.multiple_of(rb * ROW_BLK, ROW_BLK), ROW_BLK)
        s_ref[rows, :] = jnp.dot(us_ref[0, rows, :], bb_ref[0, 0], preferred_element_type=F32)
        return 0

    lax.fori_loop(0, n_rb, proj_in, 0)

    def tile_row(k):
        t = k + d * (L_BLK - 1 - 2 * k)
        return pl.ds(pl.multiple_of(t * SUB, SUB), SUB)

    is_lat = g == N_GRP - 1
    re_all = slice(0, CH_P)
    im_all = slice(CH_P, 2 * CH_P)
    a_re = a_ref[0, 0, :, re_all]
    a_im = a_ref[0, 0, :, im_all]
    zero = jnp.zeros((SUB, CH_P), F32)
    s0_re = jnp.where(is_lat, init_ref[0, 0, :, re_all], zero)
    s0_im = jnp.where(is_lat, init_ref[0, 0, :, im_all], zero)

    def scan_body(i, carry):
        s_re, s_im = carry
        for j in range(8):
            rows = tile_row(i * 8 + j)
            n_re = a_re * s_re - a_im * s_im + s_ref[rows, re_all]
            n_im = a_re * s_im + a_im * s_re + s_ref[rows, im_all]
            s_ref[rows, re_all] = n_re
            s_ref[rows, im_all] = n_im
            s_re, s_im = n_re, n_im
        return s_re, s_im

    f_re, f_im = lax.fori_loop(0, L_BLK // 8, scan_body, (s0_re, s0_im))
    fin_ref[0, 0, :, re_all] = f_re
    fin_ref[0, 0, :, im_all] = f_im

    @pl.when(is_lat)
    def _():
        part = CH_P // FIX_PARTS
        sub_id = lax.broadcasted_iota(jnp.int32, (SUB, part), 0) % LAT_CHUNKS
        fwd = d == 0
        lo = jnp.where(fwd, 1, 0)
        hi = jnp.where(fwd, LAT_CHUNKS - 1, LAT_CHUNKS - 2)
        keep = (sub_id >= lo) & (sub_id <= hi)

        def from_prev(v):
            return jnp.where(keep, jnp.where(fwd, pltpu.roll(v, 1, 0), pltpu.roll(v, SUB - 1, 0)), 0.0)

        for hh in range(FIX_PARTS):
            lanes = slice(hh * part, (hh + 1) * part)
            re_cols = lanes
            im_cols = slice(CH_P + hh * part, CH_P + (hh + 1) * part)
            pa_re, pa_im = a_re[:, lanes], a_im[:, lanes]
            pf_re, pf_im = f_re[:, lanes], f_im[:, lanes]
            p_re, p_im = pa_re, pa_im
            for _ in range(8):
                p_re, p_im = _cmul(p_re, p_im, p_re, p_im)
            t_re, t_im = pf_re, pf_im
            for _ in range(LAT_CHUNKS - 2):
                m_re, m_im = _cmul(p_re, p_im, from_prev(t_re), from_prev(t_im))
                t_re, t_im = pf_re + m_re, pf_im + m_im
            c_re, c_im = from_prev(t_re), from_prev(t_im)

            def fix_body(i, carry, pa_re=pa_re, pa_im=pa_im, re_cols=re_cols, im_cols=im_cols):
                m_re, m_im = carry
                for j in range(8):
                    rows = tile_row(i * 8 + j)
                    m_re, m_im = _cmul(m_re, m_im, pa_re, pa_im)
                    s_ref[rows, re_cols] = s_ref[rows, re_cols] + m_re
                    s_ref[rows, im_cols] = s_ref[rows, im_cols] + m_im
                return m_re, m_im

            lax.fori_loop(0, L_BLK // 8, fix_body, (c_re, c_im))

    def proj_out(rb):
        rows = pl.ds(pl.multiple_of(rb * ROW_BLK, ROW_BLK), ROW_BLK)
        return rows, jnp.dot(s_ref[rows, :].astype(BF16), cc_ref[0, 0], preferred_element_type=F32)

    @pl.when(d == 0)
    def _():
        def body(rb, _):
            rows, yb = proj_out(rb)
            y_ref[0, rows, :] = dskip_ref[...] * us_ref[0, rows, :].astype(F32) + yb
            return 0
        lax.fori_loop(0, n_rb, body, 0)

    @pl.when(d == 1)
    def _():
        def body(rb, _):
            rows, yb = proj_out(rb)
            y_ref[0, rows, :] = y_ref[0, rows, :] + yb
            return 0
        lax.fori_loop(0, n_rb, body, 0)


S5_BLK = 256
S5_NBLK = GRP_ROWS // S5_BLK
S5_STEPS = S5_BLK // SUB
S5_FIX_UNROLL = 8


def _s5_kernel(us_ref, bb_ref, cc_ref, a_ref, init_ref, dskip_ref, y_ref, fin_ref, in_a, in_b, out_a, out_b):
    g = pl.program_id(0)
    d = pl.program_id(2)
    is_lat = g == N_GRP - 1
    re_all = slice(0, CH_P)
    im_all = slice(CH_P, 2 * CH_P)
    a_re = a_ref[0, 0, :, re_all]
    a_im = a_ref[0, 0, :, im_all]
    bufs_in = (in_a, in_b)
    bufs_out = (out_a, out_b)

    def time_rows(k):
        blk = k + d * (S5_NBLK - 1 - 2 * k)
        return pl.ds(pl.multiple_of(blk * S5_BLK, S5_BLK), S5_BLK)

    def slot_rows(k):
        return slice((k // 2) * S5_BLK, (k // 2 + 1) * S5_BLK)

    def step_rows(k, j):
        q = j + d * (S5_STEPS - 1 - 2 * j)
        return pl.ds(pl.multiple_of((k // 2) * S5_BLK + q * SUB, SUB), SUB)

    def proj_in(k):
        bufs_in[k % 2][slot_rows(k), :] = jnp.dot(us_ref[0, time_rows(k), :], bb_ref[0, 0],
                                                  preferred_element_type=F32)

    def scan_block(k, carry):
        src, dst = bufs_in[k % 2], bufs_out[k % 2]
        s_re, s_im = carry
        for j in range(S5_STEPS):
            rows = step_rows(k, j)
            n_re = a_re * s_re - a_im * s_im + src[rows, re_all]
            n_im = a_re * s_im + a_im * s_re + src[rows, im_all]
            dst[rows, re_all] = n_re
            dst[rows, im_all] = n_im
            s_re, s_im = n_re, n_im
        return s_re, s_im

    def proj_out(k):
        rows = time_rows(k)
        yb = jnp.dot(bufs_out[k % 2][slot_rows(k), :].astype(BF16), cc_ref[0, 0], preferred_element_type=F32)
        y_ref[0, rows, :] = y_ref[0, rows, :] + yb

    @pl.when(d == 0)
    def _():
        y_ref[0] = dskip_ref[...] * us_ref[0].astype(F32)

    def scan_all(carry, with_out):
        proj_in(0)
        for k in range(S5_NBLK):
            if k + 1 < S5_NBLK:
                proj_in(k + 1)
            carry = scan_block(k, carry)
            if with_out and k >= 1:
                proj_out(k - 1)
        if with_out:
            proj_out(S5_NBLK - 1)
        fin_ref[0, 0, :, re_all] = carry[0]
        fin_ref[0, 0, :, im_all] = carry[1]
        return carry

    @pl.when(jnp.logical_not(is_lat))
    def _():
        zero = jnp.zeros((SUB, CH_P), F32)
        scan_all((zero, zero), True)

    @pl.when(is_lat)
    def _():
        f_re, f_im = scan_all((init_ref[0, 0, :, re_all], init_ref[0, 0, :, im_all]), False)

        sub_id = lax.broadcasted_iota(jnp.int32, (SUB, CH_P), 0) % LAT_CHUNKS
        fwd = d == 0
        lo = jnp.where(fwd, 1, 0)
        hi = jnp.where(fwd, LAT_CHUNKS - 1, LAT_CHUNKS - 2)
        keep = (sub_id >= lo) & (sub_id <= hi)

        def from_prev(v):
            return jnp.where(keep, jnp.where(fwd, pltpu.roll(v, 1, 0), pltpu.roll(v, SUB - 1, 0)), 0.0)

        p_re, p_im = a_re, a_im
        for _ in range(8):
            p_re, p_im = _cmul(p_re, p_im, p_re, p_im)
        t_re, t_im = f_re, f_im
        for _ in range(LAT_CHUNKS - 2):
            m_re, m_im = _cmul(p_re, p_im, from_prev(t_re), from_prev(t_im))
            t_re, t_im = f_re + m_re, f_im + m_im
        corr = (from_prev(t_re), from_prev(t_im))

        for k in range(S5_NBLK):
            dst = bufs_out[k % 2]

            def fix_body(i, carry, k=k, dst=dst):
                m_re, m_im = carry
                for j in range(S5_FIX_UNROLL):
                    q = i * S5_FIX_UNROLL + j
                    q = q + d * (S5_STEPS - 1 - 2 * q)
                    rows = pl.ds(pl.multiple_of((k // 2) * S5_BLK + q * SUB, SUB), SUB)
                    m_re, m_im = _cmul(m_re, m_im, a_re, a_im)
                    dst[rows, re_all] = dst[rows, re_all] + m_re
                    dst[rows, im_all] = dst[rows, im_all] + m_im
                return m_re, m_im

            corr = lax.fori_loop(0, S5_STEPS // S5_FIX_UNROLL, fix_body, corr)

        for k in range(S5_NBLK):
            proj_out(k)


def _s5(us, bbmat, ccmat, a8, init8, d_skip):
    return pl.pallas_call(
        _s5_kernel,
        grid=(N_GRP, N_S5_CHUNKS, 2),
        in_specs=[pl.BlockSpec((1, GRP_ROWS, CH_U), lambda g, c, d: (g, 0, c)),
                  pl.BlockSpec((1, 1, CH_U, 2 * CH_P), lambda g, c, d: (d, c, 0, 0)),
                  pl.BlockSpec((1, 1, 2 * CH_P, CH_U), lambda g, c, d: (d, c, 0, 0)),
                  pl.BlockSpec((1, 1, SUB, 2 * CH_P), lambda g, c, d: (d, c, 0, 0)),
                  pl.BlockSpec((1, 1, SUB, 2 * CH_P), lambda g, c, d: (d, c, 0, 0)),
                  pl.BlockSpec((1, CH_U), lambda g, c, d: (0, c))],
        out_specs=[pl.BlockSpec((1, GRP_ROWS, CH_U), lambda g, c, d: (g, 0, c)),
                   pl.BlockSpec((1, 1, SUB, 2 * CH_P), lambda g, c, d: (d, g, 0, c))],
        out_shape=[jax.ShapeDtypeStruct((N_GRP, GRP_ROWS, D_S5), F32),
                   jax.ShapeDtypeStruct((2, N_GRP, SUB, N_S5_CHUNKS * 2 * CH_P), F32)],
        scratch_shapes=[pltpu.VMEM((GRP_ROWS // 2, 2 * CH_P), F32)] * 4,
        compiler_params=_cparams(("arbitrary", "arbitrary", "arbitrary")),
        name="s5_scan",
    )(us, bbmat, ccmat, a8, init8, d_skip.reshape(1, D_S5))


def _fnet_w_kernel(c_ref, s_ref, w_ref, m1_ref, m2_ref):
    w = w_ref[...]
    m1_ref[...] = jnp.dot(c_ref[...], w, preferred_element_type=F32, precision=HIGHEST).astype(BF16)
    m2_ref[...] = jnp.dot(s_ref[...], w, preferred_element_type=F32, precision=HIGHEST).astype(BF16)


FN_SEQ = LAT_CHUNKS
FN_CTX_STEPS = (N_GRP - 1) * SUB // FN_SEQ


def _fnet_kernel(u_ref, cos_s_ref, sin_s_ref, cos_l_ref, sin_l_ref, m1_ref, m2_ref, o_ref):
    i = pl.program_id(0)
    m1 = m1_ref[...]
    m2 = m2_ref[...]

    def mix(u, cos_ref, sin_ref):
        v1 = jnp.dot(u, m1, preferred_element_type=F32).astype(BF16)
        v2 = jnp.dot(u, m2, preferred_element_type=F32).astype(BF16)
        return (jnp.dot(cos_ref[...].astype(BF16), v1, preferred_element_type=F32)
                - jnp.dot(sin_ref[...].astype(BF16), v2, preferred_element_type=F32)).astype(BF16)

    @pl.when(i < FN_CTX_STEPS)
    def _():
        for s in range(FN_SEQ):
            o_ref[0, s] = mix(u_ref[0, s], cos_s_ref, sin_s_ref)

    @pl.when(i >= FN_CTX_STEPS)
    def _():
        u = u_ref[0].reshape(FN_SEQ * L_BLK, D_FN)
        o_ref[0] = mix(u, cos_l_ref, sin_l_ref).reshape(FN_SEQ, L_BLK, D_FN)


def _dft_tables(n):
    k = np.arange(n, dtype=np.int64)
    ang = (2.0 * np.pi / n) * ((k[:, None] * k[None, :]) % n).astype(np.float64)
    scale = 1.0 / math.sqrt(n)
    return (np.cos(ang) * scale).astype(np.float32), (np.sin(ang) * scale).astype(np.float32)


def _fnet(uf, m1, m2):
    n_lat_steps = SUB // FN_SEQ
    cos_s, sin_s = _dft_tables(L_BLK)
    cos_l, sin_l = _dft_tables(FN_SEQ * L_BLK)
    full = lambda i: (0, 0)
    per_grp = SUB // FN_SEQ
    blk = pl.BlockSpec(
        (1, FN_SEQ, L_BLK, D_FN),
        lambda i: (jnp.minimum(i // per_grp, N_GRP - 1),
                   jnp.where(i < FN_CTX_STEPS, i % per_grp, i - FN_CTX_STEPS), 0, 0))
    table = lambda n: pl.BlockSpec((n, n), full)
    return pl.pallas_call(
        _fnet_kernel,
        grid=(FN_CTX_STEPS + n_lat_steps,),
        in_specs=[blk, table(L_BLK), table(L_BLK), table(FN_SEQ * L_BLK), table(FN_SEQ * L_BLK),
                  table(D_FN), table(D_FN)],
        out_specs=blk,
        out_shape=jax.ShapeDtypeStruct(uf.shape, BF16),
        compiler_params=_cparams(("arbitrary",)),
        name="fnet",
    )(uf, jnp.asarray(cos_s), jnp.asarray(sin_s), jnp.asarray(cos_l), jnp.asarray(sin_l), m1, m2)


def _fnet_weights(w_fnet):
    n_g = D_FN // FN_GW
    cos_c, sin_c = _dft_tables(FN_GW)
    eye = np.eye(n_g, dtype=np.float32)
    cos_bd = np.kron(eye, cos_c)
    sin_bd = np.kron(eye, sin_c)
    w_bd = (w_fnet[:, :, None, :] * jnp.asarray(eye)[:, None, :, None]).reshape(D_FN, D_FN)
    out = jax.ShapeDtypeStruct((D_FN, D_FN), BF16)
    return pl.pallas_call(_fnet_w_kernel, out_shape=(out, out), name="fnet_weights")(
        jnp.asarray(cos_bd), jnp.asarray(sin_bd), w_bd)


def _gelu_tanh(x):
    return 0.5 * x * (1.0 + jnp.tanh(math.sqrt(2.0 / math.pi) * (x + 0.044715 * (x * x * x))))


def _post_kernel(y_ref, yf_ref, xp_ref, xs_ref, pos_ref, mod_ref, wglu_ref, wout_ref, g2_ref, wr_ref,
                 br_ref, permt_ref, x1_ref, h2_ref, comb_ref):
    g = pl.program_id(0)
    z = _gelu_tanh(y_ref[0])
    gate = jnp.dot(z.astype(BF16), wglu_ref[...].astype(BF16), preferred_element_type=F32)
    gl = (z * jax.nn.sigmoid(gate)).astype(BF16)
    permt = permt_ref[...]
    n_q = TC // PERM_T
    nat = [jnp.dot(permt, gl[q * SUB * PERM_T:(q + 1) * SUB * PERM_T], preferred_element_type=F32).astype(BF16)
           for q in range(n_q)]
    gl_nat = jnp.concatenate(
        [nat[q][s * PERM_T:(s + 1) * PERM_T] for s in range(SUB) for q in range(n_q)], axis=0)
    w_out = wout_ref[...].astype(BF16)
    mixed = (jnp.dot(gl_nat, w_out[:D_S5], preferred_element_type=F32)
             + jnp.dot(yf_ref[0].reshape(SUB * TC, D_FN), w_out[D_S5:], preferred_element_type=F32))
    x = _load_x(g, xp_ref, xs_ref, pos_ref)
    mod = mod_ref[0]
    gate1 = mod[:, :, 2 * D:3 * D]
    shift2 = mod[:, :, 3 * D:4 * D]
    scale2 = mod[:, :, 4 * D:5 * D]
    x1 = x + gate1 * mixed.reshape(SUB, TC, D)
    x1_ref[0] = x1
    h2 = _rms(x1, g2_ref[...]) * (1.0 + scale2) + shift2
    h2_ref[0] = h2.astype(BF16)

    hr = h2.reshape(SUB * TC, D)
    h_hi = hr.astype(BF16)
    h_lo = (hr - h_hi.astype(F32)).astype(BF16)
    wr = wr_ref[...]
    w_hi = wr.astype(BF16)
    w_lo = (wr - w_hi.astype(F32)).astype(BF16)
    logits = (jnp.dot(h_hi, w_hi, preferred_element_type=F32) + jnp.dot(h_lo, w_hi, preferred_element_type=F32)
              + jnp.dot(h_hi, w_lo, preferred_element_type=F32) + br_ref[...])
    lane = lax.broadcasted_iota(jnp.int32, logits.shape, 1).astype(F32)
    top_v, hots = [], []
    cur = logits
    for _ in range(TOP_K):
        m = jnp.max(cur, axis=-1, keepdims=True)
        idx = jnp.min(jnp.where(cur == m, lane, float(E_PAD)), axis=-1, keepdims=True)
        hot = lane == idx
        top_v.append(m)
        hots.append(hot)
        cur = jnp.where(hot, -3.0e38, cur)
    exps = [jnp.exp(v - top_v[0]) for v in top_v]
    denom = exps[0] + exps[1] + exps[2] + exps[3]
    comb = jnp.zeros(logits.shape, F32)
    for k in range(TOP_K):
        comb = comb + jnp.where(hots[k], exps[k] / denom, 0.0)
    comb_ref[0] = comb.reshape(SUB, TC, E_PAD)


def _post(y, yf, xp4, xs4, pos3, modg, w_glu, w_out, norm2_g, w_router, b_router, permt):
    xp_spec, xs_spec, pos_spec = _x_specs()
    const2 = lambda g, tc: (0, 0)
    wr = jnp.zeros((D, E_PAD), F32).at[:, :N_EXPERTS].set(w_router)
    br = jnp.full((1, E_PAD), -1.0e30, F32).at[0, :N_EXPERTS].set(b_router)
    blk = lambda w: pl.BlockSpec((1, SUB, TC, w), lambda g, tc: (g, 0, tc, 0))
    return pl.pallas_call(
        _post_kernel,
        grid=(N_GRP, N_TC),
        in_specs=[pl.BlockSpec((1, SUB * TC, D_S5), lambda g, tc: (g, tc, 0)),
                  blk(D_FN), xp_spec, xs_spec, pos_spec,
                  pl.BlockSpec((1, SUB, 1, 6 * D), lambda g, tc: (g, 0, 0, 0)),
                  pl.BlockSpec((D_S5, D_S5), const2),
                  pl.BlockSpec((D, D), const2),
                  pl.BlockSpec((1, D), const2),
                  pl.BlockSpec((D, E_PAD), const2),
                  pl.BlockSpec((1, E_PAD), const2),
                  pl.BlockSpec((SUB * PERM_T, SUB * PERM_T), const2)],
        out_specs=[blk(D), blk(D), blk(E_PAD)],
        out_shape=[jax.ShapeDtypeStruct((N_GRP, SUB, L_BLK, D), F32),
                   jax.ShapeDtypeStruct((N_GRP, SUB, L_BLK, D), BF16),
                   jax.ShapeDtypeStruct((N_GRP, SUB, L_BLK, E_PAD), F32)],
        compiler_params=_cparams(("arbitrary", "arbitrary")),
        name="post_mixer",
    )(y, yf, xp4, xs4, pos3, modg, w_glu, w_out, norm2_g.reshape(1, D), wr, br, permt)


TBLK = L_BLK
N_BLK = T_TOK // TBLK
N_CTX_BLK = (N_GRP - 1) * SUB
SEG_ALIGN = 8
RB = 1280
assert RB >= TBLK * TOP_K + N_EXPERTS * (SEG_ALIGN - 1) and RB % 128 == 0
TM = 256
TM_SHIFT = 8
assert 1 << TM_SHIFT == TM
R_TOT = T_TOK * TOP_K + N_BLK * N_EXPERTS * (SEG_ALIGN - 1) + TM
TAB_ROWS = 32
assert TAB_ROWS >= N_BLK
BIG_ROWS = 32
BIG_SHIFT = 2
assert SEG_ALIGN << BIG_SHIFT == BIG_ROWS


def _plan_kernel(comb_ref, z_ref, ptab_ref, loff_ref, gst_ref, eoff_ref):
    row = lax.broadcasted_iota(jnp.int32, (TBLK, TBLK), 0)
    col = lax.broadcasted_iota(jnp.int32, (TBLK, TBLK), 1)
    earlier = jnp.where(row > col, 1.0, 0.0).astype(BF16)
    ptab_ref[...] = jnp.zeros_like(ptab_ref)

    def body(b, _):
        rows = pl.ds(pl.multiple_of(b * TBLK, TBLK), TBLK)
        m = jnp.where(comb_ref[rows, :] > 0.0, 1.0, 0.0)
        rank = jnp.dot(earlier, m.astype(BF16), preferred_element_type=F32)
        z_ref[rows, :] = m * (rank + 1.0)
        n = jnp.sum(m, axis=0, keepdims=True)
        ptab_ref[pl.ds(b, 1), :] = jnp.floor((n + (SEG_ALIGN - 1)) * (1.0 / SEG_ALIGN)) * SEG_ALIGN
        return 0

    lax.fori_loop(0, N_BLK, body, 0)

    ptab = ptab_ref[...]
    er = lax.broadcasted_iota(jnp.int32, (E_PAD, E_PAD), 0)
    ec = lax.broadcasted_iota(jnp.int32, (E_PAD, E_PAD), 1)
    before = jnp.where(er < ec, 1.0, 0.0)
    exact = functools.partial(jnp.dot, preferred_element_type=F32, precision=HIGHEST)
    loff_ref[...] = exact(ptab, before)
    tot = jnp.sum(ptab, axis=0, keepdims=True)
    eoff = exact(jnp.broadcast_to(tot, (SUB, E_PAD)), before)
    eoff_ref[...] = eoff
    br = lax.broadcasted_iota(jnp.int32, (TAB_ROWS, TAB_ROWS), 0)
    bc = lax.broadcasted_iota(jnp.int32, (TAB_ROWS, TAB_ROWS), 1)
    gst_ref[...] = eoff[0:1] + exact(jnp.where(br > bc, 1.0, 0.0), ptab)


def _plan(comb):
    tab = jax.ShapeDtypeStruct((TAB_ROWS, E_PAD), F32)
    return pl.pallas_call(
        _plan_kernel,
        out_shape=(jax.ShapeDtypeStruct((T_TOK, E_PAD), F32), tab, tab, tab,
                   jax.ShapeDtypeStruct((SUB, E_PAD), F32)),
        compiler_params=pltpu.CompilerParams(vmem_limit_bytes=VMEM_LIMIT),
        name="moe_plan",
    )(comb)


def _sort_matrix(b, z_ref, loff_ref, ptab_ref):
    loff = loff_ref[pl.ds(b, 1), :]
    size = ptab_ref[pl.ds(b, 1), :]
    r = lax.broadcasted_iota(jnp.int32, (RB, E_PAD), 0).astype(F32)
    owner = jnp.where(r >= loff, jnp.where(r < loff + size, 1.0, 0.0), 0.0)
    rank1 = r[:, 0:1] - jnp.sum(owner * loff, axis=-1, keepdims=True) + 1.0
    zt = z_ref[...].T
    v = jnp.dot(owner.astype(BF16), zt.astype(BF16), preferred_element_type=F32)
    return jnp.where(v == rank1, 1.0, 0.0), owner


def _segment_copies(b, loff_s, gst_s, p8_s, make_copy):
    def per_expert(e, counts):
        k = b * N_EXPERTS + e
        chunks = p8_s[k]
        n_big = lax.shift_right_logical(chunks, BIG_SHIFT)
        n_small = chunks - n_big * (BIG_ROWS // SEG_ALIGN)
        local0 = loff_s[k]
        global0 = gst_s[k]

        def big(j, _):
            make_copy(pl.multiple_of(local0 + j * BIG_ROWS, SEG_ALIGN),
                      pl.multiple_of(global0 + j * BIG_ROWS, SEG_ALIGN), BIG_ROWS).start()
            return 0

        def small(j, _):
            off = n_big * BIG_ROWS + j * SEG_ALIGN
            make_copy(pl.multiple_of(local0 + off, SEG_ALIGN),
                      pl.multiple_of(global0 + off, SEG_ALIGN), SEG_ALIGN).start()
            return 0

        lax.fori_loop(0, n_big, big, 0)
        lax.fori_loop(0, n_small, small, 0)
        return counts[0] + n_big, counts[1] + n_small

    return lax.fori_loop(0, N_EXPERTS, per_expert, (0, 0))


def _wait_copies(counts, make_copy):
    for n, rows in zip(counts, (BIG_ROWS, SEG_ALIGN)):
        def wait_one(i, _, rows=rows):
            make_copy(0, 0, rows).wait()
            return 0
        lax.fori_loop(0, n, wait_one, 0)


def _zero_rows_from(hbm, zeros_vmem, first_row, sem):
    def copy_to(row):
        return pltpu.make_async_copy(zeros_vmem, hbm.at[pl.ds(row, TM)], sem)

    n_full = lax.shift_right_logical(R_TOT - first_row, TM_SHIFT)

    def start_one(j, _):
        copy_to(pl.multiple_of(first_row + j * TM, SEG_ALIGN)).start()
        return 0

    def wait_one(j, _):
        copy_to(0).wait()
        return 0

    lax.fori_loop(0, n_full, start_one, 0)
    lax.fori_loop(0, n_full, wait_one, 0)
    last = copy_to(R_TOT - TM)
    last.start()
    last.wait()


def _dispatch_kernel(loff_s, gst_s, p8_s, tot_s, h_ref, z_ref, loff_ref, ptab_ref, xs_hbm, xbuf, sem, pending):
    b = pl.program_id(0)

    def make_copy(local_row, global_row, rows):
        return pltpu.make_async_copy(xbuf.at[pl.ds(local_row, rows)], xs_hbm.at[pl.ds(global_row, rows)], sem)

    pm = _sort_matrix(b, z_ref, loff_ref, ptab_ref)[0].astype(BF16)

    @pl.when(b > 0)
    def _():
        _wait_copies((pending[0], pending[1]), make_copy)

    xbuf[...] = jnp.dot(pm, h_ref[...], preferred_element_type=F32)
    n_big, n_small = _segment_copies(b, loff_s, gst_s, p8_s, make_copy)
    pending[0] = n_big
    pending[1] = n_small

    @pl.when(b == N_BLK - 1)
    def _():
        _wait_copies((n_big, n_small), make_copy)
        xbuf[0:TM, :] = jnp.zeros((TM, D), F32)
        _zero_rows_from(xs_hbm, xbuf.at[0:TM], tot_s[0], sem)


def _dispatch(h2, z, loff, ptab, loff_i, gst_i, p8_i, tot_i):
    whole = pl.BlockSpec((TAB_ROWS, E_PAD), lambda b, *_: (0, 0))
    return pl.pallas_call(
        _dispatch_kernel,
        grid_spec=pltpu.PrefetchScalarGridSpec(
            num_scalar_prefetch=4,
            grid=(N_BLK,),
            in_specs=[pl.BlockSpec((TBLK, D), lambda b, *_: (b, 0)),
                      pl.BlockSpec((TBLK, E_PAD), lambda b, *_: (b, 0)),
                      whole, whole],
            out_specs=pl.BlockSpec(memory_space=pl.ANY),
            scratch_shapes=[pltpu.VMEM((RB, D), F32), pltpu.SemaphoreType.DMA(()),
                            pltpu.SMEM((2,), jnp.int32)]),
        out_shape=jax.ShapeDtypeStruct((R_TOT, D), F32),
        compiler_params=_cparams(("arbitrary",)),
        name="moe_dispatch",
    )(loff_i, gst_i, p8_i, tot_i, h2, z, loff, ptab)


MAX_TILES = R_TOT // TM + N_EXPERTS
X_AHEAD = 3
X_SLOTS = X_AHEAD + 1
Y_SLOTS = 2
W_AHEAD = 2
W_SLOTS = W_AHEAD + 1
N_MATS = 3
FF_CHUNK = 256
CAST_ROWS = 64
TILE_STEP = 64


def _expert_kernel(eoff_s, xs_hbm, wg_hbm, wu_hbm, wd_hbm, bg_ref, bu_ref, bd_ref, ys_hbm,
                   wst, wbf, xin, yout, act_ref, w_sem, x_sem, y_sem, t_exp, t_row, t_first, t_valid, live):
    def add_expert(e, carry):
        n_t, n_live = carry
        start = eoff_s[e]
        count = eoff_s[e + 1] - start
        tiles = lax.shift_right_logical(count + (TM - 1), TM_SHIFT)

        def add_tile(i, _):
            t_exp[n_t + i] = e
            t_row[n_t + i] = start + i * TM
            t_first[n_t + i] = jnp.where(i == 0, 1, 0)
            t_valid[n_t + i] = jnp.minimum(count - i * TM, TM)
            return 0

        lax.fori_loop(0, tiles, add_tile, 0)
        has_rows = jnp.where(tiles > 0, 1, 0)

        @pl.when(tiles > 0)
        def _():
            live[n_live] = e

        return n_t + tiles, n_live + has_rows

    n_tiles, n_live = lax.fori_loop(0, N_EXPERTS, add_expert, (0, 0))

    def w_copies(e, slot):
        return [pltpu.make_async_copy(w.at[e], wst.at[slot, m], w_sem.at[slot])
                for m, w in enumerate((wg_hbm, wu_hbm, wd_hbm))]

    def tile_rows(j):
        return pl.ds(pl.multiple_of(t_row[j], SEG_ALIGN), TM)

    def x_copy(j, slot):
        return pltpu.make_async_copy(xs_hbm.at[tile_rows(j)], xin.at[slot], x_sem.at[slot])

    def y_pieces(j, op):
        ys = lax.rem(j, Y_SLOTS)
        valid = t_valid[j]
        row0 = t_row[j]

        def piece(off, rows):
            return pltpu.make_async_copy(yout.at[ys, pl.ds(off, rows)],
                                         ys_hbm.at[pl.ds(pl.multiple_of(row0 + off, SEG_ALIGN), rows)],
                                         y_sem.at[ys])

        @pl.when(valid == TM)
        def _():
            op(piece(0, TM))

        @pl.when(valid < TM)
        def _():
            n_big = lax.shift_right_logical(valid, BIG_ROWS.bit_length() - 1)
            n_small = lax.shift_right_logical(valid - n_big * BIG_ROWS, SEG_ALIGN.bit_length() - 1)

            def big(i, _):
                op(piece(pl.multiple_of(i * BIG_ROWS, SEG_ALIGN), BIG_ROWS))
                return 0

            def small(i, _):
                op(piece(pl.multiple_of(n_big * BIG_ROWS + i * SEG_ALIGN, SEG_ALIGN), SEG_ALIGN))
                return 0

            lax.fori_loop(0, n_big, big, 0)
            lax.fori_loop(0, n_small, small, 0)

    for ahead in range(W_AHEAD):
        @pl.when(n_live > ahead)
        def _(ahead=ahead):
            for cp in w_copies(live[ahead], ahead):
                cp.start()

    for ahead in range(X_AHEAD):
        @pl.when(n_tiles > ahead)
        def _(ahead=ahead):
            x_copy(ahead, ahead).start()

    def body(j, k):
        e = t_exp[j]

        @pl.when(t_first[j] == 1)
        def _():
            ws = lax.rem(k, W_SLOTS)

            @pl.when(k + W_AHEAD < n_live)
            def _():
                for cp in w_copies(live[k + W_AHEAD], lax.rem(k + W_AHEAD, W_SLOTS)):
                    cp.start()

            for cp in w_copies(e, ws):
                cp.wait()

            def cast_rows(r, _):
                rows = pl.ds(pl.multiple_of(r * CAST_ROWS, CAST_ROWS), CAST_ROWS)
                for m in range(N_MATS):
                    wbf[m, rows, :] = wst[ws, m, rows, :].astype(BF16)
                return 0

            lax.fori_loop(0, D // CAST_ROWS, cast_rows, 0)

        slot = lax.rem(j, X_SLOTS)
        x_copy(j, slot).wait()

        @pl.when(j + X_AHEAD < n_tiles)
        def _():
            x_copy(j + X_AHEAD, lax.rem(j + X_AHEAD, X_SLOTS)).start()

        @pl.when(j >= Y_SLOTS)
        def _():
            y_pieces(j - Y_SLOTS, lambda cp: cp.wait())

        ys = lax.rem(j, Y_SLOTS)

        def compute(rows):
            x = xin[slot, 0:rows, :].astype(BF16)
            for c in range(D // FF_CHUNK):
                cols = slice(c * FF_CHUNK, (c + 1) * FF_CHUNK)
                gate = jnp.dot(x, wbf[0, :, cols], preferred_element_type=F32) + bg_ref[e][:, cols]
                up = jnp.dot(x, wbf[1, :, cols], preferred_element_type=F32) + bu_ref[e][:, cols]
                gate = jnp.minimum(gate, SWIGLU_LIMIT)
                up = jnp.clip(up, -SWIGLU_LIMIT, SWIGLU_LIMIT)
                act_ref[0:rows, cols] = ((up + 1.0) * gate * jax.nn.sigmoid(SWIGLU_ALPHA * gate)).astype(BF16)
            yout[ys, 0:rows, :] = jnp.dot(act_ref[0:rows, :], wbf[2], preferred_element_type=F32) + bd_ref[e]

        valid = t_valid[j]
        for rows in range(TILE_STEP, TM + 1, TILE_STEP):
            @pl.when(jnp.logical_and(valid > rows - TILE_STEP, valid <= rows))
            def _(rows=rows):
                compute(rows)

        y_pieces(j, lambda cp: cp.start())
        return k + t_first[j]

    lax.fori_loop(0, n_tiles, body, 0)

    for back in range(Y_SLOTS, 0, -1):
        @pl.when(n_tiles >= back)
        def _(back=back):
            y_pieces(n_tiles - back, lambda cp: cp.wait())

    yout[0] = jnp.zeros((TM, D), F32)
    _zero_rows_from(ys_hbm, yout.at[0], eoff_s[N_EXPERTS], y_sem.at[0])


def _experts(xs, eoff_i, w_gate, b_gate, w_up, b_up, w_down, b_down):
    hbm = pl.BlockSpec(memory_space=pl.ANY)
    bspec = pl.BlockSpec((N_EXPERTS, 1, D), lambda i, *_: (0, 0, 0))
    return pl.pallas_call(
        _expert_kernel,
        grid_spec=pltpu.PrefetchScalarGridSpec(
            num_scalar_prefetch=1,
            grid=(1,),
            in_specs=[hbm, hbm, hbm, hbm, bspec, bspec, bspec],
            out_specs=hbm,
            scratch_shapes=[pltpu.VMEM((W_SLOTS, N_MATS, D, D), F32), pltpu.VMEM((N_MATS, D, D), BF16),
                            pltpu.VMEM((X_SLOTS, TM, D), F32), pltpu.VMEM((Y_SLOTS, TM, D), F32),
                            pltpu.VMEM((TM, D), BF16),
                            pltpu.SemaphoreType.DMA((W_SLOTS,)), pltpu.SemaphoreType.DMA((X_SLOTS,)),
                            pltpu.SemaphoreType.DMA((Y_SLOTS,)),
                            pltpu.SMEM((MAX_TILES,), jnp.int32), pltpu.SMEM((MAX_TILES,), jnp.int32),
                            pltpu.SMEM((MAX_TILES,), jnp.int32), pltpu.SMEM((MAX_TILES,), jnp.int32),
                            pltpu.SMEM((N_EXPERTS,), jnp.int32)]),
        out_shape=jax.ShapeDtypeStruct((R_TOT, D), F32),
        compiler_params=_cparams(("arbitrary",)),
        name="moe_experts",
    )(eoff_i, xs, w_gate, w_up, w_down, b_gate.reshape(N_EXPERTS, 1, D), b_up.reshape(N_EXPERTS, 1, D),
      b_down.reshape(N_EXPERTS, 1, D))


def _combine_kernel(loff_s, gst_s, p8_s, ys_hbm, z_ref, comb_ref, loff_ref, ptab_ref, x1_ref, mod_ref, gf_ref,
                    yctx_ref, ylat_ref, ybuf, sem):
    b = pl.program_id(0)

    def make_copy(local_row, global_row, rows):
        return pltpu.make_async_copy(ys_hbm.at[pl.ds(global_row, rows)], ybuf.at[pl.ds(local_row, rows)], sem)

    started = _segment_copies(b, loff_s, gst_s, p8_s, make_copy)

    last = b * N_EXPERTS + N_EXPERTS - 1
    used_chunks = lax.shift_right_logical(loff_s[last], 3) + p8_s[last]

    def zero_chunk(j, _):
        ybuf[pl.ds(pl.multiple_of(j * SEG_ALIGN, SEG_ALIGN), SEG_ALIGN), :] = jnp.zeros((SEG_ALIGN, D), F32)
        return 0

    lax.fori_loop(used_chunks, RB // SEG_ALIGN, zero_chunk, 0)

    pm, owner = _sort_matrix(b, z_ref, loff_ref, ptab_ref)
    comb = comb_ref[...]
    c_hi = comb.astype(BF16)
    rest = comb - c_hi.astype(F32)
    c_mid = rest.astype(BF16)
    c_lo = (rest - c_mid.astype(F32)).astype(BF16)
    pmb = pm.astype(BF16)
    moved = (jnp.dot(pmb, c_hi, preferred_element_type=F32) + jnp.dot(pmb, c_mid, preferred_element_type=F32)
             + jnp.dot(pmb, c_lo, preferred_element_type=F32))
    w_row = jnp.sum(owner * moved, axis=-1, keepdims=True)
    pmt = pm.T.astype(BF16)
    _wait_copies(started, make_copy)
    y = (ybuf[...] * w_row).astype(BF16)
    moe = jnp.dot(pmt, y, preferred_element_type=F32)
    gate2 = mod_ref[0][:, 5 * D:6 * D]
    x2 = x1_ref[0] + gate2 * moe
    y = _rms(x2, gf_ref[...])

    @pl.when(b < N_CTX_BLK)
    def _():
        yctx_ref[0] = y

    @pl.when(b >= N_CTX_BLK)
    def _():
        ylat_ref[0] = y


def _combine(ys, z, comb, loff, ptab, x1, modv, norm_f_g, loff_i, gst_i, p8_i):
    whole = pl.BlockSpec((TAB_ROWS, E_PAD), lambda b, *_: (0, 0))
    tok = pl.BlockSpec((TBLK, E_PAD), lambda b, *_: (b, 0))
    return pl.pallas_call(
        _combine_kernel,
        grid_spec=pltpu.PrefetchScalarGridSpec(
            num_scalar_prefetch=3,
            grid=(N_BLK,),
            in_specs=[pl.BlockSpec(memory_space=pl.ANY), tok, tok, whole, whole,
                      pl.BlockSpec((1, TBLK, D), lambda b, *_: (b, 0, 0)),
                      pl.BlockSpec((1, 1, 6 * D), lambda b, *_: (b, 0, 0)),
                      pl.BlockSpec((1, D), lambda b, *_: (0, 0))],
            out_specs=[pl.BlockSpec((1, TBLK, D), lambda b, *_: (jnp.minimum(b, N_CTX_BLK - 1), 0, 0)),
                       pl.BlockSpec((1, TBLK, D), lambda b, *_: (jnp.maximum(b - N_CTX_BLK, 0), 0, 0))],
            scratch_shapes=[pltpu.VMEM((RB, D), F32), pltpu.SemaphoreType.DMA(())]),
        out_shape=[jax.ShapeDtypeStruct((N_CTX_BLK, TBLK, D), F32),
                   jax.ShapeDtypeStruct((N_BLK - N_CTX_BLK, TBLK, D), F32)],
        compiler_params=_cparams(("arbitrary",)),
        name="moe_combine",
    )(loff_i, gst_i, p8_i, ys, z, comb, loff, ptab, x1, modv, norm_f_g.reshape(1, D))


def _moe_and_final(x1, h2, comb, modg, norm_f_g, w_gate, b_gate, w_up, b_up, w_down, b_down):
    z, ptab, loff, gst, eoff = _plan(comb)
    as_scalars = lambda t: t[:N_BLK, :N_EXPERTS].astype(jnp.int32).reshape(N_BLK * N_EXPERTS)
    loff_i, gst_i = as_scalars(loff), as_scalars(gst)
    p8_i = as_scalars(ptab) // SEG_ALIGN
    eoff_i = eoff[0, :N_EXPERTS + 1].astype(jnp.int32)
    xs = _dispatch(h2, z, loff, ptab, loff_i, gst_i, p8_i, eoff_i[N_EXPERTS:])
    ys = _experts(xs, eoff_i, w_gate, b_gate, w_up, b_up, w_down, b_down)
    return _combine(ys, z, comb, loff, ptab, x1, modg.reshape(N_BLK, 1, 6 * D), norm_f_g,
                    loff_i, gst_i, p8_i)


def _grid_pos_embed(n_tokens, dim):
    rows = n_tokens // GRID_W
    t = np.arange(rows * GRID_W)
    r = (t // GRID_W).astype(np.float32)
    col = (t % GRID_W).astype(np.float32)
    q = dim // 4
    omega = (1.0 / np.float32(POS_TEMP) ** (np.arange(q, dtype=np.float32) / np.float32(q))).astype(np.float32)

    def emb(p):
        a = p[:, None] * omega[None, :]
        return np.concatenate([np.sin(a), np.cos(a)], axis=-1)

    return np.concatenate([emb(r), emb(col)], axis=-1).astype(np.float32)


def kernel(x_prompt, x_sample, c, state_s5_re, state_s5_im, c_ctx, w_ada, b_ada, norm1_g, w_in, s5_lam_re,
           s5_lam_im, s5_log_dt, s5_b_re, s5_b_im, s5_c_re, s5_c_im, s5_d, s5_w_glu, w_fnet, w_out, norm2_g,
           w_router, b_router, w_gate, b_gate, w_up, b_up, w_down, b_down, norm_f_g):
    n_ctx, n_lat = x_prompt.shape[0], x_sample.shape[0]
    assert x_prompt.shape == (SUB * (N_GRP - 1), L_BLK, D) and x_sample.shape == (2, LAT_CHUNKS * L_BLK, D)
    assert w_ada.shape[0] == 1, "one trunk layer"
    layer = 0

    cvec = jnp.concatenate([jnp.broadcast_to(c_ctx[None], (n_ctx, D)), jnp.repeat(c, LAT_CHUNKS, axis=0)], axis=0)
    modg = _adaln(cvec, w_ada[layer], b_ada[layer]).reshape(N_GRP, SUB, 1, 6 * D)

    xp4 = x_prompt.reshape(N_GRP - 1, SUB, L_BLK, D)
    xs4 = x_sample.reshape(1, SUB, L_BLK, D)
    pos3 = jnp.asarray(_grid_pos_embed(LAT_CHUNKS * L_BLK, D).reshape(LAT_CHUNKS, L_BLK, D))
    perm = _perm_matrix()

    us, uf = _pre(xp4, xs4, pos3, modg, norm1_g[layer], w_in[layer], jnp.asarray(perm, BF16))

    bbmat, ccmat, a8 = _s5_params(s5_lam_re[layer], s5_lam_im[layer], s5_log_dt[layer], s5_b_re[layer],
                                  s5_b_im[layer], s5_c_re[layer], s5_c_im[layer])

    st = jnp.stack([state_s5_re[:, layer], state_s5_im[:, layer]], axis=2)
    st = st.reshape(n_lat, 2, 2, N_S5_CHUNKS, CH_P)
    st = jnp.transpose(st, (1, 3, 0, 2, 4)).reshape(2, N_S5_CHUNKS, n_lat, 2 * CH_P)
    init8 = jnp.zeros((2, N_S5_CHUNKS, n_lat, LAT_CHUNKS, 2 * CH_P), F32)
    init8 = init8.at[0, :, :, 0].set(st[0]).at[1, :, :, LAT_CHUNKS - 1].set(st[1])
    init8 = init8.reshape(2, N_S5_CHUNKS, SUB, 2 * CH_P)

    y_s5, fin = _s5(us, bbmat, ccmat, a8, init8, s5_d[layer])

    m1, m2 = _fnet_weights(w_fnet[layer])
    yf = _fnet(uf, m1, m2)

    x1, h2, comb = _post(y_s5, yf, xp4, xs4, pos3, modg, s5_w_glu[layer], w_out[layer], norm2_g[layer],
                         w_router[layer], b_router[layer], jnp.asarray(perm.T, BF16))

    y_prompt, y_lat = _moe_and_final(x1.reshape(N_BLK, TBLK, D), h2.reshape(T_TOK, D), comb.reshape(T_TOK, E_PAD),
                                     modg, norm_f_g, w_gate[layer], b_gate[layer], w_up[layer], b_up[layer],
                                     w_down[layer], b_down[layer])
    y_sample = y_lat.reshape(n_lat, LAT_CHUNKS * L_BLK, D)

    fin = fin[:, :N_GRP - 1].reshape(2, N_GRP - 1, SUB, N_S5_CHUNKS, 2, CH_P)
    fin = jnp.transpose(fin, (4, 1, 2, 0, 3, 5)).reshape(2, n_ctx, 1, 2, N_S5_GROUPS, S5_P)
    return (y_prompt, y_sample, fin[0], fin[1])
```

```python
import functools
import math

import numpy as np
import jax
import jax.numpy as jnp
from jax import lax
from jax.experimental import pallas as pl
from jax.experimental.pallas import tpu as pltpu

F32 = jnp.float32
BF16 = jnp.bfloat16
HIGHEST = lax.Precision.HIGHEST

D = 1024
D_S5 = 768
S5_H = 16
S5_P = 64
N_S5_GROUPS = 48
D_FN = 256
FN_GW = 64
N_EXPERTS = 32
TOP_K = 4
E_PAD = 128
SWIGLU_LIMIT = 7.0
SWIGLU_ALPHA = 1.702
RMS_EPS = 1e-6
POS_TEMP = 10000.0
GRID_W = 64

L_BLK = 256
SUB = 8
N_GRP = 3
GRP_ROWS = L_BLK * SUB
LAT_CHUNKS = 4
T_TOK = N_GRP * GRP_ROWS
TC = 128
N_TC = L_BLK // TC
PERM_T = 32
S5_CHUNK = 16
N_S5_CHUNKS = N_S5_GROUPS // S5_CHUNK
CH_U = S5_CHUNK * S5_H
CH_P = S5_CHUNK * S5_P
VMEM_LIMIT = 56 * 1024 * 1024


def _cparams(sem):
    return pltpu.CompilerParams(dimension_semantics=sem, vmem_limit_bytes=VMEM_LIMIT)


ADA_K = 128


def _adaln_kernel(c_ref, w_ref, b_ref, o_ref):
    @pl.when(pl.program_id(0) == 0)
    def _():
        o_ref[...] = jnp.broadcast_to(b_ref[...], o_ref.shape)

    c = c_ref[...]
    s = c * jax.nn.sigmoid(c)
    s_hi = s.astype(BF16)
    s_lo = (s - s_hi.astype(F32)).astype(BF16)
    w = w_ref[...]
    w_hi = w.astype(BF16)
    w_lo = (w - w_hi.astype(F32)).astype(BF16)
    o_ref[...] += (jnp.dot(s_hi, w_hi, preferred_element_type=F32) + jnp.dot(s_lo, w_hi, preferred_element_type=F32)
                   + jnp.dot(s_hi, w_lo, preferred_element_type=F32))


def _adaln(cvec, w_ada, b_ada):
    k_dim, n = w_ada.shape
    rows = cvec.shape[0]
    return pl.pallas_call(
        _adaln_kernel,
        grid=(k_dim // ADA_K,),
        in_specs=[pl.BlockSpec((rows, ADA_K), lambda k: (0, k)),
                  pl.BlockSpec((ADA_K, n), lambda k: (k, 0)),
                  pl.BlockSpec((1, n), lambda k: (0, 0))],
        out_specs=pl.BlockSpec((rows, n), lambda k: (0, 0)),
        out_shape=jax.ShapeDtypeStruct((rows, n), F32),
        compiler_params=_cparams(("arbitrary",)),
        name="adaln",
    )(cvec, w_ada, b_ada.reshape(1, n))


def _zoh(lam_re, lam_im, log_dt):
    dt = jnp.exp(log_dt)
    a_re = jnp.minimum(lam_re, -1e-4)
    a_im = lam_im
    mag = jnp.exp(a_re * dt)
    ab_re = mag * jnp.cos(a_im * dt)
    ab_im = mag * jnp.sin(a_im * dt)
    den = a_re * a_re + a_im * a_im
    nr = ab_re - 1.0
    f_re = (nr * a_re + ab_im * a_im) / den
    f_im = (ab_im * a_re - nr * a_im) / den
    return ab_re, ab_im, f_re, f_im


def _s5_params_kernel(lre3_ref, lim3_ref, ldt3_ref, lre2_ref, lim2_ref, ldt2_ref, bre_ref, bim_ref,
                      cre_ref, cim_ref, bb_ref, cc_ref, a_ref):
    exact = functools.partial(jnp.dot, preferred_element_type=F32, precision=HIGHEST)
    spread = jnp.where(lax.broadcasted_iota(jnp.int32, (S5_P, CH_P), 1) % S5_P
                       == lax.broadcasted_iota(jnp.int32, (S5_P, CH_P), 0), 1.0, 0.0)

    def block_diag(m):
        wide = jnp.dot(m.reshape(CH_U, S5_P).astype(BF16), spread.astype(BF16), preferred_element_type=F32)
        row_g = lax.broadcasted_iota(jnp.int32, (CH_U, CH_P), 0) // S5_H
        col_g = lax.broadcasted_iota(jnp.int32, (CH_U, CH_P), 1) // S5_P
        return jnp.where(row_g == col_g, wide, 0.0)

    _, _, f_re, f_im = _zoh(lre3_ref[0, 0], lim3_ref[0, 0], ldt3_ref[0, 0])
    b_re = bre_ref[0, 0]
    b_im = bim_ref[0, 0]
    bb_ref[0, 0] = jnp.concatenate([block_diag(f_re * b_re - f_im * b_im),
                                    block_diag(f_re * b_im + f_im * b_re)], axis=1).astype(BF16)
    cc_ref[0, 0] = jnp.concatenate([block_diag(cre_ref[0, 0]).T,
                                    -block_diag(cim_ref[0, 0]).T], axis=0).astype(BF16)

    ab_re, ab_im, _, _ = _zoh(lre2_ref[0, 0], lim2_ref[0, 0], ldt2_ref[0, 0])
    row_g = lax.broadcasted_iota(jnp.int32, (S5_CHUNK, CH_P), 0)
    col_g = lax.broadcasted_iota(jnp.int32, (S5_CHUNK, CH_P), 1) // S5_P

    def lane_row(a):
        flat = jnp.sum(jnp.where(row_g == col_g, exact(a, spread), 0.0), axis=0, keepdims=True)
        return jnp.broadcast_to(flat, (SUB, CH_P))

    a_ref[0, 0] = jnp.concatenate([lane_row(ab_re), lane_row(ab_im)], axis=1)


def _s5_params(lam_re, lam_im, log_dt, b_re, b_im, c_re, c_im):
    nc = N_S5_CHUNKS
    ldt = jnp.broadcast_to(log_dt[:, :, None], (2, N_S5_GROUPS, S5_P))
    g3 = lambda a: a.reshape(2, nc, S5_CHUNK, 1, S5_P)
    g2 = lambda a: a.reshape(2, nc, S5_CHUNK, S5_P)
    ghp = lambda a: a.reshape(2, nc, S5_CHUNK, S5_H, S5_P)
    spec3 = pl.BlockSpec((1, 1, S5_CHUNK, 1, S5_P), lambda d, c: (d, c, 0, 0, 0))
    spec2 = pl.BlockSpec((1, 1, S5_CHUNK, S5_P), lambda d, c: (d, c, 0, 0))
    spec_ghp = pl.BlockSpec((1, 1, S5_CHUNK, S5_H, S5_P), lambda d, c: (d, c, 0, 0, 0))
    return pl.pallas_call(
        _s5_params_kernel,
        grid=(2, nc),
        in_specs=[spec3, spec3, spec3, spec2, spec2, spec2, spec_ghp, spec_ghp, spec_ghp, spec_ghp],
        out_specs=[pl.BlockSpec((1, 1, CH_U, 2 * CH_P), lambda d, c: (d, c, 0, 0)),
                   pl.BlockSpec((1, 1, 2 * CH_P, CH_U), lambda d, c: (d, c, 0, 0)),
                   pl.BlockSpec((1, 1, SUB, 2 * CH_P), lambda d, c: (d, c, 0, 0))],
        out_shape=[jax.ShapeDtypeStruct((2, nc, CH_U, 2 * CH_P), BF16),
                   jax.ShapeDtypeStruct((2, nc, 2 * CH_P, CH_U), BF16),
                   jax.ShapeDtypeStruct((2, nc, SUB, 2 * CH_P), F32)],
        compiler_params=_cparams(("arbitrary", "arbitrary")),
        name="s5_params",
    )(g3(lam_re), g3(lam_im), g3(ldt), g2(lam_re), g2(lam_im), g2(ldt),
      ghp(jnp.swapaxes(b_re, -1, -2)), ghp(jnp.swapaxes(b_im, -1, -2)), ghp(c_re), ghp(c_im))


def _rms(x, g):
    return x * lax.rsqrt(jnp.mean(x * x, axis=-1, keepdims=True) + RMS_EPS) * g


def _load_x(g, xp_ref, xs_ref, pos_ref):
    is_lat = g == N_GRP - 1
    pos = pos_ref[...]
    pos8 = jnp.concatenate([pos, pos], axis=0)
    return jnp.where(is_lat, xs_ref[0] + pos8, xp_ref[0])


def _x_specs():
    n_ctx = N_GRP - 1
    xp_spec = pl.BlockSpec(
        (1, SUB, TC, D),
        lambda g, tc: (jnp.minimum(g, n_ctx - 1), 0, jnp.where(g >= n_ctx, N_TC - 1, tc), 0))
    xs_spec = pl.BlockSpec((1, SUB, TC, D), lambda g, tc: (0, 0, jnp.where(g >= n_ctx, tc, 0), 0))
    pos_spec = pl.BlockSpec((LAT_CHUNKS, TC, D), lambda g, tc: (0, tc, 0))
    return xp_spec, xs_spec, pos_spec


def _perm_matrix():
    p = np.zeros((SUB * PERM_T, SUB * PERM_T), np.float32)
    for s in range(SUB):
        for j in range(PERM_T):
            p[j * SUB + s, s * PERM_T + j] = 1.0
    return p


def _pre_kernel(xp_ref, xs_ref, pos_ref, mod_ref, g1_ref, win_ref, perm_ref, us_ref, uf_ref):
    g = pl.program_id(0)
    x = _load_x(g, xp_ref, xs_ref, pos_ref)
    mod = mod_ref[0]
    shift1 = mod[:, :, 0:D]
    scale1 = mod[:, :, D:2 * D]
    h = _rms(x, g1_ref[...]) * (1.0 + scale1) + shift1
    h2d = h.reshape(SUB * TC, D).astype(BF16)
    proj = jnp.dot(h2d, win_ref[...].astype(BF16), preferred_element_type=F32)
    uf_ref[0] = proj[:, D_S5:].astype(BF16).reshape(SUB, TC, D_FN)
    u = proj[:, :D_S5].astype(BF16)
    perm = perm_ref[...]
    for q in range(TC // PERM_T):
        piece = jnp.concatenate(
            [u[s * TC + q * PERM_T: s * TC + (q + 1) * PERM_T] for s in range(SUB)], axis=0)
        us_ref[0, q * SUB * PERM_T:(q + 1) * SUB * PERM_T, :] = jnp.dot(
            perm, piece, preferred_element_type=F32).astype(BF16)


def _pre(xp4, xs4, pos3, modg, norm1_g, w_in, perm):
    xp_spec, xs_spec, pos_spec = _x_specs()
    return pl.pallas_call(
        _pre_kernel,
        grid=(N_GRP, N_TC),
        in_specs=[xp_spec, xs_spec, pos_spec,
                  pl.BlockSpec((1, SUB, 1, 6 * D), lambda g, tc: (g, 0, 0, 0)),
                  pl.BlockSpec((1, D), lambda g, tc: (0, 0)),
                  pl.BlockSpec((D, D), lambda g, tc: (0, 0)),
                  pl.BlockSpec((SUB * PERM_T, SUB * PERM_T), lambda g, tc: (0, 0))],
        out_specs=[pl.BlockSpec((1, SUB * TC, D_S5), lambda g, tc: (g, tc, 0)),
                   pl.BlockSpec((1, SUB, TC, D_FN), lambda g, tc: (g, 0, tc, 0))],
        out_shape=[jax.ShapeDtypeStruct((N_GRP, GRP_ROWS, D_S5), BF16),
                   jax.ShapeDtypeStruct((N_GRP, SUB, L_BLK, D_FN), BF16)],
        compiler_params=_cparams(("arbitrary", "arbitrary")),
        name="pre_mixer",
    )(xp4, xs4, pos3, modg, norm1_g.reshape(1, D), w_in, perm)


def _cmul(ar, ai, br, bi):
    return ar * br - ai * bi, ar * bi + ai * br


S5_BLK = 256
S5_NBLK = GRP_ROWS // S5_BLK
S5_STEPS = S5_BLK // SUB


def _s5_kernel(us_ref, bb_ref, cc_ref, a_ref, init_ref, dskip_ref, y_ref, fin_ref, in_a, in_b, out_a, out_b):
    g = pl.program_id(0)
    d = pl.program_id(2)
    is_lat = g == N_GRP - 1
    re_all = slice(0, CH_P)
    im_all = slice(CH_P, 2 * CH_P)
    a_re = a_ref[0, 0, :, re_all]
    a_im = a_ref[0, 0, :, im_all]
    bufs_in = (in_a, in_b)
    bufs_out = (out_a, out_b)

    def time_rows(k):
        blk = k + d * (S5_NBLK - 1 - 2 * k)
        return pl.ds(pl.multiple_of(blk * S5_BLK, S5_BLK), S5_BLK)

    def slot_rows(k):
        return slice((k // 2) * S5_BLK, (k // 2 + 1) * S5_BLK)

    def step_rows(k, j):
        q = j + d * (S5_STEPS - 1 - 2 * j)
        return pl.ds(pl.multiple_of((k // 2) * S5_BLK + q * SUB, SUB), SUB)

    def proj_in(k):
        bufs_in[k % 2][slot_rows(k), :] = jnp.dot(us_ref[0, time_rows(k), :], bb_ref[0, 0],
                                                  preferred_element_type=F32)

    def scan_block(k, carry):
        src, dst = bufs_in[k % 2], bufs_out[k % 2]
        s_re, s_im = carry
        for j in range(S5_STEPS):
            rows = step_rows(k, j)
            n_re = a_re * s_re - a_im * s_im + src[rows, re_all]
            n_im = a_re * s_im + a_im * s_re + src[rows, im_all]
            dst[rows, re_all] = n_re
            dst[rows, im_all] = n_im
            s_re, s_im = n_re, n_im
        return s_re, s_im

    def proj_out(k):
        rows = time_rows(k)
        yb = jnp.dot(bufs_out[k % 2][slot_rows(k), :].astype(BF16), cc_ref[0, 0], preferred_element_type=F32)
        y_ref[0, rows, :] = y_ref[0, rows, :] + yb

    @pl.when(d == 0)
    def _():
        y_ref[0] = dskip_ref[...] * us_ref[0].astype(F32)

    def scan_all(carry, with_out):
        proj_in(0)
        for k in range(S5_NBLK):
            if k + 1 < S5_NBLK:
                proj_in(k + 1)
            carry = scan_block(k, carry)
            if with_out and k >= 1:
                proj_out(k - 1)
        if with_out:
            proj_out(S5_NBLK - 1)
        fin_ref[0, 0, :, re_all] = carry[0]
        fin_ref[0, 0, :, im_all] = carry[1]
        return carry

    @pl.when(jnp.logical_not(is_lat))
    def _():
        zero = jnp.zeros((SUB, CH_P), F32)
        scan_all((zero, zero), True)

    @pl.when(is_lat)
    def _():
        f_re, f_im = scan_all((init_ref[0, 0, :, re_all], init_ref[0, 0, :, im_all]), False)

        sub_id = lax.broadcasted_iota(jnp.int32, (SUB, CH_P), 0) % LAT_CHUNKS
        fwd = d == 0
        lo = jnp.where(fwd, 1, 0)
        hi = jnp.where(fwd, LAT_CHUNKS - 1, LAT_CHUNKS - 2)
        keep = (sub_id >= lo) & (sub_id <= hi)

        def from_prev(v):
            return jnp.where(keep, jnp.where(fwd, pltpu.roll(v, 1, 0), pltpu.roll(v, SUB - 1, 0)), 0.0)

        p_re, p_im = a_re, a_im
        for _ in range(8):
            p_re, p_im = _cmul(p_re, p_im, p_re, p_im)
        t_re, t_im = f_re, f_im
        for _ in range(LAT_CHUNKS - 2):
            m_re, m_im = _cmul(p_re, p_im, from_prev(t_re), from_prev(t_im))
            t_re, t_im = f_re + m_re, f_im + m_im
        corr = (from_prev(t_re), from_prev(t_im))

        def fix_block(k, carry):
            dst = bufs_out[k % 2]
            m_re, m_im = carry
            for j in range(S5_STEPS):
                rows = step_rows(k, j)
                m_re, m_im = _cmul(m_re, m_im, a_re, a_im)
                dst[rows, re_all] = dst[rows, re_all] + m_re
                dst[rows, im_all] = dst[rows, im_all] + m_im
            return m_re, m_im

        for k in range(S5_NBLK):
            corr = fix_block(k, corr)
            if k >= 1:
                proj_out(k - 1)
        proj_out(S5_NBLK - 1)


def _s5(us, bbmat, ccmat, a8, init8, d_skip):
    return pl.pallas_call(
        _s5_kernel,
        grid=(N_GRP, N_S5_CHUNKS, 2),
        in_specs=[pl.BlockSpec((1, GRP_ROWS, CH_U), lambda g, c, d: (g, 0, c)),
                  pl.BlockSpec((1, 1, CH_U, 2 * CH_P), lambda g, c, d: (d, c, 0, 0)),
                  pl.BlockSpec((1, 1, 2 * CH_P, CH_U), lambda g, c, d: (d, c, 0, 0)),
                  pl.BlockSpec((1, 1, SUB, 2 * CH_P), lambda g, c, d: (d, c, 0, 0)),
                  pl.BlockSpec((1, 1, SUB, 2 * CH_P), lambda g, c, d: (d, c, 0, 0)),
                  pl.BlockSpec((1, CH_U), lambda g, c, d: (0, c))],
        out_specs=[pl.BlockSpec((1, GRP_ROWS, CH_U), lambda g, c, d: (g, 0, c)),
                   pl.BlockSpec((1, 1, SUB, 2 * CH_P), lambda g, c, d: (d, g, 0, c))],
        out_shape=[jax.ShapeDtypeStruct((N_GRP, GRP_ROWS, D_S5), F32),
                   jax.ShapeDtypeStruct((2, N_GRP, SUB, N_S5_CHUNKS * 2 * CH_P), F32)],
        scratch_shapes=[pltpu.VMEM((GRP_ROWS // 2, 2 * CH_P), F32)] * 4,
        compiler_params=_cparams(("arbitrary", "arbitrary", "arbitrary")),
        name="s5_scan",
    )(us, bbmat, ccmat, a8, init8, d_skip.reshape(1, D_S5))


def _fnet_w_kernel(c_ref, s_ref, w_ref, m1_ref, m2_ref):
    w = w_ref[...]
    m1_ref[...] = jnp.dot(c_ref[...], w, preferred_element_type=F32, precision=HIGHEST).astype(BF16)
    m2_ref[...] = jnp.dot(s_ref[...], w, preferred_element_type=F32, precision=HIGHEST).astype(BF16)


FN_SEQ = LAT_CHUNKS
FN_CTX_STEPS = (N_GRP - 1) * SUB // FN_SEQ


def _fnet_kernel(u_ref, cos_s_ref, sin_s_ref, cos_l_ref, sin_l_ref, m1_ref, m2_ref, o_ref):
    i = pl.program_id(0)
    m1 = m1_ref[...]
    m2 = m2_ref[...]

    def mix(u, cos_ref, sin_ref):
        v1 = jnp.dot(u, m1, preferred_element_type=F32).astype(BF16)
        v2 = jnp.dot(u, m2, preferred_element_type=F32).astype(BF16)
        return (jnp.dot(cos_ref[...].astype(BF16), v1, preferred_element_type=F32)
                - jnp.dot(sin_ref[...].astype(BF16), v2, preferred_element_type=F32)).astype(BF16)

    @pl.when(i < FN_CTX_STEPS)
    def _():
        for s in range(FN_SEQ):
            o_ref[0, s] = mix(u_ref[0, s], cos_s_ref, sin_s_ref)

    @pl.when(i >= FN_CTX_STEPS)
    def _():
        u = u_ref[0].reshape(FN_SEQ * L_BLK, D_FN)
        o_ref[0] = mix(u, cos_l_ref, sin_l_ref).reshape(FN_SEQ, L_BLK, D_FN)


def _dft_tables(n):
    k = np.arange(n, dtype=np.int64)
    ang = (2.0 * np.pi / n) * ((k[:, None] * k[None, :]) % n).astype(np.float64)
    scale = 1.0 / math.sqrt(n)
    return (np.cos(ang) * scale).astype(np.float32), (np.sin(ang) * scale).astype(np.float32)


def _fnet(uf, m1, m2):
    n_lat_steps = SUB // FN_SEQ
    cos_s, sin_s = _dft_tables(L_BLK)
    cos_l, sin_l = _dft_tables(FN_SEQ * L_BLK)
    full = lambda i: (0, 0)
    per_grp = SUB // FN_SEQ
    blk = pl.BlockSpec(
        (1, FN_SEQ, L_BLK, D_FN),
        lambda i: (jnp.minimum(i // per_grp, N_GRP - 1),
                   jnp.where(i < FN_CTX_STEPS, i % per_grp, i - FN_CTX_STEPS), 0, 0))
    table = lambda n: pl.BlockSpec((n, n), full)
    return pl.pallas_call(
        _fnet_kernel,
        grid=(FN_CTX_STEPS + n_lat_steps,),
        in_specs=[blk, table(L_BLK), table(L_BLK), table(FN_SEQ * L_BLK), table(FN_SEQ * L_BLK),
                  table(D_FN), table(D_FN)],
        out_specs=blk,
        out_shape=jax.ShapeDtypeStruct(uf.shape, BF16),
        compiler_params=_cparams(("arbitrary",)),
        name="fnet",
    )(uf, jnp.asarray(cos_s), jnp.asarray(sin_s), jnp.asarray(cos_l), jnp.asarray(sin_l), m1, m2)


def _fnet_weights(w_fnet):
    n_g = D_FN // FN_GW
    cos_c, sin_c = _dft_tables(FN_GW)
    eye = np.eye(n_g, dtype=np.float32)
    cos_bd = np.kron(eye, cos_c)
    sin_bd = np.kron(eye, sin_c)
    w_bd = (w_fnet[:, :, None, :] * jnp.asarray(eye)[:, None, :, None]).reshape(D_FN, D_FN)
    out = jax.ShapeDtypeStruct((D_FN, D_FN), BF16)
    return pl.pallas_call(_fnet_w_kernel, out_shape=(out, out), name="fnet_weights")(
        jnp.asarray(cos_bd), jnp.asarray(sin_bd), w_bd)


def _gelu_tanh(x):
    return 0.5 * x * (1.0 + jnp.tanh(math.sqrt(2.0 / math.pi) * (x + 0.044715 * (x * x * x))))


def _post_kernel(y_ref, yf_ref, xp_ref, xs_ref, pos_ref, mod_ref, wglu_ref, wout_ref, g2_ref, wr_ref,
                 br_ref, permt_ref, x1_ref, h2_ref, comb_ref):
    g = pl.program_id(0)
    z = _gelu_tanh(y_ref[0])
    gate = jnp.dot(z.astype(BF16), wglu_ref[...].astype(BF16), preferred_element_type=F32)
    gl = (z * jax.nn.sigmoid(gate)).astype(BF16)
    permt = permt_ref[...]
    n_q = TC // PERM_T
    nat = [jnp.dot(permt, gl[q * SUB * PERM_T:(q + 1) * SUB * PERM_T], preferred_element_type=F32).astype(BF16)
           for q in range(n_q)]
    gl_nat = jnp.concatenate(
        [nat[q][s * PERM_T:(s + 1) * PERM_T] for s in range(SUB) for q in range(n_q)], axis=0)
    w_out = wout_ref[...].astype(BF16)
    mixed = (jnp.dot(gl_nat, w_out[:D_S5], preferred_element_type=F32)
             + jnp.dot(yf_ref[0].reshape(SUB * TC, D_FN), w_out[D_S5:], preferred_element_type=F32))
    x = _load_x(g, xp_ref, xs_ref, pos_ref)
    mod = mod_ref[0]
    gate1 = mod[:, :, 2 * D:3 * D]
    shift2 = mod[:, :, 3 * D:4 * D]
    scale2 = mod[:, :, 4 * D:5 * D]
    x1 = x + gate1 * mixed.reshape(SUB, TC, D)
    x1_ref[0] = x1
    h2 = _rms(x1, g2_ref[...]) * (1.0 + scale2) + shift2
    h2_ref[0] = h2.astype(BF16)

    hr = h2.reshape(SUB * TC, D)
    h_hi = hr.astype(BF16)
    h_lo = (hr - h_hi.astype(F32)).astype(BF16)
    wr = wr_ref[...]
    w_hi = wr.astype(BF16)
    w_lo = (wr - w_hi.astype(F32)).astype(BF16)
    logits = (jnp.dot(h_hi, w_hi, preferred_element_type=F32) + jnp.dot(h_lo, w_hi, preferred_element_type=F32)
              + jnp.dot(h_hi, w_lo, preferred_element_type=F32) + br_ref[...])
    lane = lax.broadcasted_iota(jnp.int32, logits.shape, 1).astype(F32)
    top_v, hots = [], []
    cur = logits
    for _ in range(TOP_K):
        m = jnp.max(cur, axis=-1, keepdims=True)
        idx = jnp.min(jnp.where(cur == m, lane, float(E_PAD)), axis=-1, keepdims=True)
        hot = lane == idx
        top_v.append(m)
        hots.append(hot)
        cur = jnp.where(hot, -3.0e38, cur)
    exps = [jnp.exp(v - top_v[0]) for v in top_v]
    denom = exps[0] + exps[1] + exps[2] + exps[3]
    comb = jnp.zeros(logits.shape, F32)
    for k in range(TOP_K):
        comb = comb + jnp.where(hots[k], exps[k] / denom, 0.0)
    comb_ref[0] = comb.reshape(SUB, TC, E_PAD)


def _post(y, yf, xp4, xs4, pos3, modg, w_glu, w_out, norm2_g, w_router, b_router, permt):
    xp_spec, xs_spec, pos_spec = _x_specs()
    const2 = lambda g, tc: (0, 0)
    wr = jnp.zeros((D, E_PAD), F32).at[:, :N_EXPERTS].set(w_router)
    br = jnp.full((1, E_PAD), -1.0e30, F32).at[0, :N_EXPERTS].set(b_router)
    blk = lambda w: pl.BlockSpec((1, SUB, TC, w), lambda g, tc: (g, 0, tc, 0))
    return pl.pallas_call(
        _post_kernel,
        grid=(N_GRP, N_TC),
        in_specs=[pl.BlockSpec((1, SUB * TC, D_S5), lambda g, tc: (g, tc, 0)),
                  blk(D_FN), xp_spec, xs_spec, pos_spec,
                  pl.BlockSpec((1, SUB, 1, 6 * D), lambda g, tc: (g, 0, 0, 0)),
                  pl.BlockSpec((D_S5, D_S5), const2),
                  pl.BlockSpec((D, D), const2),
                  pl.BlockSpec((1, D), const2),
                  pl.BlockSpec((D, E_PAD), const2),
                  pl.BlockSpec((1, E_PAD), const2),
                  pl.BlockSpec((SUB * PERM_T, SUB * PERM_T), const2)],
        out_specs=[blk(D), blk(D), blk(E_PAD)],
        out_shape=[jax.ShapeDtypeStruct((N_GRP, SUB, L_BLK, D), F32),
                   jax.ShapeDtypeStruct((N_GRP, SUB, L_BLK, D), BF16),
                   jax.ShapeDtypeStruct((N_GRP, SUB, L_BLK, E_PAD), F32)],
        compiler_params=_cparams(("arbitrary", "arbitrary")),
        name="post_mixer",
    )(y, yf, xp4, xs4, pos3, modg, w_glu, w_out, norm2_g.reshape(1, D), wr, br, permt)


TBLK = L_BLK
N_BLK = T_TOK // TBLK
N_CTX_BLK = (N_GRP - 1) * SUB
SEG_ALIGN = 8
RB = 1280
assert RB >= TBLK * TOP_K + N_EXPERTS * (SEG_ALIGN - 1) and RB % 128 == 0
TM = 256
TM_SHIFT = 8
assert 1 << TM_SHIFT == TM
R_TOT = T_TOK * TOP_K + N_BLK * N_EXPERTS * (SEG_ALIGN - 1) + TM
TAB_ROWS = 32
assert TAB_ROWS >= N_BLK
BIG_ROWS = 32
BIG_SHIFT = 2
assert SEG_ALIGN << BIG_SHIFT == BIG_ROWS


def _plan_kernel(comb_ref, z_ref, ptab_ref, loff_ref, loff_i_ref, gst_i_ref, p8_i_ref, eoff_i_ref):
    row = lax.broadcasted_iota(jnp.int32, (TBLK, TBLK), 0)
    col = lax.broadcasted_iota(jnp.int32, (TBLK, TBLK), 1)
    earlier = jnp.where(row > col, 1.0, 0.0).astype(BF16)
    ptab_ref[...] = jnp.zeros_like(ptab_ref)

    def body(b, _):
        rows = pl.ds(pl.multiple_of(b * TBLK, TBLK), TBLK)
        m = jnp.where(comb_ref[rows, :] > 0.0, 1.0, 0.0)
        rank = jnp.dot(earlier, m.astype(BF16), preferred_element_type=F32)
        z_ref[rows, :] = m * (rank + 1.0)
        n = jnp.sum(m, axis=0, keepdims=True)
        ptab_ref[pl.ds(b, 1), :] = jnp.floor((n + (SEG_ALIGN - 1)) * (1.0 / SEG_ALIGN)) * SEG_ALIGN
        return 0

    lax.fori_loop(0, N_BLK, body, 0)

    ptab = ptab_ref[...]
    er = lax.broadcasted_iota(jnp.int32, (E_PAD, E_PAD), 0)
    ec = lax.broadcasted_iota(jnp.int32, (E_PAD, E_PAD), 1)
    before = jnp.where(er < ec, 1.0, 0.0)
    exact = functools.partial(jnp.dot, preferred_element_type=F32, precision=HIGHEST)
    loff = exact(ptab, before)
    loff_ref[...] = loff
    as_int = lambda t: t[:, :N_EXPERTS].astype(jnp.int32)
    loff_i_ref[...] = as_int(loff)
    p8_i_ref[...] = as_int(ptab * (1.0 / SEG_ALIGN))
    tot = jnp.sum(ptab, axis=0, keepdims=True)
    eoff = exact(jnp.broadcast_to(tot, (SUB, E_PAD)), before)
    eoff_i_ref[...] = eoff.astype(jnp.int32)
    br = lax.broadcasted_iota(jnp.int32, (TAB_ROWS, TAB_ROWS), 0)
    bc = lax.broadcasted_iota(jnp.int32, (TAB_ROWS, TAB_ROWS), 1)
    gst_i_ref[...] = as_int(eoff[0:1] + exact(jnp.where(br > bc, 1.0, 0.0), ptab))


def _plan(comb):
    tab = jax.ShapeDtypeStruct((TAB_ROWS, E_PAD), F32)
    itab = jax.ShapeDtypeStruct((TAB_ROWS, N_EXPERTS), jnp.int32)
    return pl.pallas_call(
        _plan_kernel,
        out_shape=(jax.ShapeDtypeStruct((T_TOK, E_PAD), F32), tab, tab, itab, itab, itab,
                   jax.ShapeDtypeStruct((SUB, E_PAD), jnp.int32)),
        compiler_params=pltpu.CompilerParams(vmem_limit_bytes=VMEM_LIMIT),
        name="moe_plan",
    )(comb)


def _sort_matrix(b, z_ref, loff_ref, ptab_ref):
    loff = loff_ref[pl.ds(b, 1), :]
    size = ptab_ref[pl.ds(b, 1), :]
    r = lax.broadcasted_iota(jnp.int32, (RB, E_PAD), 0).astype(F32)
    owner = jnp.where(r >= loff, jnp.where(r < loff + size, 1.0, 0.0), 0.0)
    rank1 = r[:, 0:1] - jnp.sum(owner * loff, axis=-1, keepdims=True) + 1.0
    zt = z_ref[...].T
    v = jnp.dot(owner.astype(BF16), zt.astype(BF16), preferred_element_type=F32)
    return jnp.where(v == rank1, 1.0, 0.0), owner


def _segment_copies(b, loff_s, gst_s, p8_s, make_copy):
    def per_expert(e, counts):
        chunks = p8_s[b, e]
        n_big = lax.shift_right_logical(chunks, BIG_SHIFT)
        n_small = chunks - n_big * (BIG_ROWS // SEG_ALIGN)
        local0 = loff_s[b, e]
        global0 = gst_s[b, e]

        def big(j, _):
            make_copy(pl.multiple_of(local0 + j * BIG_ROWS, SEG_ALIGN),
                      pl.multiple_of(global0 + j * BIG_ROWS, SEG_ALIGN), BIG_ROWS).start()
            return 0

        def small(j, _):
            off = n_big * BIG_ROWS + j * SEG_ALIGN
            make_copy(pl.multiple_of(local0 + off, SEG_ALIGN),
                      pl.multiple_of(global0 + off, SEG_ALIGN), SEG_ALIGN).start()
            return 0

        lax.fori_loop(0, n_big, big, 0)
        lax.fori_loop(0, n_small, small, 0)
        return counts[0] + n_big, counts[1] + n_small

    return lax.fori_loop(0, N_EXPERTS, per_expert, (0, 0))


def _wait_copies(counts, make_copy):
    for n, rows in zip(counts, (BIG_ROWS, SEG_ALIGN)):
        def wait_one(i, _, rows=rows):
            make_copy(0, 0, rows).wait()
            return 0
        lax.fori_loop(0, n, wait_one, 0)


def _zero_rows_from(hbm, zeros_vmem, first_row, sem):
    def copy_to(row):
        return pltpu.make_async_copy(zeros_vmem, hbm.at[pl.ds(row, TM)], sem)

    n_full = lax.shift_right_logical(R_TOT - first_row, TM_SHIFT)

    def start_one(j, _):
        copy_to(pl.multiple_of(first_row + j * TM, SEG_ALIGN)).start()
        return 0

    def wait_one(j, _):
        copy_to(0).wait()
        return 0

    lax.fori_loop(0, n_full, start_one, 0)
    lax.fori_loop(0, n_full, wait_one, 0)
    last = copy_to(R_TOT - TM)
    last.start()
    last.wait()


def _dispatch_kernel(loff_s, gst_s, p8_s, eoff_s, h_ref, z_ref, loff_ref, ptab_ref, xs_hbm, xbuf, sem, pending):
    b = pl.program_id(0)

    def make_copy(local_row, global_row, rows):
        return pltpu.make_async_copy(xbuf.at[pl.ds(local_row, rows)], xs_hbm.at[pl.ds(global_row, rows)], sem)

    pm = _sort_matrix(b, z_ref, loff_ref, ptab_ref)[0].astype(BF16)

    @pl.when(b > 0)
    def _():
        _wait_copies((pending[0], pending[1]), make_copy)

    xbuf[...] = jnp.dot(pm, h_ref[...], preferred_element_type=F32)
    n_big, n_small = _segment_copies(b, loff_s, gst_s, p8_s, make_copy)
    pending[0] = n_big
    pending[1] = n_small

    @pl.when(b == N_BLK - 1)
    def _():
        _wait_copies((n_big, n_small), make_copy)
        xbuf[0:TM, :] = jnp.zeros((TM, D), F32)
        _zero_rows_from(xs_hbm, xbuf.at[0:TM], eoff_s[0, N_EXPERTS], sem)


def _dispatch(h2, z, loff, ptab, loff_i, gst_i, p8_i, eoff_i):
    whole = pl.BlockSpec((TAB_ROWS, E_PAD), lambda b, *_: (0, 0))
    return pl.pallas_call(
        _dispatch_kernel,
        grid_spec=pltpu.PrefetchScalarGridSpec(
            num_scalar_prefetch=4,
            grid=(N_BLK,),
            in_specs=[pl.BlockSpec((TBLK, D), lambda b, *_: (b, 0)),
                      pl.BlockSpec((TBLK, E_PAD), lambda b, *_: (b, 0)),
                      whole, whole],
            out_specs=pl.BlockSpec(memory_space=pl.ANY),
            scratch_shapes=[pltpu.VMEM((RB, D), F32), pltpu.SemaphoreType.DMA(()),
                            pltpu.SMEM((2,), jnp.int32)]),
        out_shape=jax.ShapeDtypeStruct((R_TOT, D), F32),
        compiler_params=_cparams(("arbitrary",)),
        name="moe_dispatch",
    )(loff_i, gst_i, p8_i, eoff_i, h2, z, loff, ptab)


MAX_TILES = R_TOT // TM + N_EXPERTS
X_AHEAD = 3
X_SLOTS = X_AHEAD + 1
Y_SLOTS = 2
W_AHEAD = 2
W_SLOTS = W_AHEAD + 1
N_MATS = 3
FF_CHUNK = 256
CAST_ROWS = 64
TILE_STEP = 64


def _expert_kernel(eoff_s, xs_hbm, wg_hbm, wu_hbm, wd_hbm, bg_ref, bu_ref, bd_ref, ys_hbm,
                   wst, wbf, xin, yout, act_ref, w_sem, x_sem, y_sem, t_exp, t_row, t_first, t_valid, live):
    def add_expert(e, carry):
        n_t, n_live = carry
        start = eoff_s[0, e]
        count = eoff_s[0, e + 1] - start
        tiles = lax.shift_right_logical(count + (TM - 1), TM_SHIFT)

        def add_tile(i, _):
            t_exp[n_t + i] = e
            t_row[n_t + i] = start + i * TM
            t_first[n_t + i] = jnp.where(i == 0, 1, 0)
            t_valid[n_t + i] = jnp.minimum(count - i * TM, TM)
            return 0

        lax.fori_loop(0, tiles, add_tile, 0)
        has_rows = jnp.where(tiles > 0, 1, 0)

        @pl.when(tiles > 0)
        def _():
            live[n_live] = e

        return n_t + tiles, n_live + has_rows

    n_tiles, n_live = lax.fori_loop(0, N_EXPERTS, add_expert, (0, 0))

    def w_copies(e, slot):
        return [pltpu.make_async_copy(w.at[e], wst.at[slot, m], w_sem.at[slot])
                for m, w in enumerate((wg_hbm, wu_hbm, wd_hbm))]

    def tile_rows(j):
        return pl.ds(pl.multiple_of(t_row[j], SEG_ALIGN), TM)

    def x_copy(j, slot):
        return pltpu.make_async_copy(xs_hbm.at[tile_rows(j)], xin.at[slot], x_sem.at[slot])

    def y_pieces(j, op):
        ys = lax.rem(j, Y_SLOTS)
        valid = t_valid[j]
        row0 = t_row[j]

        def piece(off, rows):
            return pltpu.make_async_copy(yout.at[ys, pl.ds(off, rows)],
                                         ys_hbm.at[pl.ds(pl.multiple_of(row0 + off, SEG_ALIGN), rows)],
                                         y_sem.at[ys])

        @pl.when(valid == TM)
        def _():
            op(piece(0, TM))

        @pl.when(valid < TM)
        def _():
            n_big = lax.shift_right_logical(valid, BIG_ROWS.bit_length() - 1)
            n_small = lax.shift_right_logical(valid - n_big * BIG_ROWS, SEG_ALIGN.bit_length() - 1)

            def big(i, _):
                op(piece(pl.multiple_of(i * BIG_ROWS, SEG_ALIGN), BIG_ROWS))
                return 0

            def small(i, _):
                op(piece(pl.multiple_of(n_big * BIG_ROWS + i * SEG_ALIGN, SEG_ALIGN), SEG_ALIGN))
                return 0

            lax.fori_loop(0, n_big, big, 0)
            lax.fori_loop(0, n_small, small, 0)

    for ahead in range(W_AHEAD):
        @pl.when(n_live > ahead)
        def _(ahead=ahead):
            for cp in w_copies(live[ahead], ahead):
                cp.start()

    for ahead in range(X_AHEAD):
        @pl.when(n_tiles > ahead)
        def _(ahead=ahead):
            x_copy(ahead, ahead).start()

    def body(j, k):
        e = t_exp[j]

        @pl.when(t_first[j] == 1)
        def _():
            ws = lax.rem(k, W_SLOTS)

            @pl.when(k + W_AHEAD < n_live)
            def _():
                for cp in w_copies(live[k + W_AHEAD], lax.rem(k + W_AHEAD, W_SLOTS)):
                    cp.start()

            for cp in w_copies(e, ws):
                cp.wait()

            def cast_rows(r, _):
                rows = pl.ds(pl.multiple_of(r * CAST_ROWS, CAST_ROWS), CAST_ROWS)
                for m in range(N_MATS):
                    wbf[m, rows, :] = wst[ws, m, rows, :].astype(BF16)
                return 0

            lax.fori_loop(0, D // CAST_ROWS, cast_rows, 0)

        slot = lax.rem(j, X_SLOTS)
        x_copy(j, slot).wait()

        @pl.when(j + X_AHEAD < n_tiles)
        def _():
            x_copy(j + X_AHEAD, lax.rem(j + X_AHEAD, X_SLOTS)).start()

        @pl.when(j >= Y_SLOTS)
        def _():
            y_pieces(j - Y_SLOTS, lambda cp: cp.wait())

        ys = lax.rem(j, Y_SLOTS)

        def compute(rows):
            x = xin[slot, 0:rows, :].astype(BF16)
            for c in range(D // FF_CHUNK):
                cols = slice(c * FF_CHUNK, (c + 1) * FF_CHUNK)
                gate = jnp.dot(x, wbf[0, :, cols], preferred_element_type=F32) + bg_ref[e][:, cols]
                up = jnp.dot(x, wbf[1, :, cols], preferred_element_type=F32) + bu_ref[e][:, cols]
                gate = jnp.minimum(gate, SWIGLU_LIMIT)
                up = jnp.clip(up, -SWIGLU_LIMIT, SWIGLU_LIMIT)
                act_ref[0:rows, cols] = ((up + 1.0) * gate * jax.nn.sigmoid(SWIGLU_ALPHA * gate)).astype(BF16)
            yout[ys, 0:rows, :] = jnp.dot(act_ref[0:rows, :], wbf[2], preferred_element_type=F32) + bd_ref[e]

        valid = t_valid[j]
        for rows in range(TILE_STEP, TM + 1, TILE_STEP):
            @pl.when(jnp.logical_and(valid > rows - TILE_STEP, valid <= rows))
            def _(rows=rows):
                compute(rows)

        y_pieces(j, lambda cp: cp.start())
        return k + t_first[j]

    lax.fori_loop(0, n_tiles, body, 0)

    for back in range(Y_SLOTS, 0, -1):
        @pl.when(n_tiles >= back)
        def _(back=back):
            y_pieces(n_tiles - back, lambda cp: cp.wait())

    yout[0] = jnp.zeros((TM, D), F32)
    _zero_rows_from(ys_hbm, yout.at[0], eoff_s[0, N_EXPERTS], y_sem.at[0])


def _experts(xs, eoff_i, w_gate, b_gate, w_up, b_up, w_down, b_down):
    hbm = pl.BlockSpec(memory_space=pl.ANY)
    bspec = pl.BlockSpec((N_EXPERTS, 1, D), lambda i, *_: (0, 0, 0))
    return pl.pallas_call(
        _expert_kernel,
        grid_spec=pltpu.PrefetchScalarGridSpec(
            num_scalar_prefetch=1,
            grid=(1,),
            in_specs=[hbm, hbm, hbm, hbm, bspec, bspec, bspec],
            out_specs=hbm,
            scratch_shapes=[pltpu.VMEM((W_SLOTS, N_MATS, D, D), F32), pltpu.VMEM((N_MATS, D, D), BF16),
                            pltpu.VMEM((X_SLOTS, TM, D), F32), pltpu.VMEM((Y_SLOTS, TM, D), F32),
                            pltpu.VMEM((TM, D), BF16),
                            pltpu.SemaphoreType.DMA((W_SLOTS,)), pltpu.SemaphoreType.DMA((X_SLOTS,)),
                            pltpu.SemaphoreType.DMA((Y_SLOTS,)),
                            pltpu.SMEM((MAX_TILES,), jnp.int32), pltpu.SMEM((MAX_TILES,), jnp.int32),
                            pltpu.SMEM((MAX_TILES,), jnp.int32), pltpu.SMEM((MAX_TILES,), jnp.int32),
                            pltpu.SMEM((N_EXPERTS,), jnp.int32)]),
        out_shape=jax.ShapeDtypeStruct((R_TOT, D), F32),
        compiler_params=_cparams(("arbitrary",)),
        name="moe_experts",
    )(eoff_i, xs, w_gate, w_up, w_down, b_gate.reshape(N_EXPERTS, 1, D), b_up.reshape(N_EXPERTS, 1, D),
      b_down.reshape(N_EXPERTS, 1, D))


def _combine_kernel(loff_s, gst_s, p8_s, ys_hbm, z_ref, comb_ref, loff_ref, ptab_ref, x1_ref, mod_ref, gf_ref,
                    yctx_ref, ylat_ref, ybuf, sem):
    b = pl.program_id(0)

    def make_copy(local_row, global_row, rows):
        return pltpu.make_async_copy(ys_hbm.at[pl.ds(global_row, rows)], ybuf.at[pl.ds(local_row, rows)], sem)

    started = _segment_copies(b, loff_s, gst_s, p8_s, make_copy)

    used_chunks = lax.shift_right_logical(loff_s[b, N_EXPERTS - 1], 3) + p8_s[b, N_EXPERTS - 1]

    def zero_chunk(j, _):
        ybuf[pl.ds(pl.multiple_of(j * SEG_ALIGN, SEG_ALIGN), SEG_ALIGN), :] = jnp.zeros((SEG_ALIGN, D), F32)
        return 0

    lax.fori_loop(used_chunks, RB // SEG_ALIGN, zero_chunk, 0)

    pm, owner = _sort_matrix(b, z_ref, loff_ref, ptab_ref)
    comb = comb_ref[...]
    c_hi = comb.astype(BF16)
    rest = comb - c_hi.astype(F32)
    c_mid = rest.astype(BF16)
    c_lo = (rest - c_mid.astype(F32)).astype(BF16)
    pmb = pm.astype(BF16)
    moved = (jnp.dot(pmb, c_hi, preferred_element_type=F32) + jnp.dot(pmb, c_mid, preferred_element_type=F32)
             + jnp.dot(pmb, c_lo, preferred_element_type=F32))
    w_row = jnp.sum(owner * moved, axis=-1, keepdims=True)
    pmt = pm.T.astype(BF16)
    _wait_copies(started, make_copy)
    y = (ybuf[...] * w_row).astype(BF16)
    moe = jnp.dot(pmt, y, preferred_element_type=F32)
    gate2 = mod_ref[0][:, 5 * D:6 * D]
    x2 = x1_ref[0] + gate2 * moe
    y = _rms(x2, gf_ref[...])

    @pl.when(b < N_CTX_BLK)
    def _():
        yctx_ref[0] = y

    @pl.when(b >= N_CTX_BLK)
    def _():
        ylat_ref[0] = y


def _combine(ys, z, comb, loff, ptab, x1, modv, norm_f_g, loff_i, gst_i, p8_i):
    whole = pl.BlockSpec((TAB_ROWS, E_PAD), lambda b, *_: (0, 0))
    tok = pl.BlockSpec((TBLK, E_PAD), lambda b, *_: (b, 0))
    return pl.pallas_call(
        _combine_kernel,
        grid_spec=pltpu.PrefetchScalarGridSpec(
            num_scalar_prefetch=3,
            grid=(N_BLK,),
            in_specs=[pl.BlockSpec(memory_space=pl.ANY), tok, tok, whole, whole,
                      pl.BlockSpec((1, TBLK, D), lambda b, *_: (b, 0, 0)),
                      pl.BlockSpec((1, 1, 6 * D), lambda b, *_: (b, 0, 0)),
                      pl.BlockSpec((1, D), lambda b, *_: (0, 0))],
            out_specs=[pl.BlockSpec((1, TBLK, D), lambda b, *_: (jnp.minimum(b, N_CTX_BLK - 1), 0, 0)),
                       pl.BlockSpec((1, TBLK, D), lambda b, *_: (jnp.maximum(b - N_CTX_BLK, 0), 0, 0))],
            scratch_shapes=[pltpu.VMEM((RB, D), F32), pltpu.SemaphoreType.DMA(())]),
        out_shape=[jax.ShapeDtypeStruct((N_CTX_BLK, TBLK, D), F32),
                   jax.ShapeDtypeStruct((N_BLK - N_CTX_BLK, TBLK, D), F32)],
        compiler_params=_cparams(("arbitrary",)),
        name="moe_combine",
    )(loff_i, gst_i, p8_i, ys, z, comb, loff, ptab, x1, modv, norm_f_g.reshape(1, D))


def _moe_and_final(x1, h2, comb, modg, norm_f_g, w_gate, b_gate, w_up, b_up, w_down, b_down):
    z, ptab, loff, loff_i, gst_i, p8_i, eoff_i = _plan(comb)
    xs = _dispatch(h2, z, loff, ptab, loff_i, gst_i, p8_i, eoff_i)
    ys = _experts(xs, eoff_i, w_gate, b_gate, w_up, b_up, w_down, b_down)
    return _combine(ys, z, comb, loff, ptab, x1, modg.reshape(N_BLK, 1, 6 * D), norm_f_g,
                    loff_i, gst_i, p8_i)


def _grid_pos_embed(n_tokens, dim):
    rows = n_tokens // GRID_W
    t = np.arange(rows * GRID_W)
    r = (t // GRID_W).astype(np.float32)
    col = (t % GRID_W).astype(np.float32)
    q = dim // 4
    omega = (1.0 / np.float32(POS_TEMP) ** (np.arange(q, dtype=np.float32) / np.float32(q))).astype(np.float32)

    def emb(p):
        a = p[:, None] * omega[None, :]
        return np.concatenate([np.sin(a), np.cos(a)], axis=-1)

    return np.concatenate([emb(r), emb(col)], axis=-1).astype(np.float32)


def kernel(x_prompt, x_sample, c, state_s5_re, state_s5_im, c_ctx, w_ada, b_ada, norm1_g, w_in, s5_lam_re,
           s5_lam_im, s5_log_dt, s5_b_re, s5_b_im, s5_c_re, s5_c_im, s5_d, s5_w_glu, w_fnet, w_out, norm2_g,
           w_router, b_router, w_gate, b_gate, w_up, b_up, w_down, b_down, norm_f_g):
    n_ctx, n_lat = x_prompt.shape[0], x_sample.shape[0]
    assert x_prompt.shape == (SUB * (N_GRP - 1), L_BLK, D) and x_sample.shape == (2, LAT_CHUNKS * L_BLK, D)
    assert w_ada.shape[0] == 1, "one trunk layer"
    layer = 0

    cvec = jnp.concatenate([jnp.broadcast_to(c_ctx[None], (n_ctx, D)), jnp.repeat(c, LAT_CHUNKS, axis=0)], axis=0)
    modg = _adaln(cvec, w_ada[layer], b_ada[layer]).reshape(N_GRP, SUB, 1, 6 * D)

    xp4 = x_prompt.reshape(N_GRP - 1, SUB, L_BLK, D)
    xs4 = x_sample.reshape(1, SUB, L_BLK, D)
    pos3 = jnp.asarray(_grid_pos_embed(LAT_CHUNKS * L_BLK, D).reshape(LAT_CHUNKS, L_BLK, D))
    perm = _perm_matrix()

    us, uf = _pre(xp4, xs4, pos3, modg, norm1_g[layer], w_in[layer], jnp.asarray(perm, BF16))

    bbmat, ccmat, a8 = _s5_params(s5_lam_re[layer], s5_lam_im[layer], s5_log_dt[layer], s5_b_re[layer],
                                  s5_b_im[layer], s5_c_re[layer], s5_c_im[layer])

    st = jnp.stack([state_s5_re[:, layer], state_s5_im[:, layer]], axis=2)
    st = st.reshape(n_lat, 2, 2, N_S5_CHUNKS, CH_P)
    st = jnp.transpose(st, (1, 3, 0, 2, 4)).reshape(2, N_S5_CHUNKS, n_lat, 2 * CH_P)
    init8 = jnp.zeros((2, N_S5_CHUNKS, n_lat, LAT_CHUNKS, 2 * CH_P), F32)
    init8 = init8.at[0, :, :, 0].set(st[0]).at[1, :, :, LAT_CHUNKS - 1].set(st[1])
    init8 = init8.reshape(2, N_S5_CHUNKS, SUB, 2 * CH_P)

    y_s5, fin = _s5(us, bbmat, ccmat, a8, init8, s5_d[layer])

    m1, m2 = _fnet_weights(w_fnet[layer])
    yf = _fnet(uf, m1, m2)

    x1, h2, comb = _post(y_s5, yf, xp4, xs4, pos3, modg, s5_w_glu[layer], w_out[layer], norm2_g[layer],
                         w_router[layer], b_router[layer], jnp.asarray(perm.T, BF16))

    y_prompt, y_lat = _moe_and_final(x1.reshape(N_BLK, TBLK, D), h2.reshape(T_TOK, D), comb.reshape(T_TOK, E_PAD),
                                     modg, norm_f_g, w_gate[layer], b_gate[layer], w_up[layer], b_up[layer],
                                     w_down[layer], b_down[layer])
    y_sample = y_lat.reshape(n_lat, LAT_CHUNKS * L_BLK, D)

    fin = fin[:, :N_GRP - 1].reshape(2, N_GRP - 1, SUB, N_S5_CHUNKS, 2, CH_P)
    fin = jnp.transpose(fin, (4, 1, 2, 0, 3, 5)).reshape(2, n_ctx, 1, 2, N_S5_GROUPS, S5_P)
    return (y_prompt, y_sample, fin[0], fin[1])
```

```python
import functools
import math

import numpy as np
import jax
import jax.numpy as jnp
from jax import lax
from jax.experimental import pallas as pl
from jax.experimental.pallas import tpu as pltpu

F32 = jnp.float32
BF16 = jnp.bfloat16
HIGHEST = lax.Precision.HIGHEST

D = 1024
D_S5 = 768
S5_H = 16
S5_P = 64
N_S5_GROUPS = 48
D_FN = 256
FN_GW = 64
N_EXPERTS = 32
TOP_K = 4
E_PAD = 128
SWIGLU_LIMIT = 7.0
SWIGLU_ALPHA = 1.702
RMS_EPS = 1e-6
POS_TEMP = 10000.0
GRID_W = 64

L_BLK = 256
SUB = 8
N_GRP = 3
GRP_ROWS = L_BLK * SUB
LAT_CHUNKS = 4
T_TOK = N_GRP * GRP_ROWS
TC = 128
N_TC = L_BLK // TC
PERM_T = 32
S5_CHUNK = 16
N_S5_CHUNKS = N_S5_GROUPS // S5_CHUNK
CH_U = S5_CHUNK * S5_H
CH_P = S5_CHUNK * S5_P
VMEM_LIMIT = 56 * 1024 * 1024


def _cparams(sem):
    return pltpu.CompilerParams(dimension_semantics=sem, vmem_limit_bytes=VMEM_LIMIT)


ADA_K = 128


def _adaln_kernel(c_ref, w_ref, b_ref, o_ref):
    @pl.when(pl.program_id(0) == 0)
    def _():
        o_ref[...] = jnp.broadcast_to(b_ref[...], o_ref.shape)

    c = c_ref[...]
    s = c * jax.nn.sigmoid(c)
    s_hi = s.astype(BF16)
    s_lo = (s - s_hi.astype(F32)).astype(BF16)
    w = w_ref[...]
    w_hi = w.astype(BF16)
    w_lo = (w - w_hi.astype(F32)).astype(BF16)
    o_ref[...] += (jnp.dot(s_hi, w_hi, preferred_element_type=F32) + jnp.dot(s_lo, w_hi, preferred_element_type=F32)
                   + jnp.dot(s_hi, w_lo, preferred_element_type=F32))


def _adaln(cvec, w_ada, b_ada):
    k_dim, n = w_ada.shape
    rows = cvec.shape[0]
    return pl.pallas_call(
        _adaln_kernel,
        grid=(k_dim // ADA_K,),
        in_specs=[pl.BlockSpec((rows, ADA_K), lambda k: (0, k)),
                  pl.BlockSpec((ADA_K, n), lambda k: (k, 0)),
                  pl.BlockSpec((1, n), lambda k: (0, 0))],
        out_specs=pl.BlockSpec((rows, n), lambda k: (0, 0)),
        out_shape=jax.ShapeDtypeStruct((rows, n), F32),
        compiler_params=_cparams(("arbitrary",)),
        name="adaln",
    )(cvec, w_ada, b_ada.reshape(1, n))


def _zoh(lam_re, lam_im, log_dt):
    dt = jnp.exp(log_dt)
    a_re = jnp.minimum(lam_re, -1e-4)
    a_im = lam_im
    mag = jnp.exp(a_re * dt)
    ab_re = mag * jnp.cos(a_im * dt)
    ab_im = mag * jnp.sin(a_im * dt)
    den = a_re * a_re + a_im * a_im
    nr = ab_re - 1.0
    f_re = (nr * a_re + ab_im * a_im) / den
    f_im = (ab_im * a_re - nr * a_im) / den
    return ab_re, ab_im, f_re, f_im


def _s5_params_kernel(lre3_ref, lim3_ref, ldt3_ref, lre2_ref, lim2_ref, ldt2_ref, bre_ref, bim_ref,
                      cre_ref, cim_ref, bb_ref, cc_ref, a_ref):
    exact = functools.partial(jnp.dot, preferred_element_type=F32, precision=HIGHEST)
    spread = jnp.where(lax.broadcasted_iota(jnp.int32, (S5_P, CH_P), 1) % S5_P
                       == lax.broadcasted_iota(jnp.int32, (S5_P, CH_P), 0), 1.0, 0.0)

    def block_diag(m):
        wide = jnp.dot(m.reshape(CH_U, S5_P).astype(BF16), spread.astype(BF16), preferred_element_type=F32)
        row_g = lax.broadcasted_iota(jnp.int32, (CH_U, CH_P), 0) // S5_H
        col_g = lax.broadcasted_iota(jnp.int32, (CH_U, CH_P), 1) // S5_P
        return jnp.where(row_g == col_g, wide, 0.0)

    _, _, f_re, f_im = _zoh(lre3_ref[0, 0], lim3_ref[0, 0], ldt3_ref[0, 0])
    b_re = bre_ref[0, 0]
    b_im = bim_ref[0, 0]
    bb_ref[0, 0] = jnp.concatenate([block_diag(f_re * b_re - f_im * b_im),
                                    block_diag(f_re * b_im + f_im * b_re)], axis=1).astype(BF16)
    cc_ref[0, 0] = jnp.concatenate([block_diag(cre_ref[0, 0]).T,
                                    -block_diag(cim_ref[0, 0]).T], axis=0).astype(BF16)

    ab_re, ab_im, _, _ = _zoh(lre2_ref[0, 0], lim2_ref[0, 0], ldt2_ref[0, 0])
    row_g = lax.broadcasted_iota(jnp.int32, (S5_CHUNK, CH_P), 0)
    col_g = lax.broadcasted_iota(jnp.int32, (S5_CHUNK, CH_P), 1) // S5_P

    def lane_row(a):
        flat = jnp.sum(jnp.where(row_g == col_g, exact(a, spread), 0.0), axis=0, keepdims=True)
        return jnp.broadcast_to(flat, (SUB, CH_P))

    a_ref[0, 0] = jnp.concatenate([lane_row(ab_re), lane_row(ab_im)], axis=1)


def _s5_params(lam_re, lam_im, log_dt, b_re, b_im, c_re, c_im):
    nc = N_S5_CHUNKS
    ldt = jnp.broadcast_to(log_dt[:, :, None], (2, N_S5_GROUPS, S5_P))
    g3 = lambda a: a.reshape(2, nc, S5_CHUNK, 1, S5_P)
    g2 = lambda a: a.reshape(2, nc, S5_CHUNK, S5_P)
    ghp = lambda a: a.reshape(2, nc, S5_CHUNK, S5_H, S5_P)
    spec3 = pl.BlockSpec((1, 1, S5_CHUNK, 1, S5_P), lambda d, c: (d, c, 0, 0, 0))
    spec2 = pl.BlockSpec((1, 1, S5_CHUNK, S5_P), lambda d, c: (d, c, 0, 0))
    spec_ghp = pl.BlockSpec((1, 1, S5_CHUNK, S5_H, S5_P), lambda d, c: (d, c, 0, 0, 0))
    return pl.pallas_call(
        _s5_params_kernel,
        grid=(2, nc),
        in_specs=[spec3, spec3, spec3, spec2, spec2, spec2, spec_ghp, spec_ghp, spec_ghp, spec_ghp],
        out_specs=[pl.BlockSpec((1, 1, CH_U, 2 * CH_P), lambda d, c: (d, c, 0, 0)),
                   pl.BlockSpec((1, 1, 2 * CH_P, CH_U), lambda d, c: (d, c, 0, 0)),
                   pl.BlockSpec((1, 1, SUB, 2 * CH_P), lambda d, c: (d, c, 0, 0))],
        out_shape=[jax.ShapeDtypeStruct((2, nc, CH_U, 2 * CH_P), BF16),
                   jax.ShapeDtypeStruct((2, nc, 2 * CH_P, CH_U), BF16),
                   jax.ShapeDtypeStruct((2, nc, SUB, 2 * CH_P), F32)],
        compiler_params=_cparams(("arbitrary", "arbitrary")),
        name="s5_params",
    )(g3(lam_re), g3(lam_im), g3(ldt), g2(lam_re), g2(lam_im), g2(ldt),
      ghp(jnp.swapaxes(b_re, -1, -2)), ghp(jnp.swapaxes(b_im, -1, -2)), ghp(c_re), ghp(c_im))


def _rms(x, g):
    return x * lax.rsqrt(jnp.mean(x * x, axis=-1, keepdims=True) + RMS_EPS) * g


def _load_x(g, xp_ref, xs_ref, pos_ref):
    is_lat = g == N_GRP - 1
    pos = pos_ref[...]
    pos8 = jnp.concatenate([pos, pos], axis=0)
    return jnp.where(is_lat, xs_ref[0] + pos8, xp_ref[0])


def _x_specs():
    n_ctx = N_GRP - 1
    xp_spec = pl.BlockSpec(
        (1, SUB, TC, D),
        lambda g, tc: (jnp.minimum(g, n_ctx - 1), 0, jnp.where(g >= n_ctx, N_TC - 1, tc), 0))
    xs_spec = pl.BlockSpec((1, SUB, TC, D), lambda g, tc: (0, 0, jnp.where(g >= n_ctx, tc, 0), 0))
    pos_spec = pl.BlockSpec((LAT_CHUNKS, TC, D), lambda g, tc: (0, tc, 0))
    return xp_spec, xs_spec, pos_spec


def _perm_matrix():
    p = np.zeros((SUB * PERM_T, SUB * PERM_T), np.float32)
    for s in range(SUB):
        for j in range(PERM_T):
            p[j * SUB + s, s * PERM_T + j] = 1.0
    return p


def _pre_kernel(xp_ref, xs_ref, pos_ref, mod_ref, g1_ref, win_ref, perm_ref, us_ref, uf_ref):
    g = pl.program_id(0)
    x = _load_x(g, xp_ref, xs_ref, pos_ref)
    mod = mod_ref[0]
    shift1 = mod[:, :, 0:D]
    scale1 = mod[:, :, D:2 * D]
    h = _rms(x, g1_ref[...]) * (1.0 + scale1) + shift1
    h2d = h.reshape(SUB * TC, D).astype(BF16)
    proj = jnp.dot(h2d, win_ref[...].astype(BF16), preferred_element_type=F32)
    uf_ref[0] = proj[:, D_S5:].astype(BF16).reshape(SUB, TC, D_FN)
    u = proj[:, :D_S5].astype(BF16)
    perm = perm_ref[...]
    for q in range(TC // PERM_T):
        piece = jnp.concatenate(
            [u[s * TC + q * PERM_T: s * TC + (q + 1) * PERM_T] for s in range(SUB)], axis=0)
        us_ref[0, q * SUB * PERM_T:(q + 1) * SUB * PERM_T, :] = jnp.dot(
            perm, piece, preferred_element_type=F32).astype(BF16)


def _pre(xp4, xs4, pos3, modg, norm1_g, w_in, perm):
    xp_spec, xs_spec, pos_spec = _x_specs()
    return pl.pallas_call(
        _pre_kernel,
        grid=(N_GRP, N_TC),
        in_specs=[xp_spec, xs_spec, pos_spec,
                  pl.BlockSpec((1, SUB, 1, 6 * D), lambda g, tc: (g, 0, 0, 0)),
                  pl.BlockSpec((1, D), lambda g, tc: (0, 0)),
                  pl.BlockSpec((D, D), lambda g, tc: (0, 0)),
                  pl.BlockSpec((SUB * PERM_T, SUB * PERM_T), lambda g, tc: (0, 0))],
        out_specs=[pl.BlockSpec((1, SUB * TC, D_S5), lambda g, tc: (g, tc, 0)),
                   pl.BlockSpec((1, SUB, TC, D_FN), lambda g, tc: (g, 0, tc, 0))],
        out_shape=[jax.ShapeDtypeStruct((N_GRP, GRP_ROWS, D_S5), BF16),
                   jax.ShapeDtypeStruct((N_GRP, SUB, L_BLK, D_FN), BF16)],
        compiler_params=_cparams(("arbitrary", "arbitrary")),
        name="pre_mixer",
    )(xp4, xs4, pos3, modg, norm1_g.reshape(1, D), w_in, perm)


def _cmul(ar, ai, br, bi):
    return ar * br - ai * bi, ar * bi + ai * br


S5_BLK = 256
S5_NBLK = GRP_ROWS // S5_BLK
S5_STEPS = S5_BLK // SUB
S5_FIX_UNROLL = 8


def _s5_kernel(us_ref, bb_ref, cc_ref, a_ref, init_ref, dskip_ref, y_ref, fin_ref, in_a, in_b, out_a, out_b):
    g = pl.program_id(0)
    d = pl.program_id(2)
    is_lat = g == N_GRP - 1
    re_all = slice(0, CH_P)
    im_all = slice(CH_P, 2 * CH_P)
    a_re = a_ref[0, 0, :, re_all]
    a_im = a_ref[0, 0, :, im_all]
    bufs_in = (in_a, in_b)
    bufs_out = (out_a, out_b)

    def time_rows(k):
        blk = k + d * (S5_NBLK - 1 - 2 * k)
        return pl.ds(pl.multiple_of(blk * S5_BLK, S5_BLK), S5_BLK)

    def slot_rows(k):
        return slice((k // 2) * S5_BLK, (k // 2 + 1) * S5_BLK)

    def step_rows(k, j):
        q = j + d * (S5_STEPS - 1 - 2 * j)
        return pl.ds(pl.multiple_of((k // 2) * S5_BLK + q * SUB, SUB), SUB)

    def proj_in(k):
        bufs_in[k % 2][slot_rows(k), :] = jnp.dot(us_ref[0, time_rows(k), :], bb_ref[0, 0],
                                                  preferred_element_type=F32)

    def scan_block(k, carry):
        src, dst = bufs_in[k % 2], bufs_out[k % 2]
        s_re, s_im = carry
        for j in range(S5_STEPS):
            rows = step_rows(k, j)
            n_re = a_re * s_re - a_im * s_im + src[rows, re_all]
            n_im = a_re * s_im + a_im * s_re + src[rows, im_all]
            dst[rows, re_all] = n_re
            dst[rows, im_all] = n_im
            s_re, s_im = n_re, n_im
        return s_re, s_im

    def proj_out(k):
        rows = time_rows(k)
        yb = jnp.dot(bufs_out[k % 2][slot_rows(k), :].astype(BF16), cc_ref[0, 0], preferred_element_type=F32)
        y_ref[0, rows, :] = y_ref[0, rows, :] + yb

    @pl.when(d == 0)
    def _():
        y_ref[0] = dskip_ref[...] * us_ref[0].astype(F32)

    def scan_all(carry, with_out):
        proj_in(0)
        for k in range(S5_NBLK):
            if k + 1 < S5_NBLK:
                proj_in(k + 1)
            carry = scan_block(k, carry)
            if with_out and k >= 1:
                proj_out(k - 1)
        if with_out:
            proj_out(S5_NBLK - 1)
        fin_ref[0, 0, :, re_all] = carry[0]
        fin_ref[0, 0, :, im_all] = carry[1]
        return carry

    @pl.when(jnp.logical_not(is_lat))
    def _():
        zero = jnp.zeros((SUB, CH_P), F32)
        scan_all((zero, zero), True)

    @pl.when(is_lat)
    def _():
        f_re, f_im = scan_all((init_ref[0, 0, :, re_all], init_ref[0, 0, :, im_all]), False)

        sub_id = lax.broadcasted_iota(jnp.int32, (SUB, CH_P), 0) % LAT_CHUNKS
        fwd = d == 0
        lo = jnp.where(fwd, 1, 0)
        hi = jnp.where(fwd, LAT_CHUNKS - 1, LAT_CHUNKS - 2)
        keep = (sub_id >= lo) & (sub_id <= hi)

        def from_prev(v):
            return jnp.where(keep, jnp.where(fwd, pltpu.roll(v, 1, 0), pltpu.roll(v, SUB - 1, 0)), 0.0)

        p_re, p_im = a_re, a_im
        for _ in range(8):
            p_re, p_im = _cmul(p_re, p_im, p_re, p_im)
        t_re, t_im = f_re, f_im
        for _ in range(LAT_CHUNKS - 2):
            m_re, m_im = _cmul(p_re, p_im, from_prev(t_re), from_prev(t_im))
            t_re, t_im = f_re + m_re, f_im + m_im
        corr = (from_prev(t_re), from_prev(t_im))

        for k in range(S5_NBLK):
            dst = bufs_out[k % 2]

            def fix_body(i, carry, k=k, dst=dst):
                m_re, m_im = carry
                for j in range(S5_FIX_UNROLL):
                    rows = step_rows(k, i * S5_FIX_UNROLL + j)
                    m_re, m_im = _cmul(m_re, m_im, a_re, a_im)
                    dst[rows, re_all] = dst[rows, re_all] + m_re
                    dst[rows, im_all] = dst[rows, im_all] + m_im
                return m_re, m_im

            corr = lax.fori_loop(0, S5_STEPS // S5_FIX_UNROLL, fix_body, corr)

        for k in range(S5_NBLK):
            proj_out(k)


def _s5(us, bbmat, ccmat, a8, init8, d_skip):
    return pl.pallas_call(
        _s5_kernel,
        grid=(N_GRP, N_S5_CHUNKS, 2),
        in_specs=[pl.BlockSpec((1, GRP_ROWS, CH_U), lambda g, c, d: (g, 0, c)),
                  pl.BlockSpec((1, 1, CH_U, 2 * CH_P), lambda g, c, d: (d, c, 0, 0)),
                  pl.BlockSpec((1, 1, 2 * CH_P, CH_U), lambda g, c, d: (d, c, 0, 0)),
                  pl.BlockSpec((1, 1, SUB, 2 * CH_P), lambda g, c, d: (d, c, 0, 0)),
                  pl.BlockSpec((1, 1, SUB, 2 * CH_P), lambda g, c, d: (d, c, 0, 0)),
                  pl.BlockSpec((1, CH_U), lambda g, c, d: (0, c))],
        out_specs=[pl.BlockSpec((1, GRP_ROWS, CH_U), lambda g, c, d: (g, 0, c)),
                   pl.BlockSpec((1, 1, SUB, 2 * CH_P), lambda g, c, d: (d, g, 0, c))],
        out_shape=[jax.ShapeDtypeStruct((N_GRP, GRP_ROWS, D_S5), F32),
                   jax.ShapeDtypeStruct((2, N_GRP, SUB, N_S5_CHUNKS * 2 * CH_P), F32)],
        scratch_shapes=[pltpu.VMEM((GRP_ROWS // 2, 2 * CH_P), F32)] * 4,
        compiler_params=_cparams(("arbitrary", "arbitrary", "arbitrary")),
        name="s5_scan",
    )(us, bbmat, ccmat, a8, init8, d_skip.reshape(1, D_S5))


def _fnet_w_kernel(c_ref, s_ref, w_ref, m1_ref, m2_ref):
    w = w_ref[...]
    m1_ref[...] = jnp.dot(c_ref[...], w, preferred_element_type=F32, precision=HIGHEST).astype(BF16)
    m2_ref[...] = jnp.dot(s_ref[...], w, preferred_element_type=F32, precision=HIGHEST).astype(BF16)


FN_SEQ = LAT_CHUNKS
FN_CTX_STEPS = (N_GRP - 1) * SUB // FN_SEQ


def _fnet_kernel(u_ref, cos_s_ref, sin_s_ref, cos_l_ref, sin_l_ref, m1_ref, m2_ref, o_ref):
    i = pl.program_id(0)
    m1 = m1_ref[...]
    m2 = m2_ref[...]

    def mix(u, cos_ref, sin_ref):
        v1 = jnp.dot(u, m1, preferred_element_type=F32).astype(BF16)
        v2 = jnp.dot(u, m2, preferred_element_type=F32).astype(BF16)
        return (jnp.dot(cos_ref[...].astype(BF16), v1, preferred_element_type=F32)
                - jnp.dot(sin_ref[...].astype(BF16), v2, preferred_element_type=F32)).astype(BF16)

    @pl.when(i < FN_CTX_STEPS)
    def _():
        for s in range(FN_SEQ):
            o_ref[0, s] = mix(u_ref[0, s], cos_s_ref, sin_s_ref)

    @pl.when(i >= FN_CTX_STEPS)
    def _():
        u = u_ref[0].reshape(FN_SEQ * L_BLK, D_FN)
        o_ref[0] = mix(u, cos_l_ref, sin_l_ref).reshape(FN_SEQ, L_BLK, D_FN)


def _dft_tables(n):
    k = np.arange(n, dtype=np.int64)
    ang = (2.0 * np.pi / n) * ((k[:, None] * k[None, :]) % n).astype(np.float64)
    scale = 1.0 / math.sqrt(n)
    return (np.cos(ang) * scale).astype(np.float32), (np.sin(ang) * scale).astype(np.float32)


def _fnet(uf, m1, m2):
    n_lat_steps = SUB // FN_SEQ
    cos_s, sin_s = _dft_tables(L_BLK)
    cos_l, sin_l = _dft_tables(FN_SEQ * L_BLK)
    full = lambda i: (0, 0)
    per_grp = SUB // FN_SEQ
    blk = pl.BlockSpec(
        (1, FN_SEQ, L_BLK, D_FN),
        lambda i: (jnp.minimum(i // per_grp, N_GRP - 1),
                   jnp.where(i < FN_CTX_STEPS, i % per_grp, i - FN_CTX_STEPS), 0, 0))
    table = lambda n: pl.BlockSpec((n, n), full)
    return pl.pallas_call(
        _fnet_kernel,
        grid=(FN_CTX_STEPS + n_lat_steps,),
        in_specs=[blk, table(L_BLK), table(L_BLK), table(FN_SEQ * L_BLK), table(FN_SEQ * L_BLK),
                  table(D_FN), table(D_FN)],
        out_specs=blk,
        out_shape=jax.ShapeDtypeStruct(uf.shape, BF16),
        compiler_params=_cparams(("arbitrary",)),
        name="fnet",
    )(uf, jnp.asarray(cos_s), jnp.asarray(sin_s), jnp.asarray(cos_l), jnp.asarray(sin_l), m1, m2)


def _fnet_weights(w_fnet):
    n_g = D_FN // FN_GW
    cos_c, sin_c = _dft_tables(FN_GW)
    eye = np.eye(n_g, dtype=np.float32)
    cos_bd = np.kron(eye, cos_c)
    sin_bd = np.kron(eye, sin_c)
    w_bd = (w_fnet[:, :, None, :] * jnp.asarray(eye)[:, None, :, None]).reshape(D_FN, D_FN)
    out = jax.ShapeDtypeStruct((D_FN, D_FN), BF16)
    return pl.pallas_call(_fnet_w_kernel, out_shape=(out, out), name="fnet_weights")(
        jnp.asarray(cos_bd), jnp.asarray(sin_bd), w_bd)


def _gelu_tanh(x):
    return 0.5 * x * (1.0 + jnp.tanh(math.sqrt(2.0 / math.pi) * (x + 0.044715 * (x * x * x))))


def _post_kernel(y_ref, yf_ref, xp_ref, xs_ref, pos_ref, mod_ref, wglu_ref, wout_ref, g2_ref, wr_ref,
                 br_ref, permt_ref, x1_ref, h2_ref, comb_ref):
    g = pl.program_id(0)
    z = _gelu_tanh(y_ref[0])
    gate = jnp.dot(z.astype(BF16), wglu_ref[...].astype(BF16), preferred_element_type=F32)
    gl = (z * jax.nn.sigmoid(gate)).astype(BF16)
    permt = permt_ref[...]
    n_q = TC // PERM_T
    nat = [jnp.dot(permt, gl[q * SUB * PERM_T:(q + 1) * SUB * PERM_T], preferred_element_type=F32).astype(BF16)
           for q in range(n_q)]
    gl_nat = jnp.concatenate(
        [nat[q][s * PERM_T:(s + 1) * PERM_T] for s in range(SUB) for q in range(n_q)], axis=0)
    w_out = wout_ref[...].astype(BF16)
    mixed = (jnp.dot(gl_nat, w_out[:D_S5], preferred_element_type=F32)
             + jnp.dot(yf_ref[0].reshape(SUB * TC, D_FN), w_out[D_S5:], preferred_element_type=F32))
    x = _load_x(g, xp_ref, xs_ref, pos_ref)
    mod = mod_ref[0]
    gate1 = mod[:, :, 2 * D:3 * D]
    shift2 = mod[:, :, 3 * D:4 * D]
    scale2 = mod[:, :, 4 * D:5 * D]
    x1 = x + gate1 * mixed.reshape(SUB, TC, D)
    x1_ref[0] = x1
    h2 = _rms(x1, g2_ref[...]) * (1.0 + scale2) + shift2
    h2_ref[0] = h2.astype(BF16)

    hr = h2.reshape(SUB * TC, D)
    h_hi = hr.astype(BF16)
    h_lo = (hr - h_hi.astype(F32)).astype(BF16)
    wr = wr_ref[...]
    w_hi = wr.astype(BF16)
    w_lo = (wr - w_hi.astype(F32)).astype(BF16)
    both = jnp.dot(h_hi, jnp.concatenate([w_hi, w_lo], axis=1), preferred_element_type=F32)
    logits = (both[:, :E_PAD] + jnp.dot(h_lo, w_hi, preferred_element_type=F32) + both[:, E_PAD:]
              + br_ref[...])
    lane = lax.broadcasted_iota(jnp.int32, logits.shape, 1).astype(F32)
    top_v, hots = [], []
    cur = logits
    for _ in range(TOP_K):
        m = jnp.max(cur, axis=-1, keepdims=True)
        idx = jnp.min(jnp.where(cur == m, lane, float(E_PAD)), axis=-1, keepdims=True)
        hot = lane == idx
        top_v.append(m)
        hots.append(hot)
        cur = jnp.where(hot, -3.0e38, cur)
    exps = [jnp.exp(v - top_v[0]) for v in top_v]
    denom = exps[0] + exps[1] + exps[2] + exps[3]
    comb = jnp.zeros(logits.shape, F32)
    for k in range(TOP_K):
        comb = comb + jnp.where(hots[k], exps[k] / denom, 0.0)
    comb_ref[0] = comb.reshape(SUB, TC, E_PAD)


def _post(y, yf, xp4, xs4, pos3, modg, w_glu, w_out, norm2_g, w_router, b_router, permt):
    xp_spec, xs_spec, pos_spec = _x_specs()
    const2 = lambda g, tc: (0, 0)
    wr = jnp.zeros((D, E_PAD), F32).at[:, :N_EXPERTS].set(w_router)
    br = jnp.full((1, E_PAD), -1.0e30, F32).at[0, :N_EXPERTS].set(b_router)
    blk = lambda w: pl.BlockSpec((1, SUB, TC, w), lambda g, tc: (g, 0, tc, 0))
    return pl.pallas_call(
        _post_kernel,
        grid=(N_GRP, N_TC),
        in_specs=[pl.BlockSpec((1, SUB * TC, D_S5), lambda g, tc: (g, tc, 0)),
                  blk(D_FN), xp_spec, xs_spec, pos_spec,
                  pl.BlockSpec((1, SUB, 1, 6 * D), lambda g, tc: (g, 0, 0, 0)),
                  pl.BlockSpec((D_S5, D_S5), const2),
                  pl.BlockSpec((D, D), const2),
                  pl.BlockSpec((1, D), const2),
                  pl.BlockSpec((D, E_PAD), const2),
                  pl.BlockSpec((1, E_PAD), const2),
                  pl.BlockSpec((SUB * PERM_T, SUB * PERM_T), const2)],
        out_specs=[blk(D), blk(D), blk(E_PAD)],
        out_shape=[jax.ShapeDtypeStruct((N_GRP, SUB, L_BLK, D), F32),
                   jax.ShapeDtypeStruct((N_GRP, SUB, L_BLK, D), BF16),
                   jax.ShapeDtypeStruct((N_GRP, SUB, L_BLK, E_PAD), F32)],
        compiler_params=_cparams(("arbitrary", "arbitrary")),
        name="post_mixer",
    )(y, yf, xp4, xs4, pos3, modg, w_glu, w_out, norm2_g.reshape(1, D), wr, br, permt)


TBLK = L_BLK
N_BLK = T_TOK // TBLK
N_CTX_BLK = (N_GRP - 1) * SUB
SEG_ALIGN = 8
RB = 1280
assert RB >= TBLK * TOP_K + N_EXPERTS * (SEG_ALIGN - 1) and RB % 128 == 0
TM = 256
TM_SHIFT = 8
assert 1 << TM_SHIFT == TM
R_TOT = T_TOK * TOP_K + N_BLK * N_EXPERTS * (SEG_ALIGN - 1) + TM
TAB_ROWS = 32
assert TAB_ROWS >= N_BLK
BIG_ROWS = 32
BIG_SHIFT = 2
assert SEG_ALIGN << BIG_SHIFT == BIG_ROWS


def _plan_kernel(comb_ref, z_ref, ptab_ref, loff_ref, loff_i_ref, gst_i_ref, p8_i_ref, eoff_i_ref):
    row = lax.broadcasted_iota(jnp.int32, (TBLK, TBLK), 0)
    col = lax.broadcasted_iota(jnp.int32, (TBLK, TBLK), 1)
    earlier = jnp.where(row > col, 1.0, 0.0).astype(BF16)
    ptab_ref[...] = jnp.zeros_like(ptab_ref)

    def body(b, _):
        rows = pl.ds(pl.multiple_of(b * TBLK, TBLK), TBLK)
        m = jnp.where(comb_ref[rows, :] > 0.0, 1.0, 0.0)
        rank = jnp.dot(earlier, m.astype(BF16), preferred_element_type=F32)
        z_ref[rows, :] = m * (rank + 1.0)
        n = jnp.sum(m, axis=0, keepdims=True)
        ptab_ref[pl.ds(b, 1), :] = jnp.floor((n + (SEG_ALIGN - 1)) * (1.0 / SEG_ALIGN)) * SEG_ALIGN
        return 0

    lax.fori_loop(0, N_BLK, body, 0)

    ptab = ptab_ref[...]
    er = lax.broadcasted_iota(jnp.int32, (E_PAD, E_PAD), 0)
    ec = lax.broadcasted_iota(jnp.int32, (E_PAD, E_PAD), 1)
    before = jnp.where(er < ec, 1.0, 0.0)
    exact = functools.partial(jnp.dot, preferred_element_type=F32, precision=HIGHEST)
    loff = exact(ptab, before)
    loff_ref[...] = loff
    as_int = lambda t: t[:, :N_EXPERTS].astype(jnp.int32)
    loff_i_ref[...] = as_int(loff)
    p8_i_ref[...] = as_int(ptab * (1.0 / SEG_ALIGN))
    tot = jnp.sum(ptab, axis=0, keepdims=True)
    eoff = exact(jnp.broadcast_to(tot, (SUB, E_PAD)), before)
    eoff_i_ref[...] = eoff.astype(jnp.int32)
    br = lax.broadcasted_iota(jnp.int32, (TAB_ROWS, TAB_ROWS), 0)
    bc = lax.broadcasted_iota(jnp.int32, (TAB_ROWS, TAB_ROWS), 1)
    gst_i_ref[...] = as_int(eoff[0:1] + exact(jnp.where(br > bc, 1.0, 0.0), ptab))


def _plan(comb):
    tab = jax.ShapeDtypeStruct((TAB_ROWS, E_PAD), F32)
    itab = jax.ShapeDtypeStruct((TAB_ROWS, N_EXPERTS), jnp.int32)
    return pl.pallas_call(
        _plan_kernel,
        out_shape=(jax.ShapeDtypeStruct((T_TOK, E_PAD), F32), tab, tab, itab, itab, itab,
                   jax.ShapeDtypeStruct((SUB, E_PAD), jnp.int32)),
        compiler_params=pltpu.CompilerParams(vmem_limit_bytes=VMEM_LIMIT),
        name="moe_plan",
    )(comb)


def _sort_matrix(b, z_ref, loff_ref, ptab_ref):
    loff = loff_ref[pl.ds(b, 1), :]
    size = ptab_ref[pl.ds(b, 1), :]
    r = lax.broadcasted_iota(jnp.int32, (RB, E_PAD), 0).astype(F32)
    owner = jnp.where(r >= loff, jnp.where(r < loff + size, 1.0, 0.0), 0.0)
    rank1 = r[:, 0:1] - jnp.sum(owner * loff, axis=-1, keepdims=True) + 1.0
    zt = z_ref[...].T
    v = jnp.dot(owner.astype(BF16), zt.astype(BF16), preferred_element_type=F32)
    return jnp.where(v == rank1, 1.0, 0.0), owner


def _segment_copies(b, loff_s, gst_s, p8_s, make_copy):
    def per_expert(e, counts):
        chunks = p8_s[b, e]
        n_big = lax.shift_right_logical(chunks, BIG_SHIFT)
        n_small = chunks - n_big * (BIG_ROWS // SEG_ALIGN)
        local0 = loff_s[b, e]
        global0 = gst_s[b, e]

        def big(j, _):
            make_copy(pl.multiple_of(local0 + j * BIG_ROWS, SEG_ALIGN),
                      pl.multiple_of(global0 + j * BIG_ROWS, SEG_ALIGN), BIG_ROWS).start()
            return 0

        def small(j, _):
            off = n_big * BIG_ROWS + j * SEG_ALIGN
            make_copy(pl.multiple_of(local0 + off, SEG_ALIGN),
                      pl.multiple_of(global0 + off, SEG_ALIGN), SEG_ALIGN).start()
            return 0

        lax.fori_loop(0, n_big, big, 0)
        lax.fori_loop(0, n_small, small, 0)
        return counts[0] + n_big, counts[1] + n_small

    return lax.fori_loop(0, N_EXPERTS, per_expert, (0, 0))


def _wait_copies(counts, make_copy):
    for n, rows in zip(counts, (BIG_ROWS, SEG_ALIGN)):
        def wait_one(i, _, rows=rows):
            make_copy(0, 0, rows).wait()
            return 0
        lax.fori_loop(0, n, wait_one, 0)


def _zero_rows_from(hbm, zeros_vmem, first_row, sem):
    def copy_to(row):
        return pltpu.make_async_copy(zeros_vmem, hbm.at[pl.ds(row, TM)], sem)

    n_full = lax.shift_right_logical(R_TOT - first_row, TM_SHIFT)

    def start_one(j, _):
        copy_to(pl.multiple_of(first_row + j * TM, SEG_ALIGN)).start()
        return 0

    def wait_one(j, _):
        copy_to(0).wait()
        return 0

    lax.fori_loop(0, n_full, start_one, 0)
    lax.fori_loop(0, n_full, wait_one, 0)
    last = copy_to(R_TOT - TM)
    last.start()
    last.wait()


def _dispatch_kernel(loff_s, gst_s, p8_s, eoff_s, h_ref, z_ref, loff_ref, ptab_ref, xs_hbm, xbuf, sem, pending):
    b = pl.program_id(0)

    def make_copy(local_row, global_row, rows):
        return pltpu.make_async_copy(xbuf.at[pl.ds(local_row, rows)], xs_hbm.at[pl.ds(global_row, rows)], sem)

    pm = _sort_matrix(b, z_ref, loff_ref, ptab_ref)[0].astype(BF16)

    @pl.when(b > 0)
    def _():
        _wait_copies((pending[0], pending[1]), make_copy)

    xbuf[...] = jnp.dot(pm, h_ref[...], preferred_element_type=F32)
    n_big, n_small = _segment_copies(b, loff_s, gst_s, p8_s, make_copy)
    pending[0] = n_big
    pending[1] = n_small

    @pl.when(b == N_BLK - 1)
    def _():
        _wait_copies((n_big, n_small), make_copy)
        xbuf[0:TM, :] = jnp.zeros((TM, D), F32)
        _zero_rows_from(xs_hbm, xbuf.at[0:TM], eoff_s[0, N_EXPERTS], sem)


def _dispatch(h2, z, loff, ptab, loff_i, gst_i, p8_i, eoff_i):
    whole = pl.BlockSpec((TAB_ROWS, E_PAD), lambda b, *_: (0, 0))
    return pl.pallas_call(
        _dispatch_kernel,
        grid_spec=pltpu.PrefetchScalarGridSpec(
            num_scalar_prefetch=4,
            grid=(N_BLK,),
            in_specs=[pl.BlockSpec((TBLK, D), lambda b, *_: (b, 0)),
                      pl.BlockSpec((TBLK, E_PAD), lambda b, *_: (b, 0)),
                      whole, whole],
            out_specs=pl.BlockSpec(memory_space=pl.ANY),
            scratch_shapes=[pltpu.VMEM((RB, D), F32), pltpu.SemaphoreType.DMA(()),
                            pltpu.SMEM((2,), jnp.int32)]),
        out_shape=jax.ShapeDtypeStruct((R_TOT, D), F32),
        compiler_params=_cparams(("arbitrary",)),
        name="moe_dispatch",
    )(loff_i, gst_i, p8_i, eoff_i, h2, z, loff, ptab)


MAX_TILES = R_TOT // TM + N_EXPERTS
X_AHEAD = 3
X_SLOTS = X_AHEAD + 1
Y_SLOTS = 2
W_AHEAD = 2
W_SLOTS = W_AHEAD + 1
N_MATS = 3
FF_CHUNK = 256
CAST_ROWS = 64
TILE_STEP = 64


def _expert_kernel(eoff_s, xs_hbm, wg_hbm, wu_hbm, wd_hbm, bg_ref, bu_ref, bd_ref, ys_hbm,
                   wst, wbf, xin, yout, act_ref, w_sem, x_sem, y_sem, t_exp, t_row, t_first, t_valid, live):
    def add_expert(e, carry):
        n_t, n_live = carry
        start = eoff_s[0, e]
        count = eoff_s[0, e + 1] - start
        tiles = lax.shift_right_logical(count + (TM - 1), TM_SHIFT)

        def add_tile(i, _):
            t_exp[n_t + i] = e
            t_row[n_t + i] = start + i * TM
            t_first[n_t + i] = jnp.where(i == 0, 1, 0)
            t_valid[n_t + i] = jnp.minimum(count - i * TM, TM)
            return 0

        lax.fori_loop(0, tiles, add_tile, 0)
        has_rows = jnp.where(tiles > 0, 1, 0)

        @pl.when(tiles > 0)
        def _():
            live[n_live] = e

        return n_t + tiles, n_live + has_rows

    n_tiles, n_live = lax.fori_loop(0, N_EXPERTS, add_expert, (0, 0))

    def w_copies(e, slot):
        return [pltpu.make_async_copy(w.at[e], wst.at[slot, m], w_sem.at[slot])
                for m, w in enumerate((wg_hbm, wu_hbm, wd_hbm))]

    def tile_rows(j):
        return pl.ds(pl.multiple_of(t_row[j], SEG_ALIGN), TM)

    def x_copy(j, slot):
        return pltpu.make_async_copy(xs_hbm.at[tile_rows(j)], xin.at[slot], x_sem.at[slot])

    def y_pieces(j, op):
        ys = lax.rem(j, Y_SLOTS)
        valid = t_valid[j]
        row0 = t_row[j]

        def piece(off, rows):
            return pltpu.make_async_copy(yout.at[ys, pl.ds(off, rows)],
                                         ys_hbm.at[pl.ds(pl.multiple_of(row0 + off, SEG_ALIGN), rows)],
                                         y_sem.at[ys])

        @pl.when(valid == TM)
        def _():
            op(piece(0, TM))

        @pl.when(valid < TM)
        def _():
            n_big = lax.shift_right_logical(valid, BIG_ROWS.bit_length() - 1)
            n_small = lax.shift_right_logical(valid - n_big * BIG_ROWS, SEG_ALIGN.bit_length() - 1)

            def big(i, _):
                op(piece(pl.multiple_of(i * BIG_ROWS, SEG_ALIGN), BIG_ROWS))
                return 0

            def small(i, _):
                op(piece(pl.multiple_of(n_big * BIG_ROWS + i * SEG_ALIGN, SEG_ALIGN), SEG_ALIGN))
                return 0

            lax.fori_loop(0, n_big, big, 0)
            lax.fori_loop(0, n_small, small, 0)

    for ahead in range(W_AHEAD):
        @pl.when(n_live > ahead)
        def _(ahead=ahead):
            for cp in w_copies(live[ahead], ahead):
                cp.start()

    for ahead in range(X_AHEAD):
        @pl.when(n_tiles > ahead)
        def _(ahead=ahead):
            x_copy(ahead, ahead).start()

    def body(j, k):
        e = t_exp[j]

        @pl.when(t_first[j] == 1)
        def _():
            ws = lax.rem(k, W_SLOTS)

            @pl.when(k + W_AHEAD < n_live)
            def _():
                for cp in w_copies(live[k + W_AHEAD], lax.rem(k + W_AHEAD, W_SLOTS)):
                    cp.start()

            for cp in w_copies(e, ws):
                cp.wait()

            def cast_rows(r, _):
                rows = pl.ds(pl.multiple_of(r * CAST_ROWS, CAST_ROWS), CAST_ROWS)
                for m in range(N_MATS):
                    wbf[m, rows, :] = wst[ws, m, rows, :].astype(BF16)
                return 0

            lax.fori_loop(0, D // CAST_ROWS, cast_rows, 0)

        slot = lax.rem(j, X_SLOTS)
        x_copy(j, slot).wait()

        @pl.when(j + X_AHEAD < n_tiles)
        def _():
            x_copy(j + X_AHEAD, lax.rem(j + X_AHEAD, X_SLOTS)).start()

        @pl.when(j >= Y_SLOTS)
        def _():
            y_pieces(j - Y_SLOTS, lambda cp: cp.wait())

        ys = lax.rem(j, Y_SLOTS)

        def compute(rows):
            x = xin[slot, 0:rows, :].astype(BF16)
            for c in range(D // FF_CHUNK):
                cols = slice(c * FF_CHUNK, (c + 1) * FF_CHUNK)
                gate = jnp.dot(x, wbf[0, :, cols], preferred_element_type=F32) + bg_ref[e][:, cols]
                up = jnp.dot(x, wbf[1, :, cols], preferred_element_type=F32) + bu_ref[e][:, cols]
                gate = jnp.minimum(gate, SWIGLU_LIMIT)
                up = jnp.clip(up, -SWIGLU_LIMIT, SWIGLU_LIMIT)
                act_ref[0:rows, cols] = ((up + 1.0) * gate * jax.nn.sigmoid(SWIGLU_ALPHA * gate)).astype(BF16)
            yout[ys, 0:rows, :] = jnp.dot(act_ref[0:rows, :], wbf[2], preferred_element_type=F32) + bd_ref[e]

        valid = t_valid[j]
        for rows in range(TILE_STEP, TM + 1, TILE_STEP):
            @pl.when(jnp.logical_and(valid > rows - TILE_STEP, valid <= rows))
            def _(rows=rows):
                compute(rows)

        y_pieces(j, lambda cp: cp.start())
        return k + t_first[j]

    lax.fori_loop(0, n_tiles, body, 0)

    for back in range(Y_SLOTS, 0, -1):
        @pl.when(n_tiles >= back)
        def _(back=back):
            y_pieces(n_tiles - back, lambda cp: cp.wait())

    yout[0] = jnp.zeros((TM, D), F32)
    _zero_rows_from(ys_hbm, yout.at[0], eoff_s[0, N_EXPERTS], y_sem.at[0])


def _experts(xs, eoff_i, w_gate, b_gate, w_up, b_up, w_down, b_down):
    hbm = pl.BlockSpec(memory_space=pl.ANY)
    bspec = pl.BlockSpec((N_EXPERTS, 1, D), lambda i, *_: (0, 0, 0))
    return pl.pallas_call(
        _expert_kernel,
        grid_spec=pltpu.PrefetchScalarGridSpec(
            num_scalar_prefetch=1,
            grid=(1,),
            in_specs=[hbm, hbm, hbm, hbm, bspec, bspec, bspec],
            out_specs=hbm,
            scratch_shapes=[pltpu.VMEM((W_SLOTS, N_MATS, D, D), F32), pltpu.VMEM((N_MATS, D, D), BF16),
                            pltpu.VMEM((X_SLOTS, TM, D), F32), pltpu.VMEM((Y_SLOTS, TM, D), F32),
                            pltpu.VMEM((TM, D), BF16),
                            pltpu.SemaphoreType.DMA((W_SLOTS,)), pltpu.SemaphoreType.DMA((X_SLOTS,)),
                            pltpu.SemaphoreType.DMA((Y_SLOTS,)),
                            pltpu.SMEM((MAX_TILES,), jnp.int32), pltpu.SMEM((MAX_TILES,), jnp.int32),
                            pltpu.SMEM((MAX_TILES,), jnp.int32), pltpu.SMEM((MAX_TILES,), jnp.int32),
                            pltpu.SMEM((N_EXPERTS,), jnp.int32)]),
        out_shape=jax.ShapeDtypeStruct((R_TOT, D), F32),
        compiler_params=_cparams(("arbitrary",)),
        name="moe_experts",
    )(eoff_i, xs, w_gate, w_up, w_down, b_gate.reshape(N_EXPERTS, 1, D), b_up.reshape(N_EXPERTS, 1, D),
      b_down.reshape(N_EXPERTS, 1, D))


def _combine_kernel(loff_s, gst_s, p8_s, ys_hbm, z_ref, comb_ref, loff_ref, ptab_ref, x1_ref, mod_ref, gf_ref,
                    yctx_ref, ylat_ref, ybuf, sem):
    b = pl.program_id(0)

    def make_copy(local_row, global_row, rows):
        return pltpu.make_async_copy(ys_hbm.at[pl.ds(global_row, rows)], ybuf.at[pl.ds(local_row, rows)], sem)

    started = _segment_copies(b, loff_s, gst_s, p8_s, make_copy)

    used_chunks = lax.shift_right_logical(loff_s[b, N_EXPERTS - 1], 3) + p8_s[b, N_EXPERTS - 1]

    def zero_chunk(j, _):
        ybuf[pl.ds(pl.multiple_of(j * SEG_ALIGN, SEG_ALIGN), SEG_ALIGN), :] = jnp.zeros((SEG_ALIGN, D), F32)
        return 0

    lax.fori_loop(used_chunks, RB // SEG_ALIGN, zero_chunk, 0)

    pm, owner = _sort_matrix(b, z_ref, loff_ref, ptab_ref)
    comb = comb_ref[...]
    c_hi = comb.astype(BF16)
    rest = comb - c_hi.astype(F32)
    c_mid = rest.astype(BF16)
    c_lo = (rest - c_mid.astype(F32)).astype(BF16)
    pmb = pm.astype(BF16)
    moved = (jnp.dot(pmb, c_hi, preferred_element_type=F32) + jnp.dot(pmb, c_mid, preferred_element_type=F32)
             + jnp.dot(pmb, c_lo, preferred_element_type=F32))
    w_row = jnp.sum(owner * moved, axis=-1, keepdims=True)
    pmt = pm.T.astype(BF16)
    _wait_copies(started, make_copy)
    y = (ybuf[...] * w_row).astype(BF16)
    moe = jnp.dot(pmt, y, preferred_element_type=F32)
    gate2 = mod_ref[0][:, 5 * D:6 * D]
    x2 = x1_ref[0] + gate2 * moe
    y = _rms(x2, gf_ref[...])

    @pl.when(b < N_CTX_BLK)
    def _():
        yctx_ref[0] = y

    @pl.when(b >= N_CTX_BLK)
    def _():
        ylat_ref[0] = y


def _combine(ys, z, comb, loff, ptab, x1, modv, norm_f_g, loff_i, gst_i, p8_i):
    whole = pl.BlockSpec((TAB_ROWS, E_PAD), lambda b, *_: (0, 0))
    tok = pl.BlockSpec((TBLK, E_PAD), lambda b, *_: (b, 0))
    return pl.pallas_call(
        _combine_kernel,
        grid_spec=pltpu.PrefetchScalarGridSpec(
            num_scalar_prefetch=3,
            grid=(N_BLK,),
            in_specs=[pl.BlockSpec(memory_space=pl.ANY), tok, tok, whole, whole,
                      pl.BlockSpec((1, TBLK, D), lambda b, *_: (b, 0, 0)),
                      pl.BlockSpec((1, 1, 6 * D), lambda b, *_: (b, 0, 0)),
                      pl.BlockSpec((1, D), lambda b, *_: (0, 0))],
            out_specs=[pl.BlockSpec((1, TBLK, D), lambda b, *_: (jnp.minimum(b, N_CTX_BLK - 1), 0, 0)),
                       pl.BlockSpec((1, TBLK, D), lambda b, *_: (jnp.maximum(b - N_CTX_BLK, 0), 0, 0))],
            scratch_shapes=[pltpu.VMEM((RB, D), F32), pltpu.SemaphoreType.DMA(())]),
        out_shape=[jax.ShapeDtypeStruct((N_CTX_BLK, TBLK, D), F32),
                   jax.ShapeDtypeStruct((N_BLK - N_CTX_BLK, TBLK, D), F32)],
        compiler_params=_cparams(("arbitrary",)),
        name="moe_combine",
    )(loff_i, gst_i, p8_i, ys, z, comb, loff, ptab, x1, modv, norm_f_g.reshape(1, D))


def _moe_and_final(x1, h2, comb, modg, norm_f_g, w_gate, b_gate, w_up, b_up, w_down, b_down):
    z, ptab, loff, loff_i, gst_i, p8_i, eoff_i = _plan(comb)
    xs = _dispatch(h2, z, loff, ptab, loff_i, gst_i, p8_i, eoff_i)
    ys = _experts(xs, eoff_i, w_gate, b_gate, w_up, b_up, w_down, b_down)
    return _combine(ys, z, comb, loff, ptab, x1, modg.reshape(N_BLK, 1, 6 * D), norm_f_g,
                    loff_i, gst_i, p8_i)


def _grid_pos_embed(n_tokens, dim):
    rows = n_tokens // GRID_W
    t = np.arange(rows * GRID_W)
    r = (t // GRID_W).astype(np.float32)
    col = (t % GRID_W).astype(np.float32)
    q = dim // 4
    omega = (1.0 / np.float32(POS_TEMP) ** (np.arange(q, dtype=np.float32) / np.float32(q))).astype(np.float32)

    def emb(p):
        a = p[:, None] * omega[None, :]
        return np.concatenate([np.sin(a), np.cos(a)], axis=-1)

    return np.concatenate([emb(r), emb(col)], axis=-1).astype(np.float32)


def kernel(x_prompt, x_sample, c, state_s5_re, state_s5_im, c_ctx, w_ada, b_ada, norm1_g, w_in, s5_lam_re,
           s5_lam_im, s5_log_dt, s5_b_re, s5_b_im, s5_c_re, s5_c_im, s5_d, s5_w_glu, w_fnet, w_out, norm2_g,
           w_router, b_router, w_gate, b_gate, w_up, b_up, w_down, b_down, norm_f_g):
    n_ctx, n_lat = x_prompt.shape[0], x_sample.shape[0]
    assert x_prompt.shape == (SUB * (N_GRP - 1), L_BLK, D) and x_sample.shape == (2, LAT_CHUNKS * L_BLK, D)
    assert w_ada.shape[0] == 1, "one trunk layer"
    layer = 0

    cvec = jnp.concatenate([jnp.broadcast_to(c_ctx[None], (n_ctx, D)), jnp.repeat(c, LAT_CHUNKS, axis=0)], axis=0)
    modg = _adaln(cvec, w_ada[layer], b_ada[layer]).reshape(N_GRP, SUB, 1, 6 * D)

    xp4 = x_prompt.reshape(N_GRP - 1, SUB, L_BLK, D)
    xs4 = x_sample.reshape(1, SUB, L_BLK, D)
    pos3 = jnp.asarray(_grid_pos_embed(LAT_CHUNKS * L_BLK, D).reshape(LAT_CHUNKS, L_BLK, D))
    perm = _perm_matrix()

    us, uf = _pre(xp4, xs4, pos3, modg, norm1_g[layer], w_in[layer], jnp.asarray(perm, BF16))

    bbmat, ccmat, a8 = _s5_params(s5_lam_re[layer], s5_lam_im[layer], s5_log_dt[layer], s5_b_re[layer],
                                  s5_b_im[layer], s5_c_re[layer], s5_c_im[layer])

    st = jnp.stack([state_s5_re[:, layer], state_s5_im[:, layer]], axis=2)
    st = st.reshape(n_lat, 2, 2, N_S5_CHUNKS, CH_P)
    st = jnp.transpose(st, (1, 3, 0, 2, 4)).reshape(2, N_S5_CHUNKS, n_lat, 2 * CH_P)
    init8 = jnp.zeros((2, N_S5_CHUNKS, n_lat, LAT_CHUNKS, 2 * CH_P), F32)
    init8 = init8.at[0, :, :, 0].set(st[0]).at[1, :, :, LAT_CHUNKS - 1].set(st[1])
    init8 = init8.reshape(2, N_S5_CHUNKS, SUB, 2 * CH_P)

    y_s5, fin = _s5(us, bbmat, ccmat, a8, init8, s5_d[layer])

    m1, m2 = _fnet_weights(w_fnet[layer])
    yf = _fnet(uf, m1, m2)

    x1, h2, comb = _post(y_s5, yf, xp4, xs4, pos3, modg, s5_w_glu[layer], w_out[layer], norm2_g[layer],
                         w_router[layer], b_router[layer], jnp.asarray(perm.T, BF16))

    y_prompt, y_lat = _moe_and_final(x1.reshape(N_BLK, TBLK, D), h2.reshape(T_TOK, D), comb.reshape(T_TOK, E_PAD),
                                     modg, norm_f_g, w_gate[layer], b_gate[layer], w_up[layer], b_up[layer],
                                     w_down[layer], b_down[layer])
    y_sample = y_lat.reshape(n_lat, LAT_CHUNKS * L_BLK, D)

    fin = fin[:, :N_GRP - 1].reshape(2, N_GRP - 1, SUB, N_S5_CHUNKS, 2, CH_P)
    fin = jnp.transpose(fin, (4, 1, 2, 0, 3, 5)).reshape(2, n_ctx, 1, 2, N_S5_GROUPS, S5_P)
    return (y_prompt, y_sample, fin[0], fin[1])
```

```python
import functools
import math

import numpy as np
import jax
import jax.numpy as jnp
from jax import lax
from jax.experimental import pallas as pl
from jax.experimental.pallas import tpu as pltpu

F32 = jnp.float32
BF16 = jnp.bfloat16
HIGHEST = lax.Precision.HIGHEST

D = 1024
D_S5 = 768
S5_H = 16
S5_P = 64
N_S5_GROUPS = 48
D_FN = 256
FN_GW = 64
N_EXPERTS = 32
TOP_K = 4
E_PAD = 128
SWIGLU_LIMIT = 7.0
SWIGLU_ALPHA = 1.702
RMS_EPS = 1e-6
POS_TEMP = 10000.0
GRID_W = 64

L_BLK = 256
SUB = 8
N_GRP = 3
GRP_ROWS = L_BLK * SUB
LAT_CHUNKS = 4
T_TOK = N_GRP * GRP_ROWS
TC = 128
N_TC = L_BLK // TC
PERM_T = 32
S5_CHUNK = 16
N_S5_CHUNKS = N_S5_GROUPS // S5_CHUNK
CH_U = S5_CHUNK * S5_H
CH_P = S5_CHUNK * S5_P
VMEM_LIMIT = 56 * 1024 * 1024


def _cparams(sem):
    return pltpu.CompilerParams(dimension_semantics=sem, vmem_limit_bytes=VMEM_LIMIT)


ADA_K = 128


def _adaln_kernel(c_ref, w_ref, b_ref, o_ref):
    @pl.when(pl.program_id(0) == 0)
    def _():
        o_ref[...] = jnp.broadcast_to(b_ref[...], o_ref.shape)

    c = c_ref[...]
    s = c * jax.nn.sigmoid(c)
    s_hi = s.astype(BF16)
    s_lo = (s - s_hi.astype(F32)).astype(BF16)
    w = w_ref[...]
    w_hi = w.astype(BF16)
    w_lo = (w - w_hi.astype(F32)).astype(BF16)
    o_ref[...] += (jnp.dot(s_hi, w_hi, preferred_element_type=F32) + jnp.dot(s_lo, w_hi, preferred_element_type=F32)
                   + jnp.dot(s_hi, w_lo, preferred_element_type=F32))


def _adaln(cvec, w_ada, b_ada):
    k_dim, n = w_ada.shape
    rows = cvec.shape[0]
    return pl.pallas_call(
        _adaln_kernel,
        grid=(k_dim // ADA_K,),
        in_specs=[pl.BlockSpec((rows, ADA_K), lambda k: (0, k)),
                  pl.BlockSpec((ADA_K, n), lambda k: (k, 0)),
                  pl.BlockSpec((1, n), lambda k: (0, 0))],
        out_specs=pl.BlockSpec((rows, n), lambda k: (0, 0)),
        out_shape=jax.ShapeDtypeStruct((rows, n), F32),
        compiler_params=_cparams(("arbitrary",)),
        name="adaln",
    )(cvec, w_ada, b_ada.reshape(1, n))


def _zoh(lam_re, lam_im, log_dt):
    dt = jnp.exp(log_dt)
    a_re = jnp.minimum(lam_re, -1e-4)
    a_im = lam_im
    mag = jnp.exp(a_re * dt)
    ab_re = mag * jnp.cos(a_im * dt)
    ab_im = mag * jnp.sin(a_im * dt)
    den = a_re * a_re + a_im * a_im
    nr = ab_re - 1.0
    f_re = (nr * a_re + ab_im * a_im) / den
    f_im = (ab_im * a_re - nr * a_im) / den
    return ab_re, ab_im, f_re, f_im


def _s5_params_kernel(lre3_ref, lim3_ref, ldt3_ref, lre2_ref, lim2_ref, ldt2_ref, bre_ref, bim_ref,
                      cre_ref, cim_ref, bb_ref, cc_ref, a_ref):
    exact = functools.partial(jnp.dot, preferred_element_type=F32, precision=HIGHEST)
    spread = jnp.where(lax.broadcasted_iota(jnp.int32, (S5_P, CH_P), 1) % S5_P
                       == lax.broadcasted_iota(jnp.int32, (S5_P, CH_P), 0), 1.0, 0.0)

    def block_diag(m):
        wide = jnp.dot(m.reshape(CH_U, S5_P).astype(BF16), spread.astype(BF16), preferred_element_type=F32)
        row_g = lax.broadcasted_iota(jnp.int32, (CH_U, CH_P), 0) // S5_H
        col_g = lax.broadcasted_iota(jnp.int32, (CH_U, CH_P), 1) // S5_P
        return jnp.where(row_g == col_g, wide, 0.0)

    _, _, f_re, f_im = _zoh(lre3_ref[0, 0], lim3_ref[0, 0], ldt3_ref[0, 0])
    b_re = bre_ref[0, 0]
    b_im = bim_ref[0, 0]
    bb_ref[0, 0] = jnp.concatenate([block_diag(f_re * b_re - f_im * b_im),
                                    block_diag(f_re * b_im + f_im * b_re)], axis=1).astype(BF16)
    cc_ref[0, 0] = jnp.concatenate([block_diag(cre_ref[0, 0]).T,
                                    -block_diag(cim_ref[0, 0]).T], axis=0).astype(BF16)

    ab_re, ab_im, _, _ = _zoh(lre2_ref[0, 0], lim2_ref[0, 0], ldt2_ref[0, 0])
    row_g = lax.broadcasted_iota(jnp.int32, (S5_CHUNK, CH_P), 0)
    col_g = lax.broadcasted_iota(jnp.int32, (S5_CHUNK, CH_P), 1) // S5_P

    def lane_row(a):
        flat = jnp.sum(jnp.where(row_g == col_g, exact(a, spread), 0.0), axis=0, keepdims=True)
        return jnp.broadcast_to(flat, (SUB, CH_P))

    a_ref[0, 0] = jnp.concatenate([lane_row(ab_re), lane_row(ab_im)], axis=1)


def _s5_params(lam_re, lam_im, log_dt, b_re, b_im, c_re, c_im):
    nc = N_S5_CHUNKS
    ldt = jnp.broadcast_to(log_dt[:, :, None], (2, N_S5_GROUPS, S5_P))
    g3 = lambda a: a.reshape(2, nc, S5_CHUNK, 1, S5_P)
    g2 = lambda a: a.reshape(2, nc, S5_CHUNK, S5_P)
    ghp = lambda a: a.reshape(2, nc, S5_CHUNK, S5_H, S5_P)
    spec3 = pl.BlockSpec((1, 1, S5_CHUNK, 1, S5_P), lambda d, c: (d, c, 0, 0, 0))
    spec2 = pl.BlockSpec((1, 1, S5_CHUNK, S5_P), lambda d, c: (d, c, 0, 0))
    spec_ghp = pl.BlockSpec((1, 1, S5_CHUNK, S5_H, S5_P), lambda d, c: (d, c, 0, 0, 0))
    return pl.pallas_call(
        _s5_params_kernel,
        grid=(2, nc),
        in_specs=[spec3, spec3, spec3, spec2, spec2, spec2, spec_ghp, spec_ghp, spec_ghp, spec_ghp],
        out_specs=[pl.BlockSpec((1, 1, CH_U, 2 * CH_P), lambda d, c: (d, c, 0, 0)),
                   pl.BlockSpec((1, 1, 2 * CH_P, CH_U), lambda d, c: (d, c, 0, 0)),
                   pl.BlockSpec((1, 1, SUB, 2 * CH_P), lambda d, c: (d, c, 0, 0))],
        out_shape=[jax.ShapeDtypeStruct((2, nc, CH_U, 2 * CH_P), BF16),
                   jax.ShapeDtypeStruct((2, nc, 2 * CH_P, CH_U), BF16),
                   jax.ShapeDtypeStruct((2, nc, SUB, 2 * CH_P), F32)],
        compiler_params=_cparams(("arbitrary", "arbitrary")),
        name="s5_params",
    )(g3(lam_re), g3(lam_im), g3(ldt), g2(lam_re), g2(lam_im), g2(ldt),
      ghp(jnp.swapaxes(b_re, -1, -2)), ghp(jnp.swapaxes(b_im, -1, -2)), ghp(c_re), ghp(c_im))


def _rms(x, g):
    return x * lax.rsqrt(jnp.mean(x * x, axis=-1, keepdims=True) + RMS_EPS) * g


def _load_x(g, xp_ref, xs_ref, pos_ref):
    is_lat = g == N_GRP - 1
    pos = pos_ref[...]
    pos8 = jnp.concatenate([pos, pos], axis=0)
    return jnp.where(is_lat, xs_ref[0] + pos8, xp_ref[0])


def _x_specs():
    n_ctx = N_GRP - 1
    xp_spec = pl.BlockSpec(
        (1, SUB, TC, D),
        lambda g, tc: (jnp.minimum(g, n_ctx - 1), 0, jnp.where(g >= n_ctx, N_TC - 1, tc), 0))
    xs_spec = pl.BlockSpec((1, SUB, TC, D), lambda g, tc: (0, 0, jnp.where(g >= n_ctx, tc, 0), 0))
    pos_spec = pl.BlockSpec((LAT_CHUNKS, TC, D), lambda g, tc: (0, tc, 0))
    return xp_spec, xs_spec, pos_spec


def _perm_matrix():
    p = np.zeros((SUB * PERM_T, SUB * PERM_T), np.float32)
    for s in range(SUB):
        for j in range(PERM_T):
            p[j * SUB + s, s * PERM_T + j] = 1.0
    return p


def _pre_kernel(xp_ref, xs_ref, pos_ref, mod_ref, g1_ref, win_ref, perm_ref, us_ref, uf_ref):
    g = pl.program_id(0)
    x = _load_x(g, xp_ref, xs_ref, pos_ref)
    mod = mod_ref[0]
    shift1 = mod[:, :, 0:D]
    scale1 = mod[:, :, D:2 * D]
    h = _rms(x, g1_ref[...]) * (1.0 + scale1) + shift1
    h2d = h.reshape(SUB * TC, D).astype(BF16)
    proj = jnp.dot(h2d, win_ref[...].astype(BF16), preferred_element_type=F32)
    uf_ref[0] = proj[:, D_S5:].astype(BF16).reshape(SUB, TC, D_FN)
    u = proj[:, :D_S5].astype(BF16)
    perm = perm_ref[...]
    for q in range(TC // PERM_T):
        piece = jnp.concatenate(
            [u[s * TC + q * PERM_T: s * TC + (q + 1) * PERM_T] for s in range(SUB)], axis=0)
        us_ref[0, q * SUB * PERM_T:(q + 1) * SUB * PERM_T, :] = jnp.dot(
            perm, piece, preferred_element_type=F32).astype(BF16)


def _pre(xp4, xs4, pos3, modg, norm1_g, w_in, perm):
    xp_spec, xs_spec, pos_spec = _x_specs()
    return pl.pallas_call(
        _pre_kernel,
        grid=(N_GRP, N_TC),
        in_specs=[xp_spec, xs_spec, pos_spec,
                  pl.BlockSpec((1, SUB, 1, 6 * D), lambda g, tc: (g, 0, 0, 0)),
                  pl.BlockSpec((1, D), lambda g, tc: (0, 0)),
                  pl.BlockSpec((D, D), lambda g, tc: (0, 0)),
                  pl.BlockSpec((SUB * PERM_T, SUB * PERM_T), lambda g, tc: (0, 0))],
        out_specs=[pl.BlockSpec((1, SUB * TC, D_S5), lambda g, tc: (g, tc, 0)),
                   pl.BlockSpec((1, SUB, TC, D_FN), lambda g, tc: (g, 0, tc, 0))],
        out_shape=[jax.ShapeDtypeStruct((N_GRP, GRP_ROWS, D_S5), BF16),
                   jax.ShapeDtypeStruct((N_GRP, SUB, L_BLK, D_FN), BF16)],
        compiler_params=_cparams(("arbitrary", "arbitrary")),
        name="pre_mixer",
    )(xp4, xs4, pos3, modg, norm1_g.reshape(1, D), w_in, perm)


def _cmul(ar, ai, br, bi):
    return ar * br - ai * bi, ar * bi + ai * br


S5_BLK = 512
S5_NBLK = GRP_ROWS // S5_BLK
S5_STEPS = S5_BLK // SUB
S5_FIX_UNROLL = 8


def _s5_kernel(us_ref, bb_ref, cc_ref, a_ref, init_ref, dskip_ref, y_ref, fin_ref, in_a, in_b, out_a, out_b):
    g = pl.program_id(0)
    d = pl.program_id(2)
    is_lat = g == N_GRP - 1
    re_all = slice(0, CH_P)
    im_all = slice(CH_P, 2 * CH_P)
    a_re = a_ref[0, 0, :, re_all]
    a_im = a_ref[0, 0, :, im_all]
    bufs_in = (in_a, in_b)
    bufs_out = (out_a, out_b)

    def time_rows(k):
        blk = k + d * (S5_NBLK - 1 - 2 * k)
        return pl.ds(pl.multiple_of(blk * S5_BLK, S5_BLK), S5_BLK)

    def slot_rows(k):
        return slice((k // 2) * S5_BLK, (k // 2 + 1) * S5_BLK)

    def step_rows(k, j):
        q = j + d * (S5_STEPS - 1 - 2 * j)
        return pl.ds(pl.multiple_of((k // 2) * S5_BLK + q * SUB, SUB), SUB)

    def proj_in(k):
        bufs_in[k % 2][slot_rows(k), :] = jnp.dot(us_ref[0, time_rows(k), :], bb_ref[0, 0],
                                                  preferred_element_type=F32)

    def scan_block(k, carry):
        src, dst = bufs_in[k % 2], bufs_out[k % 2]
        s_re, s_im = carry
        for j in range(S5_STEPS):
            rows = step_rows(k, j)
            n_re = a_re * s_re - a_im * s_im + src[rows, re_all]
            n_im = a_re * s_im + a_im * s_re + src[rows, im_all]
            dst[rows, re_all] = n_re
            dst[rows, im_all] = n_im
            s_re, s_im = n_re, n_im
        return s_re, s_im

    def proj_out(k):
        rows = time_rows(k)
        yb = jnp.dot(bufs_out[k % 2][slot_rows(k), :].astype(BF16), cc_ref[0, 0], preferred_element_type=F32)
        y_ref[0, rows, :] = y_ref[0, rows, :] + yb

    @pl.when(d == 0)
    def _():
        y_ref[0] = dskip_ref[...] * us_ref[0].astype(F32)

    def scan_all(carry, with_out):
        proj_in(0)
        for k in range(S5_NBLK):
            if k + 1 < S5_NBLK:
                proj_in(k + 1)
            carry = scan_block(k, carry)
            if with_out and k >= 1:
                proj_out(k - 1)
        if with_out:
            proj_out(S5_NBLK - 1)
        fin_ref[0, 0, :, re_all] = carry[0]
        fin_ref[0, 0, :, im_all] = carry[1]
        return carry

    @pl.when(jnp.logical_not(is_lat))
    def _():
        zero = jnp.zeros((SUB, CH_P), F32)
        scan_all((zero, zero), True)

    @pl.when(is_lat)
    def _():
        f_re, f_im = scan_all((init_ref[0, 0, :, re_all], init_ref[0, 0, :, im_all]), False)

        sub_id = lax.broadcasted_iota(jnp.int32, (SUB, CH_P), 0) % LAT_CHUNKS
        fwd = d == 0
        lo = jnp.where(fwd, 1, 0)
        hi = jnp.where(fwd, LAT_CHUNKS - 1, LAT_CHUNKS - 2)
        keep = (sub_id >= lo) & (sub_id <= hi)

        def from_prev(v):
            return jnp.where(keep, jnp.where(fwd, pltpu.roll(v, 1, 0), pltpu.roll(v, SUB - 1, 0)), 0.0)

        p_re, p_im = a_re, a_im
        for _ in range(8):
            p_re, p_im = _cmul(p_re, p_im, p_re, p_im)
        t_re, t_im = f_re, f_im
        for _ in range(LAT_CHUNKS - 2):
            m_re, m_im = _cmul(p_re, p_im, from_prev(t_re), from_prev(t_im))
            t_re, t_im = f_re + m_re, f_im + m_im
        corr = (from_prev(t_re), from_prev(t_im))

        for k in range(S5_NBLK):
            dst = bufs_out[k % 2]

            def fix_body(i, carry, k=k, dst=dst):
                m_re, m_im = carry
                for j in range(S5_FIX_UNROLL):
                    rows = step_rows(k, i * S5_FIX_UNROLL + j)
                    m_re, m_im = _cmul(m_re, m_im, a_re, a_im)
                    dst[rows, re_all] = dst[rows, re_all] + m_re
                    dst[rows, im_all] = dst[rows, im_all] + m_im
                return m_re, m_im

            corr = lax.fori_loop(0, S5_STEPS // S5_FIX_UNROLL, fix_body, corr)

        for k in range(S5_NBLK):
            proj_out(k)


def _s5(us, bbmat, ccmat, a8, init8, d_skip):
    return pl.pallas_call(
        _s5_kernel,
        grid=(N_GRP, N_S5_CHUNKS, 2),
        in_specs=[pl.BlockSpec((1, GRP_ROWS, CH_U), lambda g, c, d: (g, 0, c)),
                  pl.BlockSpec((1, 1, CH_U, 2 * CH_P), lambda g, c, d: (d, c, 0, 0)),
                  pl.BlockSpec((1, 1, 2 * CH_P, CH_U), lambda g, c, d: (d, c, 0, 0)),
                  pl.BlockSpec((1, 1, SUB, 2 * CH_P), lambda g, c, d: (d, c, 0, 0)),
                  pl.BlockSpec((1, 1, SUB, 2 * CH_P), lambda g, c, d: (d, c, 0, 0)),
                  pl.BlockSpec((1, CH_U), lambda g, c, d: (0, c))],
        out_specs=[pl.BlockSpec((1, GRP_ROWS, CH_U), lambda g, c, d: (g, 0, c)),
                   pl.BlockSpec((1, 1, SUB, 2 * CH_P), lambda g, c, d: (d, g, 0, c))],
        out_shape=[jax.ShapeDtypeStruct((N_GRP, GRP_ROWS, D_S5), F32),
                   jax.ShapeDtypeStruct((2, N_GRP, SUB, N_S5_CHUNKS * 2 * CH_P), F32)],
        scratch_shapes=[pltpu.VMEM((GRP_ROWS // 2, 2 * CH_P), F32)] * 4,
        compiler_params=_cparams(("arbitrary", "arbitrary", "arbitrary")),
        name="s5_scan",
    )(us, bbmat, ccmat, a8, init8, d_skip.reshape(1, D_S5))


def _fnet_w_kernel(c_ref, s_ref, w_ref, m1_ref, m2_ref):
    w = w_ref[...]
    m1_ref[...] = jnp.dot(c_ref[...], w, preferred_element_type=F32, precision=HIGHEST).astype(BF16)
    m2_ref[...] = jnp.dot(s_ref[...], w, preferred_element_type=F32, precision=HIGHEST).astype(BF16)


FN_SEQ = LAT_CHUNKS
FN_CTX_STEPS = (N_GRP - 1) * SUB // FN_SEQ


def _fnet_kernel(u_ref, cos_s_ref, sin_s_ref, cos_l_ref, sin_l_ref, m1_ref, m2_ref, o_ref):
    i = pl.program_id(0)
    m1 = m1_ref[...]
    m2 = m2_ref[...]

    def mix(u, cos_ref, sin_ref):
        v1 = jnp.dot(u, m1, preferred_element_type=F32).astype(BF16)
        v2 = jnp.dot(u, m2, preferred_element_type=F32).astype(BF16)
        return (jnp.dot(cos_ref[...].astype(BF16), v1, preferred_element_type=F32)
                - jnp.dot(sin_ref[...].astype(BF16), v2, preferred_element_type=F32)).astype(BF16)

    @pl.when(i < FN_CTX_STEPS)
    def _():
        for s in range(FN_SEQ):
            o_ref[0, s] = mix(u_ref[0, s], cos_s_ref, sin_s_ref)

    @pl.when(i >= FN_CTX_STEPS)
    def _():
        u = u_ref[0].reshape(FN_SEQ * L_BLK, D_FN)
        o_ref[0] = mix(u, cos_l_ref, sin_l_ref).reshape(FN_SEQ, L_BLK, D_FN)


def _dft_tables(n):
    k = np.arange(n, dtype=np.int64)
    ang = (2.0 * np.pi / n) * ((k[:, None] * k[None, :]) % n).astype(np.float64)
    scale = 1.0 / math.sqrt(n)
    return (np.cos(ang) * scale).astype(np.float32), (np.sin(ang) * scale).astype(np.float32)


def _fnet(uf, m1, m2):
    n_lat_steps = SUB // FN_SEQ
    cos_s, sin_s = _dft_tables(L_BLK)
    cos_l, sin_l = _dft_tables(FN_SEQ * L_BLK)
    full = lambda i: (0, 0)
    per_grp = SUB // FN_SEQ
    blk = pl.BlockSpec(
        (1, FN_SEQ, L_BLK, D_FN),
        lambda i: (jnp.minimum(i // per_grp, N_GRP - 1),
                   jnp.where(i < FN_CTX_STEPS, i % per_grp, i - FN_CTX_STEPS), 0, 0))
    table = lambda n: pl.BlockSpec((n, n), full)
    return pl.pallas_call(
        _fnet_kernel,
        grid=(FN_CTX_STEPS + n_lat_steps,),
        in_specs=[blk, table(L_BLK), table(L_BLK), table(FN_SEQ * L_BLK), table(FN_SEQ * L_BLK),
                  table(D_FN), table(D_FN)],
        out_specs=blk,
        out_shape=jax.ShapeDtypeStruct(uf.shape, BF16),
        compiler_params=_cparams(("arbitrary",)),
        name="fnet",
    )(uf, jnp.asarray(cos_s), jnp.asarray(sin_s), jnp.asarray(cos_l), jnp.asarray(sin_l), m1, m2)


def _fnet_weights(w_fnet):
    n_g = D_FN // FN_GW
    cos_c, sin_c = _dft_tables(FN_GW)
    eye = np.eye(n_g, dtype=np.float32)
    cos_bd = np.kron(eye, cos_c)
    sin_bd = np.kron(eye, sin_c)
    w_bd = (w_fnet[:, :, None, :] * jnp.asarray(eye)[:, None, :, None]).reshape(D_FN, D_FN)
    out = jax.ShapeDtypeStruct((D_FN, D_FN), BF16)
    return pl.pallas_call(_fnet_w_kernel, out_shape=(out, out), name="fnet_weights")(
        jnp.asarray(cos_bd), jnp.asarray(sin_bd), w_bd)


def _gelu_tanh(x):
    return 0.5 * x * (1.0 + jnp.tanh(math.sqrt(2.0 / math.pi) * (x + 0.044715 * (x * x * x))))


def _post_kernel(y_ref, yf_ref, xp_ref, xs_ref, pos_ref, mod_ref, wglu_ref, wout_ref, g2_ref, wr_ref,
                 br_ref, permt_ref, x1_ref, h2_ref, comb_ref):
    g = pl.program_id(0)
    z = _gelu_tanh(y_ref[0])
    gate = jnp.dot(z.astype(BF16), wglu_ref[...].astype(BF16), preferred_element_type=F32)
    gl = (z * jax.nn.sigmoid(gate)).astype(BF16)
    permt = permt_ref[...]
    n_q = TC // PERM_T
    nat = [jnp.dot(permt, gl[q * SUB * PERM_T:(q + 1) * SUB * PERM_T], preferred_element_type=F32).astype(BF16)
           for q in range(n_q)]
    gl_nat = jnp.concatenate(
        [nat[q][s * PERM_T:(s + 1) * PERM_T] for s in range(SUB) for q in range(n_q)], axis=0)
    w_out = wout_ref[...].astype(BF16)
    mixed = (jnp.dot(gl_nat, w_out[:D_S5], preferred_element_type=F32)
             + jnp.dot(yf_ref[0].reshape(SUB * TC, D_FN), w_out[D_S5:], preferred_element_type=F32))
    x = _load_x(g, xp_ref, xs_ref, pos_ref)
    mod = mod_ref[0]
    gate1 = mod[:, :, 2 * D:3 * D]
    shift2 = mod[:, :, 3 * D:4 * D]
    scale2 = mod[:, :, 4 * D:5 * D]
    x1 = x + gate1 * mixed.reshape(SUB, TC, D)
    x1_ref[0] = x1
    h2 = _rms(x1, g2_ref[...]) * (1.0 + scale2) + shift2
    h2_ref[0] = h2.astype(BF16)

    hr = h2.reshape(SUB * TC, D)
    h_hi = hr.astype(BF16)
    h_lo = (hr - h_hi.astype(F32)).astype(BF16)
    wr = wr_ref[...]
    w_hi = wr.astype(BF16)
    w_lo = (wr - w_hi.astype(F32)).astype(BF16)
    both = jnp.dot(h_hi, jnp.concatenate([w_hi, w_lo], axis=1), preferred_element_type=F32)
    logits = (both[:, :E_PAD] + jnp.dot(h_lo, w_hi, preferred_element_type=F32) + both[:, E_PAD:]
              + br_ref[...])
    lane = lax.broadcasted_iota(jnp.int32, logits.shape, 1).astype(F32)
    top_v, hots = [], []
    cur = logits
    for _ in range(TOP_K):
        m = jnp.max(cur, axis=-1, keepdims=True)
        idx = jnp.min(jnp.where(cur == m, lane, float(E_PAD)), axis=-1, keepdims=True)
        hot = lane == idx
        top_v.append(m)
        hots.append(hot)
        cur = jnp.where(hot, -3.0e38, cur)
    exps = [jnp.exp(v - top_v[0]) for v in top_v]
    denom = exps[0] + exps[1] + exps[2] + exps[3]
    comb = jnp.zeros(logits.shape, F32)
    for k in range(TOP_K):
        comb = comb + jnp.where(hots[k], exps[k] / denom, 0.0)
    comb_ref[0] = comb.reshape(SUB, TC, E_PAD)


def _post(y, yf, xp4, xs4, pos3, modg, w_glu, w_out, norm2_g, w_router, b_router, permt):
    xp_spec, xs_spec, pos_spec = _x_specs()
    const2 = lambda g, tc: (0, 0)
    wr = jnp.zeros((D, E_PAD), F32).at[:, :N_EXPERTS].set(w_router)
    br = jnp.full((1, E_PAD), -1.0e30, F32).at[0, :N_EXPERTS].set(b_router)
    blk = lambda w: pl.BlockSpec((1, SUB, TC, w), lambda g, tc: (g, 0, tc, 0))
    return pl.pallas_call(
        _post_kernel,
        grid=(N_GRP, N_TC),
        in_specs=[pl.BlockSpec((1, SUB * TC, D_S5), lambda g, tc: (g, tc, 0)),
                  blk(D_FN), xp_spec, xs_spec, pos_spec,
                  pl.BlockSpec((1, SUB, 1, 6 * D), lambda g, tc: (g, 0, 0, 0)),
                  pl.BlockSpec((D_S5, D_S5), const2),
                  pl.BlockSpec((D, D), const2),
                  pl.BlockSpec((1, D), const2),
                  pl.BlockSpec((D, E_PAD), const2),
                  pl.BlockSpec((1, E_PAD), const2),
                  pl.BlockSpec((SUB * PERM_T, SUB * PERM_T), const2)],
        out_specs=[blk(D), blk(D), blk(E_PAD)],
        out_shape=[jax.ShapeDtypeStruct((N_GRP, SUB, L_BLK, D), F32),
                   jax.ShapeDtypeStruct((N_GRP, SUB, L_BLK, D), BF16),
                   jax.ShapeDtypeStruct((N_GRP, SUB, L_BLK, E_PAD), F32)],
        compiler_params=_cparams(("arbitrary", "arbitrary")),
        name="post_mixer",
    )(y, yf, xp4, xs4, pos3, modg, w_glu, w_out, norm2_g.reshape(1, D), wr, br, permt)


TBLK = L_BLK
N_BLK = T_TOK // TBLK
N_CTX_BLK = (N_GRP - 1) * SUB
SEG_ALIGN = 8
RB = 1280
assert RB >= TBLK * TOP_K + N_EXPERTS * (SEG_ALIGN - 1) and RB % 128 == 0
TM = 256
TM_SHIFT = 8
assert 1 << TM_SHIFT == TM
R_TOT = T_TOK * TOP_K + N_BLK * N_EXPERTS * (SEG_ALIGN - 1) + TM
TAB_ROWS = 32
assert TAB_ROWS >= N_BLK
BIG_ROWS = 32
BIG_SHIFT = 2
assert SEG_ALIGN << BIG_SHIFT == BIG_ROWS


def _plan_kernel(comb_ref, z_ref, ptab_ref, loff_ref, loff_i_ref, gst_i_ref, p8_i_ref, eoff_i_ref):
    row = lax.broadcasted_iota(jnp.int32, (TBLK, TBLK), 0)
    col = lax.broadcasted_iota(jnp.int32, (TBLK, TBLK), 1)
    earlier = jnp.where(row > col, 1.0, 0.0).astype(BF16)
    ptab_ref[...] = jnp.zeros_like(ptab_ref)

    def body(b, _):
        rows = pl.ds(pl.multiple_of(b * TBLK, TBLK), TBLK)
        m = jnp.where(comb_ref[rows, :] > 0.0, 1.0, 0.0)
        rank = jnp.dot(earlier, m.astype(BF16), preferred_element_type=F32)
        z_ref[rows, :] = m * (rank + 1.0)
        n = jnp.sum(m, axis=0, keepdims=True)
        ptab_ref[pl.ds(b, 1), :] = jnp.floor((n + (SEG_ALIGN - 1)) * (1.0 / SEG_ALIGN)) * SEG_ALIGN
        return 0

    lax.fori_loop(0, N_BLK, body, 0)

    ptab = ptab_ref[...]
    er = lax.broadcasted_iota(jnp.int32, (E_PAD, E_PAD), 0)
    ec = lax.broadcasted_iota(jnp.int32, (E_PAD, E_PAD), 1)
    before = jnp.where(er < ec, 1.0, 0.0)
    exact = functools.partial(jnp.dot, preferred_element_type=F32, precision=HIGHEST)
    loff = exact(ptab, before)
    loff_ref[...] = loff
    as_int = lambda t: t[:, :N_EXPERTS].astype(jnp.int32)
    loff_i_ref[...] = as_int(loff)
    p8_i_ref[...] = as_int(ptab * (1.0 / SEG_ALIGN))
    tot = jnp.sum(ptab, axis=0, keepdims=True)
    eoff = exact(jnp.broadcast_to(tot, (SUB, E_PAD)), before)
    eoff_i_ref[...] = eoff.astype(jnp.int32)
    br = lax.broadcasted_iota(jnp.int32, (TAB_ROWS, TAB_ROWS), 0)
    bc = lax.broadcasted_iota(jnp.int32, (TAB_ROWS, TAB_ROWS), 1)
    gst_i_ref[...] = as_int(eoff[0:1] + exact(jnp.where(br > bc, 1.0, 0.0), ptab))


def _plan(comb):
    tab = jax.ShapeDtypeStruct((TAB_ROWS, E_PAD), F32)
    itab = jax.ShapeDtypeStruct((TAB_ROWS, N_EXPERTS), jnp.int32)
    return pl.pallas_call(
        _plan_kernel,
        out_shape=(jax.ShapeDtypeStruct((T_TOK, E_PAD), F32), tab, tab, itab, itab, itab,
                   jax.ShapeDtypeStruct((SUB, E_PAD), jnp.int32)),
        compiler_params=pltpu.CompilerParams(vmem_limit_bytes=VMEM_LIMIT),
        name="moe_plan",
    )(comb)


def _sort_matrix(b, z_ref, loff_ref, ptab_ref):
    loff = loff_ref[pl.ds(b, 1), :]
    size = ptab_ref[pl.ds(b, 1), :]
    r = lax.broadcasted_iota(jnp.int32, (RB, E_PAD), 0).astype(F32)
    owner = jnp.where(r >= loff, jnp.where(r < loff + size, 1.0, 0.0), 0.0)
    rank1 = r[:, 0:1] - jnp.sum(owner * loff, axis=-1, keepdims=True) + 1.0
    zt = z_ref[...].T
    v = jnp.dot(owner.astype(BF16), zt.astype(BF16), preferred_element_type=F32)
    return jnp.where(v == rank1, 1.0, 0.0), owner


def _segment_copies(b, loff_s, gst_s, p8_s, make_copy):
    def per_expert(e, counts):
        chunks = p8_s[b, e]
        n_big = lax.shift_right_logical(chunks, BIG_SHIFT)
        n_small = chunks - n_big * (BIG_ROWS // SEG_ALIGN)
        local0 = loff_s[b, e]
        global0 = gst_s[b, e]

        def big(j, _):
            make_copy(pl.multiple_of(local0 + j * BIG_ROWS, SEG_ALIGN),
                      pl.multiple_of(global0 + j * BIG_ROWS, SEG_ALIGN), BIG_ROWS).start()
            return 0

        def small(j, _):
            off = n_big * BIG_ROWS + j * SEG_ALIGN
            make_copy(pl.multiple_of(local0 + off, SEG_ALIGN),
                      pl.multiple_of(global0 + off, SEG_ALIGN), SEG_ALIGN).start()
            return 0

        lax.fori_loop(0, n_big, big, 0)
        lax.fori_loop(0, n_small, small, 0)
        return counts[0] + n_big, counts[1] + n_small

    return lax.fori_loop(0, N_EXPERTS, per_expert, (0, 0))


def _wait_copies(counts, make_copy):
    for n, rows in zip(counts, (BIG_ROWS, SEG_ALIGN)):
        def wait_one(i, _, rows=rows):
            make_copy(0, 0, rows).wait()
            return 0
        lax.fori_loop(0, n, wait_one, 0)


def _zero_rows_from(hbm, zeros_vmem, first_row, sem):
    def copy_to(row):
        return pltpu.make_async_copy(zeros_vmem, hbm.at[pl.ds(row, TM)], sem)

    n_full = lax.shift_right_logical(R_TOT - first_row, TM_SHIFT)

    def start_one(j, _):
        copy_to(pl.multiple_of(first_row + j * TM, SEG_ALIGN)).start()
        return 0

    def wait_one(j, _):
        copy_to(0).wait()
        return 0

    lax.fori_loop(0, n_full, start_one, 0)
    lax.fori_loop(0, n_full, wait_one, 0)
    last = copy_to(R_TOT - TM)
    last.start()
    last.wait()


def _dispatch_kernel(loff_s, gst_s, p8_s, eoff_s, h_ref, z_ref, loff_ref, ptab_ref, xs_hbm, xbuf, sem, pending):
    b = pl.program_id(0)

    def make_copy(local_row, global_row, rows):
        return pltpu.make_async_copy(xbuf.at[pl.ds(local_row, rows)], xs_hbm.at[pl.ds(global_row, rows)], sem)

    pm = _sort_matrix(b, z_ref, loff_ref, ptab_ref)[0].astype(BF16)

    @pl.when(b > 0)
    def _():
        _wait_copies((pending[0], pending[1]), make_copy)

    xbuf[...] = jnp.dot(pm, h_ref[...], preferred_element_type=F32)
    n_big, n_small = _segment_copies(b, loff_s, gst_s, p8_s, make_copy)
    pending[0] = n_big
    pending[1] = n_small

    @pl.when(b == N_BLK - 1)
    def _():
        _wait_copies((n_big, n_small), make_copy)
        xbuf[0:TM, :] = jnp.zeros((TM, D), F32)
        _zero_rows_from(xs_hbm, xbuf.at[0:TM], eoff_s[0, N_EXPERTS], sem)


def _dispatch(h2, z, loff, ptab, loff_i, gst_i, p8_i, eoff_i):
    whole = pl.BlockSpec((TAB_ROWS, E_PAD), lambda b, *_: (0, 0))
    return pl.pallas_call(
        _dispatch_kernel,
        grid_spec=pltpu.PrefetchScalarGridSpec(
            num_scalar_prefetch=4,
            grid=(N_BLK,),
            in_specs=[pl.BlockSpec((TBLK, D), lambda b, *_: (b, 0)),
                      pl.BlockSpec((TBLK, E_PAD), lambda b, *_: (b, 0)),
                      whole, whole],
            out_specs=pl.BlockSpec(memory_space=pl.ANY),
            scratch_shapes=[pltpu.VMEM((RB, D), F32), pltpu.SemaphoreType.DMA(()),
                            pltpu.SMEM((2,), jnp.int32)]),
        out_shape=jax.ShapeDtypeStruct((R_TOT, D), F32),
        compiler_params=_cparams(("arbitrary",)),
        name="moe_dispatch",
    )(loff_i, gst_i, p8_i, eoff_i, h2, z, loff, ptab)


MAX_TILES = R_TOT // TM + N_EXPERTS
X_AHEAD = 3
X_SLOTS = X_AHEAD + 1
Y_SLOTS = 2
W_AHEAD = 2
W_SLOTS = W_AHEAD + 1
N_MATS = 3
FF_CHUNK = 256
CAST_ROWS = 64
TILE_STEP = 64


def _expert_kernel(eoff_s, xs_hbm, wg_hbm, wu_hbm, wd_hbm, bg_ref, bu_ref, bd_ref, ys_hbm,
                   wst, wbf, xin, yout, act_ref, w_sem, x_sem, y_sem, t_exp, t_row, t_first, t_valid, live):
    def add_expert(e, carry):
        n_t, n_live = carry
        start = eoff_s[0, e]
        count = eoff_s[0, e + 1] - start
        tiles = lax.shift_right_logical(count + (TM - 1), TM_SHIFT)

        def add_tile(i, _):
            t_exp[n_t + i] = e
            t_row[n_t + i] = start + i * TM
            t_first[n_t + i] = jnp.where(i == 0, 1, 0)
            t_valid[n_t + i] = jnp.minimum(count - i * TM, TM)
            return 0

        lax.fori_loop(0, tiles, add_tile, 0)
        has_rows = jnp.where(tiles > 0, 1, 0)

        @pl.when(tiles > 0)
        def _():
            live[n_live] = e

        return n_t + tiles, n_live + has_rows

    n_tiles, n_live = lax.fori_loop(0, N_EXPERTS, add_expert, (0, 0))

    def w_copies(e, slot):
        return [pltpu.make_async_copy(w.at[e], wst.at[slot, m], w_sem.at[slot])
                for m, w in enumerate((wg_hbm, wu_hbm, wd_hbm))]

    def tile_rows(j):
        return pl.ds(pl.multiple_of(t_row[j], SEG_ALIGN), TM)

    def x_copy(j, slot):
        return pltpu.make_async_copy(xs_hbm.at[tile_rows(j)], xin.at[slot], x_sem.at[slot])

    def y_pieces(j, op):
        ys = lax.rem(j, Y_SLOTS)
        valid = t_valid[j]
        row0 = t_row[j]

        def piece(off, rows):
            return pltpu.make_async_copy(yout.at[ys, pl.ds(off, rows)],
                                         ys_hbm.at[pl.ds(pl.multiple_of(row0 + off, SEG_ALIGN), rows)],
                                         y_sem.at[ys])

        @pl.when(valid == TM)
        def _():
            op(piece(0, TM))

        @pl.when(valid < TM)
        def _():
            n_big = lax.shift_right_logical(valid, BIG_ROWS.bit_length() - 1)
            n_small = lax.shift_right_logical(valid - n_big * BIG_ROWS, SEG_ALIGN.bit_length() - 1)

            def big(i, _):
                op(piece(pl.multiple_of(i * BIG_ROWS, SEG_ALIGN), BIG_ROWS))
                return 0

            def small(i, _):
                op(piece(pl.multiple_of(n_big * BIG_ROWS + i * SEG_ALIGN, SEG_ALIGN), SEG_ALIGN))
                return 0

            lax.fori_loop(0, n_big, big, 0)
            lax.fori_loop(0, n_small, small, 0)

    for ahead in range(W_AHEAD):
        @pl.when(n_live > ahead)
        def _(ahead=ahead):
            for cp in w_copies(live[ahead], ahead):
                cp.start()

    for ahead in range(X_AHEAD):
        @pl.when(n_tiles > ahead)
        def _(ahead=ahead):
            x_copy(ahead, ahead).start()

    def body(j, k):
        e = t_exp[j]

        @pl.when(t_first[j] == 1)
        def _():
            ws = lax.rem(k, W_SLOTS)

            @pl.when(k + W_AHEAD < n_live)
            def _():
                for cp in w_copies(live[k + W_AHEAD], lax.rem(k + W_AHEAD, W_SLOTS)):
                    cp.start()

            for cp in w_copies(e, ws):
                cp.wait()

            def cast_rows(r, _):
                rows = pl.ds(pl.multiple_of(r * CAST_ROWS, CAST_ROWS), CAST_ROWS)
                for m in range(N_MATS):
                    wbf[m, rows, :] = wst[ws, m, rows, :].astype(BF16)
                return 0

            lax.fori_loop(0, D // CAST_ROWS, cast_rows, 0)

        slot = lax.rem(j, X_SLOTS)
        x_copy(j, slot).wait()

        @pl.when(j + X_AHEAD < n_tiles)
        def _():
            x_copy(j + X_AHEAD, lax.rem(j + X_AHEAD, X_SLOTS)).start()

        @pl.when(j >= Y_SLOTS)
        def _():
            y_pieces(j - Y_SLOTS, lambda cp: cp.wait())

        ys = lax.rem(j, Y_SLOTS)

        def compute(rows):
            x = xin[slot, 0:rows, :].astype(BF16)
            for c in range(D // FF_CHUNK):
                cols = slice(c * FF_CHUNK, (c + 1) * FF_CHUNK)
                gate = jnp.dot(x, wbf[0, :, cols], preferred_element_type=F32) + bg_ref[e][:, cols]
                up = jnp.dot(x, wbf[1, :, cols], preferred_element_type=F32) + bu_ref[e][:, cols]
                gate = jnp.minimum(gate, SWIGLU_LIMIT)
                up = jnp.clip(up, -SWIGLU_LIMIT, SWIGLU_LIMIT)
                act_ref[0:rows, cols] = ((up + 1.0) * gate * jax.nn.sigmoid(SWIGLU_ALPHA * gate)).astype(BF16)
            yout[ys, 0:rows, :] = jnp.dot(act_ref[0:rows, :], wbf[2], preferred_element_type=F32) + bd_ref[e]

        valid = t_valid[j]
        for rows in range(TILE_STEP, TM + 1, TILE_STEP):
            @pl.when(jnp.logical_and(valid > rows - TILE_STEP, valid <= rows))
            def _(rows=rows):
                compute(rows)

        y_pieces(j, lambda cp: cp.start())
        return k + t_first[j]

    lax.fori_loop(0, n_tiles, body, 0)

    for back in range(Y_SLOTS, 0, -1):
        @pl.when(n_tiles >= back)
        def _(back=back):
            y_pieces(n_tiles - back, lambda cp: cp.wait())

    yout[0] = jnp.zeros((TM, D), F32)
    _zero_rows_from(ys_hbm, yout.at[0], eoff_s[0, N_EXPERTS], y_sem.at[0])


def _experts(xs, eoff_i, w_gate, b_gate, w_up, b_up, w_down, b_down):
    hbm = pl.BlockSpec(memory_space=pl.ANY)
    bspec = pl.BlockSpec((N_EXPERTS, 1, D), lambda i, *_: (0, 0, 0))
    return pl.pallas_call(
        _expert_kernel,
        grid_spec=pltpu.PrefetchScalarGridSpec(
            num_scalar_prefetch=1,
            grid=(1,),
            in_specs=[hbm, hbm, hbm, hbm, bspec, bspec, bspec],
            out_specs=hbm,
            scratch_shapes=[pltpu.VMEM((W_SLOTS, N_MATS, D, D), F32), pltpu.VMEM((N_MATS, D, D), BF16),
                            pltpu.VMEM((X_SLOTS, TM, D), F32), pltpu.VMEM((Y_SLOTS, TM, D), F32),
                            pltpu.VMEM((TM, D), BF16),
                            pltpu.SemaphoreType.DMA((W_SLOTS,)), pltpu.SemaphoreType.DMA((X_SLOTS,)),
                            pltpu.SemaphoreType.DMA((Y_SLOTS,)),
                            pltpu.SMEM((MAX_TILES,), jnp.int32), pltpu.SMEM((MAX_TILES,), jnp.int32),
                            pltpu.SMEM((MAX_TILES,), jnp.int32), pltpu.SMEM((MAX_TILES,), jnp.int32),
                            pltpu.SMEM((N_EXPERTS,), jnp.int32)]),
        out_shape=jax.ShapeDtypeStruct((R_TOT, D), F32),
        compiler_params=_cparams(("arbitrary",)),
        name="moe_experts",
    )(eoff_i, xs, w_gate, w_up, w_down, b_gate.reshape(N_EXPERTS, 1, D), b_up.reshape(N_EXPERTS, 1, D),
      b_down.reshape(N_EXPERTS, 1, D))


def _combine_kernel(loff_s, gst_s, p8_s, ys_hbm, z_ref, comb_ref, loff_ref, ptab_ref, x1_ref, mod_ref, gf_ref,
                    yctx_ref, ylat_ref, ybuf, sem):
    b = pl.program_id(0)

    def make_copy(local_row, global_row, rows):
        return pltpu.make_async_copy(ys_hbm.at[pl.ds(global_row, rows)], ybuf.at[pl.ds(local_row, rows)], sem)

    started = _segment_copies(b, loff_s, gst_s, p8_s, make_copy)

    used_chunks = lax.shift_right_logical(loff_s[b, N_EXPERTS - 1], 3) + p8_s[b, N_EXPERTS - 1]

    def zero_chunk(j, _):
        ybuf[pl.ds(pl.multiple_of(j * SEG_ALIGN, SEG_ALIGN), SEG_ALIGN), :] = jnp.zeros((SEG_ALIGN, D), F32)
        return 0

    lax.fori_loop(used_chunks, RB // SEG_ALIGN, zero_chunk, 0)

    pm, owner = _sort_matrix(b, z_ref, loff_ref, ptab_ref)
    comb = comb_ref[...]
    c_hi = comb.astype(BF16)
    rest = comb - c_hi.astype(F32)
    c_mid = rest.astype(BF16)
    c_lo = (rest - c_mid.astype(F32)).astype(BF16)
    pmb = pm.astype(BF16)
    moved = (jnp.dot(pmb, c_hi, preferred_element_type=F32) + jnp.dot(pmb, c_mid, preferred_element_type=F32)
             + jnp.dot(pmb, c_lo, preferred_element_type=F32))
    w_row = jnp.sum(owner * moved, axis=-1, keepdims=True)
    pmt = pm.T.astype(BF16)
    _wait_copies(started, make_copy)
    y = (ybuf[...] * w_row).astype(BF16)
    moe = jnp.dot(pmt, y, preferred_element_type=F32)
    gate2 = mod_ref[0][:, 5 * D:6 * D]
    x2 = x1_ref[0] + gate2 * moe
    y = _rms(x2, gf_ref[...])

    @pl.when(b < N_CTX_BLK)
    def _():
        yctx_ref[0] = y

    @pl.when(b >= N_CTX_BLK)
    def _():
        ylat_ref[0] = y


def _combine(ys, z, comb, loff, ptab, x1, modv, norm_f_g, loff_i, gst_i, p8_i):
    whole = pl.BlockSpec((TAB_ROWS, E_PAD), lambda b, *_: (0, 0))
    tok = pl.BlockSpec((TBLK, E_PAD), lambda b, *_: (b, 0))
    return pl.pallas_call(
        _combine_kernel,
        grid_spec=pltpu.PrefetchScalarGridSpec(
            num_scalar_prefetch=3,
            grid=(N_BLK,),
            in_specs=[pl.BlockSpec(memory_space=pl.ANY), tok, tok, whole, whole,
                      pl.BlockSpec((1, TBLK, D), lambda b, *_: (b, 0, 0)),
                      pl.BlockSpec((1, 1, 6 * D), lambda b, *_: (b, 0, 0)),
                      pl.BlockSpec((1, D), lambda b, *_: (0, 0))],
            out_specs=[pl.BlockSpec((1, TBLK, D), lambda b, *_: (jnp.minimum(b, N_CTX_BLK - 1), 0, 0)),
                       pl.BlockSpec((1, TBLK, D), lambda b, *_: (jnp.maximum(b - N_CTX_BLK, 0), 0, 0))],
            scratch_shapes=[pltpu.VMEM((RB, D), F32), pltpu.SemaphoreType.DMA(())]),
        out_shape=[jax.ShapeDtypeStruct((N_CTX_BLK, TBLK, D), F32),
                   jax.ShapeDtypeStruct((N_BLK - N_CTX_BLK, TBLK, D), F32)],
        compiler_params=_cparams(("arbitrary",)),
        name="moe_combine",
    )(loff_i, gst_i, p8_i, ys, z, comb, loff, ptab, x1, modv, norm_f_g.reshape(1, D))


def _moe_and_final(x1, h2, comb, modg, norm_f_g, w_gate, b_gate, w_up, b_up, w_down, b_down):
    z, ptab, loff, loff_i, gst_i, p8_i, eoff_i = _plan(comb)
    xs = _dispatch(h2, z, loff, ptab, loff_i, gst_i, p8_i, eoff_i)
    ys = _experts(xs, eoff_i, w_gate, b_gate, w_up, b_up, w_down, b_down)
    return _combine(ys, z, comb, loff, ptab, x1, modg.reshape(N_BLK, 1, 6 * D), norm_f_g,
                    loff_i, gst_i, p8_i)


def _grid_pos_embed(n_tokens, dim):
    rows = n_tokens // GRID_W
    t = np.arange(rows * GRID_W)
    r = (t // GRID_W).astype(np.float32)
    col = (t % GRID_W).astype(np.float32)
    q = dim // 4
    omega = (1.0 / np.float32(POS_TEMP) ** (np.arange(q, dtype=np.float32) / np.float32(q))).astype(np.float32)

    def emb(p):
        a = p[:, None] * omega[None, :]
        return np.concatenate([np.sin(a), np.cos(a)], axis=-1)

    return np.concatenate([emb(r), emb(col)], axis=-1).astype(np.float32)


def kernel(x_prompt, x_sample, c, state_s5_re, state_s5_im, c_ctx, w_ada, b_ada, norm1_g, w_in, s5_lam_re,
           s5_lam_im, s5_log_dt, s5_b_re, s5_b_im, s5_c_re, s5_c_im, s5_d, s5_w_glu, w_fnet, w_out, norm2_g,
           w_router, b_router, w_gate, b_gate, w_up, b_up, w_down, b_down, norm_f_g):
    n_ctx, n_lat = x_prompt.shape[0], x_sample.shape[0]
    assert x_prompt.shape == (SUB * (N_GRP - 1), L_BLK, D) and x_sample.shape == (2, LAT_CHUNKS * L_BLK, D)
    assert w_ada.shape[0] == 1, "one trunk layer"
    layer = 0

    cvec = jnp.concatenate([jnp.broadcast_to(c_ctx[None], (n_ctx, D)), jnp.repeat(c, LAT_CHUNKS, axis=0)], axis=0)
    modg = _adaln(cvec, w_ada[layer], b_ada[layer]).reshape(N_GRP, SUB, 1, 6 * D)

    xp4 = x_prompt.reshape(N_GRP - 1, SUB, L_BLK, D)
    xs4 = x_sample.reshape(1, SUB, L_BLK, D)
    pos3 = jnp.asarray(_grid_pos_embed(LAT_CHUNKS * L_BLK, D).reshape(LAT_CHUNKS, L_BLK, D))
    perm = _perm_matrix()

    us, uf = _pre(xp4, xs4, pos3, modg, norm1_g[layer], w_in[layer], jnp.asarray(perm, BF16))

    bbmat, ccmat, a8 = _s5_params(s5_lam_re[layer], s5_lam_im[layer], s5_log_dt[layer], s5_b_re[layer],
                                  s5_b_im[layer], s5_c_re[layer], s5_c_im[layer])

    st = jnp.stack([state_s5_re[:, layer], state_s5_im[:, layer]], axis=2)
    st = st.reshape(n_lat, 2, 2, N_S5_CHUNKS, CH_P)
    st = jnp.transpose(st, (1, 3, 0, 2, 4)).reshape(2, N_S5_CHUNKS, n_lat, 2 * CH_P)
    init8 = jnp.zeros((2, N_S5_CHUNKS, n_lat, LAT_CHUNKS, 2 * CH_P), F32)
    init8 = init8.at[0, :, :, 0].set(st[0]).at[1, :, :, LAT_CHUNKS - 1].set(st[1])
    init8 = init8.reshape(2, N_S5_CHUNKS, SUB, 2 * CH_P)

    y_s5, fin = _s5(us, bbmat, ccmat, a8, init8, s5_d[layer])

    m1, m2 = _fnet_weights(w_fnet[layer])
    yf = _fnet(uf, m1, m2)

    x1, h2, comb = _post(y_s5, yf, xp4, xs4, pos3, modg, s5_w_glu[layer], w_out[layer], norm2_g[layer],
                         w_router[layer], b_router[layer], jnp.asarray(perm.T, BF16))

    y_prompt, y_lat = _moe_and_final(x1.reshape(N_BLK, TBLK, D), h2.reshape(T_TOK, D), comb.reshape(T_TOK, E_PAD),
                                     modg, norm_f_g, w_gate[layer], b_gate[layer], w_up[layer], b_up[layer],
                                     w_down[layer], b_down[layer])
    y_sample = y_lat.reshape(n_lat, LAT_CHUNKS * L_BLK, D)

    fin = fin[:, :N_GRP - 1].reshape(2, N_GRP - 1, SUB, N_S5_CHUNKS, 2, CH_P)
    fin = jnp.transpose(fin, (4, 1, 2, 0, 3, 5)).reshape(2, n_ctx, 1, 2, N_S5_GROUPS, S5_P)
    return (y_prompt, y_sample, fin[0], fin[1])
```

```python
import functools
import math

import numpy as np
import jax
import jax.numpy as jnp
from jax import lax
from jax.experimental import pallas as pl
from jax.experimental.pallas import tpu as pltpu

F32 = jnp.float32
BF16 = jnp.bfloat16
HIGHEST = lax.Precision.HIGHEST

D = 1024
D_S5 = 768
S5_H = 16
S5_P = 64
N_S5_GROUPS = 48
D_FN = 256
FN_GW = 64
N_EXPERTS = 32
TOP_K = 4
E_PAD = 128
SWIGLU_LIMIT = 7.0
SWIGLU_ALPHA = 1.702
RMS_EPS = 1e-6
POS_TEMP = 10000.0
GRID_W = 64

L_BLK = 256
SUB = 8
N_GRP = 3
GRP_ROWS = L_BLK * SUB
LAT_CHUNKS = 4
T_TOK = N_GRP * GRP_ROWS
TC = 128
N_TC = L_BLK // TC
PERM_T = 32
S5_CHUNK = 16
N_S5_CHUNKS = N_S5_GROUPS // S5_CHUNK
CH_U = S5_CHUNK * S5_H
CH_P = S5_CHUNK * S5_P
VMEM_LIMIT = 56 * 1024 * 1024


def _cparams(sem):
    return pltpu.CompilerParams(dimension_semantics=sem, vmem_limit_bytes=VMEM_LIMIT)


ADA_K = 128


def _adaln_kernel(c_ref, w_ref, b_ref, o_ref):
    @pl.when(pl.program_id(0) == 0)
    def _():
        o_ref[...] = jnp.broadcast_to(b_ref[...], o_ref.shape)

    c = c_ref[...]
    s = c * jax.nn.sigmoid(c)
    s_hi = s.astype(BF16)
    s_lo = (s - s_hi.astype(F32)).astype(BF16)
    w = w_ref[...]
    w_hi = w.astype(BF16)
    w_lo = (w - w_hi.astype(F32)).astype(BF16)
    o_ref[...] += (jnp.dot(s_hi, w_hi, preferred_element_type=F32) + jnp.dot(s_lo, w_hi, preferred_element_type=F32)
                   + jnp.dot(s_hi, w_lo, preferred_element_type=F32))


def _adaln(cvec, w_ada, b_ada):
    k_dim, n = w_ada.shape
    rows = cvec.shape[0]
    return pl.pallas_call(
        _adaln_kernel,
        grid=(k_dim // ADA_K,),
        in_specs=[pl.BlockSpec((rows, ADA_K), lambda k: (0, k)),
                  pl.BlockSpec((ADA_K, n), lambda k: (k, 0)),
                  pl.BlockSpec((1, n), lambda k: (0, 0))],
        out_specs=pl.BlockSpec((rows, n), lambda k: (0, 0)),
        out_shape=jax.ShapeDtypeStruct((rows, n), F32),
        compiler_params=_cparams(("arbitrary",)),
        name="adaln",
    )(cvec, w_ada, b_ada.reshape(1, n))


def _zoh(lam_re, lam_im, log_dt):
    dt = jnp.exp(log_dt)
    a_re = jnp.minimum(lam_re, -1e-4)
    a_im = lam_im
    mag = jnp.exp(a_re * dt)
    ab_re = mag * jnp.cos(a_im * dt)
    ab_im = mag * jnp.sin(a_im * dt)
    den = a_re * a_re + a_im * a_im
    nr = ab_re - 1.0
    f_re = (nr * a_re + ab_im * a_im) / den
    f_im = (ab_im * a_re - nr * a_im) / den
    return ab_re, ab_im, f_re, f_im


def _s5_params_kernel(lre3_ref, lim3_ref, ldt3_ref, lre2_ref, lim2_ref, ldt2_ref, bre_ref, bim_ref,
                      cre_ref, cim_ref, bb_ref, cc_ref, a_ref):
    exact = functools.partial(jnp.dot, preferred_element_type=F32, precision=HIGHEST)
    spread = jnp.where(lax.broadcasted_iota(jnp.int32, (S5_P, CH_P), 1) % S5_P
                       == lax.broadcasted_iota(jnp.int32, (S5_P, CH_P), 0), 1.0, 0.0)

    def block_diag(m):
        wide = jnp.dot(m.reshape(CH_U, S5_P).astype(BF16), spread.astype(BF16), preferred_element_type=F32)
        row_g = lax.broadcasted_iota(jnp.int32, (CH_U, CH_P), 0) // S5_H
        col_g = lax.broadcasted_iota(jnp.int32, (CH_U, CH_P), 1) // S5_P
        return jnp.where(row_g == col_g, wide, 0.0)

    _, _, f_re, f_im = _zoh(lre3_ref[0, 0], lim3_ref[0, 0], ldt3_ref[0, 0])
    b_re = bre_ref[0, 0]
    b_im = bim_ref[0, 0]
    bb_ref[0, 0] = jnp.concatenate([block_diag(f_re * b_re - f_im * b_im),
                                    block_diag(f_re * b_im + f_im * b_re)], axis=1).astype(BF16)
    cc_ref[0, 0] = jnp.concatenate([block_diag(cre_ref[0, 0]).T,
                                    -block_diag(cim_ref[0, 0]).T], axis=0).astype(BF16)

    ab_re, ab_im, _, _ = _zoh(lre2_ref[0, 0], lim2_ref[0, 0], ldt2_ref[0, 0])
    row_g = lax.broadcasted_iota(jnp.int32, (S5_CHUNK, CH_P), 0)
    col_g = lax.broadcasted_iota(jnp.int32, (S5_CHUNK, CH_P), 1) // S5_P

    def lane_row(a):
        flat = jnp.sum(jnp.where(row_g == col_g, exact(a, spread), 0.0), axis=0, keepdims=True)
        return jnp.broadcast_to(flat, (SUB, CH_P))

    a_ref[0, 0] = jnp.concatenate([lane_row(ab_re), lane_row(ab_im)], axis=1)


def _s5_params(lam_re, lam_im, log_dt, b_re, b_im, c_re, c_im):
    nc = N_S5_CHUNKS
    ldt = jnp.broadcast_to(log_dt[:, :, None], (2, N_S5_GROUPS, S5_P))
    g3 = lambda a: a.reshape(2, nc, S5_CHUNK, 1, S5_P)
    g2 = lambda a: a.reshape(2, nc, S5_CHUNK, S5_P)
    ghp = lambda a: a.reshape(2, nc, S5_CHUNK, S5_H, S5_P)
    spec3 = pl.BlockSpec((1, 1, S5_CHUNK, 1, S5_P), lambda d, c: (d, c, 0, 0, 0))
    spec2 = pl.BlockSpec((1, 1, S5_CHUNK, S5_P), lambda d, c: (d, c, 0, 0))
    spec_ghp = pl.BlockSpec((1, 1, S5_CHUNK, S5_H, S5_P), lambda d, c: (d, c, 0, 0, 0))
    return pl.pallas_call(
        _s5_params_kernel,
        grid=(2, nc),
        in_specs=[spec3, spec3, spec3, spec2, spec2, spec2, spec_ghp, spec_ghp, spec_ghp, spec_ghp],
        out_specs=[pl.BlockSpec((1, 1, CH_U, 2 * CH_P), lambda d, c: (d, c, 0, 0)),
                   pl.BlockSpec((1, 1, 2 * CH_P, CH_U), lambda d, c: (d, c, 0, 0)),
                   pl.BlockSpec((1, 1, SUB, 2 * CH_P), lambda d, c: (d, c, 0, 0))],
        out_shape=[jax.ShapeDtypeStruct((2, nc, CH_U, 2 * CH_P), BF16),
                   jax.ShapeDtypeStruct((2, nc, 2 * CH_P, CH_U), BF16),
                   jax.ShapeDtypeStruct((2, nc, SUB, 2 * CH_P), F32)],
        compiler_params=_cparams(("arbitrary", "arbitrary")),
        name="s5_params",
    )(g3(lam_re), g3(lam_im), g3(ldt), g2(lam_re), g2(lam_im), g2(ldt),
      ghp(jnp.swapaxes(b_re, -1, -2)), ghp(jnp.swapaxes(b_im, -1, -2)), ghp(c_re), ghp(c_im))


def _rms(x, g):
    return x * lax.rsqrt(jnp.mean(x * x, axis=-1, keepdims=True) + RMS_EPS) * g


def _load_x(g, xp_ref, xs_ref, pos_ref):
    is_lat = g == N_GRP - 1
    pos = pos_ref[...]
    pos8 = jnp.concatenate([pos, pos], axis=0)
    return jnp.where(is_lat, xs_ref[0] + pos8, xp_ref[0])


def _x_specs():
    n_ctx = N_GRP - 1
    xp_spec = pl.BlockSpec(
        (1, SUB, TC, D),
        lambda g, tc: (jnp.minimum(g, n_ctx - 1), 0, jnp.where(g >= n_ctx, N_TC - 1, tc), 0))
    xs_spec = pl.BlockSpec((1, SUB, TC, D), lambda g, tc: (0, 0, jnp.where(g >= n_ctx, tc, 0), 0))
    pos_spec = pl.BlockSpec((LAT_CHUNKS, TC, D), lambda g, tc: (0, tc, 0))
    return xp_spec, xs_spec, pos_spec


def _perm_matrix():
    p = np.zeros((SUB * PERM_T, SUB * PERM_T), np.float32)
    for s in range(SUB):
        for j in range(PERM_T):
            p[j * SUB + s, s * PERM_T + j] = 1.0
    return p


def _pre_kernel(xp_ref, xs_ref, pos_ref, mod_ref, g1_ref, win_ref, perm_ref, us_ref, uf_ref):
    g = pl.program_id(0)
    x = _load_x(g, xp_ref, xs_ref, pos_ref)
    mod = mod_ref[0]
    shift1 = mod[:, :, 0:D]
    scale1 = mod[:, :, D:2 * D]
    h = _rms(x, g1_ref[...]) * (1.0 + scale1) + shift1
    h2d = h.reshape(SUB * TC, D).astype(BF16)
    proj = jnp.dot(h2d, win_ref[...].astype(BF16), preferred_element_type=F32)
    uf_ref[0] = proj[:, D_S5:].astype(BF16).reshape(SUB, TC, D_FN)
    u = proj[:, :D_S5].astype(BF16)
    perm = perm_ref[...]
    for q in range(TC // PERM_T):
        piece = jnp.concatenate(
            [u[s * TC + q * PERM_T: s * TC + (q + 1) * PERM_T] for s in range(SUB)], axis=0)
        us_ref[0, q * SUB * PERM_T:(q + 1) * SUB * PERM_T, :] = jnp.dot(
            perm, piece, preferred_element_type=F32).astype(BF16)


def _pre(xp4, xs4, pos3, modg, norm1_g, w_in, perm):
    xp_spec, xs_spec, pos_spec = _x_specs()
    return pl.pallas_call(
        _pre_kernel,
        grid=(N_GRP, N_TC),
        in_specs=[xp_spec, xs_spec, pos_spec,
                  pl.BlockSpec((1, SUB, 1, 6 * D), lambda g, tc: (g, 0, 0, 0)),
                  pl.BlockSpec((1, D), lambda g, tc: (0, 0)),
                  pl.BlockSpec((D, D), lambda g, tc: (0, 0)),
                  pl.BlockSpec((SUB * PERM_T, SUB * PERM_T), lambda g, tc: (0, 0))],
        out_specs=[pl.BlockSpec((1, SUB * TC, D_S5), lambda g, tc: (g, tc, 0)),
                   pl.BlockSpec((1, SUB, TC, D_FN), lambda g, tc: (g, 0, tc, 0))],
        out_shape=[jax.ShapeDtypeStruct((N_GRP, GRP_ROWS, D_S5), BF16),
                   jax.ShapeDtypeStruct((N_GRP, SUB, L_BLK, D_FN), BF16)],
        compiler_params=_cparams(("arbitrary", "arbitrary")),
        name="pre_mixer",
    )(xp4, xs4, pos3, modg, norm1_g.reshape(1, D), w_in, perm)


def _cmul(ar, ai, br, bi):
    return ar * br - ai * bi, ar * bi + ai * br


S5_BLK = 512
S5_NBLK = GRP_ROWS // S5_BLK
S5_STEPS = S5_BLK // SUB
S5_FIX_UNROLL = 8


def _s5_kernel(us_ref, bb_ref, cc_ref, a_ref, init_ref, dskip_ref, y_ref, fin_ref, in_a, in_b, out_a, out_b):
    g = pl.program_id(0)
    d = pl.program_id(2)
    is_lat = g == N_GRP - 1
    re_all = slice(0, CH_P)
    im_all = slice(CH_P, 2 * CH_P)
    a_re = a_ref[0, 0, :, re_all]
    a_im = a_ref[0, 0, :, im_all]
    bufs_in = (in_a, in_b)
    bufs_out = (out_a, out_b)

    def time_rows(k):
        blk = k + d * (S5_NBLK - 1 - 2 * k)
        return pl.ds(pl.multiple_of(blk * S5_BLK, S5_BLK), S5_BLK)

    def slot_rows(k):
        return slice((k // 2) * S5_BLK, (k // 2 + 1) * S5_BLK)

    def step_rows(k, j):
        q = j + d * (S5_STEPS - 1 - 2 * j)
        return pl.ds(pl.multiple_of((k // 2) * S5_BLK + q * SUB, SUB), SUB)

    def proj_in(k):
        bufs_in[k % 2][slot_rows(k), :] = jnp.dot(us_ref[0, time_rows(k), :], bb_ref[0, 0],
                                                  preferred_element_type=F32)

    def scan_block(k, carry):
        src, dst = bufs_in[k % 2], bufs_out[k % 2]
        s_re, s_im = carry
        for j in range(S5_STEPS):
            rows = step_rows(k, j)
            n_re = a_re * s_re - a_im * s_im + src[rows, re_all]
            n_im = a_re * s_im + a_im * s_re + src[rows, im_all]
            dst[rows, re_all] = n_re
            dst[rows, im_all] = n_im
            s_re, s_im = n_re, n_im
        return s_re, s_im

    def proj_out(k):
        rows = time_rows(k)
        yb = jnp.dot(bufs_out[k % 2][slot_rows(k), :].astype(BF16), cc_ref[0, 0], preferred_element_type=F32)
        y_ref[0, rows, :] = y_ref[0, rows, :] + yb

    @pl.when(d == 0)
    def _():
        y_ref[0] = dskip_ref[...] * us_ref[0].astype(F32)

    def scan_all(carry, with_out):
        proj_in(0)
        for k in range(S5_NBLK):
            if k + 1 < S5_NBLK:
                proj_in(k + 1)
            carry = scan_block(k, carry)
            if with_out and k >= 1:
                proj_out(k - 1)
        if with_out:
            proj_out(S5_NBLK - 1)
        fin_ref[0, 0, :, re_all] = carry[0]
        fin_ref[0, 0, :, im_all] = carry[1]
        return carry

    @pl.when(jnp.logical_not(is_lat))
    def _():
        zero = jnp.zeros((SUB, CH_P), F32)
        scan_all((zero, zero), True)

    @pl.when(is_lat)
    def _():
        f_re, f_im = scan_all((init_ref[0, 0, :, re_all], init_ref[0, 0, :, im_all]), False)

        sub_id = lax.broadcasted_iota(jnp.int32, (SUB, CH_P), 0) % LAT_CHUNKS
        fwd = d == 0
        lo = jnp.where(fwd, 1, 0)
        hi = jnp.where(fwd, LAT_CHUNKS - 1, LAT_CHUNKS - 2)
        keep = (sub_id >= lo) & (sub_id <= hi)

        def from_prev(v):
            return jnp.where(keep, jnp.where(fwd, pltpu.roll(v, 1, 0), pltpu.roll(v, SUB - 1, 0)), 0.0)

        p_re, p_im = a_re, a_im
        for _ in range(8):
            p_re, p_im = _cmul(p_re, p_im, p_re, p_im)
        t_re, t_im = f_re, f_im
        for _ in range(LAT_CHUNKS - 2):
            m_re, m_im = _cmul(p_re, p_im, from_prev(t_re), from_prev(t_im))
            t_re, t_im = f_re + m_re, f_im + m_im
        corr = (from_prev(t_re), from_prev(t_im))

        for k in range(S5_NBLK):
            dst = bufs_out[k % 2]

            def fix_body(i, carry, k=k, dst=dst):
                m_re, m_im = carry
                for j in range(S5_FIX_UNROLL):
                    rows = step_rows(k, i * S5_FIX_UNROLL + j)
                    m_re, m_im = _cmul(m_re, m_im, a_re, a_im)
                    dst[rows, re_all] = dst[rows, re_all] + m_re
                    dst[rows, im_all] = dst[rows, im_all] + m_im
                return m_re, m_im

            corr = lax.fori_loop(0, S5_STEPS // S5_FIX_UNROLL, fix_body, corr)

        for k in range(S5_NBLK):
            proj_out(k)


def _s5(us, bbmat, ccmat, a8, init8, d_skip):
    return pl.pallas_call(
        _s5_kernel,
        grid=(N_GRP, N_S5_CHUNKS, 2),
        in_specs=[pl.BlockSpec((1, GRP_ROWS, CH_U), lambda g, c, d: (g, 0, c)),
                  pl.BlockSpec((1, 1, CH_U, 2 * CH_P), lambda g, c, d: (d, c, 0, 0)),
                  pl.BlockSpec((1, 1, 2 * CH_P, CH_U), lambda g, c, d: (d, c, 0, 0)),
                  pl.BlockSpec((1, 1, SUB, 2 * CH_P), lambda g, c, d: (d, c, 0, 0)),
                  pl.BlockSpec((1, 1, SUB, 2 * CH_P), lambda g, c, d: (d, c, 0, 0)),
                  pl.BlockSpec((1, CH_U), lambda g, c, d: (0, c))],
        out_specs=[pl.BlockSpec((1, GRP_ROWS, CH_U), lambda g, c, d: (g, 0, c)),
                   pl.BlockSpec((1, 1, SUB, 2 * CH_P), lambda g, c, d: (d, g, 0, c))],
        out_shape=[jax.ShapeDtypeStruct((N_GRP, GRP_ROWS, D_S5), F32),
                   jax.ShapeDtypeStruct((2, N_GRP, SUB, N_S5_CHUNKS * 2 * CH_P), F32)],
        scratch_shapes=[pltpu.VMEM((GRP_ROWS // 2, 2 * CH_P), F32)] * 4,
        compiler_params=_cparams(("arbitrary", "arbitrary", "arbitrary")),
        name="s5_scan",
    )(us, bbmat, ccmat, a8, init8, d_skip.reshape(1, D_S5))


def _fnet_w_kernel(c_ref, s_ref, w_ref, m1_ref, m2_ref):
    w = w_ref[...]
    m1_ref[...] = jnp.dot(c_ref[...], w, preferred_element_type=F32, precision=HIGHEST).astype(BF16)
    m2_ref[...] = jnp.dot(s_ref[...], w, preferred_element_type=F32, precision=HIGHEST).astype(BF16)


FN_SEQ = LAT_CHUNKS
FN_CTX_STEPS = (N_GRP - 1) * SUB // FN_SEQ


def _fnet_kernel(u_ref, cos_s_ref, sin_s_ref, cos_l_ref, sin_l_ref, m1_ref, m2_ref, o_ref):
    i = pl.program_id(0)
    m1 = m1_ref[...]
    m2 = m2_ref[...]

    def mix(u, cos_ref, sin_ref):
        v1 = jnp.dot(u, m1, preferred_element_type=F32).astype(BF16)
        v2 = jnp.dot(u, m2, preferred_element_type=F32).astype(BF16)
        return (jnp.dot(cos_ref[...].astype(BF16), v1, preferred_element_type=F32)
                - jnp.dot(sin_ref[...].astype(BF16), v2, preferred_element_type=F32)).astype(BF16)

    @pl.when(i < FN_CTX_STEPS)
    def _():
        for s in range(FN_SEQ):
            o_ref[0, s] = mix(u_ref[0, s], cos_s_ref, sin_s_ref)

    @pl.when(i >= FN_CTX_STEPS)
    def _():
        u = u_ref[0].reshape(FN_SEQ * L_BLK, D_FN)
        o_ref[0] = mix(u, cos_l_ref, sin_l_ref).reshape(FN_SEQ, L_BLK, D_FN)


def _dft_tables(n):
    k = np.arange(n, dtype=np.int64)
    ang = (2.0 * np.pi / n) * ((k[:, None] * k[None, :]) % n).astype(np.float64)
    scale = 1.0 / math.sqrt(n)
    return (np.cos(ang) * scale).astype(np.float32), (np.sin(ang) * scale).astype(np.float32)


def _fnet(uf, m1, m2):
    n_lat_steps = SUB // FN_SEQ
    cos_s, sin_s = _dft_tables(L_BLK)
    cos_l, sin_l = _dft_tables(FN_SEQ * L_BLK)
    full = lambda i: (0, 0)
    per_grp = SUB // FN_SEQ
    blk = pl.BlockSpec(
        (1, FN_SEQ, L_BLK, D_FN),
        lambda i: (jnp.minimum(i // per_grp, N_GRP - 1),
                   jnp.where(i < FN_CTX_STEPS, i % per_grp, i - FN_CTX_STEPS), 0, 0))
    table = lambda n: pl.BlockSpec((n, n), full)
    return pl.pallas_call(
        _fnet_kernel,
        grid=(FN_CTX_STEPS + n_lat_steps,),
        in_specs=[blk, table(L_BLK), table(L_BLK), table(FN_SEQ * L_BLK), table(FN_SEQ * L_BLK),
                  table(D_FN), table(D_FN)],
        out_specs=blk,
        out_shape=jax.ShapeDtypeStruct(uf.shape, BF16),
        compiler_params=_cparams(("arbitrary",)),
        name="fnet",
    )(uf, jnp.asarray(cos_s), jnp.asarray(sin_s), jnp.asarray(cos_l), jnp.asarray(sin_l), m1, m2)


def _fnet_weights(w_fnet):
    n_g = D_FN // FN_GW
    cos_c, sin_c = _dft_tables(FN_GW)
    eye = np.eye(n_g, dtype=np.float32)
    cos_bd = np.kron(eye, cos_c)
    sin_bd = np.kron(eye, sin_c)
    w_bd = (w_fnet[:, :, None, :] * jnp.asarray(eye)[:, None, :, None]).reshape(D_FN, D_FN)
    out = jax.ShapeDtypeStruct((D_FN, D_FN), BF16)
    return pl.pallas_call(_fnet_w_kernel, out_shape=(out, out), name="fnet_weights")(
        jnp.asarray(cos_bd), jnp.asarray(sin_bd), w_bd)


def _gelu_tanh(x):
    return 0.5 * x * (1.0 + jnp.tanh(math.sqrt(2.0 / math.pi) * (x + 0.044715 * (x * x * x))))


def _post_kernel(y_ref, yf_ref, xp_ref, xs_ref, pos_ref, mod_ref, wglu_ref, wout_ref, g2_ref, wr_ref,
                 br_ref, permt_ref, x1_ref, h2_ref, comb_ref):
    g = pl.program_id(0)
    z = _gelu_tanh(y_ref[0])
    gate = jnp.dot(z.astype(BF16), wglu_ref[...].astype(BF16), preferred_element_type=F32)
    gl = (z * jax.nn.sigmoid(gate)).astype(BF16)
    permt = permt_ref[...]
    n_q = TC // PERM_T
    nat = [jnp.dot(permt, gl[q * SUB * PERM_T:(q + 1) * SUB * PERM_T], preferred_element_type=F32).astype(BF16)
           for q in range(n_q)]
    gl_nat = jnp.concatenate(
        [nat[q][s * PERM_T:(s + 1) * PERM_T] for s in range(SUB) for q in range(n_q)], axis=0)
    w_out = wout_ref[...].astype(BF16)
    mixed = (jnp.dot(gl_nat, w_out[:D_S5], preferred_element_type=F32)
             + jnp.dot(yf_ref[0].reshape(SUB * TC, D_FN), w_out[D_S5:], preferred_element_type=F32))
    x = _load_x(g, xp_ref, xs_ref, pos_ref)
    mod = mod_ref[0]
    gate1 = mod[:, :, 2 * D:3 * D]
    shift2 = mod[:, :, 3 * D:4 * D]
    scale2 = mod[:, :, 4 * D:5 * D]
    x1 = x + gate1 * mixed.reshape(SUB, TC, D)
    x1_ref[0] = x1
    h2 = _rms(x1, g2_ref[...]) * (1.0 + scale2) + shift2
    h2_ref[0] = h2.astype(BF16)

    hr = h2.reshape(SUB * TC, D)
    h_hi = hr.astype(BF16)
    h_lo = (hr - h_hi.astype(F32)).astype(BF16)
    wr = wr_ref[...]
    w_hi = wr.astype(BF16)
    w_lo = (wr - w_hi.astype(F32)).astype(BF16)
    both = jnp.dot(h_hi, jnp.concatenate([w_hi, w_lo], axis=1), preferred_element_type=F32)
    logits = (both[:, :E_PAD] + jnp.dot(h_lo, w_hi, preferred_element_type=F32) + both[:, E_PAD:]
              + br_ref[...])
    lane = lax.broadcasted_iota(jnp.int32, logits.shape, 1).astype(F32)
    top_v, hots = [], []
    cur = logits
    for _ in range(TOP_K):
        m = jnp.max(cur, axis=-1, keepdims=True)
        idx = jnp.min(jnp.where(cur == m, lane, float(E_PAD)), axis=-1, keepdims=True)
        hot = lane == idx
        top_v.append(m)
        hots.append(hot)
        cur = jnp.where(hot, -3.0e38, cur)
    exps = [jnp.exp(v - top_v[0]) for v in top_v]
    denom = exps[0] + exps[1] + exps[2] + exps[3]
    comb = jnp.zeros(logits.shape, F32)
    for k in range(TOP_K):
        comb = comb + jnp.where(hots[k], exps[k] / denom, 0.0)
    comb_ref[0] = comb.reshape(SUB, TC, E_PAD)


def _post(y, yf, xp4, xs4, pos3, modg, w_glu, w_out, norm2_g, w_router, b_router, permt):
    xp_spec, xs_spec, pos_spec = _x_specs()
    const2 = lambda g, tc: (0, 0)
    wr = jnp.zeros((D, E_PAD), F32).at[:, :N_EXPERTS].set(w_router)
    br = jnp.full((1, E_PAD), -1.0e30, F32).at[0, :N_EXPERTS].set(b_router)
    blk = lambda w: pl.BlockSpec((1, SUB, TC, w), lambda g, tc: (g, 0, tc, 0))
    return pl.pallas_call(
        _post_kernel,
        grid=(N_GRP, N_TC),
        in_specs=[pl.BlockSpec((1, SUB * TC, D_S5), lambda g, tc: (g, tc, 0)),
                  blk(D_FN), xp_spec, xs_spec, pos_spec,
                  pl.BlockSpec((1, SUB, 1, 6 * D), lambda g, tc: (g, 0, 0, 0)),
                  pl.BlockSpec((D_S5, D_S5), const2),
                  pl.BlockSpec((D, D), const2),
                  pl.BlockSpec((1, D), const2),
                  pl.BlockSpec((D, E_PAD), const2),
                  pl.BlockSpec((1, E_PAD), const2),
                  pl.BlockSpec((SUB * PERM_T, SUB * PERM_T), const2)],
        out_specs=[blk(D), blk(D), blk(E_PAD)],
        out_shape=[jax.ShapeDtypeStruct((N_GRP, SUB, L_BLK, D), F32),
                   jax.ShapeDtypeStruct((N_GRP, SUB, L_BLK, D), BF16),
                   jax.ShapeDtypeStruct((N_GRP, SUB, L_BLK, E_PAD), F32)],
        compiler_params=_cparams(("arbitrary", "arbitrary")),
        name="post_mixer",
    )(y, yf, xp4, xs4, pos3, modg, w_glu, w_out, norm2_g.reshape(1, D), wr, br, permt)


TBLK = L_BLK
N_BLK = T_TOK // TBLK
N_CTX_BLK = (N_GRP - 1) * SUB
SEG_ALIGN = 8
RB = 1280
assert RB >= TBLK * TOP_K + N_EXPERTS * (SEG_ALIGN - 1) and RB % 128 == 0
TM = 256
TM_SHIFT = 8
assert 1 << TM_SHIFT == TM
R_TOT = T_TOK * TOP_K + N_BLK * N_EXPERTS * (SEG_ALIGN - 1) + TM
TAB_ROWS = 32
assert TAB_ROWS >= N_BLK
BIG_ROWS = 32
BIG_SHIFT = 2
assert SEG_ALIGN << BIG_SHIFT == BIG_ROWS


def _plan_kernel(comb_ref, z_ref, ptab_ref, loff_ref, loff_i_ref, gst_i_ref, p8_i_ref, eoff_i_ref):
    row = lax.broadcasted_iota(jnp.int32, (TBLK, TBLK), 0)
    col = lax.broadcasted_iota(jnp.int32, (TBLK, TBLK), 1)
    earlier = jnp.where(row > col, 1.0, 0.0).astype(BF16)
    ptab_ref[...] = jnp.zeros_like(ptab_ref)

    def body(b, _):
        rows = pl.ds(pl.multiple_of(b * TBLK, TBLK), TBLK)
        m = jnp.where(comb_ref[rows, :] > 0.0, 1.0, 0.0)
        rank = jnp.dot(earlier, m.astype(BF16), preferred_element_type=F32)
        z_ref[rows, :] = m * (rank + 1.0)
        n = jnp.sum(m, axis=0, keepdims=True)
        ptab_ref[pl.ds(b, 1), :] = jnp.floor((n + (SEG_ALIGN - 1)) * (1.0 / SEG_ALIGN)) * SEG_ALIGN
        return 0

    lax.fori_loop(0, N_BLK, body, 0)

    ptab = ptab_ref[...]
    er = lax.broadcasted_iota(jnp.int32, (E_PAD, E_PAD), 0)
    ec = lax.broadcasted_iota(jnp.int32, (E_PAD, E_PAD), 1)
    before = jnp.where(er < ec, 1.0, 0.0)
    exact = functools.partial(jnp.dot, preferred_element_type=F32, precision=HIGHEST)
    loff = exact(ptab, before)
    loff_ref[...] = loff
    as_int = lambda t: t[:, :N_EXPERTS].astype(jnp.int32)
    loff_i_ref[...] = as_int(loff)
    p8_i_ref[...] = as_int(ptab * (1.0 / SEG_ALIGN))
    tot = jnp.sum(ptab, axis=0, keepdims=True)
    eoff = exact(jnp.broadcast_to(tot, (SUB, E_PAD)), before)
    eoff_i_ref[...] = eoff.astype(jnp.int32)
    br = lax.broadcasted_iota(jnp.int32, (TAB_ROWS, TAB_ROWS), 0)
    bc = lax.broadcasted_iota(jnp.int32, (TAB_ROWS, TAB_ROWS), 1)
    gst_i_ref[...] = as_int(eoff[0:1] + exact(jnp.where(br > bc, 1.0, 0.0), ptab))


def _plan(comb):
    tab = jax.ShapeDtypeStruct((TAB_ROWS, E_PAD), F32)
    itab = jax.ShapeDtypeStruct((TAB_ROWS, N_EXPERTS), jnp.int32)
    return pl.pallas_call(
        _plan_kernel,
        out_shape=(jax.ShapeDtypeStruct((T_TOK, E_PAD), F32), tab, tab, itab, itab, itab,
                   jax.ShapeDtypeStruct((SUB, E_PAD), jnp.int32)),
        compiler_params=pltpu.CompilerParams(vmem_limit_bytes=VMEM_LIMIT),
        name="moe_plan",
    )(comb)


def _sort_matrix(b, z_ref, loff_ref, ptab_ref):
    loff = loff_ref[pl.ds(b, 1), :]
    size = ptab_ref[pl.ds(b, 1), :]
    r = lax.broadcasted_iota(jnp.int32, (RB, E_PAD), 0).astype(F32)
    owner = jnp.where(r >= loff, jnp.where(r < loff + size, 1.0, 0.0), 0.0)
    rank1 = r[:, 0:1] - jnp.sum(owner * loff, axis=-1, keepdims=True) + 1.0
    zt = z_ref[...].T
    v = jnp.dot(owner.astype(BF16), zt.astype(BF16), preferred_element_type=F32)
    return jnp.where(v == rank1, 1.0, 0.0), owner


def _segment_copies(b, loff_s, gst_s, p8_s, make_copy):
    def per_expert(e, counts):
        chunks = p8_s[b, e]
        n_big = lax.shift_right_logical(chunks, BIG_SHIFT)
        n_small = chunks - n_big * (BIG_ROWS // SEG_ALIGN)
        local0 = loff_s[b, e]
        global0 = gst_s[b, e]

        def big(j, _):
            make_copy(pl.multiple_of(local0 + j * BIG_ROWS, SEG_ALIGN),
                      pl.multiple_of(global0 + j * BIG_ROWS, SEG_ALIGN), BIG_ROWS).start()
            return 0

        def small(j, _):
            off = n_big * BIG_ROWS + j * SEG_ALIGN
            make_copy(pl.multiple_of(local0 + off, SEG_ALIGN),
                      pl.multiple_of(global0 + off, SEG_ALIGN), SEG_ALIGN).start()
            return 0

        lax.fori_loop(0, n_big, big, 0)
        lax.fori_loop(0, n_small, small, 0)
        return counts[0] + n_big, counts[1] + n_small

    return lax.fori_loop(0, N_EXPERTS, per_expert, (0, 0))


def _wait_copies(counts, make_copy):
    for n, rows in zip(counts, (BIG_ROWS, SEG_ALIGN)):
        def wait_one(i, _, rows=rows):
            make_copy(0, 0, rows).wait()
            return 0
        lax.fori_loop(0, n, wait_one, 0)


def _zero_rows_from(hbm, zeros_vmem, first_row, sem):
    def copy_to(row):
        return pltpu.make_async_copy(zeros_vmem, hbm.at[pl.ds(row, TM)], sem)

    n_full = lax.shift_right_logical(R_TOT - first_row, TM_SHIFT)

    def start_one(j, _):
        copy_to(pl.multiple_of(first_row + j * TM, SEG_ALIGN)).start()
        return 0

    def wait_one(j, _):
        copy_to(0).wait()
        return 0

    lax.fori_loop(0, n_full, start_one, 0)
    lax.fori_loop(0, n_full, wait_one, 0)
    last = copy_to(R_TOT - TM)
    last.start()
    last.wait()


def _dispatch_kernel(loff_s, gst_s, p8_s, eoff_s, h_ref, z_ref, loff_ref, ptab_ref, xs_hbm, xbuf, sem, pending):
    b = pl.program_id(0)
    slot = lax.rem(b, 2)

    def copies_of(s):
        def make_copy(local_row, global_row, rows):
            return pltpu.make_async_copy(xbuf.at[s, pl.ds(local_row, rows)], xs_hbm.at[pl.ds(global_row, rows)],
                                         sem.at[s])
        return make_copy

    pm = _sort_matrix(b, z_ref, loff_ref, ptab_ref)[0].astype(BF16)

    @pl.when(b >= 2)
    def _():
        _wait_copies((pending[2 * slot], pending[2 * slot + 1]), copies_of(slot))

    xbuf[slot] = jnp.dot(pm, h_ref[...], preferred_element_type=F32)
    n_big, n_small = _segment_copies(b, loff_s, gst_s, p8_s, copies_of(slot))
    pending[2 * slot] = n_big
    pending[2 * slot + 1] = n_small

    @pl.when(b == N_BLK - 1)
    def _():
        _wait_copies((n_big, n_small), copies_of(slot))
        _wait_copies((pending[2 * (1 - slot)], pending[2 * (1 - slot) + 1]), copies_of(1 - slot))
        xbuf[0, 0:TM, :] = jnp.zeros((TM, D), F32)
        _zero_rows_from(xs_hbm, xbuf.at[0, 0:TM], eoff_s[0, N_EXPERTS], sem.at[0])


def _dispatch(h2, z, loff, ptab, loff_i, gst_i, p8_i, eoff_i):
    whole = pl.BlockSpec((TAB_ROWS, E_PAD), lambda b, *_: (0, 0))
    return pl.pallas_call(
        _dispatch_kernel,
        grid_spec=pltpu.PrefetchScalarGridSpec(
            num_scalar_prefetch=4,
            grid=(N_BLK,),
            in_specs=[pl.BlockSpec((TBLK, D), lambda b, *_: (b, 0)),
                      pl.BlockSpec((TBLK, E_PAD), lambda b, *_: (b, 0)),
                      whole, whole],
            out_specs=pl.BlockSpec(memory_space=pl.ANY),
            scratch_shapes=[pltpu.VMEM((2, RB, D), F32), pltpu.SemaphoreType.DMA((2,)),
                            pltpu.SMEM((4,), jnp.int32)]),
        out_shape=jax.ShapeDtypeStruct((R_TOT, D), F32),
        compiler_params=_cparams(("arbitrary",)),
        name="moe_dispatch",
    )(loff_i, gst_i, p8_i, eoff_i, h2, z, loff, ptab)


MAX_TILES = R_TOT // TM + N_EXPERTS
X_AHEAD = 3
X_SLOTS = X_AHEAD + 1
Y_SLOTS = 2
W_AHEAD = 2
W_SLOTS = W_AHEAD + 1
N_MATS = 3
FF_CHUNK = 256
CAST_ROWS = 64
TILE_STEP = 64


def _expert_kernel(eoff_s, xs_hbm, wg_hbm, wu_hbm, wd_hbm, bg_ref, bu_ref, bd_ref, ys_hbm,
                   wst, wbf, xin, yout, act_ref, w_sem, x_sem, y_sem, t_exp, t_row, t_first, t_valid, live):
    def add_expert(e, carry):
        n_t, n_live = carry
        start = eoff_s[0, e]
        count = eoff_s[0, e + 1] - start
        tiles = lax.shift_right_logical(count + (TM - 1), TM_SHIFT)

        def add_tile(i, _):
            t_exp[n_t + i] = e
            t_row[n_t + i] = start + i * TM
            t_first[n_t + i] = jnp.where(i == 0, 1, 0)
            t_valid[n_t + i] = jnp.minimum(count - i * TM, TM)
            return 0

        lax.fori_loop(0, tiles, add_tile, 0)
        has_rows = jnp.where(tiles > 0, 1, 0)

        @pl.when(tiles > 0)
        def _():
            live[n_live] = e

        return n_t + tiles, n_live + has_rows

    n_tiles, n_live = lax.fori_loop(0, N_EXPERTS, add_expert, (0, 0))

    def w_copies(e, slot):
        return [pltpu.make_async_copy(w.at[e], wst.at[slot, m], w_sem.at[slot])
                for m, w in enumerate((wg_hbm, wu_hbm, wd_hbm))]

    def tile_rows(j):
        return pl.ds(pl.multiple_of(t_row[j], SEG_ALIGN), TM)

    def x_copy(j, slot):
        return pltpu.make_async_copy(xs_hbm.at[tile_rows(j)], xin.at[slot], x_sem.at[slot])

    def y_pieces(j, op):
        ys = lax.rem(j, Y_SLOTS)
        valid = t_valid[j]
        row0 = t_row[j]

        def piece(off, rows):
            return pltpu.make_async_copy(yout.at[ys, pl.ds(off, rows)],
                                         ys_hbm.at[pl.ds(pl.multiple_of(row0 + off, SEG_ALIGN), rows)],
                                         y_sem.at[ys])

        @pl.when(valid == TM)
        def _():
            op(piece(0, TM))

        @pl.when(valid < TM)
        def _():
            n_big = lax.shift_right_logical(valid, BIG_ROWS.bit_length() - 1)
            n_small = lax.shift_right_logical(valid - n_big * BIG_ROWS, SEG_ALIGN.bit_length() - 1)

            def big(i, _):
                op(piece(pl.multiple_of(i * BIG_ROWS, SEG_ALIGN), BIG_ROWS))
                return 0

            def small(i, _):
                op(piece(pl.multiple_of(n_big * BIG_ROWS + i * SEG_ALIGN, SEG_ALIGN), SEG_ALIGN))
                return 0

            lax.fori_loop(0, n_big, big, 0)
            lax.fori_loop(0, n_small, small, 0)

    for ahead in range(W_AHEAD):
        @pl.when(n_live > ahead)
        def _(ahead=ahead):
            for cp in w_copies(live[ahead], ahead):
                cp.start()

    for ahead in range(X_AHEAD):
        @pl.when(n_tiles > ahead)
        def _(ahead=ahead):
            x_copy(ahead, ahead).start()

    def body(j, k):
        e = t_exp[j]

        @pl.when(t_first[j] == 1)
        def _():
            ws = lax.rem(k, W_SLOTS)

            @pl.when(k + W_AHEAD < n_live)
            def _():
                for cp in w_copies(live[k + W_AHEAD], lax.rem(k + W_AHEAD, W_SLOTS)):
                    cp.start()

            for cp in w_copies(e, ws):
                cp.wait()

            def cast_rows(r, _):
                rows = pl.ds(pl.multiple_of(r * CAST_ROWS, CAST_ROWS), CAST_ROWS)
                for m in range(N_MATS):
                    wbf[m, rows, :] = wst[ws, m, rows, :].astype(BF16)
                return 0

            lax.fori_loop(0, D // CAST_ROWS, cast_rows, 0)

        slot = lax.rem(j, X_SLOTS)
        x_copy(j, slot).wait()

        @pl.when(j + X_AHEAD < n_tiles)
        def _():
            x_copy(j + X_AHEAD, lax.rem(j + X_AHEAD, X_SLOTS)).start()

        @pl.when(j >= Y_SLOTS)
        def _():
            y_pieces(j - Y_SLOTS, lambda cp: cp.wait())

        ys = lax.rem(j, Y_SLOTS)

        def compute(rows):
            x = xin[slot, 0:rows, :].astype(BF16)
            for c in range(D // FF_CHUNK):
                cols = slice(c * FF_CHUNK, (c + 1) * FF_CHUNK)
                gate = jnp.dot(x, wbf[0, :, cols], preferred_element_type=F32) + bg_ref[e][:, cols]
                up = jnp.dot(x, wbf[1, :, cols], preferred_element_type=F32) + bu_ref[e][:, cols]
                gate = jnp.minimum(gate, SWIGLU_LIMIT)
                up = jnp.clip(up, -SWIGLU_LIMIT, SWIGLU_LIMIT)
                act_ref[0:rows, cols] = ((up + 1.0) * gate * jax.nn.sigmoid(SWIGLU_ALPHA * gate)).astype(BF16)
            yout[ys, 0:rows, :] = jnp.dot(act_ref[0:rows, :], wbf[2], preferred_element_type=F32) + bd_ref[e]

        valid = t_valid[j]
        for rows in range(TILE_STEP, TM + 1, TILE_STEP):
            @pl.when(jnp.logical_and(valid > rows - TILE_STEP, valid <= rows))
            def _(rows=rows):
                compute(rows)

        y_pieces(j, lambda cp: cp.start())
        return k + t_first[j]

    lax.fori_loop(0, n_tiles, body, 0)

    for back in range(Y_SLOTS, 0, -1):
        @pl.when(n_tiles >= back)
        def _(back=back):
            y_pieces(n_tiles - back, lambda cp: cp.wait())

    yout[0] = jnp.zeros((TM, D), F32)
    _zero_rows_from(ys_hbm, yout.at[0], eoff_s[0, N_EXPERTS], y_sem.at[0])


def _experts(xs, eoff_i, w_gate, b_gate, w_up, b_up, w_down, b_down):
    hbm = pl.BlockSpec(memory_space=pl.ANY)
    bspec = pl.BlockSpec((N_EXPERTS, 1, D), lambda i, *_: (0, 0, 0))
    return pl.pallas_call(
        _expert_kernel,
        grid_spec=pltpu.PrefetchScalarGridSpec(
            num_scalar_prefetch=1,
            grid=(1,),
            in_specs=[hbm, hbm, hbm, hbm, bspec, bspec, bspec],
            out_specs=hbm,
            scratch_shapes=[pltpu.VMEM((W_SLOTS, N_MATS, D, D), F32), pltpu.VMEM((N_MATS, D, D), BF16),
                            pltpu.VMEM((X_SLOTS, TM, D), F32), pltpu.VMEM((Y_SLOTS, TM, D), F32),
                            pltpu.VMEM((TM, D), BF16),
                            pltpu.SemaphoreType.DMA((W_SLOTS,)), pltpu.SemaphoreType.DMA((X_SLOTS,)),
                            pltpu.SemaphoreType.DMA((Y_SLOTS,)),
                            pltpu.SMEM((MAX_TILES,), jnp.int32), pltpu.SMEM((MAX_TILES,), jnp.int32),
                            pltpu.SMEM((MAX_TILES,), jnp.int32), pltpu.SMEM((MAX_TILES,), jnp.int32),
                            pltpu.SMEM((N_EXPERTS,), jnp.int32)]),
        out_shape=jax.ShapeDtypeStruct((R_TOT, D), F32),
        compiler_params=_cparams(("arbitrary",)),
        name="moe_experts",
    )(eoff_i, xs, w_gate, w_up, w_down, b_gate.reshape(N_EXPERTS, 1, D), b_up.reshape(N_EXPERTS, 1, D),
      b_down.reshape(N_EXPERTS, 1, D))


def _combine_kernel(loff_s, gst_s, p8_s, ys_hbm, z_ref, comb_ref, loff_ref, ptab_ref, x1_ref, mod_ref, gf_ref,
                    yctx_ref, ylat_ref, ybuf, sem, pending):
    b = pl.program_id(0)
    slot = lax.rem(b, 2)

    def copies_of(s):
        def make_copy(local_row, global_row, rows):
            return pltpu.make_async_copy(ys_hbm.at[pl.ds(global_row, rows)], ybuf.at[s, pl.ds(local_row, rows)],
                                         sem.at[s])
        return make_copy

    def fetch(blk, s):
        n_big, n_small = _segment_copies(blk, loff_s, gst_s, p8_s, copies_of(s))
        pending[2 * s] = n_big
        pending[2 * s + 1] = n_small
        used_chunks = lax.shift_right_logical(loff_s[blk, N_EXPERTS - 1], 3) + p8_s[blk, N_EXPERTS - 1]

        def zero_chunk(j, _):
            ybuf[s, pl.ds(pl.multiple_of(j * SEG_ALIGN, SEG_ALIGN), SEG_ALIGN), :] = jnp.zeros((SEG_ALIGN, D), F32)
            return 0

        lax.fori_loop(used_chunks, RB // SEG_ALIGN, zero_chunk, 0)

    @pl.when(b == 0)
    def _():
        fetch(0, 0)

    @pl.when(b + 1 < N_BLK)
    def _():
        fetch(b + 1, 1 - slot)

    pm, owner = _sort_matrix(b, z_ref, loff_ref, ptab_ref)
    comb = comb_ref[...]
    c_hi = comb.astype(BF16)
    rest = comb - c_hi.astype(F32)
    c_mid = rest.astype(BF16)
    c_lo = (rest - c_mid.astype(F32)).astype(BF16)
    pmb = pm.astype(BF16)
    moved = (jnp.dot(pmb, c_hi, preferred_element_type=F32) + jnp.dot(pmb, c_mid, preferred_element_type=F32)
             + jnp.dot(pmb, c_lo, preferred_element_type=F32))
    w_row = jnp.sum(owner * moved, axis=-1, keepdims=True)
    pmt = pm.T.astype(BF16)
    _wait_copies((pending[2 * slot], pending[2 * slot + 1]), copies_of(slot))
    y = (ybuf[slot] * w_row).astype(BF16)
    moe = jnp.dot(pmt, y, preferred_element_type=F32)
    gate2 = mod_ref[0][:, 5 * D:6 * D]
    x2 = x1_ref[0] + gate2 * moe
    y = _rms(x2, gf_ref[...])

    @pl.when(b < N_CTX_BLK)
    def _():
        yctx_ref[0] = y

    @pl.when(b >= N_CTX_BLK)
    def _():
        ylat_ref[0] = y


def _combine(ys, z, comb, loff, ptab, x1, modv, norm_f_g, loff_i, gst_i, p8_i):
    whole = pl.BlockSpec((TAB_ROWS, E_PAD), lambda b, *_: (0, 0))
    tok = pl.BlockSpec((TBLK, E_PAD), lambda b, *_: (b, 0))
    return pl.pallas_call(
        _combine_kernel,
        grid_spec=pltpu.PrefetchScalarGridSpec(
            num_scalar_prefetch=3,
            grid=(N_BLK,),
            in_specs=[pl.BlockSpec(memory_space=pl.ANY), tok, tok, whole, whole,
                      pl.BlockSpec((1, TBLK, D), lambda b, *_: (b, 0, 0)),
                      pl.BlockSpec((1, 1, 6 * D), lambda b, *_: (b, 0, 0)),
                      pl.BlockSpec((1, D), lambda b, *_: (0, 0))],
            out_specs=[pl.BlockSpec((1, TBLK, D), lambda b, *_: (jnp.minimum(b, N_CTX_BLK - 1), 0, 0)),
                       pl.BlockSpec((1, TBLK, D), lambda b, *_: (jnp.maximum(b - N_CTX_BLK, 0), 0, 0))],
            scratch_shapes=[pltpu.VMEM((2, RB, D), F32), pltpu.SemaphoreType.DMA((2,)),
                            pltpu.SMEM((4,), jnp.int32)]),
        out_shape=[jax.ShapeDtypeStruct((N_CTX_BLK, TBLK, D), F32),
                   jax.ShapeDtypeStruct((N_BLK - N_CTX_BLK, TBLK, D), F32)],
        compiler_params=_cparams(("arbitrary",)),
        name="moe_combine",
    )(loff_i, gst_i, p8_i, ys, z, comb, loff, ptab, x1, modv, norm_f_g.reshape(1, D))


def _moe_and_final(x1, h2, comb, modg, norm_f_g, w_gate, b_gate, w_up, b_up, w_down, b_down):
    z, ptab, loff, loff_i, gst_i, p8_i, eoff_i = _plan(comb)
    xs = _dispatch(h2, z, loff, ptab, loff_i, gst_i, p8_i, eoff_i)
    ys = _experts(xs, eoff_i, w_gate, b_gate, w_up, b_up, w_down, b_down)
    return _combine(ys, z, comb, loff, ptab, x1, modg.reshape(N_BLK, 1, 6 * D), norm_f_g,
                    loff_i, gst_i, p8_i)


def _grid_pos_embed(n_tokens, dim):
    rows = n_tokens // GRID_W
    t = np.arange(rows * GRID_W)
    r = (t // GRID_W).astype(np.float32)
    col = (t % GRID_W).astype(np.float32)
    q = dim // 4
    omega = (1.0 / np.float32(POS_TEMP) ** (np.arange(q, dtype=np.float32) / np.float32(q))).astype(np.float32)

    def emb(p):
        a = p[:, None] * omega[None, :]
        return np.concatenate([np.sin(a), np.cos(a)], axis=-1)

    return np.concatenate([emb(r), emb(col)], axis=-1).astype(np.float32)


def kernel(x_prompt, x_sample, c, state_s5_re, state_s5_im, c_ctx, w_ada, b_ada, norm1_g, w_in, s5_lam_re,
           s5_lam_im, s5_log_dt, s5_b_re, s5_b_im, s5_c_re, s5_c_im, s5_d, s5_w_glu, w_fnet, w_out, norm2_g,
           w_router, b_router, w_gate, b_gate, w_up, b_up, w_down, b_down, norm_f_g):
    n_ctx, n_lat = x_prompt.shape[0], x_sample.shape[0]
    assert x_prompt.shape == (SUB * (N_GRP - 1), L_BLK, D) and x_sample.shape == (2, LAT_CHUNKS * L_BLK, D)
    assert w_ada.shape[0] == 1, "one trunk layer"
    layer = 0

    cvec = jnp.concatenate([jnp.broadcast_to(c_ctx[None], (n_ctx, D)), jnp.repeat(c, LAT_CHUNKS, axis=0)], axis=0)
    modg = _adaln(cvec, w_ada[layer], b_ada[layer]).reshape(N_GRP, SUB, 1, 6 * D)

    xp4 = x_prompt.reshape(N_GRP - 1, SUB, L_BLK, D)
    xs4 = x_sample.reshape(1, SUB, L_BLK, D)
    pos3 = jnp.asarray(_grid_pos_embed(LAT_CHUNKS * L_BLK, D).reshape(LAT_CHUNKS, L_BLK, D))
    perm = _perm_matrix()

    us, uf = _pre(xp4, xs4, pos3, modg, norm1_g[layer], w_in[layer], jnp.asarray(perm, BF16))

    bbmat, ccmat, a8 = _s5_params(s5_lam_re[layer], s5_lam_im[layer], s5_log_dt[layer], s5_b_re[layer],
                                  s5_b_im[layer], s5_c_re[layer], s5_c_im[layer])

    st = jnp.stack([state_s5_re[:, layer], state_s5_im[:, layer]], axis=2)
    st = st.reshape(n_lat, 2, 2, N_S5_CHUNKS, CH_P)
    st = jnp.transpose(st, (1, 3, 0, 2, 4)).reshape(2, N_S5_CHUNKS, n_lat, 2 * CH_P)
    init8 = jnp.zeros((2, N_S5_CHUNKS, n_lat, LAT_CHUNKS, 2 * CH_P), F32)
    init8 = init8.at[0, :, :, 0].set(st[0]).at[1, :, :, LAT_CHUNKS - 1].set(st[1])
    init8 = init8.reshape(2, N_S5_CHUNKS, SUB, 2 * CH_P)

    y_s5, fin = _s5(us, bbmat, ccmat, a8, init8, s5_d[layer])

    m1, m2 = _fnet_weights(w_fnet[layer])
    yf = _fnet(uf, m1, m2)

    x1, h2, comb = _post(y_s5, yf, xp4, xs4, pos3, modg, s5_w_glu[layer], w_out[layer], norm2_g[layer],
                         w_router[layer], b_router[layer], jnp.asarray(perm.T, BF16))

    y_prompt, y_lat = _moe_and_final(x1.reshape(N_BLK, TBLK, D), h2.reshape(T_TOK, D), comb.reshape(T_TOK, E_PAD),
                                     modg, norm_f_g, w_gate[layer], b_gate[layer], w_up[layer], b_up[layer],
                                     w_down[layer], b_down[layer])
    y_sample = y_lat.reshape(n_lat, LAT_CHUNKS * L_BLK, D)

    fin = fin[:, :N_GRP - 1].reshape(2, N_GRP - 1, SUB, N_S5_CHUNKS, 2, CH_P)
    fin = jnp.transpose(fin, (4, 1, 2, 0, 3, 5)).reshape(2, n_ctx, 1, 2, N_S5_GROUPS, S5_P)
    return (y_prompt, y_sample, fin[0], fin[1])
```

```python
import functools
import math

import numpy as np
import jax
import jax.numpy as jnp
from jax import lax
from jax.experimental import pallas as pl
from jax.experimental.pallas import tpu as pltpu

F32 = jnp.float32
BF16 = jnp.bfloat16
HIGHEST = lax.Precision.HIGHEST

D = 1024
D_S5 = 768
S5_H = 16
S5_P = 64
N_S5_GROUPS = 48
D_FN = 256
FN_GW = 64
N_EXPERTS = 32
TOP_K = 4
E_PAD = 128
SWIGLU_LIMIT = 7.0
SWIGLU_ALPHA = 1.702
RMS_EPS = 1e-6
POS_TEMP = 10000.0
GRID_W = 64

L_BLK = 256
SUB = 8
N_GRP = 3
GRP_ROWS = L_BLK * SUB
LAT_CHUNKS = 4
T_TOK = N_GRP * GRP_ROWS
TC = 128
N_TC = L_BLK // TC
PERM_T = 32
S5_CHUNK = 16
N_S5_CHUNKS = N_S5_GROUPS // S5_CHUNK
CH_U = S5_CHUNK * S5_H
CH_P = S5_CHUNK * S5_P
VMEM_LIMIT = 56 * 1024 * 1024


def _cparams(sem):
    return pltpu.CompilerParams(dimension_semantics=sem, vmem_limit_bytes=VMEM_LIMIT)


ADA_K = 128


def _adaln_kernel(c_ref, w_ref, b_ref, o_ref):
    @pl.when(pl.program_id(0) == 0)
    def _():
        o_ref[...] = jnp.broadcast_to(b_ref[...], o_ref.shape)

    c = c_ref[...]
    s = c * jax.nn.sigmoid(c)
    s_hi = s.astype(BF16)
    s_lo = (s - s_hi.astype(F32)).astype(BF16)
    w = w_ref[...]
    w_hi = w.astype(BF16)
    w_lo = (w - w_hi.astype(F32)).astype(BF16)
    o_ref[...] += (jnp.dot(s_hi, w_hi, preferred_element_type=F32) + jnp.dot(s_lo, w_hi, preferred_element_type=F32)
                   + jnp.dot(s_hi, w_lo, preferred_element_type=F32))


def _adaln(cvec, w_ada, b_ada):
    k_dim, n = w_ada.shape
    rows = cvec.shape[0]
    return pl.pallas_call(
        _adaln_kernel,
        grid=(k_dim // ADA_K,),
        in_specs=[pl.BlockSpec((rows, ADA_K), lambda k: (0, k)),
                  pl.BlockSpec((ADA_K, n), lambda k: (k, 0)),
                  pl.BlockSpec((1, n), lambda k: (0, 0))],
        out_specs=pl.BlockSpec((rows, n), lambda k: (0, 0)),
        out_shape=jax.ShapeDtypeStruct((rows, n), F32),
        compiler_params=_cparams(("arbitrary",)),
        name="adaln",
    )(cvec, w_ada, b_ada.reshape(1, n))


def _zoh(lam_re, lam_im, log_dt):
    dt = jnp.exp(log_dt)
    a_re = jnp.minimum(lam_re, -1e-4)
    a_im = lam_im
    mag = jnp.exp(a_re * dt)
    ab_re = mag * jnp.cos(a_im * dt)
    ab_im = mag * jnp.sin(a_im * dt)
    den = a_re * a_re + a_im * a_im
    nr = ab_re - 1.0
    f_re = (nr * a_re + ab_im * a_im) / den
    f_im = (ab_im * a_re - nr * a_im) / den
    return ab_re, ab_im, f_re, f_im


def _s5_params_kernel(lre3_ref, lim3_ref, ldt3_ref, lre2_ref, lim2_ref, ldt2_ref, bre_ref, bim_ref,
                      cre_ref, cim_ref, bb_ref, cc_ref, a_ref):
    exact = functools.partial(jnp.dot, preferred_element_type=F32, precision=HIGHEST)
    spread = jnp.where(lax.broadcasted_iota(jnp.int32, (S5_P, CH_P), 1) % S5_P
                       == lax.broadcasted_iota(jnp.int32, (S5_P, CH_P), 0), 1.0, 0.0)

    def block_diag(m):
        wide = jnp.dot(m.reshape(CH_U, S5_P).astype(BF16), spread.astype(BF16), preferred_element_type=F32)
        row_g = lax.broadcasted_iota(jnp.int32, (CH_U, CH_P), 0) // S5_H
        col_g = lax.broadcasted_iota(jnp.int32, (CH_U, CH_P), 1) // S5_P
        return jnp.where(row_g == col_g, wide, 0.0)

    _, _, f_re, f_im = _zoh(lre3_ref[0, 0], lim3_ref[0, 0], ldt3_ref[0, 0])
    b_re = bre_ref[0, 0]
    b_im = bim_ref[0, 0]
    bb_ref[0, 0] = jnp.concatenate([block_diag(f_re * b_re - f_im * b_im),
                                    block_diag(f_re * b_im + f_im * b_re)], axis=1).astype(BF16)
    cc_ref[0, 0] = jnp.concatenate([block_diag(cre_ref[0, 0]).T,
                                    -block_diag(cim_ref[0, 0]).T], axis=0).astype(BF16)

    ab_re, ab_im, _, _ = _zoh(lre2_ref[0, 0], lim2_ref[0, 0], ldt2_ref[0, 0])
    row_g = lax.broadcasted_iota(jnp.int32, (S5_CHUNK, CH_P), 0)
    col_g = lax.broadcasted_iota(jnp.int32, (S5_CHUNK, CH_P), 1) // S5_P

    def lane_row(a):
        flat = jnp.sum(jnp.where(row_g == col_g, exact(a, spread), 0.0), axis=0, keepdims=True)
        return jnp.broadcast_to(flat, (SUB, CH_P))

    a_ref[0, 0] = jnp.concatenate([lane_row(ab_re), lane_row(ab_im)], axis=1)


def _s5_params(lam_re, lam_im, log_dt, b_re, b_im, c_re, c_im):
    nc = N_S5_CHUNKS
    ldt = jnp.broadcast_to(log_dt[:, :, None], (2, N_S5_GROUPS, S5_P))
    g3 = lambda a: a.reshape(2, nc, S5_CHUNK, 1, S5_P)
    g2 = lambda a: a.reshape(2, nc, S5_CHUNK, S5_P)
    ghp = lambda a: a.reshape(2, nc, S5_CHUNK, S5_H, S5_P)
    spec3 = pl.BlockSpec((1, 1, S5_CHUNK, 1, S5_P), lambda d, c: (d, c, 0, 0, 0))
    spec2 = pl.BlockSpec((1, 1, S5_CHUNK, S5_P), lambda d, c: (d, c, 0, 0))
    spec_ghp = pl.BlockSpec((1, 1, S5_CHUNK, S5_H, S5_P), lambda d, c: (d, c, 0, 0, 0))
    return pl.pallas_call(
        _s5_params_kernel,
        grid=(2, nc),
        in_specs=[spec3, spec3, spec3, spec2, spec2, spec2, spec_ghp, spec_ghp, spec_ghp, spec_ghp],
        out_specs=[pl.BlockSpec((1, 1, CH_U, 2 * CH_P), lambda d, c: (d, c, 0, 0)),
                   pl.BlockSpec((1, 1, 2 * CH_P, CH_U), lambda d, c: (d, c, 0, 0)),
                   pl.BlockSpec((1, 1, SUB, 2 * CH_P), lambda d, c: (d, c, 0, 0))],
        out_shape=[jax.ShapeDtypeStruct((2, nc, CH_U, 2 * CH_P), BF16),
                   jax.ShapeDtypeStruct((2, nc, 2 * CH_P, CH_U), BF16),
                   jax.ShapeDtypeStruct((2, nc, SUB, 2 * CH_P), F32)],
        compiler_params=_cparams(("arbitrary", "arbitrary")),
        name="s5_params",
    )(g3(lam_re), g3(lam_im), g3(ldt), g2(lam_re), g2(lam_im), g2(ldt),
      ghp(jnp.swapaxes(b_re, -1, -2)), ghp(jnp.swapaxes(b_im, -1, -2)), ghp(c_re), ghp(c_im))


def _rms(x, g):
    return x * lax.rsqrt(jnp.mean(x * x, axis=-1, keepdims=True) + RMS_EPS) * g


def _load_x(g, xp_ref, xs_ref, pos_ref):
    is_lat = g == N_GRP - 1
    pos = pos_ref[...]
    pos8 = jnp.concatenate([pos, pos], axis=0)
    return jnp.where(is_lat, xs_ref[0] + pos8, xp_ref[0])


def _x_specs():
    n_ctx = N_GRP - 1
    xp_spec = pl.BlockSpec(
        (1, SUB, TC, D),
        lambda g, tc: (jnp.minimum(g, n_ctx - 1), 0, jnp.where(g >= n_ctx, N_TC - 1, tc), 0))
    xs_spec = pl.BlockSpec((1, SUB, TC, D), lambda g, tc: (0, 0, jnp.where(g >= n_ctx, tc, 0), 0))
    pos_spec = pl.BlockSpec((LAT_CHUNKS, TC, D), lambda g, tc: (0, tc, 0))
    return xp_spec, xs_spec, pos_spec


def _perm_matrix():
    p = np.zeros((SUB * PERM_T, SUB * PERM_T), np.float32)
    for s in range(SUB):
        for j in range(PERM_T):
            p[j * SUB + s, s * PERM_T + j] = 1.0
    return p


def _pre_kernel(xp_ref, xs_ref, pos_ref, mod_ref, g1_ref, win_ref, perm_ref, us_ref, uf_ref):
    g = pl.program_id(0)
    x = _load_x(g, xp_ref, xs_ref, pos_ref)
    mod = mod_ref[0]
    shift1 = mod[:, :, 0:D]
    scale1 = mod[:, :, D:2 * D]
    h = _rms(x, g1_ref[...]) * (1.0 + scale1) + shift1
    h2d = h.reshape(SUB * TC, D).astype(BF16)
    proj = jnp.dot(h2d, win_ref[...].astype(BF16), preferred_element_type=F32)
    uf_ref[0] = proj[:, D_S5:].astype(BF16).reshape(SUB, TC, D_FN)
    u = proj[:, :D_S5].astype(BF16)
    perm = perm_ref[...]
    for q in range(TC // PERM_T):
        piece = jnp.concatenate(
            [u[s * TC + q * PERM_T: s * TC + (q + 1) * PERM_T] for s in range(SUB)], axis=0)
        us_ref[0, q * SUB * PERM_T:(q + 1) * SUB * PERM_T, :] = jnp.dot(
            perm, piece, preferred_element_type=F32).astype(BF16)


def _pre(xp4, xs4, pos3, modg, norm1_g, w_in, perm):
    xp_spec, xs_spec, pos_spec = _x_specs()
    return pl.pallas_call(
        _pre_kernel,
        grid=(N_GRP, N_TC),
        in_specs=[xp_spec, xs_spec, pos_spec,
                  pl.BlockSpec((1, SUB, 1, 6 * D), lambda g, tc: (g, 0, 0, 0)),
                  pl.BlockSpec((1, D), lambda g, tc: (0, 0)),
                  pl.BlockSpec((D, D), lambda g, tc: (0, 0)),
                  pl.BlockSpec((SUB * PERM_T, SUB * PERM_T), lambda g, tc: (0, 0))],
        out_specs=[pl.BlockSpec((1, SUB * TC, D_S5), lambda g, tc: (g, tc, 0)),
                   pl.BlockSpec((1, SUB, TC, D_FN), lambda g, tc: (g, 0, tc, 0))],
        out_shape=[jax.ShapeDtypeStruct((N_GRP, GRP_ROWS, D_S5), BF16),
                   jax.ShapeDtypeStruct((N_GRP, SUB, L_BLK, D_FN), BF16)],
        compiler_params=_cparams(("arbitrary", "arbitrary")),
        name="pre_mixer",
    )(xp4, xs4, pos3, modg, norm1_g.reshape(1, D), w_in, perm)


def _cmul(ar, ai, br, bi):
    return ar * br - ai * bi, ar * bi + ai * br


S5_BLK = 512
S5_NBLK = GRP_ROWS // S5_BLK
S5_STEPS = S5_BLK // SUB
S5_FIX_UNROLL = 8


def _s5_kernel(us_ref, bb_ref, cc_ref, a_ref, init_ref, dskip_ref, y_ref, fin_ref, in_a, in_b, out_a, out_b):
    g = pl.program_id(0)
    d = pl.program_id(2)
    is_lat = g == N_GRP - 1
    re_all = slice(0, CH_P)
    im_all = slice(CH_P, 2 * CH_P)
    a_re = a_ref[0, 0, :, re_all]
    a_im = a_ref[0, 0, :, im_all]
    bufs_in = (in_a, in_b)
    bufs_out = (out_a, out_b)

    def time_rows(k):
        blk = k + d * (S5_NBLK - 1 - 2 * k)
        return pl.ds(pl.multiple_of(blk * S5_BLK, S5_BLK), S5_BLK)

    def slot_rows(k):
        return slice((k // 2) * S5_BLK, (k // 2 + 1) * S5_BLK)

    def step_rows(k, j):
        q = j + d * (S5_STEPS - 1 - 2 * j)
        return pl.ds(pl.multiple_of((k // 2) * S5_BLK + q * SUB, SUB), SUB)

    def proj_in(k):
        bufs_in[k % 2][slot_rows(k), :] = jnp.dot(us_ref[0, time_rows(k), :], bb_ref[0, 0],
                                                  preferred_element_type=F32)

    def scan_block(k, carry):
        src, dst = bufs_in[k % 2], bufs_out[k % 2]
        s_re, s_im = carry
        for j in range(S5_STEPS):
            rows = step_rows(k, j)
            n_re = a_re * s_re - a_im * s_im + src[rows, re_all]
            n_im = a_re * s_im + a_im * s_re + src[rows, im_all]
            dst[rows, re_all] = n_re
            dst[rows, im_all] = n_im
            s_re, s_im = n_re, n_im
        return s_re, s_im

    def proj_out(k):
        rows = time_rows(k)
        yb = jnp.dot(bufs_out[k % 2][slot_rows(k), :].astype(BF16), cc_ref[0, 0], preferred_element_type=F32)
        y_ref[0, rows, :] = y_ref[0, rows, :] + yb

    @pl.when(d == 0)
    def _():
        y_ref[0] = dskip_ref[...] * us_ref[0].astype(F32)

    def scan_all(carry, with_out):
        proj_in(0)
        for k in range(S5_NBLK):
            if k + 1 < S5_NBLK:
                proj_in(k + 1)
            carry = scan_block(k, carry)
            if with_out and k >= 1:
                proj_out(k - 1)
        if with_out:
            proj_out(S5_NBLK - 1)
        fin_ref[0, 0, :, re_all] = carry[0]
        fin_ref[0, 0, :, im_all] = carry[1]
        return carry

    @pl.when(jnp.logical_not(is_lat))
    def _():
        zero = jnp.zeros((SUB, CH_P), F32)
        scan_all((zero, zero), True)

    @pl.when(is_lat)
    def _():
        f_re, f_im = scan_all((init_ref[0, 0, :, re_all], init_ref[0, 0, :, im_all]), False)

        sub_id = lax.broadcasted_iota(jnp.int32, (SUB, CH_P), 0) % LAT_CHUNKS
        fwd = d == 0
        lo = jnp.where(fwd, 1, 0)
        hi = jnp.where(fwd, LAT_CHUNKS - 1, LAT_CHUNKS - 2)
        keep = (sub_id >= lo) & (sub_id <= hi)

        def from_prev(v):
            return jnp.where(keep, jnp.where(fwd, pltpu.roll(v, 1, 0), pltpu.roll(v, SUB - 1, 0)), 0.0)

        p_re, p_im = a_re, a_im
        for _ in range(8):
            p_re, p_im = _cmul(p_re, p_im, p_re, p_im)
        t_re, t_im = f_re, f_im
        for _ in range(LAT_CHUNKS - 2):
            m_re, m_im = _cmul(p_re, p_im, from_prev(t_re), from_prev(t_im))
            t_re, t_im = f_re + m_re, f_im + m_im
        corr = (from_prev(t_re), from_prev(t_im))

        for k in range(S5_NBLK):
            dst = bufs_out[k % 2]

            def fix_body(i, carry, k=k, dst=dst):
                m_re, m_im = carry
                for j in range(S5_FIX_UNROLL):
                    rows = step_rows(k, i * S5_FIX_UNROLL + j)
                    m_re, m_im = _cmul(m_re, m_im, a_re, a_im)
                    dst[rows, re_all] = dst[rows, re_all] + m_re
                    dst[rows, im_all] = dst[rows, im_all] + m_im
                return m_re, m_im

            corr = lax.fori_loop(0, S5_STEPS // S5_FIX_UNROLL, fix_body, corr)

        for k in range(S5_NBLK):
            proj_out(k)


def _s5(us, bbmat, ccmat, a8, init8, d_skip):
    return pl.pallas_call(
        _s5_kernel,
        grid=(N_GRP, N_S5_CHUNKS, 2),
        in_specs=[pl.BlockSpec((1, GRP_ROWS, CH_U), lambda g, c, d: (g, 0, c)),
                  pl.BlockSpec((1, 1, CH_U, 2 * CH_P), lambda g, c, d: (d, c, 0, 0)),
                  pl.BlockSpec((1, 1, 2 * CH_P, CH_U), lambda g, c, d: (d, c, 0, 0)),
                  pl.BlockSpec((1, 1, SUB, 2 * CH_P), lambda g, c, d: (d, c, 0, 0)),
                  pl.BlockSpec((1, 1, SUB, 2 * CH_P), lambda g, c, d: (d, c, 0, 0)),
                  pl.BlockSpec((1, CH_U), lambda g, c, d: (0, c))],
        out_specs=[pl.BlockSpec((1, GRP_ROWS, CH_U), lambda g, c, d: (g, 0, c)),
                   pl.BlockSpec((1, 1, SUB, 2 * CH_P), lambda g, c, d: (d, g, 0, c))],
        out_shape=[jax.ShapeDtypeStruct((N_GRP, GRP_ROWS, D_S5), F32),
                   jax.ShapeDtypeStruct((2, N_GRP, SUB, N_S5_CHUNKS * 2 * CH_P), F32)],
        scratch_shapes=[pltpu.VMEM((GRP_ROWS // 2, 2 * CH_P), F32)] * 4,
        compiler_params=_cparams(("arbitrary", "arbitrary", "arbitrary")),
        name="s5_scan",
    )(us, bbmat, ccmat, a8, init8, d_skip.reshape(1, D_S5))


def _fnet_w_kernel(c_ref, s_ref, w_ref, m1_ref, m2_ref):
    w = w_ref[...]
    m1_ref[...] = jnp.dot(c_ref[...], w, preferred_element_type=F32, precision=HIGHEST).astype(BF16)
    m2_ref[...] = jnp.dot(s_ref[...], w, preferred_element_type=F32, precision=HIGHEST).astype(BF16)


FN_SEQ = LAT_CHUNKS
FN_CTX_STEPS = (N_GRP - 1) * SUB // FN_SEQ


def _fnet_kernel(u_ref, cos_s_ref, sin_s_ref, cos_l_ref, sin_l_ref, m1_ref, m2_ref, o_ref):
    i = pl.program_id(0)
    m1 = m1_ref[...]
    m2 = m2_ref[...]

    def mix(u, cos_ref, sin_ref):
        v1 = jnp.dot(u, m1, preferred_element_type=F32).astype(BF16)
        v2 = jnp.dot(u, m2, preferred_element_type=F32).astype(BF16)
        return (jnp.dot(cos_ref[...].astype(BF16), v1, preferred_element_type=F32)
                - jnp.dot(sin_ref[...].astype(BF16), v2, preferred_element_type=F32)).astype(BF16)

    @pl.when(i < FN_CTX_STEPS)
    def _():
        for s in range(FN_SEQ):
            o_ref[0, s] = mix(u_ref[0, s], cos_s_ref, sin_s_ref)

    @pl.when(i >= FN_CTX_STEPS)
    def _():
        u = u_ref[0].reshape(FN_SEQ * L_BLK, D_FN)
        o_ref[0] = mix(u, cos_l_ref, sin_l_ref).reshape(FN_SEQ, L_BLK, D_FN)


def _dft_tables(n):
    k = np.arange(n, dtype=np.int64)
    ang = (2.0 * np.pi / n) * ((k[:, None] * k[None, :]) % n).astype(np.float64)
    scale = 1.0 / math.sqrt(n)
    return (np.cos(ang) * scale).astype(np.float32), (np.sin(ang) * scale).astype(np.float32)


def _fnet(uf, m1, m2):
    n_lat_steps = SUB // FN_SEQ
    cos_s, sin_s = _dft_tables(L_BLK)
    cos_l, sin_l = _dft_tables(FN_SEQ * L_BLK)
    full = lambda i: (0, 0)
    per_grp = SUB // FN_SEQ
    blk = pl.BlockSpec(
        (1, FN_SEQ, L_BLK, D_FN),
        lambda i: (jnp.minimum(i // per_grp, N_GRP - 1),
                   jnp.where(i < FN_CTX_STEPS, i % per_grp, i - FN_CTX_STEPS), 0, 0))
    table = lambda n: pl.BlockSpec((n, n), full)
    return pl.pallas_call(
        _fnet_kernel,
        grid=(FN_CTX_STEPS + n_lat_steps,),
        in_specs=[blk, table(L_BLK), table(L_BLK), table(FN_SEQ * L_BLK), table(FN_SEQ * L_BLK),
                  table(D_FN), table(D_FN)],
        out_specs=blk,
        out_shape=jax.ShapeDtypeStruct(uf.shape, BF16),
        compiler_params=_cparams(("arbitrary",)),
        name="fnet",
    )(uf, jnp.asarray(cos_s), jnp.asarray(sin_s), jnp.asarray(cos_l), jnp.asarray(sin_l), m1, m2)


def _fnet_weights(w_fnet):
    n_g = D_FN // FN_GW
    cos_c, sin_c = _dft_tables(FN_GW)
    eye = np.eye(n_g, dtype=np.float32)
    cos_bd = np.kron(eye, cos_c)
    sin_bd = np.kron(eye, sin_c)
    w_bd = (w_fnet[:, :, None, :] * jnp.asarray(eye)[:, None, :, None]).reshape(D_FN, D_FN)
    out = jax.ShapeDtypeStruct((D_FN, D_FN), BF16)
    return pl.pallas_call(_fnet_w_kernel, out_shape=(out, out), name="fnet_weights")(
        jnp.asarray(cos_bd), jnp.asarray(sin_bd), w_bd)


def _gelu_tanh(x):
    return 0.5 * x * (1.0 + jnp.tanh(math.sqrt(2.0 / math.pi) * (x + 0.044715 * (x * x * x))))


def _post_kernel(y_ref, yf_ref, xp_ref, xs_ref, pos_ref, mod_ref, wglu_ref, wout_ref, g2_ref, wr_ref,
                 br_ref, permt_ref, x1_ref, h2_ref, comb_ref):
    g = pl.program_id(0)
    z = _gelu_tanh(y_ref[0])
    gate = jnp.dot(z.astype(BF16), wglu_ref[...].astype(BF16), preferred_element_type=F32)
    gl = (z * jax.nn.sigmoid(gate)).astype(BF16)
    permt = permt_ref[...]
    n_q = TC // PERM_T
    nat = [jnp.dot(permt, gl[q * SUB * PERM_T:(q + 1) * SUB * PERM_T], preferred_element_type=F32).astype(BF16)
           for q in range(n_q)]
    gl_nat = jnp.concatenate(
        [nat[q][s * PERM_T:(s + 1) * PERM_T] for s in range(SUB) for q in range(n_q)], axis=0)
    w_out = wout_ref[...].astype(BF16)
    mixed = (jnp.dot(gl_nat, w_out[:D_S5], preferred_element_type=F32)
             + jnp.dot(yf_ref[0].reshape(SUB * TC, D_FN), w_out[D_S5:], preferred_element_type=F32))
    x = _load_x(g, xp_ref, xs_ref, pos_ref)
    mod = mod_ref[0]
    gate1 = mod[:, :, 2 * D:3 * D]
    shift2 = mod[:, :, 3 * D:4 * D]
    scale2 = mod[:, :, 4 * D:5 * D]
    x1 = x + gate1 * mixed.reshape(SUB, TC, D)
    x1_ref[0] = x1
    h2 = _rms(x1, g2_ref[...]) * (1.0 + scale2) + shift2
    h2_ref[0] = h2.astype(BF16)

    hr = h2.reshape(SUB * TC, D)
    h_hi = hr.astype(BF16)
    h_lo = (hr - h_hi.astype(F32)).astype(BF16)
    wr = wr_ref[...]
    w_hi = wr.astype(BF16)
    w_lo = (wr - w_hi.astype(F32)).astype(BF16)
    both = jnp.dot(h_hi, jnp.concatenate([w_hi, w_lo], axis=1), preferred_element_type=F32)
    logits = (both[:, :E_PAD] + jnp.dot(h_lo, w_hi, preferred_element_type=F32) + both[:, E_PAD:]
              + br_ref[...])
    lane = lax.broadcasted_iota(jnp.int32, logits.shape, 1).astype(F32)
    top_v, hots = [], []
    cur = logits
    for _ in range(TOP_K):
        m = jnp.max(cur, axis=-1, keepdims=True)
        idx = jnp.min(jnp.where(cur == m, lane, float(E_PAD)), axis=-1, keepdims=True)
        hot = lane == idx
        top_v.append(m)
        hots.append(hot)
        cur = jnp.where(hot, -3.0e38, cur)
    exps = [jnp.exp(v - top_v[0]) for v in top_v]
    denom = exps[0] + exps[1] + exps[2] + exps[3]
    comb = jnp.zeros(logits.shape, F32)
    for k in range(TOP_K):
        comb = comb + jnp.where(hots[k], exps[k] / denom, 0.0)
    comb_ref[0] = comb.reshape(SUB, TC, E_PAD)


def _post(y, yf, xp4, xs4, pos3, modg, w_glu, w_out, norm2_g, w_router, b_router, permt):
    xp_spec, xs_spec, pos_spec = _x_specs()
    const2 = lambda g, tc: (0, 0)
    wr = jnp.zeros((D, E_PAD), F32).at[:, :N_EXPERTS].set(w_router)
    br = jnp.full((1, E_PAD), -1.0e30, F32).at[0, :N_EXPERTS].set(b_router)
    blk = lambda w: pl.BlockSpec((1, SUB, TC, w), lambda g, tc: (g, 0, tc, 0))
    return pl.pallas_call(
        _post_kernel,
        grid=(N_GRP, N_TC),
        in_specs=[pl.BlockSpec((1, SUB * TC, D_S5), lambda g, tc: (g, tc, 0)),
                  blk(D_FN), xp_spec, xs_spec, pos_spec,
                  pl.BlockSpec((1, SUB, 1, 6 * D), lambda g, tc: (g, 0, 0, 0)),
                  pl.BlockSpec((D_S5, D_S5), const2),
                  pl.BlockSpec((D, D), const2),
                  pl.BlockSpec((1, D), const2),
                  pl.BlockSpec((D, E_PAD), const2),
                  pl.BlockSpec((1, E_PAD), const2),
                  pl.BlockSpec((SUB * PERM_T, SUB * PERM_T), const2)],
        out_specs=[blk(D), blk(D), blk(E_PAD)],
        out_shape=[jax.ShapeDtypeStruct((N_GRP, SUB, L_BLK, D), F32),
                   jax.ShapeDtypeStruct((N_GRP, SUB, L_BLK, D), BF16),
                   jax.ShapeDtypeStruct((N_GRP, SUB, L_BLK, E_PAD), F32)],
        compiler_params=_cparams(("arbitrary", "arbitrary")),
        name="post_mixer",
    )(y, yf, xp4, xs4, pos3, modg, w_glu, w_out, norm2_g.reshape(1, D), wr, br, permt)


TBLK = L_BLK
N_BLK = T_TOK // TBLK
N_CTX_BLK = (N_GRP - 1) * SUB
SEG_ALIGN = 8
RB = 1280
assert RB >= TBLK * TOP_K + N_EXPERTS * (SEG_ALIGN - 1) and RB % 128 == 0
TM = 256
TM_SHIFT = 8
assert 1 << TM_SHIFT == TM
R_TOT = T_TOK * TOP_K + N_BLK * N_EXPERTS * (SEG_ALIGN - 1) + TM
TAB_ROWS = 32
assert TAB_ROWS >= N_BLK
BIG_ROWS = 32
BIG_SHIFT = 2
assert SEG_ALIGN << BIG_SHIFT == BIG_ROWS


def _plan_kernel(comb_ref, z_ref, ptab_ref, loff_ref, loff_i_ref, gst_i_ref, p8_i_ref, eoff_i_ref):
    row = lax.broadcasted_iota(jnp.int32, (TBLK, TBLK), 0)
    col = lax.broadcasted_iota(jnp.int32, (TBLK, TBLK), 1)
    earlier = jnp.where(row > col, 1.0, 0.0).astype(BF16)
    ptab_ref[...] = jnp.zeros_like(ptab_ref)

    def body(b, _):
        rows = pl.ds(pl.multiple_of(b * TBLK, TBLK), TBLK)
        m = jnp.where(comb_ref[rows, :] > 0.0, 1.0, 0.0)
        rank = jnp.dot(earlier, m.astype(BF16), preferred_element_type=F32)
        z_ref[rows, :] = m * (rank + 1.0)
        n = jnp.sum(m, axis=0, keepdims=True)
        ptab_ref[pl.ds(b, 1), :] = jnp.floor((n + (SEG_ALIGN - 1)) * (1.0 / SEG_ALIGN)) * SEG_ALIGN
        return 0

    lax.fori_loop(0, N_BLK, body, 0)

    ptab = ptab_ref[...]
    er = lax.broadcasted_iota(jnp.int32, (E_PAD, E_PAD), 0)
    ec = lax.broadcasted_iota(jnp.int32, (E_PAD, E_PAD), 1)
    before = jnp.where(er < ec, 1.0, 0.0)
    exact = functools.partial(jnp.dot, preferred_element_type=F32, precision=HIGHEST)
    loff = exact(ptab, before)
    loff_ref[...] = loff
    as_int = lambda t: t[:, :N_EXPERTS].astype(jnp.int32)
    loff_i_ref[...] = as_int(loff)
    p8_i_ref[...] = as_int(ptab * (1.0 / SEG_ALIGN))
    tot = jnp.sum(ptab, axis=0, keepdims=True)
    eoff = exact(jnp.broadcast_to(tot, (SUB, E_PAD)), before)
    eoff_i_ref[...] = eoff.astype(jnp.int32)
    br = lax.broadcasted_iota(jnp.int32, (TAB_ROWS, TAB_ROWS), 0)
    bc = lax.broadcasted_iota(jnp.int32, (TAB_ROWS, TAB_ROWS), 1)
    gst_i_ref[...] = as_int(eoff[0:1] + exact(jnp.where(br > bc, 1.0, 0.0), ptab))


def _plan(comb):
    tab = jax.ShapeDtypeStruct((TAB_ROWS, E_PAD), F32)
    itab = jax.ShapeDtypeStruct((TAB_ROWS, N_EXPERTS), jnp.int32)
    return pl.pallas_call(
        _plan_kernel,
        out_shape=(jax.ShapeDtypeStruct((T_TOK, E_PAD), F32), tab, tab, itab, itab, itab,
                   jax.ShapeDtypeStruct((SUB, E_PAD), jnp.int32)),
        compiler_params=pltpu.CompilerParams(vmem_limit_bytes=VMEM_LIMIT),
        name="moe_plan",
    )(comb)


def _sort_matrix(b, z_ref, loff_ref, ptab_ref):
    loff = loff_ref[pl.ds(b, 1), :]
    size = ptab_ref[pl.ds(b, 1), :]
    r = lax.broadcasted_iota(jnp.int32, (RB, E_PAD), 0).astype(F32)
    owner = jnp.where(r >= loff, jnp.where(r < loff + size, 1.0, 0.0), 0.0)
    rank1 = r[:, 0:1] - jnp.sum(owner * loff, axis=-1, keepdims=True) + 1.0
    zt = z_ref[...].T
    v = jnp.dot(owner.astype(BF16), zt.astype(BF16), preferred_element_type=F32)
    return jnp.where(v == rank1, 1.0, 0.0), owner


def _segment_copies(b, loff_s, gst_s, p8_s, make_copy):
    def per_expert(e, counts):
        chunks = p8_s[b, e]
        n_big = lax.shift_right_logical(chunks, BIG_SHIFT)
        n_small = chunks - n_big * (BIG_ROWS // SEG_ALIGN)
        local0 = loff_s[b, e]
        global0 = gst_s[b, e]

        def big(j, _):
            make_copy(pl.multiple_of(local0 + j * BIG_ROWS, SEG_ALIGN),
                      pl.multiple_of(global0 + j * BIG_ROWS, SEG_ALIGN), BIG_ROWS).start()
            return 0

        def small(j, _):
            off = n_big * BIG_ROWS + j * SEG_ALIGN
            make_copy(pl.multiple_of(local0 + off, SEG_ALIGN),
                      pl.multiple_of(global0 + off, SEG_ALIGN), SEG_ALIGN).start()
            return 0

        lax.fori_loop(0, n_big, big, 0)
        lax.fori_loop(0, n_small, small, 0)
        return counts[0] + n_big, counts[1] + n_small

    return lax.fori_loop(0, N_EXPERTS, per_expert, (0, 0))


def _wait_copies(counts, make_copy):
    for n, rows in zip(counts, (BIG_ROWS, SEG_ALIGN)):
        def wait_one(i, _, rows=rows):
            make_copy(0, 0, rows).wait()
            return 0
        lax.fori_loop(0, n, wait_one, 0)


def _zero_rows_from(hbm, zeros_vmem, first_row, sem):
    def copy_to(row):
        return pltpu.make_async_copy(zeros_vmem, hbm.at[pl.ds(row, TM)], sem)

    n_full = lax.shift_right_logical(R_TOT - first_row, TM_SHIFT)

    def start_one(j, _):
        copy_to(pl.multiple_of(first_row + j * TM, SEG_ALIGN)).start()
        return 0

    def wait_one(j, _):
        copy_to(0).wait()
        return 0

    lax.fori_loop(0, n_full, start_one, 0)
    lax.fori_loop(0, n_full, wait_one, 0)
    last = copy_to(R_TOT - TM)
    last.start()
    last.wait()


def _dispatch_kernel(loff_s, gst_s, p8_s, eoff_s, h_ref, z_ref, loff_ref, ptab_ref, xs_hbm, xbuf, sem, pending):
    b = pl.program_id(0)
    slot = lax.rem(b, 2)

    def copies_of(s):
        def make_copy(local_row, global_row, rows):
            return pltpu.make_async_copy(xbuf.at[s, pl.ds(local_row, rows)], xs_hbm.at[pl.ds(global_row, rows)],
                                         sem.at[s])
        return make_copy

    pm = _sort_matrix(b, z_ref, loff_ref, ptab_ref)[0].astype(BF16)

    @pl.when(b >= 2)
    def _():
        _wait_copies((pending[2 * slot], pending[2 * slot + 1]), copies_of(slot))

    xbuf[slot] = jnp.dot(pm, h_ref[...], preferred_element_type=F32)
    n_big, n_small = _segment_copies(b, loff_s, gst_s, p8_s, copies_of(slot))
    pending[2 * slot] = n_big
    pending[2 * slot + 1] = n_small

    @pl.when(b == N_BLK - 1)
    def _():
        _wait_copies((n_big, n_small), copies_of(slot))
        _wait_copies((pending[2 * (1 - slot)], pending[2 * (1 - slot) + 1]), copies_of(1 - slot))
        xbuf[0, 0:TM, :] = jnp.zeros((TM, D), F32)
        _zero_rows_from(xs_hbm, xbuf.at[0, 0:TM], eoff_s[0, N_EXPERTS], sem.at[0])


def _dispatch(h2, z, loff, ptab, loff_i, gst_i, p8_i, eoff_i):
    whole = pl.BlockSpec((TAB_ROWS, E_PAD), lambda b, *_: (0, 0))
    return pl.pallas_call(
        _dispatch_kernel,
        grid_spec=pltpu.PrefetchScalarGridSpec(
            num_scalar_prefetch=4,
            grid=(N_BLK,),
            in_specs=[pl.BlockSpec((TBLK, D), lambda b, *_: (b, 0)),
                      pl.BlockSpec((TBLK, E_PAD), lambda b, *_: (b, 0)),
                      whole, whole],
            out_specs=pl.BlockSpec(memory_space=pl.ANY),
            scratch_shapes=[pltpu.VMEM((2, RB, D), F32), pltpu.SemaphoreType.DMA((2,)),
                            pltpu.SMEM((4,), jnp.int32)]),
        out_shape=jax.ShapeDtypeStruct((R_TOT, D), F32),
        compiler_params=_cparams(("arbitrary",)),
        name="moe_dispatch",
    )(loff_i, gst_i, p8_i, eoff_i, h2, z, loff, ptab)


MAX_TILES = R_TOT // TM + N_EXPERTS
X_AHEAD = 3
X_SLOTS = X_AHEAD + 1
Y_SLOTS = 2
W_AHEAD = 2
W_SLOTS = W_AHEAD + 1
N_MATS = 3
FF_CHUNK = 256
CAST_ROWS = 64
TILE_STEP = 64


def _expert_kernel(eoff_s, xs_hbm, wg_hbm, wu_hbm, wd_hbm, bg_ref, bu_ref, bd_ref, ys_hbm,
                   wst, wbf, xin, yout, act_ref, w_sem, x_sem, y_sem, t_exp, t_row, t_first, t_valid, live):
    def add_expert(e, carry):
        n_t, n_live = carry
        start = eoff_s[0, e]
        count = eoff_s[0, e + 1] - start
        tiles = lax.shift_right_logical(count + (TM - 1), TM_SHIFT)

        def add_tile(i, _):
            t_exp[n_t + i] = e
            t_row[n_t + i] = start + i * TM
            t_first[n_t + i] = jnp.where(i == 0, 1, 0)
            t_valid[n_t + i] = jnp.minimum(count - i * TM, TM)
            return 0

        lax.fori_loop(0, tiles, add_tile, 0)
        has_rows = jnp.where(tiles > 0, 1, 0)

        @pl.when(tiles > 0)
        def _():
            live[n_live] = e

        return n_t + tiles, n_live + has_rows

    n_tiles, n_live = lax.fori_loop(0, N_EXPERTS, add_expert, (0, 0))

    def w_copies(e, slot):
        return [pltpu.make_async_copy(w.at[e], wst.at[slot, m], w_sem.at[slot])
                for m, w in enumerate((wg_hbm, wu_hbm, wd_hbm))]

    def tile_rows(j):
        return pl.ds(pl.multiple_of(t_row[j], SEG_ALIGN), TM)

    def x_copy(j, slot):
        return pltpu.make_async_copy(xs_hbm.at[tile_rows(j)], xin.at[slot], x_sem.at[slot])

    def y_pieces(j, op):
        ys = lax.rem(j, Y_SLOTS)
        valid = t_valid[j]
        row0 = t_row[j]

        def piece(off, rows):
            return pltpu.make_async_copy(yout.at[ys, pl.ds(off, rows)],
                                         ys_hbm.at[pl.ds(pl.multiple_of(row0 + off, SEG_ALIGN), rows)],
                                         y_sem.at[ys])

        @pl.when(valid == TM)
        def _():
            op(piece(0, TM))

        @pl.when(valid < TM)
        def _():
            n_big = lax.shift_right_logical(valid, BIG_ROWS.bit_length() - 1)
            n_small = lax.shift_right_logical(valid - n_big * BIG_ROWS, SEG_ALIGN.bit_length() - 1)

            def big(i, _):
                op(piece(pl.multiple_of(i * BIG_ROWS, SEG_ALIGN), BIG_ROWS))
                return 0

            def small(i, _):
                op(piece(pl.multiple_of(n_big * BIG_ROWS + i * SEG_ALIGN, SEG_ALIGN), SEG_ALIGN))
                return 0

            lax.fori_loop(0, n_big, big, 0)
            lax.fori_loop(0, n_small, small, 0)

    for ahead in range(W_AHEAD):
        @pl.when(n_live > ahead)
        def _(ahead=ahead):
            for cp in w_copies(live[ahead], ahead):
                cp.start()

    for ahead in range(X_AHEAD):
        @pl.when(n_tiles > ahead)
        def _(ahead=ahead):
            x_copy(ahead, ahead).start()

    def body(j, k):
        e = t_exp[j]

        @pl.when(t_first[j] == 1)
        def _():
            ws = lax.rem(k, W_SLOTS)

            @pl.when(k + W_AHEAD < n_live)
            def _():
                for cp in w_copies(live[k + W_AHEAD], lax.rem(k + W_AHEAD, W_SLOTS)):
                    cp.start()

            for cp in w_copies(e, ws):
                cp.wait()

            def cast_rows(r, _):
                rows = pl.ds(pl.multiple_of(r * CAST_ROWS, CAST_ROWS), CAST_ROWS)
                for m in range(N_MATS):
                    wbf[m, rows, :] = wst[ws, m, rows, :].astype(BF16)
                return 0

            lax.fori_loop(0, D // CAST_ROWS, cast_rows, 0)

        slot = lax.rem(j, X_SLOTS)
        x_copy(j, slot).wait()

        @pl.when(j + X_AHEAD < n_tiles)
        def _():
            x_copy(j + X_AHEAD, lax.rem(j + X_AHEAD, X_SLOTS)).start()

        @pl.when(j >= Y_SLOTS)
        def _():
            y_pieces(j - Y_SLOTS, lambda cp: cp.wait())

        ys = lax.rem(j, Y_SLOTS)

        def compute(rows):
            x = xin[slot, 0:rows, :].astype(BF16)
            for c in range(D // FF_CHUNK):
                cols = slice(c * FF_CHUNK, (c + 1) * FF_CHUNK)
                gate = jnp.dot(x, wbf[0, :, cols], preferred_element_type=F32) + bg_ref[e][:, cols]
                up = jnp.dot(x, wbf[1, :, cols], preferred_element_type=F32) + bu_ref[e][:, cols]
                gate = jnp.minimum(gate, SWIGLU_LIMIT)
                up = jnp.clip(up, -SWIGLU_LIMIT, SWIGLU_LIMIT)
                act_ref[0:rows, cols] = ((up + 1.0) * gate * jax.nn.sigmoid(SWIGLU_ALPHA * gate)).astype(BF16)
            yout[ys, 0:rows, :] = jnp.dot(act_ref[0:rows, :], wbf[2], preferred_element_type=F32) + bd_ref[e]

        valid = t_valid[j]
        for rows in range(TILE_STEP, TM + 1, TILE_STEP):
            @pl.when(jnp.logical_and(valid > rows - TILE_STEP, valid <= rows))
            def _(rows=rows):
                compute(rows)

        y_pieces(j, lambda cp: cp.start())
        return k + t_first[j]

    lax.fori_loop(0, n_tiles, body, 0)

    for back in range(Y_SLOTS, 0, -1):
        @pl.when(n_tiles >= back)
        def _(back=back):
            y_pieces(n_tiles - back, lambda cp: cp.wait())

    yout[0] = jnp.zeros((TM, D), F32)
    _zero_rows_from(ys_hbm, yout.at[0], eoff_s[0, N_EXPERTS], y_sem.at[0])


def _experts(xs, eoff_i, w_gate, b_gate, w_up, b_up, w_down, b_down):
    hbm = pl.BlockSpec(memory_space=pl.ANY)
    bspec = pl.BlockSpec((N_EXPERTS, 1, D), lambda i, *_: (0, 0, 0))
    return pl.pallas_call(
        _expert_kernel,
        grid_spec=pltpu.PrefetchScalarGridSpec(
            num_scalar_prefetch=1,
            grid=(1,),
            in_specs=[hbm, hbm, hbm, hbm, bspec, bspec, bspec],
            out_specs=hbm,
            scratch_shapes=[pltpu.VMEM((W_SLOTS, N_MATS, D, D), F32), pltpu.VMEM((N_MATS, D, D), BF16),
                            pltpu.VMEM((X_SLOTS, TM, D), F32), pltpu.VMEM((Y_SLOTS, TM, D), F32),
                            pltpu.VMEM((TM, D), BF16),
                            pltpu.SemaphoreType.DMA((W_SLOTS,)), pltpu.SemaphoreType.DMA((X_SLOTS,)),
                            pltpu.SemaphoreType.DMA((Y_SLOTS,)),
                            pltpu.SMEM((MAX_TILES,), jnp.int32), pltpu.SMEM((MAX_TILES,), jnp.int32),
                            pltpu.SMEM((MAX_TILES,), jnp.int32), pltpu.SMEM((MAX_TILES,), jnp.int32),
                            pltpu.SMEM((N_EXPERTS,), jnp.int32)]),
        out_shape=jax.ShapeDtypeStruct((R_TOT, D), F32),
        compiler_params=_cparams(("arbitrary",)),
        name="moe_experts",
    )(eoff_i, xs, w_gate, w_up, w_down, b_gate.reshape(N_EXPERTS, 1, D), b_up.reshape(N_EXPERTS, 1, D),
      b_down.reshape(N_EXPERTS, 1, D))


def _combine_kernel(loff_s, gst_s, p8_s, ys_hbm, z_ref, comb_ref, loff_ref, ptab_ref, x1_ref, mod_ref, gf_ref,
                    yctx_ref, ylat_ref, ybuf, sem):
    b = pl.program_id(0)

    def make_copy(local_row, global_row, rows):
        return pltpu.make_async_copy(ys_hbm.at[pl.ds(global_row, rows)], ybuf.at[pl.ds(local_row, rows)], sem)

    started = _segment_copies(b, loff_s, gst_s, p8_s, make_copy)

    used_chunks = lax.shift_right_logical(loff_s[b, N_EXPERTS - 1], 3) + p8_s[b, N_EXPERTS - 1]

    def zero_chunk(j, _):
        ybuf[pl.ds(pl.multiple_of(j * SEG_ALIGN, SEG_ALIGN), SEG_ALIGN), :] = jnp.zeros((SEG_ALIGN, D), F32)
        return 0

    lax.fori_loop(used_chunks, RB // SEG_ALIGN, zero_chunk, 0)

    pm, owner = _sort_matrix(b, z_ref, loff_ref, ptab_ref)
    comb = comb_ref[...]
    c_hi = comb.astype(BF16)
    rest = comb - c_hi.astype(F32)
    c_mid = rest.astype(BF16)
    c_lo = (rest - c_mid.astype(F32)).astype(BF16)
    pmb = pm.astype(BF16)
    moved = (jnp.dot(pmb, c_hi, preferred_element_type=F32) + jnp.dot(pmb, c_mid, preferred_element_type=F32)
             + jnp.dot(pmb, c_lo, preferred_element_type=F32))
    w_row = jnp.sum(owner * moved, axis=-1, keepdims=True)
    pmt = pm.T.astype(BF16)
    _wait_copies(started, make_copy)
    y = (ybuf[...] * w_row).astype(BF16)
    moe = jnp.dot(pmt, y, preferred_element_type=F32)
    gate2 = mod_ref[0][:, 5 * D:6 * D]
    x2 = x1_ref[0] + gate2 * moe
    y = _rms(x2, gf_ref[...])

    @pl.when(b < N_CTX_BLK)
    def _():
        yctx_ref[0] = y

    @pl.when(b >= N_CTX_BLK)
    def _():
        ylat_ref[0] = y


def _combine(ys, z, comb, loff, ptab, x1, modv, norm_f_g, loff_i, gst_i, p8_i):
    whole = pl.BlockSpec((TAB_ROWS, E_PAD), lambda b, *_: (0, 0))
    tok = pl.BlockSpec((TBLK, E_PAD), lambda b, *_: (b, 0))
    return pl.pallas_call(
        _combine_kernel,
        grid_spec=pltpu.PrefetchScalarGridSpec(
            num_scalar_prefetch=3,
            grid=(N_BLK,),
            in_specs=[pl.BlockSpec(memory_space=pl.ANY), tok, tok, whole, whole,
                      pl.BlockSpec((1, TBLK, D), lambda b, *_: (b, 0, 0)),
                      pl.BlockSpec((1, 1, 6 * D), lambda b, *_: (b, 0, 0)),
                      pl.BlockSpec((1, D), lambda b, *_: (0, 0))],
            out_specs=[pl.BlockSpec((1, TBLK, D), lambda b, *_: (jnp.minimum(b, N_CTX_BLK - 1), 0, 0)),
                       pl.BlockSpec((1, TBLK, D), lambda b, *_: (jnp.maximum(b - N_CTX_BLK, 0), 0, 0))],
            scratch_shapes=[pltpu.VMEM((RB, D), F32), pltpu.SemaphoreType.DMA(())]),
        out_shape=[jax.ShapeDtypeStruct((N_CTX_BLK, TBLK, D), F32),
                   jax.ShapeDtypeStruct((N_BLK - N_CTX_BLK, TBLK, D), F32)],
        compiler_params=_cparams(("arbitrary",)),
        name="moe_combine",
    )(loff_i, gst_i, p8_i, ys, z, comb, loff, ptab, x1, modv, norm_f_g.reshape(1, D))


def _moe_and_final(x1, h2, comb, modg, norm_f_g, w_gate, b_gate, w_up, b_up, w_down, b_down):
    z, ptab, loff, loff_i, gst_i, p8_i, eoff_i = _plan(comb)
    xs = _dispatch(h2, z, loff, ptab, loff_i, gst_i, p8_i, eoff_i)
    ys = _experts(xs, eoff_i, w_gate, b_gate, w_up, b_up, w_down, b_down)
    return _combine(ys, z, comb, loff, ptab, x1, modg.reshape(N_BLK, 1, 6 * D), norm_f_g,
                    loff_i, gst_i, p8_i)


def _grid_pos_embed(n_tokens, dim):
    rows = n_tokens // GRID_W
    t = np.arange(rows * GRID_W)
    r = (t // GRID_W).astype(np.float32)
    col = (t % GRID_W).astype(np.float32)
    q = dim // 4
    omega = (1.0 / np.float32(POS_TEMP) ** (np.arange(q, dtype=np.float32) / np.float32(q))).astype(np.float32)

    def emb(p):
        a = p[:, None] * omega[None, :]
        return np.concatenate([np.sin(a), np.cos(a)], axis=-1)

    return np.concatenate([emb(r), emb(col)], axis=-1).astype(np.float32)


def kernel(x_prompt, x_sample, c, state_s5_re, state_s5_im, c_ctx, w_ada, b_ada, norm1_g, w_in, s5_lam_re,
           s5_lam_im, s5_log_dt, s5_b_re, s5_b_im, s5_c_re, s5_c_im, s5_d, s5_w_glu, w_fnet, w_out, norm2_g,
           w_router, b_router, w_gate, b_gate, w_up, b_up, w_down, b_down, norm_f_g):
    n_ctx, n_lat = x_prompt.shape[0], x_sample.shape[0]
    assert x_prompt.shape == (SUB * (N_GRP - 1), L_BLK, D) and x_sample.shape == (2, LAT_CHUNKS * L_BLK, D)
    assert w_ada.shape[0] == 1, "one trunk layer"
    layer = 0

    cvec = jnp.concatenate([jnp.broadcast_to(c_ctx[None], (n_ctx, D)), jnp.repeat(c, LAT_CHUNKS, axis=0)], axis=0)
    modg = _adaln(cvec, w_ada[layer], b_ada[layer]).reshape(N_GRP, SUB, 1, 6 * D)

    xp4 = x_prompt.reshape(N_GRP - 1, SUB, L_BLK, D)
    xs4 = x_sample.reshape(1, SUB, L_BLK, D)
    pos3 = jnp.asarray(_grid_pos_embed(LAT_CHUNKS * L_BLK, D).reshape(LAT_CHUNKS, L_BLK, D))
    perm = _perm_matrix()

    us, uf = _pre(xp4, xs4, pos3, modg, norm1_g[layer], w_in[layer], jnp.asarray(perm, BF16))

    bbmat, ccmat, a8 = _s5_params(s5_lam_re[layer], s5_lam_im[layer], s5_log_dt[layer], s5_b_re[layer],
                                  s5_b_im[layer], s5_c_re[layer], s5_c_im[layer])

    st = jnp.stack([state_s5_re[:, layer], state_s5_im[:, layer]], axis=2)
    st = st.reshape(n_lat, 2, 2, N_S5_CHUNKS, CH_P)
    st = jnp.transpose(st, (1, 3, 0, 2, 4)).reshape(2, N_S5_CHUNKS, n_lat, 2 * CH_P)
    init8 = jnp.zeros((2, N_S5_CHUNKS, n_lat, LAT_CHUNKS, 2 * CH_P), F32)
    init8 = init8.at[0, :, :, 0].set(st[0]).at[1, :, :, LAT_CHUNKS - 1].set(st[1])
    init8 = init8.reshape(2, N_S5_CHUNKS, SUB, 2 * CH_P)

    y_s5, fin = _s5(us, bbmat, ccmat, a8, init8, s5_d[layer])

    m1, m2 = _fnet_weights(w_fnet[layer])
    yf = _fnet(uf, m1, m2)

    x1, h2, comb = _post(y_s5, yf, xp4, xs4, pos3, modg, s5_w_glu[layer], w_out[layer], norm2_g[layer],
                         w_router[layer], b_router[layer], jnp.asarray(perm.T, BF16))

    y_prompt, y_lat = _moe_and_final(x1.reshape(N_BLK, TBLK, D), h2.reshape(T_TOK, D), comb.reshape(T_TOK, E_PAD),
                                     modg, norm_f_g, w_gate[layer], b_gate[layer], w_up[layer], b_up[layer],
                                     w_down[layer], b_down[layer])
    y_sample = y_lat.reshape(n_lat, LAT_CHUNKS * L_BLK, D)

    fin = fin[:, :N_GRP - 1].reshape(2, N_GRP - 1, SUB, N_S5_CHUNKS, 2, CH_P)
    fin = jnp.transpose(fin, (4, 1, 2, 0, 3, 5)).reshape(2, n_ctx, 1, 2, N_S5_GROUPS, S5_P)
    return (y_prompt, y_sample, fin[0], fin[1])
```

```python
import functools
import math

import numpy as np
import jax
import jax.numpy as jnp
from jax import lax
from jax.experimental import pallas as pl
from jax.experimental.pallas import tpu as pltpu

F32 = jnp.float32
BF16 = jnp.bfloat16
HIGHEST = lax.Precision.HIGHEST

D = 1024
D_S5 = 768
S5_H = 16
S5_P = 64
N_S5_GROUPS = 48
D_FN = 256
FN_GW = 64
N_EXPERTS = 32
TOP_K = 4
E_PAD = 128
SWIGLU_LIMIT = 7.0
SWIGLU_ALPHA = 1.702
RMS_EPS = 1e-6
POS_TEMP = 10000.0
GRID_W = 64

L_BLK = 256
SUB = 8
N_GRP = 3
GRP_ROWS = L_BLK * SUB
LAT_CHUNKS = 4
T_TOK = N_GRP * GRP_ROWS
TC = 128
N_TC = L_BLK // TC
PERM_T = 32
S5_CHUNK = 16
N_S5_CHUNKS = N_S5_GROUPS // S5_CHUNK
CH_U = S5_CHUNK * S5_H
CH_P = S5_CHUNK * S5_P
V7X_VMEM_BYTES = 64 * 1024 * 1024
VMEM_LIMIT = V7X_VMEM_BYTES * 7 // 8


def _cparams(sem):
    return pltpu.CompilerParams(dimension_semantics=sem, vmem_limit_bytes=VMEM_LIMIT)


ADA_K = 128


def _adaln_kernel(c_ref, w_ref, b_ref, o_ref):
    @pl.when(pl.program_id(0) == 0)
    def _():
        o_ref[...] = jnp.broadcast_to(b_ref[...], o_ref.shape)

    c = c_ref[...]
    s = c * jax.nn.sigmoid(c)
    s_hi = s.astype(BF16)
    s_lo = (s - s_hi.astype(F32)).astype(BF16)
    w = w_ref[...]
    w_hi = w.astype(BF16)
    w_lo = (w - w_hi.astype(F32)).astype(BF16)
    o_ref[...] += (jnp.dot(s_hi, w_hi, preferred_element_type=F32) + jnp.dot(s_lo, w_hi, preferred_element_type=F32)
                   + jnp.dot(s_hi, w_lo, preferred_element_type=F32))


def _adaln(cvec, w_ada, b_ada):
    k_dim, n = w_ada.shape
    rows = cvec.shape[0]
    return pl.pallas_call(
        _adaln_kernel,
        grid=(k_dim // ADA_K,),
        in_specs=[pl.BlockSpec((rows, ADA_K), lambda k: (0, k)),
                  pl.BlockSpec((ADA_K, n), lambda k: (k, 0)),
                  pl.BlockSpec((1, n), lambda k: (0, 0))],
        out_specs=pl.BlockSpec((rows, n), lambda k: (0, 0)),
        out_shape=jax.ShapeDtypeStruct((rows, n), F32),
        compiler_params=_cparams(("arbitrary",)),
        name="adaln",
    )(cvec, w_ada, b_ada.reshape(1, n))


def _zoh(lam_re, lam_im, log_dt):
    dt = jnp.exp(log_dt)
    a_re = jnp.minimum(lam_re, -1e-4)
    a_im = lam_im
    mag = jnp.exp(a_re * dt)
    ab_re = mag * jnp.cos(a_im * dt)
    ab_im = mag * jnp.sin(a_im * dt)
    den = a_re * a_re + a_im * a_im
    nr = ab_re - 1.0
    f_re = (nr * a_re + ab_im * a_im) / den
    f_im = (ab_im * a_re - nr * a_im) / den
    return ab_re, ab_im, f_re, f_im


def _s5_params_kernel(lre3_ref, lim3_ref, ldt3_ref, lre2_ref, lim2_ref, ldt2_ref, bre_ref, bim_ref,
                      cre_ref, cim_ref, bb_ref, cc_ref, a_ref):
    exact = functools.partial(jnp.dot, preferred_element_type=F32, precision=HIGHEST)
    spread = jnp.where(lax.broadcasted_iota(jnp.int32, (S5_P, CH_P), 1) % S5_P
                       == lax.broadcasted_iota(jnp.int32, (S5_P, CH_P), 0), 1.0, 0.0)

    def block_diag(m):
        wide = jnp.dot(m.reshape(CH_U, S5_P).astype(BF16), spread.astype(BF16), preferred_element_type=F32)
        row_g = lax.broadcasted_iota(jnp.int32, (CH_U, CH_P), 0) // S5_H
        col_g = lax.broadcasted_iota(jnp.int32, (CH_U, CH_P), 1) // S5_P
        return jnp.where(row_g == col_g, wide, 0.0)

    _, _, f_re, f_im = _zoh(lre3_ref[0, 0], lim3_ref[0, 0], ldt3_ref[0, 0])
    b_re = bre_ref[0, 0]
    b_im = bim_ref[0, 0]
    bb_ref[0, 0] = jnp.concatenate([block_diag(f_re * b_re - f_im * b_im),
                                    block_diag(f_re * b_im + f_im * b_re)], axis=1).astype(BF16)
    cc_ref[0, 0] = jnp.concatenate([block_diag(cre_ref[0, 0]).T,
                                    -block_diag(cim_ref[0, 0]).T], axis=0).astype(BF16)

    ab_re, ab_im, _, _ = _zoh(lre2_ref[0, 0], lim2_ref[0, 0], ldt2_ref[0, 0])
    row_g = lax.broadcasted_iota(jnp.int32, (S5_CHUNK, CH_P), 0)
    col_g = lax.broadcasted_iota(jnp.int32, (S5_CHUNK, CH_P), 1) // S5_P

    def lane_row(a):
        flat = jnp.sum(jnp.where(row_g == col_g, exact(a, spread), 0.0), axis=0, keepdims=True)
        return jnp.broadcast_to(flat, (SUB, CH_P))

    a_ref[0, 0] = jnp.concatenate([lane_row(ab_re), lane_row(ab_im)], axis=1)


def _s5_params(lam_re, lam_im, log_dt, b_re, b_im, c_re, c_im):
    nc = N_S5_CHUNKS
    ldt = jnp.broadcast_to(log_dt[:, :, None], (2, N_S5_GROUPS, S5_P))
    g3 = lambda a: a.reshape(2, nc, S5_CHUNK, 1, S5_P)
    g2 = lambda a: a.reshape(2, nc, S5_CHUNK, S5_P)
    ghp = lambda a: a.reshape(2, nc, S5_CHUNK, S5_H, S5_P)
    spec3 = pl.BlockSpec((1, 1, S5_CHUNK, 1, S5_P), lambda d, c: (d, c, 0, 0, 0))
    spec2 = pl.BlockSpec((1, 1, S5_CHUNK, S5_P), lambda d, c: (d, c, 0, 0))
    spec_ghp = pl.BlockSpec((1, 1, S5_CHUNK, S5_H, S5_P), lambda d, c: (d, c, 0, 0, 0))
    return pl.pallas_call(
        _s5_params_kernel,
        grid=(2, nc),
        in_specs=[spec3, spec3, spec3, spec2, spec2, spec2, spec_ghp, spec_ghp, spec_ghp, spec_ghp],
        out_specs=[pl.BlockSpec((1, 1, CH_U, 2 * CH_P), lambda d, c: (d, c, 0, 0)),
                   pl.BlockSpec((1, 1, 2 * CH_P, CH_U), lambda d, c: (d, c, 0, 0)),
                   pl.BlockSpec((1, 1, SUB, 2 * CH_P), lambda d, c: (d, c, 0, 0))],
        out_shape=[jax.ShapeDtypeStruct((2, nc, CH_U, 2 * CH_P), BF16),
                   jax.ShapeDtypeStruct((2, nc, 2 * CH_P, CH_U), BF16),
                   jax.ShapeDtypeStruct((2, nc, SUB, 2 * CH_P), F32)],
        compiler_params=_cparams(("arbitrary", "arbitrary")),
        name="s5_params",
    )(g3(lam_re), g3(lam_im), g3(ldt), g2(lam_re), g2(lam_im), g2(ldt),
      ghp(jnp.swapaxes(b_re, -1, -2)), ghp(jnp.swapaxes(b_im, -1, -2)), ghp(c_re), ghp(c_im))


def _rms(x, g):
    return x * lax.rsqrt(jnp.mean(x * x, axis=-1, keepdims=True) + RMS_EPS) * g


def _load_x(g, xp_ref, xs_ref, pos_ref):
    is_lat = g == N_GRP - 1
    pos = pos_ref[...]
    pos8 = jnp.concatenate([pos, pos], axis=0)
    return jnp.where(is_lat, xs_ref[0] + pos8, xp_ref[0])


def _x_specs():
    n_ctx = N_GRP - 1
    xp_spec = pl.BlockSpec(
        (1, SUB, TC, D),
        lambda g, tc: (jnp.minimum(g, n_ctx - 1), 0, jnp.where(g >= n_ctx, N_TC - 1, tc), 0))
    xs_spec = pl.BlockSpec((1, SUB, TC, D), lambda g, tc: (0, 0, jnp.where(g >= n_ctx, tc, 0), 0))
    pos_spec = pl.BlockSpec((LAT_CHUNKS, TC, D), lambda g, tc: (0, tc, 0))
    return xp_spec, xs_spec, pos_spec


def _perm_matrix():
    p = np.zeros((SUB * PERM_T, SUB * PERM_T), np.float32)
    for s in range(SUB):
        for j in range(PERM_T):
            p[j * SUB + s, s * PERM_T + j] = 1.0
    return p


def _pre_kernel(xp_ref, xs_ref, pos_ref, mod_ref, g1_ref, win_ref, perm_ref, us_ref, uf_ref):
    g = pl.program_id(0)
    x = _load_x(g, xp_ref, xs_ref, pos_ref)
    mod = mod_ref[0]
    shift1 = mod[:, :, 0:D]
    scale1 = mod[:, :, D:2 * D]
    h = _rms(x, g1_ref[...]) * (1.0 + scale1) + shift1
    h2d = h.reshape(SUB * TC, D).astype(BF16)
    proj = jnp.dot(h2d, win_ref[...].astype(BF16), preferred_element_type=F32)
    uf_ref[0] = proj[:, D_S5:].astype(BF16).reshape(SUB, TC, D_FN)
    u = proj[:, :D_S5].astype(BF16)
    perm = perm_ref[...]
    for q in range(TC // PERM_T):
        piece = jnp.concatenate(
            [u[s * TC + q * PERM_T: s * TC + (q + 1) * PERM_T] for s in range(SUB)], axis=0)
        us_ref[0, q * SUB * PERM_T:(q + 1) * SUB * PERM_T, :] = jnp.dot(
            perm, piece, preferred_element_type=F32).astype(BF16)


def _pre(xp4, xs4, pos3, modg, norm1_g, w_in, perm):
    xp_spec, xs_spec, pos_spec = _x_specs()
    return pl.pallas_call(
        _pre_kernel,
        grid=(N_GRP, N_TC),
        in_specs=[xp_spec, xs_spec, pos_spec,
                  pl.BlockSpec((1, SUB, 1, 6 * D), lambda g, tc: (g, 0, 0, 0)),
                  pl.BlockSpec((1, D), lambda g, tc: (0, 0)),
                  pl.BlockSpec((D, D), lambda g, tc: (0, 0)),
                  pl.BlockSpec((SUB * PERM_T, SUB * PERM_T), lambda g, tc: (0, 0))],
        out_specs=[pl.BlockSpec((1, SUB * TC, D_S5), lambda g, tc: (g, tc, 0)),
                   pl.BlockSpec((1, SUB, TC, D_FN), lambda g, tc: (g, 0, tc, 0))],
        out_shape=[jax.ShapeDtypeStruct((N_GRP, GRP_ROWS, D_S5), BF16),
                   jax.ShapeDtypeStruct((N_GRP, SUB, L_BLK, D_FN), BF16)],
        compiler_params=_cparams(("arbitrary", "arbitrary")),
        name="pre_mixer",
    )(xp4, xs4, pos3, modg, norm1_g.reshape(1, D), w_in, perm)


def _cmul(ar, ai, br, bi):
    return ar * br - ai * bi, ar * bi + ai * br


S5_BLK = 512
S5_NBLK = GRP_ROWS // S5_BLK
S5_STEPS = S5_BLK // SUB
S5_FIX_UNROLL = 8


def _s5_kernel(us_ref, bb_ref, cc_ref, a_ref, init_ref, dskip_ref, y_ref, fin_ref, in_a, in_b, out_a, out_b):
    g = pl.program_id(0)
    d = pl.program_id(2)
    is_lat = g == N_GRP - 1
    re_all = slice(0, CH_P)
    im_all = slice(CH_P, 2 * CH_P)
    a_re = a_ref[0, 0, :, re_all]
    a_im = a_ref[0, 0, :, im_all]
    bufs_in = (in_a, in_b)
    bufs_out = (out_a, out_b)

    def time_rows(k):
        blk = k + d * (S5_NBLK - 1 - 2 * k)
        return pl.ds(pl.multiple_of(blk * S5_BLK, S5_BLK), S5_BLK)

    def slot_rows(k):
        return slice((k // 2) * S5_BLK, (k // 2 + 1) * S5_BLK)

    def step_rows(k, j):
        q = j + d * (S5_STEPS - 1 - 2 * j)
        return pl.ds(pl.multiple_of((k // 2) * S5_BLK + q * SUB, SUB), SUB)

    def proj_in(k):
        bufs_in[k % 2][slot_rows(k), :] = jnp.dot(us_ref[0, time_rows(k), :], bb_ref[0, 0],
                                                  preferred_element_type=F32)

    def scan_block(k, carry):
        src, dst = bufs_in[k % 2], bufs_out[k % 2]
        s_re, s_im = carry
        for j in range(S5_STEPS):
            rows = step_rows(k, j)
            n_re = a_re * s_re - a_im * s_im + src[rows, re_all]
            n_im = a_re * s_im + a_im * s_re + src[rows, im_all]
            dst[rows, re_all] = n_re
            dst[rows, im_all] = n_im
            s_re, s_im = n_re, n_im
        return s_re, s_im

    def proj_out(k):
        rows = time_rows(k)
        yb = jnp.dot(bufs_out[k % 2][slot_rows(k), :].astype(BF16), cc_ref[0, 0], preferred_element_type=F32)
        y_ref[0, rows, :] = y_ref[0, rows, :] + yb

    @pl.when(d == 0)
    def _():
        y_ref[0] = dskip_ref[...] * us_ref[0].astype(F32)

    def scan_all(carry, with_out):
        proj_in(0)
        for k in range(S5_NBLK):
            if with_out and k >= 1:
                proj_out(k - 1)
            if k + 1 < S5_NBLK:
                proj_in(k + 1)
            carry = scan_block(k, carry)
        if with_out:
            proj_out(S5_NBLK - 1)
        fin_ref[0, 0, :, re_all] = carry[0]
        fin_ref[0, 0, :, im_all] = carry[1]
        return carry

    @pl.when(jnp.logical_not(is_lat))
    def _():
        zero = jnp.zeros((SUB, CH_P), F32)
        scan_all((zero, zero), True)

    @pl.when(is_lat)
    def _():
        f_re, f_im = scan_all((init_ref[0, 0, :, re_all], init_ref[0, 0, :, im_all]), False)

        sub_id = lax.broadcasted_iota(jnp.int32, (SUB, CH_P), 0) % LAT_CHUNKS
        fwd = d == 0
        lo = jnp.where(fwd, 1, 0)
        hi = jnp.where(fwd, LAT_CHUNKS - 1, LAT_CHUNKS - 2)
        keep = (sub_id >= lo) & (sub_id <= hi)

        def from_prev(v):
            return jnp.where(keep, jnp.where(fwd, pltpu.roll(v, 1, 0), pltpu.roll(v, SUB - 1, 0)), 0.0)

        p_re, p_im = a_re, a_im
        for _ in range(8):
            p_re, p_im = _cmul(p_re, p_im, p_re, p_im)
        t_re, t_im = f_re, f_im
        for _ in range(LAT_CHUNKS - 2):
            m_re, m_im = _cmul(p_re, p_im, from_prev(t_re), from_prev(t_im))
            t_re, t_im = f_re + m_re, f_im + m_im
        corr = (from_prev(t_re), from_prev(t_im))

        for k in range(S5_NBLK):
            dst = bufs_out[k % 2]

            def fix_body(i, carry, k=k, dst=dst):
                m_re, m_im = carry
                for j in range(S5_FIX_UNROLL):
                    rows = step_rows(k, i * S5_FIX_UNROLL + j)
                    m_re, m_im = _cmul(m_re, m_im, a_re, a_im)
                    dst[rows, re_all] = dst[rows, re_all] + m_re
                    dst[rows, im_all] = dst[rows, im_all] + m_im
                return m_re, m_im

            corr = lax.fori_loop(0, S5_STEPS // S5_FIX_UNROLL, fix_body, corr)

        for k in range(S5_NBLK):
            proj_out(k)


def _s5(us, bbmat, ccmat, a8, init8, d_skip):
    return pl.pallas_call(
        _s5_kernel,
        grid=(N_GRP, N_S5_CHUNKS, 2),
        in_specs=[pl.BlockSpec((1, GRP_ROWS, CH_U), lambda g, c, d: (g, 0, c)),
                  pl.BlockSpec((1, 1, CH_U, 2 * CH_P), lambda g, c, d: (d, c, 0, 0)),
                  pl.BlockSpec((1, 1, 2 * CH_P, CH_U), lambda g, c, d: (d, c, 0, 0)),
                  pl.BlockSpec((1, 1, SUB, 2 * CH_P), lambda g, c, d: (d, c, 0, 0)),
                  pl.BlockSpec((1, 1, SUB, 2 * CH_P), lambda g, c, d: (d, c, 0, 0)),
                  pl.BlockSpec((1, CH_U), lambda g, c, d: (0, c))],
        out_specs=[pl.BlockSpec((1, GRP_ROWS, CH_U), lambda g, c, d: (g, 0, c)),
                   pl.BlockSpec((1, 1, SUB, 2 * CH_P), lambda g, c, d: (d, g, 0, c))],
        out_shape=[jax.ShapeDtypeStruct((N_GRP, GRP_ROWS, D_S5), F32),
                   jax.ShapeDtypeStruct((2, N_GRP, SUB, N_S5_CHUNKS * 2 * CH_P), F32)],
        scratch_shapes=[pltpu.VMEM((GRP_ROWS // 2, 2 * CH_P), F32)] * 4,
        compiler_params=_cparams(("arbitrary", "arbitrary", "arbitrary")),
        name="s5_scan",
    )(us, bbmat, ccmat, a8, init8, d_skip.reshape(1, D_S5))


def _fnet_w_kernel(c_ref, s_ref, w_ref, m1_ref, m2_ref):
    w = w_ref[...]
    m1_ref[...] = jnp.dot(c_ref[...], w, preferred_element_type=F32, precision=HIGHEST).astype(BF16)
    m2_ref[...] = jnp.dot(s_ref[...], w, preferred_element_type=F32, precision=HIGHEST).astype(BF16)


FN_SEQ = LAT_CHUNKS
FN_CTX_STEPS = (N_GRP - 1) * SUB // FN_SEQ


def _fnet_kernel(u_ref, cos_s_ref, sin_s_ref, cos_l_ref, sin_l_ref, m1_ref, m2_ref, o_ref):
    i = pl.program_id(0)
    m1 = m1_ref[...]
    m2 = m2_ref[...]

    def mix(u, cos_ref, sin_ref):
        v1 = jnp.dot(u, m1, preferred_element_type=F32).astype(BF16)
        v2 = jnp.dot(u, m2, preferred_element_type=F32).astype(BF16)
        return (jnp.dot(cos_ref[...].astype(BF16), v1, preferred_element_type=F32)
                - jnp.dot(sin_ref[...].astype(BF16), v2, preferred_element_type=F32)).astype(BF16)

    @pl.when(i < FN_CTX_STEPS)
    def _():
        for s in range(FN_SEQ):
            o_ref[0, s] = mix(u_ref[0, s], cos_s_ref, sin_s_ref)

    @pl.when(i >= FN_CTX_STEPS)
    def _():
        u = u_ref[0].reshape(FN_SEQ * L_BLK, D_FN)
        o_ref[0] = mix(u, cos_l_ref, sin_l_ref).reshape(FN_SEQ, L_BLK, D_FN)


def _dft_tables(n):
    k = np.arange(n, dtype=np.int64)
    ang = (2.0 * np.pi / n) * ((k[:, None] * k[None, :]) % n).astype(np.float64)
    scale = 1.0 / math.sqrt(n)
    return (np.cos(ang) * scale).astype(np.float32), (np.sin(ang) * scale).astype(np.float32)


def _fnet(uf, m1, m2):
    n_lat_steps = SUB // FN_SEQ
    cos_s, sin_s = _dft_tables(L_BLK)
    cos_l, sin_l = _dft_tables(FN_SEQ * L_BLK)
    full = lambda i: (0, 0)
    per_grp = SUB // FN_SEQ
    blk = pl.BlockSpec(
        (1, FN_SEQ, L_BLK, D_FN),
        lambda i: (jnp.minimum(i // per_grp, N_GRP - 1),
                   jnp.where(i < FN_CTX_STEPS, i % per_grp, i - FN_CTX_STEPS), 0, 0))
    table = lambda n: pl.BlockSpec((n, n), full)
    return pl.pallas_call(
        _fnet_kernel,
        grid=(FN_CTX_STEPS + n_lat_steps,),
        in_specs=[blk, table(L_BLK), table(L_BLK), table(FN_SEQ * L_BLK), table(FN_SEQ * L_BLK),
                  table(D_FN), table(D_FN)],
        out_specs=blk,
        out_shape=jax.ShapeDtypeStruct(uf.shape, BF16),
        compiler_params=_cparams(("arbitrary",)),
        name="fnet",
    )(uf, jnp.asarray(cos_s), jnp.asarray(sin_s), jnp.asarray(cos_l), jnp.asarray(sin_l), m1, m2)


def _fnet_weights(w_fnet):
    n_g = D_FN // FN_GW
    cos_c, sin_c = _dft_tables(FN_GW)
    eye = np.eye(n_g, dtype=np.float32)
    cos_bd = np.kron(eye, cos_c)
    sin_bd = np.kron(eye, sin_c)
    w_bd = (w_fnet[:, :, None, :] * jnp.asarray(eye)[:, None, :, None]).reshape(D_FN, D_FN)
    out = jax.ShapeDtypeStruct((D_FN, D_FN), BF16)
    return pl.pallas_call(_fnet_w_kernel, out_shape=(out, out), name="fnet_weights")(
        jnp.asarray(cos_bd), jnp.asarray(sin_bd), w_bd)


def _gelu_tanh(x):
    return 0.5 * x * (1.0 + jnp.tanh(math.sqrt(2.0 / math.pi) * (x + 0.044715 * (x * x * x))))


def _post_kernel(y_ref, yf_ref, xp_ref, xs_ref, pos_ref, mod_ref, wglu_ref, wout_ref, g2_ref, wr_ref,
                 br_ref, permt_ref, x1_ref, h2_ref, comb_ref):
    g = pl.program_id(0)
    z = _gelu_tanh(y_ref[0])
    gate = jnp.dot(z.astype(BF16), wglu_ref[...].astype(BF16), preferred_element_type=F32)
    gl = (z * jax.nn.sigmoid(gate)).astype(BF16)
    permt = permt_ref[...]
    n_q = TC // PERM_T
    nat = [jnp.dot(permt, gl[q * SUB * PERM_T:(q + 1) * SUB * PERM_T], preferred_element_type=F32).astype(BF16)
           for q in range(n_q)]
    gl_nat = jnp.concatenate(
        [nat[q][s * PERM_T:(s + 1) * PERM_T] for s in range(SUB) for q in range(n_q)], axis=0)
    w_out = wout_ref[...].astype(BF16)
    mixed = (jnp.dot(gl_nat, w_out[:D_S5], preferred_element_type=F32)
             + jnp.dot(yf_ref[0].reshape(SUB * TC, D_FN), w_out[D_S5:], preferred_element_type=F32))
    x = _load_x(g, xp_ref, xs_ref, pos_ref)
    mod = mod_ref[0]
    gate1 = mod[:, :, 2 * D:3 * D]
    shift2 = mod[:, :, 3 * D:4 * D]
    scale2 = mod[:, :, 4 * D:5 * D]
    x1 = x + gate1 * mixed.reshape(SUB, TC, D)
    x1_ref[0] = x1
    h2 = _rms(x1, g2_ref[...]) * (1.0 + scale2) + shift2
    h2_ref[0] = h2.astype(BF16)

    hr = h2.reshape(SUB * TC, D)
    h_hi = hr.astype(BF16)
    h_lo = (hr - h_hi.astype(F32)).astype(BF16)
    wr = wr_ref[...]
    w_hi = wr.astype(BF16)
    w_lo = (wr - w_hi.astype(F32)).astype(BF16)
    both = jnp.dot(h_hi, jnp.concatenate([w_hi, w_lo], axis=1), preferred_element_type=F32)
    logits = (both[:, :E_PAD] + jnp.dot(h_lo, w_hi, preferred_element_type=F32) + both[:, E_PAD:]
              + br_ref[...])
    lane = lax.broadcasted_iota(jnp.int32, logits.shape, 1).astype(F32)
    top_v, hots = [], []
    cur = logits
    for _ in range(TOP_K):
        m = jnp.max(cur, axis=-1, keepdims=True)
        idx = jnp.min(jnp.where(cur == m, lane, float(E_PAD)), axis=-1, keepdims=True)
        hot = lane == idx
        top_v.append(m)
        hots.append(hot)
        cur = jnp.where(hot, -3.0e38, cur)
    exps = [jnp.exp(v - top_v[0]) for v in top_v]
    denom = exps[0] + exps[1] + exps[2] + exps[3]
    comb = jnp.zeros(logits.shape, F32)
    for k in range(TOP_K):
        comb = comb + jnp.where(hots[k], exps[k] / denom, 0.0)
    comb_ref[0] = comb.reshape(SUB, TC, E_PAD)


def _post(y, yf, xp4, xs4, pos3, modg, w_glu, w_out, norm2_g, w_router, b_router, permt):
    xp_spec, xs_spec, pos_spec = _x_specs()
    const2 = lambda g, tc: (0, 0)
    wr = jnp.zeros((D, E_PAD), F32).at[:, :N_EXPERTS].set(w_router)
    br = jnp.full((1, E_PAD), -1.0e30, F32).at[0, :N_EXPERTS].set(b_router)
    blk = lambda w: pl.BlockSpec((1, SUB, TC, w), lambda g, tc: (g, 0, tc, 0))
    return pl.pallas_call(
        _post_kernel,
        grid=(N_GRP, N_TC),
        in_specs=[pl.BlockSpec((1, SUB * TC, D_S5), lambda g, tc: (g, tc, 0)),
                  blk(D_FN), xp_spec, xs_spec, pos_spec,
                  pl.BlockSpec((1, SUB, 1, 6 * D), lambda g, tc: (g, 0, 0, 0)),
                  pl.BlockSpec((D_S5, D_S5), const2),
                  pl.BlockSpec((D, D), const2),
                  pl.BlockSpec((1, D), const2),
                  pl.BlockSpec((D, E_PAD), const2),
                  pl.BlockSpec((1, E_PAD), const2),
                  pl.BlockSpec((SUB * PERM_T, SUB * PERM_T), const2)],
        out_specs=[blk(D), blk(D), blk(E_PAD)],
        out_shape=[jax.ShapeDtypeStruct((N_GRP, SUB, L_BLK, D), F32),
                   jax.ShapeDtypeStruct((N_GRP, SUB, L_BLK, D), BF16),
                   jax.ShapeDtypeStruct((N_GRP, SUB, L_BLK, E_PAD), F32)],
        compiler_params=_cparams(("arbitrary", "arbitrary")),
        name="post_mixer",
    )(y, yf, xp4, xs4, pos3, modg, w_glu, w_out, norm2_g.reshape(1, D), wr, br, permt)


TBLK = L_BLK
N_BLK = T_TOK // TBLK
N_CTX_BLK = (N_GRP - 1) * SUB
SEG_ALIGN = 8
RB = 1280
assert RB >= TBLK * TOP_K + N_EXPERTS * (SEG_ALIGN - 1) and RB % 128 == 0
TM = 256
TM_SHIFT = 8
assert 1 << TM_SHIFT == TM
R_TOT = T_TOK * TOP_K + N_BLK * N_EXPERTS * (SEG_ALIGN - 1) + TM
TAB_ROWS = 32
assert TAB_ROWS >= N_BLK
BIG_ROWS = 32
BIG_SHIFT = 2
assert SEG_ALIGN << BIG_SHIFT == BIG_ROWS


def _plan_kernel(comb_ref, z_ref, ptab_ref, loff_ref, loff_i_ref, gst_i_ref, p8_i_ref, eoff_i_ref):
    row = lax.broadcasted_iota(jnp.int32, (TBLK, TBLK), 0)
    col = lax.broadcasted_iota(jnp.int32, (TBLK, TBLK), 1)
    earlier = jnp.where(row > col, 1.0, 0.0).astype(BF16)
    ptab_ref[...] = jnp.zeros_like(ptab_ref)

    def body(b, _):
        rows = pl.ds(pl.multiple_of(b * TBLK, TBLK), TBLK)
        m = jnp.where(comb_ref[rows, :] > 0.0, 1.0, 0.0)
        rank = jnp.dot(earlier, m.astype(BF16), preferred_element_type=F32)
        z_ref[rows, :] = m * (rank + 1.0)
        n = jnp.sum(m, axis=0, keepdims=True)
        ptab_ref[pl.ds(b, 1), :] = jnp.floor((n + (SEG_ALIGN - 1)) * (1.0 / SEG_ALIGN)) * SEG_ALIGN
        return 0

    lax.fori_loop(0, N_BLK, body, 0)

    ptab = ptab_ref[...]
    er = lax.broadcasted_iota(jnp.int32, (E_PAD, E_PAD), 0)
    ec = lax.broadcasted_iota(jnp.int32, (E_PAD, E_PAD), 1)
    before = jnp.where(er < ec, 1.0, 0.0)
    exact = functools.partial(jnp.dot, preferred_element_type=F32, precision=HIGHEST)
    loff = exact(ptab, before)
    loff_ref[...] = loff
    as_int = lambda t: t[:, :N_EXPERTS].astype(jnp.int32)
    loff_i_ref[...] = as_int(loff)
    p8_i_ref[...] = as_int(ptab * (1.0 / SEG_ALIGN))
    tot = jnp.sum(ptab, axis=0, keepdims=True)
    eoff = exact(jnp.broadcast_to(tot, (SUB, E_PAD)), before)
    eoff_i_ref[...] = eoff.astype(jnp.int32)
    br = lax.broadcasted_iota(jnp.int32, (TAB_ROWS, TAB_ROWS), 0)
    bc = lax.broadcasted_iota(jnp.int32, (TAB_ROWS, TAB_ROWS), 1)
    gst_i_ref[...] = as_int(eoff[0:1] + exact(jnp.where(br > bc, 1.0, 0.0), ptab))


def _plan(comb):
    tab = jax.ShapeDtypeStruct((TAB_ROWS, E_PAD), F32)
    itab = jax.ShapeDtypeStruct((TAB_ROWS, N_EXPERTS), jnp.int32)
    return pl.pallas_call(
        _plan_kernel,
        out_shape=(jax.ShapeDtypeStruct((T_TOK, E_PAD), F32), tab, tab, itab, itab, itab,
                   jax.ShapeDtypeStruct((SUB, E_PAD), jnp.int32)),
        compiler_params=pltpu.CompilerParams(vmem_limit_bytes=VMEM_LIMIT),
        name="moe_plan",
    )(comb)


def _sort_matrix(b, z_ref, loff_ref, ptab_ref):
    loff = loff_ref[pl.ds(b, 1), :]
    size = ptab_ref[pl.ds(b, 1), :]
    r = lax.broadcasted_iota(jnp.int32, (RB, E_PAD), 0).astype(F32)
    owner = jnp.where(r >= loff, jnp.where(r < loff + size, 1.0, 0.0), 0.0)
    rank1 = r[:, 0:1] - jnp.sum(owner * loff, axis=-1, keepdims=True) + 1.0
    zt = z_ref[...].T
    v = jnp.dot(owner.astype(BF16), zt.astype(BF16), preferred_element_type=F32)
    return jnp.where(v == rank1, 1.0, 0.0), owner


def _segment_copies(b, loff_s, gst_s, p8_s, make_copy):
    def per_expert(e, counts):
        chunks = p8_s[b, e]
        n_big = lax.shift_right_logical(chunks, BIG_SHIFT)
        n_small = chunks - n_big * (BIG_ROWS // SEG_ALIGN)
        local0 = loff_s[b, e]
        global0 = gst_s[b, e]

        def big(j, _):
            make_copy(pl.multiple_of(local0 + j * BIG_ROWS, SEG_ALIGN),
                      pl.multiple_of(global0 + j * BIG_ROWS, SEG_ALIGN), BIG_ROWS).start()
            return 0

        def small(j, _):
            off = n_big * BIG_ROWS + j * SEG_ALIGN
            make_copy(pl.multiple_of(local0 + off, SEG_ALIGN),
                      pl.multiple_of(global0 + off, SEG_ALIGN), SEG_ALIGN).start()
            return 0

        lax.fori_loop(0, n_big, big, 0)
        lax.fori_loop(0, n_small, small, 0)
        return counts[0] + n_big, counts[1] + n_small

    return lax.fori_loop(0, N_EXPERTS, per_expert, (0, 0))


def _wait_copies(counts, make_copy):
    for n, rows in zip(counts, (BIG_ROWS, SEG_ALIGN)):
        def wait_one(i, _, rows=rows):
            make_copy(0, 0, rows).wait()
            return 0
        lax.fori_loop(0, n, wait_one, 0)


def _zero_rows_from(hbm, zeros_vmem, first_row, sem):
    def copy_to(row):
        return pltpu.make_async_copy(zeros_vmem, hbm.at[pl.ds(row, TM)], sem)

    n_full = lax.shift_right_logical(R_TOT - first_row, TM_SHIFT)

    def start_one(j, _):
        copy_to(pl.multiple_of(first_row + j * TM, SEG_ALIGN)).start()
        return 0

    def wait_one(j, _):
        copy_to(0).wait()
        return 0

    lax.fori_loop(0, n_full, start_one, 0)
    lax.fori_loop(0, n_full, wait_one, 0)
    last = copy_to(R_TOT - TM)
    last.start()
    last.wait()


def _dispatch_kernel(loff_s, gst_s, p8_s, eoff_s, h_ref, z_ref, loff_ref, ptab_ref, xs_hbm, xbuf, sem, pending):
    b = pl.program_id(0)
    slot = lax.rem(b, 2)

    def copies_of(s):
        def make_copy(local_row, global_row, rows):
            return pltpu.make_async_copy(xbuf.at[s, pl.ds(local_row, rows)], xs_hbm.at[pl.ds(global_row, rows)],
                                         sem.at[s])
        return make_copy

    pm = _sort_matrix(b, z_ref, loff_ref, ptab_ref)[0].astype(BF16)

    @pl.when(b >= 2)
    def _():
        _wait_copies((pending[2 * slot], pending[2 * slot + 1]), copies_of(slot))

    xbuf[slot] = jnp.dot(pm, h_ref[...], preferred_element_type=F32)
    n_big, n_small = _segment_copies(b, loff_s, gst_s, p8_s, copies_of(slot))
    pending[2 * slot] = n_big
    pending[2 * slot + 1] = n_small

    @pl.when(b == N_BLK - 1)
    def _():
        _wait_copies((n_big, n_small), copies_of(slot))
        _wait_copies((pending[2 * (1 - slot)], pending[2 * (1 - slot) + 1]), copies_of(1 - slot))
        xbuf[0, 0:TM, :] = jnp.zeros((TM, D), F32)
        _zero_rows_from(xs_hbm, xbuf.at[0, 0:TM], eoff_s[0, N_EXPERTS], sem.at[0])


def _dispatch(h2, z, loff, ptab, loff_i, gst_i, p8_i, eoff_i):
    whole = pl.BlockSpec((TAB_ROWS, E_PAD), lambda b, *_: (0, 0))
    return pl.pallas_call(
        _dispatch_kernel,
        grid_spec=pltpu.PrefetchScalarGridSpec(
            num_scalar_prefetch=4,
            grid=(N_BLK,),
            in_specs=[pl.BlockSpec((TBLK, D), lambda b, *_: (b, 0)),
                      pl.BlockSpec((TBLK, E_PAD), lambda b, *_: (b, 0)),
                      whole, whole],
            out_specs=pl.BlockSpec(memory_space=pl.ANY),
            scratch_shapes=[pltpu.VMEM((2, RB, D), F32), pltpu.SemaphoreType.DMA((2,)),
                            pltpu.SMEM((4,), jnp.int32)]),
        out_shape=jax.ShapeDtypeStruct((R_TOT, D), F32),
        compiler_params=_cparams(("arbitrary",)),
        name="moe_dispatch",
    )(loff_i, gst_i, p8_i, eoff_i, h2, z, loff, ptab)


MAX_TILES = R_TOT // TM + N_EXPERTS
X_AHEAD = 3
X_SLOTS = X_AHEAD + 1
Y_SLOTS = 2
W_AHEAD = 2
W_SLOTS = W_AHEAD + 1
N_MATS = 3
FF_CHUNK = 512
CAST_ROWS = 64
TILE_STEP = 64


def _expert_kernel(eoff_s, xs_hbm, wg_hbm, wu_hbm, wd_hbm, bg_ref, bu_ref, bd_ref, ys_hbm,
                   wst, wbf, xin, yout, act_ref, w_sem, x_sem, y_sem, t_exp, t_row, t_first, t_valid, live):
    def add_expert(e, carry):
        n_t, n_live = carry
        start = eoff_s[0, e]
        count = eoff_s[0, e + 1] - start
        tiles = lax.shift_right_logical(count + (TM - 1), TM_SHIFT)

        def add_tile(i, _):
            t_exp[n_t + i] = e
            t_row[n_t + i] = start + i * TM
            t_first[n_t + i] = jnp.where(i == 0, 1, 0)
            t_valid[n_t + i] = jnp.minimum(count - i * TM, TM)
            return 0

        lax.fori_loop(0, tiles, add_tile, 0)
        has_rows = jnp.where(tiles > 0, 1, 0)

        @pl.when(tiles > 0)
        def _():
            live[n_live] = e

        return n_t + tiles, n_live + has_rows

    n_tiles, n_live = lax.fori_loop(0, N_EXPERTS, add_expert, (0, 0))

    def w_copies(e, slot):
        return [pltpu.make_async_copy(w.at[e], wst.at[slot, m], w_sem.at[slot])
                for m, w in enumerate((wg_hbm, wu_hbm, wd_hbm))]

    def tile_rows(j):
        return pl.ds(pl.multiple_of(t_row[j], SEG_ALIGN), TM)

    def x_copy(j, slot):
        return pltpu.make_async_copy(xs_hbm.at[tile_rows(j)], xin.at[slot], x_sem.at[slot])

    def y_pieces(j, op):
        ys = lax.rem(j, Y_SLOTS)
        valid = t_valid[j]
        row0 = t_row[j]

        def piece(off, rows):
            return pltpu.make_async_copy(yout.at[ys, pl.ds(off, rows)],
                                         ys_hbm.at[pl.ds(pl.multiple_of(row0 + off, SEG_ALIGN), rows)],
                                         y_sem.at[ys])

        @pl.when(valid == TM)
        def _():
            op(piece(0, TM))

        @pl.when(valid < TM)
        def _():
            n_big = lax.shift_right_logical(valid, BIG_ROWS.bit_length() - 1)
            n_small = lax.shift_right_logical(valid - n_big * BIG_ROWS, SEG_ALIGN.bit_length() - 1)

            def big(i, _):
                op(piece(pl.multiple_of(i * BIG_ROWS, SEG_ALIGN), BIG_ROWS))
                return 0

            def small(i, _):
                op(piece(pl.multiple_of(n_big * BIG_ROWS + i * SEG_ALIGN, SEG_ALIGN), SEG_ALIGN))
                return 0

            lax.fori_loop(0, n_big, big, 0)
            lax.fori_loop(0, n_small, small, 0)

    for ahead in range(W_AHEAD):
        @pl.when(n_live > ahead)
        def _(ahead=ahead):
            for cp in w_copies(live[ahead], ahead):
                cp.start()

    for ahead in range(X_AHEAD):
        @pl.when(n_tiles > ahead)
        def _(ahead=ahead):
            x_copy(ahead, ahead).start()

    def body(j, k):
        e = t_exp[j]

        @pl.when(t_first[j] == 1)
        def _():
            ws = lax.rem(k, W_SLOTS)

            @pl.when(k + W_AHEAD < n_live)
            def _():
                for cp in w_copies(live[k + W_AHEAD], lax.rem(k + W_AHEAD, W_SLOTS)):
                    cp.start()

            for cp in w_copies(e, ws):
                cp.wait()

            def cast_rows(r, _):
                rows = pl.ds(pl.multiple_of(r * CAST_ROWS, CAST_ROWS), CAST_ROWS)
                for m in range(N_MATS):
                    wbf[m, rows, :] = wst[ws, m, rows, :].astype(BF16)
                return 0

            lax.fori_loop(0, D // CAST_ROWS, cast_rows, 0)

        slot = lax.rem(j, X_SLOTS)
        x_copy(j, slot).wait()

        @pl.when(j + X_AHEAD < n_tiles)
        def _():
            x_copy(j + X_AHEAD, lax.rem(j + X_AHEAD, X_SLOTS)).start()

        @pl.when(j >= Y_SLOTS)
        def _():
            y_pieces(j - Y_SLOTS, lambda cp: cp.wait())

        ys = lax.rem(j, Y_SLOTS)

        def compute(rows):
            x = xin[slot, 0:rows, :].astype(BF16)
            for c in range(D // FF_CHUNK):
                cols = slice(c * FF_CHUNK, (c + 1) * FF_CHUNK)
                gate = jnp.dot(x, wbf[0, :, cols], preferred_element_type=F32) + bg_ref[e][:, cols]
                up = jnp.dot(x, wbf[1, :, cols], preferred_element_type=F32) + bu_ref[e][:, cols]
                gate = jnp.minimum(gate, SWIGLU_LIMIT)
                up = jnp.clip(up, -SWIGLU_LIMIT, SWIGLU_LIMIT)
                act_ref[0:rows, cols] = ((up + 1.0) * gate * jax.nn.sigmoid(SWIGLU_ALPHA * gate)).astype(BF16)
            yout[ys, 0:rows, :] = jnp.dot(act_ref[0:rows, :], wbf[2], preferred_element_type=F32) + bd_ref[e]

        valid = t_valid[j]
        for rows in range(TILE_STEP, TM + 1, TILE_STEP):
            @pl.when(jnp.logical_and(valid > rows - TILE_STEP, valid <= rows))
            def _(rows=rows):
                compute(rows)

        y_pieces(j, lambda cp: cp.start())
        return k + t_first[j]

    lax.fori_loop(0, n_tiles, body, 0)

    for back in range(Y_SLOTS, 0, -1):
        @pl.when(n_tiles >= back)
        def _(back=back):
            y_pieces(n_tiles - back, lambda cp: cp.wait())

    yout[0] = jnp.zeros((TM, D), F32)
    _zero_rows_from(ys_hbm, yout.at[0], eoff_s[0, N_EXPERTS], y_sem.at[0])


def _experts(xs, eoff_i, w_gate, b_gate, w_up, b_up, w_down, b_down):
    hbm = pl.BlockSpec(memory_space=pl.ANY)
    bspec = pl.BlockSpec((N_EXPERTS, 1, D), lambda i, *_: (0, 0, 0))
    return pl.pallas_call(
        _expert_kernel,
        grid_spec=pltpu.PrefetchScalarGridSpec(
            num_scalar_prefetch=1,
            grid=(1,),
            in_specs=[hbm, hbm, hbm, hbm, bspec, bspec, bspec],
            out_specs=hbm,
            scratch_shapes=[pltpu.VMEM((W_SLOTS, N_MATS, D, D), F32), pltpu.VMEM((N_MATS, D, D), BF16),
                            pltpu.VMEM((X_SLOTS, TM, D), F32), pltpu.VMEM((Y_SLOTS, TM, D), F32),
                            pltpu.VMEM((TM, D), BF16),
                            pltpu.SemaphoreType.DMA((W_SLOTS,)), pltpu.SemaphoreType.DMA((X_SLOTS,)),
                            pltpu.SemaphoreType.DMA((Y_SLOTS,)),
                            pltpu.SMEM((MAX_TILES,), jnp.int32), pltpu.SMEM((MAX_TILES,), jnp.int32),
                            pltpu.SMEM((MAX_TILES,), jnp.int32), pltpu.SMEM((MAX_TILES,), jnp.int32),
                            pltpu.SMEM((N_EXPERTS,), jnp.int32)]),
        out_shape=jax.ShapeDtypeStruct((R_TOT, D), F32),
        compiler_params=_cparams(("arbitrary",)),
        name="moe_experts",
    )(eoff_i, xs, w_gate, w_up, w_down, b_gate.reshape(N_EXPERTS, 1, D), b_up.reshape(N_EXPERTS, 1, D),
      b_down.reshape(N_EXPERTS, 1, D))


def _combine_kernel(loff_s, gst_s, p8_s, ys_hbm, z_ref, comb_ref, loff_ref, ptab_ref, x1_ref, mod_ref, gf_ref,
                    yctx_ref, ylat_ref, ybuf, sem):
    b = pl.program_id(0)

    def make_copy(local_row, global_row, rows):
        return pltpu.make_async_copy(ys_hbm.at[pl.ds(global_row, rows)], ybuf.at[pl.ds(local_row, rows)], sem)

    started = _segment_copies(b, loff_s, gst_s, p8_s, make_copy)

    used_chunks = lax.shift_right_logical(loff_s[b, N_EXPERTS - 1], 3) + p8_s[b, N_EXPERTS - 1]

    def zero_chunk(j, _):
        ybuf[pl.ds(pl.multiple_of(j * SEG_ALIGN, SEG_ALIGN), SEG_ALIGN), :] = jnp.zeros((SEG_ALIGN, D), F32)
        return 0

    lax.fori_loop(used_chunks, RB // SEG_ALIGN, zero_chunk, 0)

    pm, owner = _sort_matrix(b, z_ref, loff_ref, ptab_ref)
    comb = comb_ref[...]
    c_hi = comb.astype(BF16)
    rest = comb - c_hi.astype(F32)
    c_mid = rest.astype(BF16)
    c_lo = (rest - c_mid.astype(F32)).astype(BF16)
    pmb = pm.astype(BF16)
    moved = (jnp.dot(pmb, c_hi, preferred_element_type=F32) + jnp.dot(pmb, c_mid, preferred_element_type=F32)
             + jnp.dot(pmb, c_lo, preferred_element_type=F32))
    w_row = jnp.sum(owner * moved, axis=-1, keepdims=True)
    pmt = pm.T.astype(BF16)
    _wait_copies(started, make_copy)
    y = (ybuf[...] * w_row).astype(BF16)
    moe = jnp.dot(pmt, y, preferred_element_type=F32)
    gate2 = mod_ref[0][:, 5 * D:6 * D]
    x2 = x1_ref[0] + gate2 * moe
    y = _rms(x2, gf_ref[...])

    @pl.when(b < N_CTX_BLK)
    def _():
        yctx_ref[0] = y

    @pl.when(b >= N_CTX_BLK)
    def _():
        ylat_ref[0] = y


def _combine(ys, z, comb, loff, ptab, x1, modv, norm_f_g, loff_i, gst_i, p8_i):
    whole = pl.BlockSpec((TAB_ROWS, E_PAD), lambda b, *_: (0, 0))
    tok = pl.BlockSpec((TBLK, E_PAD), lambda b, *_: (b, 0))
    return pl.pallas_call(
        _combine_kernel,
        grid_spec=pltpu.PrefetchScalarGridSpec(
            num_scalar_prefetch=3,
            grid=(N_BLK,),
            in_specs=[pl.BlockSpec(memory_space=pl.ANY), tok, tok, whole, whole,
                      pl.BlockSpec((1, TBLK, D), lambda b, *_: (b, 0, 0)),
                      pl.BlockSpec((1, 1, 6 * D), lambda b, *_: (b, 0, 0)),
                      pl.BlockSpec((1, D), lambda b, *_: (0, 0))],
            out_specs=[pl.BlockSpec((1, TBLK, D), lambda b, *_: (jnp.minimum(b, N_CTX_BLK - 1), 0, 0)),
                       pl.BlockSpec((1, TBLK, D), lambda b, *_: (jnp.maximum(b - N_CTX_BLK, 0), 0, 0))],
            scratch_shapes=[pltpu.VMEM((RB, D), F32), pltpu.SemaphoreType.DMA(())]),
        out_shape=[jax.ShapeDtypeStruct((N_CTX_BLK, TBLK, D), F32),
                   jax.ShapeDtypeStruct((N_BLK - N_CTX_BLK, TBLK, D), F32)],
        compiler_params=_cparams(("arbitrary",)),
        name="moe_combine",
    )(loff_i, gst_i, p8_i, ys, z, comb, loff, ptab, x1, modv, norm_f_g.reshape(1, D))


def _moe_and_final(x1, h2, comb, modg, norm_f_g, w_gate, b_gate, w_up, b_up, w_down, b_down):
    z, ptab, loff, loff_i, gst_i, p8_i, eoff_i = _plan(comb)
    xs = _dispatch(h2, z, loff, ptab, loff_i, gst_i, p8_i, eoff_i)
    ys = _experts(xs, eoff_i, w_gate, b_gate, w_up, b_up, w_down, b_down)
    return _combine(ys, z, comb, loff, ptab, x1, modg.reshape(N_BLK, 1, 6 * D), norm_f_g,
                    loff_i, gst_i, p8_i)


def _grid_pos_embed(n_tokens, dim):
    rows = n_tokens // GRID_W
    t = np.arange(rows * GRID_W)
    r = (t // GRID_W).astype(np.float32)
    col = (t % GRID_W).astype(np.float32)
    q = dim // 4
    omega = (1.0 / np.float32(POS_TEMP) ** (np.arange(q, dtype=np.float32) / np.float32(q))).astype(np.float32)

    def emb(p):
        a = p[:, None] * omega[None, :]
        return np.concatenate([np.sin(a), np.cos(a)], axis=-1)

    return np.concatenate([emb(r), emb(col)], axis=-1).astype(np.float32)


def kernel(x_prompt, x_sample, c, state_s5_re, state_s5_im, c_ctx, w_ada, b_ada, norm1_g, w_in, s5_lam_re,
           s5_lam_im, s5_log_dt, s5_b_re, s5_b_im, s5_c_re, s5_c_im, s5_d, s5_w_glu, w_fnet, w_out, norm2_g,
           w_router, b_router, w_gate, b_gate, w_up, b_up, w_down, b_down, norm_f_g):
    n_ctx, n_lat = x_prompt.shape[0], x_sample.shape[0]
    assert x_prompt.shape == (SUB * (N_GRP - 1), L_BLK, D) and x_sample.shape == (2, LAT_CHUNKS * L_BLK, D)
    assert w_ada.shape[0] == 1, "one trunk layer"
    layer = 0

    cvec = jnp.concatenate([jnp.broadcast_to(c_ctx[None], (n_ctx, D)), jnp.repeat(c, LAT_CHUNKS, axis=0)], axis=0)
    modg = _adaln(cvec, w_ada[layer], b_ada[layer]).reshape(N_GRP, SUB, 1, 6 * D)

    xp4 = x_prompt.reshape(N_GRP - 1, SUB, L_BLK, D)
    xs4 = x_sample.reshape(1, SUB, L_BLK, D)
    pos3 = jnp.asarray(_grid_pos_embed(LAT_CHUNKS * L_BLK, D).reshape(LAT_CHUNKS, L_BLK, D))
    perm = _perm_matrix()

    us, uf = _pre(xp4, xs4, pos3, modg, norm1_g[layer], w_in[layer], jnp.asarray(perm, BF16))

    bbmat, ccmat, a8 = _s5_params(s5_lam_re[layer], s5_lam_im[layer], s5_log_dt[layer], s5_b_re[layer],
                                  s5_b_im[layer], s5_c_re[layer], s5_c_im[layer])

    st = jnp.stack([state_s5_re[:, layer], state_s5_im[:, layer]], axis=2)
    st = st.reshape(n_lat, 2, 2, N_S5_CHUNKS, CH_P)
    st = jnp.transpose(st, (1, 3, 0, 2, 4)).reshape(2, N_S5_CHUNKS, n_lat, 2 * CH_P)
    init8 = jnp.zeros((2, N_S5_CHUNKS, n_lat, LAT_CHUNKS, 2 * CH_P), F32)
    init8 = init8.at[0, :, :, 0].set(st[0]).at[1, :, :, LAT_CHUNKS - 1].set(st[1])
    init8 = init8.reshape(2, N_S5_CHUNKS, SUB, 2 * CH_P)

    y_s5, fin = _s5(us, bbmat, ccmat, a8, init8, s5_d[layer])

    m1, m2 = _fnet_weights(w_fnet[layer])
    yf = _fnet(uf, m1, m2)

    x1, h2, comb = _post(y_s5, yf, xp4, xs4, pos3, modg, s5_w_glu[layer], w_out[layer], norm2_g[layer],
                         w_router[layer], b_router[layer], jnp.asarray(perm.T, BF16))

    y_prompt, y_lat = _moe_and_final(x1.reshape(N_BLK, TBLK, D), h2.reshape(T_TOK, D), comb.reshape(T_TOK, E_PAD),
                                     modg, norm_f_g, w_gate[layer], b_gate[layer], w_up[layer], b_up[layer],
                                     w_down[layer], b_down[layer])
    y_sample = y_lat.reshape(n_lat, LAT_CHUNKS * L_BLK, D)

    fin = fin[:, :N_GRP - 1].reshape(2, N_GRP - 1, SUB, N_S5_CHUNKS, 2, CH_P)
    fin = jnp.transpose(fin, (4, 1, 2, 0, 3, 5)).reshape(2, n_ctx, 1, 2, N_S5_GROUPS, S5_P)
    return (y_prompt, y_sample, fin[0], fin[1])
```

```python
import functools
import math

import numpy as np
import jax
import jax.numpy as jnp
from jax import lax
from jax.experimental import pallas as pl
from jax.experimental.pallas import tpu as pltpu

F32 = jnp.float32
BF16 = jnp.bfloat16
HIGHEST = lax.Precision.HIGHEST

D = 1024
D_S5 = 768
S5_H = 16
S5_P = 64
N_S5_GROUPS = 48
D_FN = 256
FN_GW = 64
N_EXPERTS = 32
TOP_K = 4
E_PAD = 128
SWIGLU_LIMIT = 7.0
SWIGLU_ALPHA = 1.702
RMS_EPS = 1e-6
POS_TEMP = 10000.0
GRID_W = 64

L_BLK = 256
SUB = 8
N_GRP = 3
GRP_ROWS = L_BLK * SUB
LAT_CHUNKS = 4
T_TOK = N_GRP * GRP_ROWS
TC = 128
N_TC = L_BLK // TC
PERM_T = 32
S5_CHUNK = 16
N_S5_CHUNKS = N_S5_GROUPS // S5_CHUNK
CH_U = S5_CHUNK * S5_H
CH_P = S5_CHUNK * S5_P
V7X_VMEM_BYTES = 64 * 1024 * 1024
VMEM_LIMIT = V7X_VMEM_BYTES * 7 // 8


def _cparams(sem):
    return pltpu.CompilerParams(dimension_semantics=sem, vmem_limit_bytes=VMEM_LIMIT)


ADA_K = 128


def _adaln_kernel(c_ref, w_ref, b_ref, o_ref):
    @pl.when(pl.program_id(0) == 0)
    def _():
        o_ref[...] = jnp.broadcast_to(b_ref[...], o_ref.shape)

    c = c_ref[...]
    s = c * jax.nn.sigmoid(c)
    s_hi = s.astype(BF16)
    s_lo = (s - s_hi.astype(F32)).astype(BF16)
    w = w_ref[...]
    w_hi = w.astype(BF16)
    w_lo = (w - w_hi.astype(F32)).astype(BF16)
    o_ref[...] += (jnp.dot(s_hi, w_hi, preferred_element_type=F32) + jnp.dot(s_lo, w_hi, preferred_element_type=F32)
                   + jnp.dot(s_hi, w_lo, preferred_element_type=F32))


def _adaln(cvec, w_ada, b_ada):
    k_dim, n = w_ada.shape
    rows = cvec.shape[0]
    return pl.pallas_call(
        _adaln_kernel,
        grid=(k_dim // ADA_K,),
        in_specs=[pl.BlockSpec((rows, ADA_K), lambda k: (0, k)),
                  pl.BlockSpec((ADA_K, n), lambda k: (k, 0)),
                  pl.BlockSpec((1, n), lambda k: (0, 0))],
        out_specs=pl.BlockSpec((rows, n), lambda k: (0, 0)),
        out_shape=jax.ShapeDtypeStruct((rows, n), F32),
        compiler_params=_cparams(("arbitrary",)),
        name="adaln",
    )(cvec, w_ada, b_ada.reshape(1, n))


def _zoh(lam_re, lam_im, log_dt):
    dt = jnp.exp(log_dt)
    a_re = jnp.minimum(lam_re, -1e-4)
    a_im = lam_im
    mag = jnp.exp(a_re * dt)
    ab_re = mag * jnp.cos(a_im * dt)
    ab_im = mag * jnp.sin(a_im * dt)
    den = a_re * a_re + a_im * a_im
    nr = ab_re - 1.0
    f_re = (nr * a_re + ab_im * a_im) / den
    f_im = (ab_im * a_re - nr * a_im) / den
    return ab_re, ab_im, f_re, f_im


def _s5_params_kernel(lre3_ref, lim3_ref, ldt3_ref, lre2_ref, lim2_ref, ldt2_ref, bre_ref, bim_ref,
                      cre_ref, cim_ref, bb_ref, cc_ref, a_ref):
    exact = functools.partial(jnp.dot, preferred_element_type=F32, precision=HIGHEST)
    spread = jnp.where(lax.broadcasted_iota(jnp.int32, (S5_P, CH_P), 1) % S5_P
                       == lax.broadcasted_iota(jnp.int32, (S5_P, CH_P), 0), 1.0, 0.0)

    def block_diag(m):
        wide = jnp.dot(m.reshape(CH_U, S5_P).astype(BF16), spread.astype(BF16), preferred_element_type=F32)
        row_g = lax.broadcasted_iota(jnp.int32, (CH_U, CH_P), 0) // S5_H
        col_g = lax.broadcasted_iota(jnp.int32, (CH_U, CH_P), 1) // S5_P
        return jnp.where(row_g == col_g, wide, 0.0)

    _, _, f_re, f_im = _zoh(lre3_ref[0, 0], lim3_ref[0, 0], ldt3_ref[0, 0])
    b_re = bre_ref[0, 0]
    b_im = bim_ref[0, 0]
    bb_ref[0, 0] = jnp.concatenate([block_diag(f_re * b_re - f_im * b_im),
                                    block_diag(f_re * b_im + f_im * b_re)], axis=1).astype(BF16)
    cc_ref[0, 0] = jnp.concatenate([block_diag(cre_ref[0, 0]).T,
                                    -block_diag(cim_ref[0, 0]).T], axis=0).astype(BF16)

    ab_re, ab_im, _, _ = _zoh(lre2_ref[0, 0], lim2_ref[0, 0], ldt2_ref[0, 0])
    row_g = lax.broadcasted_iota(jnp.int32, (S5_CHUNK, CH_P), 0)
    col_g = lax.broadcasted_iota(jnp.int32, (S5_CHUNK, CH_P), 1) // S5_P

    def lane_row(a):
        flat = jnp.sum(jnp.where(row_g == col_g, exact(a, spread), 0.0), axis=0, keepdims=True)
        return jnp.broadcast_to(flat, (SUB, CH_P))

    a_ref[0, 0] = jnp.concatenate([lane_row(ab_re), lane_row(ab_im)], axis=1)


def _s5_params(lam_re, lam_im, log_dt, b_re, b_im, c_re, c_im):
    nc = N_S5_CHUNKS
    ldt = jnp.broadcast_to(log_dt[:, :, None], (2, N_S5_GROUPS, S5_P))
    g3 = lambda a: a.reshape(2, nc, S5_CHUNK, 1, S5_P)
    g2 = lambda a: a.reshape(2, nc, S5_CHUNK, S5_P)
    ghp = lambda a: a.reshape(2, nc, S5_CHUNK, S5_H, S5_P)
    spec3 = pl.BlockSpec((1, 1, S5_CHUNK, 1, S5_P), lambda d, c: (d, c, 0, 0, 0))
    spec2 = pl.BlockSpec((1, 1, S5_CHUNK, S5_P), lambda d, c: (d, c, 0, 0))
    spec_ghp = pl.BlockSpec((1, 1, S5_CHUNK, S5_H, S5_P), lambda d, c: (d, c, 0, 0, 0))
    return pl.pallas_call(
        _s5_params_kernel,
        grid=(2, nc),
        in_specs=[spec3, spec3, spec3, spec2, spec2, spec2, spec_ghp, spec_ghp, spec_ghp, spec_ghp],
        out_specs=[pl.BlockSpec((1, 1, CH_U, 2 * CH_P), lambda d, c: (d, c, 0, 0)),
                   pl.BlockSpec((1, 1, 2 * CH_P, CH_U), lambda d, c: (d, c, 0, 0)),
                   pl.BlockSpec((1, 1, SUB, 2 * CH_P), lambda d, c: (d, c, 0, 0))],
        out_shape=[jax.ShapeDtypeStruct((2, nc, CH_U, 2 * CH_P), BF16),
                   jax.ShapeDtypeStruct((2, nc, 2 * CH_P, CH_U), BF16),
                   jax.ShapeDtypeStruct((2, nc, SUB, 2 * CH_P), F32)],
        compiler_params=_cparams(("arbitrary", "arbitrary")),
        name="s5_params",
    )(g3(lam_re), g3(lam_im), g3(ldt), g2(lam_re), g2(lam_im), g2(ldt),
      ghp(jnp.swapaxes(b_re, -1, -2)), ghp(jnp.swapaxes(b_im, -1, -2)), ghp(c_re), ghp(c_im))


def _rms(x, g):
    return x * lax.rsqrt(jnp.mean(x * x, axis=-1, keepdims=True) + RMS_EPS) * g


def _load_x(g, xp_ref, xs_ref, pos_ref):
    is_lat = g == N_GRP - 1
    pos = pos_ref[...]
    pos8 = jnp.concatenate([pos, pos], axis=0)
    return jnp.where(is_lat, xs_ref[0] + pos8, xp_ref[0])


def _x_specs():
    n_ctx = N_GRP - 1
    xp_spec = pl.BlockSpec(
        (1, SUB, TC, D),
        lambda g, tc: (jnp.minimum(g, n_ctx - 1), 0, jnp.where(g >= n_ctx, N_TC - 1, tc), 0))
    xs_spec = pl.BlockSpec((1, SUB, TC, D), lambda g, tc: (0, 0, jnp.where(g >= n_ctx, tc, 0), 0))
    pos_spec = pl.BlockSpec((LAT_CHUNKS, TC, D), lambda g, tc: (0, tc, 0))
    return xp_spec, xs_spec, pos_spec


def _perm_matrix():
    p = np.zeros((SUB * PERM_T, SUB * PERM_T), np.float32)
    for s in range(SUB):
        for j in range(PERM_T):
            p[j * SUB + s, s * PERM_T + j] = 1.0
    return p


def _pre_kernel(xp_ref, xs_ref, pos_ref, mod_ref, g1_ref, win_ref, perm_ref, us_ref, uf_ref):
    g = pl.program_id(0)
    x = _load_x(g, xp_ref, xs_ref, pos_ref)
    mod = mod_ref[0]
    shift1 = mod[:, :, 0:D]
    scale1 = mod[:, :, D:2 * D]
    h = _rms(x, g1_ref[...]) * (1.0 + scale1) + shift1
    h2d = h.reshape(SUB * TC, D).astype(BF16)
    proj = jnp.dot(h2d, win_ref[...].astype(BF16), preferred_element_type=F32)
    uf_ref[0] = proj[:, D_S5:].astype(BF16).reshape(SUB, TC, D_FN)
    u = proj[:, :D_S5].astype(BF16)
    perm = perm_ref[...]
    for q in range(TC // PERM_T):
        piece = jnp.concatenate(
            [u[s * TC + q * PERM_T: s * TC + (q + 1) * PERM_T] for s in range(SUB)], axis=0)
        us_ref[0, q * SUB * PERM_T:(q + 1) * SUB * PERM_T, :] = jnp.dot(
            perm, piece, preferred_element_type=F32).astype(BF16)


def _pre(xp4, xs4, pos3, modg, norm1_g, w_in, perm):
    xp_spec, xs_spec, pos_spec = _x_specs()
    return pl.pallas_call(
        _pre_kernel,
        grid=(N_GRP, N_TC),
        in_specs=[xp_spec, xs_spec, pos_spec,
                  pl.BlockSpec((1, SUB, 1, 6 * D), lambda g, tc: (g, 0, 0, 0)),
                  pl.BlockSpec((1, D), lambda g, tc: (0, 0)),
                  pl.BlockSpec((D, D), lambda g, tc: (0, 0)),
                  pl.BlockSpec((SUB * PERM_T, SUB * PERM_T), lambda g, tc: (0, 0))],
        out_specs=[pl.BlockSpec((1, SUB * TC, D_S5), lambda g, tc: (g, tc, 0)),
                   pl.BlockSpec((1, SUB, TC, D_FN), lambda g, tc: (g, 0, tc, 0))],
        out_shape=[jax.ShapeDtypeStruct((N_GRP, GRP_ROWS, D_S5), BF16),
                   jax.ShapeDtypeStruct((N_GRP, SUB, L_BLK, D_FN), BF16)],
        compiler_params=_cparams(("arbitrary", "arbitrary")),
        name="pre_mixer",
    )(xp4, xs4, pos3, modg, norm1_g.reshape(1, D), w_in, perm)


def _cmul(ar, ai, br, bi):
    return ar * br - ai * bi, ar * bi + ai * br


S5_BLK = 512
S5_NBLK = GRP_ROWS // S5_BLK
S5_STEPS = S5_BLK // SUB
S5_FIX_UNROLL = 8


def _s5_kernel(us_ref, bb_ref, cc_ref, a_ref, init_ref, dskip_ref, y_ref, fin_ref, in_a, in_b, out_a, out_b):
    g = pl.program_id(0)
    d = pl.program_id(2)
    is_lat = g == N_GRP - 1
    re_all = slice(0, CH_P)
    im_all = slice(CH_P, 2 * CH_P)
    a_re = a_ref[0, 0, :, re_all]
    a_im = a_ref[0, 0, :, im_all]
    bufs_in = (in_a, in_b)
    bufs_out = (out_a, out_b)

    def time_rows(k):
        blk = k + d * (S5_NBLK - 1 - 2 * k)
        return pl.ds(pl.multiple_of(blk * S5_BLK, S5_BLK), S5_BLK)

    def slot_rows(k):
        return slice((k // 2) * S5_BLK, (k // 2 + 1) * S5_BLK)

    def step_rows(k, j):
        q = j + d * (S5_STEPS - 1 - 2 * j)
        return pl.ds(pl.multiple_of((k // 2) * S5_BLK + q * SUB, SUB), SUB)

    def proj_in(k):
        bufs_in[k % 2][slot_rows(k), :] = jnp.dot(us_ref[0, time_rows(k), :], bb_ref[0, 0],
                                                  preferred_element_type=F32)

    def scan_block(k, carry):
        src, dst = bufs_in[k % 2], bufs_out[k % 2]
        s_re, s_im = carry
        for j in range(S5_STEPS):
            rows = step_rows(k, j)
            n_re = a_re * s_re - a_im * s_im + src[rows, re_all]
            n_im = a_re * s_im + a_im * s_re + src[rows, im_all]
            dst[rows, re_all] = n_re
            dst[rows, im_all] = n_im
            s_re, s_im = n_re, n_im
        return s_re, s_im

    def proj_out(k):
        rows = time_rows(k)
        yb = jnp.dot(bufs_out[k % 2][slot_rows(k), :].astype(BF16), cc_ref[0, 0], preferred_element_type=F32)
        y_ref[0, rows, :] = y_ref[0, rows, :] + yb

    @pl.when(d == 0)
    def _():
        y_ref[0] = dskip_ref[...] * us_ref[0].astype(F32)

    def scan_all(carry, with_out):
        proj_in(0)
        for k in range(S5_NBLK):
            if with_out and k >= 1:
                proj_out(k - 1)
            if k + 1 < S5_NBLK:
                proj_in(k + 1)
            carry = scan_block(k, carry)
        if with_out:
            proj_out(S5_NBLK - 1)
        fin_ref[0, 0, :, re_all] = carry[0]
        fin_ref[0, 0, :, im_all] = carry[1]
        return carry

    @pl.when(jnp.logical_not(is_lat))
    def _():
        zero = jnp.zeros((SUB, CH_P), F32)
        scan_all((zero, zero), True)

    @pl.when(is_lat)
    def _():
        f_re, f_im = scan_all((init_ref[0, 0, :, re_all], init_ref[0, 0, :, im_all]), False)

        sub_id = lax.broadcasted_iota(jnp.int32, (SUB, CH_P), 0) % LAT_CHUNKS
        fwd = d == 0
        lo = jnp.where(fwd, 1, 0)
        hi = jnp.where(fwd, LAT_CHUNKS - 1, LAT_CHUNKS - 2)
        keep = (sub_id >= lo) & (sub_id <= hi)

        def from_prev(v):
            return jnp.where(keep, jnp.where(fwd, pltpu.roll(v, 1, 0), pltpu.roll(v, SUB - 1, 0)), 0.0)

        p_re, p_im = a_re, a_im
        for _ in range(8):
            p_re, p_im = _cmul(p_re, p_im, p_re, p_im)
        t_re, t_im = f_re, f_im
        for _ in range(LAT_CHUNKS - 2):
            m_re, m_im = _cmul(p_re, p_im, from_prev(t_re), from_prev(t_im))
            t_re, t_im = f_re + m_re, f_im + m_im
        corr = (from_prev(t_re), from_prev(t_im))

        for k in range(S5_NBLK):
            dst = bufs_out[k % 2]

            def fix_body(i, carry, k=k, dst=dst):
                m_re, m_im = carry
                for j in range(S5_FIX_UNROLL):
                    rows = step_rows(k, i * S5_FIX_UNROLL + j)
                    m_re, m_im = _cmul(m_re, m_im, a_re, a_im)
                    dst[rows, re_all] = dst[rows, re_all] + m_re
                    dst[rows, im_all] = dst[rows, im_all] + m_im
                return m_re, m_im

            corr = lax.fori_loop(0, S5_STEPS // S5_FIX_UNROLL, fix_body, corr)

        for k in range(S5_NBLK):
            proj_out(k)


def _s5(us, bbmat, ccmat, a8, init8, d_skip):
    return pl.pallas_call(
        _s5_kernel,
        grid=(N_GRP, N_S5_CHUNKS, 2),
        in_specs=[pl.BlockSpec((1, GRP_ROWS, CH_U), lambda g, c, d: (g, 0, c)),
                  pl.BlockSpec((1, 1, CH_U, 2 * CH_P), lambda g, c, d: (d, c, 0, 0)),
                  pl.BlockSpec((1, 1, 2 * CH_P, CH_U), lambda g, c, d: (d, c, 0, 0)),
                  pl.BlockSpec((1, 1, SUB, 2 * CH_P), lambda g, c, d: (d, c, 0, 0)),
                  pl.BlockSpec((1, 1, SUB, 2 * CH_P), lambda g, c, d: (d, c, 0, 0)),
                  pl.BlockSpec((1, CH_U), lambda g, c, d: (0, c))],
        out_specs=[pl.BlockSpec((1, GRP_ROWS, CH_U), lambda g, c, d: (g, 0, c)),
                   pl.BlockSpec((1, 1, SUB, 2 * CH_P), lambda g, c, d: (d, g, 0, c))],
        out_shape=[jax.ShapeDtypeStruct((N_GRP, GRP_ROWS, D_S5), F32),
                   jax.ShapeDtypeStruct((2, N_GRP, SUB, N_S5_CHUNKS * 2 * CH_P), F32)],
        scratch_shapes=[pltpu.VMEM((GRP_ROWS // 2, 2 * CH_P), F32)] * 4,
        compiler_params=_cparams(("arbitrary", "arbitrary", "arbitrary")),
        name="s5_scan",
    )(us, bbmat, ccmat, a8, init8, d_skip.reshape(1, D_S5))


def _fnet_w_kernel(c_ref, s_ref, w_ref, m1_ref, m2_ref):
    w = w_ref[...]
    m1_ref[...] = jnp.dot(c_ref[...], w, preferred_element_type=F32, precision=HIGHEST).astype(BF16)
    m2_ref[...] = jnp.dot(s_ref[...], w, preferred_element_type=F32, precision=HIGHEST).astype(BF16)


FN_SEQ = LAT_CHUNKS
FN_CTX_STEPS = (N_GRP - 1) * SUB // FN_SEQ


def _fnet_kernel(u_ref, cos_s_ref, sin_s_ref, cos_l_ref, sin_l_ref, m1_ref, m2_ref, o_ref):
    i = pl.program_id(0)
    m1 = m1_ref[...]
    m2 = m2_ref[...]

    def mix(u, cos_ref, sin_ref):
        v1 = jnp.dot(u, m1, preferred_element_type=F32).astype(BF16)
        v2 = jnp.dot(u, m2, preferred_element_type=F32).astype(BF16)
        return (jnp.dot(cos_ref[...].astype(BF16), v1, preferred_element_type=F32)
                - jnp.dot(sin_ref[...].astype(BF16), v2, preferred_element_type=F32)).astype(BF16)

    @pl.when(i < FN_CTX_STEPS)
    def _():
        for s in range(FN_SEQ):
            o_ref[0, s] = mix(u_ref[0, s], cos_s_ref, sin_s_ref)

    @pl.when(i >= FN_CTX_STEPS)
    def _():
        u = u_ref[0].reshape(FN_SEQ * L_BLK, D_FN)
        o_ref[0] = mix(u, cos_l_ref, sin_l_ref).reshape(FN_SEQ, L_BLK, D_FN)


def _dft_tables(n):
    k = np.arange(n, dtype=np.int64)
    ang = (2.0 * np.pi / n) * ((k[:, None] * k[None, :]) % n).astype(np.float64)
    scale = 1.0 / math.sqrt(n)
    return (np.cos(ang) * scale).astype(np.float32), (np.sin(ang) * scale).astype(np.float32)


def _fnet(uf, m1, m2):
    n_lat_steps = SUB // FN_SEQ
    cos_s, sin_s = _dft_tables(L_BLK)
    cos_l, sin_l = _dft_tables(FN_SEQ * L_BLK)
    full = lambda i: (0, 0)
    per_grp = SUB // FN_SEQ
    blk = pl.BlockSpec(
        (1, FN_SEQ, L_BLK, D_FN),
        lambda i: (jnp.minimum(i // per_grp, N_GRP - 1),
                   jnp.where(i < FN_CTX_STEPS, i % per_grp, i - FN_CTX_STEPS), 0, 0))
    table = lambda n: pl.BlockSpec((n, n), full)
    return pl.pallas_call(
        _fnet_kernel,
        grid=(FN_CTX_STEPS + n_lat_steps,),
        in_specs=[blk, table(L_BLK), table(L_BLK), table(FN_SEQ * L_BLK), table(FN_SEQ * L_BLK),
                  table(D_FN), table(D_FN)],
        out_specs=blk,
        out_shape=jax.ShapeDtypeStruct(uf.shape, BF16),
        compiler_params=_cparams(("arbitrary",)),
        name="fnet",
    )(uf, jnp.asarray(cos_s), jnp.asarray(sin_s), jnp.asarray(cos_l), jnp.asarray(sin_l), m1, m2)


def _fnet_weights(w_fnet):
    n_g = D_FN // FN_GW
    cos_c, sin_c = _dft_tables(FN_GW)
    eye = np.eye(n_g, dtype=np.float32)
    cos_bd = np.kron(eye, cos_c)
    sin_bd = np.kron(eye, sin_c)
    w_bd = (w_fnet[:, :, None, :] * jnp.asarray(eye)[:, None, :, None]).reshape(D_FN, D_FN)
    out = jax.ShapeDtypeStruct((D_FN, D_FN), BF16)
    return pl.pallas_call(_fnet_w_kernel, out_shape=(out, out), name="fnet_weights")(
        jnp.asarray(cos_bd), jnp.asarray(sin_bd), w_bd)


def _gelu_tanh(x):
    return 0.5 * x * (1.0 + jnp.tanh(math.sqrt(2.0 / math.pi) * (x + 0.044715 * (x * x * x))))


def _post_kernel(y_ref, yf_ref, xp_ref, xs_ref, pos_ref, mod_ref, wglu_ref, wout_ref, g2_ref, wr_ref,
                 br_ref, permt_ref, x1_ref, h2_ref, comb_ref):
    g = pl.program_id(0)
    z = _gelu_tanh(y_ref[0])
    gate = jnp.dot(z.astype(BF16), wglu_ref[...].astype(BF16), preferred_element_type=F32)
    gl = (z * jax.nn.sigmoid(gate)).astype(BF16)
    permt = permt_ref[...]
    n_q = TC // PERM_T
    nat = [jnp.dot(permt, gl[q * SUB * PERM_T:(q + 1) * SUB * PERM_T], preferred_element_type=F32).astype(BF16)
           for q in range(n_q)]
    gl_nat = jnp.concatenate(
        [nat[q][s * PERM_T:(s + 1) * PERM_T] for s in range(SUB) for q in range(n_q)], axis=0)
    w_out = wout_ref[...].astype(BF16)
    mixed = (jnp.dot(gl_nat, w_out[:D_S5], preferred_element_type=F32)
             + jnp.dot(yf_ref[0].reshape(SUB * TC, D_FN), w_out[D_S5:], preferred_element_type=F32))
    x = _load_x(g, xp_ref, xs_ref, pos_ref)
    mod = mod_ref[0]
    gate1 = mod[:, :, 2 * D:3 * D]
    shift2 = mod[:, :, 3 * D:4 * D]
    scale2 = mod[:, :, 4 * D:5 * D]
    x1 = x + gate1 * mixed.reshape(SUB, TC, D)
    x1_ref[0] = x1
    h2 = _rms(x1, g2_ref[...]) * (1.0 + scale2) + shift2
    h2_ref[0] = h2.astype(BF16)

    hr = h2.reshape(SUB * TC, D)
    h_hi = hr.astype(BF16)
    h_lo = (hr - h_hi.astype(F32)).astype(BF16)
    wr = wr_ref[...]
    w_hi = wr.astype(BF16)
    w_lo = (wr - w_hi.astype(F32)).astype(BF16)
    both = jnp.dot(h_hi, jnp.concatenate([w_hi, w_lo], axis=1), preferred_element_type=F32)
    logits = (both[:, :E_PAD] + jnp.dot(h_lo, w_hi, preferred_element_type=F32) + both[:, E_PAD:]
              + br_ref[...])
    lane = lax.broadcasted_iota(jnp.int32, logits.shape, 1).astype(F32)
    top_v, hots = [], []
    cur = logits
    for _ in range(TOP_K):
        m = jnp.max(cur, axis=-1, keepdims=True)
        idx = jnp.min(jnp.where(cur == m, lane, float(E_PAD)), axis=-1, keepdims=True)
        hot = lane == idx
        top_v.append(m)
        hots.append(hot)
        cur = jnp.where(hot, -3.0e38, cur)
    exps = [jnp.exp(v - top_v[0]) for v in top_v]
    denom = exps[0] + exps[1] + exps[2] + exps[3]
    comb = jnp.zeros(logits.shape, F32)
    for k in range(TOP_K):
        comb = comb + jnp.where(hots[k], exps[k] / denom, 0.0)
    comb_ref[0] = comb.reshape(SUB, TC, E_PAD)


def _post(y, yf, xp4, xs4, pos3, modg, w_glu, w_out, norm2_g, w_router, b_router, permt):
    xp_spec, xs_spec, pos_spec = _x_specs()
    const2 = lambda g, tc: (0, 0)
    wr = jnp.zeros((D, E_PAD), F32).at[:, :N_EXPERTS].set(w_router)
    br = jnp.full((1, E_PAD), -1.0e30, F32).at[0, :N_EXPERTS].set(b_router)
    blk = lambda w: pl.BlockSpec((1, SUB, TC, w), lambda g, tc: (g, 0, tc, 0))
    return pl.pallas_call(
        _post_kernel,
        grid=(N_GRP, N_TC),
        in_specs=[pl.BlockSpec((1, SUB * TC, D_S5), lambda g, tc: (g, tc, 0)),
                  blk(D_FN), xp_spec, xs_spec, pos_spec,
                  pl.BlockSpec((1, SUB, 1, 6 * D), lambda g, tc: (g, 0, 0, 0)),
                  pl.BlockSpec((D_S5, D_S5), const2),
                  pl.BlockSpec((D, D), const2),
                  pl.BlockSpec((1, D), const2),
                  pl.BlockSpec((D, E_PAD), const2),
                  pl.BlockSpec((1, E_PAD), const2),
                  pl.BlockSpec((SUB * PERM_T, SUB * PERM_T), const2)],
        out_specs=[blk(D), blk(D), blk(E_PAD)],
        out_shape=[jax.ShapeDtypeStruct((N_GRP, SUB, L_BLK, D), F32),
                   jax.ShapeDtypeStruct((N_GRP, SUB, L_BLK, D), BF16),
                   jax.ShapeDtypeStruct((N_GRP, SUB, L_BLK, E_PAD), F32)],
        compiler_params=_cparams(("arbitrary", "arbitrary")),
        name="post_mixer",
    )(y, yf, xp4, xs4, pos3, modg, w_glu, w_out, norm2_g.reshape(1, D), wr, br, permt)


TBLK = L_BLK
N_BLK = T_TOK // TBLK
N_CTX_BLK = (N_GRP - 1) * SUB
SEG_ALIGN = 8
RB = 1280
assert RB >= TBLK * TOP_K + N_EXPERTS * (SEG_ALIGN - 1) and RB % 128 == 0
TM = 256
TM_SHIFT = 8
assert 1 << TM_SHIFT == TM
R_TOT = T_TOK * TOP_K + N_BLK * N_EXPERTS * (SEG_ALIGN - 1) + TM
TAB_ROWS = 32
assert TAB_ROWS >= N_BLK
BIG_ROWS = 32
BIG_SHIFT = 2
assert SEG_ALIGN << BIG_SHIFT == BIG_ROWS
SMALL_COPY_PRIORITY = 1


def _plan_kernel(comb_ref, z_ref, ptab_ref, loff_ref, loff_i_ref, gst_i_ref, p8_i_ref, eoff_i_ref):
    row = lax.broadcasted_iota(jnp.int32, (TBLK, TBLK), 0)
    col = lax.broadcasted_iota(jnp.int32, (TBLK, TBLK), 1)
    earlier = jnp.where(row > col, 1.0, 0.0).astype(BF16)
    ptab_ref[...] = jnp.zeros_like(ptab_ref)

    def body(b, _):
        rows = pl.ds(pl.multiple_of(b * TBLK, TBLK), TBLK)
        m = jnp.where(comb_ref[rows, :] > 0.0, 1.0, 0.0)
        rank = jnp.dot(earlier, m.astype(BF16), preferred_element_type=F32)
        z_ref[rows, :] = m * (rank + 1.0)
        n = jnp.sum(m, axis=0, keepdims=True)
        ptab_ref[pl.ds(b, 1), :] = jnp.floor((n + (SEG_ALIGN - 1)) * (1.0 / SEG_ALIGN)) * SEG_ALIGN
        return 0

    lax.fori_loop(0, N_BLK, body, 0)

    ptab = ptab_ref[...]
    er = lax.broadcasted_iota(jnp.int32, (E_PAD, E_PAD), 0)
    ec = lax.broadcasted_iota(jnp.int32, (E_PAD, E_PAD), 1)
    before = jnp.where(er < ec, 1.0, 0.0)
    exact = functools.partial(jnp.dot, preferred_element_type=F32, precision=HIGHEST)
    loff = exact(ptab, before)
    loff_ref[...] = loff
    as_int = lambda t: t[:, :N_EXPERTS].astype(jnp.int32)
    loff_i_ref[...] = as_int(loff)
    p8_i_ref[...] = as_int(ptab * (1.0 / SEG_ALIGN))
    tot = jnp.sum(ptab, axis=0, keepdims=True)
    eoff = exact(jnp.broadcast_to(tot, (SUB, E_PAD)), before)
    eoff_i_ref[...] = eoff.astype(jnp.int32)
    br = lax.broadcasted_iota(jnp.int32, (TAB_ROWS, TAB_ROWS), 0)
    bc = lax.broadcasted_iota(jnp.int32, (TAB_ROWS, TAB_ROWS), 1)
    gst_i_ref[...] = as_int(eoff[0:1] + exact(jnp.where(br > bc, 1.0, 0.0), ptab))


def _plan(comb):
    tab = jax.ShapeDtypeStruct((TAB_ROWS, E_PAD), F32)
    itab = jax.ShapeDtypeStruct((TAB_ROWS, N_EXPERTS), jnp.int32)
    return pl.pallas_call(
        _plan_kernel,
        out_shape=(jax.ShapeDtypeStruct((T_TOK, E_PAD), F32), tab, tab, itab, itab, itab,
                   jax.ShapeDtypeStruct((SUB, E_PAD), jnp.int32)),
        compiler_params=pltpu.CompilerParams(vmem_limit_bytes=VMEM_LIMIT),
        name="moe_plan",
    )(comb)


def _sort_matrix(b, z_ref, loff_ref, ptab_ref):
    loff = loff_ref[pl.ds(b, 1), :]
    size = ptab_ref[pl.ds(b, 1), :]
    r = lax.broadcasted_iota(jnp.int32, (RB, E_PAD), 0).astype(F32)
    owner = jnp.where(r >= loff, jnp.where(r < loff + size, 1.0, 0.0), 0.0)
    rank1 = r[:, 0:1] - jnp.sum(owner * loff, axis=-1, keepdims=True) + 1.0
    zt = z_ref[...].T
    v = jnp.dot(owner.astype(BF16), zt.astype(BF16), preferred_element_type=F32)
    return jnp.where(v == rank1, 1.0, 0.0), owner


def _segment_copies(b, loff_s, gst_s, p8_s, make_copy):
    def per_expert(e, counts):
        chunks = p8_s[b, e]
        n_big = lax.shift_right_logical(chunks, BIG_SHIFT)
        n_small = chunks - n_big * (BIG_ROWS // SEG_ALIGN)
        local0 = loff_s[b, e]
        global0 = gst_s[b, e]

        def big(j, _):
            make_copy(pl.multiple_of(local0 + j * BIG_ROWS, SEG_ALIGN),
                      pl.multiple_of(global0 + j * BIG_ROWS, SEG_ALIGN), BIG_ROWS).start()
            return 0

        def small(j, _):
            off = n_big * BIG_ROWS + j * SEG_ALIGN
            make_copy(pl.multiple_of(local0 + off, SEG_ALIGN),
                      pl.multiple_of(global0 + off, SEG_ALIGN), SEG_ALIGN).start(priority=SMALL_COPY_PRIORITY)
            return 0

        lax.fori_loop(0, n_big, big, 0)
        lax.fori_loop(0, n_small, small, 0)
        return counts[0] + n_big, counts[1] + n_small

    return lax.fori_loop(0, N_EXPERTS, per_expert, (0, 0))


def _wait_copies(counts, make_copy):
    for n, rows in zip(counts, (BIG_ROWS, SEG_ALIGN)):
        def wait_one(i, _, rows=rows):
            make_copy(0, 0, rows).wait()
            return 0
        lax.fori_loop(0, n, wait_one, 0)


def _zero_rows_from(hbm, zeros_vmem, first_row, sem):
    def copy_to(row):
        return pltpu.make_async_copy(zeros_vmem, hbm.at[pl.ds(row, TM)], sem)

    n_full = lax.shift_right_logical(R_TOT - first_row, TM_SHIFT)

    def start_one(j, _):
        copy_to(pl.multiple_of(first_row + j * TM, SEG_ALIGN)).start()
        return 0

    def wait_one(j, _):
        copy_to(0).wait()
        return 0

    lax.fori_loop(0, n_full, start_one, 0)
    lax.fori_loop(0, n_full, wait_one, 0)
    last = copy_to(R_TOT - TM)
    last.start()
    last.wait()


def _dispatch_kernel(loff_s, gst_s, p8_s, eoff_s, h_ref, z_ref, loff_ref, ptab_ref, xs_hbm, xbuf, sem, pending):
    b = pl.program_id(0)
    slot = lax.rem(b, 2)

    def copies_of(s):
        def make_copy(local_row, global_row, rows):
            return pltpu.make_async_copy(xbuf.at[s, pl.ds(local_row, rows)], xs_hbm.at[pl.ds(global_row, rows)],
                                         sem.at[s])
        return make_copy

    pm = _sort_matrix(b, z_ref, loff_ref, ptab_ref)[0].astype(BF16)

    @pl.when(b >= 2)
    def _():
        _wait_copies((pending[2 * slot], pending[2 * slot + 1]), copies_of(slot))

    xbuf[slot] = jnp.dot(pm, h_ref[...], preferred_element_type=F32)
    n_big, n_small = _segment_copies(b, loff_s, gst_s, p8_s, copies_of(slot))
    pending[2 * slot] = n_big
    pending[2 * slot + 1] = n_small

    @pl.when(b == N_BLK - 1)
    def _():
        _wait_copies((n_big, n_small), copies_of(slot))
        _wait_copies((pending[2 * (1 - slot)], pending[2 * (1 - slot) + 1]), copies_of(1 - slot))
        xbuf[0, 0:TM, :] = jnp.zeros((TM, D), F32)
        _zero_rows_from(xs_hbm, xbuf.at[0, 0:TM], eoff_s[0, N_EXPERTS], sem.at[0])


def _dispatch(h2, z, loff, ptab, loff_i, gst_i, p8_i, eoff_i):
    whole = pl.BlockSpec((TAB_ROWS, E_PAD), lambda b, *_: (0, 0))
    return pl.pallas_call(
        _dispatch_kernel,
        grid_spec=pltpu.PrefetchScalarGridSpec(
            num_scalar_prefetch=4,
            grid=(N_BLK,),
            in_specs=[pl.BlockSpec((TBLK, D), lambda b, *_: (b, 0)),
                      pl.BlockSpec((TBLK, E_PAD), lambda b, *_: (b, 0)),
                      whole, whole],
            out_specs=pl.BlockSpec(memory_space=pl.ANY),
            scratch_shapes=[pltpu.VMEM((2, RB, D), F32), pltpu.SemaphoreType.DMA((2,)),
                            pltpu.SMEM((4,), jnp.int32)]),
        out_shape=jax.ShapeDtypeStruct((R_TOT, D), F32),
        compiler_params=_cparams(("arbitrary",)),
        name="moe_dispatch",
    )(loff_i, gst_i, p8_i, eoff_i, h2, z, loff, ptab)


MAX_TILES = R_TOT // TM + N_EXPERTS
X_AHEAD = 3
X_SLOTS = X_AHEAD + 1
Y_SLOTS = 2
W_AHEAD = 2
W_SLOTS = W_AHEAD + 1
N_MATS = 3
FF_CHUNK = 512
CAST_ROWS = 64
TILE_STEP = 64


def _expert_kernel(eoff_s, xs_hbm, wg_hbm, wu_hbm, wd_hbm, bg_ref, bu_ref, bd_ref, ys_hbm,
                   wst, wbf, xin, yout, act_ref, w_sem, x_sem, y_sem, t_exp, t_row, t_first, t_valid, live):
    def add_expert(e, carry):
        n_t, n_live = carry
        start = eoff_s[0, e]
        count = eoff_s[0, e + 1] - start
        tiles = lax.shift_right_logical(count + (TM - 1), TM_SHIFT)

        def add_tile(i, _):
            t_exp[n_t + i] = e
            t_row[n_t + i] = start + i * TM
            t_first[n_t + i] = jnp.where(i == 0, 1, 0)
            t_valid[n_t + i] = jnp.minimum(count - i * TM, TM)
            return 0

        lax.fori_loop(0, tiles, add_tile, 0)
        has_rows = jnp.where(tiles > 0, 1, 0)

        @pl.when(tiles > 0)
        def _():
            live[n_live] = e

        return n_t + tiles, n_live + has_rows

    n_tiles, n_live = lax.fori_loop(0, N_EXPERTS, add_expert, (0, 0))

    def w_copies(e, slot):
        return [pltpu.make_async_copy(w.at[e], wst.at[slot, m], w_sem.at[slot])
                for m, w in enumerate((wg_hbm, wu_hbm, wd_hbm))]

    def tile_rows(j):
        return pl.ds(pl.multiple_of(t_row[j], SEG_ALIGN), TM)

    def x_copy(j, slot):
        return pltpu.make_async_copy(xs_hbm.at[tile_rows(j)], xin.at[slot], x_sem.at[slot])

    def y_pieces(j, op):
        ys = lax.rem(j, Y_SLOTS)
        valid = t_valid[j]
        row0 = t_row[j]

        def piece(off, rows):
            return pltpu.make_async_copy(yout.at[ys, pl.ds(off, rows)],
                                         ys_hbm.at[pl.ds(pl.multiple_of(row0 + off, SEG_ALIGN), rows)],
                                         y_sem.at[ys])

        @pl.when(valid == TM)
        def _():
            op(piece(0, TM))

        @pl.when(valid < TM)
        def _():
            n_big = lax.shift_right_logical(valid, BIG_ROWS.bit_length() - 1)
            n_small = lax.shift_right_logical(valid - n_big * BIG_ROWS, SEG_ALIGN.bit_length() - 1)

            def big(i, _):
                op(piece(pl.multiple_of(i * BIG_ROWS, SEG_ALIGN), BIG_ROWS))
                return 0

            def small(i, _):
                op(piece(pl.multiple_of(n_big * BIG_ROWS + i * SEG_ALIGN, SEG_ALIGN), SEG_ALIGN))
                return 0

            lax.fori_loop(0, n_big, big, 0)
            lax.fori_loop(0, n_small, small, 0)

    for ahead in range(W_AHEAD):
        @pl.when(n_live > ahead)
        def _(ahead=ahead):
            for cp in w_copies(live[ahead], ahead):
                cp.start()

    for ahead in range(X_AHEAD):
        @pl.when(n_tiles > ahead)
        def _(ahead=ahead):
            x_copy(ahead, ahead).start()

    def body(j, k):
        e = t_exp[j]

        @pl.when(t_first[j] == 1)
        def _():
            ws = lax.rem(k, W_SLOTS)

            @pl.when(k + W_AHEAD < n_live)
            def _():
                for cp in w_copies(live[k + W_AHEAD], lax.rem(k + W_AHEAD, W_SLOTS)):
                    cp.start()

            for cp in w_copies(e, ws):
                cp.wait()

            def cast_rows(r, _):
                rows = pl.ds(pl.multiple_of(r * CAST_ROWS, CAST_ROWS), CAST_ROWS)
                for m in range(N_MATS):
                    wbf[m, rows, :] = wst[ws, m, rows, :].astype(BF16)
                return 0

            lax.fori_loop(0, D // CAST_ROWS, cast_rows, 0)

        slot = lax.rem(j, X_SLOTS)
        x_copy(j, slot).wait()

        @pl.when(j + X_AHEAD < n_tiles)
        def _():
            x_copy(j + X_AHEAD, lax.rem(j + X_AHEAD, X_SLOTS)).start()

        @pl.when(j >= Y_SLOTS)
        def _():
            y_pieces(j - Y_SLOTS, lambda cp: cp.wait())

        ys = lax.rem(j, Y_SLOTS)

        def compute(rows):
            x = xin[slot, 0:rows, :].astype(BF16)
            for c in range(D // FF_CHUNK):
                cols = slice(c * FF_CHUNK, (c + 1) * FF_CHUNK)
                gate = jnp.dot(x, wbf[0, :, cols], preferred_element_type=F32) + bg_ref[e][:, cols]
                up = jnp.dot(x, wbf[1, :, cols], preferred_element_type=F32) + bu_ref[e][:, cols]
                gate = jnp.minimum(gate, SWIGLU_LIMIT)
                up = jnp.clip(up, -SWIGLU_LIMIT, SWIGLU_LIMIT)
                act_ref[0:rows, cols] = ((up + 1.0) * gate * jax.nn.sigmoid(SWIGLU_ALPHA * gate)).astype(BF16)
            yout[ys, 0:rows, :] = jnp.dot(act_ref[0:rows, :], wbf[2], preferred_element_type=F32) + bd_ref[e]

        valid = t_valid[j]
        for rows in range(TILE_STEP, TM + 1, TILE_STEP):
            @pl.when(jnp.logical_and(valid > rows - TILE_STEP, valid <= rows))
            def _(rows=rows):
                compute(rows)

        y_pieces(j, lambda cp: cp.start())
        return k + t_first[j]

    lax.fori_loop(0, n_tiles, body, 0)

    for back in range(Y_SLOTS, 0, -1):
        @pl.when(n_tiles >= back)
        def _(back=back):
            y_pieces(n_tiles - back, lambda cp: cp.wait())

    yout[0] = jnp.zeros((TM, D), F32)
    _zero_rows_from(ys_hbm, yout.at[0], eoff_s[0, N_EXPERTS], y_sem.at[0])


def _experts(xs, eoff_i, w_gate, b_gate, w_up, b_up, w_down, b_down):
    hbm = pl.BlockSpec(memory_space=pl.ANY)
    bspec = pl.BlockSpec((N_EXPERTS, 1, D), lambda i, *_: (0, 0, 0))
    return pl.pallas_call(
        _expert_kernel,
        grid_spec=pltpu.PrefetchScalarGridSpec(
            num_scalar_prefetch=1,
            grid=(1,),
            in_specs=[hbm, hbm, hbm, hbm, bspec, bspec, bspec],
            out_specs=hbm,
            scratch_shapes=[pltpu.VMEM((W_SLOTS, N_MATS, D, D), F32), pltpu.VMEM((N_MATS, D, D), BF16),
                            pltpu.VMEM((X_SLOTS, TM, D), F32), pltpu.VMEM((Y_SLOTS, TM, D), F32),
                            pltpu.VMEM((TM, D), BF16),
                            pltpu.SemaphoreType.DMA((W_SLOTS,)), pltpu.SemaphoreType.DMA((X_SLOTS,)),
                            pltpu.SemaphoreType.DMA((Y_SLOTS,)),
                            pltpu.SMEM((MAX_TILES,), jnp.int32), pltpu.SMEM((MAX_TILES,), jnp.int32),
                            pltpu.SMEM((MAX_TILES,), jnp.int32), pltpu.SMEM((MAX_TILES,), jnp.int32),
                            pltpu.SMEM((N_EXPERTS,), jnp.int32)]),
        out_shape=jax.ShapeDtypeStruct((R_TOT, D), F32),
        compiler_params=_cparams(("arbitrary",)),
        name="moe_experts",
    )(eoff_i, xs, w_gate, w_up, w_down, b_gate.reshape(N_EXPERTS, 1, D), b_up.reshape(N_EXPERTS, 1, D),
      b_down.reshape(N_EXPERTS, 1, D))


def _combine_kernel(loff_s, gst_s, p8_s, ys_hbm, z_ref, comb_ref, loff_ref, ptab_ref, x1_ref, mod_ref, gf_ref,
                    yctx_ref, ylat_ref, ybuf, sem):
    b = pl.program_id(0)

    def make_copy(local_row, global_row, rows):
        return pltpu.make_async_copy(ys_hbm.at[pl.ds(global_row, rows)], ybuf.at[pl.ds(local_row, rows)], sem)

    started = _segment_copies(b, loff_s, gst_s, p8_s, make_copy)

    used_chunks = lax.shift_right_logical(loff_s[b, N_EXPERTS - 1], 3) + p8_s[b, N_EXPERTS - 1]

    def zero_chunk(j, _):
        ybuf[pl.ds(pl.multiple_of(j * SEG_ALIGN, SEG_ALIGN), SEG_ALIGN), :] = jnp.zeros((SEG_ALIGN, D), F32)
        return 0

    lax.fori_loop(used_chunks, RB // SEG_ALIGN, zero_chunk, 0)

    pm, owner = _sort_matrix(b, z_ref, loff_ref, ptab_ref)
    comb = comb_ref[...]
    c_hi = comb.astype(BF16)
    rest = comb - c_hi.astype(F32)
    c_mid = rest.astype(BF16)
    c_lo = (rest - c_mid.astype(F32)).astype(BF16)
    pmb = pm.astype(BF16)
    moved = (jnp.dot(pmb, c_hi, preferred_element_type=F32) + jnp.dot(pmb, c_mid, preferred_element_type=F32)
             + jnp.dot(pmb, c_lo, preferred_element_type=F32))
    w_row = jnp.sum(owner * moved, axis=-1, keepdims=True)
    pmt = pm.T.astype(BF16)
    _wait_copies(started, make_copy)
    y = (ybuf[...] * w_row).astype(BF16)
    moe = jnp.dot(pmt, y, preferred_element_type=F32)
    gate2 = mod_ref[0][:, 5 * D:6 * D]
    x2 = x1_ref[0] + gate2 * moe
    y = _rms(x2, gf_ref[...])

    @pl.when(b < N_CTX_BLK)
    def _():
        yctx_ref[0] = y

    @pl.when(b >= N_CTX_BLK)
    def _():
        ylat_ref[0] = y


def _combine(ys, z, comb, loff, ptab, x1, modv, norm_f_g, loff_i, gst_i, p8_i):
    whole = pl.BlockSpec((TAB_ROWS, E_PAD), lambda b, *_: (0, 0))
    tok = pl.BlockSpec((TBLK, E_PAD), lambda b, *_: (b, 0))
    return pl.pallas_call(
        _combine_kernel,
        grid_spec=pltpu.PrefetchScalarGridSpec(
            num_scalar_prefetch=3,
            grid=(N_BLK,),
            in_specs=[pl.BlockSpec(memory_space=pl.ANY), tok, tok, whole, whole,
                      pl.BlockSpec((1, TBLK, D), lambda b, *_: (b, 0, 0)),
                      pl.BlockSpec((1, 1, 6 * D), lambda b, *_: (b, 0, 0)),
                      pl.BlockSpec((1, D), lambda b, *_: (0, 0))],
            out_specs=[pl.BlockSpec((1, TBLK, D), lambda b, *_: (jnp.minimum(b, N_CTX_BLK - 1), 0, 0)),
                       pl.BlockSpec((1, TBLK, D), lambda b, *_: (jnp.maximum(b - N_CTX_BLK, 0), 0, 0))],
            scratch_shapes=[pltpu.VMEM((RB, D), F32), pltpu.SemaphoreType.DMA(())]),
        out_shape=[jax.ShapeDtypeStruct((N_CTX_BLK, TBLK, D), F32),
                   jax.ShapeDtypeStruct((N_BLK - N_CTX_BLK, TBLK, D), F32)],
        compiler_params=_cparams(("arbitrary",)),
        name="moe_combine",
    )(loff_i, gst_i, p8_i, ys, z, comb, loff, ptab, x1, modv, norm_f_g.reshape(1, D))


def _moe_and_final(x1, h2, comb, modg, norm_f_g, w_gate, b_gate, w_up, b_up, w_down, b_down):
    z, ptab, loff, loff_i, gst_i, p8_i, eoff_i = _plan(comb)
    xs = _dispatch(h2, z, loff, ptab, loff_i, gst_i, p8_i, eoff_i)
    ys = _experts(xs, eoff_i, w_gate, b_gate, w_up, b_up, w_down, b_down)
    return _combine(ys, z, comb, loff, ptab, x1, modg.reshape(N_BLK, 1, 6 * D), norm_f_g,
                    loff_i, gst_i, p8_i)


def _grid_pos_embed(n_tokens, dim):
    rows = n_tokens // GRID_W
    t = np.arange(rows * GRID_W)
    r = (t // GRID_W).astype(np.float32)
    col = (t % GRID_W).astype(np.float32)
    q = dim // 4
    omega = (1.0 / np.float32(POS_TEMP) ** (np.arange(q, dtype=np.float32) / np.float32(q))).astype(np.float32)

    def emb(p):
        a = p[:, None] * omega[None, :]
        return np.concatenate([np.sin(a), np.cos(a)], axis=-1)

    return np.concatenate([emb(r), emb(col)], axis=-1).astype(np.float32)


def kernel(x_prompt, x_sample, c, state_s5_re, state_s5_im, c_ctx, w_ada, b_ada, norm1_g, w_in, s5_lam_re,
           s5_lam_im, s5_log_dt, s5_b_re, s5_b_im, s5_c_re, s5_c_im, s5_d, s5_w_glu, w_fnet, w_out, norm2_g,
           w_router, b_router, w_gate, b_gate, w_up, b_up, w_down, b_down, norm_f_g):
    n_ctx, n_lat = x_prompt.shape[0], x_sample.shape[0]
    assert x_prompt.shape == (SUB * (N_GRP - 1), L_BLK, D) and x_sample.shape == (2, LAT_CHUNKS * L_BLK, D)
    assert w_ada.shape[0] == 1, "one trunk layer"
    layer = 0

    cvec = jnp.concatenate([jnp.broadcast_to(c_ctx[None], (n_ctx, D)), jnp.repeat(c, LAT_CHUNKS, axis=0)], axis=0)
    modg = _adaln(cvec, w_ada[layer], b_ada[layer]).reshape(N_GRP, SUB, 1, 6 * D)

    xp4 = x_prompt.reshape(N_GRP - 1, SUB, L_BLK, D)
    xs4 = x_sample.reshape(1, SUB, L_BLK, D)
    pos3 = jnp.asarray(_grid_pos_embed(LAT_CHUNKS * L_BLK, D).reshape(LAT_CHUNKS, L_BLK, D))
    perm = _perm_matrix()

    us, uf = _pre(xp4, xs4, pos3, modg, norm1_g[layer], w_in[layer], jnp.asarray(perm, BF16))

    bbmat, ccmat, a8 = _s5_params(s5_lam_re[layer], s5_lam_im[layer], s5_log_dt[layer], s5_b_re[layer],
                                  s5_b_im[layer], s5_c_re[layer], s5_c_im[layer])

    st = jnp.stack([state_s5_re[:, layer], state_s5_im[:, layer]], axis=2)
    st = st.reshape(n_lat, 2, 2, N_S5_CHUNKS, CH_P)
    st = jnp.transpose(st, (1, 3, 0, 2, 4)).reshape(2, N_S5_CHUNKS, n_lat, 2 * CH_P)
    init8 = jnp.zeros((2, N_S5_CHUNKS, n_lat, LAT_CHUNKS, 2 * CH_P), F32)
    init8 = init8.at[0, :, :, 0].set(st[0]).at[1, :, :, LAT_CHUNKS - 1].set(st[1])
    init8 = init8.reshape(2, N_S5_CHUNKS, SUB, 2 * CH_P)

    y_s5, fin = _s5(us, bbmat, ccmat, a8, init8, s5_d[layer])

    m1, m2 = _fnet_weights(w_fnet[layer])
    yf = _fnet(uf, m1, m2)

    x1, h2, comb = _post(y_s5, yf, xp4, xs4, pos3, modg, s5_w_glu[layer], w_out[layer], norm2_g[layer],
                         w_router[layer], b_router[layer], jnp.asarray(perm.T, BF16))

    y_prompt, y_lat = _moe_and_final(x1.reshape(N_BLK, TBLK, D), h2.reshape(T_TOK, D), comb.reshape(T_TOK, E_PAD),
                                     modg, norm_f_g, w_gate[layer], b_gate[layer], w_up[layer], b_up[layer],
                                     w_down[layer], b_down[layer])
    y_sample = y_lat.reshape(n_lat, LAT_CHUNKS * L_BLK, D)

    fin = fin[:, :N_GRP - 1].reshape(2, N_GRP - 1, SUB, N_S5_CHUNKS, 2, CH_P)
    fin = jnp.transpose(fin, (4, 1, 2, 0, 3, 5)).reshape(2, n_ctx, 1, 2, N_S5_GROUPS, S5_P)
    return (y_prompt, y_sample, fin[0], fin[1])
```

```python
import functools
import math

import numpy as np
import jax
import jax.numpy as jnp
from jax import lax
from jax.experimental import pallas as pl
from jax.experimental.pallas import tpu as pltpu

F32 = jnp.float32
BF16 = jnp.bfloat16
HIGHEST = lax.Precision.HIGHEST

D = 1024
D_S5 = 768
S5_H = 16
S5_P = 64
N_S5_GROUPS = 48
D_FN = 256
FN_GW = 64
N_EXPERTS = 32
TOP_K = 4
E_PAD = 128
SWIGLU_LIMIT = 7.0
SWIGLU_ALPHA = 1.702
RMS_EPS = 1e-6
POS_TEMP = 10000.0
GRID_W = 64

L_BLK = 256
SUB = 8
N_GRP = 3
GRP_ROWS = L_BLK * SUB
LAT_CHUNKS = 4
T_TOK = N_GRP * GRP_ROWS
TC = 128
N_TC = L_BLK // TC
PERM_T = 32
S5_CHUNK = 16
N_S5_CHUNKS = N_S5_GROUPS // S5_CHUNK
CH_U = S5_CHUNK * S5_H
CH_P = S5_CHUNK * S5_P
V7X_VMEM_BYTES = 64 * 1024 * 1024
VMEM_LIMIT = V7X_VMEM_BYTES * 7 // 8


def _cparams(sem):
    return pltpu.CompilerParams(dimension_semantics=sem, vmem_limit_bytes=VMEM_LIMIT)


ADA_K = 128


def _adaln_kernel(c_ref, w_ref, b_ref, o_ref):
    @pl.when(pl.program_id(0) == 0)
    def _():
        o_ref[...] = jnp.broadcast_to(b_ref[...], o_ref.shape)

    c = c_ref[...]
    s = c * jax.nn.sigmoid(c)
    s_hi = s.astype(BF16)
    s_lo = (s - s_hi.astype(F32)).astype(BF16)
    w = w_ref[...]
    w_hi = w.astype(BF16)
    w_lo = (w - w_hi.astype(F32)).astype(BF16)
    o_ref[...] += (jnp.dot(s_hi, w_hi, preferred_element_type=F32) + jnp.dot(s_lo, w_hi, preferred_element_type=F32)
                   + jnp.dot(s_hi, w_lo, preferred_element_type=F32))


def _adaln(cvec, w_ada, b_ada):
    k_dim, n = w_ada.shape
    rows = cvec.shape[0]
    return pl.pallas_call(
        _adaln_kernel,
        grid=(k_dim // ADA_K,),
        in_specs=[pl.BlockSpec((rows, ADA_K), lambda k: (0, k)),
                  pl.BlockSpec((ADA_K, n), lambda k: (k, 0)),
                  pl.BlockSpec((1, n), lambda k: (0, 0))],
        out_specs=pl.BlockSpec((rows, n), lambda k: (0, 0)),
        out_shape=jax.ShapeDtypeStruct((rows, n), F32),
        compiler_params=_cparams(("arbitrary",)),
        name="adaln",
    )(cvec, w_ada, b_ada.reshape(1, n))


def _zoh(lam_re, lam_im, log_dt):
    dt = jnp.exp(log_dt)
    a_re = jnp.minimum(lam_re, -1e-4)
    a_im = lam_im
    mag = jnp.exp(a_re * dt)
    ab_re = mag * jnp.cos(a_im * dt)
    ab_im = mag * jnp.sin(a_im * dt)
    den = a_re * a_re + a_im * a_im
    nr = ab_re - 1.0
    f_re = (nr * a_re + ab_im * a_im) / den
    f_im = (ab_im * a_re - nr * a_im) / den
    return ab_re, ab_im, f_re, f_im


def _s5_params_kernel(lre3_ref, lim3_ref, ldt3_ref, lre2_ref, lim2_ref, ldt2_ref, bre_ref, bim_ref,
                      cre_ref, cim_ref, bb_ref, cc_ref, a_ref):
    exact = functools.partial(jnp.dot, preferred_element_type=F32, precision=HIGHEST)
    spread = jnp.where(lax.broadcasted_iota(jnp.int32, (S5_P, CH_P), 1) % S5_P
                       == lax.broadcasted_iota(jnp.int32, (S5_P, CH_P), 0), 1.0, 0.0)

    def block_diag(m):
        wide = jnp.dot(m.reshape(CH_U, S5_P).astype(BF16), spread.astype(BF16), preferred_element_type=F32)
        row_g = lax.broadcasted_iota(jnp.int32, (CH_U, CH_P), 0) // S5_H
        col_g = lax.broadcasted_iota(jnp.int32, (CH_U, CH_P), 1) // S5_P
        return jnp.where(row_g == col_g, wide, 0.0)

    _, _, f_re, f_im = _zoh(lre3_ref[0, 0], lim3_ref[0, 0], ldt3_ref[0, 0])
    b_re = bre_ref[0, 0]
    b_im = bim_ref[0, 0]
    bb_ref[0, 0] = jnp.concatenate([block_diag(f_re * b_re - f_im * b_im),
                                    block_diag(f_re * b_im + f_im * b_re)], axis=1).astype(BF16)
    cc_ref[0, 0] = jnp.concatenate([block_diag(cre_ref[0, 0]).T,
                                    -block_diag(cim_ref[0, 0]).T], axis=0).astype(BF16)

    ab_re, ab_im, _, _ = _zoh(lre2_ref[0, 0], lim2_ref[0, 0], ldt2_ref[0, 0])
    row_g = lax.broadcasted_iota(jnp.int32, (S5_CHUNK, CH_P), 0)
    col_g = lax.broadcasted_iota(jnp.int32, (S5_CHUNK, CH_P), 1) // S5_P

    def lane_row(a):
        flat = jnp.sum(jnp.where(row_g == col_g, exact(a, spread), 0.0), axis=0, keepdims=True)
        return jnp.broadcast_to(flat, (SUB, CH_P))

    a_ref[0, 0] = jnp.concatenate([lane_row(ab_re), lane_row(ab_im)], axis=1)


def _s5_params(lam_re, lam_im, log_dt, b_re, b_im, c_re, c_im):
    nc = N_S5_CHUNKS
    ldt = jnp.broadcast_to(log_dt[:, :, None], (2, N_S5_GROUPS, S5_P))
    g3 = lambda a: a.reshape(2, nc, S5_CHUNK, 1, S5_P)
    g2 = lambda a: a.reshape(2, nc, S5_CHUNK, S5_P)
    ghp = lambda a: a.reshape(2, nc, S5_CHUNK, S5_H, S5_P)
    spec3 = pl.BlockSpec((1, 1, S5_CHUNK, 1, S5_P), lambda d, c: (d, c, 0, 0, 0))
    spec2 = pl.BlockSpec((1, 1, S5_CHUNK, S5_P), lambda d, c: (d, c, 0, 0))
    spec_ghp = pl.BlockSpec((1, 1, S5_CHUNK, S5_H, S5_P), lambda d, c: (d, c, 0, 0, 0))
    return pl.pallas_call(
        _s5_params_kernel,
        grid=(2, nc),
        in_specs=[spec3, spec3, spec3, spec2, spec2, spec2, spec_ghp, spec_ghp, spec_ghp, spec_ghp],
        out_specs=[pl.BlockSpec((1, 1, CH_U, 2 * CH_P), lambda d, c: (d, c, 0, 0)),
                   pl.BlockSpec((1, 1, 2 * CH_P, CH_U), lambda d, c: (d, c, 0, 0)),
                   pl.BlockSpec((1, 1, SUB, 2 * CH_P), lambda d, c: (d, c, 0, 0))],
        out_shape=[jax.ShapeDtypeStruct((2, nc, CH_U, 2 * CH_P), BF16),
                   jax.ShapeDtypeStruct((2, nc, 2 * CH_P, CH_U), BF16),
                   jax.ShapeDtypeStruct((2, nc, SUB, 2 * CH_P), F32)],
        compiler_params=_cparams(("arbitrary", "arbitrary")),
        name="s5_params",
    )(g3(lam_re), g3(lam_im), g3(ldt), g2(lam_re), g2(lam_im), g2(ldt),
      ghp(jnp.swapaxes(b_re, -1, -2)), ghp(jnp.swapaxes(b_im, -1, -2)), ghp(c_re), ghp(c_im))


def _rms(x, g):
    return x * lax.rsqrt(jnp.mean(x * x, axis=-1, keepdims=True) + RMS_EPS) * g


def _load_x(g, xp_ref, xs_ref, pos_ref):
    is_lat = g == N_GRP - 1
    pos = pos_ref[...]
    pos8 = jnp.concatenate([pos, pos], axis=0)
    return jnp.where(is_lat, xs_ref[0] + pos8, xp_ref[0])


def _x_specs():
    n_ctx = N_GRP - 1
    xp_spec = pl.BlockSpec(
        (1, SUB, TC, D),
        lambda g, tc: (jnp.minimum(g, n_ctx - 1), 0, jnp.where(g >= n_ctx, N_TC - 1, tc), 0))
    xs_spec = pl.BlockSpec((1, SUB, TC, D), lambda g, tc: (0, 0, jnp.where(g >= n_ctx, tc, 0), 0))
    pos_spec = pl.BlockSpec((LAT_CHUNKS, TC, D), lambda g, tc: (0, tc, 0))
    return xp_spec, xs_spec, pos_spec


def _perm_matrix():
    p = np.zeros((SUB * PERM_T, SUB * PERM_T), np.float32)
    for s in range(SUB):
        for j in range(PERM_T):
            p[j * SUB + s, s * PERM_T + j] = 1.0
    return p


def _pre_kernel(xp_ref, xs_ref, pos_ref, mod_ref, g1_ref, win_ref, perm_ref, us_ref, uf_ref):
    g = pl.program_id(0)
    x = _load_x(g, xp_ref, xs_ref, pos_ref)
    mod = mod_ref[0]
    shift1 = mod[:, :, 0:D]
    scale1 = mod[:, :, D:2 * D]
    h = _rms(x, g1_ref[...]) * (1.0 + scale1) + shift1
    h2d = h.reshape(SUB * TC, D).astype(BF16)
    proj = jnp.dot(h2d, win_ref[...].astype(BF16), preferred_element_type=F32)
    uf_ref[0] = proj[:, D_S5:].astype(BF16).reshape(SUB, TC, D_FN)
    u = proj[:, :D_S5].astype(BF16)
    perm = perm_ref[...]
    for q in range(TC // PERM_T):
        piece = jnp.concatenate(
            [u[s * TC + q * PERM_T: s * TC + (q + 1) * PERM_T] for s in range(SUB)], axis=0)
        us_ref[0, q * SUB * PERM_T:(q + 1) * SUB * PERM_T, :] = jnp.dot(
            perm, piece, preferred_element_type=F32).astype(BF16)


def _pre(xp4, xs4, pos3, modg, norm1_g, w_in, perm):
    xp_spec, xs_spec, pos_spec = _x_specs()
    return pl.pallas_call(
        _pre_kernel,
        grid=(N_GRP, N_TC),
        in_specs=[xp_spec, xs_spec, pos_spec,
                  pl.BlockSpec((1, SUB, 1, 6 * D), lambda g, tc: (g, 0, 0, 0)),
                  pl.BlockSpec((1, D), lambda g, tc: (0, 0)),
                  pl.BlockSpec((D, D), lambda g, tc: (0, 0)),
                  pl.BlockSpec((SUB * PERM_T, SUB * PERM_T), lambda g, tc: (0, 0))],
        out_specs=[pl.BlockSpec((1, SUB * TC, D_S5), lambda g, tc: (g, tc, 0)),
                   pl.BlockSpec((1, SUB, TC, D_FN), lambda g, tc: (g, 0, tc, 0))],
        out_shape=[jax.ShapeDtypeStruct((N_GRP, GRP_ROWS, D_S5), BF16),
                   jax.ShapeDtypeStruct((N_GRP, SUB, L_BLK, D_FN), BF16)],
        compiler_params=_cparams(("arbitrary", "arbitrary")),
        name="pre_mixer",
    )(xp4, xs4, pos3, modg, norm1_g.reshape(1, D), w_in, perm)


def _cmul(ar, ai, br, bi):
    return ar * br - ai * bi, ar * bi + ai * br


S5_BLK = 512
S5_NBLK = GRP_ROWS // S5_BLK
S5_STEPS = S5_BLK // SUB


def _s5_kernel(us_ref, bb_ref, cc_ref, a_ref, init_ref, dskip_ref, y_ref, fin_ref, in_a, in_b, out_a, out_b):
    g = pl.program_id(0)
    d = pl.program_id(2)
    is_lat = g == N_GRP - 1
    re_all = slice(0, CH_P)
    im_all = slice(CH_P, 2 * CH_P)
    a_re = a_ref[0, 0, :, re_all]
    a_im = a_ref[0, 0, :, im_all]
    bufs_in = (in_a, in_b)
    bufs_out = (out_a, out_b)

    def time_rows(k):
        blk = k + d * (S5_NBLK - 1 - 2 * k)
        return pl.ds(pl.multiple_of(blk * S5_BLK, S5_BLK), S5_BLK)

    def slot_rows(k):
        return slice((k // 2) * S5_BLK, (k // 2 + 1) * S5_BLK)

    def step_rows(k, j):
        q = j + d * (S5_STEPS - 1 - 2 * j)
        return pl.ds(pl.multiple_of((k // 2) * S5_BLK + q * SUB, SUB), SUB)

    def proj_in(k):
        bufs_in[k % 2][slot_rows(k), :] = jnp.dot(us_ref[0, time_rows(k), :], bb_ref[0, 0],
                                                  preferred_element_type=F32)

    def scan_block(k, carry):
        src, dst = bufs_in[k % 2], bufs_out[k % 2]
        s_re, s_im = carry
        for j in range(S5_STEPS):
            rows = step_rows(k, j)
            n_re = a_re * s_re - a_im * s_im + src[rows, re_all]
            n_im = a_re * s_im + a_im * s_re + src[rows, im_all]
            dst[rows, re_all] = n_re
            dst[rows, im_all] = n_im
            s_re, s_im = n_re, n_im
        return s_re, s_im

    def proj_out(k):
        rows = time_rows(k)
        yb = jnp.dot(bufs_out[k % 2][slot_rows(k), :].astype(BF16), cc_ref[0, 0], preferred_element_type=F32)
        y_ref[0, rows, :] = y_ref[0, rows, :] + yb

    @pl.when(d == 0)
    def _():
        y_ref[0] = dskip_ref[...] * us_ref[0].astype(F32)

    def scan_all(carry, with_out):
        proj_in(0)
        for k in range(S5_NBLK):
            if with_out and k >= 1:
                proj_out(k - 1)
            if k + 1 < S5_NBLK:
                proj_in(k + 1)
            carry = scan_block(k, carry)
        if with_out:
            proj_out(S5_NBLK - 1)
        fin_ref[0, 0, :, re_all] = carry[0]
        fin_ref[0, 0, :, im_all] = carry[1]
        return carry

    @pl.when(jnp.logical_not(is_lat))
    def _():
        zero = jnp.zeros((SUB, CH_P), F32)
        scan_all((zero, zero), True)

    @pl.when(is_lat)
    def _():
        f_re, f_im = scan_all((init_ref[0, 0, :, re_all], init_ref[0, 0, :, im_all]), False)

        sub_id = lax.broadcasted_iota(jnp.int32, (SUB, CH_P), 0) % LAT_CHUNKS
        fwd = d == 0
        lo = jnp.where(fwd, 1, 0)
        hi = jnp.where(fwd, LAT_CHUNKS - 1, LAT_CHUNKS - 2)
        keep = (sub_id >= lo) & (sub_id <= hi)

        def from_prev(v):
            return jnp.where(keep, jnp.where(fwd, pltpu.roll(v, 1, 0), pltpu.roll(v, SUB - 1, 0)), 0.0)

        p_re, p_im = a_re, a_im
        for _ in range(8):
            p_re, p_im = _cmul(p_re, p_im, p_re, p_im)
        t_re, t_im = f_re, f_im
        for _ in range(LAT_CHUNKS - 2):
            m_re, m_im = _cmul(p_re, p_im, from_prev(t_re), from_prev(t_im))
            t_re, t_im = f_re + m_re, f_im + m_im
        corr = (from_prev(t_re), from_prev(t_im))

        def fix_block(k, carry):
            dst = bufs_out[k % 2]
            m_re, m_im = carry
            for j in range(S5_STEPS):
                rows = step_rows(k, j)
                m_re, m_im = _cmul(m_re, m_im, a_re, a_im)
                dst[rows, re_all] = dst[rows, re_all] + m_re
                dst[rows, im_all] = dst[rows, im_all] + m_im
            return m_re, m_im

        for k in range(S5_NBLK):
            if k >= 1:
                proj_out(k - 1)
            corr = fix_block(k, corr)
        proj_out(S5_NBLK - 1)


def _s5(us, bbmat, ccmat, a8, init8, d_skip):
    return pl.pallas_call(
        _s5_kernel,
        grid=(N_GRP, N_S5_CHUNKS, 2),
        in_specs=[pl.BlockSpec((1, GRP_ROWS, CH_U), lambda g, c, d: (g, 0, c)),
                  pl.BlockSpec((1, 1, CH_U, 2 * CH_P), lambda g, c, d: (d, c, 0, 0)),
                  pl.BlockSpec((1, 1, 2 * CH_P, CH_U), lambda g, c, d: (d, c, 0, 0)),
                  pl.BlockSpec((1, 1, SUB, 2 * CH_P), lambda g, c, d: (d, c, 0, 0)),
                  pl.BlockSpec((1, 1, SUB, 2 * CH_P), lambda g, c, d: (d, c, 0, 0)),
                  pl.BlockSpec((1, CH_U), lambda g, c, d: (0, c))],
        out_specs=[pl.BlockSpec((1, GRP_ROWS, CH_U), lambda g, c, d: (g, 0, c)),
                   pl.BlockSpec((1, 1, SUB, 2 * CH_P), lambda g, c, d: (d, g, 0, c))],
        out_shape=[jax.ShapeDtypeStruct((N_GRP, GRP_ROWS, D_S5), F32),
                   jax.ShapeDtypeStruct((2, N_GRP, SUB, N_S5_CHUNKS * 2 * CH_P), F32)],
        scratch_shapes=[pltpu.VMEM((GRP_ROWS // 2, 2 * CH_P), F32)] * 4,
        compiler_params=_cparams(("arbitrary", "arbitrary", "arbitrary")),
        name="s5_scan",
    )(us, bbmat, ccmat, a8, init8, d_skip.reshape(1, D_S5))


def _fnet_w_kernel(c_ref, s_ref, w_ref, m1_ref, m2_ref):
    w = w_ref[...]
    m1_ref[...] = jnp.dot(c_ref[...], w, preferred_element_type=F32, precision=HIGHEST).astype(BF16)
    m2_ref[...] = jnp.dot(s_ref[...], w, preferred_element_type=F32, precision=HIGHEST).astype(BF16)


FN_SEQ = LAT_CHUNKS
FN_CTX_STEPS = (N_GRP - 1) * SUB // FN_SEQ


def _fnet_kernel(u_ref, cos_s_ref, sin_s_ref, cos_l_ref, sin_l_ref, m1_ref, m2_ref, o_ref):
    i = pl.program_id(0)
    m1 = m1_ref[...]
    m2 = m2_ref[...]

    def mix(u, cos_ref, sin_ref):
        v1 = jnp.dot(u, m1, preferred_element_type=F32).astype(BF16)
        v2 = jnp.dot(u, m2, preferred_element_type=F32).astype(BF16)
        return (jnp.dot(cos_ref[...].astype(BF16), v1, preferred_element_type=F32)
                - jnp.dot(sin_ref[...].astype(BF16), v2, preferred_element_type=F32)).astype(BF16)

    @pl.when(i < FN_CTX_STEPS)
    def _():
        for s in range(FN_SEQ):
            o_ref[0, s] = mix(u_ref[0, s], cos_s_ref, sin_s_ref)

    @pl.when(i >= FN_CTX_STEPS)
    def _():
        u = u_ref[0].reshape(FN_SEQ * L_BLK, D_FN)
        o_ref[0] = mix(u, cos_l_ref, sin_l_ref).reshape(FN_SEQ, L_BLK, D_FN)


def _dft_tables(n):
    k = np.arange(n, dtype=np.int64)
    ang = (2.0 * np.pi / n) * ((k[:, None] * k[None, :]) % n).astype(np.float64)
    scale = 1.0 / math.sqrt(n)
    return (np.cos(ang) * scale).astype(np.float32), (np.sin(ang) * scale).astype(np.float32)


def _fnet(uf, m1, m2):
    n_lat_steps = SUB // FN_SEQ
    cos_s, sin_s = _dft_tables(L_BLK)
    cos_l, sin_l = _dft_tables(FN_SEQ * L_BLK)
    full = lambda i: (0, 0)
    per_grp = SUB // FN_SEQ
    blk = pl.BlockSpec(
        (1, FN_SEQ, L_BLK, D_FN),
        lambda i: (jnp.minimum(i // per_grp, N_GRP - 1),
                   jnp.where(i < FN_CTX_STEPS, i % per_grp, i - FN_CTX_STEPS), 0, 0))
    table = lambda n: pl.BlockSpec((n, n), full)
    return pl.pallas_call(
        _fnet_kernel,
        grid=(FN_CTX_STEPS + n_lat_steps,),
        in_specs=[blk, table(L_BLK), table(L_BLK), table(FN_SEQ * L_BLK), table(FN_SEQ * L_BLK),
                  table(D_FN), table(D_FN)],
        out_specs=blk,
        out_shape=jax.ShapeDtypeStruct(uf.shape, BF16),
        compiler_params=_cparams(("arbitrary",)),
        name="fnet",
    )(uf, jnp.asarray(cos_s), jnp.asarray(sin_s), jnp.asarray(cos_l), jnp.asarray(sin_l), m1, m2)


def _fnet_weights(w_fnet):
    n_g = D_FN // FN_GW
    cos_c, sin_c = _dft_tables(FN_GW)
    eye = np.eye(n_g, dtype=np.float32)
    cos_bd = np.kron(eye, cos_c)
    sin_bd = np.kron(eye, sin_c)
    w_bd = (w_fnet[:, :, None, :] * jnp.asarray(eye)[:, None, :, None]).reshape(D_FN, D_FN)
    out = jax.ShapeDtypeStruct((D_FN, D_FN), BF16)
    return pl.pallas_call(_fnet_w_kernel, out_shape=(out, out), name="fnet_weights")(
        jnp.asarray(cos_bd), jnp.asarray(sin_bd), w_bd)


def _gelu_tanh(x):
    return 0.5 * x * (1.0 + jnp.tanh(math.sqrt(2.0 / math.pi) * (x + 0.044715 * (x * x * x))))


def _post_kernel(y_ref, yf_ref, xp_ref, xs_ref, pos_ref, mod_ref, wglu_ref, wout_ref, g2_ref, wr_ref,
                 br_ref, permt_ref, x1_ref, h2_ref, comb_ref):
    g = pl.program_id(0)
    z = _gelu_tanh(y_ref[0])
    gate = jnp.dot(z.astype(BF16), wglu_ref[...].astype(BF16), preferred_element_type=F32)
    gl = (z * jax.nn.sigmoid(gate)).astype(BF16)
    permt = permt_ref[...]
    n_q = TC // PERM_T
    nat = [jnp.dot(permt, gl[q * SUB * PERM_T:(q + 1) * SUB * PERM_T], preferred_element_type=F32).astype(BF16)
           for q in range(n_q)]
    gl_nat = jnp.concatenate(
        [nat[q][s * PERM_T:(s + 1) * PERM_T] for s in range(SUB) for q in range(n_q)], axis=0)
    w_out = wout_ref[...].astype(BF16)
    mixed = (jnp.dot(gl_nat, w_out[:D_S5], preferred_element_type=F32)
             + jnp.dot(yf_ref[0].reshape(SUB * TC, D_FN), w_out[D_S5:], preferred_element_type=F32))
    x = _load_x(g, xp_ref, xs_ref, pos_ref)
    mod = mod_ref[0]
    gate1 = mod[:, :, 2 * D:3 * D]
    shift2 = mod[:, :, 3 * D:4 * D]
    scale2 = mod[:, :, 4 * D:5 * D]
    x1 = x + gate1 * mixed.reshape(SUB, TC, D)
    x1_ref[0] = x1
    h2 = _rms(x1, g2_ref[...]) * (1.0 + scale2) + shift2
    h2_ref[0] = h2.astype(BF16)

    hr = h2.reshape(SUB * TC, D)
    h_hi = hr.astype(BF16)
    h_lo = (hr - h_hi.astype(F32)).astype(BF16)
    wr = wr_ref[...]
    w_hi = wr.astype(BF16)
    w_lo = (wr - w_hi.astype(F32)).astype(BF16)
    both = jnp.dot(h_hi, jnp.concatenate([w_hi, w_lo], axis=1), preferred_element_type=F32)
    logits = (both[:, :E_PAD] + jnp.dot(h_lo, w_hi, preferred_element_type=F32) + both[:, E_PAD:]
              + br_ref[...])
    lane = lax.broadcasted_iota(jnp.int32, logits.shape, 1).astype(F32)
    top_v, hots = [], []
    cur = logits
    for _ in range(TOP_K):
        m = jnp.max(cur, axis=-1, keepdims=True)
        idx = jnp.min(jnp.where(cur == m, lane, float(E_PAD)), axis=-1, keepdims=True)
        hot = lane == idx
        top_v.append(m)
        hots.append(hot)
        cur = jnp.where(hot, -3.0e38, cur)
    exps = [jnp.exp(v - top_v[0]) for v in top_v]
    denom = exps[0] + exps[1] + exps[2] + exps[3]
    comb = jnp.zeros(logits.shape, F32)
    for k in range(TOP_K):
        comb = comb + jnp.where(hots[k], exps[k] / denom, 0.0)
    comb_ref[0] = comb.reshape(SUB, TC, E_PAD)


def _post(y, yf, xp4, xs4, pos3, modg, w_glu, w_out, norm2_g, w_router, b_router, permt):
    xp_spec, xs_spec, pos_spec = _x_specs()
    const2 = lambda g, tc: (0, 0)
    wr = jnp.zeros((D, E_PAD), F32).at[:, :N_EXPERTS].set(w_router)
    br = jnp.full((1, E_PAD), -1.0e30, F32).at[0, :N_EXPERTS].set(b_router)
    blk = lambda w: pl.BlockSpec((1, SUB, TC, w), lambda g, tc: (g, 0, tc, 0))
    return pl.pallas_call(
        _post_kernel,
        grid=(N_GRP, N_TC),
        in_specs=[pl.BlockSpec((1, SUB * TC, D_S5), lambda g, tc: (g, tc, 0)),
                  blk(D_FN), xp_spec, xs_spec, pos_spec,
                  pl.BlockSpec((1, SUB, 1, 6 * D), lambda g, tc: (g, 0, 0, 0)),
                  pl.BlockSpec((D_S5, D_S5), const2),
                  pl.BlockSpec((D, D), const2),
                  pl.BlockSpec((1, D), const2),
                  pl.BlockSpec((D, E_PAD), const2),
                  pl.BlockSpec((1, E_PAD), const2),
                  pl.BlockSpec((SUB * PERM_T, SUB * PERM_T), const2)],
        out_specs=[blk(D), blk(D), blk(E_PAD)],
        out_shape=[jax.ShapeDtypeStruct((N_GRP, SUB, L_BLK, D), F32),
                   jax.ShapeDtypeStruct((N_GRP, SUB, L_BLK, D), BF16),
                   jax.ShapeDtypeStruct((N_GRP, SUB, L_BLK, E_PAD), F32)],
        compiler_params=_cparams(("arbitrary", "arbitrary")),
        name="post_mixer",
    )(y, yf, xp4, xs4, pos3, modg, w_glu, w_out, norm2_g.reshape(1, D), wr, br, permt)


TBLK = L_BLK
N_BLK = T_TOK // TBLK
N_CTX_BLK = (N_GRP - 1) * SUB
SEG_ALIGN = 8
RB = 1280
assert RB >= TBLK * TOP_K + N_EXPERTS * (SEG_ALIGN - 1) and RB % 128 == 0
TM = 256
TM_SHIFT = 8
assert 1 << TM_SHIFT == TM
R_TOT = T_TOK * TOP_K + N_BLK * N_EXPERTS * (SEG_ALIGN - 1) + TM
TAB_ROWS = 32
assert TAB_ROWS >= N_BLK
BIG_ROWS = 32
BIG_SHIFT = 2
assert SEG_ALIGN << BIG_SHIFT == BIG_ROWS
SMALL_COPY_PRIORITY = 1


def _plan_kernel(comb_ref, z_ref, ptab_ref, loff_ref, loff_i_ref, gst_i_ref, p8_i_ref, eoff_i_ref):
    row = lax.broadcasted_iota(jnp.int32, (TBLK, TBLK), 0)
    col = lax.broadcasted_iota(jnp.int32, (TBLK, TBLK), 1)
    earlier = jnp.where(row > col, 1.0, 0.0).astype(BF16)
    ptab_ref[...] = jnp.zeros_like(ptab_ref)

    def body(b, _):
        rows = pl.ds(pl.multiple_of(b * TBLK, TBLK), TBLK)
        m = jnp.where(comb_ref[rows, :] > 0.0, 1.0, 0.0)
        rank = jnp.dot(earlier, m.astype(BF16), preferred_element_type=F32)
        z_ref[rows, :] = m * (rank + 1.0)
        n = jnp.sum(m, axis=0, keepdims=True)
        ptab_ref[pl.ds(b, 1), :] = jnp.floor((n + (SEG_ALIGN - 1)) * (1.0 / SEG_ALIGN)) * SEG_ALIGN
        return 0

    lax.fori_loop(0, N_BLK, body, 0)

    ptab = ptab_ref[...]
    er = lax.broadcasted_iota(jnp.int32, (E_PAD, E_PAD), 0)
    ec = lax.broadcasted_iota(jnp.int32, (E_PAD, E_PAD), 1)
    before = jnp.where(er < ec, 1.0, 0.0)
    exact = functools.partial(jnp.dot, preferred_element_type=F32, precision=HIGHEST)
    loff = exact(ptab, before)
    loff_ref[...] = loff
    as_int = lambda t: t[:, :N_EXPERTS].astype(jnp.int32)
    loff_i_ref[...] = as_int(loff)
    p8_i_ref[...] = as_int(ptab * (1.0 / SEG_ALIGN))
    tot = jnp.sum(ptab, axis=0, keepdims=True)
    eoff = exact(jnp.broadcast_to(tot, (SUB, E_PAD)), before)
    eoff_i_ref[...] = eoff.astype(jnp.int32)
    br = lax.broadcasted_iota(jnp.int32, (TAB_ROWS, TAB_ROWS), 0)
    bc = lax.broadcasted_iota(jnp.int32, (TAB_ROWS, TAB_ROWS), 1)
    gst_i_ref[...] = as_int(eoff[0:1] + exact(jnp.where(br > bc, 1.0, 0.0), ptab))


def _plan(comb):
    tab = jax.ShapeDtypeStruct((TAB_ROWS, E_PAD), F32)
    itab = jax.ShapeDtypeStruct((TAB_ROWS, N_EXPERTS), jnp.int32)
    return pl.pallas_call(
        _plan_kernel,
        out_shape=(jax.ShapeDtypeStruct((T_TOK, E_PAD), F32), tab, tab, itab, itab, itab,
                   jax.ShapeDtypeStruct((SUB, E_PAD), jnp.int32)),
        compiler_params=pltpu.CompilerParams(vmem_limit_bytes=VMEM_LIMIT),
        name="moe_plan",
    )(comb)


def _sort_matrix(b, z_ref, loff_ref, ptab_ref):
    loff = loff_ref[pl.ds(b, 1), :]
    size = ptab_ref[pl.ds(b, 1), :]
    r = lax.broadcasted_iota(jnp.int32, (RB, E_PAD), 0).astype(F32)
    owner = jnp.where(r >= loff, jnp.where(r < loff + size, 1.0, 0.0), 0.0)
    rank1 = r[:, 0:1] - jnp.sum(owner * loff, axis=-1, keepdims=True) + 1.0
    zt = z_ref[...].T
    v = jnp.dot(owner.astype(BF16), zt.astype(BF16), preferred_element_type=F32)
    return jnp.where(v == rank1, 1.0, 0.0), owner


def _segment_copies(b, loff_s, gst_s, p8_s, make_copy):
    def per_expert(e, counts):
        chunks = p8_s[b, e]
        n_big = lax.shift_right_logical(chunks, BIG_SHIFT)
        n_small = chunks - n_big * (BIG_ROWS // SEG_ALIGN)
        local0 = loff_s[b, e]
        global0 = gst_s[b, e]

        def big(j, _):
            make_copy(pl.multiple_of(local0 + j * BIG_ROWS, SEG_ALIGN),
                      pl.multiple_of(global0 + j * BIG_ROWS, SEG_ALIGN), BIG_ROWS).start()
            return 0

        def small(j, _):
            off = n_big * BIG_ROWS + j * SEG_ALIGN
            make_copy(pl.multiple_of(local0 + off, SEG_ALIGN),
                      pl.multiple_of(global0 + off, SEG_ALIGN), SEG_ALIGN).start(priority=SMALL_COPY_PRIORITY)
            return 0

        lax.fori_loop(0, n_big, big, 0)
        lax.fori_loop(0, n_small, small, 0)
        return counts[0] + n_big, counts[1] + n_small

    return lax.fori_loop(0, N_EXPERTS, per_expert, (0, 0))


def _wait_copies(counts, make_copy):
    for n, rows in zip(counts, (BIG_ROWS, SEG_ALIGN)):
        def wait_one(i, _, rows=rows):
            make_copy(0, 0, rows).wait()
            return 0
        lax.fori_loop(0, n, wait_one, 0)


def _zero_rows_from(hbm, zeros_vmem, first_row, sem):
    def copy_to(row):
        return pltpu.make_async_copy(zeros_vmem, hbm.at[pl.ds(row, TM)], sem)

    n_full = lax.shift_right_logical(R_TOT - first_row, TM_SHIFT)

    def start_one(j, _):
        copy_to(pl.multiple_of(first_row + j * TM, SEG_ALIGN)).start()
        return 0

    def wait_one(j, _):
        copy_to(0).wait()
        return 0

    lax.fori_loop(0, n_full, start_one, 0)
    lax.fori_loop(0, n_full, wait_one, 0)
    last = copy_to(R_TOT - TM)
    last.start()
    last.wait()


def _dispatch_kernel(loff_s, gst_s, p8_s, eoff_s, h_ref, z_ref, loff_ref, ptab_ref, xs_hbm, xbuf, sem, pending):
    b = pl.program_id(0)
    slot = lax.rem(b, 2)

    def copies_of(s):
        def make_copy(local_row, global_row, rows):
            return pltpu.make_async_copy(xbuf.at[s, pl.ds(local_row, rows)], xs_hbm.at[pl.ds(global_row, rows)],
                                         sem.at[s])
        return make_copy

    pm = _sort_matrix(b, z_ref, loff_ref, ptab_ref)[0].astype(BF16)

    @pl.when(b >= 2)
    def _():
        _wait_copies((pending[2 * slot], pending[2 * slot + 1]), copies_of(slot))

    xbuf[slot] = jnp.dot(pm, h_ref[...], preferred_element_type=F32)
    n_big, n_small = _segment_copies(b, loff_s, gst_s, p8_s, copies_of(slot))
    pending[2 * slot] = n_big
    pending[2 * slot + 1] = n_small

    @pl.when(b == N_BLK - 1)
    def _():
        _wait_copies((n_big, n_small), copies_of(slot))
        _wait_copies((pending[2 * (1 - slot)], pending[2 * (1 - slot) + 1]), copies_of(1 - slot))
        xbuf[0, 0:TM, :] = jnp.zeros((TM, D), F32)
        _zero_rows_from(xs_hbm, xbuf.at[0, 0:TM], eoff_s[0, N_EXPERTS], sem.at[0])


def _dispatch(h2, z, loff, ptab, loff_i, gst_i, p8_i, eoff_i):
    whole = pl.BlockSpec((TAB_ROWS, E_PAD), lambda b, *_: (0, 0))
    return pl.pallas_call(
        _dispatch_kernel,
        grid_spec=pltpu.PrefetchScalarGridSpec(
            num_scalar_prefetch=4,
            grid=(N_BLK,),
            in_specs=[pl.BlockSpec((TBLK, D), lambda b, *_: (b, 0)),
                      pl.BlockSpec((TBLK, E_PAD), lambda b, *_: (b, 0)),
                      whole, whole],
            out_specs=pl.BlockSpec(memory_space=pl.ANY),
            scratch_shapes=[pltpu.VMEM((2, RB, D), F32), pltpu.SemaphoreType.DMA((2,)),
                            pltpu.SMEM((4,), jnp.int32)]),
        out_shape=jax.ShapeDtypeStruct((R_TOT, D), F32),
        compiler_params=_cparams(("arbitrary",)),
        name="moe_dispatch",
    )(loff_i, gst_i, p8_i, eoff_i, h2, z, loff, ptab)


MAX_TILES = R_TOT // TM + N_EXPERTS
X_AHEAD = 3
X_SLOTS = X_AHEAD + 1
Y_SLOTS = 2
W_AHEAD = 2
W_SLOTS = W_AHEAD + 1
N_MATS = 3
FF_CHUNK = 512
CAST_ROWS = 64
TILE_STEP = 64


def _expert_kernel(eoff_s, xs_hbm, wg_hbm, wu_hbm, wd_hbm, bg_ref, bu_ref, bd_ref, ys_hbm,
                   wst, wbf, xin, yout, act_ref, w_sem, x_sem, y_sem, t_exp, t_row, t_first, t_valid, live):
    def add_expert(e, carry):
        n_t, n_live = carry
        start = eoff_s[0, e]
        count = eoff_s[0, e + 1] - start
        tiles = lax.shift_right_logical(count + (TM - 1), TM_SHIFT)

        def add_tile(i, _):
            t_exp[n_t + i] = e
            t_row[n_t + i] = start + i * TM
            t_first[n_t + i] = jnp.where(i == 0, 1, 0)
            t_valid[n_t + i] = jnp.minimum(count - i * TM, TM)
            return 0

        lax.fori_loop(0, tiles, add_tile, 0)
        has_rows = jnp.where(tiles > 0, 1, 0)

        @pl.when(tiles > 0)
        def _():
            live[n_live] = e

        return n_t + tiles, n_live + has_rows

    n_tiles, n_live = lax.fori_loop(0, N_EXPERTS, add_expert, (0, 0))

    def w_copies(e, slot):
        return [pltpu.make_async_copy(w.at[e], wst.at[slot, m], w_sem.at[slot])
                for m, w in enumerate((wg_hbm, wu_hbm, wd_hbm))]

    def tile_rows(j):
        return pl.ds(pl.multiple_of(t_row[j], SEG_ALIGN), TM)

    def x_copy(j, slot):
        return pltpu.make_async_copy(xs_hbm.at[tile_rows(j)], xin.at[slot], x_sem.at[slot])

    def y_pieces(j, op):
        ys = lax.rem(j, Y_SLOTS)
        valid = t_valid[j]
        row0 = t_row[j]

        def piece(off, rows):
            return pltpu.make_async_copy(yout.at[ys, pl.ds(off, rows)],
                                         ys_hbm.at[pl.ds(pl.multiple_of(row0 + off, SEG_ALIGN), rows)],
                                         y_sem.at[ys])

        @pl.when(valid == TM)
        def _():
            op(piece(0, TM))

        @pl.when(valid < TM)
        def _():
            n_big = lax.shift_right_logical(valid, BIG_ROWS.bit_length() - 1)
            n_small = lax.shift_right_logical(valid - n_big * BIG_ROWS, SEG_ALIGN.bit_length() - 1)

            def big(i, _):
                op(piece(pl.multiple_of(i * BIG_ROWS, SEG_ALIGN), BIG_ROWS))
                return 0

            def small(i, _):
                op(piece(pl.multiple_of(n_big * BIG_ROWS + i * SEG_ALIGN, SEG_ALIGN), SEG_ALIGN))
                return 0

            lax.fori_loop(0, n_big, big, 0)
            lax.fori_loop(0, n_small, small, 0)

    for ahead in range(W_AHEAD):
        @pl.when(n_live > ahead)
        def _(ahead=ahead):
            for cp in w_copies(live[ahead], ahead):
                cp.start()

    for ahead in range(X_AHEAD):
        @pl.when(n_tiles > ahead)
        def _(ahead=ahead):
            x_copy(ahead, ahead).start()

    def body(j, k):
        e = t_exp[j]

        @pl.when(t_first[j] == 1)
        def _():
            ws = lax.rem(k, W_SLOTS)

            @pl.when(k + W_AHEAD < n_live)
            def _():
                for cp in w_copies(live[k + W_AHEAD], lax.rem(k + W_AHEAD, W_SLOTS)):
                    cp.start()

            for cp in w_copies(e, ws):
                cp.wait()

            def cast_rows(r, _):
                rows = pl.ds(pl.multiple_of(r * CAST_ROWS, CAST_ROWS), CAST_ROWS)
                for m in range(N_MATS):
                    wbf[m, rows, :] = wst[ws, m, rows, :].astype(BF16)
                return 0

            lax.fori_loop(0, D // CAST_ROWS, cast_rows, 0)

        slot = lax.rem(j, X_SLOTS)
        x_copy(j, slot).wait()

        @pl.when(j + X_AHEAD < n_tiles)
        def _():
            x_copy(j + X_AHEAD, lax.rem(j + X_AHEAD, X_SLOTS)).start()

        @pl.when(j >= Y_SLOTS)
        def _():
            y_pieces(j - Y_SLOTS, lambda cp: cp.wait())

        ys = lax.rem(j, Y_SLOTS)

        def compute(rows):
            x = xin[slot, 0:rows, :].astype(BF16)
            for c in range(D // FF_CHUNK):
                cols = slice(c * FF_CHUNK, (c + 1) * FF_CHUNK)
                gate = jnp.dot(x, wbf[0, :, cols], preferred_element_type=F32) + bg_ref[e][:, cols]
                up = jnp.dot(x, wbf[1, :, cols], preferred_element_type=F32) + bu_ref[e][:, cols]
                gate = jnp.minimum(gate, SWIGLU_LIMIT)
                up = jnp.clip(up, -SWIGLU_LIMIT, SWIGLU_LIMIT)
                act_ref[0:rows, cols] = ((up + 1.0) * gate * jax.nn.sigmoid(SWIGLU_ALPHA * gate)).astype(BF16)
            yout[ys, 0:rows, :] = jnp.dot(act_ref[0:rows, :], wbf[2], preferred_element_type=F32) + bd_ref[e]

        valid = t_valid[j]
        for rows in range(TILE_STEP, TM + 1, TILE_STEP):
            @pl.when(jnp.logical_and(valid > rows - TILE_STEP, valid <= rows))
            def _(rows=rows):
                compute(rows)

        y_pieces(j, lambda cp: cp.start())
        return k + t_first[j]

    lax.fori_loop(0, n_tiles, body, 0)

    for back in range(Y_SLOTS, 0, -1):
        @pl.when(n_tiles >= back)
        def _(back=back):
            y_pieces(n_tiles - back, lambda cp: cp.wait())

    yout[0] = jnp.zeros((TM, D), F32)
    _zero_rows_from(ys_hbm, yout.at[0], eoff_s[0, N_EXPERTS], y_sem.at[0])


def _experts(xs, eoff_i, w_gate, b_gate, w_up, b_up, w_down, b_down):
    hbm = pl.BlockSpec(memory_space=pl.ANY)
    bspec = pl.BlockSpec((N_EXPERTS, 1, D), lambda i, *_: (0, 0, 0))
    return pl.pallas_call(
        _expert_kernel,
        grid_spec=pltpu.PrefetchScalarGridSpec(
            num_scalar_prefetch=1,
            grid=(1,),
            in_specs=[hbm, hbm, hbm, hbm, bspec, bspec, bspec],
            out_specs=hbm,
            scratch_shapes=[pltpu.VMEM((W_SLOTS, N_MATS, D, D), F32), pltpu.VMEM((N_MATS, D, D), BF16),
                            pltpu.VMEM((X_SLOTS, TM, D), F32), pltpu.VMEM((Y_SLOTS, TM, D), F32),
                            pltpu.VMEM((TM, D), BF16),
                            pltpu.SemaphoreType.DMA((W_SLOTS,)), pltpu.SemaphoreType.DMA((X_SLOTS,)),
                            pltpu.SemaphoreType.DMA((Y_SLOTS,)),
                            pltpu.SMEM((MAX_TILES,), jnp.int32), pltpu.SMEM((MAX_TILES,), jnp.int32),
                            pltpu.SMEM((MAX_TILES,), jnp.int32), pltpu.SMEM((MAX_TILES,), jnp.int32),
                            pltpu.SMEM((N_EXPERTS,), jnp.int32)]),
        out_shape=jax.ShapeDtypeStruct((R_TOT, D), F32),
        compiler_params=_cparams(("arbitrary",)),
        name="moe_experts",
    )(eoff_i, xs, w_gate, w_up, w_down, b_gate.reshape(N_EXPERTS, 1, D), b_up.reshape(N_EXPERTS, 1, D),
      b_down.reshape(N_EXPERTS, 1, D))


def _combine_kernel(loff_s, gst_s, p8_s, ys_hbm, z_ref, comb_ref, loff_ref, ptab_ref, x1_ref, mod_ref, gf_ref,
                    yctx_ref, ylat_ref, ybuf, sem):
    b = pl.program_id(0)

    def make_copy(local_row, global_row, rows):
        return pltpu.make_async_copy(ys_hbm.at[pl.ds(global_row, rows)], ybuf.at[pl.ds(local_row, rows)], sem)

    started = _segment_copies(b, loff_s, gst_s, p8_s, make_copy)

    used_chunks = lax.shift_right_logical(loff_s[b, N_EXPERTS - 1], 3) + p8_s[b, N_EXPERTS - 1]

    def zero_chunk(j, _):
        ybuf[pl.ds(pl.multiple_of(j * SEG_ALIGN, SEG_ALIGN), SEG_ALIGN), :] = jnp.zeros((SEG_ALIGN, D), F32)
        return 0

    lax.fori_loop(used_chunks, RB // SEG_ALIGN, zero_chunk, 0)

    pm, owner = _sort_matrix(b, z_ref, loff_ref, ptab_ref)
    comb = comb_ref[...]
    c_hi = comb.astype(BF16)
    rest = comb - c_hi.astype(F32)
    c_mid = rest.astype(BF16)
    c_lo = (rest - c_mid.astype(F32)).astype(BF16)
    pmb = pm.astype(BF16)
    moved = (jnp.dot(pmb, c_hi, preferred_element_type=F32) + jnp.dot(pmb, c_mid, preferred_element_type=F32)
             + jnp.dot(pmb, c_lo, preferred_element_type=F32))
    w_row = jnp.sum(owner * moved, axis=-1, keepdims=True)
    pmt = pm.T.astype(BF16)
    _wait_copies(started, make_copy)
    y = (ybuf[...] * w_row).astype(BF16)
    moe = jnp.dot(pmt, y, preferred_element_type=F32)
    gate2 = mod_ref[0][:, 5 * D:6 * D]
    x2 = x1_ref[0] + gate2 * moe
    y = _rms(x2, gf_ref[...])

    @pl.when(b < N_CTX_BLK)
    def _():
        yctx_ref[0] = y

    @pl.when(b >= N_CTX_BLK)
    def _():
        ylat_ref[0] = y


def _combine(ys, z, comb, loff, ptab, x1, modv, norm_f_g, loff_i, gst_i, p8_i):
    whole = pl.BlockSpec((TAB_ROWS, E_PAD), lambda b, *_: (0, 0))
    tok = pl.BlockSpec((TBLK, E_PAD), lambda b, *_: (b, 0))
    return pl.pallas_call(
        _combine_kernel,
        grid_spec=pltpu.PrefetchScalarGridSpec(
            num_scalar_prefetch=3,
            grid=(N_BLK,),
            in_specs=[pl.BlockSpec(memory_space=pl.ANY), tok, tok, whole, whole,
                      pl.BlockSpec((1, TBLK, D), lambda b, *_: (b, 0, 0)),
                      pl.BlockSpec((1, 1, 6 * D), lambda b, *_: (b, 0, 0)),
                      pl.BlockSpec((1, D), lambda b, *_: (0, 0))],
            out_specs=[pl.BlockSpec((1, TBLK, D), lambda b, *_: (jnp.minimum(b, N_CTX_BLK - 1), 0, 0)),
                       pl.BlockSpec((1, TBLK, D), lambda b, *_: (jnp.maximum(b - N_CTX_BLK, 0), 0, 0))],
            scratch_shapes=[pltpu.VMEM((RB, D), F32), pltpu.SemaphoreType.DMA(())]),
        out_shape=[jax.ShapeDtypeStruct((N_CTX_BLK, TBLK, D), F32),
                   jax.ShapeDtypeStruct((N_BLK - N_CTX_BLK, TBLK, D), F32)],
        compiler_params=_cparams(("arbitrary",)),
        name="moe_combine",
    )(loff_i, gst_i, p8_i, ys, z, comb, loff, ptab, x1, modv, norm_f_g.reshape(1, D))


def _moe_and_final(x1, h2, comb, modg, norm_f_g, w_gate, b_gate, w_up, b_up, w_down, b_down):
    z, ptab, loff, loff_i, gst_i, p8_i, eoff_i = _plan(comb)
    xs = _dispatch(h2, z, loff, ptab, loff_i, gst_i, p8_i, eoff_i)
    ys = _experts(xs, eoff_i, w_gate, b_gate, w_up, b_up, w_down, b_down)
    return _combine(ys, z, comb, loff, ptab, x1, modg.reshape(N_BLK, 1, 6 * D), norm_f_g,
                    loff_i, gst_i, p8_i)


def _grid_pos_embed(n_tokens, dim):
    rows = n_tokens // GRID_W
    t = np.arange(rows * GRID_W)
    r = (t // GRID_W).astype(np.float32)
    col = (t % GRID_W).astype(np.float32)
    q = dim // 4
    omega = (1.0 / np.float32(POS_TEMP) ** (np.arange(q, dtype=np.float32) / np.float32(q))).astype(np.float32)

    def emb(p):
        a = p[:, None] * omega[None, :]
        return np.concatenate([np.sin(a), np.cos(a)], axis=-1)

    return np.concatenate([emb(r), emb(col)], axis=-1).astype(np.float32)


def kernel(x_prompt, x_sample, c, state_s5_re, state_s5_im, c_ctx, w_ada, b_ada, norm1_g, w_in, s5_lam_re,
           s5_lam_im, s5_log_dt, s5_b_re, s5_b_im, s5_c_re, s5_c_im, s5_d, s5_w_glu, w_fnet, w_out, norm2_g,
           w_router, b_router, w_gate, b_gate, w_up, b_up, w_down, b_down, norm_f_g):
    n_ctx, n_lat = x_prompt.shape[0], x_sample.shape[0]
    assert x_prompt.shape == (SUB * (N_GRP - 1), L_BLK, D) and x_sample.shape == (2, LAT_CHUNKS * L_BLK, D)
    assert w_ada.shape[0] == 1, "one trunk layer"
    layer = 0

    cvec = jnp.concatenate([jnp.broadcast_to(c_ctx[None], (n_ctx, D)), jnp.repeat(c, LAT_CHUNKS, axis=0)], axis=0)
    modg = _adaln(cvec, w_ada[layer], b_ada[layer]).reshape(N_GRP, SUB, 1, 6 * D)

    xp4 = x_prompt.reshape(N_GRP - 1, SUB, L_BLK, D)
    xs4 = x_sample.reshape(1, SUB, L_BLK, D)
    pos3 = jnp.asarray(_grid_pos_embed(LAT_CHUNKS * L_BLK, D).reshape(LAT_CHUNKS, L_BLK, D))
    perm = _perm_matrix()

    us, uf = _pre(xp4, xs4, pos3, modg, norm1_g[layer], w_in[layer], jnp.asarray(perm, BF16))

    bbmat, ccmat, a8 = _s5_params(s5_lam_re[layer], s5_lam_im[layer], s5_log_dt[layer], s5_b_re[layer],
                                  s5_b_im[layer], s5_c_re[layer], s5_c_im[layer])

    st = jnp.stack([state_s5_re[:, layer], state_s5_im[:, layer]], axis=2)
    st = st.reshape(n_lat, 2, 2, N_S5_CHUNKS, CH_P)
    st = jnp.transpose(st, (1, 3, 0, 2, 4)).reshape(2, N_S5_CHUNKS, n_lat, 2 * CH_P)
    init8 = jnp.zeros((2, N_S5_CHUNKS, n_lat, LAT_CHUNKS, 2 * CH_P), F32)
    init8 = init8.at[0, :, :, 0].set(st[0]).at[1, :, :, LAT_CHUNKS - 1].set(st[1])
    init8 = init8.reshape(2, N_S5_CHUNKS, SUB, 2 * CH_P)

    y_s5, fin = _s5(us, bbmat, ccmat, a8, init8, s5_d[layer])

    m1, m2 = _fnet_weights(w_fnet[layer])
    yf = _fnet(uf, m1, m2)

    x1, h2, comb = _post(y_s5, yf, xp4, xs4, pos3, modg, s5_w_glu[layer], w_out[layer], norm2_g[layer],
                         w_router[layer], b_router[layer], jnp.asarray(perm.T, BF16))

    y_prompt, y_lat = _moe_and_final(x1.reshape(N_BLK, TBLK, D), h2.reshape(T_TOK, D), comb.reshape(T_TOK, E_PAD),
                                     modg, norm_f_g, w_gate[layer], b_gate[layer], w_up[layer], b_up[layer],
                                     w_down[layer], b_down[layer])
    y_sample = y_lat.reshape(n_lat, LAT_CHUNKS * L_BLK, D)

    fin = fin[:, :N_GRP - 1].reshape(2, N_GRP - 1, SUB, N_S5_CHUNKS, 2, CH_P)
    fin = jnp.transpose(fin, (4, 1, 2, 0, 3, 5)).reshape(2, n_ctx, 1, 2, N_S5_GROUPS, S5_P)
    return (y_prompt, y_sample, fin[0], fin[1])
```

```python
import functools
import math

import numpy as np
import jax
import jax.numpy as jnp
from jax import lax
from jax.experimental import pallas as pl
from jax.experimental.pallas import tpu as pltpu

F32 = jnp.float32
BF16 = jnp.bfloat16
HIGHEST = lax.Precision.HIGHEST

D = 1024
D_S5 = 768
S5_H = 16
S5_P = 64
N_S5_GROUPS = 48
D_FN = 256
FN_GW = 64
N_EXPERTS = 32
TOP_K = 4
E_PAD = 128
SWIGLU_LIMIT = 7.0
SWIGLU_ALPHA = 1.702
RMS_EPS = 1e-6
POS_TEMP = 10000.0
GRID_W = 64

L_BLK = 256
SUB = 8
N_GRP = 3
GRP_ROWS = L_BLK * SUB
LAT_CHUNKS = 4
T_TOK = N_GRP * GRP_ROWS
TC = 128
N_TC = L_BLK // TC
PERM_T = 32
S5_CHUNK = 16
N_S5_CHUNKS = N_S5_GROUPS // S5_CHUNK
CH_U = S5_CHUNK * S5_H
CH_P = S5_CHUNK * S5_P
V7X_VMEM_BYTES = 64 * 1024 * 1024
VMEM_LIMIT = V7X_VMEM_BYTES * 7 // 8


def _cparams(sem):
    return pltpu.CompilerParams(dimension_semantics=sem, vmem_limit_bytes=VMEM_LIMIT)


ADA_K = 256


def _adaln_kernel(c_ref, w_ref, b_ref, o_ref):
    @pl.when(pl.program_id(0) == 0)
    def _():
        o_ref[...] = jnp.broadcast_to(b_ref[...], o_ref.shape)

    c = c_ref[...]
    s = c * jax.nn.sigmoid(c)
    s_hi = s.astype(BF16)
    s_lo = (s - s_hi.astype(F32)).astype(BF16)
    w = w_ref[...]
    w_hi = w.astype(BF16)
    w_lo = (w - w_hi.astype(F32)).astype(BF16)
    o_ref[...] += (jnp.dot(s_hi, w_hi, preferred_element_type=F32) + jnp.dot(s_lo, w_hi, preferred_element_type=F32)
                   + jnp.dot(s_hi, w_lo, preferred_element_type=F32))


def _adaln(cvec, w_ada, b_ada):
    k_dim, n = w_ada.shape
    rows = cvec.shape[0]
    return pl.pallas_call(
        _adaln_kernel,
        grid=(k_dim // ADA_K,),
        in_specs=[pl.BlockSpec((rows, ADA_K), lambda k: (0, k)),
                  pl.BlockSpec((ADA_K, n), lambda k: (k, 0)),
                  pl.BlockSpec((1, n), lambda k: (0, 0))],
        out_specs=pl.BlockSpec((rows, n), lambda k: (0, 0)),
        out_shape=jax.ShapeDtypeStruct((rows, n), F32),
        compiler_params=_cparams(("arbitrary",)),
        name="adaln",
    )(cvec, w_ada, b_ada.reshape(1, n))


def _zoh(lam_re, lam_im, log_dt):
    dt = jnp.exp(log_dt)
    a_re = jnp.minimum(lam_re, -1e-4)
    a_im = lam_im
    mag = jnp.exp(a_re * dt)
    ab_re = mag * jnp.cos(a_im * dt)
    ab_im = mag * jnp.sin(a_im * dt)
    den = a_re * a_re + a_im * a_im
    nr = ab_re - 1.0
    f_re = (nr * a_re + ab_im * a_im) / den
    f_im = (ab_im * a_re - nr * a_im) / den
    return ab_re, ab_im, f_re, f_im


def _s5_params_kernel(lre3_ref, lim3_ref, ldt3_ref, lre2_ref, lim2_ref, ldt2_ref, bre_ref, bim_ref,
                      cre_ref, cim_ref, bb_ref, cc_ref, a_ref):
    exact = functools.partial(jnp.dot, preferred_element_type=F32, precision=HIGHEST)
    spread = jnp.where(lax.broadcasted_iota(jnp.int32, (S5_P, CH_P), 1) % S5_P
                       == lax.broadcasted_iota(jnp.int32, (S5_P, CH_P), 0), 1.0, 0.0)

    def block_diag(m):
        wide = jnp.dot(m.reshape(CH_U, S5_P).astype(BF16), spread.astype(BF16), preferred_element_type=F32)
        row_g = lax.broadcasted_iota(jnp.int32, (CH_U, CH_P), 0) // S5_H
        col_g = lax.broadcasted_iota(jnp.int32, (CH_U, CH_P), 1) // S5_P
        return jnp.where(row_g == col_g, wide, 0.0)

    _, _, f_re, f_im = _zoh(lre3_ref[0, 0], lim3_ref[0, 0], ldt3_ref[0, 0])
    b_re = bre_ref[0, 0]
    b_im = bim_ref[0, 0]
    bb_ref[0, 0] = jnp.concatenate([block_diag(f_re * b_re - f_im * b_im),
                                    block_diag(f_re * b_im + f_im * b_re)], axis=1).astype(BF16)
    cc_ref[0, 0] = jnp.concatenate([block_diag(cre_ref[0, 0]).T,
                                    -block_diag(cim_ref[0, 0]).T], axis=0).astype(BF16)

    ab_re, ab_im, _, _ = _zoh(lre2_ref[0, 0], lim2_ref[0, 0], ldt2_ref[0, 0])
    row_g = lax.broadcasted_iota(jnp.int32, (S5_CHUNK, CH_P), 0)
    col_g = lax.broadcasted_iota(jnp.int32, (S5_CHUNK, CH_P), 1) // S5_P

    def lane_row(a):
        flat = jnp.sum(jnp.where(row_g == col_g, exact(a, spread), 0.0), axis=0, keepdims=True)
        return jnp.broadcast_to(flat, (SUB, CH_P))

    a_ref[0, 0] = jnp.concatenate([lane_row(ab_re), lane_row(ab_im)], axis=1)


def _s5_params(lam_re, lam_im, log_dt, b_re, b_im, c_re, c_im):
    nc = N_S5_CHUNKS
    ldt = jnp.broadcast_to(log_dt[:, :, None], (2, N_S5_GROUPS, S5_P))
    g3 = lambda a: a.reshape(2, nc, S5_CHUNK, 1, S5_P)
    g2 = lambda a: a.reshape(2, nc, S5_CHUNK, S5_P)
    ghp = lambda a: a.reshape(2, nc, S5_CHUNK, S5_H, S5_P)
    spec3 = pl.BlockSpec((1, 1, S5_CHUNK, 1, S5_P), lambda d, c: (d, c, 0, 0, 0))
    spec2 = pl.BlockSpec((1, 1, S5_CHUNK, S5_P), lambda d, c: (d, c, 0, 0))
    spec_ghp = pl.BlockSpec((1, 1, S5_CHUNK, S5_H, S5_P), lambda d, c: (d, c, 0, 0, 0))
    return pl.pallas_call(
        _s5_params_kernel,
        grid=(2, nc),
        in_specs=[spec3, spec3, spec3, spec2, spec2, spec2, spec_ghp, spec_ghp, spec_ghp, spec_ghp],
        out_specs=[pl.BlockSpec((1, 1, CH_U, 2 * CH_P), lambda d, c: (d, c, 0, 0)),
                   pl.BlockSpec((1, 1, 2 * CH_P, CH_U), lambda d, c: (d, c, 0, 0)),
                   pl.BlockSpec((1, 1, SUB, 2 * CH_P), lambda d, c: (d, c, 0, 0))],
        out_shape=[jax.ShapeDtypeStruct((2, nc, CH_U, 2 * CH_P), BF16),
                   jax.ShapeDtypeStruct((2, nc, 2 * CH_P, CH_U), BF16),
                   jax.ShapeDtypeStruct((2, nc, SUB, 2 * CH_P), F32)],
        compiler_params=_cparams(("arbitrary", "arbitrary")),
        name="s5_params",
    )(g3(lam_re), g3(lam_im), g3(ldt), g2(lam_re), g2(lam_im), g2(ldt),
      ghp(jnp.swapaxes(b_re, -1, -2)), ghp(jnp.swapaxes(b_im, -1, -2)), ghp(c_re), ghp(c_im))


def _rms(x, g):
    return x * lax.rsqrt(jnp.mean(x * x, axis=-1, keepdims=True) + RMS_EPS) * g


def _load_x(g, xp_ref, xs_ref, pos_ref):
    is_lat = g == N_GRP - 1
    pos = pos_ref[...]
    pos8 = jnp.concatenate([pos, pos], axis=0)
    return jnp.where(is_lat, xs_ref[0] + pos8, xp_ref[0])


def _x_specs():
    n_ctx = N_GRP - 1
    xp_spec = pl.BlockSpec(
        (1, SUB, TC, D),
        lambda g, tc: (jnp.minimum(g, n_ctx - 1), 0, jnp.where(g >= n_ctx, N_TC - 1, tc), 0))
    xs_spec = pl.BlockSpec((1, SUB, TC, D), lambda g, tc: (0, 0, jnp.where(g >= n_ctx, tc, 0), 0))
    pos_spec = pl.BlockSpec((LAT_CHUNKS, TC, D), lambda g, tc: (0, tc, 0))
    return xp_spec, xs_spec, pos_spec


def _perm_matrix():
    p = np.zeros((SUB * PERM_T, SUB * PERM_T), np.float32)
    for s in range(SUB):
        for j in range(PERM_T):
            p[j * SUB + s, s * PERM_T + j] = 1.0
    return p


def _pre_kernel(xp_ref, xs_ref, pos_ref, mod_ref, g1_ref, win_ref, perm_ref, us_ref, uf_ref):
    g = pl.program_id(0)
    x = _load_x(g, xp_ref, xs_ref, pos_ref)
    mod = mod_ref[0]
    shift1 = mod[:, :, 0:D]
    scale1 = mod[:, :, D:2 * D]
    h = _rms(x, g1_ref[...]) * (1.0 + scale1) + shift1
    h2d = h.reshape(SUB * TC, D).astype(BF16)
    proj = jnp.dot(h2d, win_ref[...].astype(BF16), preferred_element_type=F32)
    uf_ref[0] = proj[:, D_S5:].astype(BF16).reshape(SUB, TC, D_FN)
    u = proj[:, :D_S5].astype(BF16)
    perm = perm_ref[...]
    for q in range(TC // PERM_T):
        piece = jnp.concatenate(
            [u[s * TC + q * PERM_T: s * TC + (q + 1) * PERM_T] for s in range(SUB)], axis=0)
        us_ref[0, q * SUB * PERM_T:(q + 1) * SUB * PERM_T, :] = jnp.dot(
            perm, piece, preferred_element_type=F32).astype(BF16)


def _pre(xp4, xs4, pos3, modg, norm1_g, w_in, perm):
    xp_spec, xs_spec, pos_spec = _x_specs()
    return pl.pallas_call(
        _pre_kernel,
        grid=(N_GRP, N_TC),
        in_specs=[xp_spec, xs_spec, pos_spec,
                  pl.BlockSpec((1, SUB, 1, 6 * D), lambda g, tc: (g, 0, 0, 0)),
                  pl.BlockSpec((1, D), lambda g, tc: (0, 0)),
                  pl.BlockSpec((D, D), lambda g, tc: (0, 0)),
                  pl.BlockSpec((SUB * PERM_T, SUB * PERM_T), lambda g, tc: (0, 0))],
        out_specs=[pl.BlockSpec((1, SUB * TC, D_S5), lambda g, tc: (g, tc, 0)),
                   pl.BlockSpec((1, SUB, TC, D_FN), lambda g, tc: (g, 0, tc, 0))],
        out_shape=[jax.ShapeDtypeStruct((N_GRP, GRP_ROWS, D_S5), BF16),
                   jax.ShapeDtypeStruct((N_GRP, SUB, L_BLK, D_FN), BF16)],
        compiler_params=_cparams(("arbitrary", "arbitrary")),
        name="pre_mixer",
    )(xp4, xs4, pos3, modg, norm1_g.reshape(1, D), w_in, perm)


def _cmul(ar, ai, br, bi):
    return ar * br - ai * bi, ar * bi + ai * br


S5_BLK = 512
S5_NBLK = GRP_ROWS // S5_BLK
S5_STEPS = S5_BLK // SUB


def _s5_kernel(us_ref, bb_ref, cc_ref, a_ref, init_ref, dskip_ref, y_ref, fin_ref, in_a, in_b, out_a, out_b):
    g = pl.program_id(0)
    d = pl.program_id(2)
    is_lat = g == N_GRP - 1
    re_all = slice(0, CH_P)
    im_all = slice(CH_P, 2 * CH_P)
    a_re = a_ref[0, 0, :, re_all]
    a_im = a_ref[0, 0, :, im_all]
    bufs_in = (in_a, in_b)
    bufs_out = (out_a, out_b)

    def time_rows(k):
        blk = k + d * (S5_NBLK - 1 - 2 * k)
        return pl.ds(pl.multiple_of(blk * S5_BLK, S5_BLK), S5_BLK)

    def slot_rows(k):
        return slice((k // 2) * S5_BLK, (k // 2 + 1) * S5_BLK)

    def step_rows(k, j):
        q = j + d * (S5_STEPS - 1 - 2 * j)
        return pl.ds(pl.multiple_of((k // 2) * S5_BLK + q * SUB, SUB), SUB)

    def proj_in(k):
        bufs_in[k % 2][slot_rows(k), :] = jnp.dot(us_ref[0, time_rows(k), :], bb_ref[0, 0],
                                                  preferred_element_type=F32)

    def scan_block(k, carry):
        src, dst = bufs_in[k % 2], bufs_out[k % 2]
        s_re, s_im = carry
        for j in range(S5_STEPS):
            rows = step_rows(k, j)
            n_re = a_re * s_re - a_im * s_im + src[rows, re_all]
            n_im = a_re * s_im + a_im * s_re + src[rows, im_all]
            dst[rows, re_all] = n_re
            dst[rows, im_all] = n_im
            s_re, s_im = n_re, n_im
        return s_re, s_im

    def proj_out(k):
        rows = time_rows(k)
        yb = jnp.dot(bufs_out[k % 2][slot_rows(k), :].astype(BF16), cc_ref[0, 0], preferred_element_type=F32)
        y_ref[0, rows, :] = y_ref[0, rows, :] + yb

    @pl.when(d == 0)
    def _():
        y_ref[0] = dskip_ref[...] * us_ref[0].astype(F32)

    def scan_all(carry, with_out):
        proj_in(0)
        for k in range(S5_NBLK):
            if with_out and k >= 1:
                proj_out(k - 1)
            if k + 1 < S5_NBLK:
                proj_in(k + 1)
            carry = scan_block(k, carry)
        if with_out:
            proj_out(S5_NBLK - 1)
        fin_ref[0, 0, :, re_all] = carry[0]
        fin_ref[0, 0, :, im_all] = carry[1]
        return carry

    @pl.when(jnp.logical_not(is_lat))
    def _():
        zero = jnp.zeros((SUB, CH_P), F32)
        scan_all((zero, zero), True)

    @pl.when(is_lat)
    def _():
        f_re, f_im = scan_all((init_ref[0, 0, :, re_all], init_ref[0, 0, :, im_all]), False)

        sub_id = lax.broadcasted_iota(jnp.int32, (SUB, CH_P), 0) % LAT_CHUNKS
        fwd = d == 0
        lo = jnp.where(fwd, 1, 0)
        hi = jnp.where(fwd, LAT_CHUNKS - 1, LAT_CHUNKS - 2)
        keep = (sub_id >= lo) & (sub_id <= hi)

        def from_prev(v):
            return jnp.where(keep, jnp.where(fwd, pltpu.roll(v, 1, 0), pltpu.roll(v, SUB - 1, 0)), 0.0)

        p_re, p_im = a_re, a_im
        for _ in range(8):
            p_re, p_im = _cmul(p_re, p_im, p_re, p_im)
        t_re, t_im = f_re, f_im
        for _ in range(LAT_CHUNKS - 2):
            m_re, m_im = _cmul(p_re, p_im, from_prev(t_re), from_prev(t_im))
            t_re, t_im = f_re + m_re, f_im + m_im
        corr = (from_prev(t_re), from_prev(t_im))

        def fix_block(k, carry):
            dst = bufs_out[k % 2]
            m_re, m_im = carry
            for j in range(S5_STEPS):
                rows = step_rows(k, j)
                m_re, m_im = _cmul(m_re, m_im, a_re, a_im)
                dst[rows, re_all] = dst[rows, re_all] + m_re
                dst[rows, im_all] = dst[rows, im_all] + m_im
            return m_re, m_im

        for k in range(S5_NBLK):
            if k >= 1:
                proj_out(k - 1)
            corr = fix_block(k, corr)
        proj_out(S5_NBLK - 1)


def _s5(us, bbmat, ccmat, a8, init8, d_skip):
    return pl.pallas_call(
        _s5_kernel,
        grid=(N_GRP, N_S5_CHUNKS, 2),
        in_specs=[pl.BlockSpec((1, GRP_ROWS, CH_U), lambda g, c, d: (g, 0, c)),
                  pl.BlockSpec((1, 1, CH_U, 2 * CH_P), lambda g, c, d: (d, c, 0, 0)),
                  pl.BlockSpec((1, 1, 2 * CH_P, CH_U), lambda g, c, d: (d, c, 0, 0)),
                  pl.BlockSpec((1, 1, SUB, 2 * CH_P), lambda g, c, d: (d, c, 0, 0)),
                  pl.BlockSpec((1, 1, SUB, 2 * CH_P), lambda g, c, d: (d, c, 0, 0)),
                  pl.BlockSpec((1, CH_U), lambda g, c, d: (0, c))],
        out_specs=[pl.BlockSpec((1, GRP_ROWS, CH_U), lambda g, c, d: (g, 0, c)),
                   pl.BlockSpec((1, 1, SUB, 2 * CH_P), lambda g, c, d: (d, g, 0, c))],
        out_shape=[jax.ShapeDtypeStruct((N_GRP, GRP_ROWS, D_S5), F32),
                   jax.ShapeDtypeStruct((2, N_GRP, SUB, N_S5_CHUNKS * 2 * CH_P), F32)],
        scratch_shapes=[pltpu.VMEM((GRP_ROWS // 2, 2 * CH_P), F32)] * 4,
        compiler_params=_cparams(("arbitrary", "arbitrary", "arbitrary")),
        name="s5_scan",
    )(us, bbmat, ccmat, a8, init8, d_skip.reshape(1, D_S5))


def _fnet_w_kernel(c_ref, s_ref, w_ref, m1_ref, m2_ref):
    w = w_ref[...]
    m1_ref[...] = jnp.dot(c_ref[...], w, preferred_element_type=F32, precision=HIGHEST).astype(BF16)
    m2_ref[...] = jnp.dot(s_ref[...], w, preferred_element_type=F32, precision=HIGHEST).astype(BF16)


FN_SEQ = LAT_CHUNKS
FN_CTX_STEPS = (N_GRP - 1) * SUB // FN_SEQ


def _fnet_kernel(u_ref, cos_s_ref, sin_s_ref, cos_l_ref, sin_l_ref, m1_ref, m2_ref, o_ref):
    i = pl.program_id(0)
    m1 = m1_ref[...]
    m2 = m2_ref[...]

    def mix(u, cos_ref, sin_ref):
        v1 = jnp.dot(u, m1, preferred_element_type=F32).astype(BF16)
        v2 = jnp.dot(u, m2, preferred_element_type=F32).astype(BF16)
        return (jnp.dot(cos_ref[...].astype(BF16), v1, preferred_element_type=F32)
                - jnp.dot(sin_ref[...].astype(BF16), v2, preferred_element_type=F32)).astype(BF16)

    @pl.when(i < FN_CTX_STEPS)
    def _():
        for s in range(FN_SEQ):
            o_ref[0, s] = mix(u_ref[0, s], cos_s_ref, sin_s_ref)

    @pl.when(i >= FN_CTX_STEPS)
    def _():
        u = u_ref[0].reshape(FN_SEQ * L_BLK, D_FN)
        o_ref[0] = mix(u, cos_l_ref, sin_l_ref).reshape(FN_SEQ, L_BLK, D_FN)


def _dft_tables(n):
    k = np.arange(n, dtype=np.int64)
    ang = (2.0 * np.pi / n) * ((k[:, None] * k[None, :]) % n).astype(np.float64)
    scale = 1.0 / math.sqrt(n)
    return (np.cos(ang) * scale).astype(np.float32), (np.sin(ang) * scale).astype(np.float32)


def _fnet(uf, m1, m2):
    n_lat_steps = SUB // FN_SEQ
    cos_s, sin_s = _dft_tables(L_BLK)
    cos_l, sin_l = _dft_tables(FN_SEQ * L_BLK)
    full = lambda i: (0, 0)
    per_grp = SUB // FN_SEQ
    blk = pl.BlockSpec(
        (1, FN_SEQ, L_BLK, D_FN),
        lambda i: (jnp.minimum(i // per_grp, N_GRP - 1),
                   jnp.where(i < FN_CTX_STEPS, i % per_grp, i - FN_CTX_STEPS), 0, 0))
    table = lambda n: pl.BlockSpec((n, n), full)
    return pl.pallas_call(
        _fnet_kernel,
        grid=(FN_CTX_STEPS + n_lat_steps,),
        in_specs=[blk, table(L_BLK), table(L_BLK), table(FN_SEQ * L_BLK), table(FN_SEQ * L_BLK),
                  table(D_FN), table(D_FN)],
        out_specs=blk,
        out_shape=jax.ShapeDtypeStruct(uf.shape, BF16),
        compiler_params=_cparams(("arbitrary",)),
        name="fnet",
    )(uf, jnp.asarray(cos_s), jnp.asarray(sin_s), jnp.asarray(cos_l), jnp.asarray(sin_l), m1, m2)


def _fnet_weights(w_fnet):
    n_g = D_FN // FN_GW
    cos_c, sin_c = _dft_tables(FN_GW)
    eye = np.eye(n_g, dtype=np.float32)
    cos_bd = np.kron(eye, cos_c)
    sin_bd = np.kron(eye, sin_c)
    w_bd = (w_fnet[:, :, None, :] * jnp.asarray(eye)[:, None, :, None]).reshape(D_FN, D_FN)
    out = jax.ShapeDtypeStruct((D_FN, D_FN), BF16)
    return pl.pallas_call(_fnet_w_kernel, out_shape=(out, out), name="fnet_weights")(
        jnp.asarray(cos_bd), jnp.asarray(sin_bd), w_bd)


def _gelu_tanh(x):
    return 0.5 * x * (1.0 + jnp.tanh(math.sqrt(2.0 / math.pi) * (x + 0.044715 * (x * x * x))))


def _post_kernel(y_ref, yf_ref, xp_ref, xs_ref, pos_ref, mod_ref, wglu_ref, wout_ref, g2_ref, wr_ref,
                 br_ref, permt_ref, x1_ref, h2_ref, comb_ref):
    g = pl.program_id(0)
    z = _gelu_tanh(y_ref[0])
    gate = jnp.dot(z.astype(BF16), wglu_ref[...].astype(BF16), preferred_element_type=F32)
    gl = (z * jax.nn.sigmoid(gate)).astype(BF16)
    permt = permt_ref[...]
    n_q = TC // PERM_T
    nat = [jnp.dot(permt, gl[q * SUB * PERM_T:(q + 1) * SUB * PERM_T], preferred_element_type=F32).astype(BF16)
           for q in range(n_q)]
    gl_nat = jnp.concatenate(
        [nat[q][s * PERM_T:(s + 1) * PERM_T] for s in range(SUB) for q in range(n_q)], axis=0)
    w_out = wout_ref[...].astype(BF16)
    mixed = (jnp.dot(gl_nat, w_out[:D_S5], preferred_element_type=F32)
             + jnp.dot(yf_ref[0].reshape(SUB * TC, D_FN), w_out[D_S5:], preferred_element_type=F32))
    x = _load_x(g, xp_ref, xs_ref, pos_ref)
    mod = mod_ref[0]
    gate1 = mod[:, :, 2 * D:3 * D]
    shift2 = mod[:, :, 3 * D:4 * D]
    scale2 = mod[:, :, 4 * D:5 * D]
    x1 = x + gate1 * mixed.reshape(SUB, TC, D)
    x1_ref[0] = x1
    h2 = _rms(x1, g2_ref[...]) * (1.0 + scale2) + shift2
    h2_ref[0] = h2.astype(BF16)

    hr = h2.reshape(SUB * TC, D)
    h_hi = hr.astype(BF16)
    h_lo = (hr - h_hi.astype(F32)).astype(BF16)
    wr = wr_ref[...]
    w_hi = wr.astype(BF16)
    w_lo = (wr - w_hi.astype(F32)).astype(BF16)
    both = jnp.dot(h_hi, jnp.concatenate([w_hi, w_lo], axis=1), preferred_element_type=F32)
    logits = (both[:, :E_PAD] + jnp.dot(h_lo, w_hi, preferred_element_type=F32) + both[:, E_PAD:]
              + br_ref[...])
    lane = lax.broadcasted_iota(jnp.int32, logits.shape, 1).astype(F32)
    top_v, hots = [], []
    cur = logits
    for _ in range(TOP_K):
        m = jnp.max(cur, axis=-1, keepdims=True)
        idx = jnp.min(jnp.where(cur == m, lane, float(E_PAD)), axis=-1, keepdims=True)
        hot = lane == idx
        top_v.append(m)
        hots.append(hot)
        cur = jnp.where(hot, -3.0e38, cur)
    exps = [jnp.exp(v - top_v[0]) for v in top_v]
    denom = exps[0] + exps[1] + exps[2] + exps[3]
    comb = jnp.zeros(logits.shape, F32)
    for k in range(TOP_K):
        comb = comb + jnp.where(hots[k], exps[k] / denom, 0.0)
    comb_ref[0] = comb.reshape(SUB, TC, E_PAD)


def _post(y, yf, xp4, xs4, pos3, modg, w_glu, w_out, norm2_g, w_router, b_router, permt):
    xp_spec, xs_spec, pos_spec = _x_specs()
    const2 = lambda g, tc: (0, 0)
    wr = jnp.zeros((D, E_PAD), F32).at[:, :N_EXPERTS].set(w_router)
    br = jnp.full((1, E_PAD), -1.0e30, F32).at[0, :N_EXPERTS].set(b_router)
    blk = lambda w: pl.BlockSpec((1, SUB, TC, w), lambda g, tc: (g, 0, tc, 0))
    return pl.pallas_call(
        _post_kernel,
        grid=(N_GRP, N_TC),
        in_specs=[pl.BlockSpec((1, SUB * TC, D_S5), lambda g, tc: (g, tc, 0)),
                  blk(D_FN), xp_spec, xs_spec, pos_spec,
                  pl.BlockSpec((1, SUB, 1, 6 * D), lambda g, tc: (g, 0, 0, 0)),
                  pl.BlockSpec((D_S5, D_S5), const2),
                  pl.BlockSpec((D, D), const2),
                  pl.BlockSpec((1, D), const2),
                  pl.BlockSpec((D, E_PAD), const2),
                  pl.BlockSpec((1, E_PAD), const2),
                  pl.BlockSpec((SUB * PERM_T, SUB * PERM_T), const2)],
        out_specs=[blk(D), blk(D), blk(E_PAD)],
        out_shape=[jax.ShapeDtypeStruct((N_GRP, SUB, L_BLK, D), F32),
                   jax.ShapeDtypeStruct((N_GRP, SUB, L_BLK, D), BF16),
                   jax.ShapeDtypeStruct((N_GRP, SUB, L_BLK, E_PAD), F32)],
        compiler_params=_cparams(("arbitrary", "arbitrary")),
        name="post_mixer",
    )(y, yf, xp4, xs4, pos3, modg, w_glu, w_out, norm2_g.reshape(1, D), wr, br, permt)


TBLK = L_BLK
N_BLK = T_TOK // TBLK
N_CTX_BLK = (N_GRP - 1) * SUB
SEG_ALIGN = 8
RB = 1280
assert RB >= TBLK * TOP_K + N_EXPERTS * (SEG_ALIGN - 1) and RB % 128 == 0
TM = 256
TM_SHIFT = 8
assert 1 << TM_SHIFT == TM
R_TOT = T_TOK * TOP_K + N_BLK * N_EXPERTS * (SEG_ALIGN - 1) + TM
TAB_ROWS = 32
assert TAB_ROWS >= N_BLK
BIG_ROWS = 32
BIG_SHIFT = 2
assert SEG_ALIGN << BIG_SHIFT == BIG_ROWS
SMALL_COPY_PRIORITY = 1


def _plan_kernel(comb_ref, z_ref, ptab_ref, loff_ref, loff_i_ref, gst_i_ref, p8_i_ref, eoff_i_ref):
    row = lax.broadcasted_iota(jnp.int32, (TBLK, TBLK), 0)
    col = lax.broadcasted_iota(jnp.int32, (TBLK, TBLK), 1)
    earlier = jnp.where(row > col, 1.0, 0.0).astype(BF16)
    ptab_ref[...] = jnp.zeros_like(ptab_ref)

    def body(b, _):
        rows = pl.ds(pl.multiple_of(b * TBLK, TBLK), TBLK)
        m = jnp.where(comb_ref[rows, :] > 0.0, 1.0, 0.0)
        rank = jnp.dot(earlier, m.astype(BF16), preferred_element_type=F32)
        z_ref[rows, :] = m * (rank + 1.0)
        n = jnp.sum(m, axis=0, keepdims=True)
        ptab_ref[pl.ds(b, 1), :] = jnp.floor((n + (SEG_ALIGN - 1)) * (1.0 / SEG_ALIGN)) * SEG_ALIGN
        return 0

    lax.fori_loop(0, N_BLK, body, 0)

    ptab = ptab_ref[...]
    er = lax.broadcasted_iota(jnp.int32, (E_PAD, E_PAD), 0)
    ec = lax.broadcasted_iota(jnp.int32, (E_PAD, E_PAD), 1)
    before = jnp.where(er < ec, 1.0, 0.0)
    exact = functools.partial(jnp.dot, preferred_element_type=F32, precision=HIGHEST)
    loff = exact(ptab, before)
    loff_ref[...] = loff
    as_int = lambda t: t[:, :N_EXPERTS].astype(jnp.int32)
    loff_i_ref[...] = as_int(loff)
    p8_i_ref[...] = as_int(ptab * (1.0 / SEG_ALIGN))
    tot = jnp.sum(ptab, axis=0, keepdims=True)
    eoff = exact(jnp.broadcast_to(tot, (SUB, E_PAD)), before)
    eoff_i_ref[...] = eoff.astype(jnp.int32)
    br = lax.broadcasted_iota(jnp.int32, (TAB_ROWS, TAB_ROWS), 0)
    bc = lax.broadcasted_iota(jnp.int32, (TAB_ROWS, TAB_ROWS), 1)
    gst_i_ref[...] = as_int(eoff[0:1] + exact(jnp.where(br > bc, 1.0, 0.0), ptab))


def _plan(comb):
    tab = jax.ShapeDtypeStruct((TAB_ROWS, E_PAD), F32)
    itab = jax.ShapeDtypeStruct((TAB_ROWS, N_EXPERTS), jnp.int32)
    return pl.pallas_call(
        _plan_kernel,
        out_shape=(jax.ShapeDtypeStruct((T_TOK, E_PAD), F32), tab, tab, itab, itab, itab,
                   jax.ShapeDtypeStruct((SUB, E_PAD), jnp.int32)),
        compiler_params=pltpu.CompilerParams(vmem_limit_bytes=VMEM_LIMIT),
        name="moe_plan",
    )(comb)


def _sort_matrix(b, z_ref, loff_ref, ptab_ref):
    loff = loff_ref[pl.ds(b, 1), :]
    size = ptab_ref[pl.ds(b, 1), :]
    r = lax.broadcasted_iota(jnp.int32, (RB, E_PAD), 0).astype(F32)
    owner = jnp.where(r >= loff, jnp.where(r < loff + size, 1.0, 0.0), 0.0)
    rank1 = r[:, 0:1] - jnp.sum(owner * loff, axis=-1, keepdims=True) + 1.0
    zt = z_ref[...].T
    v = jnp.dot(owner.astype(BF16), zt.astype(BF16), preferred_element_type=F32)
    return jnp.where(v == rank1, 1.0, 0.0), owner


def _segment_copies(b, loff_s, gst_s, p8_s, make_copy):
    def per_expert(e, counts):
        chunks = p8_s[b, e]
        n_big = lax.shift_right_logical(chunks, BIG_SHIFT)
        n_small = chunks - n_big * (BIG_ROWS // SEG_ALIGN)
        local0 = loff_s[b, e]
        global0 = gst_s[b, e]

        def big(j, _):
            make_copy(pl.multiple_of(local0 + j * BIG_ROWS, SEG_ALIGN),
                      pl.multiple_of(global0 + j * BIG_ROWS, SEG_ALIGN), BIG_ROWS).start()
            return 0

        def small(j, _):
            off = n_big * BIG_ROWS + j * SEG_ALIGN
            make_copy(pl.multiple_of(local0 + off, SEG_ALIGN),
                      pl.multiple_of(global0 + off, SEG_ALIGN), SEG_ALIGN).start(priority=SMALL_COPY_PRIORITY)
            return 0

        lax.fori_loop(0, n_big, big, 0)
        lax.fori_loop(0, n_small, small, 0)
        return counts[0] + n_big, counts[1] + n_small

    return lax.fori_loop(0, N_EXPERTS, per_expert, (0, 0))


def _wait_copies(counts, make_copy):
    for n, rows in zip(counts, (BIG_ROWS, SEG_ALIGN)):
        def wait_one(i, _, rows=rows):
            make_copy(0, 0, rows).wait()
            return 0
        lax.fori_loop(0, n, wait_one, 0)


def _zero_rows_from(hbm, zeros_vmem, first_row, sem):
    def copy_to(row):
        return pltpu.make_async_copy(zeros_vmem, hbm.at[pl.ds(row, TM)], sem)

    n_full = lax.shift_right_logical(R_TOT - first_row, TM_SHIFT)

    def start_one(j, _):
        copy_to(pl.multiple_of(first_row + j * TM, SEG_ALIGN)).start()
        return 0

    def wait_one(j, _):
        copy_to(0).wait()
        return 0

    lax.fori_loop(0, n_full, start_one, 0)
    lax.fori_loop(0, n_full, wait_one, 0)
    last = copy_to(R_TOT - TM)
    last.start()
    last.wait()


def _dispatch_kernel(loff_s, gst_s, p8_s, eoff_s, h_ref, z_ref, loff_ref, ptab_ref, xs_hbm, xbuf, sem, pending):
    b = pl.program_id(0)
    slot = lax.rem(b, 2)

    def copies_of(s):
        def make_copy(local_row, global_row, rows):
            return pltpu.make_async_copy(xbuf.at[s, pl.ds(local_row, rows)], xs_hbm.at[pl.ds(global_row, rows)],
                                         sem.at[s])
        return make_copy

    pm = _sort_matrix(b, z_ref, loff_ref, ptab_ref)[0].astype(BF16)

    @pl.when(b >= 2)
    def _():
        _wait_copies((pending[2 * slot], pending[2 * slot + 1]), copies_of(slot))

    xbuf[slot] = jnp.dot(pm, h_ref[...], preferred_element_type=F32)
    n_big, n_small = _segment_copies(b, loff_s, gst_s, p8_s, copies_of(slot))
    pending[2 * slot] = n_big
    pending[2 * slot + 1] = n_small

    @pl.when(b == N_BLK - 1)
    def _():
        _wait_copies((n_big, n_small), copies_of(slot))
        _wait_copies((pending[2 * (1 - slot)], pending[2 * (1 - slot) + 1]), copies_of(1 - slot))
        xbuf[0, 0:TM, :] = jnp.zeros((TM, D), F32)
        _zero_rows_from(xs_hbm, xbuf.at[0, 0:TM], eoff_s[0, N_EXPERTS], sem.at[0])


def _dispatch(h2, z, loff, ptab, loff_i, gst_i, p8_i, eoff_i):
    whole = pl.BlockSpec((TAB_ROWS, E_PAD), lambda b, *_: (0, 0))
    return pl.pallas_call(
        _dispatch_kernel,
        grid_spec=pltpu.PrefetchScalarGridSpec(
            num_scalar_prefetch=4,
            grid=(N_BLK,),
            in_specs=[pl.BlockSpec((TBLK, D), lambda b, *_: (b, 0)),
                      pl.BlockSpec((TBLK, E_PAD), lambda b, *_: (b, 0)),
                      whole, whole],
            out_specs=pl.BlockSpec(memory_space=pl.ANY),
            scratch_shapes=[pltpu.VMEM((2, RB, D), F32), pltpu.SemaphoreType.DMA((2,)),
                            pltpu.SMEM((4,), jnp.int32)]),
        out_shape=jax.ShapeDtypeStruct((R_TOT, D), F32),
        compiler_params=_cparams(("arbitrary",)),
        name="moe_dispatch",
    )(loff_i, gst_i, p8_i, eoff_i, h2, z, loff, ptab)


MAX_TILES = R_TOT // TM + N_EXPERTS
X_AHEAD = 3
X_SLOTS = X_AHEAD + 1
Y_SLOTS = 2
W_AHEAD = 2
W_SLOTS = W_AHEAD + 1
N_MATS = 3
FF_CHUNK = 512
CAST_ROWS = 64
TILE_STEP = 64


def _expert_kernel(eoff_s, xs_hbm, wg_hbm, wu_hbm, wd_hbm, bg_ref, bu_ref, bd_ref, ys_hbm,
                   wst, wbf, xin, yout, act_ref, w_sem, x_sem, y_sem, t_exp, t_row, t_first, t_valid, live):
    def add_expert(e, carry):
        n_t, n_live = carry
        start = eoff_s[0, e]
        count = eoff_s[0, e + 1] - start
        tiles = lax.shift_right_logical(count + (TM - 1), TM_SHIFT)

        def add_tile(i, _):
            t_exp[n_t + i] = e
            t_row[n_t + i] = start + i * TM
            t_first[n_t + i] = jnp.where(i == 0, 1, 0)
            t_valid[n_t + i] = jnp.minimum(count - i * TM, TM)
            return 0

        lax.fori_loop(0, tiles, add_tile, 0)
        has_rows = jnp.where(tiles > 0, 1, 0)

        @pl.when(tiles > 0)
        def _():
            live[n_live] = e

        return n_t + tiles, n_live + has_rows

    n_tiles, n_live = lax.fori_loop(0, N_EXPERTS, add_expert, (0, 0))

    def w_copies(e, slot):
        return [pltpu.make_async_copy(w.at[e], wst.at[slot, m], w_sem.at[slot])
                for m, w in enumerate((wg_hbm, wu_hbm, wd_hbm))]

    def tile_rows(j):
        return pl.ds(pl.multiple_of(t_row[j], SEG_ALIGN), TM)

    def x_copy(j, slot):
        return pltpu.make_async_copy(xs_hbm.at[tile_rows(j)], xin.at[slot], x_sem.at[slot])

    def y_pieces(j, op):
        ys = lax.rem(j, Y_SLOTS)
        valid = t_valid[j]
        row0 = t_row[j]

        def piece(off, rows):
            return pltpu.make_async_copy(yout.at[ys, pl.ds(off, rows)],
                                         ys_hbm.at[pl.ds(pl.multiple_of(row0 + off, SEG_ALIGN), rows)],
                                         y_sem.at[ys])

        @pl.when(valid == TM)
        def _():
            op(piece(0, TM))

        @pl.when(valid < TM)
        def _():
            n_big = lax.shift_right_logical(valid, BIG_ROWS.bit_length() - 1)
            n_small = lax.shift_right_logical(valid - n_big * BIG_ROWS, SEG_ALIGN.bit_length() - 1)

            def big(i, _):
                op(piece(pl.multiple_of(i * BIG_ROWS, SEG_ALIGN), BIG_ROWS))
                return 0

            def small(i, _):
                op(piece(pl.multiple_of(n_big * BIG_ROWS + i * SEG_ALIGN, SEG_ALIGN), SEG_ALIGN))
                return 0

            lax.fori_loop(0, n_big, big, 0)
            lax.fori_loop(0, n_small, small, 0)

    for ahead in range(W_AHEAD):
        @pl.when(n_live > ahead)
        def _(ahead=ahead):
            for cp in w_copies(live[ahead], ahead):
                cp.start()

    for ahead in range(X_AHEAD):
        @pl.when(n_tiles > ahead)
        def _(ahead=ahead):
            x_copy(ahead, ahead).start()

    def body(j, k):
        e = t_exp[j]

        @pl.when(t_first[j] == 1)
        def _():
            ws = lax.rem(k, W_SLOTS)

            @pl.when(k + W_AHEAD < n_live)
            def _():
                for cp in w_copies(live[k + W_AHEAD], lax.rem(k + W_AHEAD, W_SLOTS)):
                    cp.start()

            for cp in w_copies(e, ws):
                cp.wait()

            def cast_rows(r, _):
                rows = pl.ds(pl.multiple_of(r * CAST_ROWS, CAST_ROWS), CAST_ROWS)
                for m in range(N_MATS):
                    wbf[m, rows, :] = wst[ws, m, rows, :].astype(BF16)
                return 0

            lax.fori_loop(0, D // CAST_ROWS, cast_rows, 0)

        slot = lax.rem(j, X_SLOTS)
        x_copy(j, slot).wait()

        @pl.when(j + X_AHEAD < n_tiles)
        def _():
            x_copy(j + X_AHEAD, lax.rem(j + X_AHEAD, X_SLOTS)).start()

        @pl.when(j >= Y_SLOTS)
        def _():
            y_pieces(j - Y_SLOTS, lambda cp: cp.wait())

        ys = lax.rem(j, Y_SLOTS)

        def compute(rows):
            x = xin[slot, 0:rows, :].astype(BF16)
            for c in range(D // FF_CHUNK):
                cols = slice(c * FF_CHUNK, (c + 1) * FF_CHUNK)
                gate = jnp.dot(x, wbf[0, :, cols], preferred_element_type=F32) + bg_ref[e][:, cols]
                up = jnp.dot(x, wbf[1, :, cols], preferred_element_type=F32) + bu_ref[e][:, cols]
                gate = jnp.minimum(gate, SWIGLU_LIMIT)
                up = jnp.clip(up, -SWIGLU_LIMIT, SWIGLU_LIMIT)
                act_ref[0:rows, cols] = ((up + 1.0) * gate * jax.nn.sigmoid(SWIGLU_ALPHA * gate)).astype(BF16)
            yout[ys, 0:rows, :] = jnp.dot(act_ref[0:rows, :], wbf[2], preferred_element_type=F32) + bd_ref[e]

        valid = t_valid[j]
        for rows in range(TILE_STEP, TM + 1, TILE_STEP):
            @pl.when(jnp.logical_and(valid > rows - TILE_STEP, valid <= rows))
            def _(rows=rows):
                compute(rows)

        y_pieces(j, lambda cp: cp.start())
        return k + t_first[j]

    lax.fori_loop(0, n_tiles, body, 0)

    for back in range(Y_SLOTS, 0, -1):
        @pl.when(n_tiles >= back)
        def _(back=back):
            y_pieces(n_tiles - back, lambda cp: cp.wait())

    yout[0] = jnp.zeros((TM, D), F32)
    _zero_rows_from(ys_hbm, yout.at[0], eoff_s[0, N_EXPERTS], y_sem.at[0])


def _experts(xs, eoff_i, w_gate, b_gate, w_up, b_up, w_down, b_down):
    hbm = pl.BlockSpec(memory_space=pl.ANY)
    bspec = pl.BlockSpec((N_EXPERTS, 1, D), lambda i, *_: (0, 0, 0))
    return pl.pallas_call(
        _expert_kernel,
        grid_spec=pltpu.PrefetchScalarGridSpec(
            num_scalar_prefetch=1,
            grid=(1,),
            in_specs=[hbm, hbm, hbm, hbm, bspec, bspec, bspec],
            out_specs=hbm,
            scratch_shapes=[pltpu.VMEM((W_SLOTS, N_MATS, D, D), F32), pltpu.VMEM((N_MATS, D, D), BF16),
                            pltpu.VMEM((X_SLOTS, TM, D), F32), pltpu.VMEM((Y_SLOTS, TM, D), F32),
                            pltpu.VMEM((TM, D), BF16),
                            pltpu.SemaphoreType.DMA((W_SLOTS,)), pltpu.SemaphoreType.DMA((X_SLOTS,)),
                            pltpu.SemaphoreType.DMA((Y_SLOTS,)),
                            pltpu.SMEM((MAX_TILES,), jnp.int32), pltpu.SMEM((MAX_TILES,), jnp.int32),
                            pltpu.SMEM((MAX_TILES,), jnp.int32), pltpu.SMEM((MAX_TILES,), jnp.int32),
                            pltpu.SMEM((N_EXPERTS,), jnp.int32)]),
        out_shape=jax.ShapeDtypeStruct((R_TOT, D), F32),
        compiler_params=_cparams(("arbitrary",)),
        name="moe_experts",
    )(eoff_i, xs, w_gate, w_up, w_down, b_gate.reshape(N_EXPERTS, 1, D), b_up.reshape(N_EXPERTS, 1, D),
      b_down.reshape(N_EXPERTS, 1, D))


def _combine_kernel(loff_s, gst_s, p8_s, ys_hbm, z_ref, comb_ref, loff_ref, ptab_ref, x1_ref, mod_ref, gf_ref,
                    yctx_ref, ylat_ref, ybuf, sem):
    b = pl.program_id(0)

    def make_copy(local_row, global_row, rows):
        return pltpu.make_async_copy(ys_hbm.at[pl.ds(global_row, rows)], ybuf.at[pl.ds(local_row, rows)], sem)

    started = _segment_copies(b, loff_s, gst_s, p8_s, make_copy)

    used_chunks = lax.shift_right_logical(loff_s[b, N_EXPERTS - 1], 3) + p8_s[b, N_EXPERTS - 1]

    def zero_chunk(j, _):
        ybuf[pl.ds(pl.multiple_of(j * SEG_ALIGN, SEG_ALIGN), SEG_ALIGN), :] = jnp.zeros((SEG_ALIGN, D), F32)
        return 0

    lax.fori_loop(used_chunks, RB // SEG_ALIGN, zero_chunk, 0)

    pm, owner = _sort_matrix(b, z_ref, loff_ref, ptab_ref)
    comb = comb_ref[...]
    c_hi = comb.astype(BF16)
    rest = comb - c_hi.astype(F32)
    c_mid = rest.astype(BF16)
    c_lo = (rest - c_mid.astype(F32)).astype(BF16)
    pmb = pm.astype(BF16)
    moved = (jnp.dot(pmb, c_hi, preferred_element_type=F32) + jnp.dot(pmb, c_mid, preferred_element_type=F32)
             + jnp.dot(pmb, c_lo, preferred_element_type=F32))
    w_row = jnp.sum(owner * moved, axis=-1, keepdims=True)
    pmt = pm.T.astype(BF16)
    _wait_copies(started, make_copy)
    y = (ybuf[...] * w_row).astype(BF16)
    moe = jnp.dot(pmt, y, preferred_element_type=F32)
    gate2 = mod_ref[0][:, 5 * D:6 * D]
    x2 = x1_ref[0] + gate2 * moe
    y = _rms(x2, gf_ref[...])

    @pl.when(b < N_CTX_BLK)
    def _():
        yctx_ref[0] = y

    @pl.when(b >= N_CTX_BLK)
    def _():
        ylat_ref[0] = y


def _combine(ys, z, comb, loff, ptab, x1, modv, norm_f_g, loff_i, gst_i, p8_i):
    whole = pl.BlockSpec((TAB_ROWS, E_PAD), lambda b, *_: (0, 0))
    tok = pl.BlockSpec((TBLK, E_PAD), lambda b, *_: (b, 0))
    return pl.pallas_call(
        _combine_kernel,
        grid_spec=pltpu.PrefetchScalarGridSpec(
            num_scalar_prefetch=3,
            grid=(N_BLK,),
            in_specs=[pl.BlockSpec(memory_space=pl.ANY), tok, tok, whole, whole,
                      pl.BlockSpec((1, TBLK, D), lambda b, *_: (b, 0, 0)),
                      pl.BlockSpec((1, 1, 6 * D), lambda b, *_: (b, 0, 0)),
                      pl.BlockSpec((1, D), lambda b, *_: (0, 0))],
            out_specs=[pl.BlockSpec((1, TBLK, D), lambda b, *_: (jnp.minimum(b, N_CTX_BLK - 1), 0, 0)),
                       pl.BlockSpec((1, TBLK, D), lambda b, *_: (jnp.maximum(b - N_CTX_BLK, 0), 0, 0))],
            scratch_shapes=[pltpu.VMEM((RB, D), F32), pltpu.SemaphoreType.DMA(())]),
        out_shape=[jax.ShapeDtypeStruct((N_CTX_BLK, TBLK, D), F32),
                   jax.ShapeDtypeStruct((N_BLK - N_CTX_BLK, TBLK, D), F32)],
        compiler_params=_cparams(("arbitrary",)),
        name="moe_combine",
    )(loff_i, gst_i, p8_i, ys, z, comb, loff, ptab, x1, modv, norm_f_g.reshape(1, D))


def _moe_and_final(x1, h2, comb, modg, norm_f_g, w_gate, b_gate, w_up, b_up, w_down, b_down):
    z, ptab, loff, loff_i, gst_i, p8_i, eoff_i = _plan(comb)
    xs = _dispatch(h2, z, loff, ptab, loff_i, gst_i, p8_i, eoff_i)
    ys = _experts(xs, eoff_i, w_gate, b_gate, w_up, b_up, w_down, b_down)
    return _combine(ys, z, comb, loff, ptab, x1, modg.reshape(N_BLK, 1, 6 * D), norm_f_g,
                    loff_i, gst_i, p8_i)


def _grid_pos_embed(n_tokens, dim):
    rows = n_tokens // GRID_W
    t = np.arange(rows * GRID_W)
    r = (t // GRID_W).astype(np.float32)
    col = (t % GRID_W).astype(np.float32)
    q = dim // 4
    omega = (1.0 / np.float32(POS_TEMP) ** (np.arange(q, dtype=np.float32) / np.float32(q))).astype(np.float32)

    def emb(p):
        a = p[:, None] * omega[None, :]
        return np.concatenate([np.sin(a), np.cos(a)], axis=-1)

    return np.concatenate([emb(r), emb(col)], axis=-1).astype(np.float32)


def kernel(x_prompt, x_sample, c, state_s5_re, state_s5_im, c_ctx, w_ada, b_ada, norm1_g, w_in, s5_lam_re,
           s5_lam_im, s5_log_dt, s5_b_re, s5_b_im, s5_c_re, s5_c_im, s5_d, s5_w_glu, w_fnet, w_out, norm2_g,
           w_router, b_router, w_gate, b_gate, w_up, b_up, w_down, b_down, norm_f_g):
    n_ctx, n_lat = x_prompt.shape[0], x_sample.shape[0]
    assert x_prompt.shape == (SUB * (N_GRP - 1), L_BLK, D) and x_sample.shape == (2, LAT_CHUNKS * L_BLK, D)
    assert w_ada.shape[0] == 1, "one trunk layer"
    layer = 0

    cvec = jnp.concatenate([jnp.broadcast_to(c_ctx[None], (n_ctx, D)), jnp.repeat(c, LAT_CHUNKS, axis=0)], axis=0)
    modg = _adaln(cvec, w_ada[layer], b_ada[layer]).reshape(N_GRP, SUB, 1, 6 * D)

    xp4 = x_prompt.reshape(N_GRP - 1, SUB, L_BLK, D)
    xs4 = x_sample.reshape(1, SUB, L_BLK, D)
    pos3 = jnp.asarray(_grid_pos_embed(LAT_CHUNKS * L_BLK, D).reshape(LAT_CHUNKS, L_BLK, D))
    perm = _perm_matrix()

    us, uf = _pre(xp4, xs4, pos3, modg, norm1_g[layer], w_in[layer], jnp.asarray(perm, BF16))

    bbmat, ccmat, a8 = _s5_params(s5_lam_re[layer], s5_lam_im[layer], s5_log_dt[layer], s5_b_re[layer],
                                  s5_b_im[layer], s5_c_re[layer], s5_c_im[layer])

    st = jnp.stack([state_s5_re[:, layer], state_s5_im[:, layer]], axis=2)
    st = st.reshape(n_lat, 2, 2, N_S5_CHUNKS, CH_P)
    st = jnp.transpose(st, (1, 3, 0, 2, 4)).reshape(2, N_S5_CHUNKS, n_lat, 2 * CH_P)
    init8 = jnp.zeros((2, N_S5_CHUNKS, n_lat, LAT_CHUNKS, 2 * CH_P), F32)
    init8 = init8.at[0, :, :, 0].set(st[0]).at[1, :, :, LAT_CHUNKS - 1].set(st[1])
    init8 = init8.reshape(2, N_S5_CHUNKS, SUB, 2 * CH_P)

    y_s5, fin = _s5(us, bbmat, ccmat, a8, init8, s5_d[layer])

    m1, m2 = _fnet_weights(w_fnet[layer])
    yf = _fnet(uf, m1, m2)

    x1, h2, comb = _post(y_s5, yf, xp4, xs4, pos3, modg, s5_w_glu[layer], w_out[layer], norm2_g[layer],
                         w_router[layer], b_router[layer], jnp.asarray(perm.T, BF16))

    y_prompt, y_lat = _moe_and_final(x1.reshape(N_BLK, TBLK, D), h2.reshape(T_TOK, D), comb.reshape(T_TOK, E_PAD),
                                     modg, norm_f_g, w_gate[layer], b_gate[layer], w_up[layer], b_up[layer],
                                     w_down[layer], b_down[layer])
    y_sample = y_lat.reshape(n_lat, LAT_CHUNKS * L_BLK, D)

    fin = fin[:, :N_GRP - 1].reshape(2, N_GRP - 1, SUB, N_S5_CHUNKS, 2, CH_P)
    fin = jnp.transpose(fin, (4, 1, 2, 0, 3, 5)).reshape(2, n_ctx, 1, 2, N_S5_GROUPS, S5_P)
    return (y_prompt, y_sample, fin[0], fin[1])
```
